```python
import math
import jax, jax.numpy as jnp
from jax import lax
import numpy as np

D_MODEL = 2048
BATCH = 4
SEQ = 2048
DEPTH = 2
DEC_BATCH = 128
DEC_SEQ = 8
PAST_LEN = 8192
PAGE_SIZE = 128

N_META = 16
N_MIXERS = 2
N_HGRN_LAYERS = (DEPTH + 1) // 2
N_SWA_LAYERS = DEPTH // 2
HGRN_EXPAND = 128
HGRN_WIDTH = D_MODEL
HGRN_HEADS = HGRN_WIDTH // HGRN_EXPAND
HGRN_DK = HGRN_EXPAND
HGRN_DV = HGRN_WIDTH // HGRN_HEADS
HGRN_CHUNK = 64
SWA_HEAD_DIM = 64
SWA_Q_HEADS = D_MODEL // SWA_HEAD_DIM
SWA_KV_HEADS = SWA_Q_HEADS // 8
SWA_GROUP = SWA_Q_HEADS // SWA_KV_HEADS
SWA_WIDTH = SWA_Q_HEADS * SWA_HEAD_DIM
SWA_KV_WIDTH = SWA_KV_HEADS * SWA_HEAD_DIM
SWA_SCALE = SWA_HEAD_DIM ** -0.5
WINDOW = 128
SWA_BLOCK = 128
DEEPNORM_ALPHA = (2.0 * DEPTH) ** 0.25
DEEPNORM_BETA = (8.0 * DEPTH) ** -0.25
LN_EPS = 1e-5
RMS_EPS = 1e-6

kernel_name = 'hybrid_hgrn2_swa_sink_deepnorm_step'


def layer_norm(x, g, b):
    mu = jnp.mean(x, axis=-1, keepdims=True)
    var = jnp.mean(jnp.square(x - mu), axis=-1, keepdims=True)
    return (x - mu) * lax.rsqrt(var + LN_EPS) * g.astype(jnp.float32) + b.astype(jnp.float32)


def hgrn_chunk(q, logf, k, v, s0):
    c = q.shape[1]
    b = jnp.cumsum(logf, axis=1)
    causal = jnp.tril(jnp.ones((c, c), dtype=bool))[None, :, :, None, None]
    diff = jnp.where(causal, b[:, :, None] - b[:, None, :], -jnp.inf)
    a = jnp.sum(q[:, :, None] * jnp.exp(diff) * k[:, None, :], axis=-1)
    o = (jnp.einsum('bthk,bhkv->bthv', q * jnp.exp(b), s0)
         + jnp.einsum('btsh,bshv->bthv', a, v))
    b_last = b[:, -1]
    s_new = (jnp.exp(b_last)[..., None] * s0
             + jnp.einsum('bshk,bshv->bhkv', k * jnp.exp(b_last[:, None] - b), v))
    return o, s_new


def hgrn_branch(xin, w_in, lb, norm_w, w_out, s0, n_lead, chunk):
    bsz, t, _ = xin.shape
    h = jnp.einsum('btd,de->bte', xin, w_in.astype(jnp.float32))
    q, fx, i_in, g = jnp.split(h, 4, axis=-1)
    heads = lambda a: a.reshape(bsz, t, HGRN_HEADS, -1)
    f = lb + (1.0 - lb) * jax.nn.sigmoid(heads(fx))
    q, logf, k, v = heads(q), jnp.log(f), 1.0 - f, heads(i_in)
    outs = []
    s = s0
    if n_lead > 0:
        o_lead, s = hgrn_chunk(q[:, :n_lead], logf[:, :n_lead], k[:, :n_lead], v[:, :n_lead], s)
        outs.append(o_lead)
    rest = t - n_lead
    nc = rest // chunk
    blocks = tuple(a[:, n_lead:].reshape(bsz, nc, chunk, HGRN_HEADS, -1).swapaxes(0, 1)
                   for a in (q, logf, k, v))

    def step(state, blk):
        o_blk, state = hgrn_chunk(*blk, state)
        return state, o_blk

    s, o_blocks = lax.scan(step, s, blocks)
    outs.append(o_blocks.swapaxes(0, 1).reshape(bsz, rest, HGRN_HEADS, HGRN_DV))
    o = jnp.concatenate(outs, axis=1)
    o = o * lax.rsqrt(jnp.mean(jnp.square(o), axis=-1, keepdims=True) + RMS_EPS) * norm_w.astype(jnp.float32)
    o = o.reshape(bsz, t, HGRN_WIDTH) * jax.nn.silu(g)
    y = jnp.einsum('bte,ed->btd', o, w_out.astype(jnp.float32))
    return y, s


def swa_project(xin, w_in):
    bsz, t, _ = xin.shape
    h = jnp.einsum('btd,de->bte', xin, w_in.astype(jnp.float32))
    q, k, v, g = jnp.split(h, [SWA_WIDTH, SWA_WIDTH + SWA_KV_WIDTH, SWA_WIDTH + 2 * SWA_KV_WIDTH], axis=-1)
    q = q.reshape(bsz, t, SWA_KV_HEADS, SWA_GROUP, SWA_HEAD_DIM)
    k = k.reshape(bsz, t, SWA_KV_HEADS, SWA_HEAD_DIM)
    v = v.reshape(bsz, t, SWA_KV_HEADS, SWA_HEAD_DIM)
    return q, k, v, g


def sink_attention(q, k, v, mask, sinks):
    s = jnp.einsum('bnqhgd,bnshd->bnhgqs', q, k) * SWA_SCALE
    s = jnp.where(mask[None, :, None, None], s, -jnp.inf)
    sink = sinks.astype(jnp.float32)[None, None, :, :, None, None]
    m = jnp.maximum(jnp.max(s, axis=-1, keepdims=True), sink)
    p = jnp.exp(s - m)
    p = p / (jnp.sum(p, axis=-1, keepdims=True) + jnp.exp(sink - m))
    return jnp.einsum('bnhgqs,bnshd->bnqhgd', p, v)


def swa_output(o, g, w_out):
    return jnp.einsum('bte,ed->btd', o * jax.nn.silu(g), w_out.astype(jnp.float32))


def swa_prompt(xin, w_in, sinks, w_out):
    bsz, t, _ = xin.shape
    q, k, v, g = swa_project(xin, w_in)
    nb = -(-t // SWA_BLOCK)
    pad = nb * SWA_BLOCK - t

    def blocks(a):
        a = jnp.pad(a, ((0, 0), (0, pad)) + ((0, 0),) * (a.ndim - 2))
        return a.reshape((bsz, nb, SWA_BLOCK) + a.shape[2:])

    def band(a):
        ab = blocks(a)
        prev = jnp.concatenate([jnp.zeros_like(ab[:, :1]), ab[:, :-1]], axis=1)
        return jnp.concatenate([prev, ab], axis=2)

    qb = blocks(q)
    kb, vb = band(k), band(v)
    qi = jnp.arange(SWA_BLOCK)[:, None]
    kj = jnp.arange(2 * SWA_BLOCK)[None, :]
    dist = SWA_BLOCK + qi - kj
    in_window = (dist >= 0) & (dist < WINDOW)
    kpos = (jnp.arange(nb)[:, None] - 1) * SWA_BLOCK + kj
    mask = in_window[None] & (kpos >= 0)[:, None, :]
    o = sink_attention(qb, kb, vb, mask, sinks.reshape(SWA_KV_HEADS, SWA_GROUP))
    o = o.reshape(bsz, nb * SWA_BLOCK, SWA_WIDTH)[:, :t]
    y = swa_output(o, g, w_out)
    return y, k[:, t - WINDOW:], v[:, t - WINDOW:]


def swa_sample(xin, cache_k, cache_v, w_in, sinks, w_out):
    bsz, t, _ = xin.shape
    q, k, v, g = swa_project(xin, w_in)
    keys = jnp.concatenate([cache_k.astype(jnp.float32), k], axis=1)
    vals = jnp.concatenate([cache_v.astype(jnp.float32), v], axis=1)
    qi = jnp.arange(t)[:, None]
    kj = jnp.arange(WINDOW + t)[None, :]
    dist = WINDOW + qi - kj
    mask = ((dist >= 0) & (dist < WINDOW))[None]
    o = sink_attention(q[:, None], keys[:, None], vals[:, None], mask,
                       sinks.reshape(SWA_KV_HEADS, SWA_GROUP))
    o = o.reshape(bsz, t, SWA_WIDTH)
    y = swa_output(o, g, w_out)
    return y, keys[:, -WINDOW:], vals[:, -WINDOW:]


def setup_inputs(seed: int = 0) -> dict:
    key = jax.random.key(seed)
    ks = jax.random.split(key, 16)
    nrm = jax.random.normal
    f32 = jnp.float32
    swa_in_width = 2 * SWA_WIDTH + 2 * SWA_KV_WIDTH
    return {
        'x_prompt': nrm(ks[0], (BATCH, SEQ, D_MODEL), f32),
        'x_sample': nrm(ks[1], (DEC_BATCH, DEC_SEQ, D_MODEL), f32),
        'state_hgrn': 0.5 * nrm(ks[2], (N_HGRN_LAYERS, DEC_BATCH, HGRN_HEADS, HGRN_DK, HGRN_DV), f32),
        'cache_swa_k': nrm(ks[3], (N_SWA_LAYERS, DEC_BATCH, WINDOW, SWA_KV_HEADS, SWA_HEAD_DIM), f32),
        'cache_swa_v': nrm(ks[4], (N_SWA_LAYERS, DEC_BATCH, WINDOW, SWA_KV_HEADS, SWA_HEAD_DIM), f32),
        'meta_tokens': nrm(ks[5], (N_META, D_MODEL), f32),
        'hgrn_w_in': nrm(ks[6], (N_HGRN_LAYERS, D_MODEL, 4 * HGRN_WIDTH), f32) * D_MODEL ** -0.5,
        'hgrn_lb_logits': 0.1 * nrm(ks[7], (DEPTH + 1, HGRN_WIDTH), f32),
        'hgrn_norm_w': 1.0 + 0.02 * nrm(ks[8], (N_HGRN_LAYERS, HGRN_HEADS, HGRN_DV), f32),
        'hgrn_w_out': nrm(ks[9], (N_HGRN_LAYERS, HGRN_WIDTH, D_MODEL), f32) * (HGRN_WIDTH ** -0.5 * DEEPNORM_BETA),
        'swa_w_in': nrm(ks[10], (N_SWA_LAYERS, D_MODEL, swa_in_width), f32) * D_MODEL ** -0.5,
        'swa_sinks': 0.5 * nrm(ks[11], (N_SWA_LAYERS, SWA_Q_HEADS), f32),
        'swa_w_out': nrm(ks[12], (N_SWA_LAYERS, SWA_WIDTH, D_MODEL), f32) * (SWA_WIDTH ** -0.5 * DEEPNORM_BETA),
        'ln_g': 1.0 + 0.02 * nrm(ks[13], (DEPTH, D_MODEL), f32),
        'ln_b': 0.02 * nrm(ks[14], (DEPTH, D_MODEL), f32),
    }


def reference(x_prompt, x_sample, state_hgrn, cache_swa_k, cache_swa_v, meta_tokens,
              hgrn_w_in, hgrn_lb_logits, hgrn_norm_w, hgrn_w_out,
              swa_w_in, swa_sinks, swa_w_out, ln_g, ln_b):
    out_dtype = x_prompt.dtype
    f32 = jnp.float32
    bsz = x_prompt.shape[0]
    meta = jnp.broadcast_to(meta_tokens.astype(f32)[None], (bsz, N_META, D_MODEL))
    xp = jnp.concatenate([meta, x_prompt.astype(f32)], axis=1)
    xs = x_sample.astype(f32)
    lb_all = jnp.cumsum(jax.nn.softmax(hgrn_lb_logits.astype(f32), axis=0), axis=0)
    st_p, st_s, kp, vp, ksm, vsm = [], [], [], [], [], []
    for i in range(DEPTH):
        j = i // N_MIXERS
        if i % N_MIXERS == 0:
            lb = lb_all[i].reshape(HGRN_HEADS, HGRN_DK)
            s0p = jnp.zeros((bsz, HGRN_HEADS, HGRN_DK, HGRN_DV), f32)
            yp, sp = hgrn_branch(xp, hgrn_w_in[j], lb, hgrn_norm_w[j], hgrn_w_out[j], s0p, N_META, HGRN_CHUNK)
            ys, ss = hgrn_branch(xs, hgrn_w_in[j], lb, hgrn_norm_w[j], hgrn_w_out[j],
                                 state_hgrn[j].astype(f32), 0, xs.shape[1])
            st_p.append(sp)
            st_s.append(ss)
        else:
            yp, kpr, vpr = swa_prompt(xp, swa_w_in[j], swa_sinks[j], swa_w_out[j])
            ys, ksr, vsr = swa_sample(xs, cache_swa_k[j], cache_swa_v[j], swa_w_in[j], swa_sinks[j], swa_w_out[j])
            kp.append(kpr)
            vp.append(vpr)
            ksm.append(ksr)
            vsm.append(vsr)
        xp = layer_norm(DEEPNORM_ALPHA * xp + yp, ln_g[i], ln_b[i])
        xs = layer_norm(DEEPNORM_ALPHA * xs + ys, ln_g[i], ln_b[i])
    y_prompt = xp[:, N_META:].astype(out_dtype)
    y_sample = xs.astype(out_dtype)
    new_state_hgrn_prompt = jnp.stack(st_p).astype(out_dtype)
    new_state_hgrn_sample = jnp.stack(st_s).astype(out_dtype)
    new_cache_swa_k_prompt = jnp.stack(kp).astype(out_dtype)
    new_cache_swa_v_prompt = jnp.stack(vp).astype(out_dtype)
    new_cache_swa_k_sample = jnp.stack(ksm).astype(out_dtype)
    new_cache_swa_v_sample = jnp.stack(vsm).astype(out_dtype)
    return (y_prompt, y_sample, new_state_hgrn_prompt, new_state_hgrn_sample,
            new_cache_swa_k_prompt, new_cache_swa_v_prompt, new_cache_swa_k_sample, new_cache_swa_v_sample)
```

```python
import functools

import numpy as np
import jax
import jax.numpy as jnp
from jax import lax
from jax.experimental import pallas as pl
from jax.experimental.pallas import tpu as pltpu

F32 = jnp.float32
BF16 = jnp.bfloat16

D_MODEL = 2048
N_META = 16
DEPTH = 2
HGRN_HEADS = 16
HGRN_DK = 128
HGRN_DV = 128
SWA_Q_HEADS = 32
SWA_KV_HEADS = 4
SWA_GROUP = 8
SWA_HEAD_DIM = 64
SWA_KV_WIDTH = SWA_KV_HEADS * SWA_HEAD_DIM
SWA_SCALE = SWA_HEAD_DIM ** -0.5
WINDOW = 128
DEEPNORM_ALPHA = (2.0 * DEPTH) ** 0.25
LN_EPS = 1e-5
RMS_EPS = 1e-6

CHUNK = 128
LEVELS = 7
SEQ_BLOCK = 128
VMEM_LIMIT = 56 * 1024 * 1024


def _dot(a, b):
    return jnp.dot(a, b, preferred_element_type=F32)


def _dot_nt(a, b):
    return lax.dot_general(a, b, (((1,), (1,)), ((), ())), preferred_element_type=F32)


def _dot_tn(a, b):
    return lax.dot_general(a, b, (((0,), (0,)), ((), ())), preferred_element_type=F32)


def _silu(x):
    return x * jax.nn.sigmoid(x)


def _mm_kernel(x_ref, w_ref, o_ref):
    o_ref[...] = _dot(x_ref[...].astype(BF16), w_ref[...]).astype(o_ref.dtype)


def _row_tile(n, want):
    if n <= want:
        return n
    return max(t for t in range(16, want + 1, 16) if n % t == 0)


def _matmul(x, w, *, tm, tn, out_dtype=F32):
    n, k = x.shape
    e = w.shape[1]
    tm = _row_tile(n, tm)
    assert n % tm == 0 and e % tn == 0
    return pl.pallas_call(
        _mm_kernel,
        grid=(n // tm, e // tn),
        in_specs=[pl.BlockSpec((tm, k), lambda i, j: (i, 0)),
                  pl.BlockSpec((k, tn), lambda i, j: (0, j))],
        out_specs=pl.BlockSpec((tm, tn), lambda i, j: (i, j)),
        out_shape=jax.ShapeDtypeStruct((n, e), out_dtype),
        compiler_params=pltpu.CompilerParams(
            dimension_semantics=("parallel", "parallel"), vmem_limit_bytes=VMEM_LIMIT),
        name="proj_in",
    )(x, w)


def _mm_ln_kernel(a_ref, w_ref, x_ref, g_ref, b_ref, o_ref):
    z = DEEPNORM_ALPHA * x_ref[...] + _dot(a_ref[...], w_ref[...])
    mu = jnp.mean(z, axis=-1, keepdims=True)
    zc = z - mu
    var = jnp.mean(zc * zc, axis=-1, keepdims=True)
    o_ref[...] = zc * lax.rsqrt(var + LN_EPS) * g_ref[...] + b_ref[...]


def _matmul_deepnorm(a, w, x, g, b, *, tm):
    n, k = a.shape
    d = w.shape[1]
    tm = _row_tile(n, tm)
    return pl.pallas_call(
        _mm_ln_kernel,
        grid=(n // tm,),
        in_specs=[pl.BlockSpec((tm, k), lambda i: (i, 0)),
                  pl.BlockSpec((k, d), lambda i: (0, 0)),
                  pl.BlockSpec((tm, d), lambda i: (i, 0)),
                  pl.BlockSpec((1, d), lambda i: (0, 0)),
                  pl.BlockSpec((1, d), lambda i: (0, 0))],
        out_specs=pl.BlockSpec((tm, d), lambda i: (i, 0)),
        out_shape=jax.ShapeDtypeStruct((n, d), F32),
        compiler_params=pltpu.CompilerParams(
            dimension_semantics=("parallel",), vmem_limit_bytes=VMEM_LIMIT),
        name="proj_out_deepnorm",
    )(a, w, x, g, b)


def _level_masks():
    t = np.arange(CHUNK)
    out = np.zeros((LEVELS + 1, CHUNK, CHUNK), np.float32)
    out[0] = t[:, None] == t[None, :]
    for l in range(LEVELS):
        h = 1 << l
        same = (t[:, None] >> (l + 1)) == (t[None, :] >> (l + 1))
        out[l + 1] = same & ((t[:, None] & h) != 0) & ((t[None, :] & h) == 0)
    return out


def _half_total(cum, level):
    rows, lanes = cum.shape
    blk = 1 << (level + 1)
    idx = (1 << level) - 1
    if blk >= 8:
        x = cum.reshape(rows // blk, blk, lanes)
        return jnp.broadcast_to(x[:, idx:idx + 1, :], x.shape).reshape(rows, lanes)
    x = cum.reshape(rows // 8, 8, lanes)
    sub = lax.broadcasted_iota(jnp.int32, x.shape, 1)
    out = None
    for start in range(8 - blk, -1, -blk):
        piece = jnp.broadcast_to(x[:, start + idx:start + idx + 1, :], x.shape)
        out = piece if out is None else jnp.where(sub < start + blk, piece, out)
    return out.reshape(rows, lanes)


def _gates(fx, lb):
    f = lb + (1.0 - lb) * jax.nn.sigmoid(fx)
    return jnp.log(f), 1.0 - f


def _split3(x):
    hi = x.astype(BF16)
    r = x - hi.astype(F32)
    mid = r.astype(BF16)
    lo = (r - mid.astype(F32)).astype(BF16)
    return hi, mid, lo


def _state_update(s_old, kd, v, blast, pad_rows):
    c = kd.shape[0]
    hi, mid, lo = _split3(jnp.exp(blast))
    row = lax.broadcasted_iota(jnp.int32, (pad_rows, HGRN_DK), 0)
    dec = jnp.where(row == 0, hi.astype(F32),
                    jnp.where(row == 1, mid.astype(F32), jnp.where(row == 2, lo.astype(F32), 0.0)))
    lhs = jnp.concatenate([kd, dec], axis=0).astype(BF16)
    rhs = jnp.concatenate(
        [jnp.concatenate([v, jnp.zeros((c, HGRN_DV), F32)], axis=1),
         jnp.concatenate([jnp.zeros((pad_rows, HGRN_DV), F32), jnp.ones((pad_rows, HGRN_DV), F32)], axis=1)],
        axis=0).astype(BF16)
    both = _dot_tn(lhs, rhs)
    return both[:, HGRN_DV:] * s_old + both[:, :HGRN_DV]


def _norm_gate(o, g, nw):
    o = o * lax.rsqrt(jnp.mean(o * o, axis=-1, keepdims=True) + RMS_EPS) * nw
    return o * _silu(g)


def _hgrn_seq_kernel(q_ref, fx_ref, i_ref, g_ref, lb_ref, nw_ref, s0_ref, m_ref,
                     og_ref, sout_ref, s_scr, *, n_chunks):
    s_scr[...] = s0_ref[0, 0]
    lb = lb_ref[...]
    nw = nw_ref[...]
    row = lax.broadcasted_iota(jnp.int32, (CHUNK, HGRN_DK), 0)

    def chunk(c, carry):
        rows = pl.ds(pl.multiple_of(c * CHUNK, CHUNK), CHUNK)
        q = q_ref[rows, :]
        v = i_ref[rows, :]
        logf, kk = _gates(fx_ref[rows, :], lb)
        a = _dot_nt(q.astype(BF16), kk.astype(BF16)) * m_ref[0]
        cum = logf
        for l in range(LEVELS):
            upper = (row & (1 << l)) != 0
            tot = _half_total(cum, l)
            x = jnp.exp(jnp.where(upper, cum, tot - cum))
            z = (jnp.where(upper, q, kk) * x).astype(BF16)
            a = a + _dot_nt(z, z) * m_ref[l + 1]
            cum = jnp.where(upper, cum + tot, cum)
        blast = cum[CHUNK - 1:CHUNK, :]
        qd = (q * jnp.exp(cum)).astype(BF16)
        kd = kk * jnp.exp(blast - cum)
        s_old = s_scr[...]
        o = _dot(qd, s_old.astype(BF16)) + _dot(a.astype(BF16), v.astype(BF16))
        s_scr[...] = _state_update(s_old, kd, v, blast, 16)
        og_ref[rows, :] = _norm_gate(o, g_ref[rows, :], nw).astype(og_ref.dtype)
        return carry

    lax.fori_loop(0, n_chunks, chunk, 0)
    sout_ref[0, 0] = s_scr[...]


def _hgrn_seq(h, lb, nw, s0, *, n_seq, seq_len, row_block0):
    assert seq_len % CHUNK == 0
    s0_batched = s0.shape[0] != 1
    col = lambda part: (lambda b, hd: (row_block0 + b, part * HGRN_HEADS + hd))
    masks = jnp.asarray(_level_masks())
    return pl.pallas_call(
        functools.partial(_hgrn_seq_kernel, n_chunks=seq_len // CHUNK),
        grid=(n_seq, HGRN_HEADS),
        in_specs=[pl.BlockSpec((seq_len, HGRN_DK), col(0)),
                  pl.BlockSpec((seq_len, HGRN_DK), col(1)),
                  pl.BlockSpec((seq_len, HGRN_DV), col(2)),
                  pl.BlockSpec((seq_len, HGRN_DV), col(3)),
                  pl.BlockSpec((1, HGRN_DK), lambda b, hd: (0, hd)),
                  pl.BlockSpec((1, HGRN_DV), lambda b, hd: (0, hd)),
                  pl.BlockSpec((1, 1, HGRN_DK, HGRN_DV),
                               (lambda b, hd: (b, hd, 0, 0)) if s0_batched else (lambda b, hd: (0, hd, 0, 0))),
                  pl.BlockSpec((LEVELS + 1, CHUNK, CHUNK), lambda b, hd: (0, 0, 0))],
        out_specs=[pl.BlockSpec((seq_len, HGRN_DV), lambda b, hd: (b, hd)),
                   pl.BlockSpec((1, 1, HGRN_DK, HGRN_DV), lambda b, hd: (b, hd, 0, 0))],
        out_shape=[jax.ShapeDtypeStruct((n_seq * seq_len, HGRN_HEADS * HGRN_DV), BF16),
                   jax.ShapeDtypeStruct((n_seq, HGRN_HEADS, HGRN_DK, HGRN_DV), F32)],
        scratch_shapes=[pltpu.VMEM((HGRN_DK, HGRN_DV), F32)],
        compiler_params=pltpu.CompilerParams(
            dimension_semantics=("parallel", "parallel"), vmem_limit_bytes=VMEM_LIMIT),
        name="hgrn_seq",
    )(h, h, h, h, lb, nw, s0, masks)


def _hgrn_step_kernel(q_ref, fx_ref, i_ref, g_ref, lb_ref, nw_ref, s_ref, og_ref, sout_ref, *, n_seq, t):
    lb = lb_ref[...]
    nw = nw_ref[...]
    row = lax.broadcasted_iota(jnp.int32, (t, HGRN_DK), 0)

    def seq(b, carry):
        rows = pl.ds(pl.multiple_of(b * t, t), t)
        q = q_ref[rows, :]
        v = i_ref[rows, :]
        logf, kk = _gates(fx_ref[rows, :], lb)
        cum = logf
        shift = 1
        while shift < t:
            cum = cum + jnp.where(row >= shift, pltpu.roll(cum, shift, 0), 0.0)
            shift *= 2
        intra = (jnp.sum(q * kk, axis=-1, keepdims=True)) * v
        for d in range(1, t):
            valid = row >= d
            x = jnp.exp(jnp.where(valid, cum - pltpu.roll(cum, d, 0), 0.0))
            w = jnp.sum(jnp.where(valid, q * x * pltpu.roll(kk, d, 0), 0.0), axis=-1, keepdims=True)
            intra = intra + w * pltpu.roll(v, d, 0)
        blast = cum[t - 1:t, :]
        qd = (q * jnp.exp(cum)).astype(BF16)
        kd = kk * jnp.exp(blast - cum)
        s_old = s_ref[b, 0]
        o = _dot(qd, s_old.astype(BF16)) + intra
        sout_ref[b, 0] = _state_update(s_old, kd, v, blast, 8)
        og_ref[rows, :] = _norm_gate(o, g_ref[rows, :], nw).astype(og_ref.dtype)
        return carry

    lax.fori_loop(0, n_seq, seq, 0)


def _hgrn_step(h, lb, nw, s0, *, n_seq, t, seq_per_block):
    assert n_seq % seq_per_block == 0 and t == 8
    rows = seq_per_block * t
    col = lambda part: (lambda i, hd: (i, part * HGRN_HEADS + hd))
    return pl.pallas_call(
        functools.partial(_hgrn_step_kernel, n_seq=seq_per_block, t=t),
        grid=(n_seq // seq_per_block, HGRN_HEADS),
        in_specs=[pl.BlockSpec((rows, HGRN_DK), col(0)),
                  pl.BlockSpec((rows, HGRN_DK), col(1)),
                  pl.BlockSpec((rows, HGRN_DV), col(2)),
                  pl.BlockSpec((rows, HGRN_DV), col(3)),
                  pl.BlockSpec((1, HGRN_DK), lambda i, hd: (0, hd)),
                  pl.BlockSpec((1, HGRN_DV), lambda i, hd: (0, hd)),
                  pl.BlockSpec((seq_per_block, 1, HGRN_DK, HGRN_DV), lambda i, hd: (i, hd, 0, 0))],
        out_specs=[pl.BlockSpec((rows, HGRN_DV), lambda i, hd: (i, hd)),
                   pl.BlockSpec((seq_per_block, 1, HGRN_DK, HGRN_DV), lambda i, hd: (i, hd, 0, 0))],
        out_shape=[jax.ShapeDtypeStruct((n_seq * t, HGRN_HEADS * HGRN_DV), BF16),
                   jax.ShapeDtypeStruct((n_seq, HGRN_HEADS, HGRN_DK, HGRN_DV), F32)],
        compiler_params=pltpu.CompilerParams(
            dimension_semantics=("parallel", "parallel"), vmem_limit_bytes=VMEM_LIMIT),
        name="hgrn_step",
    )(h, h, h, h, lb, nw, s0)


def _attn_seq_kernel(q_ref, g_ref, kc_ref, vc_ref, kp_ref, vp_ref, meta_ref, sink_ref, og_ref):
    j = pl.program_id(1)
    first = j == 0
    kprev = jnp.where(first, meta_ref[:, :SWA_KV_WIDTH], kp_ref[...])
    vprev = jnp.where(first, meta_ref[:, SWA_KV_WIDTH:], vp_ref[...])
    kband = jnp.concatenate([kprev, kc_ref[...]], axis=0).astype(BF16)
    vband = jnp.concatenate([vprev, vc_ref[...]], axis=0).astype(BF16)
    qi = lax.broadcasted_iota(jnp.int32, (SEQ_BLOCK, 2 * SEQ_BLOCK), 0)
    kj = lax.broadcasted_iota(jnp.int32, (SEQ_BLOCK, 2 * SEQ_BLOCK), 1)
    dist = SEQ_BLOCK + qi - kj
    mask = (dist >= 0) & (dist < WINDOW) & (jnp.logical_not(first) | (kj >= SEQ_BLOCK - N_META))
    ones = jnp.ones((2 * SEQ_BLOCK, SWA_HEAD_DIM), BF16)
    for kvh in range(SWA_KV_HEADS):
        ks = slice(kvh * SWA_HEAD_DIM, (kvh + 1) * SWA_HEAD_DIM)
        kh = kband[:, ks]
        vaug = jnp.concatenate([vband[:, ks], ones], axis=1)
        for gq in range(SWA_GROUP):
            head = kvh * SWA_GROUP + gq
            hs = slice(head * SWA_HEAD_DIM, (head + 1) * SWA_HEAD_DIM)
            s = _dot_nt(q_ref[:, hs].astype(BF16), kh) * SWA_SCALE
            s = jnp.where(mask, s, -jnp.inf)
            sink = sink_ref[:, head:head + 1]
            m = jnp.maximum(jnp.max(s, axis=-1, keepdims=True), sink)
            p = jnp.exp(s - m)
            oa = _dot(p.astype(BF16), vaug)
            denom = oa[:, SWA_HEAD_DIM:SWA_HEAD_DIM + 1] + jnp.exp(sink - m)
            o = oa[:, :SWA_HEAD_DIM] / denom
            og_ref[:, hs] = (o * _silu(g_ref[:, hs])).astype(og_ref.dtype)


def _attn_seq(h1, h1_meta, sinks, *, n_seq, seq_len, meta_row_block):
    nb = seq_len // SEQ_BLOCK
    width = SWA_Q_HEADS * SWA_HEAD_DIM
    kcol = 2 * width // SWA_KV_WIDTH
    return pl.pallas_call(
        _attn_seq_kernel,
        grid=(n_seq, nb),
        in_specs=[pl.BlockSpec((SEQ_BLOCK, width), lambda b, j: (b * nb + j, 0)),
                  pl.BlockSpec((SEQ_BLOCK, width), lambda b, j: (b * nb + j, 1)),
                  pl.BlockSpec((SEQ_BLOCK, SWA_KV_WIDTH), lambda b, j: (b * nb + j, kcol)),
                  pl.BlockSpec((SEQ_BLOCK, SWA_KV_WIDTH), lambda b, j: (b * nb + j, kcol + 1)),
                  pl.BlockSpec((SEQ_BLOCK, SWA_KV_WIDTH), lambda b, j: (b * nb + jnp.maximum(j - 1, 0), kcol)),
                  pl.BlockSpec((SEQ_BLOCK, SWA_KV_WIDTH), lambda b, j: (b * nb + jnp.maximum(j - 1, 0), kcol + 1)),
                  pl.BlockSpec((SEQ_BLOCK, 2 * SWA_KV_WIDTH), lambda b, j: (meta_row_block, kcol // 2)),
                  pl.BlockSpec((1, SWA_Q_HEADS), lambda b, j: (0, 0))],
        out_specs=pl.BlockSpec((SEQ_BLOCK, width), lambda b, j: (b * nb + j, 0)),
        out_shape=jax.ShapeDtypeStruct((n_seq * seq_len, width), BF16),
        compiler_params=pltpu.CompilerParams(
            dimension_semantics=("parallel", "arbitrary"), vmem_limit_bytes=VMEM_LIMIT),
        name="attn_seq",
    )(h1, h1, h1, h1, h1, h1, h1_meta, sinks)


def _attn_step_kernel(q_ref, g_ref, kn_ref, vn_ref, ck_ref, cv_ref, sink_ref, og_ref, nk_ref, nv_ref, *, n_seq, t):
    keys = WINDOW + t
    kj = lax.broadcasted_iota(jnp.int32, (keys, SWA_GROUP * t), 0)
    qt = lax.broadcasted_iota(jnp.int32, (keys, SWA_GROUP * t), 1) % t
    mask = (kj >= qt + 1) & (kj <= WINDOW + qt)

    def seq(b, carry):
        rows = pl.ds(pl.multiple_of(b * t, t), t)
        kc = ck_ref[b]
        vc = cv_ref[b]
        kn = kn_ref[rows, :]
        vn = vn_ref[rows, :]
        nk_ref[b, 0:WINDOW - t, :] = kc[t:, :]
        nk_ref[b, WINDOW - t:WINDOW, :] = kn
        nv_ref[b, 0:WINDOW - t, :] = vc[t:, :]
        nv_ref[b, WINDOW - t:WINDOW, :] = vn
        kall = jnp.concatenate([kc, kn], axis=0).astype(BF16)
        vall = jnp.concatenate([vc, vn], axis=0).astype(BF16)
        q = q_ref[rows, :]
        g = g_ref[rows, :]
        for kvh in range(SWA_KV_HEADS):
            ks = slice(kvh * SWA_HEAD_DIM, (kvh + 1) * SWA_HEAD_DIM)
            heads = [kvh * SWA_GROUP + gq for gq in range(SWA_GROUP)]
            qs = jnp.concatenate([q[:, hd * SWA_HEAD_DIM:(hd + 1) * SWA_HEAD_DIM] for hd in heads], axis=0)
            s = _dot_nt(kall[:, ks], qs.astype(BF16)) * SWA_SCALE
            s = jnp.where(mask, s, -jnp.inf)
            sink = sink_ref[kvh:kvh + 1, :]
            m = jnp.maximum(jnp.max(s, axis=0, keepdims=True), sink)
            p = jnp.exp(s - m)
            p = p / (jnp.sum(p, axis=0, keepdims=True) + jnp.exp(sink - m))
            o = _dot_tn(p.astype(BF16), vall[:, ks])
            for gq, hd in enumerate(heads):
                hs = slice(hd * SWA_HEAD_DIM, (hd + 1) * SWA_HEAD_DIM)
                og_ref[rows, hs] = (o[gq * t:(gq + 1) * t, :] * _silu(g[:, hs])).astype(og_ref.dtype)
        return carry

    lax.fori_loop(0, n_seq, seq, 0)


def _attn_step(h1, cache_k, cache_v, sink_cols, *, n_seq, t, seq_per_block):
    assert n_seq % seq_per_block == 0
    rows = seq_per_block * t
    width = SWA_Q_HEADS * SWA_HEAD_DIM
    kcol = 2 * width // SWA_KV_WIDTH
    cache_spec = pl.BlockSpec((seq_per_block, WINDOW, SWA_KV_WIDTH), lambda i: (i, 0, 0))
    return pl.pallas_call(
        functools.partial(_attn_step_kernel, n_seq=seq_per_block, t=t),
        grid=(n_seq // seq_per_block,),
        in_specs=[pl.BlockSpec((rows, width), lambda i: (i, 0)),
                  pl.BlockSpec((rows, width), lambda i: (i, 1)),
                  pl.BlockSpec((rows, SWA_KV_WIDTH), lambda i: (i, kcol)),
                  pl.BlockSpec((rows, SWA_KV_WIDTH), lambda i: (i, kcol + 1)),
                  cache_spec, cache_spec,
                  pl.BlockSpec((SWA_KV_HEADS, SWA_GROUP * t), lambda i: (0, 0))],
        out_specs=[pl.BlockSpec((rows, width), lambda i: (i, 0)), cache_spec, cache_spec],
        out_shape=[jax.ShapeDtypeStruct((n_seq * t, width), BF16),
                   jax.ShapeDtypeStruct(cache_k.shape, F32),
                   jax.ShapeDtypeStruct(cache_v.shape, F32)],
        compiler_params=pltpu.CompilerParams(
            dimension_semantics=("parallel",), vmem_limit_bytes=VMEM_LIMIT),
        name="attn_step",
    )(h1, h1, h1, h1, cache_k, cache_v, sink_cols)


def kernel(x_prompt, x_sample, state_hgrn, cache_swa_k, cache_swa_v, meta_tokens,
           hgrn_w_in, hgrn_lb_logits, hgrn_norm_w, hgrn_w_out,
           swa_w_in, swa_sinks, swa_w_out, ln_g, ln_b):
    out_dtype = x_prompt.dtype
    bsz, seq, d = x_prompt.shape
    dec_b, dec_t, _ = x_sample.shape
    n_p = bsz * seq
    n_s = dec_b * dec_t
    width = SWA_Q_HEADS * SWA_HEAD_DIM

    w_in0 = hgrn_w_in[0].astype(BF16)
    w_out0 = hgrn_w_out[0].astype(BF16)
    wi = swa_w_in[0]
    w_in1 = jnp.concatenate([wi[:, :width], wi[:, width + 2 * SWA_KV_WIDTH:],
                             wi[:, width:width + 2 * SWA_KV_WIDTH]], axis=1).astype(BF16)
    w_out1 = swa_w_out[0].astype(BF16)
    lb = jnp.cumsum(jax.nn.softmax(hgrn_lb_logits.astype(F32), axis=0), axis=0)[0:1]
    nw = hgrn_norm_w[0].astype(F32).reshape(1, -1)
    sinks = swa_sinks[0].astype(F32).reshape(1, SWA_Q_HEADS)
    sink_cols = jnp.repeat(sinks.reshape(SWA_KV_HEADS, SWA_GROUP), dec_t, axis=1)
    g0, b0 = ln_g[0:1].astype(F32), ln_b[0:1].astype(F32)
    g1, b1 = ln_g[1:2].astype(F32), ln_b[1:2].astype(F32)

    x_p = x_prompt.astype(F32).reshape(n_p, d)
    meta_block = jnp.concatenate(
        [jnp.zeros((SEQ_BLOCK - N_META, d), F32), meta_tokens.astype(F32)], axis=0)
    x_sm = jnp.concatenate([x_sample.astype(F32).reshape(n_s, d), meta_block], axis=0)
    n_sm = n_s + SEQ_BLOCK
    meta_blk = n_s // SEQ_BLOCK

    h_sm = _matmul(x_sm, w_in0, tm=n_sm, tn=512)
    h_p = _matmul(x_p, w_in0, tm=1024, tn=1024)
    zero_state = jnp.zeros((1, HGRN_HEADS, HGRN_DK, HGRN_DV), F32)
    og_meta, s_meta = _hgrn_seq(h_sm, lb, nw, zero_state, n_seq=1, seq_len=SEQ_BLOCK, row_block0=meta_blk)
    og_s, st_s = _hgrn_step(h_sm, lb, nw, state_hgrn[0].astype(F32), n_seq=dec_b, t=dec_t, seq_per_block=32)
    og_p, st_p = _hgrn_seq(h_p, lb, nw, s_meta, n_seq=bsz, seq_len=seq, row_block0=0)
    x1_sm = _matmul_deepnorm(jnp.concatenate([og_s, og_meta], axis=0), w_out0, x_sm, g0, b0, tm=512)
    x1_p = _matmul_deepnorm(og_p, w_out0, x_p, g0, b0, tm=512)

    h1_sm = _matmul(x1_sm, w_in1, tm=n_sm, tn=512)
    h1_p = _matmul(x1_p, w_in1, tm=1024, tn=1536)
    og1_p = _attn_seq(h1_p, h1_sm, sinks, n_seq=bsz, seq_len=seq, meta_row_block=meta_blk)
    ck = cache_swa_k[0].astype(F32).reshape(dec_b, WINDOW, SWA_KV_WIDTH)
    cv = cache_swa_v[0].astype(F32).reshape(dec_b, WINDOW, SWA_KV_WIDTH)
    og1_s, nk_s, nv_s = _attn_step(h1_sm, ck, cv, sink_cols, n_seq=dec_b, t=dec_t, seq_per_block=16)
    y_p = _matmul_deepnorm(og1_p, w_out1, x1_p, g1, b1, tm=512)
    y_s = _matmul_deepnorm(og1_s, w_out1, x1_sm, g1, b1, tm=512)

    kv_p = h1_p.reshape(bsz, seq, -1)[:, seq - WINDOW:, 2 * width:]
    cache_shape = (1, bsz, WINDOW, SWA_KV_HEADS, SWA_HEAD_DIM)
    return (y_p.reshape(bsz, seq, d).astype(out_dtype),
            y_s.reshape(dec_b, dec_t, d).astype(out_dtype),
            st_p[None].astype(out_dtype),
            st_s[None].astype(out_dtype),
            kv_p[..., :SWA_KV_WIDTH].reshape(cache_shape).astype(out_dtype),
            kv_p[..., SWA_KV_WIDTH:].reshape(cache_shape).astype(out_dtype),
            nk_s.reshape((1,) + cache_swa_k.shape[1:]).astype(out_dtype),
            nv_s.reshape((1,) + cache_swa_v.shape[1:]).astype(out_dtype))
```

```python
import functools

import numpy as np
import jax
import jax.numpy as jnp
from jax import lax
from jax.experimental import pallas as pl
from jax.experimental.pallas import tpu as pltpu

F32 = jnp.float32
BF16 = jnp.bfloat16

D_MODEL = 2048
N_META = 16
DEPTH = 2
HGRN_HEADS = 16
HGRN_DK = 128
HGRN_DV = 128
SWA_Q_HEADS = 32
SWA_KV_HEADS = 4
SWA_GROUP = 8
SWA_HEAD_DIM = 64
SWA_KV_WIDTH = SWA_KV_HEADS * SWA_HEAD_DIM
SWA_SCALE = SWA_HEAD_DIM ** -0.5
WINDOW = 128
DEEPNORM_ALPHA = (2.0 * DEPTH) ** 0.25
LN_EPS = 1e-5
RMS_EPS = 1e-6
LOG2E = 1.4426950408889634

CHUNK = 128
LEVELS = 7
SEQ_BLOCK = 128
VMEM_LIMIT = 56 * 1024 * 1024


def _dot(a, b):
    return jnp.dot(a, b, preferred_element_type=F32)


def _dot_nt(a, b):
    return lax.dot_general(a, b, (((1,), (1,)), ((), ())), preferred_element_type=F32)


def _dot_tn(a, b):
    return lax.dot_general(a, b, (((0,), (0,)), ((), ())), preferred_element_type=F32)


def _silu(x):
    return x * jax.nn.sigmoid(x)


def _mm_kernel(x_ref, w_ref, o_ref):
    o_ref[...] = _dot(x_ref[...].astype(BF16), w_ref[...]).astype(o_ref.dtype)


def _row_tile(n, want):
    if n <= want:
        return n
    return max(t for t in range(16, want + 1, 16) if n % t == 0)


def _matmul(x, w, *, tm, tn, out_dtype=F32):
    n, k = x.shape
    e = w.shape[1]
    tm = _row_tile(n, tm)
    assert n % tm == 0 and e % tn == 0
    return pl.pallas_call(
        _mm_kernel,
        grid=(n // tm, e // tn),
        in_specs=[pl.BlockSpec((tm, k), lambda i, j: (i, 0)),
                  pl.BlockSpec((k, tn), lambda i, j: (0, j))],
        out_specs=pl.BlockSpec((tm, tn), lambda i, j: (i, j)),
        out_shape=jax.ShapeDtypeStruct((n, e), out_dtype),
        compiler_params=pltpu.CompilerParams(
            dimension_semantics=("parallel", "parallel"), vmem_limit_bytes=VMEM_LIMIT),
        name="proj_in",
    )(x, w)


def _mm_ln_kernel(a_ref, w_ref, x_ref, g_ref, b_ref, o_ref):
    z = DEEPNORM_ALPHA * x_ref[...] + _dot(a_ref[...], w_ref[...])
    mu = jnp.mean(z, axis=-1, keepdims=True)
    zc = z - mu
    var = jnp.mean(zc * zc, axis=-1, keepdims=True)
    o_ref[...] = zc * lax.rsqrt(var + LN_EPS) * g_ref[...] + b_ref[...]


def _matmul_deepnorm(a, w, x, g, b, *, tm):
    n, k = a.shape
    d = w.shape[1]
    tm = _row_tile(n, tm)
    return pl.pallas_call(
        _mm_ln_kernel,
        grid=(n // tm,),
        in_specs=[pl.BlockSpec((tm, k), lambda i: (i, 0)),
                  pl.BlockSpec((k, d), lambda i: (0, 0)),
                  pl.BlockSpec((tm, d), lambda i: (i, 0)),
                  pl.BlockSpec((1, d), lambda i: (0, 0)),
                  pl.BlockSpec((1, d), lambda i: (0, 0))],
        out_specs=pl.BlockSpec((tm, d), lambda i: (i, 0)),
        out_shape=jax.ShapeDtypeStruct((n, d), F32),
        compiler_params=pltpu.CompilerParams(
            dimension_semantics=("parallel",), vmem_limit_bytes=VMEM_LIMIT),
        name="proj_out_deepnorm",
    )(a, w, x, g, b)


def _level_masks():
    t = np.arange(CHUNK)
    out = np.zeros((LEVELS + 1, CHUNK, CHUNK), np.float32)
    out[0] = t[:, None] == t[None, :]
    for l in range(LEVELS):
        h = 1 << l
        same = (t[:, None] >> (l + 1)) == (t[None, :] >> (l + 1))
        out[l + 1] = same & ((t[:, None] & h) != 0) & ((t[None, :] & h) == 0)
    return out


def _half_total(cum, level):
    rows, lanes = cum.shape
    blk = 1 << (level + 1)
    idx = (1 << level) - 1
    if blk >= 8:
        x = cum.reshape(rows // blk, blk, lanes)
        return jnp.broadcast_to(x[:, idx:idx + 1, :], x.shape).reshape(rows, lanes)
    x = cum.reshape(rows // 8, 8, lanes)
    sub = lax.broadcasted_iota(jnp.int32, x.shape, 1)
    out = None
    for start in range(8 - blk, -1, -blk):
        piece = jnp.broadcast_to(x[:, start + idx:start + idx + 1, :], x.shape)
        out = piece if out is None else jnp.where(sub < start + blk, piece, out)
    return out.reshape(rows, lanes)


def _gates(fx, lb):
    f = lb + (1.0 - lb) * jax.nn.sigmoid(fx)
    return jnp.log(f), 1.0 - f


def _split3(x):
    hi = x.astype(BF16)
    r = x - hi.astype(F32)
    mid = r.astype(BF16)
    lo = (r - mid.astype(F32)).astype(BF16)
    return hi, mid, lo


def _state_update(s_old, kd, v, blast, pad_rows):
    c = kd.shape[0]
    hi, mid, lo = _split3(jnp.exp(blast))
    row = lax.broadcasted_iota(jnp.int32, (pad_rows, HGRN_DK), 0)
    dec = jnp.where(row == 0, hi.astype(F32),
                    jnp.where(row == 1, mid.astype(F32), jnp.where(row == 2, lo.astype(F32), 0.0)))
    lhs = jnp.concatenate([kd, dec], axis=0).astype(BF16)
    rhs = jnp.concatenate(
        [jnp.concatenate([v, jnp.zeros((c, HGRN_DV), F32)], axis=1),
         jnp.concatenate([jnp.zeros((pad_rows, HGRN_DV), F32), jnp.ones((pad_rows, HGRN_DV), F32)], axis=1)],
        axis=0).astype(BF16)
    both = _dot_tn(lhs, rhs)
    return both[:, HGRN_DV:] * s_old + both[:, :HGRN_DV]


def _norm_gate(o, g, nw):
    o = o * lax.rsqrt(jnp.mean(o * o, axis=-1, keepdims=True) + RMS_EPS) * nw
    return o * _silu(g)


def _hgrn_seq_kernel(q_ref, fx_ref, i_ref, g_ref, lb_ref, nw_ref, s0_ref, m_ref,
                     og_ref, sout_ref, s_scr, *, n_chunks):
    s_scr[...] = s0_ref[0, 0]
    lb = lb_ref[...]
    nw = nw_ref[...]
    row = lax.broadcasted_iota(jnp.int32, (CHUNK, HGRN_DK), 0)

    def chunk(c, carry):
        rows = pl.ds(pl.multiple_of(c * CHUNK, CHUNK), CHUNK)
        q = q_ref[rows, :]
        v = i_ref[rows, :]
        logf, kk = _gates(fx_ref[rows, :], lb)
        a = _dot_nt(q.astype(BF16), kk.astype(BF16)) * m_ref[0]
        cum = logf
        for l in range(LEVELS):
            upper = (row & (1 << l)) != 0
            tot = _half_total(cum, l)
            x = jnp.exp(jnp.where(upper, cum, tot - cum))
            z = (jnp.where(upper, q, kk) * x).astype(BF16)
            a = a + _dot_nt(z, z) * m_ref[l + 1]
            cum = jnp.where(upper, cum + tot, cum)
        blast = cum[CHUNK - 1:CHUNK, :]
        qd = (q * jnp.exp(cum)).astype(BF16)
        kd = kk * jnp.exp(blast - cum)
        s_old = s_scr[...]
        o = _dot(qd, s_old.astype(BF16)) + _dot(a.astype(BF16), v.astype(BF16))
        s_scr[...] = _state_update(s_old, kd, v, blast, 16)
        og_ref[rows, :] = _norm_gate(o, g_ref[rows, :], nw).astype(og_ref.dtype)
        return carry

    lax.fori_loop(0, n_chunks, chunk, 0)
    sout_ref[0, 0] = s_scr[...]


def _hgrn_seq(h, lb, nw, s0, *, n_seq, seq_len, row_block0):
    assert seq_len % CHUNK == 0
    s0_batched = s0.shape[0] != 1
    col = lambda part: (lambda b, hd: (row_block0 + b, part * HGRN_HEADS + hd))
    masks = jnp.asarray(_level_masks())
    return pl.pallas_call(
        functools.partial(_hgrn_seq_kernel, n_chunks=seq_len // CHUNK),
        grid=(n_seq, HGRN_HEADS),
        in_specs=[pl.BlockSpec((seq_len, HGRN_DK), col(0)),
                  pl.BlockSpec((seq_len, HGRN_DK), col(1)),
                  pl.BlockSpec((seq_len, HGRN_DV), col(2)),
                  pl.BlockSpec((seq_len, HGRN_DV), col(3)),
                  pl.BlockSpec((1, HGRN_DK), lambda b, hd: (0, hd)),
                  pl.BlockSpec((1, HGRN_DV), lambda b, hd: (0, hd)),
                  pl.BlockSpec((1, 1, HGRN_DK, HGRN_DV),
                               (lambda b, hd: (b, hd, 0, 0)) if s0_batched else (lambda b, hd: (0, hd, 0, 0))),
                  pl.BlockSpec((LEVELS + 1, CHUNK, CHUNK), lambda b, hd: (0, 0, 0))],
        out_specs=[pl.BlockSpec((seq_len, HGRN_DV), lambda b, hd: (b, hd)),
                   pl.BlockSpec((1, 1, HGRN_DK, HGRN_DV), lambda b, hd: (b, hd, 0, 0))],
        out_shape=[jax.ShapeDtypeStruct((n_seq * seq_len, HGRN_HEADS * HGRN_DV), BF16),
                   jax.ShapeDtypeStruct((n_seq, HGRN_HEADS, HGRN_DK, HGRN_DV), F32)],
        scratch_shapes=[pltpu.VMEM((HGRN_DK, HGRN_DV), F32)],
        compiler_params=pltpu.CompilerParams(
            dimension_semantics=("parallel", "parallel"), vmem_limit_bytes=VMEM_LIMIT),
        name="hgrn_seq",
    )(h, h, h, h, lb, nw, s0, masks)


def _hgrn_step_kernel(q_ref, fx_ref, i_ref, g_ref, lb_ref, nw_ref, s_ref, og_ref, sout_ref, *, n_seq, t):
    lb = lb_ref[...]
    nw = nw_ref[...]
    row = lax.broadcasted_iota(jnp.int32, (t, HGRN_DK), 0)

    def seq(b, carry):
        rows = pl.ds(pl.multiple_of(b * t, t), t)
        q = q_ref[rows, :]
        v = i_ref[rows, :]
        logf, kk = _gates(fx_ref[rows, :], lb)
        cum = logf
        shift = 1
        while shift < t:
            cum = cum + jnp.where(row >= shift, pltpu.roll(cum, shift, 0), 0.0)
            shift *= 2
        intra = (jnp.sum(q * kk, axis=-1, keepdims=True)) * v
        for d in range(1, t):
            valid = row >= d
            x = jnp.exp(jnp.where(valid, cum - pltpu.roll(cum, d, 0), 0.0))
            w = jnp.sum(jnp.where(valid, q * x * pltpu.roll(kk, d, 0), 0.0), axis=-1, keepdims=True)
            intra = intra + w * pltpu.roll(v, d, 0)
        blast = cum[t - 1:t, :]
        qd = (q * jnp.exp(cum)).astype(BF16)
        kd = kk * jnp.exp(blast - cum)
        s_old = s_ref[b, 0]
        o = _dot(qd, s_old.astype(BF16)) + intra
        sout_ref[b, 0] = _state_update(s_old, kd, v, blast, 8)
        og_ref[rows, :] = _norm_gate(o, g_ref[rows, :], nw).astype(og_ref.dtype)
        return carry

    lax.fori_loop(0, n_seq, seq, 0, unroll=8)


def _hgrn_step(h, lb, nw, s0, *, n_seq, t, seq_per_block):
    assert n_seq % seq_per_block == 0 and t == 8
    rows = seq_per_block * t
    col = lambda part: (lambda i, hd: (i, part * HGRN_HEADS + hd))
    return pl.pallas_call(
        functools.partial(_hgrn_step_kernel, n_seq=seq_per_block, t=t),
        grid=(n_seq // seq_per_block, HGRN_HEADS),
        in_specs=[pl.BlockSpec((rows, HGRN_DK), col(0)),
                  pl.BlockSpec((rows, HGRN_DK), col(1)),
                  pl.BlockSpec((rows, HGRN_DV), col(2)),
                  pl.BlockSpec((rows, HGRN_DV), col(3)),
                  pl.BlockSpec((1, HGRN_DK), lambda i, hd: (0, hd)),
                  pl.BlockSpec((1, HGRN_DV), lambda i, hd: (0, hd)),
                  pl.BlockSpec((seq_per_block, 1, HGRN_DK, HGRN_DV), lambda i, hd: (i, hd, 0, 0))],
        out_specs=[pl.BlockSpec((rows, HGRN_DV), lambda i, hd: (i, hd)),
                   pl.BlockSpec((seq_per_block, 1, HGRN_DK, HGRN_DV), lambda i, hd: (i, hd, 0, 0))],
        out_shape=[jax.ShapeDtypeStruct((n_seq * t, HGRN_HEADS * HGRN_DV), BF16),
                   jax.ShapeDtypeStruct((n_seq, HGRN_HEADS, HGRN_DK, HGRN_DV), F32)],
        compiler_params=pltpu.CompilerParams(
            dimension_semantics=("parallel", "parallel"), vmem_limit_bytes=VMEM_LIMIT),
        name="hgrn_step",
    )(h, h, h, h, lb, nw, s0)


def _attn_seq_kernel(q_ref, g_ref, kc_ref, vc_ref, kp_ref, vp_ref, meta_ref, sink_ref, og_ref):
    first = pl.program_id(1) == 0
    kprev = jnp.where(first, meta_ref[:, :SWA_KV_WIDTH], kp_ref[...])
    vprev = jnp.where(first, meta_ref[:, SWA_KV_WIDTH:], vp_ref[...])
    kband = (jnp.concatenate([kprev, kc_ref[...]], axis=0) * (SWA_SCALE * LOG2E)).astype(BF16)
    vband = jnp.concatenate([vprev, vc_ref[...]], axis=0).astype(BF16)
    kj = lax.broadcasted_iota(jnp.int32, (2 * SEQ_BLOCK, SEQ_BLOCK), 0)
    qi = lax.broadcasted_iota(jnp.int32, (2 * SEQ_BLOCK, SEQ_BLOCK), 1)
    dist = SEQ_BLOCK + qi - kj
    valid = (dist >= 0) & (dist < WINDOW) & (jnp.logical_not(first) | (kj >= SEQ_BLOCK - N_META))
    madd = jnp.where(valid, 0.0, -jnp.inf)
    zeros = jnp.zeros((2 * SEQ_BLOCK, SWA_HEAD_DIM), BF16)
    pairs = SWA_GROUP // 2
    pw = 2 * SWA_HEAD_DIM

    def block_diag(x):
        return jnp.concatenate([jnp.concatenate([x, zeros], axis=1),
                                jnp.concatenate([zeros, x], axis=1)], axis=0)

    for kvh in range(SWA_KV_HEADS):
        ks = slice(kvh * SWA_HEAD_DIM, (kvh + 1) * SWA_HEAD_DIM)
        col0 = kvh * SWA_GROUP * SWA_HEAD_DIM
        k2 = block_diag(kband[:, ks])
        v2 = block_diag(vband[:, ks])
        xq = jnp.concatenate([q_ref[:, col0 + p * pw:col0 + (p + 1) * pw].astype(BF16)
                              for p in range(pairs)], axis=0)
        st = _dot_nt(k2, xq)
        for pr in range(pairs):
            head = kvh * SWA_GROUP + 2 * pr
            s2 = st[:, pr * SEQ_BLOCK:(pr + 1) * SEQ_BLOCK].reshape(2, 2 * SEQ_BLOCK, SEQ_BLOCK) + madd[None]
            sink = jnp.concatenate([jnp.full((1, 1, SEQ_BLOCK), sink_ref[head] * LOG2E, F32),
                                    jnp.full((1, 1, SEQ_BLOCK), sink_ref[head + 1] * LOG2E, F32)], axis=0)
            m = jnp.maximum(jnp.max(s2, axis=1, keepdims=True), sink)
            p = jnp.exp2(s2 - m)
            den = jnp.sum(p, axis=1, keepdims=True) + jnp.exp2(sink - m)
            pn = (p * (1.0 / den)).astype(BF16).reshape(4 * SEQ_BLOCK, SEQ_BLOCK)
            o = _dot_tn(pn, v2)
            cs = slice(col0 + pr * pw, col0 + (pr + 1) * pw)
            og_ref[:, cs] = (o * _silu(g_ref[:, cs])).astype(og_ref.dtype)


def _attn_seq(h1, h1_meta, sinks, *, n_seq, seq_len, meta_row_block):
    nb = seq_len // SEQ_BLOCK
    width = SWA_Q_HEADS * SWA_HEAD_DIM
    kcol = 2 * width // SWA_KV_WIDTH
    return pl.pallas_call(
        _attn_seq_kernel,
        grid=(n_seq, nb),
        in_specs=[pl.BlockSpec((SEQ_BLOCK, width), lambda b, j: (b * nb + j, 0)),
                  pl.BlockSpec((SEQ_BLOCK, width), lambda b, j: (b * nb + j, 1)),
                  pl.BlockSpec((SEQ_BLOCK, SWA_KV_WIDTH), lambda b, j: (b * nb + j, kcol)),
                  pl.BlockSpec((SEQ_BLOCK, SWA_KV_WIDTH), lambda b, j: (b * nb + j, kcol + 1)),
                  pl.BlockSpec((SEQ_BLOCK, SWA_KV_WIDTH), lambda b, j: (b * nb + jnp.maximum(j - 1, 0), kcol)),
                  pl.BlockSpec((SEQ_BLOCK, SWA_KV_WIDTH), lambda b, j: (b * nb + jnp.maximum(j - 1, 0), kcol + 1)),
                  pl.BlockSpec((SEQ_BLOCK, 2 * SWA_KV_WIDTH), lambda b, j: (meta_row_block, kcol // 2)),
                  pl.BlockSpec(memory_space=pltpu.SMEM)],
        out_specs=pl.BlockSpec((SEQ_BLOCK, width), lambda b, j: (b * nb + j, 0)),
        out_shape=jax.ShapeDtypeStruct((n_seq * seq_len, width), BF16),
        compiler_params=pltpu.CompilerParams(
            dimension_semantics=("parallel", "arbitrary"), vmem_limit_bytes=VMEM_LIMIT),
        name="attn_seq",
    )(h1, h1, h1, h1, h1, h1, h1_meta, sinks)


def _attn_step_kernel(q_ref, g_ref, kn_ref, vn_ref, ck_ref, cv_ref, sink_ref, og_ref, nk_ref, nv_ref, *, n_seq, t):
    keys = WINDOW + t
    kj = lax.broadcasted_iota(jnp.int32, (keys, SWA_GROUP * t), 0)
    qt = lax.broadcasted_iota(jnp.int32, (keys, SWA_GROUP * t), 1) % t
    mask = (kj >= qt + 1) & (kj <= WINDOW + qt)

    def seq(b, carry):
        rows = pl.ds(pl.multiple_of(b * t, t), t)
        kc = ck_ref[b]
        vc = cv_ref[b]
        kn = kn_ref[rows, :]
        vn = vn_ref[rows, :]
        nk_ref[b, 0:WINDOW - t, :] = kc[t:, :]
        nk_ref[b, WINDOW - t:WINDOW, :] = kn
        nv_ref[b, 0:WINDOW - t, :] = vc[t:, :]
        nv_ref[b, WINDOW - t:WINDOW, :] = vn
        kall = jnp.concatenate([kc, kn], axis=0).astype(BF16)
        vall = jnp.concatenate([vc, vn], axis=0).astype(BF16)
        q = q_ref[rows, :]
        g = g_ref[rows, :]
        for kvh in range(SWA_KV_HEADS):
            ks = slice(kvh * SWA_HEAD_DIM, (kvh + 1) * SWA_HEAD_DIM)
            heads = [kvh * SWA_GROUP + gq for gq in range(SWA_GROUP)]
            qs = jnp.concatenate([q[:, hd * SWA_HEAD_DIM:(hd + 1) * SWA_HEAD_DIM] for hd in heads], axis=0)
            s = _dot_nt(kall[:, ks], qs.astype(BF16)) * SWA_SCALE
            s = jnp.where(mask, s, -jnp.inf)
            sink = sink_ref[kvh:kvh + 1, :]
            m = jnp.maximum(jnp.max(s, axis=0, keepdims=True), sink)
            p = jnp.exp(s - m)
            p = p / (jnp.sum(p, axis=0, keepdims=True) + jnp.exp(sink - m))
            o = _dot_tn(p.astype(BF16), vall[:, ks])
            for gq, hd in enumerate(heads):
                hs = slice(hd * SWA_HEAD_DIM, (hd + 1) * SWA_HEAD_DIM)
                og_ref[rows, hs] = (o[gq * t:(gq + 1) * t, :] * _silu(g[:, hs])).astype(og_ref.dtype)
        return carry

    lax.fori_loop(0, n_seq, seq, 0, unroll=2)


def _attn_step(h1, cache_k, cache_v, sink_cols, *, n_seq, t, seq_per_block):
    assert n_seq % seq_per_block == 0
    rows = seq_per_block * t
    width = SWA_Q_HEADS * SWA_HEAD_DIM
    kcol = 2 * width // SWA_KV_WIDTH
    cache_spec = pl.BlockSpec((seq_per_block, WINDOW, SWA_KV_WIDTH), lambda i: (i, 0, 0))
    return pl.pallas_call(
        functools.partial(_attn_step_kernel, n_seq=seq_per_block, t=t),
        grid=(n_seq // seq_per_block,),
        in_specs=[pl.BlockSpec((rows, width), lambda i: (i, 0)),
                  pl.BlockSpec((rows, width), lambda i: (i, 1)),
                  pl.BlockSpec((rows, SWA_KV_WIDTH), lambda i: (i, kcol)),
                  pl.BlockSpec((rows, SWA_KV_WIDTH), lambda i: (i, kcol + 1)),
                  cache_spec, cache_spec,
                  pl.BlockSpec((SWA_KV_HEADS, SWA_GROUP * t), lambda i: (0, 0))],
        out_specs=[pl.BlockSpec((rows, width), lambda i: (i, 0)), cache_spec, cache_spec],
        out_shape=[jax.ShapeDtypeStruct((n_seq * t, width), BF16),
                   jax.ShapeDtypeStruct(cache_k.shape, F32),
                   jax.ShapeDtypeStruct(cache_v.shape, F32)],
        compiler_params=pltpu.CompilerParams(
            dimension_semantics=("parallel",), vmem_limit_bytes=VMEM_LIMIT),
        name="attn_step",
    )(h1, h1, h1, h1, cache_k, cache_v, sink_cols)


def kernel(x_prompt, x_sample, state_hgrn, cache_swa_k, cache_swa_v, meta_tokens,
           hgrn_w_in, hgrn_lb_logits, hgrn_norm_w, hgrn_w_out,
           swa_w_in, swa_sinks, swa_w_out, ln_g, ln_b):
    out_dtype = x_prompt.dtype
    bsz, seq, d = x_prompt.shape
    dec_b, dec_t, _ = x_sample.shape
    n_p = bsz * seq
    n_s = dec_b * dec_t
    width = SWA_Q_HEADS * SWA_HEAD_DIM

    w_in0 = hgrn_w_in[0].astype(BF16)
    w_out0 = hgrn_w_out[0].astype(BF16)
    wi = swa_w_in[0]
    w_in1 = jnp.concatenate([wi[:, :width], wi[:, width + 2 * SWA_KV_WIDTH:],
                             wi[:, width:width + 2 * SWA_KV_WIDTH]], axis=1).astype(BF16)
    w_out1 = swa_w_out[0].astype(BF16)
    lb = jnp.cumsum(jax.nn.softmax(hgrn_lb_logits.astype(F32), axis=0), axis=0)[0:1]
    nw = hgrn_norm_w[0].astype(F32).reshape(1, -1)
    sinks = swa_sinks[0].astype(F32).reshape(1, SWA_Q_HEADS)
    sink_cols = jnp.repeat(sinks.reshape(SWA_KV_HEADS, SWA_GROUP), dec_t, axis=1)
    g0, b0 = ln_g[0:1].astype(F32), ln_b[0:1].astype(F32)
    g1, b1 = ln_g[1:2].astype(F32), ln_b[1:2].astype(F32)

    x_p = x_prompt.astype(F32).reshape(n_p, d)
    meta_block = jnp.concatenate(
        [jnp.zeros((SEQ_BLOCK - N_META, d), F32), meta_tokens.astype(F32)], axis=0)
    x_sm = jnp.concatenate([x_sample.astype(F32).reshape(n_s, d), meta_block], axis=0)
    n_sm = n_s + SEQ_BLOCK
    meta_blk = n_s // SEQ_BLOCK

    h_sm = _matmul(x_sm, w_in0, tm=n_sm, tn=512)
    h_p = _matmul(x_p, w_in0, tm=1024, tn=1024)
    zero_state = jnp.zeros((1, HGRN_HEADS, HGRN_DK, HGRN_DV), F32)
    og_meta, s_meta = _hgrn_seq(h_sm, lb, nw, zero_state, n_seq=1, seq_len=SEQ_BLOCK, row_block0=meta_blk)
    og_s, st_s = _hgrn_step(h_sm, lb, nw, state_hgrn[0].astype(F32), n_seq=dec_b, t=dec_t, seq_per_block=32)
    og_p, st_p = _hgrn_seq(h_p, lb, nw, s_meta, n_seq=bsz, seq_len=seq, row_block0=0)
    x1_sm = _matmul_deepnorm(jnp.concatenate([og_s, og_meta], axis=0), w_out0, x_sm, g0, b0, tm=512)
    x1_p = _matmul_deepnorm(og_p, w_out0, x_p, g0, b0, tm=512)

    h1_sm = _matmul(x1_sm, w_in1, tm=n_sm, tn=512)
    h1_p = _matmul(x1_p, w_in1, tm=1024, tn=1536)
    og1_p = _attn_seq(h1_p, h1_sm, sinks.reshape(-1), n_seq=bsz, seq_len=seq, meta_row_block=meta_blk)
    ck = cache_swa_k[0].astype(F32).reshape(dec_b, WINDOW, SWA_KV_WIDTH)
    cv = cache_swa_v[0].astype(F32).reshape(dec_b, WINDOW, SWA_KV_WIDTH)
    og1_s, nk_s, nv_s = _attn_step(h1_sm, ck, cv, sink_cols, n_seq=dec_b, t=dec_t, seq_per_block=16)
    y_p = _matmul_deepnorm(og1_p, w_out1, x1_p, g1, b1, tm=512)
    y_s = _matmul_deepnorm(og1_s, w_out1, x1_sm, g1, b1, tm=512)

    kv_p = h1_p.reshape(bsz, seq, -1)[:, seq - WINDOW:, 2 * width:]
    cache_shape = (1, bsz, WINDOW, SWA_KV_HEADS, SWA_HEAD_DIM)
    return (y_p.reshape(bsz, seq, d).astype(out_dtype),
            y_s.reshape(dec_b, dec_t, d).astype(out_dtype),
            st_p[None].astype(out_dtype),
            st_s[None].astype(out_dtype),
            kv_p[..., :SWA_KV_WIDTH].reshape(cache_shape).astype(out_dtype),
            kv_p[..., SWA_KV_WIDTH:].reshape(cache_shape).astype(out_dtype),
            nk_s.reshape((1,) + cache_swa_k.shape[1:]).astype(out_dtype),
            nv_s.reshape((1,) + cache_swa_v.shape[1:]).astype(out_dtype))
```

```python
import functools

import numpy as np
import jax
import jax.numpy as jnp
from jax import lax
from jax.experimental import pallas as pl
from jax.experimental.pallas import tpu as pltpu

F32 = jnp.float32
BF16 = jnp.bfloat16

D_MODEL = 2048
N_META = 16
DEPTH = 2
HGRN_HEADS = 16
HGRN_DK = 128
HGRN_DV = 128
SWA_Q_HEADS = 32
SWA_KV_HEADS = 4
SWA_GROUP = 8
SWA_HEAD_DIM = 64
SWA_KV_WIDTH = SWA_KV_HEADS * SWA_HEAD_DIM
SWA_SCALE = SWA_HEAD_DIM ** -0.5
WINDOW = 128
DEEPNORM_ALPHA = (2.0 * DEPTH) ** 0.25
LN_EPS = 1e-5
RMS_EPS = 1e-6
LOG2E = 1.4426950408889634

CHUNK = 128
LEVELS = 7
MASK_ROWS = 16
SEQ_BLOCK = 128
HGRN_SEQ_HEADS = 4
VMEM_LIMIT = 56 * 1024 * 1024


def _dot(a, b):
    return jnp.dot(a, b, preferred_element_type=F32)


def _dot_nt(a, b):
    return lax.dot_general(a, b, (((1,), (1,)), ((), ())), preferred_element_type=F32)


def _dot_tn(a, b):
    return lax.dot_general(a, b, (((0,), (0,)), ((), ())), preferred_element_type=F32)


def _silu(x):
    return x * jax.nn.sigmoid(x)


def _mm_kernel(x_ref, w_ref, o_ref):
    o_ref[...] = _dot(x_ref[...].astype(BF16), w_ref[...]).astype(o_ref.dtype)


def _row_tile(n, want):
    if n <= want:
        return n
    return max(t for t in range(16, want + 1, 16) if n % t == 0)


def _matmul(x, w, *, tm, tn, out_dtype=F32):
    n, k = x.shape
    e = w.shape[1]
    tm = _row_tile(n, tm)
    assert n % tm == 0 and e % tn == 0
    return pl.pallas_call(
        _mm_kernel,
        grid=(n // tm, e // tn),
        in_specs=[pl.BlockSpec((tm, k), lambda i, j: (i, 0)),
                  pl.BlockSpec((k, tn), lambda i, j: (0, j))],
        out_specs=pl.BlockSpec((tm, tn), lambda i, j: (i, j)),
        out_shape=jax.ShapeDtypeStruct((n, e), out_dtype),
        compiler_params=pltpu.CompilerParams(
            dimension_semantics=("parallel", "parallel"), vmem_limit_bytes=VMEM_LIMIT),
        name="proj_in",
    )(x, w)


def _mm_ln_kernel(a_ref, w_ref, x_ref, g_ref, b_ref, o_ref):
    z = DEEPNORM_ALPHA * x_ref[...] + _dot(a_ref[...], w_ref[...])
    mu = jnp.mean(z, axis=-1, keepdims=True)
    zc = z - mu
    var = jnp.mean(zc * zc, axis=-1, keepdims=True)
    o_ref[...] = zc * lax.rsqrt(var + LN_EPS) * g_ref[...] + b_ref[...]


def _matmul_deepnorm(a, w, x, g, b, *, tm):
    n, k = a.shape
    d = w.shape[1]
    tm = _row_tile(n, tm)
    return pl.pallas_call(
        _mm_ln_kernel,
        grid=(n // tm,),
        in_specs=[pl.BlockSpec((tm, k), lambda i: (i, 0)),
                  pl.BlockSpec((k, d), lambda i: (0, 0)),
                  pl.BlockSpec((tm, d), lambda i: (i, 0)),
                  pl.BlockSpec((1, d), lambda i: (0, 0)),
                  pl.BlockSpec((1, d), lambda i: (0, 0))],
        out_specs=pl.BlockSpec((tm, d), lambda i: (i, 0)),
        out_shape=jax.ShapeDtypeStruct((n, d), F32),
        compiler_params=pltpu.CompilerParams(
            dimension_semantics=("parallel",), vmem_limit_bytes=VMEM_LIMIT),
        name="proj_out_deepnorm",
    )(a, w, x, g, b)


def _level_masks():
    t = np.arange(CHUNK)
    out = np.zeros((LEVELS + 1, CHUNK, CHUNK), np.float32)
    out[0] = t[:, None] == t[None, :]
    for l in range(LEVELS):
        h = 1 << l
        same = (t[:, None] >> (l + 1)) == (t[None, :] >> (l + 1))
        out[l + 1] = same & ((t[:, None] & h) != 0) & ((t[None, :] & h) == 0)
    return out


def _half_total(cum, level):
    rows, lanes = cum.shape
    blk = 1 << (level + 1)
    idx = (1 << level) - 1
    if blk >= 8:
        x = cum.reshape(rows // blk, blk, lanes)
        return jnp.broadcast_to(x[:, idx:idx + 1, :], x.shape).reshape(rows, lanes)
    x = cum.reshape(rows // 8, 8, lanes)
    sub = lax.broadcasted_iota(jnp.int32, x.shape, 1)
    out = None
    for start in range(8 - blk, -1, -blk):
        piece = jnp.broadcast_to(x[:, start + idx:start + idx + 1, :], x.shape)
        out = piece if out is None else jnp.where(sub < start + blk, piece, out)
    return out.reshape(rows, lanes)


def _gates(fx, lb):
    f = lb + (1.0 - lb) * jax.nn.sigmoid(fx)
    return f, jnp.log2(f), 1.0 - f


def _small_levels(logf, f, q, kk):
    tiles = (CHUNK // 8, 8, HGRN_DK)
    c, f3, q3, k3 = (x.reshape(tiles) for x in (logf, f, q, kk))
    sub = lax.broadcasted_iota(jnp.int32, (1, 8, HGRN_DK), 1)
    bcast = lambda x, r: jnp.broadcast_to(x[:, r:r + 1, :], tiles)
    up = (sub & 1) != 0
    zs = [jnp.where(up, q3 * f3, k3)]
    c = c + jnp.where(up, pltpu.roll(c, 1, 1), 0.0)
    for level, tot in ((1, lambda c: jnp.where(sub < 4, bcast(c, 1), bcast(c, 5))),
                       (2, lambda c: bcast(c, 3))):
        up = (sub & (1 << level)) != 0
        t = tot(c)
        zs.append(jnp.where(up, q3, k3) * jnp.exp2(jnp.where(up, c, t - c)))
        c = c + jnp.where(up, t, 0.0)
    return [z.reshape(CHUNK, HGRN_DK) for z in zs], c.reshape(CHUNK, HGRN_DK)


def _level_large(cum, q, kk, level):
    half = 1 << level
    args, bases, cums = [], [], []
    for r0 in range(0, CHUNK, 2 * half):
        lo, up = slice(r0, r0 + half), slice(r0 + half, r0 + 2 * half)
        tot = cum[r0 + half - 1:r0 + half, :]
        args += [tot - cum[lo], cum[up]]
        bases += [kk[lo], q[up]]
        cums += [cum[lo], cum[up] + tot]
    x = jnp.exp2(jnp.concatenate(args, axis=0))
    return jnp.concatenate(bases, axis=0) * x, jnp.concatenate(cums, axis=0)


def _split3(x):
    hi = x.astype(BF16)
    r = x - hi.astype(F32)
    mid = r.astype(BF16)
    lo = (r - mid.astype(F32)).astype(BF16)
    return hi, mid, lo


def _state_update(s_old, kd, v, blast, pad_rows):
    c = kd.shape[0]
    hi, mid, lo = _split3(jnp.exp2(blast))
    row = lax.broadcasted_iota(jnp.int32, (pad_rows, HGRN_DK), 0)
    dec = jnp.where(row == 0, hi.astype(F32),
                    jnp.where(row == 1, mid.astype(F32), jnp.where(row == 2, lo.astype(F32), 0.0)))
    lhs = jnp.concatenate([kd, dec], axis=0).astype(BF16)
    rhs = jnp.concatenate(
        [jnp.concatenate([v, jnp.zeros((c, HGRN_DV), F32)], axis=1),
         jnp.concatenate([jnp.zeros((pad_rows, HGRN_DV), F32), jnp.ones((pad_rows, HGRN_DV), F32)], axis=1)],
        axis=0).astype(BF16)
    both = _dot_tn(lhs, rhs)
    return both[:, HGRN_DV:] * s_old + both[:, :HGRN_DV]


def _norm_gate(o, g, nw):
    o = o * lax.rsqrt(jnp.mean(o * o, axis=-1, keepdims=True) + RMS_EPS) * nw
    return o * _silu(g)


def _hgrn_seq_kernel(q_ref, fx_ref, i_ref, g_ref, lb_ref, nw_ref, s0_ref, m_ref,
                     og_ref, sout_ref, s_scr, *, n_chunks):
    heads = s_scr.shape[0]
    s_scr[...] = s0_ref[0]

    def chunk(c, carry):
        rows = pl.ds(pl.multiple_of(c * CHUNK, CHUNK), CHUNK)
        hcols = [slice(hd * HGRN_DK, (hd + 1) * HGRN_DK) for hd in range(heads)]
        q = [q_ref[rows, cs] for cs in hcols]
        v = [i_ref[rows, cs] for cs in hcols]
        gates = [_gates(fx_ref[rows, cs], lb_ref[:, cs]) for cs in hcols]
        kk = [gt[2] for gt in gates]
        small = [_small_levels(gt[1], gt[0], q[hd], kk[hd]) for hd, gt in enumerate(gates)]
        zs = [sm[0] for sm in small]
        cum = [sm[1] for sm in small]
        for level in range(3, LEVELS):
            for hd in range(heads):
                z, cum[hd] = _level_large(cum[hd], q[hd], kk[hd], level)
                zs[hd].append(z)
        nblk = CHUNK // MASK_ROWS
        a = [[None] * nblk for _ in range(heads)]
        for level in range(LEVELS):
            step = 1 << (level - 4) if level >= 4 else 0
            blocks = [i for i in range(nblk) if level < 4 or (i & step)]
            for hd in range(heads):
                z = zs[hd][level].astype(BF16)
                lhs = z if level < 4 else jnp.concatenate(
                    [z[i * MASK_ROWS:(i + 1) * MASK_ROWS] for i in blocks], axis=0)
                term = _dot_nt(lhs, z)
                for n, i in enumerate(blocks):
                    t = (term[n * MASK_ROWS:(n + 1) * MASK_ROWS]
                         * m_ref[level + 1, i * MASK_ROWS:(i + 1) * MASK_ROWS, :])
                    a[hd][i] = t if a[hd][i] is None else a[hd][i] + t
        for hd in range(heads):
            blast = cum[hd][CHUNK - 1:CHUNK, :]
            qd = (q[hd] * jnp.exp2(cum[hd])).astype(BF16)
            kd = kk[hd] * jnp.exp2(blast - cum[hd])
            s_old = s_scr[hd]
            amat = jnp.concatenate(a[hd], axis=0).astype(BF16)
            o = (_dot(qd, s_old.astype(BF16)) + _dot(amat, v[hd].astype(BF16))
                 + jnp.sum(q[hd] * kk[hd], axis=-1, keepdims=True) * v[hd])
            s_scr[hd] = _state_update(s_old, kd, v[hd], blast, 16)
            og_ref[rows, hcols[hd]] = _norm_gate(
                o, g_ref[rows, hcols[hd]], nw_ref[:, hcols[hd]]).astype(og_ref.dtype)
        return carry

    lax.fori_loop(0, n_chunks, chunk, 0)
    sout_ref[0] = s_scr[...]


def _hgrn_seq(h, lb, nw, s0, *, n_seq, seq_len, row_block0):
    assert seq_len % CHUNK == 0
    hb = HGRN_SEQ_HEADS
    width = hb * HGRN_DK
    groups = HGRN_HEADS // hb
    s0_batched = s0.shape[0] != 1
    col = lambda part: (lambda b, hg: (row_block0 + b, part * groups + hg))
    masks = jnp.asarray(_level_masks())
    return pl.pallas_call(
        functools.partial(_hgrn_seq_kernel, n_chunks=seq_len // CHUNK),
        grid=(n_seq, groups),
        in_specs=[pl.BlockSpec((seq_len, width), col(0)),
                  pl.BlockSpec((seq_len, width), col(1)),
                  pl.BlockSpec((seq_len, width), col(2)),
                  pl.BlockSpec((seq_len, width), col(3)),
                  pl.BlockSpec((1, width), lambda b, hg: (0, hg)),
                  pl.BlockSpec((1, width), lambda b, hg: (0, hg)),
                  pl.BlockSpec((1, hb, HGRN_DK, HGRN_DV),
                               (lambda b, hg: (b, hg, 0, 0)) if s0_batched else (lambda b, hg: (0, hg, 0, 0))),
                  pl.BlockSpec((LEVELS + 1, CHUNK, CHUNK), lambda b, hg: (0, 0, 0))],
        out_specs=[pl.BlockSpec((seq_len, width), lambda b, hg: (b, hg)),
                   pl.BlockSpec((1, hb, HGRN_DK, HGRN_DV), lambda b, hg: (b, hg, 0, 0))],
        out_shape=[jax.ShapeDtypeStruct((n_seq * seq_len, HGRN_HEADS * HGRN_DV), BF16),
                   jax.ShapeDtypeStruct((n_seq, HGRN_HEADS, HGRN_DK, HGRN_DV), F32)],
        scratch_shapes=[pltpu.VMEM((hb, HGRN_DK, HGRN_DV), F32)],
        compiler_params=pltpu.CompilerParams(
            dimension_semantics=("parallel", "parallel"), vmem_limit_bytes=VMEM_LIMIT),
        name="hgrn_seq",
    )(h, h, h, h, lb, nw, s0, masks)


def _hgrn_step_kernel(q_ref, fx_ref, i_ref, g_ref, lb_ref, nw_ref, s_ref, og_ref, sout_ref, *, n_seq, t):
    lb = lb_ref[...]
    nw = nw_ref[...]
    row = lax.broadcasted_iota(jnp.int32, (t, HGRN_DK), 0)

    def seq(b, carry):
        rows = pl.ds(pl.multiple_of(b * t, t), t)
        q = q_ref[rows, :]
        v = i_ref[rows, :]
        _, logf, kk = _gates(fx_ref[rows, :], lb)
        cum = logf
        shift = 1
        while shift < t:
            cum = cum + jnp.where(row >= shift, pltpu.roll(cum, shift, 0), 0.0)
            shift *= 2
        intra = (jnp.sum(q * kk, axis=-1, keepdims=True)) * v
        for d in range(1, t):
            valid = row >= d
            x = jnp.exp2(jnp.where(valid, cum - pltpu.roll(cum, d, 0), 0.0))
            w = jnp.sum(jnp.where(valid, q * x * pltpu.roll(kk, d, 0), 0.0), axis=-1, keepdims=True)
            intra = intra + w * pltpu.roll(v, d, 0)
        blast = cum[t - 1:t, :]
        qd = (q * jnp.exp2(cum)).astype(BF16)
        kd = kk * jnp.exp2(blast - cum)
        s_old = s_ref[b, 0]
        o = _dot(qd, s_old.astype(BF16)) + intra
        sout_ref[b, 0] = _state_update(s_old, kd, v, blast, 8)
        og_ref[rows, :] = _norm_gate(o, g_ref[rows, :], nw).astype(og_ref.dtype)
        return carry

    lax.fori_loop(0, n_seq, seq, 0, unroll=8)


def _hgrn_step(h, lb, nw, s0, *, n_seq, t, seq_per_block):
    assert n_seq % seq_per_block == 0 and t == 8
    rows = seq_per_block * t
    col = lambda part: (lambda i, hd: (i, part * HGRN_HEADS + hd))
    return pl.pallas_call(
        functools.partial(_hgrn_step_kernel, n_seq=seq_per_block, t=t),
        grid=(n_seq // seq_per_block, HGRN_HEADS),
        in_specs=[pl.BlockSpec((rows, HGRN_DK), col(0)),
                  pl.BlockSpec((rows, HGRN_DK), col(1)),
                  pl.BlockSpec((rows, HGRN_DV), col(2)),
                  pl.BlockSpec((rows, HGRN_DV), col(3)),
                  pl.BlockSpec((1, HGRN_DK), lambda i, hd: (0, hd)),
                  pl.BlockSpec((1, HGRN_DV), lambda i, hd: (0, hd)),
                  pl.BlockSpec((seq_per_block, 1, HGRN_DK, HGRN_DV), lambda i, hd: (i, hd, 0, 0))],
        out_specs=[pl.BlockSpec((rows, HGRN_DV), lambda i, hd: (i, hd)),
                   pl.BlockSpec((seq_per_block, 1, HGRN_DK, HGRN_DV), lambda i, hd: (i, hd, 0, 0))],
        out_shape=[jax.ShapeDtypeStruct((n_seq * t, HGRN_HEADS * HGRN_DV), BF16),
                   jax.ShapeDtypeStruct((n_seq, HGRN_HEADS, HGRN_DK, HGRN_DV), F32)],
        compiler_params=pltpu.CompilerParams(
            dimension_semantics=("parallel", "parallel"), vmem_limit_bytes=VMEM_LIMIT),
        name="hgrn_step",
    )(h, h, h, h, lb, nw, s0)


def _attn_seq_kernel(q_ref, g_ref, kc_ref, vc_ref, kp_ref, vp_ref, meta_ref, sink_ref, og_ref):
    first = pl.program_id(1) == 0
    kprev = jnp.where(first, meta_ref[:, :SWA_KV_WIDTH], kp_ref[...])
    vprev = jnp.where(first, meta_ref[:, SWA_KV_WIDTH:], vp_ref[...])
    kband = (jnp.concatenate([kprev, kc_ref[...]], axis=0) * (SWA_SCALE * LOG2E)).astype(BF16)
    vband = jnp.concatenate([vprev, vc_ref[...]], axis=0).astype(BF16)
    kj = lax.broadcasted_iota(jnp.int32, (2 * SEQ_BLOCK, SEQ_BLOCK), 0)
    qi = lax.broadcasted_iota(jnp.int32, (2 * SEQ_BLOCK, SEQ_BLOCK), 1)
    dist = SEQ_BLOCK + qi - kj
    valid = (dist >= 0) & (dist < WINDOW) & (jnp.logical_not(first) | (kj >= SEQ_BLOCK - N_META))
    madd = jnp.where(valid, 0.0, -jnp.inf)
    zeros = jnp.zeros((2 * SEQ_BLOCK, SWA_HEAD_DIM), BF16)
    pairs = SWA_GROUP // 2
    pw = 2 * SWA_HEAD_DIM

    def block_diag(x):
        return jnp.concatenate([jnp.concatenate([x, zeros], axis=1),
                                jnp.concatenate([zeros, x], axis=1)], axis=0)

    for kvh in range(SWA_KV_HEADS):
        ks = slice(kvh * SWA_HEAD_DIM, (kvh + 1) * SWA_HEAD_DIM)
        col0 = kvh * SWA_GROUP * SWA_HEAD_DIM
        k2 = block_diag(kband[:, ks])
        v2 = block_diag(vband[:, ks])
        xq = jnp.concatenate([q_ref[:, col0 + p * pw:col0 + (p + 1) * pw].astype(BF16)
                              for p in range(pairs)], axis=0)
        st = _dot_nt(k2, xq)
        for pr in range(pairs):
            head = kvh * SWA_GROUP + 2 * pr
            s2 = st[:, pr * SEQ_BLOCK:(pr + 1) * SEQ_BLOCK].reshape(2, 2 * SEQ_BLOCK, SEQ_BLOCK) + madd[None]
            sink = jnp.concatenate([jnp.full((1, 1, SEQ_BLOCK), sink_ref[head] * LOG2E, F32),
                                    jnp.full((1, 1, SEQ_BLOCK), sink_ref[head + 1] * LOG2E, F32)], axis=0)
            m = jnp.maximum(jnp.max(s2, axis=1, keepdims=True), sink)
            p = jnp.exp2(s2 - m)
            den = jnp.sum(p, axis=1, keepdims=True) + jnp.exp2(sink - m)
            pn = (p * (1.0 / den)).astype(BF16).reshape(4 * SEQ_BLOCK, SEQ_BLOCK)
            o = _dot_tn(pn, v2)
            cs = slice(col0 + pr * pw, col0 + (pr + 1) * pw)
            og_ref[:, cs] = (o * _silu(g_ref[:, cs])).astype(og_ref.dtype)


def _attn_seq(h1, h1_meta, sinks, *, n_seq, seq_len, meta_row_block):
    nb = seq_len // SEQ_BLOCK
    width = SWA_Q_HEADS * SWA_HEAD_DIM
    kcol = 2 * width // SWA_KV_WIDTH
    return pl.pallas_call(
        _attn_seq_kernel,
        grid=(n_seq, nb),
        in_specs=[pl.BlockSpec((SEQ_BLOCK, width), lambda b, j: (b * nb + j, 0)),
                  pl.BlockSpec((SEQ_BLOCK, width), lambda b, j: (b * nb + j, 1)),
                  pl.BlockSpec((SEQ_BLOCK, SWA_KV_WIDTH), lambda b, j: (b * nb + j, kcol)),
                  pl.BlockSpec((SEQ_BLOCK, SWA_KV_WIDTH), lambda b, j: (b * nb + j, kcol + 1)),
                  pl.BlockSpec((SEQ_BLOCK, SWA_KV_WIDTH), lambda b, j: (b * nb + jnp.maximum(j - 1, 0), kcol)),
                  pl.BlockSpec((SEQ_BLOCK, SWA_KV_WIDTH), lambda b, j: (b * nb + jnp.maximum(j - 1, 0), kcol + 1)),
                  pl.BlockSpec((SEQ_BLOCK, 2 * SWA_KV_WIDTH), lambda b, j: (meta_row_block, kcol // 2)),
                  pl.BlockSpec(memory_space=pltpu.SMEM)],
        out_specs=pl.BlockSpec((SEQ_BLOCK, width), lambda b, j: (b * nb + j, 0)),
        out_shape=jax.ShapeDtypeStruct((n_seq * seq_len, width), BF16),
        compiler_params=pltpu.CompilerParams(
            dimension_semantics=("parallel", "arbitrary"), vmem_limit_bytes=VMEM_LIMIT),
        name="attn_seq",
    )(h1, h1, h1, h1, h1, h1, h1_meta, sinks)


def _attn_step_kernel(q_ref, g_ref, kn_ref, vn_ref, ck_ref, cv_ref, sink_ref, og_ref, nk_ref, nv_ref, *, n_seq, t):
    keys = WINDOW + t
    kj = lax.broadcasted_iota(jnp.int32, (keys, SWA_GROUP * t), 0)
    qt = lax.broadcasted_iota(jnp.int32, (keys, SWA_GROUP * t), 1) % t
    mask = (kj >= qt + 1) & (kj <= WINDOW + qt)

    def seq(b, carry):
        rows = pl.ds(pl.multiple_of(b * t, t), t)
        kc = ck_ref[b]
        vc = cv_ref[b]
        kn = kn_ref[rows, :]
        vn = vn_ref[rows, :]
        nk_ref[b, 0:WINDOW - t, :] = kc[t:, :]
        nk_ref[b, WINDOW - t:WINDOW, :] = kn
        nv_ref[b, 0:WINDOW - t, :] = vc[t:, :]
        nv_ref[b, WINDOW - t:WINDOW, :] = vn
        kall = jnp.concatenate([kc, kn], axis=0).astype(BF16)
        vall = jnp.concatenate([vc, vn], axis=0).astype(BF16)
        q = q_ref[rows, :]
        g = g_ref[rows, :]
        for kvh in range(SWA_KV_HEADS):
            ks = slice(kvh * SWA_HEAD_DIM, (kvh + 1) * SWA_HEAD_DIM)
            heads = [kvh * SWA_GROUP + gq for gq in range(SWA_GROUP)]
            qs = jnp.concatenate([q[:, hd * SWA_HEAD_DIM:(hd + 1) * SWA_HEAD_DIM] for hd in heads], axis=0)
            s = _dot_nt(kall[:, ks], qs.astype(BF16)) * SWA_SCALE
            s = jnp.where(mask, s, -jnp.inf)
            sink = sink_ref[kvh:kvh + 1, :]
            m = jnp.maximum(jnp.max(s, axis=0, keepdims=True), sink)
            p = jnp.exp(s - m)
            p = p / (jnp.sum(p, axis=0, keepdims=True) + jnp.exp(sink - m))
            o = _dot_tn(p.astype(BF16), vall[:, ks])
            for gq, hd in enumerate(heads):
                hs = slice(hd * SWA_HEAD_DIM, (hd + 1) * SWA_HEAD_DIM)
                og_ref[rows, hs] = (o[gq * t:(gq + 1) * t, :] * _silu(g[:, hs])).astype(og_ref.dtype)
        return carry

    lax.fori_loop(0, n_seq, seq, 0, unroll=2)


def _attn_step(h1, cache_k, cache_v, sink_cols, *, n_seq, t, seq_per_block):
    assert n_seq % seq_per_block == 0
    rows = seq_per_block * t
    width = SWA_Q_HEADS * SWA_HEAD_DIM
    kcol = 2 * width // SWA_KV_WIDTH
    cache_spec = pl.BlockSpec((seq_per_block, WINDOW, SWA_KV_WIDTH), lambda i: (i, 0, 0))
    return pl.pallas_call(
        functools.partial(_attn_step_kernel, n_seq=seq_per_block, t=t),
        grid=(n_seq // seq_per_block,),
        in_specs=[pl.BlockSpec((rows, width), lambda i: (i, 0)),
                  pl.BlockSpec((rows, width), lambda i: (i, 1)),
                  pl.BlockSpec((rows, SWA_KV_WIDTH), lambda i: (i, kcol)),
                  pl.BlockSpec((rows, SWA_KV_WIDTH), lambda i: (i, kcol + 1)),
                  cache_spec, cache_spec,
                  pl.BlockSpec((SWA_KV_HEADS, SWA_GROUP * t), lambda i: (0, 0))],
        out_specs=[pl.BlockSpec((rows, width), lambda i: (i, 0)), cache_spec, cache_spec],
        out_shape=[jax.ShapeDtypeStruct((n_seq * t, width), BF16),
                   jax.ShapeDtypeStruct(cache_k.shape, F32),
                   jax.ShapeDtypeStruct(cache_v.shape, F32)],
        compiler_params=pltpu.CompilerParams(
            dimension_semantics=("parallel",), vmem_limit_bytes=VMEM_LIMIT),
        name="attn_step",
    )(h1, h1, h1, h1, cache_k, cache_v, sink_cols)


def kernel(x_prompt, x_sample, state_hgrn, cache_swa_k, cache_swa_v, meta_tokens,
           hgrn_w_in, hgrn_lb_logits, hgrn_norm_w, hgrn_w_out,
           swa_w_in, swa_sinks, swa_w_out, ln_g, ln_b):
    out_dtype = x_prompt.dtype
    bsz, seq, d = x_prompt.shape
    dec_b, dec_t, _ = x_sample.shape
    n_p = bsz * seq
    n_s = dec_b * dec_t
    width = SWA_Q_HEADS * SWA_HEAD_DIM

    w_in0 = hgrn_w_in[0].astype(BF16)
    w_out0 = hgrn_w_out[0].astype(BF16)
    wi = swa_w_in[0]
    w_in1 = jnp.concatenate([wi[:, :width], wi[:, width + 2 * SWA_KV_WIDTH:],
                             wi[:, width:width + 2 * SWA_KV_WIDTH]], axis=1).astype(BF16)
    w_out1 = swa_w_out[0].astype(BF16)
    lb = jnp.cumsum(jax.nn.softmax(hgrn_lb_logits.astype(F32), axis=0), axis=0)[0:1]
    nw = hgrn_norm_w[0].astype(F32).reshape(1, -1)
    sinks = swa_sinks[0].astype(F32).reshape(1, SWA_Q_HEADS)
    sink_cols = jnp.repeat(sinks.reshape(SWA_KV_HEADS, SWA_GROUP), dec_t, axis=1)
    g0, b0 = ln_g[0:1].astype(F32), ln_b[0:1].astype(F32)
    g1, b1 = ln_g[1:2].astype(F32), ln_b[1:2].astype(F32)

    x_p = x_prompt.astype(F32).reshape(n_p, d)
    meta_block = jnp.concatenate(
        [jnp.zeros((SEQ_BLOCK - N_META, d), F32), meta_tokens.astype(F32)], axis=0)
    x_sm = jnp.concatenate([x_sample.astype(F32).reshape(n_s, d), meta_block], axis=0)
    n_sm = n_s + SEQ_BLOCK
    meta_blk = n_s // SEQ_BLOCK

    h_sm = _matmul(x_sm, w_in0, tm=n_sm, tn=512)
    h_p = _matmul(x_p, w_in0, tm=1024, tn=1024)
    zero_state = jnp.zeros((1, HGRN_HEADS, HGRN_DK, HGRN_DV), F32)
    og_meta, s_meta = _hgrn_seq(h_sm, lb, nw, zero_state, n_seq=1, seq_len=SEQ_BLOCK, row_block0=meta_blk)
    og_s, st_s = _hgrn_step(h_sm, lb, nw, state_hgrn[0].astype(F32), n_seq=dec_b, t=dec_t, seq_per_block=32)
    og_p, st_p = _hgrn_seq(h_p, lb, nw, s_meta, n_seq=bsz, seq_len=seq, row_block0=0)
    x1_sm = _matmul_deepnorm(jnp.concatenate([og_s, og_meta], axis=0), w_out0, x_sm, g0, b0, tm=512)
    x1_p = _matmul_deepnorm(og_p, w_out0, x_p, g0, b0, tm=512)

    h1_sm = _matmul(x1_sm, w_in1, tm=n_sm, tn=512)
    h1_p = _matmul(x1_p, w_in1, tm=1024, tn=1536)
    og1_p = _attn_seq(h1_p, h1_sm, sinks.reshape(-1), n_seq=bsz, seq_len=seq, meta_row_block=meta_blk)
    ck = cache_swa_k[0].astype(F32).reshape(dec_b, WINDOW, SWA_KV_WIDTH)
    cv = cache_swa_v[0].astype(F32).reshape(dec_b, WINDOW, SWA_KV_WIDTH)
    og1_s, nk_s, nv_s = _attn_step(h1_sm, ck, cv, sink_cols, n_seq=dec_b, t=dec_t, seq_per_block=16)
    y_p = _matmul_deepnorm(og1_p, w_out1, x1_p, g1, b1, tm=512)
    y_s = _matmul_deepnorm(og1_s, w_out1, x1_sm, g1, b1, tm=512)

    kv_p = h1_p.reshape(bsz, seq, -1)[:, seq - WINDOW:, 2 * width:]
    cache_shape = (1, bsz, WINDOW, SWA_KV_HEADS, SWA_HEAD_DIM)
    return (y_p.reshape(bsz, seq, d).astype(out_dtype),
            y_s.reshape(dec_b, dec_t, d).astype(out_dtype),
            st_p[None].astype(out_dtype),
            st_s[None].astype(out_dtype),
            kv_p[..., :SWA_KV_WIDTH].reshape(cache_shape).astype(out_dtype),
            kv_p[..., SWA_KV_WIDTH:].reshape(cache_shape).astype(out_dtype),
            nk_s.reshape((1,) + cache_swa_k.shape[1:]).astype(out_dtype),
            nv_s.reshape((1,) + cache_swa_v.shape[1:]).astype(out_dtype))
```

```python
import functools

import numpy as np
import jax
import jax.numpy as jnp
from jax import lax
from jax.experimental import pallas as pl
from jax.experimental.pallas import tpu as pltpu

F32 = jnp.float32
BF16 = jnp.bfloat16

D_MODEL = 2048
N_META = 16
DEPTH = 2
HGRN_HEADS = 16
HGRN_DK = 128
HGRN_DV = 128
SWA_Q_HEADS = 32
SWA_KV_HEADS = 4
SWA_GROUP = 8
SWA_HEAD_DIM = 64
SWA_KV_WIDTH = SWA_KV_HEADS * SWA_HEAD_DIM
SWA_SCALE = SWA_HEAD_DIM ** -0.5
WINDOW = 128
DEEPNORM_ALPHA = (2.0 * DEPTH) ** 0.25
LN_EPS = 1e-5
RMS_EPS = 1e-6
LOG2E = 1.4426950408889634

CHUNK = 128
LEVELS = 7
MASK_ROWS = 16
SEQ_BLOCK = 128
STEP_GROUP = 8
SEQ_UNROLL = 4
HGRN_SEQ_HEADS = 4
VMEM_LIMIT = 56 * 1024 * 1024


def _dot(a, b):
    return jnp.dot(a, b, preferred_element_type=F32)


def _dot_nt(a, b):
    return lax.dot_general(a, b, (((1,), (1,)), ((), ())), preferred_element_type=F32)


def _dot_tn(a, b):
    return lax.dot_general(a, b, (((0,), (0,)), ((), ())), preferred_element_type=F32)


def _silu(x):
    return x * jax.nn.sigmoid(x)


def _mm_kernel(x_ref, w_ref, o_ref):
    o_ref[...] = _dot(x_ref[...].astype(BF16), w_ref[...]).astype(o_ref.dtype)


def _row_tile(n, want):
    if n <= want:
        return n
    return max(t for t in range(16, want + 1, 16) if n % t == 0)


def _matmul(x, w, *, tm, tn, out_dtype=F32):
    n, k = x.shape
    e = w.shape[1]
    tm = _row_tile(n, tm)
    assert n % tm == 0 and e % tn == 0
    return pl.pallas_call(
        _mm_kernel,
        grid=(n // tm, e // tn),
        in_specs=[pl.BlockSpec((tm, k), lambda i, j: (i, 0)),
                  pl.BlockSpec((k, tn), lambda i, j: (0, j))],
        out_specs=pl.BlockSpec((tm, tn), lambda i, j: (i, j)),
        out_shape=jax.ShapeDtypeStruct((n, e), out_dtype),
        compiler_params=pltpu.CompilerParams(
            dimension_semantics=("parallel", "parallel"), vmem_limit_bytes=VMEM_LIMIT),
        name="proj_in",
    )(x, w)


def _mm_ln_kernel(a_ref, w_ref, x_ref, g_ref, b_ref, o_ref):
    z = DEEPNORM_ALPHA * x_ref[...] + _dot(a_ref[...], w_ref[...])
    mu = jnp.mean(z, axis=-1, keepdims=True)
    zc = z - mu
    var = jnp.mean(zc * zc, axis=-1, keepdims=True)
    o_ref[...] = zc * lax.rsqrt(var + LN_EPS) * g_ref[...] + b_ref[...]


def _matmul_deepnorm(a, w, x, g, b, *, tm):
    n, k = a.shape
    d = w.shape[1]
    tm = _row_tile(n, tm)
    return pl.pallas_call(
        _mm_ln_kernel,
        grid=(n // tm,),
        in_specs=[pl.BlockSpec((tm, k), lambda i: (i, 0)),
                  pl.BlockSpec((k, d), lambda i: (0, 0)),
                  pl.BlockSpec((tm, d), lambda i: (i, 0)),
                  pl.BlockSpec((1, d), lambda i: (0, 0)),
                  pl.BlockSpec((1, d), lambda i: (0, 0))],
        out_specs=pl.BlockSpec((tm, d), lambda i: (i, 0)),
        out_shape=jax.ShapeDtypeStruct((n, d), F32),
        compiler_params=pltpu.CompilerParams(
            dimension_semantics=("parallel",), vmem_limit_bytes=VMEM_LIMIT),
        name="proj_out_deepnorm",
    )(a, w, x, g, b)


def _level_masks():
    t = np.arange(CHUNK)
    out = np.zeros((LEVELS + 1, CHUNK, CHUNK), np.float32)
    out[0] = t[:, None] == t[None, :]
    for l in range(LEVELS):
        h = 1 << l
        same = (t[:, None] >> (l + 1)) == (t[None, :] >> (l + 1))
        out[l + 1] = same & ((t[:, None] & h) != 0) & ((t[None, :] & h) == 0)
    return out


def _half_total(cum, level):
    rows, lanes = cum.shape
    blk = 1 << (level + 1)
    idx = (1 << level) - 1
    if blk >= 8:
        x = cum.reshape(rows // blk, blk, lanes)
        return jnp.broadcast_to(x[:, idx:idx + 1, :], x.shape).reshape(rows, lanes)
    x = cum.reshape(rows // 8, 8, lanes)
    sub = lax.broadcasted_iota(jnp.int32, x.shape, 1)
    out = None
    for start in range(8 - blk, -1, -blk):
        piece = jnp.broadcast_to(x[:, start + idx:start + idx + 1, :], x.shape)
        out = piece if out is None else jnp.where(sub < start + blk, piece, out)
    return out.reshape(rows, lanes)


def _gates(fx, lb):
    f = lb + (1.0 - lb) * jax.nn.sigmoid(fx)
    return f, jnp.log2(f), 1.0 - f


def _small_levels(logf, f, q, kk):
    tiles = (CHUNK // 8, 8, HGRN_DK)
    c, f3, q3, k3 = (x.reshape(tiles) for x in (logf, f, q, kk))
    sub = lax.broadcasted_iota(jnp.int32, (1, 8, HGRN_DK), 1)
    bcast = lambda x, r: jnp.broadcast_to(x[:, r:r + 1, :], tiles)
    up = (sub & 1) != 0
    zs = [jnp.where(up, q3 * f3, k3)]
    c = c + jnp.where(up, pltpu.roll(c, 1, 1), 0.0)
    for level, tot in ((1, lambda c: jnp.where(sub < 4, bcast(c, 1), bcast(c, 5))),
                       (2, lambda c: bcast(c, 3))):
        up = (sub & (1 << level)) != 0
        t = tot(c)
        zs.append(jnp.where(up, q3, k3) * jnp.exp2(jnp.where(up, c, t - c)))
        c = c + jnp.where(up, t, 0.0)
    return [z.reshape(CHUNK, HGRN_DK) for z in zs], c.reshape(CHUNK, HGRN_DK)


def _level_large(cum, q, kk, level):
    half = 1 << level
    args, bases, cums = [], [], []
    for r0 in range(0, CHUNK, 2 * half):
        lo, up = slice(r0, r0 + half), slice(r0 + half, r0 + 2 * half)
        tot = cum[r0 + half - 1:r0 + half, :]
        args += [tot - cum[lo], cum[up]]
        bases += [kk[lo], q[up]]
        cums += [cum[lo], cum[up] + tot]
    x = jnp.exp2(jnp.concatenate(args, axis=0))
    return jnp.concatenate(bases, axis=0) * x, jnp.concatenate(cums, axis=0)


def _split3(x):
    hi = x.astype(BF16)
    r = x - hi.astype(F32)
    mid = r.astype(BF16)
    lo = (r - mid.astype(F32)).astype(BF16)
    return hi, mid, lo


def _state_update(s_old, kd, v, blast, pad_rows):
    c = kd.shape[0]
    hi, mid, lo = _split3(jnp.exp2(blast))
    row = lax.broadcasted_iota(jnp.int32, (pad_rows, HGRN_DK), 0)
    dec = jnp.where(row == 0, hi.astype(F32),
                    jnp.where(row == 1, mid.astype(F32), jnp.where(row == 2, lo.astype(F32), 0.0)))
    lhs = jnp.concatenate([kd, dec], axis=0).astype(BF16)
    rhs = jnp.concatenate(
        [jnp.concatenate([v, jnp.zeros((c, HGRN_DV), F32)], axis=1),
         jnp.concatenate([jnp.zeros((pad_rows, HGRN_DV), F32), jnp.ones((pad_rows, HGRN_DV), F32)], axis=1)],
        axis=0).astype(BF16)
    both = _dot_tn(lhs, rhs)
    return both[:, HGRN_DV:] * s_old + both[:, :HGRN_DV]


def _norm_gate(o, g, nw):
    o = o * lax.rsqrt(jnp.mean(o * o, axis=-1, keepdims=True) + RMS_EPS) * nw
    return o * _silu(g)


def _hgrn_seq_kernel(q_ref, fx_ref, i_ref, g_ref, lb_ref, nw_ref, s0_ref, m_ref,
                     og_ref, sout_ref, s_scr, *, n_chunks):
    heads = s_scr.shape[0]
    s_scr[...] = s0_ref[0]

    def chunk(c, carry):
        rows = pl.ds(pl.multiple_of(c * CHUNK, CHUNK), CHUNK)
        hcols = [slice(hd * HGRN_DK, (hd + 1) * HGRN_DK) for hd in range(heads)]
        q = [q_ref[rows, cs] for cs in hcols]
        v = [i_ref[rows, cs] for cs in hcols]
        gates = [_gates(fx_ref[rows, cs], lb_ref[:, cs]) for cs in hcols]
        kk = [gt[2] for gt in gates]
        small = [_small_levels(gt[1], gt[0], q[hd], kk[hd]) for hd, gt in enumerate(gates)]
        zs = [sm[0] for sm in small]
        cum = [sm[1] for sm in small]
        for level in range(3, LEVELS):
            for hd in range(heads):
                z, cum[hd] = _level_large(cum[hd], q[hd], kk[hd], level)
                zs[hd].append(z)
        nblk = CHUNK // MASK_ROWS
        a = [[None] * nblk for _ in range(heads)]
        for level in range(LEVELS):
            step = 1 << (level - 4) if level >= 4 else 0
            blocks = [i for i in range(nblk) if level < 4 or (i & step)]
            for hd in range(heads):
                z = zs[hd][level].astype(BF16)
                lhs = z if level < 4 else jnp.concatenate(
                    [z[i * MASK_ROWS:(i + 1) * MASK_ROWS] for i in blocks], axis=0)
                term = _dot_nt(lhs, z)
                for n, i in enumerate(blocks):
                    t = (term[n * MASK_ROWS:(n + 1) * MASK_ROWS]
                         * m_ref[level + 1, i * MASK_ROWS:(i + 1) * MASK_ROWS, :])
                    a[hd][i] = t if a[hd][i] is None else a[hd][i] + t
        for hd in range(heads):
            blast = cum[hd][CHUNK - 1:CHUNK, :]
            qd = (q[hd] * jnp.exp2(cum[hd])).astype(BF16)
            kd = kk[hd] * jnp.exp2(blast - cum[hd])
            s_old = s_scr[hd]
            amat = jnp.concatenate(a[hd], axis=0).astype(BF16)
            o = (_dot(qd, s_old.astype(BF16)) + _dot(amat, v[hd].astype(BF16))
                 + jnp.sum(q[hd] * kk[hd], axis=-1, keepdims=True) * v[hd])
            s_scr[hd] = _state_update(s_old, kd, v[hd], blast, 16)
            og_ref[rows, hcols[hd]] = _norm_gate(
                o, g_ref[rows, hcols[hd]], nw_ref[:, hcols[hd]]).astype(og_ref.dtype)
        return carry

    lax.fori_loop(0, n_chunks, chunk, 0)
    sout_ref[0] = s_scr[...]


def _hgrn_seq(h, lb, nw, s0, *, n_seq, seq_len, row_block0):
    assert seq_len % CHUNK == 0
    hb = HGRN_SEQ_HEADS
    width = hb * HGRN_DK
    groups = HGRN_HEADS // hb
    s0_batched = s0.shape[0] != 1
    col = lambda part: (lambda b, hg: (row_block0 + b, part * groups + hg))
    masks = jnp.asarray(_level_masks())
    return pl.pallas_call(
        functools.partial(_hgrn_seq_kernel, n_chunks=seq_len // CHUNK),
        grid=(n_seq, groups),
        in_specs=[pl.BlockSpec((seq_len, width), col(0)),
                  pl.BlockSpec((seq_len, width), col(1)),
                  pl.BlockSpec((seq_len, width), col(2)),
                  pl.BlockSpec((seq_len, width), col(3)),
                  pl.BlockSpec((1, width), lambda b, hg: (0, hg)),
                  pl.BlockSpec((1, width), lambda b, hg: (0, hg)),
                  pl.BlockSpec((1, hb, HGRN_DK, HGRN_DV),
                               (lambda b, hg: (b, hg, 0, 0)) if s0_batched else (lambda b, hg: (0, hg, 0, 0))),
                  pl.BlockSpec((LEVELS + 1, CHUNK, CHUNK), lambda b, hg: (0, 0, 0))],
        out_specs=[pl.BlockSpec((seq_len, width), lambda b, hg: (b, hg)),
                   pl.BlockSpec((1, hb, HGRN_DK, HGRN_DV), lambda b, hg: (b, hg, 0, 0))],
        out_shape=[jax.ShapeDtypeStruct((n_seq * seq_len, HGRN_HEADS * HGRN_DV), BF16),
                   jax.ShapeDtypeStruct((n_seq, HGRN_HEADS, HGRN_DK, HGRN_DV), F32)],
        scratch_shapes=[pltpu.VMEM((hb, HGRN_DK, HGRN_DV), F32)],
        compiler_params=pltpu.CompilerParams(
            dimension_semantics=("parallel", "parallel"), vmem_limit_bytes=VMEM_LIMIT),
        name="hgrn_seq",
    )(h, h, h, h, lb, nw, s0, masks)


def _hgrn_step_kernel(q_ref, fx_ref, i_ref, g_ref, lb_ref, nw_ref, s_ref, og_ref, sout_ref, *, n_seq, t):
    lb = lb_ref[...]
    nw = nw_ref[...]
    tile = (STEP_GROUP, t, HGRN_DK)
    sub = lax.broadcasted_iota(jnp.int32, (1, t, HGRN_DK), 1)

    def group(i, carry):
        rows = pl.ds(pl.multiple_of(i * (STEP_GROUP * t), STEP_GROUP * t), STEP_GROUP * t)
        q = q_ref[rows, :].reshape(tile)
        v = i_ref[rows, :].reshape(tile)
        _, logf, kk = _gates(fx_ref[rows, :].reshape(tile), lb)
        cum = logf
        shift = 1
        while shift < t:
            cum = cum + jnp.where(sub >= shift, pltpu.roll(cum, shift, 1), 0.0)
            shift *= 2
        intra = (jnp.sum(q * kk, axis=-1, keepdims=True)) * v
        for d in range(1, t):
            valid = sub >= d
            x = jnp.exp2(jnp.where(valid, cum - pltpu.roll(cum, d, 1), 0.0))
            w = jnp.sum(jnp.where(valid, q * x * pltpu.roll(kk, d, 1), 0.0), axis=-1, keepdims=True)
            intra = intra + w * pltpu.roll(v, d, 1)
        blast = cum[:, t - 1:t, :]
        qd = q * jnp.exp2(cum)
        kd = kk * jnp.exp2(blast - cum)
        inter = []
        for n in range(STEP_GROUP):
            b = i * STEP_GROUP + n
            s_old = s_ref[b, 0]
            inter.append(_dot(qd[n].astype(BF16), s_old.astype(BF16)))
            sout_ref[b, 0] = _state_update(s_old, kd[n], v[n], blast[n], 8)
        o = jnp.stack(inter, axis=0) + intra
        og = _norm_gate(o, g_ref[rows, :].reshape(tile), nw)
        og_ref[rows, :] = og.reshape(STEP_GROUP * t, HGRN_DK).astype(og_ref.dtype)
        return carry

    lax.fori_loop(0, n_seq // STEP_GROUP, group, 0)


def _hgrn_step(h, lb, nw, s0, *, n_seq, t, seq_per_block):
    seq_per_block = min(seq_per_block, n_seq)
    assert n_seq % seq_per_block == 0 and seq_per_block % STEP_GROUP == 0 and t == 8
    rows = seq_per_block * t
    col = lambda part: (lambda i, hd: (i, part * HGRN_HEADS + hd))
    return pl.pallas_call(
        functools.partial(_hgrn_step_kernel, n_seq=seq_per_block, t=t),
        grid=(n_seq // seq_per_block, HGRN_HEADS),
        in_specs=[pl.BlockSpec((rows, HGRN_DK), col(0)),
                  pl.BlockSpec((rows, HGRN_DK), col(1)),
                  pl.BlockSpec((rows, HGRN_DV), col(2)),
                  pl.BlockSpec((rows, HGRN_DV), col(3)),
                  pl.BlockSpec((1, HGRN_DK), lambda i, hd: (0, hd)),
                  pl.BlockSpec((1, HGRN_DV), lambda i, hd: (0, hd)),
                  pl.BlockSpec((seq_per_block, 1, HGRN_DK, HGRN_DV), lambda i, hd: (i, hd, 0, 0))],
        out_specs=[pl.BlockSpec((rows, HGRN_DV), lambda i, hd: (i, hd)),
                   pl.BlockSpec((seq_per_block, 1, HGRN_DK, HGRN_DV), lambda i, hd: (i, hd, 0, 0))],
        out_shape=[jax.ShapeDtypeStruct((n_seq * t, HGRN_HEADS * HGRN_DV), BF16),
                   jax.ShapeDtypeStruct((n_seq, HGRN_HEADS, HGRN_DK, HGRN_DV), F32)],
        compiler_params=pltpu.CompilerParams(
            dimension_semantics=("parallel", "parallel"), vmem_limit_bytes=VMEM_LIMIT),
        name="hgrn_step",
    )(h, h, h, h, lb, nw, s0)


def _attn_seq_kernel(q_ref, g_ref, kc_ref, vc_ref, kp_ref, vp_ref, meta_ref, sink_ref, og_ref):
    first = pl.program_id(1) == 0
    kprev = jnp.where(first, meta_ref[:, :SWA_KV_WIDTH], kp_ref[...])
    vprev = jnp.where(first, meta_ref[:, SWA_KV_WIDTH:], vp_ref[...])
    kband = (jnp.concatenate([kprev, kc_ref[...]], axis=0) * (SWA_SCALE * LOG2E)).astype(BF16)
    vband = jnp.concatenate([vprev, vc_ref[...]], axis=0).astype(BF16)
    kj = lax.broadcasted_iota(jnp.int32, (2 * SEQ_BLOCK, SEQ_BLOCK), 0)
    qi = lax.broadcasted_iota(jnp.int32, (2 * SEQ_BLOCK, SEQ_BLOCK), 1)
    dist = SEQ_BLOCK + qi - kj
    valid = (dist >= 0) & (dist < WINDOW) & (jnp.logical_not(first) | (kj >= SEQ_BLOCK - N_META))
    madd = jnp.where(valid, 0.0, -jnp.inf)
    zeros = jnp.zeros((2 * SEQ_BLOCK, SWA_HEAD_DIM), BF16)
    pairs = SWA_GROUP // 2
    pw = 2 * SWA_HEAD_DIM

    def block_diag(x):
        return jnp.concatenate([jnp.concatenate([x, zeros], axis=1),
                                jnp.concatenate([zeros, x], axis=1)], axis=0)

    sts = []
    for kvh in range(SWA_KV_HEADS):
        ks = slice(kvh * SWA_HEAD_DIM, (kvh + 1) * SWA_HEAD_DIM)
        col0 = kvh * SWA_GROUP * SWA_HEAD_DIM
        xq = jnp.concatenate([q_ref[:, col0 + p * pw:col0 + (p + 1) * pw].astype(BF16)
                              for p in range(pairs)], axis=0)
        sts.append(_dot_nt(block_diag(kband[:, ks]), xq))
    for kvh in range(SWA_KV_HEADS):
        ks = slice(kvh * SWA_HEAD_DIM, (kvh + 1) * SWA_HEAD_DIM)
        col0 = kvh * SWA_GROUP * SWA_HEAD_DIM
        v2 = block_diag(vband[:, ks])
        pns = []
        for pr in range(pairs):
            head = kvh * SWA_GROUP + 2 * pr
            s2 = (sts[kvh][:, pr * SEQ_BLOCK:(pr + 1) * SEQ_BLOCK].reshape(2, 2 * SEQ_BLOCK, SEQ_BLOCK)
                  + madd[None])
            sink = jnp.concatenate([jnp.full((1, 1, SEQ_BLOCK), sink_ref[head] * LOG2E, F32),
                                    jnp.full((1, 1, SEQ_BLOCK), sink_ref[head + 1] * LOG2E, F32)], axis=0)
            m = jnp.maximum(jnp.max(s2, axis=1, keepdims=True), sink)
            p = jnp.exp2(s2 - m)
            den = jnp.sum(p, axis=1, keepdims=True) + jnp.exp2(sink - m)
            pns.append((p * (1.0 / den)).astype(BF16).reshape(4 * SEQ_BLOCK, SEQ_BLOCK))
        for pr in range(pairs):
            o = _dot_tn(pns[pr], v2)
            cs = slice(col0 + pr * pw, col0 + (pr + 1) * pw)
            og_ref[:, cs] = (o * _silu(g_ref[:, cs])).astype(og_ref.dtype)


def _attn_seq(h1, h1_meta, sinks, *, n_seq, seq_len, meta_row_block):
    nb = seq_len // SEQ_BLOCK
    width = SWA_Q_HEADS * SWA_HEAD_DIM
    kcol = 2 * width // SWA_KV_WIDTH
    return pl.pallas_call(
        _attn_seq_kernel,
        grid=(n_seq, nb),
        in_specs=[pl.BlockSpec((SEQ_BLOCK, width), lambda b, j: (b * nb + j, 0)),
                  pl.BlockSpec((SEQ_BLOCK, width), lambda b, j: (b * nb + j, 1)),
                  pl.BlockSpec((SEQ_BLOCK, SWA_KV_WIDTH), lambda b, j: (b * nb + j, kcol)),
                  pl.BlockSpec((SEQ_BLOCK, SWA_KV_WIDTH), lambda b, j: (b * nb + j, kcol + 1)),
                  pl.BlockSpec((SEQ_BLOCK, SWA_KV_WIDTH), lambda b, j: (b * nb + jnp.maximum(j - 1, 0), kcol)),
                  pl.BlockSpec((SEQ_BLOCK, SWA_KV_WIDTH), lambda b, j: (b * nb + jnp.maximum(j - 1, 0), kcol + 1)),
                  pl.BlockSpec((SEQ_BLOCK, 2 * SWA_KV_WIDTH), lambda b, j: (meta_row_block, kcol // 2)),
                  pl.BlockSpec(memory_space=pltpu.SMEM)],
        out_specs=pl.BlockSpec((SEQ_BLOCK, width), lambda b, j: (b * nb + j, 0)),
        out_shape=jax.ShapeDtypeStruct((n_seq * seq_len, width), BF16),
        compiler_params=pltpu.CompilerParams(
            dimension_semantics=("parallel", "arbitrary"), vmem_limit_bytes=VMEM_LIMIT),
        name="attn_seq",
    )(h1, h1, h1, h1, h1, h1, h1_meta, sinks)


def _attn_step_kernel(q_ref, g_ref, kn_ref, vn_ref, ck_ref, cv_ref, sink_ref, og_ref, nk_ref, nv_ref, *, n_seq, t):
    keys = WINDOW + t
    hd = SWA_HEAD_DIM
    tiles = SWA_Q_HEADS // 2
    kj = lax.broadcasted_iota(jnp.int32, (keys, SWA_Q_HEADS * t), 0)
    qt = lax.broadcasted_iota(jnp.int32, (keys, SWA_Q_HEADS * t), 1) % t
    madd = jnp.where((kj >= qt + 1) & (kj <= WINDOW + qt), 0.0, -jnp.inf)
    low = lax.broadcasted_iota(jnp.int32, (t, 2 * hd), 1) < hd
    zero_tile = jnp.zeros((t, 2 * hd), F32)
    sink = sink_ref[...] * LOG2E

    def group(i, carry):
        seqs = [i * SEQ_UNROLL + n for n in range(SEQ_UNROLL)]
        rows = [pl.ds(pl.multiple_of(b * t, t), t) for b in seqs]
        st, vall = [], []
        for b, rw in zip(seqs, rows):
            kc, vc = ck_ref[b], cv_ref[b]
            kn, vn = kn_ref[rw, :], vn_ref[rw, :]
            nk_ref[b, 0:WINDOW - t, :] = kc[t:, :]
            nk_ref[b, WINDOW - t:WINDOW, :] = kn
            nv_ref[b, 0:WINDOW - t, :] = vc[t:, :]
            nv_ref[b, WINDOW - t:WINDOW, :] = vn
            kall = jnp.concatenate([kc, kn], axis=0).astype(BF16)
            vall.append(jnp.concatenate([vc, vn], axis=0).astype(BF16))
            q = q_ref[rw, :] * (SWA_SCALE * LOG2E)
            qtile = [q[:, j * 2 * hd:(j + 1) * 2 * hd] for j in range(tiles)]
            qswap = [pltpu.roll(x, hd, 1) for x in qtile]
            groups = []
            for kvh in range(SWA_KV_HEADS):
                for gq in range(SWA_GROUP):
                    j = (kvh * SWA_GROUP + gq) // 2
                    src = qtile[j] if gq % 2 == kvh % 2 else qswap[j]
                    half = jnp.where(low, src, 0.0) if kvh % 2 == 0 else jnp.where(low, 0.0, src)
                    groups.append(jnp.concatenate(
                        [half if c == kvh // 2 else zero_tile for c in range(SWA_KV_HEADS // 2)], axis=1))
            qbd = jnp.concatenate(groups, axis=0).astype(BF16)
            st.append(_dot_nt(kall, qbd))
        pn = []
        for s in st:
            s = s + madd
            m = jnp.maximum(jnp.max(s, axis=0, keepdims=True), sink)
            p = jnp.exp2(s - m)
            den = jnp.sum(p, axis=0, keepdims=True) + jnp.exp2(sink - m)
            pn.append((p * (1.0 / den)).astype(BF16))
        of = [_dot_tn(p, vl) for p, vl in zip(pn, vall)]
        for o, rw in zip(of, rows):
            g = g_ref[rw, :]
            out = []
            for j in range(tiles):
                kvh = (2 * j) // SWA_GROUP
                ct = slice((kvh // 2) * 2 * hd, (kvh // 2 + 1) * 2 * hd)
                ra = o[(2 * j) * t:(2 * j + 1) * t, ct]
                rb = o[(2 * j + 1) * t:(2 * j + 2) * t, ct]
                if kvh % 2 == 0:
                    out.append(jnp.where(low, ra, pltpu.roll(rb, hd, 1)))
                else:
                    out.append(jnp.where(low, pltpu.roll(ra, hd, 1), rb))
            og_ref[rw, :] = (jnp.concatenate(out, axis=1) * _silu(g)).astype(og_ref.dtype)
        return carry

    lax.fori_loop(0, n_seq // SEQ_UNROLL, group, 0)


def _attn_step(h1, cache_k, cache_v, sink_cols, *, n_seq, t, seq_per_block):
    assert n_seq % seq_per_block == 0
    rows = seq_per_block * t
    width = SWA_Q_HEADS * SWA_HEAD_DIM
    kcol = 2 * width // SWA_KV_WIDTH
    cache_spec = pl.BlockSpec((seq_per_block, WINDOW, SWA_KV_WIDTH), lambda i: (i, 0, 0))
    return pl.pallas_call(
        functools.partial(_attn_step_kernel, n_seq=seq_per_block, t=t),
        grid=(n_seq // seq_per_block,),
        in_specs=[pl.BlockSpec((rows, width), lambda i: (i, 0)),
                  pl.BlockSpec((rows, width), lambda i: (i, 1)),
                  pl.BlockSpec((rows, SWA_KV_WIDTH), lambda i: (i, kcol)),
                  pl.BlockSpec((rows, SWA_KV_WIDTH), lambda i: (i, kcol + 1)),
                  cache_spec, cache_spec,
                  pl.BlockSpec((1, SWA_Q_HEADS * t), lambda i: (0, 0))],
        out_specs=[pl.BlockSpec((rows, width), lambda i: (i, 0)), cache_spec, cache_spec],
        out_shape=[jax.ShapeDtypeStruct((n_seq * t, width), BF16),
                   jax.ShapeDtypeStruct(cache_k.shape, F32),
                   jax.ShapeDtypeStruct(cache_v.shape, F32)],
        compiler_params=pltpu.CompilerParams(
            dimension_semantics=("parallel",), vmem_limit_bytes=VMEM_LIMIT),
        name="attn_step",
    )(h1, h1, h1, h1, cache_k, cache_v, sink_cols)


def kernel(x_prompt, x_sample, state_hgrn, cache_swa_k, cache_swa_v, meta_tokens,
           hgrn_w_in, hgrn_lb_logits, hgrn_norm_w, hgrn_w_out,
           swa_w_in, swa_sinks, swa_w_out, ln_g, ln_b):
    out_dtype = x_prompt.dtype
    bsz, seq, d = x_prompt.shape
    dec_b, dec_t, _ = x_sample.shape
    n_p = bsz * seq
    n_s = dec_b * dec_t
    width = SWA_Q_HEADS * SWA_HEAD_DIM

    w_in0 = hgrn_w_in[0].astype(BF16)
    w_out0 = hgrn_w_out[0].astype(BF16)
    wi = swa_w_in[0]
    w_in1 = jnp.concatenate([wi[:, :width], wi[:, width + 2 * SWA_KV_WIDTH:],
                             wi[:, width:width + 2 * SWA_KV_WIDTH]], axis=1).astype(BF16)
    w_out1 = swa_w_out[0].astype(BF16)
    lb = jnp.cumsum(jax.nn.softmax(hgrn_lb_logits.astype(F32), axis=0), axis=0)[0:1]
    nw = hgrn_norm_w[0].astype(F32).reshape(1, -1)
    sinks = swa_sinks[0].astype(F32).reshape(1, SWA_Q_HEADS)
    sink_cols = jnp.repeat(sinks, dec_t, axis=1)
    g0, b0 = ln_g[0:1].astype(F32), ln_b[0:1].astype(F32)
    g1, b1 = ln_g[1:2].astype(F32), ln_b[1:2].astype(F32)

    x_p = x_prompt.astype(F32).reshape(n_p, d)
    meta_block = jnp.concatenate(
        [jnp.zeros((SEQ_BLOCK - N_META, d), F32), meta_tokens.astype(F32)], axis=0)
    x_sm = jnp.concatenate([x_sample.astype(F32).reshape(n_s, d), meta_block], axis=0)
    n_sm = n_s + SEQ_BLOCK
    meta_blk = n_s // SEQ_BLOCK

    h_sm = _matmul(x_sm, w_in0, tm=n_sm, tn=512)
    h_p = _matmul(x_p, w_in0, tm=1024, tn=1024)
    zero_state = jnp.zeros((1, HGRN_HEADS, HGRN_DK, HGRN_DV), F32)
    og_meta, s_meta = _hgrn_seq(h_sm, lb, nw, zero_state, n_seq=1, seq_len=SEQ_BLOCK, row_block0=meta_blk)
    og_s, st_s = _hgrn_step(h_sm, lb, nw, state_hgrn[0].astype(F32), n_seq=dec_b, t=dec_t, seq_per_block=64)
    og_p, st_p = _hgrn_seq(h_p, lb, nw, s_meta, n_seq=bsz, seq_len=seq, row_block0=0)
    x1_sm = _matmul_deepnorm(jnp.concatenate([og_s, og_meta], axis=0), w_out0, x_sm, g0, b0, tm=512)
    x1_p = _matmul_deepnorm(og_p, w_out0, x_p, g0, b0, tm=512)

    h1_sm = _matmul(x1_sm, w_in1, tm=n_sm, tn=512)
    h1_p = _matmul(x1_p, w_in1, tm=1024, tn=1536)
    og1_p = _attn_seq(h1_p, h1_sm, sinks.reshape(-1), n_seq=bsz, seq_len=seq, meta_row_block=meta_blk)
    ck = cache_swa_k[0].astype(F32).reshape(dec_b, WINDOW, SWA_KV_WIDTH)
    cv = cache_swa_v[0].astype(F32).reshape(dec_b, WINDOW, SWA_KV_WIDTH)
    og1_s, nk_s, nv_s = _attn_step(h1_sm, ck, cv, sink_cols, n_seq=dec_b, t=dec_t, seq_per_block=16)
    y_p = _matmul_deepnorm(og1_p, w_out1, x1_p, g1, b1, tm=512)
    y_s = _matmul_deepnorm(og1_s, w_out1, x1_sm, g1, b1, tm=512)

    kv_p = h1_p.reshape(bsz, seq, -1)[:, seq - WINDOW:, 2 * width:]
    cache_shape = (1, bsz, WINDOW, SWA_KV_HEADS, SWA_HEAD_DIM)
    return (y_p.reshape(bsz, seq, d).astype(out_dtype),
            y_s.reshape(dec_b, dec_t, d).astype(out_dtype),
            st_p[None].astype(out_dtype),
            st_s[None].astype(out_dtype),
            kv_p[..., :SWA_KV_WIDTH].reshape(cache_shape).astype(out_dtype),
            kv_p[..., SWA_KV_WIDTH:].reshape(cache_shape).astype(out_dtype),
            nk_s.reshape((1,) + cache_swa_k.shape[1:]).astype(out_dtype),
            nv_s.reshape((1,) + cache_swa_v.shape[1:]).astype(out_dtype))
```

```python
import functools

import numpy as np
import jax
import jax.numpy as jnp
from jax import lax
from jax.experimental import pallas as pl
from jax.experimental.pallas import tpu as pltpu

F32 = jnp.float32
BF16 = jnp.bfloat16

D_MODEL = 2048
N_META = 16
DEPTH = 2
HGRN_HEADS = 16
HGRN_DK = 128
HGRN_DV = 128
SWA_Q_HEADS = 32
SWA_KV_HEADS = 4
SWA_GROUP = 8
SWA_HEAD_DIM = 64
SWA_KV_WIDTH = SWA_KV_HEADS * SWA_HEAD_DIM
SWA_SCALE = SWA_HEAD_DIM ** -0.5
WINDOW = 128
DEEPNORM_ALPHA = (2.0 * DEPTH) ** 0.25
LN_EPS = 1e-5
RMS_EPS = 1e-6
LOG2E = 1.4426950408889634

CHUNK = 128
LEVELS = 7
MAX_FACTORED_LOG2_DECAY = 100.0
MASK_ROWS = 16
SEQ_BLOCK = 128
STEP_GROUP = 8
SEQ_UNROLL = 4
HGRN_SEQ_HEADS = 4
VMEM_LIMIT = 56 * 1024 * 1024


def _dot(a, b):
    return jnp.dot(a, b, preferred_element_type=F32)


def _dot_nt(a, b):
    return lax.dot_general(a, b, (((1,), (1,)), ((), ())), preferred_element_type=F32)


def _dot_tn(a, b):
    return lax.dot_general(a, b, (((0,), (0,)), ((), ())), preferred_element_type=F32)


def _silu(x):
    return x * jax.nn.sigmoid(x)


def _mm_kernel(x_ref, w_ref, o_ref):
    o_ref[...] = _dot(x_ref[...].astype(BF16), w_ref[...]).astype(o_ref.dtype)


def _row_tile(n, want):
    if n <= want:
        return n
    return max(t for t in range(16, want + 1, 16) if n % t == 0)


def _matmul(x, w, *, tm, tn, out_dtype=F32):
    n, k = x.shape
    e = w.shape[1]
    tm = _row_tile(n, tm)
    assert n % tm == 0 and e % tn == 0
    return pl.pallas_call(
        _mm_kernel,
        grid=(n // tm, e // tn),
        in_specs=[pl.BlockSpec((tm, k), lambda i, j: (i, 0)),
                  pl.BlockSpec((k, tn), lambda i, j: (0, j))],
        out_specs=pl.BlockSpec((tm, tn), lambda i, j: (i, j)),
        out_shape=jax.ShapeDtypeStruct((n, e), out_dtype),
        compiler_params=pltpu.CompilerParams(
            dimension_semantics=("parallel", "parallel"), vmem_limit_bytes=VMEM_LIMIT),
        name="proj_in",
    )(x, w)


def _mm_ln_kernel(a_ref, w_ref, x_ref, g_ref, b_ref, o_ref):
    z = DEEPNORM_ALPHA * x_ref[...] + _dot(a_ref[...], w_ref[...])
    mu = jnp.mean(z, axis=-1, keepdims=True)
    zc = z - mu
    var = jnp.mean(zc * zc, axis=-1, keepdims=True)
    o_ref[...] = zc * lax.rsqrt(var + LN_EPS) * g_ref[...] + b_ref[...]


def _matmul_deepnorm(a, w, x, g, b, *, tm):
    n, k = a.shape
    d = w.shape[1]
    tm = _row_tile(n, tm)
    return pl.pallas_call(
        _mm_ln_kernel,
        grid=(n // tm,),
        in_specs=[pl.BlockSpec((tm, k), lambda i: (i, 0)),
                  pl.BlockSpec((k, d), lambda i: (0, 0)),
                  pl.BlockSpec((tm, d), lambda i: (i, 0)),
                  pl.BlockSpec((1, d), lambda i: (0, 0)),
                  pl.BlockSpec((1, d), lambda i: (0, 0))],
        out_specs=pl.BlockSpec((tm, d), lambda i: (i, 0)),
        out_shape=jax.ShapeDtypeStruct((n, d), F32),
        compiler_params=pltpu.CompilerParams(
            dimension_semantics=("parallel",), vmem_limit_bytes=VMEM_LIMIT),
        name="proj_out_deepnorm",
    )(a, w, x, g, b)


def _level_masks():
    t = np.arange(CHUNK)
    out = np.zeros((LEVELS + 1, CHUNK, CHUNK), np.float32)
    out[0] = t[:, None] > t[None, :]
    for l in range(LEVELS):
        h = 1 << l
        same = (t[:, None] >> (l + 1)) == (t[None, :] >> (l + 1))
        out[l + 1] = same & ((t[:, None] & h) != 0) & ((t[None, :] & h) == 0)
    return out


def _half_total(cum, level):
    rows, lanes = cum.shape
    blk = 1 << (level + 1)
    idx = (1 << level) - 1
    if blk >= 8:
        x = cum.reshape(rows // blk, blk, lanes)
        return jnp.broadcast_to(x[:, idx:idx + 1, :], x.shape).reshape(rows, lanes)
    x = cum.reshape(rows // 8, 8, lanes)
    sub = lax.broadcasted_iota(jnp.int32, x.shape, 1)
    out = None
    for start in range(8 - blk, -1, -blk):
        piece = jnp.broadcast_to(x[:, start + idx:start + idx + 1, :], x.shape)
        out = piece if out is None else jnp.where(sub < start + blk, piece, out)
    return out.reshape(rows, lanes)


def _gates(fx, lb):
    f = lb + (1.0 - lb) * jax.nn.sigmoid(fx)
    return f, jnp.log2(f), 1.0 - f


def _prefix_scan(logf):
    tiles = (CHUNK // 8, 8, HGRN_DK)
    c = logf.reshape(tiles)
    sub = lax.broadcasted_iota(jnp.int32, (1, 8, HGRN_DK), 1)
    bcast = lambda x, r: jnp.broadcast_to(x[:, r:r + 1, :], tiles)
    c = c + jnp.where((sub & 1) != 0, pltpu.roll(c, 1, 1), 0.0)
    c = c + jnp.where((sub & 2) != 0, jnp.where(sub < 4, bcast(c, 1), bcast(c, 5)), 0.0)
    c = c + jnp.where((sub & 4) != 0, bcast(c, 3), 0.0)
    c = c.reshape(CHUNK, HGRN_DK)
    for level in range(3, LEVELS):
        half = 1 << level
        pieces = []
        for r0 in range(0, CHUNK, 2 * half):
            pieces += [c[r0:r0 + half], c[r0 + half:r0 + 2 * half] + c[r0 + half - 1:r0 + half, :]]
        c = jnp.concatenate(pieces, axis=0)
    return c


def _small_levels(logf, f, q, kk):
    tiles = (CHUNK // 8, 8, HGRN_DK)
    c, f3, q3, k3 = (x.reshape(tiles) for x in (logf, f, q, kk))
    sub = lax.broadcasted_iota(jnp.int32, (1, 8, HGRN_DK), 1)
    bcast = lambda x, r: jnp.broadcast_to(x[:, r:r + 1, :], tiles)
    up = (sub & 1) != 0
    zs = [jnp.where(up, q3 * f3, k3)]
    c = c + jnp.where(up, pltpu.roll(c, 1, 1), 0.0)
    for level, tot in ((1, lambda c: jnp.where(sub < 4, bcast(c, 1), bcast(c, 5))),
                       (2, lambda c: bcast(c, 3))):
        up = (sub & (1 << level)) != 0
        t = tot(c)
        zs.append(jnp.where(up, q3, k3) * jnp.exp2(jnp.where(up, c, t - c)))
        c = c + jnp.where(up, t, 0.0)
    return [z.reshape(CHUNK, HGRN_DK) for z in zs], c.reshape(CHUNK, HGRN_DK)


def _level_large(cum, q, kk, level):
    half = 1 << level
    args, bases, cums = [], [], []
    for r0 in range(0, CHUNK, 2 * half):
        lo, up = slice(r0, r0 + half), slice(r0 + half, r0 + 2 * half)
        tot = cum[r0 + half - 1:r0 + half, :]
        args += [tot - cum[lo], cum[up]]
        bases += [kk[lo], q[up]]
        cums += [cum[lo], cum[up] + tot]
    x = jnp.exp2(jnp.concatenate(args, axis=0))
    return jnp.concatenate(bases, axis=0) * x, jnp.concatenate(cums, axis=0)


def _split3(x):
    hi = x.astype(BF16)
    r = x - hi.astype(F32)
    mid = r.astype(BF16)
    lo = (r - mid.astype(F32)).astype(BF16)
    return hi, mid, lo


def _state_update(s_old, kd, v, blast, pad_rows):
    c = kd.shape[0]
    hi, mid, lo = _split3(jnp.exp2(blast))
    row = lax.broadcasted_iota(jnp.int32, (pad_rows, HGRN_DK), 0)
    dec = jnp.where(row == 0, hi.astype(F32),
                    jnp.where(row == 1, mid.astype(F32), jnp.where(row == 2, lo.astype(F32), 0.0)))
    lhs = jnp.concatenate([kd, dec], axis=0).astype(BF16)
    rhs = jnp.concatenate(
        [jnp.concatenate([v, jnp.zeros((c, HGRN_DV), F32)], axis=1),
         jnp.concatenate([jnp.zeros((pad_rows, HGRN_DV), F32), jnp.ones((pad_rows, HGRN_DV), F32)], axis=1)],
        axis=0).astype(BF16)
    both = _dot_tn(lhs, rhs)
    return both[:, HGRN_DV:] * s_old + both[:, :HGRN_DV]


def _norm_gate(o, g, nw):
    o = o * lax.rsqrt(jnp.mean(o * o, axis=-1, keepdims=True) + RMS_EPS) * nw
    return o * _silu(g)


def _hgrn_seq_kernel(q_ref, fx_ref, i_ref, g_ref, lb_ref, nw_ref, s0_ref, m_ref,
                     og_ref, sout_ref, s_scr, o_scr, *, n_chunks):
    heads = s_scr.shape[0]
    s_scr[...] = s0_ref[0]

    def chunk(c, carry):
        rows = pl.ds(pl.multiple_of(c * CHUNK, CHUNK), CHUNK)
        hcols = [slice(hd * HGRN_DK, (hd + 1) * HGRN_DK) for hd in range(heads)]
        q = [q_ref[rows, cs] for cs in hcols]
        v = [i_ref[rows, cs] for cs in hcols]
        gates = [_gates(fx_ref[rows, cs], lb_ref[:, cs]) for cs in hcols]
        kk = [gt[2] for gt in gates]
        cum = [_prefix_scan(gt[1]) for gt in gates]
        mid = CHUNK // 2 - 1
        spread = None
        for b in cum:
            s_hd = jnp.maximum(b[0:1, :] - b[mid:mid + 1, :], b[mid:mid + 1, :] - b[CHUNK - 1:CHUNK, :])
            spread = s_hd if spread is None else jnp.maximum(spread, s_hd)
        factorable = jnp.max(spread) <= MAX_FACTORED_LOG2_DECAY

        vb = [x.astype(BF16) for x in v]
        for hd in range(heads):
            d = cum[hd] - cum[hd][mid:mid + 1, :]
            qe = (q[hd] * jnp.exp2(d)).astype(BF16)
            ke = (kk[hd] * jnp.exp2(-d)).astype(BF16)
            amat = (_dot_nt(qe, ke) * m_ref[0]).astype(BF16)
            blast = cum[hd][CHUNK - 1:CHUNK, :]
            qd = (q[hd] * jnp.exp2(cum[hd])).astype(BF16)
            kd = kk[hd] * jnp.exp2(blast - cum[hd])
            s_old = s_scr[hd]
            base = _dot(qd, s_old.astype(BF16)) + jnp.sum(q[hd] * kk[hd], axis=-1, keepdims=True) * v[hd]
            o_scr[hd] = base
            s_scr[hd] = _state_update(s_old, kd, v[hd], blast, 16)
            og_ref[rows, hcols[hd]] = _norm_gate(
                base + _dot(amat, vb[hd]), g_ref[rows, hcols[hd]], nw_ref[:, hcols[hd]]).astype(og_ref.dtype)

        @pl.when(jnp.logical_not(factorable))
        def _():
            small = [_small_levels(gt[1], gt[0], q[hd], kk[hd]) for hd, gt in enumerate(gates)]
            zs = [sm[0] for sm in small]
            part = [sm[1] for sm in small]
            for level in range(3, LEVELS):
                for hd in range(heads):
                    z, part[hd] = _level_large(part[hd], q[hd], kk[hd], level)
                    zs[hd].append(z)
            nblk = CHUNK // MASK_ROWS
            a = [[None] * nblk for _ in range(heads)]
            for level in range(LEVELS):
                step = 1 << (level - 4) if level >= 4 else 0
                blocks = [i for i in range(nblk) if level < 4 or (i & step)]
                for hd in range(heads):
                    z = zs[hd][level].astype(BF16)
                    lhs = z if level < 4 else jnp.concatenate(
                        [z[i * MASK_ROWS:(i + 1) * MASK_ROWS] for i in blocks], axis=0)
                    term = _dot_nt(lhs, z)
                    for n, i in enumerate(blocks):
                        t = (term[n * MASK_ROWS:(n + 1) * MASK_ROWS]
                             * m_ref[level + 1, i * MASK_ROWS:(i + 1) * MASK_ROWS, :])
                        a[hd][i] = t if a[hd][i] is None else a[hd][i] + t
            for hd in range(heads):
                amat = jnp.concatenate(a[hd], axis=0).astype(BF16)
                og_ref[rows, hcols[hd]] = _norm_gate(
                    o_scr[hd] + _dot(amat, vb[hd]),
                    g_ref[rows, hcols[hd]], nw_ref[:, hcols[hd]]).astype(og_ref.dtype)
        return carry

    lax.fori_loop(0, n_chunks, chunk, 0)
    sout_ref[0] = s_scr[...]


def _hgrn_seq(h, lb, nw, s0, *, n_seq, seq_len, row_block0):
    assert seq_len % CHUNK == 0
    hb = HGRN_SEQ_HEADS
    width = hb * HGRN_DK
    groups = HGRN_HEADS // hb
    s0_batched = s0.shape[0] != 1
    col = lambda part: (lambda b, hg: (row_block0 + b, part * groups + hg))
    masks = jnp.asarray(_level_masks())
    return pl.pallas_call(
        functools.partial(_hgrn_seq_kernel, n_chunks=seq_len // CHUNK),
        grid=(n_seq, groups),
        in_specs=[pl.BlockSpec((seq_len, width), col(0)),
                  pl.BlockSpec((seq_len, width), col(1)),
                  pl.BlockSpec((seq_len, width), col(2)),
                  pl.BlockSpec((seq_len, width), col(3)),
                  pl.BlockSpec((1, width), lambda b, hg: (0, hg)),
                  pl.BlockSpec((1, width), lambda b, hg: (0, hg)),
                  pl.BlockSpec((1, hb, HGRN_DK, HGRN_DV),
                               (lambda b, hg: (b, hg, 0, 0)) if s0_batched else (lambda b, hg: (0, hg, 0, 0))),
                  pl.BlockSpec((LEVELS + 1, CHUNK, CHUNK), lambda b, hg: (0, 0, 0))],
        out_specs=[pl.BlockSpec((seq_len, width), lambda b, hg: (b, hg)),
                   pl.BlockSpec((1, hb, HGRN_DK, HGRN_DV), lambda b, hg: (b, hg, 0, 0))],
        out_shape=[jax.ShapeDtypeStruct((n_seq * seq_len, HGRN_HEADS * HGRN_DV), BF16),
                   jax.ShapeDtypeStruct((n_seq, HGRN_HEADS, HGRN_DK, HGRN_DV), F32)],
        scratch_shapes=[pltpu.VMEM((hb, HGRN_DK, HGRN_DV), F32),
                        pltpu.VMEM((hb, CHUNK, HGRN_DV), F32)],
        compiler_params=pltpu.CompilerParams(
            dimension_semantics=("parallel", "parallel"), vmem_limit_bytes=VMEM_LIMIT),
        name="hgrn_seq",
    )(h, h, h, h, lb, nw, s0, masks)


def _hgrn_step_kernel(q_ref, fx_ref, i_ref, g_ref, lb_ref, nw_ref, s_ref, og_ref, sout_ref, *, n_seq, t):
    lb = lb_ref[...]
    nw = nw_ref[...]
    tile = (STEP_GROUP, t, HGRN_DK)
    sub = lax.broadcasted_iota(jnp.int32, (1, t, HGRN_DK), 1)

    def group(i, carry):
        rows = pl.ds(pl.multiple_of(i * (STEP_GROUP * t), STEP_GROUP * t), STEP_GROUP * t)
        q = q_ref[rows, :].reshape(tile)
        v = i_ref[rows, :].reshape(tile)
        _, logf, kk = _gates(fx_ref[rows, :].reshape(tile), lb)
        cum = logf
        shift = 1
        while shift < t:
            cum = cum + jnp.where(sub >= shift, pltpu.roll(cum, shift, 1), 0.0)
            shift *= 2
        intra = (jnp.sum(q * kk, axis=-1, keepdims=True)) * v
        for d in range(1, t):
            valid = sub >= d
            x = jnp.exp2(jnp.where(valid, cum - pltpu.roll(cum, d, 1), 0.0))
            w = jnp.sum(jnp.where(valid, q * x * pltpu.roll(kk, d, 1), 0.0), axis=-1, keepdims=True)
            intra = intra + w * pltpu.roll(v, d, 1)
        blast = cum[:, t - 1:t, :]
        qd = q * jnp.exp2(cum)
        kd = kk * jnp.exp2(blast - cum)
        inter = []
        for n in range(STEP_GROUP):
            b = i * STEP_GROUP + n
            s_old = s_ref[b, 0]
            inter.append(_dot(qd[n].astype(BF16), s_old.astype(BF16)))
            sout_ref[b, 0] = _state_update(s_old, kd[n], v[n], blast[n], 8)
        o = jnp.stack(inter, axis=0) + intra
        og = _norm_gate(o, g_ref[rows, :].reshape(tile), nw)
        og_ref[rows, :] = og.reshape(STEP_GROUP * t, HGRN_DK).astype(og_ref.dtype)
        return carry

    lax.fori_loop(0, n_seq // STEP_GROUP, group, 0)


def _hgrn_step(h, lb, nw, s0, *, n_seq, t, seq_per_block):
    seq_per_block = min(seq_per_block, n_seq)
    assert n_seq % seq_per_block == 0 and seq_per_block % STEP_GROUP == 0 and t == 8
    rows = seq_per_block * t
    col = lambda part: (lambda i, hd: (i, part * HGRN_HEADS + hd))
    return pl.pallas_call(
        functools.partial(_hgrn_step_kernel, n_seq=seq_per_block, t=t),
        grid=(n_seq // seq_per_block, HGRN_HEADS),
        in_specs=[pl.BlockSpec((rows, HGRN_DK), col(0)),
                  pl.BlockSpec((rows, HGRN_DK), col(1)),
                  pl.BlockSpec((rows, HGRN_DV), col(2)),
                  pl.BlockSpec((rows, HGRN_DV), col(3)),
                  pl.BlockSpec((1, HGRN_DK), lambda i, hd: (0, hd)),
                  pl.BlockSpec((1, HGRN_DV), lambda i, hd: (0, hd)),
                  pl.BlockSpec((seq_per_block, 1, HGRN_DK, HGRN_DV), lambda i, hd: (i, hd, 0, 0))],
        out_specs=[pl.BlockSpec((rows, HGRN_DV), lambda i, hd: (i, hd)),
                   pl.BlockSpec((seq_per_block, 1, HGRN_DK, HGRN_DV), lambda i, hd: (i, hd, 0, 0))],
        out_shape=[jax.ShapeDtypeStruct((n_seq * t, HGRN_HEADS * HGRN_DV), BF16),
                   jax.ShapeDtypeStruct((n_seq, HGRN_HEADS, HGRN_DK, HGRN_DV), F32)],
        compiler_params=pltpu.CompilerParams(
            dimension_semantics=("parallel", "parallel"), vmem_limit_bytes=VMEM_LIMIT),
        name="hgrn_step",
    )(h, h, h, h, lb, nw, s0)


def _attn_seq_kernel(q_ref, g_ref, kc_ref, vc_ref, kp_ref, vp_ref, meta_ref, sink_ref, og_ref):
    first = pl.program_id(1) == 0
    kprev = jnp.where(first, meta_ref[:, :SWA_KV_WIDTH], kp_ref[...])
    vprev = jnp.where(first, meta_ref[:, SWA_KV_WIDTH:], vp_ref[...])
    kband = (jnp.concatenate([kprev, kc_ref[...]], axis=0) * (SWA_SCALE * LOG2E)).astype(BF16)
    vband = jnp.concatenate([vprev, vc_ref[...]], axis=0).astype(BF16)
    kj = lax.broadcasted_iota(jnp.int32, (2 * SEQ_BLOCK, SEQ_BLOCK), 0)
    qi = lax.broadcasted_iota(jnp.int32, (2 * SEQ_BLOCK, SEQ_BLOCK), 1)
    dist = SEQ_BLOCK + qi - kj
    valid = (dist >= 0) & (dist < WINDOW) & (jnp.logical_not(first) | (kj >= SEQ_BLOCK - N_META))
    madd = jnp.where(valid, 0.0, -jnp.inf)
    zeros = jnp.zeros((2 * SEQ_BLOCK, SWA_HEAD_DIM), BF16)
    pairs = SWA_GROUP // 2
    pw = 2 * SWA_HEAD_DIM

    def block_diag(x):
        return jnp.concatenate([jnp.concatenate([x, zeros], axis=1),
                                jnp.concatenate([zeros, x], axis=1)], axis=0)

    sts = []
    for kvh in range(SWA_KV_HEADS):
        ks = slice(kvh * SWA_HEAD_DIM, (kvh + 1) * SWA_HEAD_DIM)
        col0 = kvh * SWA_GROUP * SWA_HEAD_DIM
        xq = jnp.concatenate([q_ref[:, col0 + p * pw:col0 + (p + 1) * pw].astype(BF16)
                              for p in range(pairs)], axis=0)
        sts.append(_dot_nt(block_diag(kband[:, ks]), xq))
    for kvh in range(SWA_KV_HEADS):
        ks = slice(kvh * SWA_HEAD_DIM, (kvh + 1) * SWA_HEAD_DIM)
        col0 = kvh * SWA_GROUP * SWA_HEAD_DIM
        v2 = block_diag(vband[:, ks])
        pns = []
        for pr in range(pairs):
            head = kvh * SWA_GROUP + 2 * pr
            s2 = (sts[kvh][:, pr * SEQ_BLOCK:(pr + 1) * SEQ_BLOCK].reshape(2, 2 * SEQ_BLOCK, SEQ_BLOCK)
                  + madd[None])
            sink = jnp.concatenate([jnp.full((1, 1, SEQ_BLOCK), sink_ref[head] * LOG2E, F32),
                                    jnp.full((1, 1, SEQ_BLOCK), sink_ref[head + 1] * LOG2E, F32)], axis=0)
            m = jnp.maximum(jnp.max(s2, axis=1, keepdims=True), sink)
            p = jnp.exp2(s2 - m)
            den = jnp.sum(p, axis=1, keepdims=True) + jnp.exp2(sink - m)
            pns.append((p * (1.0 / den)).astype(BF16).reshape(4 * SEQ_BLOCK, SEQ_BLOCK))
        for pr in range(pairs):
            o = _dot_tn(pns[pr], v2)
            cs = slice(col0 + pr * pw, col0 + (pr + 1) * pw)
            og_ref[:, cs] = (o * _silu(g_ref[:, cs])).astype(og_ref.dtype)


def _attn_seq(h1, h1_meta, sinks, *, n_seq, seq_len, meta_row_block):
    nb = seq_len // SEQ_BLOCK
    width = SWA_Q_HEADS * SWA_HEAD_DIM
    kcol = 2 * width // SWA_KV_WIDTH
    return pl.pallas_call(
        _attn_seq_kernel,
        grid=(n_seq, nb),
        in_specs=[pl.BlockSpec((SEQ_BLOCK, width), lambda b, j: (b * nb + j, 0)),
                  pl.BlockSpec((SEQ_BLOCK, width), lambda b, j: (b * nb + j, 1)),
                  pl.BlockSpec((SEQ_BLOCK, SWA_KV_WIDTH), lambda b, j: (b * nb + j, kcol)),
                  pl.BlockSpec((SEQ_BLOCK, SWA_KV_WIDTH), lambda b, j: (b * nb + j, kcol + 1)),
                  pl.BlockSpec((SEQ_BLOCK, SWA_KV_WIDTH), lambda b, j: (b * nb + jnp.maximum(j - 1, 0), kcol)),
                  pl.BlockSpec((SEQ_BLOCK, SWA_KV_WIDTH), lambda b, j: (b * nb + jnp.maximum(j - 1, 0), kcol + 1)),
                  pl.BlockSpec((SEQ_BLOCK, 2 * SWA_KV_WIDTH), lambda b, j: (meta_row_block, kcol // 2)),
                  pl.BlockSpec(memory_space=pltpu.SMEM)],
        out_specs=pl.BlockSpec((SEQ_BLOCK, width), lambda b, j: (b * nb + j, 0)),
        out_shape=jax.ShapeDtypeStruct((n_seq * seq_len, width), BF16),
        compiler_params=pltpu.CompilerParams(
            dimension_semantics=("parallel", "arbitrary"), vmem_limit_bytes=VMEM_LIMIT),
        name="attn_seq",
    )(h1, h1, h1, h1, h1, h1, h1_meta, sinks)


def _attn_step_kernel(q_ref, g_ref, kn_ref, vn_ref, ck_ref, cv_ref, sink_ref, og_ref, nk_ref, nv_ref, *, n_seq, t):
    keys = WINDOW + t
    hd = SWA_HEAD_DIM
    tiles = SWA_Q_HEADS // 2
    kj = lax.broadcasted_iota(jnp.int32, (keys, SWA_Q_HEADS * t), 0)
    qt = lax.broadcasted_iota(jnp.int32, (keys, SWA_Q_HEADS * t), 1) % t
    madd = jnp.where((kj >= qt + 1) & (kj <= WINDOW + qt), 0.0, -jnp.inf)
    low = lax.broadcasted_iota(jnp.int32, (t, 2 * hd), 1) < hd
    zero_tile = jnp.zeros((t, 2 * hd), F32)
    sink = sink_ref[...] * LOG2E

    def group(i, carry):
        seqs = [i * SEQ_UNROLL + n for n in range(SEQ_UNROLL)]
        rows = [pl.ds(pl.multiple_of(b * t, t), t) for b in seqs]
        st, vall = [], []
        for b, rw in zip(seqs, rows):
            kc, vc = ck_ref[b], cv_ref[b]
            kn, vn = kn_ref[rw, :], vn_ref[rw, :]
            nk_ref[b, 0:WINDOW - t, :] = kc[t:, :]
            nk_ref[b, WINDOW - t:WINDOW, :] = kn
            nv_ref[b, 0:WINDOW - t, :] = vc[t:, :]
            nv_ref[b, WINDOW - t:WINDOW, :] = vn
            kall = jnp.concatenate([kc, kn], axis=0).astype(BF16)
            vall.append(jnp.concatenate([vc, vn], axis=0).astype(BF16))
            q = q_ref[rw, :] * (SWA_SCALE * LOG2E)
            qtile = [q[:, j * 2 * hd:(j + 1) * 2 * hd] for j in range(tiles)]
            qswap = [pltpu.roll(x, hd, 1) for x in qtile]
            groups = []
            for kvh in range(SWA_KV_HEADS):
                for gq in range(SWA_GROUP):
                    j = (kvh * SWA_GROUP + gq) // 2
                    src = qtile[j] if gq % 2 == kvh % 2 else qswap[j]
                    half = jnp.where(low, src, 0.0) if kvh % 2 == 0 else jnp.where(low, 0.0, src)
                    groups.append(jnp.concatenate(
                        [half if c == kvh // 2 else zero_tile for c in range(SWA_KV_HEADS // 2)], axis=1))
            qbd = jnp.concatenate(groups, axis=0).astype(BF16)
            st.append(_dot_nt(kall, qbd))
        pn = []
        for s in st:
            s = s + madd
            m = jnp.maximum(jnp.max(s, axis=0, keepdims=True), sink)
            p = jnp.exp2(s - m)
            den = jnp.sum(p, axis=0, keepdims=True) + jnp.exp2(sink - m)
            pn.append((p * (1.0 / den)).astype(BF16))
        of = [_dot_tn(p, vl) for p, vl in zip(pn, vall)]
        for o, rw in zip(of, rows):
            g = g_ref[rw, :]
            out = []
            for j in range(tiles):
                kvh = (2 * j) // SWA_GROUP
                ct = slice((kvh // 2) * 2 * hd, (kvh // 2 + 1) * 2 * hd)
                ra = o[(2 * j) * t:(2 * j + 1) * t, ct]
                rb = o[(2 * j + 1) * t:(2 * j + 2) * t, ct]
                if kvh % 2 == 0:
                    out.append(jnp.where(low, ra, pltpu.roll(rb, hd, 1)))
                else:
                    out.append(jnp.where(low, pltpu.roll(ra, hd, 1), rb))
            og_ref[rw, :] = (jnp.concatenate(out, axis=1) * _silu(g)).astype(og_ref.dtype)
        return carry

    lax.fori_loop(0, n_seq // SEQ_UNROLL, group, 0)


def _attn_step(h1, cache_k, cache_v, sink_cols, *, n_seq, t, seq_per_block):
    assert n_seq % seq_per_block == 0
    rows = seq_per_block * t
    width = SWA_Q_HEADS * SWA_HEAD_DIM
    kcol = 2 * width // SWA_KV_WIDTH
    cache_spec = pl.BlockSpec((seq_per_block, WINDOW, SWA_KV_WIDTH), lambda i: (i, 0, 0))
    return pl.pallas_call(
        functools.partial(_attn_step_kernel, n_seq=seq_per_block, t=t),
        grid=(n_seq // seq_per_block,),
        in_specs=[pl.BlockSpec((rows, width), lambda i: (i, 0)),
                  pl.BlockSpec((rows, width), lambda i: (i, 1)),
                  pl.BlockSpec((rows, SWA_KV_WIDTH), lambda i: (i, kcol)),
                  pl.BlockSpec((rows, SWA_KV_WIDTH), lambda i: (i, kcol + 1)),
                  cache_spec, cache_spec,
                  pl.BlockSpec((1, SWA_Q_HEADS * t), lambda i: (0, 0))],
        out_specs=[pl.BlockSpec((rows, width), lambda i: (i, 0)), cache_spec, cache_spec],
        out_shape=[jax.ShapeDtypeStruct((n_seq * t, width), BF16),
                   jax.ShapeDtypeStruct(cache_k.shape, F32),
                   jax.ShapeDtypeStruct(cache_v.shape, F32)],
        compiler_params=pltpu.CompilerParams(
            dimension_semantics=("parallel",), vmem_limit_bytes=VMEM_LIMIT),
        name="attn_step",
    )(h1, h1, h1, h1, cache_k, cache_v, sink_cols)


def kernel(x_prompt, x_sample, state_hgrn, cache_swa_k, cache_swa_v, meta_tokens,
           hgrn_w_in, hgrn_lb_logits, hgrn_norm_w, hgrn_w_out,
           swa_w_in, swa_sinks, swa_w_out, ln_g, ln_b):
    out_dtype = x_prompt.dtype
    bsz, seq, d = x_prompt.shape
    dec_b, dec_t, _ = x_sample.shape
    n_p = bsz * seq
    n_s = dec_b * dec_t
    width = SWA_Q_HEADS * SWA_HEAD_DIM

    w_in0 = hgrn_w_in[0].astype(BF16)
    w_out0 = hgrn_w_out[0].astype(BF16)
    wi = swa_w_in[0]
    w_in1 = jnp.concatenate([wi[:, :width], wi[:, width + 2 * SWA_KV_WIDTH:],
                             wi[:, width:width + 2 * SWA_KV_WIDTH]], axis=1).astype(BF16)
    w_out1 = swa_w_out[0].astype(BF16)
    lb = jnp.cumsum(jax.nn.softmax(hgrn_lb_logits.astype(F32), axis=0), axis=0)[0:1]
    nw = hgrn_norm_w[0].astype(F32).reshape(1, -1)
    sinks = swa_sinks[0].astype(F32).reshape(1, SWA_Q_HEADS)
    sink_cols = jnp.repeat(sinks, dec_t, axis=1)
    g0, b0 = ln_g[0:1].astype(F32), ln_b[0:1].astype(F32)
    g1, b1 = ln_g[1:2].astype(F32), ln_b[1:2].astype(F32)

    x_p = x_prompt.astype(F32).reshape(n_p, d)
    meta_block = jnp.concatenate(
        [jnp.zeros((SEQ_BLOCK - N_META, d), F32), meta_tokens.astype(F32)], axis=0)
    x_sm = jnp.concatenate([x_sample.astype(F32).reshape(n_s, d), meta_block], axis=0)
    n_sm = n_s + SEQ_BLOCK
    meta_blk = n_s // SEQ_BLOCK

    h_sm = _matmul(x_sm, w_in0, tm=n_sm, tn=512)
    h_p = _matmul(x_p, w_in0, tm=1024, tn=1024)
    zero_state = jnp.zeros((1, HGRN_HEADS, HGRN_DK, HGRN_DV), F32)
    og_meta, s_meta = _hgrn_seq(h_sm, lb, nw, zero_state, n_seq=1, seq_len=SEQ_BLOCK, row_block0=meta_blk)
    og_s, st_s = _hgrn_step(h_sm, lb, nw, state_hgrn[0].astype(F32), n_seq=dec_b, t=dec_t, seq_per_block=64)
    og_p, st_p = _hgrn_seq(h_p, lb, nw, s_meta, n_seq=bsz, seq_len=seq, row_block0=0)
    x1_sm = _matmul_deepnorm(jnp.concatenate([og_s, og_meta], axis=0), w_out0, x_sm, g0, b0, tm=512)
    x1_p = _matmul_deepnorm(og_p, w_out0, x_p, g0, b0, tm=512)

    h1_sm = _matmul(x1_sm, w_in1, tm=n_sm, tn=512)
    h1_p = _matmul(x1_p, w_in1, tm=1024, tn=1536)
    og1_p = _attn_seq(h1_p, h1_sm, sinks.reshape(-1), n_seq=bsz, seq_len=seq, meta_row_block=meta_blk)
    ck = cache_swa_k[0].astype(F32).reshape(dec_b, WINDOW, SWA_KV_WIDTH)
    cv = cache_swa_v[0].astype(F32).reshape(dec_b, WINDOW, SWA_KV_WIDTH)
    og1_s, nk_s, nv_s = _attn_step(h1_sm, ck, cv, sink_cols, n_seq=dec_b, t=dec_t, seq_per_block=16)
    y_p = _matmul_deepnorm(og1_p, w_out1, x1_p, g1, b1, tm=512)
    y_s = _matmul_deepnorm(og1_s, w_out1, x1_sm, g1, b1, tm=512)

    kv_p = h1_p.reshape(bsz, seq, -1)[:, seq - WINDOW:, 2 * width:]
    cache_shape = (1, bsz, WINDOW, SWA_KV_HEADS, SWA_HEAD_DIM)
    return (y_p.reshape(bsz, seq, d).astype(out_dtype),
            y_s.reshape(dec_b, dec_t, d).astype(out_dtype),
            st_p[None].astype(out_dtype),
            st_s[None].astype(out_dtype),
            kv_p[..., :SWA_KV_WIDTH].reshape(cache_shape).astype(out_dtype),
            kv_p[..., SWA_KV_WIDTH:].reshape(cache_shape).astype(out_dtype),
            nk_s.reshape((1,) + cache_swa_k.shape[1:]).astype(out_dtype),
            nv_s.reshape((1,) + cache_swa_v.shape[1:]).astype(out_dtype))
```

```python
import functools

import numpy as np
import jax
import jax.numpy as jnp
from jax import lax
from jax.experimental import pallas as pl
from jax.experimental.pallas import tpu as pltpu

F32 = jnp.float32
BF16 = jnp.bfloat16

D_MODEL = 2048
N_META = 16
DEPTH = 2
HGRN_HEADS = 16
HGRN_DK = 128
HGRN_DV = 128
SWA_Q_HEADS = 32
SWA_KV_HEADS = 4
SWA_GROUP = 8
SWA_HEAD_DIM = 64
SWA_KV_WIDTH = SWA_KV_HEADS * SWA_HEAD_DIM
SWA_SCALE = SWA_HEAD_DIM ** -0.5
WINDOW = 128
DEEPNORM_ALPHA = (2.0 * DEPTH) ** 0.25
LN_EPS = 1e-5
RMS_EPS = 1e-6
LOG2E = 1.4426950408889634

CHUNK = 128
LEVELS = 7
MAX_FACTORED_LOG2_DECAY = 100.0
MASK_ROWS = 16
SEQ_BLOCK = 128
STEP_GROUP = 8
SEQ_UNROLL = 4
HGRN_SEQ_HEADS = 8
HGRN_SEQ_ROWS = 512
VMEM_LIMIT = 56 * 1024 * 1024


def _dot(a, b):
    return jnp.dot(a, b, preferred_element_type=F32)


def _dot_nt(a, b):
    return lax.dot_general(a, b, (((1,), (1,)), ((), ())), preferred_element_type=F32)


def _dot_tn(a, b):
    return lax.dot_general(a, b, (((0,), (0,)), ((), ())), preferred_element_type=F32)


def _silu(x):
    return x * jax.nn.sigmoid(x)


def _mm_kernel(x_ref, w_ref, o_ref):
    o_ref[...] = _dot(x_ref[...].astype(BF16), w_ref[...]).astype(o_ref.dtype)


def _row_tile(n, want):
    if n <= want:
        return n
    return max(t for t in range(16, want + 1, 16) if n % t == 0)


def _matmul(x, w, *, tm, tn, out_dtype=F32):
    n, k = x.shape
    e = w.shape[1]
    tm = _row_tile(n, tm)
    assert n % tm == 0 and e % tn == 0
    return pl.pallas_call(
        _mm_kernel,
        grid=(n // tm, e // tn),
        in_specs=[pl.BlockSpec((tm, k), lambda i, j: (i, 0)),
                  pl.BlockSpec((k, tn), lambda i, j: (0, j))],
        out_specs=pl.BlockSpec((tm, tn), lambda i, j: (i, j)),
        out_shape=jax.ShapeDtypeStruct((n, e), out_dtype),
        compiler_params=pltpu.CompilerParams(
            dimension_semantics=("parallel", "parallel"), vmem_limit_bytes=VMEM_LIMIT),
        name="proj_in",
    )(x, w)


def _mm_ln_kernel(a_ref, w_ref, x_ref, g_ref, b_ref, o_ref):
    z = DEEPNORM_ALPHA * x_ref[...] + _dot(a_ref[...], w_ref[...])
    mu = jnp.mean(z, axis=-1, keepdims=True)
    zc = z - mu
    var = jnp.mean(zc * zc, axis=-1, keepdims=True)
    o_ref[...] = zc * lax.rsqrt(var + LN_EPS) * g_ref[...] + b_ref[...]


def _matmul_deepnorm(a, w, x, g, b, *, tm):
    n, k = a.shape
    d = w.shape[1]
    tm = _row_tile(n, tm)
    return pl.pallas_call(
        _mm_ln_kernel,
        grid=(n // tm,),
        in_specs=[pl.BlockSpec((tm, k), lambda i: (i, 0)),
                  pl.BlockSpec((k, d), lambda i: (0, 0)),
                  pl.BlockSpec((tm, d), lambda i: (i, 0)),
                  pl.BlockSpec((1, d), lambda i: (0, 0)),
                  pl.BlockSpec((1, d), lambda i: (0, 0))],
        out_specs=pl.BlockSpec((tm, d), lambda i: (i, 0)),
        out_shape=jax.ShapeDtypeStruct((n, d), F32),
        compiler_params=pltpu.CompilerParams(
            dimension_semantics=("parallel",), vmem_limit_bytes=VMEM_LIMIT),
        name="proj_out_deepnorm",
    )(a, w, x, g, b)


def _level_masks():
    t = np.arange(CHUNK)
    out = np.zeros((LEVELS + 1, CHUNK, CHUNK), np.float32)
    out[0] = t[:, None] > t[None, :]
    for l in range(LEVELS):
        h = 1 << l
        same = (t[:, None] >> (l + 1)) == (t[None, :] >> (l + 1))
        out[l + 1] = same & ((t[:, None] & h) != 0) & ((t[None, :] & h) == 0)
    return out


def _half_total(cum, level):
    rows, lanes = cum.shape
    blk = 1 << (level + 1)
    idx = (1 << level) - 1
    if blk >= 8:
        x = cum.reshape(rows // blk, blk, lanes)
        return jnp.broadcast_to(x[:, idx:idx + 1, :], x.shape).reshape(rows, lanes)
    x = cum.reshape(rows // 8, 8, lanes)
    sub = lax.broadcasted_iota(jnp.int32, x.shape, 1)
    out = None
    for start in range(8 - blk, -1, -blk):
        piece = jnp.broadcast_to(x[:, start + idx:start + idx + 1, :], x.shape)
        out = piece if out is None else jnp.where(sub < start + blk, piece, out)
    return out.reshape(rows, lanes)


def _gates(fx, lb):
    f = lb + (1.0 - lb) * jax.nn.sigmoid(fx)
    return f, jnp.log2(f), 1.0 - f


def _prefix_scan(logf):
    tiles = (CHUNK // 8, 8, HGRN_DK)
    c = logf.reshape(tiles)
    sub = lax.broadcasted_iota(jnp.int32, (1, 8, HGRN_DK), 1)
    bcast = lambda x, r: jnp.broadcast_to(x[:, r:r + 1, :], tiles)
    c = c + jnp.where((sub & 1) != 0, pltpu.roll(c, 1, 1), 0.0)
    c = c + jnp.where((sub & 2) != 0, jnp.where(sub < 4, bcast(c, 1), bcast(c, 5)), 0.0)
    c = c + jnp.where((sub & 4) != 0, bcast(c, 3), 0.0)
    c = c.reshape(CHUNK, HGRN_DK)
    for level in range(3, LEVELS):
        half = 1 << level
        pieces = []
        for r0 in range(0, CHUNK, 2 * half):
            pieces += [c[r0:r0 + half], c[r0 + half:r0 + 2 * half] + c[r0 + half - 1:r0 + half, :]]
        c = jnp.concatenate(pieces, axis=0)
    return c


def _small_levels(logf, f, q, kk):
    tiles = (CHUNK // 8, 8, HGRN_DK)
    c, f3, q3, k3 = (x.reshape(tiles) for x in (logf, f, q, kk))
    sub = lax.broadcasted_iota(jnp.int32, (1, 8, HGRN_DK), 1)
    bcast = lambda x, r: jnp.broadcast_to(x[:, r:r + 1, :], tiles)
    up = (sub & 1) != 0
    zs = [jnp.where(up, q3 * f3, k3)]
    c = c + jnp.where(up, pltpu.roll(c, 1, 1), 0.0)
    for level, tot in ((1, lambda c: jnp.where(sub < 4, bcast(c, 1), bcast(c, 5))),
                       (2, lambda c: bcast(c, 3))):
        up = (sub & (1 << level)) != 0
        t = tot(c)
        zs.append(jnp.where(up, q3, k3) * jnp.exp2(jnp.where(up, c, t - c)))
        c = c + jnp.where(up, t, 0.0)
    return [z.reshape(CHUNK, HGRN_DK) for z in zs], c.reshape(CHUNK, HGRN_DK)


def _level_large(cum, q, kk, level):
    half = 1 << level
    args, bases, cums = [], [], []
    for r0 in range(0, CHUNK, 2 * half):
        lo, up = slice(r0, r0 + half), slice(r0 + half, r0 + 2 * half)
        tot = cum[r0 + half - 1:r0 + half, :]
        args += [tot - cum[lo], cum[up]]
        bases += [kk[lo], q[up]]
        cums += [cum[lo], cum[up] + tot]
    x = jnp.exp2(jnp.concatenate(args, axis=0))
    return jnp.concatenate(bases, axis=0) * x, jnp.concatenate(cums, axis=0)


def _split3(x):
    hi = x.astype(BF16)
    r = x - hi.astype(F32)
    mid = r.astype(BF16)
    lo = (r - mid.astype(F32)).astype(BF16)
    return hi, mid, lo


def _state_update(s_old, kd, v, blast, pad_rows):
    c = kd.shape[0]
    hi, mid, lo = _split3(jnp.exp2(blast))
    row = lax.broadcasted_iota(jnp.int32, (pad_rows, HGRN_DK), 0)
    dec = jnp.where(row == 0, hi.astype(F32),
                    jnp.where(row == 1, mid.astype(F32), jnp.where(row == 2, lo.astype(F32), 0.0)))
    lhs = jnp.concatenate([kd, dec], axis=0).astype(BF16)
    rhs = jnp.concatenate(
        [jnp.concatenate([v, jnp.zeros((c, HGRN_DV), F32)], axis=1),
         jnp.concatenate([jnp.zeros((pad_rows, HGRN_DV), F32), jnp.ones((pad_rows, HGRN_DV), F32)], axis=1)],
        axis=0).astype(BF16)
    both = _dot_tn(lhs, rhs)
    return both[:, HGRN_DV:] * s_old + both[:, :HGRN_DV]


def _norm_gate(o, g, nw):
    o = o * lax.rsqrt(jnp.mean(o * o, axis=-1, keepdims=True) + RMS_EPS) * nw
    return o * _silu(g)


def _hgrn_seq_kernel(q_ref, fx_ref, i_ref, g_ref, lb_ref, nw_ref, s0_ref, m_ref,
                     og_ref, sout_ref, s_scr, o_scr, *, n_chunks):
    heads = s_scr.shape[0]

    @pl.when(pl.program_id(2) == 0)
    def _():
        s_scr[...] = s0_ref[0]

    def chunk(c, carry):
        rows = pl.ds(pl.multiple_of(c * CHUNK, CHUNK), CHUNK)
        hcols = [slice(hd * HGRN_DK, (hd + 1) * HGRN_DK) for hd in range(heads)]
        q = [q_ref[rows, cs] for cs in hcols]
        v = [i_ref[rows, cs] for cs in hcols]
        gates = [_gates(fx_ref[rows, cs], lb_ref[:, cs]) for cs in hcols]
        kk = [gt[2] for gt in gates]
        cum = [_prefix_scan(gt[1]) for gt in gates]
        mid = CHUNK // 2 - 1
        spread = None
        for b in cum:
            s_hd = jnp.maximum(b[0:1, :] - b[mid:mid + 1, :], b[mid:mid + 1, :] - b[CHUNK - 1:CHUNK, :])
            spread = s_hd if spread is None else jnp.maximum(spread, s_hd)
        factorable = jnp.max(spread) <= MAX_FACTORED_LOG2_DECAY

        vb = [x.astype(BF16) for x in v]
        for hd in range(heads):
            d = cum[hd] - cum[hd][mid:mid + 1, :]
            qe = (q[hd] * jnp.exp2(d)).astype(BF16)
            ke = (kk[hd] * jnp.exp2(-d)).astype(BF16)
            amat = (_dot_nt(qe, ke) * m_ref[0]).astype(BF16)
            blast = cum[hd][CHUNK - 1:CHUNK, :]
            qd = (q[hd] * jnp.exp2(cum[hd])).astype(BF16)
            kd = kk[hd] * jnp.exp2(blast - cum[hd])
            s_old = s_scr[hd]
            base = _dot(qd, s_old.astype(BF16)) + jnp.sum(q[hd] * kk[hd], axis=-1, keepdims=True) * v[hd]
            o_scr[hd] = base
            s_scr[hd] = _state_update(s_old, kd, v[hd], blast, 16)
            og_ref[rows, hcols[hd]] = _norm_gate(
                base + _dot(amat, vb[hd]), g_ref[rows, hcols[hd]], nw_ref[:, hcols[hd]]).astype(og_ref.dtype)

        @pl.when(jnp.logical_not(factorable))
        def _():
            small = [_small_levels(gt[1], gt[0], q[hd], kk[hd]) for hd, gt in enumerate(gates)]
            zs = [sm[0] for sm in small]
            part = [sm[1] for sm in small]
            for level in range(3, LEVELS):
                for hd in range(heads):
                    z, part[hd] = _level_large(part[hd], q[hd], kk[hd], level)
                    zs[hd].append(z)
            nblk = CHUNK // MASK_ROWS
            a = [[None] * nblk for _ in range(heads)]
            for level in range(LEVELS):
                step = 1 << (level - 4) if level >= 4 else 0
                blocks = [i for i in range(nblk) if level < 4 or (i & step)]
                for hd in range(heads):
                    z = zs[hd][level].astype(BF16)
                    lhs = z if level < 4 else jnp.concatenate(
                        [z[i * MASK_ROWS:(i + 1) * MASK_ROWS] for i in blocks], axis=0)
                    term = _dot_nt(lhs, z)
                    for n, i in enumerate(blocks):
                        t = (term[n * MASK_ROWS:(n + 1) * MASK_ROWS]
                             * m_ref[level + 1, i * MASK_ROWS:(i + 1) * MASK_ROWS, :])
                        a[hd][i] = t if a[hd][i] is None else a[hd][i] + t
            for hd in range(heads):
                amat = jnp.concatenate(a[hd], axis=0).astype(BF16)
                og_ref[rows, hcols[hd]] = _norm_gate(
                    o_scr[hd] + _dot(amat, vb[hd]),
                    g_ref[rows, hcols[hd]], nw_ref[:, hcols[hd]]).astype(og_ref.dtype)
        return carry

    lax.fori_loop(0, n_chunks, chunk, 0)

    @pl.when(pl.program_id(2) == pl.num_programs(2) - 1)
    def _():
        sout_ref[0] = s_scr[...]


def _hgrn_seq(h, lb, nw, s0, *, n_seq, seq_len, row_block0):
    hb = HGRN_SEQ_HEADS
    width = hb * HGRN_DK
    groups = HGRN_HEADS // hb
    rb = min(HGRN_SEQ_ROWS, seq_len)
    nrb = seq_len // rb
    assert seq_len % rb == 0 and rb % CHUNK == 0 and (row_block0 * SEQ_BLOCK) % rb == 0
    rb0 = row_block0 * SEQ_BLOCK // rb
    s0_batched = s0.shape[0] != 1
    col = lambda part: (lambda b, hg, r: (rb0 + b * nrb + r, part * groups + hg))
    masks = jnp.asarray(_level_masks())
    return pl.pallas_call(
        functools.partial(_hgrn_seq_kernel, n_chunks=rb // CHUNK),
        grid=(n_seq, groups, nrb),
        in_specs=[pl.BlockSpec((rb, width), col(0)),
                  pl.BlockSpec((rb, width), col(1)),
                  pl.BlockSpec((rb, width), col(2)),
                  pl.BlockSpec((rb, width), col(3)),
                  pl.BlockSpec((1, width), lambda b, hg, r: (0, hg)),
                  pl.BlockSpec((1, width), lambda b, hg, r: (0, hg)),
                  pl.BlockSpec((1, hb, HGRN_DK, HGRN_DV),
                               (lambda b, hg, r: (b, hg, 0, 0)) if s0_batched else (lambda b, hg, r: (0, hg, 0, 0))),
                  pl.BlockSpec((LEVELS + 1, CHUNK, CHUNK), lambda b, hg, r: (0, 0, 0))],
        out_specs=[pl.BlockSpec((rb, width), lambda b, hg, r: (b * nrb + r, hg)),
                   pl.BlockSpec((1, hb, HGRN_DK, HGRN_DV), lambda b, hg, r: (b, hg, 0, 0))],
        out_shape=[jax.ShapeDtypeStruct((n_seq * seq_len, HGRN_HEADS * HGRN_DV), BF16),
                   jax.ShapeDtypeStruct((n_seq, HGRN_HEADS, HGRN_DK, HGRN_DV), F32)],
        scratch_shapes=[pltpu.VMEM((hb, HGRN_DK, HGRN_DV), F32),
                        pltpu.VMEM((hb, CHUNK, HGRN_DV), F32)],
        compiler_params=pltpu.CompilerParams(
            dimension_semantics=("parallel", "parallel", "arbitrary"), vmem_limit_bytes=VMEM_LIMIT),
        name="hgrn_seq",
    )(h, h, h, h, lb, nw, s0, masks)


def _hgrn_step_kernel(q_ref, fx_ref, i_ref, g_ref, lb_ref, nw_ref, s_ref, og_ref, sout_ref, *, n_seq, t):
    lb = lb_ref[...]
    nw = nw_ref[...]
    tile = (STEP_GROUP, t, HGRN_DK)
    sub = lax.broadcasted_iota(jnp.int32, (1, t, HGRN_DK), 1)

    def group(i, carry):
        rows = pl.ds(pl.multiple_of(i * (STEP_GROUP * t), STEP_GROUP * t), STEP_GROUP * t)
        q = q_ref[rows, :].reshape(tile)
        v = i_ref[rows, :].reshape(tile)
        _, logf, kk = _gates(fx_ref[rows, :].reshape(tile), lb)
        cum = logf
        shift = 1
        while shift < t:
            cum = cum + jnp.where(sub >= shift, pltpu.roll(cum, shift, 1), 0.0)
            shift *= 2
        intra = (jnp.sum(q * kk, axis=-1, keepdims=True)) * v
        for d in range(1, t):
            valid = sub >= d
            x = jnp.exp2(jnp.where(valid, cum - pltpu.roll(cum, d, 1), 0.0))
            w = jnp.sum(jnp.where(valid, q * x * pltpu.roll(kk, d, 1), 0.0), axis=-1, keepdims=True)
            intra = intra + w * pltpu.roll(v, d, 1)
        blast = cum[:, t - 1:t, :]
        qd = q * jnp.exp2(cum)
        kd = kk * jnp.exp2(blast - cum)
        inter = []
        for n in range(STEP_GROUP):
            b = i * STEP_GROUP + n
            s_old = s_ref[b, 0]
            inter.append(_dot(qd[n].astype(BF16), s_old.astype(BF16)))
            sout_ref[b, 0] = _state_update(s_old, kd[n], v[n], blast[n], 8)
        o = jnp.stack(inter, axis=0) + intra
        og = _norm_gate(o, g_ref[rows, :].reshape(tile), nw)
        og_ref[rows, :] = og.reshape(STEP_GROUP * t, HGRN_DK).astype(og_ref.dtype)
        return carry

    lax.fori_loop(0, n_seq // STEP_GROUP, group, 0)


def _hgrn_step(h, lb, nw, s0, *, n_seq, t, seq_per_block):
    seq_per_block = min(seq_per_block, n_seq)
    assert n_seq % seq_per_block == 0 and seq_per_block % STEP_GROUP == 0 and t == 8
    rows = seq_per_block * t
    col = lambda part: (lambda i, hd: (i, part * HGRN_HEADS + hd))
    return pl.pallas_call(
        functools.partial(_hgrn_step_kernel, n_seq=seq_per_block, t=t),
        grid=(n_seq // seq_per_block, HGRN_HEADS),
        in_specs=[pl.BlockSpec((rows, HGRN_DK), col(0)),
                  pl.BlockSpec((rows, HGRN_DK), col(1)),
                  pl.BlockSpec((rows, HGRN_DV), col(2)),
                  pl.BlockSpec((rows, HGRN_DV), col(3)),
                  pl.BlockSpec((1, HGRN_DK), lambda i, hd: (0, hd)),
                  pl.BlockSpec((1, HGRN_DV), lambda i, hd: (0, hd)),
                  pl.BlockSpec((seq_per_block, 1, HGRN_DK, HGRN_DV), lambda i, hd: (i, hd, 0, 0))],
        out_specs=[pl.BlockSpec((rows, HGRN_DV), lambda i, hd: (i, hd)),
                   pl.BlockSpec((seq_per_block, 1, HGRN_DK, HGRN_DV), lambda i, hd: (i, hd, 0, 0))],
        out_shape=[jax.ShapeDtypeStruct((n_seq * t, HGRN_HEADS * HGRN_DV), BF16),
                   jax.ShapeDtypeStruct((n_seq, HGRN_HEADS, HGRN_DK, HGRN_DV), F32)],
        compiler_params=pltpu.CompilerParams(
            dimension_semantics=("parallel", "parallel"), vmem_limit_bytes=VMEM_LIMIT),
        name="hgrn_step",
    )(h, h, h, h, lb, nw, s0)


def _attn_seq_kernel(q_ref, g_ref, kc_ref, vc_ref, kp_ref, vp_ref, meta_ref, sink_ref, og_ref):
    first = pl.program_id(1) == 0
    kprev = jnp.where(first, meta_ref[:, :SWA_KV_WIDTH], kp_ref[...])
    vprev = jnp.where(first, meta_ref[:, SWA_KV_WIDTH:], vp_ref[...])
    kband = (jnp.concatenate([kprev, kc_ref[...]], axis=0) * (SWA_SCALE * LOG2E)).astype(BF16)
    vband = jnp.concatenate([vprev, vc_ref[...]], axis=0).astype(BF16)
    kj = lax.broadcasted_iota(jnp.int32, (2 * SEQ_BLOCK, SEQ_BLOCK), 0)
    qi = lax.broadcasted_iota(jnp.int32, (2 * SEQ_BLOCK, SEQ_BLOCK), 1)
    dist = SEQ_BLOCK + qi - kj
    valid = (dist >= 0) & (dist < WINDOW) & (jnp.logical_not(first) | (kj >= SEQ_BLOCK - N_META))
    madd = jnp.where(valid, 0.0, -jnp.inf)
    zeros = jnp.zeros((2 * SEQ_BLOCK, SWA_HEAD_DIM), BF16)
    pairs = SWA_GROUP // 2
    pw = 2 * SWA_HEAD_DIM

    def block_diag(x):
        return jnp.concatenate([jnp.concatenate([x, zeros], axis=1),
                                jnp.concatenate([zeros, x], axis=1)], axis=0)

    sts = []
    for kvh in range(SWA_KV_HEADS):
        ks = slice(kvh * SWA_HEAD_DIM, (kvh + 1) * SWA_HEAD_DIM)
        col0 = kvh * SWA_GROUP * SWA_HEAD_DIM
        xq = jnp.concatenate([q_ref[:, col0 + p * pw:col0 + (p + 1) * pw].astype(BF16)
                              for p in range(pairs)], axis=0)
        sts.append(_dot_nt(block_diag(kband[:, ks]), xq))
    for kvh in range(SWA_KV_HEADS):
        ks = slice(kvh * SWA_HEAD_DIM, (kvh + 1) * SWA_HEAD_DIM)
        col0 = kvh * SWA_GROUP * SWA_HEAD_DIM
        v2 = block_diag(vband[:, ks])
        pns = []
        for pr in range(pairs):
            head = kvh * SWA_GROUP + 2 * pr
            s2 = (sts[kvh][:, pr * SEQ_BLOCK:(pr + 1) * SEQ_BLOCK].reshape(2, 2 * SEQ_BLOCK, SEQ_BLOCK)
                  + madd[None])
            sink = jnp.concatenate([jnp.full((1, 1, SEQ_BLOCK), sink_ref[head] * LOG2E, F32),
                                    jnp.full((1, 1, SEQ_BLOCK), sink_ref[head + 1] * LOG2E, F32)], axis=0)
            m = jnp.maximum(jnp.max(s2, axis=1, keepdims=True), sink)
            p = jnp.exp2(s2 - m)
            den = jnp.sum(p, axis=1, keepdims=True) + jnp.exp2(sink - m)
            pns.append((p * (1.0 / den)).astype(BF16).reshape(4 * SEQ_BLOCK, SEQ_BLOCK))
        for pr in range(pairs):
            o = _dot_tn(pns[pr], v2)
            cs = slice(col0 + pr * pw, col0 + (pr + 1) * pw)
            og_ref[:, cs] = (o * _silu(g_ref[:, cs])).astype(og_ref.dtype)


def _attn_seq(h1, h1_meta, sinks, *, n_seq, seq_len, meta_row_block):
    nb = seq_len // SEQ_BLOCK
    width = SWA_Q_HEADS * SWA_HEAD_DIM
    kcol = 2 * width // SWA_KV_WIDTH
    return pl.pallas_call(
        _attn_seq_kernel,
        grid=(n_seq, nb),
        in_specs=[pl.BlockSpec((SEQ_BLOCK, width), lambda b, j: (b * nb + j, 0)),
                  pl.BlockSpec((SEQ_BLOCK, width), lambda b, j: (b * nb + j, 1)),
                  pl.BlockSpec((SEQ_BLOCK, SWA_KV_WIDTH), lambda b, j: (b * nb + j, kcol)),
                  pl.BlockSpec((SEQ_BLOCK, SWA_KV_WIDTH), lambda b, j: (b * nb + j, kcol + 1)),
                  pl.BlockSpec((SEQ_BLOCK, SWA_KV_WIDTH), lambda b, j: (b * nb + jnp.maximum(j - 1, 0), kcol)),
                  pl.BlockSpec((SEQ_BLOCK, SWA_KV_WIDTH), lambda b, j: (b * nb + jnp.maximum(j - 1, 0), kcol + 1)),
                  pl.BlockSpec((SEQ_BLOCK, 2 * SWA_KV_WIDTH), lambda b, j: (meta_row_block, kcol // 2)),
                  pl.BlockSpec(memory_space=pltpu.SMEM)],
        out_specs=pl.BlockSpec((SEQ_BLOCK, width), lambda b, j: (b * nb + j, 0)),
        out_shape=jax.ShapeDtypeStruct((n_seq * seq_len, width), BF16),
        compiler_params=pltpu.CompilerParams(
            dimension_semantics=("parallel", "arbitrary"), vmem_limit_bytes=VMEM_LIMIT),
        name="attn_seq",
    )(h1, h1, h1, h1, h1, h1, h1_meta, sinks)


def _attn_step_kernel(q_ref, g_ref, kn_ref, vn_ref, ck_ref, cv_ref, sink_ref, og_ref, nk_ref, nv_ref, *, n_seq, t):
    keys = WINDOW + t
    hd = SWA_HEAD_DIM
    tiles = SWA_Q_HEADS // 2
    kj = lax.broadcasted_iota(jnp.int32, (keys, SWA_Q_HEADS * t), 0)
    qt = lax.broadcasted_iota(jnp.int32, (keys, SWA_Q_HEADS * t), 1) % t
    madd = jnp.where((kj >= qt + 1) & (kj <= WINDOW + qt), 0.0, -jnp.inf)
    low = lax.broadcasted_iota(jnp.int32, (t, 2 * hd), 1) < hd
    zero_tile = jnp.zeros((t, 2 * hd), F32)
    sink = sink_ref[...] * LOG2E

    def group(i, carry):
        seqs = [i * SEQ_UNROLL + n for n in range(SEQ_UNROLL)]
        rows = [pl.ds(pl.multiple_of(b * t, t), t) for b in seqs]
        st, vall = [], []
        for b, rw in zip(seqs, rows):
            kc, vc = ck_ref[b], cv_ref[b]
            kn, vn = kn_ref[rw, :], vn_ref[rw, :]
            nk_ref[b, 0:WINDOW - t, :] = kc[t:, :]
            nk_ref[b, WINDOW - t:WINDOW, :] = kn
            nv_ref[b, 0:WINDOW - t, :] = vc[t:, :]
            nv_ref[b, WINDOW - t:WINDOW, :] = vn
            kall = jnp.concatenate([kc, kn], axis=0).astype(BF16)
            vall.append(jnp.concatenate([vc, vn], axis=0).astype(BF16))
            q = q_ref[rw, :] * (SWA_SCALE * LOG2E)
            qtile = [q[:, j * 2 * hd:(j + 1) * 2 * hd] for j in range(tiles)]
            qswap = [pltpu.roll(x, hd, 1) for x in qtile]
            groups = []
            for kvh in range(SWA_KV_HEADS):
                for gq in range(SWA_GROUP):
                    j = (kvh * SWA_GROUP + gq) // 2
                    src = qtile[j] if gq % 2 == kvh % 2 else qswap[j]
                    half = jnp.where(low, src, 0.0) if kvh % 2 == 0 else jnp.where(low, 0.0, src)
                    groups.append(jnp.concatenate(
                        [half if c == kvh // 2 else zero_tile for c in range(SWA_KV_HEADS // 2)], axis=1))
            qbd = jnp.concatenate(groups, axis=0).astype(BF16)
            st.append(_dot_nt(kall, qbd))
        pn = []
        for s in st:
            s = s + madd
            m = jnp.maximum(jnp.max(s, axis=0, keepdims=True), sink)
            p = jnp.exp2(s - m)
            den = jnp.sum(p, axis=0, keepdims=True) + jnp.exp2(sink - m)
            pn.append((p * (1.0 / den)).astype(BF16))
        of = [_dot_tn(p, vl) for p, vl in zip(pn, vall)]
        for o, rw in zip(of, rows):
            g = g_ref[rw, :]
            out = []
            for j in range(tiles):
                kvh = (2 * j) // SWA_GROUP
                ct = slice((kvh // 2) * 2 * hd, (kvh // 2 + 1) * 2 * hd)
                ra = o[(2 * j) * t:(2 * j + 1) * t, ct]
                rb = o[(2 * j + 1) * t:(2 * j + 2) * t, ct]
                if kvh % 2 == 0:
                    out.append(jnp.where(low, ra, pltpu.roll(rb, hd, 1)))
                else:
                    out.append(jnp.where(low, pltpu.roll(ra, hd, 1), rb))
            og_ref[rw, :] = (jnp.concatenate(out, axis=1) * _silu(g)).astype(og_ref.dtype)
        return carry

    lax.fori_loop(0, n_seq // SEQ_UNROLL, group, 0)


def _attn_step(h1, cache_k, cache_v, sink_cols, *, n_seq, t, seq_per_block):
    assert n_seq % seq_per_block == 0
    rows = seq_per_block * t
    width = SWA_Q_HEADS * SWA_HEAD_DIM
    kcol = 2 * width // SWA_KV_WIDTH
    cache_spec = pl.BlockSpec((seq_per_block, WINDOW, SWA_KV_WIDTH), lambda i: (i, 0, 0))
    return pl.pallas_call(
        functools.partial(_attn_step_kernel, n_seq=seq_per_block, t=t),
        grid=(n_seq // seq_per_block,),
        in_specs=[pl.BlockSpec((rows, width), lambda i: (i, 0)),
                  pl.BlockSpec((rows, width), lambda i: (i, 1)),
                  pl.BlockSpec((rows, SWA_KV_WIDTH), lambda i: (i, kcol)),
                  pl.BlockSpec((rows, SWA_KV_WIDTH), lambda i: (i, kcol + 1)),
                  cache_spec, cache_spec,
                  pl.BlockSpec((1, SWA_Q_HEADS * t), lambda i: (0, 0))],
        out_specs=[pl.BlockSpec((rows, width), lambda i: (i, 0)), cache_spec, cache_spec],
        out_shape=[jax.ShapeDtypeStruct((n_seq * t, width), BF16),
                   jax.ShapeDtypeStruct(cache_k.shape, F32),
                   jax.ShapeDtypeStruct(cache_v.shape, F32)],
        compiler_params=pltpu.CompilerParams(
            dimension_semantics=("parallel",), vmem_limit_bytes=VMEM_LIMIT),
        name="attn_step",
    )(h1, h1, h1, h1, cache_k, cache_v, sink_cols)


def kernel(x_prompt, x_sample, state_hgrn, cache_swa_k, cache_swa_v, meta_tokens,
           hgrn_w_in, hgrn_lb_logits, hgrn_norm_w, hgrn_w_out,
           swa_w_in, swa_sinks, swa_w_out, ln_g, ln_b):
    out_dtype = x_prompt.dtype
    bsz, seq, d = x_prompt.shape
    dec_b, dec_t, _ = x_sample.shape
    n_p = bsz * seq
    n_s = dec_b * dec_t
    width = SWA_Q_HEADS * SWA_HEAD_DIM

    w_in0 = hgrn_w_in[0].astype(BF16)
    w_out0 = hgrn_w_out[0].astype(BF16)
    wi = swa_w_in[0]
    w_in1 = jnp.concatenate([wi[:, :width], wi[:, width + 2 * SWA_KV_WIDTH:],
                             wi[:, width:width + 2 * SWA_KV_WIDTH]], axis=1).astype(BF16)
    w_out1 = swa_w_out[0].astype(BF16)
    lb = jnp.cumsum(jax.nn.softmax(hgrn_lb_logits.astype(F32), axis=0), axis=0)[0:1]
    nw = hgrn_norm_w[0].astype(F32).reshape(1, -1)
    sinks = swa_sinks[0].astype(F32).reshape(1, SWA_Q_HEADS)
    sink_cols = jnp.repeat(sinks, dec_t, axis=1)
    g0, b0 = ln_g[0:1].astype(F32), ln_b[0:1].astype(F32)
    g1, b1 = ln_g[1:2].astype(F32), ln_b[1:2].astype(F32)

    x_p = x_prompt.astype(F32).reshape(n_p, d)
    meta_block = jnp.concatenate(
        [jnp.zeros((SEQ_BLOCK - N_META, d), F32), meta_tokens.astype(F32)], axis=0)
    x_sm = jnp.concatenate([x_sample.astype(F32).reshape(n_s, d), meta_block], axis=0)
    n_sm = n_s + SEQ_BLOCK
    meta_blk = n_s // SEQ_BLOCK

    h_sm = _matmul(x_sm, w_in0, tm=n_sm, tn=512)
    h_p = _matmul(x_p, w_in0, tm=1024, tn=1024)
    zero_state = jnp.zeros((1, HGRN_HEADS, HGRN_DK, HGRN_DV), F32)
    og_meta, s_meta = _hgrn_seq(h_sm, lb, nw, zero_state, n_seq=1, seq_len=SEQ_BLOCK, row_block0=meta_blk)
    og_s, st_s = _hgrn_step(h_sm, lb, nw, state_hgrn[0].astype(F32), n_seq=dec_b, t=dec_t, seq_per_block=64)
    og_p, st_p = _hgrn_seq(h_p, lb, nw, s_meta, n_seq=bsz, seq_len=seq, row_block0=0)
    x1_sm = _matmul_deepnorm(jnp.concatenate([og_s, og_meta], axis=0), w_out0, x_sm, g0, b0, tm=512)
    x1_p = _matmul_deepnorm(og_p, w_out0, x_p, g0, b0, tm=512)

    h1_sm = _matmul(x1_sm, w_in1, tm=n_sm, tn=512)
    h1_p = _matmul(x1_p, w_in1, tm=1024, tn=1536)
    og1_p = _attn_seq(h1_p, h1_sm, sinks.reshape(-1), n_seq=bsz, seq_len=seq, meta_row_block=meta_blk)
    ck = cache_swa_k[0].astype(F32).reshape(dec_b, WINDOW, SWA_KV_WIDTH)
    cv = cache_swa_v[0].astype(F32).reshape(dec_b, WINDOW, SWA_KV_WIDTH)
    og1_s, nk_s, nv_s = _attn_step(h1_sm, ck, cv, sink_cols, n_seq=dec_b, t=dec_t, seq_per_block=16)
    y_p = _matmul_deepnorm(og1_p, w_out1, x1_p, g1, b1, tm=512)
    y_s = _matmul_deepnorm(og1_s, w_out1, x1_sm, g1, b1, tm=512)

    kv_p = h1_p.reshape(bsz, seq, -1)[:, seq - WINDOW:, 2 * width:]
    cache_shape = (1, bsz, WINDOW, SWA_KV_HEADS, SWA_HEAD_DIM)
    return (y_p.reshape(bsz, seq, d).astype(out_dtype),
            y_s.reshape(dec_b, dec_t, d).astype(out_dtype),
            st_p[None].astype(out_dtype),
            st_s[None].astype(out_dtype),
            kv_p[..., :SWA_KV_WIDTH].reshape(cache_shape).astype(out_dtype),
            kv_p[..., SWA_KV_WIDTH:].reshape(cache_shape).astype(out_dtype),
            nk_s.reshape((1,) + cache_swa_k.shape[1:]).astype(out_dtype),
            nv_s.reshape((1,) + cache_swa_v.shape[1:]).astype(out_dtype))
```

```python
import functools

import numpy as np
import jax
import jax.numpy as jnp
from jax import lax
from jax.experimental import pallas as pl
from jax.experimental.pallas import tpu as pltpu

F32 = jnp.float32
BF16 = jnp.bfloat16

D_MODEL = 2048
N_META = 16
DEPTH = 2
HGRN_HEADS = 16
HGRN_DK = 128
HGRN_DV = 128
SWA_Q_HEADS = 32
SWA_KV_HEADS = 4
SWA_GROUP = 8
SWA_HEAD_DIM = 64
SWA_KV_WIDTH = SWA_KV_HEADS * SWA_HEAD_DIM
SWA_SCALE = SWA_HEAD_DIM ** -0.5
WINDOW = 128
DEEPNORM_ALPHA = (2.0 * DEPTH) ** 0.25
LN_EPS = 1e-5
RMS_EPS = 1e-6
LOG2E = 1.4426950408889634

CHUNK = 128
LEVELS = 7
MAX_FACTORED_LOG2_DECAY = 100.0
MASK_ROWS = 16
SEQ_BLOCK = 128
STEP_GROUP = 16
SEQ_UNROLL = 4
HGRN_SEQ_HEADS = 8
HGRN_SEQ_ROWS = 512
VMEM_LIMIT = 56 * 1024 * 1024


def _dot(a, b):
    return jnp.dot(a, b, preferred_element_type=F32)


def _dot_nt(a, b):
    return lax.dot_general(a, b, (((1,), (1,)), ((), ())), preferred_element_type=F32)


def _dot_tn(a, b):
    return lax.dot_general(a, b, (((0,), (0,)), ((), ())), preferred_element_type=F32)


def _silu(x):
    return x * jax.nn.sigmoid(x)


def _mm_kernel(x_ref, w_ref, o_ref):
    o_ref[...] = _dot(x_ref[...].astype(BF16), w_ref[...]).astype(o_ref.dtype)


def _row_tile(n, want):
    if n <= want:
        return n
    return max(t for t in range(16, want + 1, 16) if n % t == 0)


def _matmul(x, w, *, tm, tn, out_dtype=F32):
    n, k = x.shape
    e = w.shape[1]
    tm = _row_tile(n, tm)
    assert n % tm == 0 and e % tn == 0
    return pl.pallas_call(
        _mm_kernel,
        grid=(n // tm, e // tn),
        in_specs=[pl.BlockSpec((tm, k), lambda i, j: (i, 0)),
                  pl.BlockSpec((k, tn), lambda i, j: (0, j))],
        out_specs=pl.BlockSpec((tm, tn), lambda i, j: (i, j)),
        out_shape=jax.ShapeDtypeStruct((n, e), out_dtype),
        compiler_params=pltpu.CompilerParams(
            dimension_semantics=("parallel", "parallel"), vmem_limit_bytes=VMEM_LIMIT),
        name="proj_in",
    )(x, w)


def _mm_ln_kernel(a_ref, w_ref, x_ref, g_ref, b_ref, o_ref):
    z = DEEPNORM_ALPHA * x_ref[...] + _dot(a_ref[...], w_ref[...])
    mu = jnp.mean(z, axis=-1, keepdims=True)
    zc = z - mu
    var = jnp.mean(zc * zc, axis=-1, keepdims=True)
    o_ref[...] = zc * lax.rsqrt(var + LN_EPS) * g_ref[...] + b_ref[...]


def _matmul_deepnorm(a, w, x, g, b, *, tm):
    n, k = a.shape
    d = w.shape[1]
    tm = _row_tile(n, tm)
    return pl.pallas_call(
        _mm_ln_kernel,
        grid=(n // tm,),
        in_specs=[pl.BlockSpec((tm, k), lambda i: (i, 0)),
                  pl.BlockSpec((k, d), lambda i: (0, 0)),
                  pl.BlockSpec((tm, d), lambda i: (i, 0)),
                  pl.BlockSpec((1, d), lambda i: (0, 0)),
                  pl.BlockSpec((1, d), lambda i: (0, 0))],
        out_specs=pl.BlockSpec((tm, d), lambda i: (i, 0)),
        out_shape=jax.ShapeDtypeStruct((n, d), F32),
        compiler_params=pltpu.CompilerParams(
            dimension_semantics=("parallel",), vmem_limit_bytes=VMEM_LIMIT),
        name="proj_out_deepnorm",
    )(a, w, x, g, b)


def _level_masks():
    t = np.arange(CHUNK)
    out = np.zeros((LEVELS + 1, CHUNK, CHUNK), np.float32)
    out[0] = t[:, None] > t[None, :]
    for l in range(LEVELS):
        h = 1 << l
        same = (t[:, None] >> (l + 1)) == (t[None, :] >> (l + 1))
        out[l + 1] = same & ((t[:, None] & h) != 0) & ((t[None, :] & h) == 0)
    return out


def _half_total(cum, level):
    rows, lanes = cum.shape
    blk = 1 << (level + 1)
    idx = (1 << level) - 1
    if blk >= 8:
        x = cum.reshape(rows // blk, blk, lanes)
        return jnp.broadcast_to(x[:, idx:idx + 1, :], x.shape).reshape(rows, lanes)
    x = cum.reshape(rows // 8, 8, lanes)
    sub = lax.broadcasted_iota(jnp.int32, x.shape, 1)
    out = None
    for start in range(8 - blk, -1, -blk):
        piece = jnp.broadcast_to(x[:, start + idx:start + idx + 1, :], x.shape)
        out = piece if out is None else jnp.where(sub < start + blk, piece, out)
    return out.reshape(rows, lanes)


def _gates(fx, lb):
    f = lb + (1.0 - lb) * jax.nn.sigmoid(fx)
    return f, jnp.log2(f), 1.0 - f


def _prefix_scan(logf):
    tiles = (CHUNK // 8, 8, HGRN_DK)
    c = logf.reshape(tiles)
    sub = lax.broadcasted_iota(jnp.int32, (1, 8, HGRN_DK), 1)
    bcast = lambda x, r: jnp.broadcast_to(x[:, r:r + 1, :], tiles)
    c = c + jnp.where((sub & 1) != 0, pltpu.roll(c, 1, 1), 0.0)
    c = c + jnp.where((sub & 2) != 0, jnp.where(sub < 4, bcast(c, 1), bcast(c, 5)), 0.0)
    c = c + jnp.where((sub & 4) != 0, bcast(c, 3), 0.0)
    c = c.reshape(CHUNK, HGRN_DK)
    for level in range(3, LEVELS):
        half = 1 << level
        pieces = []
        for r0 in range(0, CHUNK, 2 * half):
            pieces += [c[r0:r0 + half], c[r0 + half:r0 + 2 * half] + c[r0 + half - 1:r0 + half, :]]
        c = jnp.concatenate(pieces, axis=0)
    return c


def _small_levels(logf, f, q, kk):
    tiles = (CHUNK // 8, 8, HGRN_DK)
    c, f3, q3, k3 = (x.reshape(tiles) for x in (logf, f, q, kk))
    sub = lax.broadcasted_iota(jnp.int32, (1, 8, HGRN_DK), 1)
    bcast = lambda x, r: jnp.broadcast_to(x[:, r:r + 1, :], tiles)
    up = (sub & 1) != 0
    zs = [jnp.where(up, q3 * f3, k3)]
    c = c + jnp.where(up, pltpu.roll(c, 1, 1), 0.0)
    for level, tot in ((1, lambda c: jnp.where(sub < 4, bcast(c, 1), bcast(c, 5))),
                       (2, lambda c: bcast(c, 3))):
        up = (sub & (1 << level)) != 0
        t = tot(c)
        zs.append(jnp.where(up, q3, k3) * jnp.exp2(jnp.where(up, c, t - c)))
        c = c + jnp.where(up, t, 0.0)
    return [z.reshape(CHUNK, HGRN_DK) for z in zs], c.reshape(CHUNK, HGRN_DK)


def _level_large(cum, q, kk, level):
    half = 1 << level
    args, bases, cums = [], [], []
    for r0 in range(0, CHUNK, 2 * half):
        lo, up = slice(r0, r0 + half), slice(r0 + half, r0 + 2 * half)
        tot = cum[r0 + half - 1:r0 + half, :]
        args += [tot - cum[lo], cum[up]]
        bases += [kk[lo], q[up]]
        cums += [cum[lo], cum[up] + tot]
    x = jnp.exp2(jnp.concatenate(args, axis=0))
    return jnp.concatenate(bases, axis=0) * x, jnp.concatenate(cums, axis=0)


def _split3(x):
    hi = x.astype(BF16)
    r = x - hi.astype(F32)
    mid = r.astype(BF16)
    lo = (r - mid.astype(F32)).astype(BF16)
    return hi, mid, lo


def _state_update(s_old, kd, v, blast, pad_rows):
    c = kd.shape[0]
    hi, mid, lo = _split3(jnp.exp2(blast))
    row = lax.broadcasted_iota(jnp.int32, (pad_rows, HGRN_DK), 0)
    dec = jnp.where(row == 0, hi.astype(F32),
                    jnp.where(row == 1, mid.astype(F32), jnp.where(row == 2, lo.astype(F32), 0.0)))
    lhs = jnp.concatenate([kd, dec], axis=0).astype(BF16)
    rhs = jnp.concatenate(
        [jnp.concatenate([v, jnp.zeros((c, HGRN_DV), F32)], axis=1),
         jnp.concatenate([jnp.zeros((pad_rows, HGRN_DV), F32), jnp.ones((pad_rows, HGRN_DV), F32)], axis=1)],
        axis=0).astype(BF16)
    both = _dot_tn(lhs, rhs)
    return both[:, HGRN_DV:] * s_old + both[:, :HGRN_DV]


def _norm_gate(o, g, nw):
    o = o * lax.rsqrt(jnp.mean(o * o, axis=-1, keepdims=True) + RMS_EPS) * nw
    return o * _silu(g)


def _hgrn_seq_kernel(q_ref, fx_ref, i_ref, g_ref, lb_ref, nw_ref, s0_ref, m_ref,
                     og_ref, sout_ref, s_scr, o_scr, *, n_chunks):
    heads = s_scr.shape[0]

    @pl.when(pl.program_id(2) == 0)
    def _():
        s_scr[...] = s0_ref[0]

    def chunk(c, carry):
        rows = pl.ds(pl.multiple_of(c * CHUNK, CHUNK), CHUNK)
        hcols = [slice(hd * HGRN_DK, (hd + 1) * HGRN_DK) for hd in range(heads)]
        q = [q_ref[rows, cs] for cs in hcols]
        v = [i_ref[rows, cs] for cs in hcols]
        gates = [_gates(fx_ref[rows, cs], lb_ref[:, cs]) for cs in hcols]
        kk = [gt[2] for gt in gates]
        cum = [_prefix_scan(gt[1]) for gt in gates]
        mid = CHUNK // 2 - 1
        spread = None
        for b in cum:
            s_hd = jnp.maximum(b[0:1, :] - b[mid:mid + 1, :], b[mid:mid + 1, :] - b[CHUNK - 1:CHUNK, :])
            spread = s_hd if spread is None else jnp.maximum(spread, s_hd)
        factorable = jnp.max(spread) <= MAX_FACTORED_LOG2_DECAY

        vb = [x.astype(BF16) for x in v]
        for hd in range(heads):
            d = cum[hd] - cum[hd][mid:mid + 1, :]
            qe = (q[hd] * jnp.exp2(d)).astype(BF16)
            ke = (kk[hd] * jnp.exp2(-d)).astype(BF16)
            amat = (_dot_nt(qe, ke) * m_ref[0]).astype(BF16)
            blast = cum[hd][CHUNK - 1:CHUNK, :]
            qd = (q[hd] * jnp.exp2(cum[hd])).astype(BF16)
            kd = kk[hd] * jnp.exp2(blast - cum[hd])
            s_old = s_scr[hd]
            base = _dot(qd, s_old.astype(BF16)) + jnp.sum(q[hd] * kk[hd], axis=-1, keepdims=True) * v[hd]
            o_scr[hd] = base
            s_scr[hd] = _state_update(s_old, kd, v[hd], blast, 16)
            og_ref[rows, hcols[hd]] = _norm_gate(
                base + _dot(amat, vb[hd]), g_ref[rows, hcols[hd]], nw_ref[:, hcols[hd]]).astype(og_ref.dtype)

        @pl.when(jnp.logical_not(factorable))
        def _():
            small = [_small_levels(gt[1], gt[0], q[hd], kk[hd]) for hd, gt in enumerate(gates)]
            zs = [sm[0] for sm in small]
            part = [sm[1] for sm in small]
            for level in range(3, LEVELS):
                for hd in range(heads):
                    z, part[hd] = _level_large(part[hd], q[hd], kk[hd], level)
                    zs[hd].append(z)
            nblk = CHUNK // MASK_ROWS
            a = [[None] * nblk for _ in range(heads)]
            for level in range(LEVELS):
                step = 1 << (level - 4) if level >= 4 else 0
                blocks = [i for i in range(nblk) if level < 4 or (i & step)]
                for hd in range(heads):
                    z = zs[hd][level].astype(BF16)
                    lhs = z if level < 4 else jnp.concatenate(
                        [z[i * MASK_ROWS:(i + 1) * MASK_ROWS] for i in blocks], axis=0)
                    term = _dot_nt(lhs, z)
                    for n, i in enumerate(blocks):
                        t = (term[n * MASK_ROWS:(n + 1) * MASK_ROWS]
                             * m_ref[level + 1, i * MASK_ROWS:(i + 1) * MASK_ROWS, :])
                        a[hd][i] = t if a[hd][i] is None else a[hd][i] + t
            for hd in range(heads):
                amat = jnp.concatenate(a[hd], axis=0).astype(BF16)
                og_ref[rows, hcols[hd]] = _norm_gate(
                    o_scr[hd] + _dot(amat, vb[hd]),
                    g_ref[rows, hcols[hd]], nw_ref[:, hcols[hd]]).astype(og_ref.dtype)
        return carry

    lax.fori_loop(0, n_chunks, chunk, 0)

    @pl.when(pl.program_id(2) == pl.num_programs(2) - 1)
    def _():
        sout_ref[0] = s_scr[...]


def _hgrn_seq(h, lb, nw, s0, *, n_seq, seq_len, row_block0):
    hb = HGRN_SEQ_HEADS
    width = hb * HGRN_DK
    groups = HGRN_HEADS // hb
    rb = min(HGRN_SEQ_ROWS, seq_len)
    nrb = seq_len // rb
    assert seq_len % rb == 0 and rb % CHUNK == 0 and (row_block0 * SEQ_BLOCK) % rb == 0
    rb0 = row_block0 * SEQ_BLOCK // rb
    s0_batched = s0.shape[0] != 1
    col = lambda part: (lambda b, hg, r: (rb0 + b * nrb + r, part * groups + hg))
    masks = jnp.asarray(_level_masks())
    return pl.pallas_call(
        functools.partial(_hgrn_seq_kernel, n_chunks=rb // CHUNK),
        grid=(n_seq, groups, nrb),
        in_specs=[pl.BlockSpec((rb, width), col(0)),
                  pl.BlockSpec((rb, width), col(1)),
                  pl.BlockSpec((rb, width), col(2)),
                  pl.BlockSpec((rb, width), col(3)),
                  pl.BlockSpec((1, width), lambda b, hg, r: (0, hg)),
                  pl.BlockSpec((1, width), lambda b, hg, r: (0, hg)),
                  pl.BlockSpec((1, hb, HGRN_DK, HGRN_DV),
                               (lambda b, hg, r: (b, hg, 0, 0)) if s0_batched else (lambda b, hg, r: (0, hg, 0, 0))),
                  pl.BlockSpec((LEVELS + 1, CHUNK, CHUNK), lambda b, hg, r: (0, 0, 0))],
        out_specs=[pl.BlockSpec((rb, width), lambda b, hg, r: (b * nrb + r, hg)),
                   pl.BlockSpec((1, hb, HGRN_DK, HGRN_DV), lambda b, hg, r: (b, hg, 0, 0))],
        out_shape=[jax.ShapeDtypeStruct((n_seq * seq_len, HGRN_HEADS * HGRN_DV), BF16),
                   jax.ShapeDtypeStruct((n_seq, HGRN_HEADS, HGRN_DK, HGRN_DV), F32)],
        scratch_shapes=[pltpu.VMEM((hb, HGRN_DK, HGRN_DV), F32),
                        pltpu.VMEM((hb, CHUNK, HGRN_DV), F32)],
        compiler_params=pltpu.CompilerParams(
            dimension_semantics=("parallel", "parallel", "arbitrary"), vmem_limit_bytes=VMEM_LIMIT),
        name="hgrn_seq",
    )(h, h, h, h, lb, nw, s0, masks)


def _hgrn_step_kernel(q_ref, fx_ref, i_ref, g_ref, lb_ref, nw_ref, s_ref, og_ref, sout_ref, *, n_seq, t):
    lb = lb_ref[...]
    nw = nw_ref[...]
    tile = (STEP_GROUP, t, HGRN_DK)
    sub = lax.broadcasted_iota(jnp.int32, (1, t, HGRN_DK), 1)

    def group(i, carry):
        rows = pl.ds(pl.multiple_of(i * (STEP_GROUP * t), STEP_GROUP * t), STEP_GROUP * t)
        q = q_ref[rows, :].reshape(tile)
        v = i_ref[rows, :].reshape(tile)
        _, logf, kk = _gates(fx_ref[rows, :].reshape(tile), lb)
        cum = logf
        shift = 1
        while shift < t:
            cum = cum + jnp.where(sub >= shift, pltpu.roll(cum, shift, 1), 0.0)
            shift *= 2
        intra = (jnp.sum(q * kk, axis=-1, keepdims=True)) * v
        for d in range(1, t):
            valid = sub >= d
            x = jnp.exp2(jnp.where(valid, cum - pltpu.roll(cum, d, 1), 0.0))
            w = jnp.sum(jnp.where(valid, q * x * pltpu.roll(kk, d, 1), 0.0), axis=-1, keepdims=True)
            intra = intra + w * pltpu.roll(v, d, 1)
        blast = cum[:, t - 1:t, :]
        qd = q * jnp.exp2(cum)
        kd = kk * jnp.exp2(blast - cum)
        seqs = [i * STEP_GROUP + n for n in range(STEP_GROUP)]
        inter = [_dot(qd[n].astype(BF16), s_ref[b, 0].astype(BF16)) for n, b in enumerate(seqs)]
        for n, b in enumerate(seqs):
            sout_ref[b, 0] = _state_update(s_ref[b, 0], kd[n], v[n], blast[n], 8)
        o = jnp.stack(inter, axis=0) + intra
        og = _norm_gate(o, g_ref[rows, :].reshape(tile), nw)
        og_ref[rows, :] = og.reshape(STEP_GROUP * t, HGRN_DK).astype(og_ref.dtype)
        return carry

    lax.fori_loop(0, n_seq // STEP_GROUP, group, 0)


def _hgrn_step(h, lb, nw, s0, *, n_seq, t, seq_per_block):
    seq_per_block = min(seq_per_block, n_seq)
    assert n_seq % seq_per_block == 0 and seq_per_block % STEP_GROUP == 0 and t == 8
    rows = seq_per_block * t
    col = lambda part: (lambda i, hd: (i, part * HGRN_HEADS + hd))
    return pl.pallas_call(
        functools.partial(_hgrn_step_kernel, n_seq=seq_per_block, t=t),
        grid=(n_seq // seq_per_block, HGRN_HEADS),
        in_specs=[pl.BlockSpec((rows, HGRN_DK), col(0)),
                  pl.BlockSpec((rows, HGRN_DK), col(1)),
                  pl.BlockSpec((rows, HGRN_DV), col(2)),
                  pl.BlockSpec((rows, HGRN_DV), col(3)),
                  pl.BlockSpec((1, HGRN_DK), lambda i, hd: (0, hd)),
                  pl.BlockSpec((1, HGRN_DV), lambda i, hd: (0, hd)),
                  pl.BlockSpec((seq_per_block, 1, HGRN_DK, HGRN_DV), lambda i, hd: (i, hd, 0, 0))],
        out_specs=[pl.BlockSpec((rows, HGRN_DV), lambda i, hd: (i, hd)),
                   pl.BlockSpec((seq_per_block, 1, HGRN_DK, HGRN_DV), lambda i, hd: (i, hd, 0, 0))],
        out_shape=[jax.ShapeDtypeStruct((n_seq * t, HGRN_HEADS * HGRN_DV), BF16),
                   jax.ShapeDtypeStruct((n_seq, HGRN_HEADS, HGRN_DK, HGRN_DV), F32)],
        compiler_params=pltpu.CompilerParams(
            dimension_semantics=("parallel", "parallel"), vmem_limit_bytes=VMEM_LIMIT),
        name="hgrn_step",
    )(h, h, h, h, lb, nw, s0)


def _attn_seq_kernel(q_ref, g_ref, kc_ref, vc_ref, kp_ref, vp_ref, meta_ref, sink_ref, og_ref):
    first = pl.program_id(1) == 0
    kprev = jnp.where(first, meta_ref[:, :SWA_KV_WIDTH], kp_ref[...])
    vprev = jnp.where(first, meta_ref[:, SWA_KV_WIDTH:], vp_ref[...])
    kband = (jnp.concatenate([kprev, kc_ref[...]], axis=0) * (SWA_SCALE * LOG2E)).astype(BF16)
    vband = jnp.concatenate([vprev, vc_ref[...]], axis=0).astype(BF16)
    kj = lax.broadcasted_iota(jnp.int32, (2 * SEQ_BLOCK, SEQ_BLOCK), 0)
    qi = lax.broadcasted_iota(jnp.int32, (2 * SEQ_BLOCK, SEQ_BLOCK), 1)
    dist = SEQ_BLOCK + qi - kj
    valid = (dist >= 0) & (dist < WINDOW) & (jnp.logical_not(first) | (kj >= SEQ_BLOCK - N_META))
    madd = jnp.where(valid, 0.0, -jnp.inf)
    zeros = jnp.zeros((2 * SEQ_BLOCK, SWA_HEAD_DIM), BF16)
    pairs = SWA_GROUP // 2
    pw = 2 * SWA_HEAD_DIM

    def block_diag(x):
        return jnp.concatenate([jnp.concatenate([x, zeros], axis=1),
                                jnp.concatenate([zeros, x], axis=1)], axis=0)

    sts = []
    for kvh in range(SWA_KV_HEADS):
        ks = slice(kvh * SWA_HEAD_DIM, (kvh + 1) * SWA_HEAD_DIM)
        col0 = kvh * SWA_GROUP * SWA_HEAD_DIM
        xq = jnp.concatenate([q_ref[:, col0 + p * pw:col0 + (p + 1) * pw].astype(BF16)
                              for p in range(pairs)], axis=0)
        sts.append(_dot_nt(block_diag(kband[:, ks]), xq))
    for kvh in range(SWA_KV_HEADS):
        ks = slice(kvh * SWA_HEAD_DIM, (kvh + 1) * SWA_HEAD_DIM)
        col0 = kvh * SWA_GROUP * SWA_HEAD_DIM
        v2 = block_diag(vband[:, ks])
        pns = []
        for pr in range(pairs):
            head = kvh * SWA_GROUP + 2 * pr
            s2 = (sts[kvh][:, pr * SEQ_BLOCK:(pr + 1) * SEQ_BLOCK].reshape(2, 2 * SEQ_BLOCK, SEQ_BLOCK)
                  + madd[None])
            sink = jnp.concatenate([jnp.full((1, 1, SEQ_BLOCK), sink_ref[head] * LOG2E, F32),
                                    jnp.full((1, 1, SEQ_BLOCK), sink_ref[head + 1] * LOG2E, F32)], axis=0)
            m = jnp.maximum(jnp.max(s2, axis=1, keepdims=True), sink)
            p = jnp.exp2(s2 - m)
            den = jnp.sum(p, axis=1, keepdims=True) + jnp.exp2(sink - m)
            pns.append((p * (1.0 / den)).astype(BF16).reshape(4 * SEQ_BLOCK, SEQ_BLOCK))
        for pr in range(pairs):
            o = _dot_tn(pns[pr], v2)
            cs = slice(col0 + pr * pw, col0 + (pr + 1) * pw)
            og_ref[:, cs] = (o * _silu(g_ref[:, cs])).astype(og_ref.dtype)


def _attn_seq(h1, h1_meta, sinks, *, n_seq, seq_len, meta_row_block):
    nb = seq_len // SEQ_BLOCK
    width = SWA_Q_HEADS * SWA_HEAD_DIM
    kcol = 2 * width // SWA_KV_WIDTH
    return pl.pallas_call(
        _attn_seq_kernel,
        grid=(n_seq, nb),
        in_specs=[pl.BlockSpec((SEQ_BLOCK, width), lambda b, j: (b * nb + j, 0)),
                  pl.BlockSpec((SEQ_BLOCK, width), lambda b, j: (b * nb + j, 1)),
                  pl.BlockSpec((SEQ_BLOCK, SWA_KV_WIDTH), lambda b, j: (b * nb + j, kcol)),
                  pl.BlockSpec((SEQ_BLOCK, SWA_KV_WIDTH), lambda b, j: (b * nb + j, kcol + 1)),
                  pl.BlockSpec((SEQ_BLOCK, SWA_KV_WIDTH), lambda b, j: (b * nb + jnp.maximum(j - 1, 0), kcol)),
                  pl.BlockSpec((SEQ_BLOCK, SWA_KV_WIDTH), lambda b, j: (b * nb + jnp.maximum(j - 1, 0), kcol + 1)),
                  pl.BlockSpec((SEQ_BLOCK, 2 * SWA_KV_WIDTH), lambda b, j: (meta_row_block, kcol // 2)),
                  pl.BlockSpec(memory_space=pltpu.SMEM)],
        out_specs=pl.BlockSpec((SEQ_BLOCK, width), lambda b, j: (b * nb + j, 0)),
        out_shape=jax.ShapeDtypeStruct((n_seq * seq_len, width), BF16),
        compiler_params=pltpu.CompilerParams(
            dimension_semantics=("parallel", "arbitrary"), vmem_limit_bytes=VMEM_LIMIT),
        name="attn_seq",
    )(h1, h1, h1, h1, h1, h1, h1_meta, sinks)


def _attn_step_kernel(q_ref, g_ref, kn_ref, vn_ref, ck_ref, cv_ref, sink_ref, og_ref, nk_ref, nv_ref, *, n_seq, t):
    keys = WINDOW + t
    hd = SWA_HEAD_DIM
    tiles = SWA_Q_HEADS // 2
    kj = lax.broadcasted_iota(jnp.int32, (keys, SWA_Q_HEADS * t), 0)
    qt = lax.broadcasted_iota(jnp.int32, (keys, SWA_Q_HEADS * t), 1) % t
    madd = jnp.where((kj >= qt + 1) & (kj <= WINDOW + qt), 0.0, -jnp.inf)
    low = lax.broadcasted_iota(jnp.int32, (t, 2 * hd), 1) < hd
    zero_tile = jnp.zeros((t, 2 * hd), F32)
    sink = sink_ref[...] * LOG2E

    def group(i, carry):
        seqs = [i * SEQ_UNROLL + n for n in range(SEQ_UNROLL)]
        rows = [pl.ds(pl.multiple_of(b * t, t), t) for b in seqs]
        st, vall = [], []
        for b, rw in zip(seqs, rows):
            kc, vc = ck_ref[b], cv_ref[b]
            kn, vn = kn_ref[rw, :], vn_ref[rw, :]
            nk_ref[b, 0:WINDOW - t, :] = kc[t:, :]
            nk_ref[b, WINDOW - t:WINDOW, :] = kn
            nv_ref[b, 0:WINDOW - t, :] = vc[t:, :]
            nv_ref[b, WINDOW - t:WINDOW, :] = vn
            kall = jnp.concatenate([kc, kn], axis=0).astype(BF16)
            vall.append(jnp.concatenate([vc, vn], axis=0).astype(BF16))
            q = q_ref[rw, :] * (SWA_SCALE * LOG2E)
            qtile = [q[:, j * 2 * hd:(j + 1) * 2 * hd] for j in range(tiles)]
            qswap = [pltpu.roll(x, hd, 1) for x in qtile]
            groups = []
            for kvh in range(SWA_KV_HEADS):
                for gq in range(SWA_GROUP):
                    j = (kvh * SWA_GROUP + gq) // 2
                    src = qtile[j] if gq % 2 == kvh % 2 else qswap[j]
                    half = jnp.where(low, src, 0.0) if kvh % 2 == 0 else jnp.where(low, 0.0, src)
                    groups.append(jnp.concatenate(
                        [half if c == kvh // 2 else zero_tile for c in range(SWA_KV_HEADS // 2)], axis=1))
            qbd = jnp.concatenate(groups, axis=0).astype(BF16)
            st.append(_dot_nt(kall, qbd))
        pn = []
        for s in st:
            s = s + madd
            m = jnp.maximum(jnp.max(s, axis=0, keepdims=True), sink)
            p = jnp.exp2(s - m)
            den = jnp.sum(p, axis=0, keepdims=True) + jnp.exp2(sink - m)
            pn.append((p * (1.0 / den)).astype(BF16))
        of = [_dot_tn(p, vl) for p, vl in zip(pn, vall)]
        for o, rw in zip(of, rows):
            g = g_ref[rw, :]
            out = []
            for j in range(tiles):
                kvh = (2 * j) // SWA_GROUP
                ct = slice((kvh // 2) * 2 * hd, (kvh // 2 + 1) * 2 * hd)
                ra = o[(2 * j) * t:(2 * j + 1) * t, ct]
                rb = o[(2 * j + 1) * t:(2 * j + 2) * t, ct]
                if kvh % 2 == 0:
                    out.append(jnp.where(low, ra, pltpu.roll(rb, hd, 1)))
                else:
                    out.append(jnp.where(low, pltpu.roll(ra, hd, 1), rb))
            og_ref[rw, :] = (jnp.concatenate(out, axis=1) * _silu(g)).astype(og_ref.dtype)
        return carry

    lax.fori_loop(0, n_seq // SEQ_UNROLL, group, 0)


def _attn_step(h1, cache_k, cache_v, sink_cols, *, n_seq, t, seq_per_block):
    assert n_seq % seq_per_block == 0
    rows = seq_per_block * t
    width = SWA_Q_HEADS * SWA_HEAD_DIM
    kcol = 2 * width // SWA_KV_WIDTH
    cache_spec = pl.BlockSpec((seq_per_block, WINDOW, SWA_KV_WIDTH), lambda i: (i, 0, 0))
    return pl.pallas_call(
        functools.partial(_attn_step_kernel, n_seq=seq_per_block, t=t),
        grid=(n_seq // seq_per_block,),
        in_specs=[pl.BlockSpec((rows, width), lambda i: (i, 0)),
                  pl.BlockSpec((rows, width), lambda i: (i, 1)),
                  pl.BlockSpec((rows, SWA_KV_WIDTH), lambda i: (i, kcol)),
                  pl.BlockSpec((rows, SWA_KV_WIDTH), lambda i: (i, kcol + 1)),
                  cache_spec, cache_spec,
                  pl.BlockSpec((1, SWA_Q_HEADS * t), lambda i: (0, 0))],
        out_specs=[pl.BlockSpec((rows, width), lambda i: (i, 0)), cache_spec, cache_spec],
        out_shape=[jax.ShapeDtypeStruct((n_seq * t, width), BF16),
                   jax.ShapeDtypeStruct(cache_k.shape, F32),
                   jax.ShapeDtypeStruct(cache_v.shape, F32)],
        compiler_params=pltpu.CompilerParams(
            dimension_semantics=("parallel",), vmem_limit_bytes=VMEM_LIMIT),
        name="attn_step",
    )(h1, h1, h1, h1, cache_k, cache_v, sink_cols)


def kernel(x_prompt, x_sample, state_hgrn, cache_swa_k, cache_swa_v, meta_tokens,
           hgrn_w_in, hgrn_lb_logits, hgrn_norm_w, hgrn_w_out,
           swa_w_in, swa_sinks, swa_w_out, ln_g, ln_b):
    out_dtype = x_prompt.dtype
    bsz, seq, d = x_prompt.shape
    dec_b, dec_t, _ = x_sample.shape
    n_p = bsz * seq
    n_s = dec_b * dec_t
    width = SWA_Q_HEADS * SWA_HEAD_DIM

    w_in0 = hgrn_w_in[0].astype(BF16)
    w_out0 = hgrn_w_out[0].astype(BF16)
    wi = swa_w_in[0]
    w_in1 = jnp.concatenate([wi[:, :width], wi[:, width + 2 * SWA_KV_WIDTH:],
                             wi[:, width:width + 2 * SWA_KV_WIDTH]], axis=1).astype(BF16)
    w_out1 = swa_w_out[0].astype(BF16)
    lb = jnp.cumsum(jax.nn.softmax(hgrn_lb_logits.astype(F32), axis=0), axis=0)[0:1]
    nw = hgrn_norm_w[0].astype(F32).reshape(1, -1)
    sinks = swa_sinks[0].astype(F32).reshape(1, SWA_Q_HEADS)
    sink_cols = jnp.repeat(sinks, dec_t, axis=1)
    g0, b0 = ln_g[0:1].astype(F32), ln_b[0:1].astype(F32)
    g1, b1 = ln_g[1:2].astype(F32), ln_b[1:2].astype(F32)

    x_p = x_prompt.astype(F32).reshape(n_p, d)
    meta_block = jnp.concatenate(
        [jnp.zeros((SEQ_BLOCK - N_META, d), F32), meta_tokens.astype(F32)], axis=0)
    x_sm = jnp.concatenate([x_sample.astype(F32).reshape(n_s, d), meta_block], axis=0)
    n_sm = n_s + SEQ_BLOCK
    meta_blk = n_s // SEQ_BLOCK

    h_sm = _matmul(x_sm, w_in0, tm=n_sm, tn=512)
    h_p = _matmul(x_p, w_in0, tm=1024, tn=1024)
    zero_state = jnp.zeros((1, HGRN_HEADS, HGRN_DK, HGRN_DV), F32)
    og_meta, s_meta = _hgrn_seq(h_sm, lb, nw, zero_state, n_seq=1, seq_len=SEQ_BLOCK, row_block0=meta_blk)
    og_s, st_s = _hgrn_step(h_sm, lb, nw, state_hgrn[0].astype(F32), n_seq=dec_b, t=dec_t, seq_per_block=64)
    og_p, st_p = _hgrn_seq(h_p, lb, nw, s_meta, n_seq=bsz, seq_len=seq, row_block0=0)
    x1_sm = _matmul_deepnorm(jnp.concatenate([og_s, og_meta], axis=0), w_out0, x_sm, g0, b0, tm=512)
    x1_p = _matmul_deepnorm(og_p, w_out0, x_p, g0, b0, tm=512)

    h1_sm = _matmul(x1_sm, w_in1, tm=n_sm, tn=512)
    h1_p = _matmul(x1_p, w_in1, tm=1024, tn=1536)
    og1_p = _attn_seq(h1_p, h1_sm, sinks.reshape(-1), n_seq=bsz, seq_len=seq, meta_row_block=meta_blk)
    ck = cache_swa_k[0].astype(F32).reshape(dec_b, WINDOW, SWA_KV_WIDTH)
    cv = cache_swa_v[0].astype(F32).reshape(dec_b, WINDOW, SWA_KV_WIDTH)
    og1_s, nk_s, nv_s = _attn_step(h1_sm, ck, cv, sink_cols, n_seq=dec_b, t=dec_t, seq_per_block=16)
    y_p = _matmul_deepnorm(og1_p, w_out1, x1_p, g1, b1, tm=512)
    y_s = _matmul_deepnorm(og1_s, w_out1, x1_sm, g1, b1, tm=512)

    kv_p = h1_p.reshape(bsz, seq, -1)[:, seq - WINDOW:, 2 * width:]
    cache_shape = (1, bsz, WINDOW, SWA_KV_HEADS, SWA_HEAD_DIM)
    return (y_p.reshape(bsz, seq, d).astype(out_dtype),
            y_s.reshape(dec_b, dec_t, d).astype(out_dtype),
            st_p[None].astype(out_dtype),
            st_s[None].astype(out_dtype),
            kv_p[..., :SWA_KV_WIDTH].reshape(cache_shape).astype(out_dtype),
            kv_p[..., SWA_KV_WIDTH:].reshape(cache_shape).astype(out_dtype),
            nk_s.reshape((1,) + cache_swa_k.shape[1:]).astype(out_dtype),
            nv_s.reshape((1,) + cache_swa_v.shape[1:]).astype(out_dtype))
```

```python
import functools

import numpy as np
import jax
import jax.numpy as jnp
from jax import lax
from jax.experimental import pallas as pl
from jax.experimental.pallas import tpu as pltpu

F32 = jnp.float32
BF16 = jnp.bfloat16

D_MODEL = 2048
N_META = 16
DEPTH = 2
HGRN_HEADS = 16
HGRN_DK = 128
HGRN_DV = 128
SWA_Q_HEADS = 32
SWA_KV_HEADS = 4
SWA_GROUP = 8
SWA_HEAD_DIM = 64
SWA_KV_WIDTH = SWA_KV_HEADS * SWA_HEAD_DIM
SWA_SCALE = SWA_HEAD_DIM ** -0.5
WINDOW = 128
DEEPNORM_ALPHA = (2.0 * DEPTH) ** 0.25
LN_EPS = 1e-5
RMS_EPS = 1e-6
LOG2E = 1.4426950408889634

CHUNK = 128
LEVELS = 7
MAX_FACTORED_LOG2_DECAY = 100.0
MASK_ROWS = 16
SEQ_BLOCK = 128
STEP_GROUP = 16
SEQ_UNROLL = 4
HGRN_SEQ_HEADS = 8
HGRN_SEQ_ROWS = 512
VMEM_LIMIT = 56 * 1024 * 1024


def _dot(a, b):
    return jnp.dot(a, b, preferred_element_type=F32)


def _dot_nt(a, b):
    return lax.dot_general(a, b, (((1,), (1,)), ((), ())), preferred_element_type=F32)


def _dot_tn(a, b):
    return lax.dot_general(a, b, (((0,), (0,)), ((), ())), preferred_element_type=F32)


def _silu(x):
    return x * jax.nn.sigmoid(x)


def _mm_kernel(x_ref, w_ref, o_ref):
    o_ref[...] = _dot(x_ref[...].astype(BF16), w_ref[...]).astype(o_ref.dtype)


def _row_tile(n, want):
    if n <= want:
        return n
    return max(t for t in range(16, want + 1, 16) if n % t == 0)


def _matmul(x, w, *, tm, tn, out_dtype=F32):
    n, k = x.shape
    e = w.shape[1]
    tm = _row_tile(n, tm)
    assert n % tm == 0 and e % tn == 0
    return pl.pallas_call(
        _mm_kernel,
        grid=(n // tm, e // tn),
        in_specs=[pl.BlockSpec((tm, k), lambda i, j: (i, 0)),
                  pl.BlockSpec((k, tn), lambda i, j: (0, j))],
        out_specs=pl.BlockSpec((tm, tn), lambda i, j: (i, j)),
        out_shape=jax.ShapeDtypeStruct((n, e), out_dtype),
        compiler_params=pltpu.CompilerParams(
            dimension_semantics=("parallel", "parallel"), vmem_limit_bytes=VMEM_LIMIT),
        name="proj_in",
    )(x, w)


def _mm_cast_kernel(xs_ref, xm_ref, w_ref, o_ref, wb_ref, xb_scr):
    @pl.when(pl.program_id(0) == 0)
    def _():
        n_s = xs_ref.shape[0]
        xb_scr[0:n_s, :] = xs_ref[...].astype(BF16)
        xb_scr[n_s:, :] = xm_ref[...].astype(BF16)

    wb = w_ref[...].astype(BF16)
    wb_ref[...] = wb
    o_ref[...] = _dot(xb_scr[...], wb)


def _matmul_cast(xs, xm, w, *, xs_rows, xm_block, tn, col_tile=lambda j: j):
    k, e = w.shape
    n = xs_rows + SEQ_BLOCK
    assert e % tn == 0
    return pl.pallas_call(
        _mm_cast_kernel,
        grid=(e // tn,),
        in_specs=[pl.BlockSpec((xs_rows, k), lambda j: (0, 0)),
                  pl.BlockSpec((SEQ_BLOCK, k), lambda j: (xm_block, 0)),
                  pl.BlockSpec((k, tn), lambda j: (0, col_tile(j)))],
        out_specs=[pl.BlockSpec((n, tn), lambda j: (0, j)),
                   pl.BlockSpec((k, tn), lambda j: (0, j))],
        out_shape=[jax.ShapeDtypeStruct((n, e), F32), jax.ShapeDtypeStruct((k, e), BF16)],
        scratch_shapes=[pltpu.VMEM((n, k), BF16)],
        compiler_params=pltpu.CompilerParams(
            dimension_semantics=("arbitrary",), vmem_limit_bytes=VMEM_LIMIT),
        name="proj_in_cast",
    )(xs, xm, w)


def _mm_ln_tail_kernel(a_ref, at_ref, w_ref, x_ref, xt_ref, g_ref, b_ref, o_ref):
    tail = pl.program_id(0) == pl.num_programs(0) - 1
    a = jnp.where(tail, at_ref[...], a_ref[...])
    x = jnp.where(tail, xt_ref[...], x_ref[...])
    z = DEEPNORM_ALPHA * x + _dot(a, w_ref[...])
    mu = jnp.mean(z, axis=-1, keepdims=True)
    zc = z - mu
    var = jnp.mean(zc * zc, axis=-1, keepdims=True)
    o_ref[...] = zc * lax.rsqrt(var + LN_EPS) * g_ref[...] + b_ref[...]


def _matmul_deepnorm_tail(a, a_tail, w, x, x_tail, g, b):
    n, k = a.shape
    d = w.shape[1]
    tm = SEQ_BLOCK
    last = n // tm - 1
    body = lambda i: (jnp.minimum(i, last), 0)
    const = lambda i: (0, 0)
    return pl.pallas_call(
        _mm_ln_tail_kernel,
        grid=(n // tm + 1,),
        in_specs=[pl.BlockSpec((tm, k), body), pl.BlockSpec((tm, k), const),
                  pl.BlockSpec((k, d), const),
                  pl.BlockSpec((tm, d), body), pl.BlockSpec((tm, d), const),
                  pl.BlockSpec((1, d), const), pl.BlockSpec((1, d), const)],
        out_specs=pl.BlockSpec((tm, d), lambda i: (i, 0)),
        out_shape=jax.ShapeDtypeStruct((n + tm, d), F32),
        compiler_params=pltpu.CompilerParams(
            dimension_semantics=("parallel",), vmem_limit_bytes=VMEM_LIMIT),
        name="proj_out_deepnorm_tail",
    )(a, a_tail, w, x, x_tail, g, b)


def _mm_ln_kernel(a_ref, w_ref, x_ref, g_ref, b_ref, o_ref):
    z = DEEPNORM_ALPHA * x_ref[...] + _dot(a_ref[...], w_ref[...])
    mu = jnp.mean(z, axis=-1, keepdims=True)
    zc = z - mu
    var = jnp.mean(zc * zc, axis=-1, keepdims=True)
    o_ref[...] = zc * lax.rsqrt(var + LN_EPS) * g_ref[...] + b_ref[...]


def _matmul_deepnorm(a, w, x, g, b, *, tm):
    n, k = a.shape
    d = w.shape[1]
    tm = _row_tile(n, tm)
    return pl.pallas_call(
        _mm_ln_kernel,
        grid=(n // tm,),
        in_specs=[pl.BlockSpec((tm, k), lambda i: (i, 0)),
                  pl.BlockSpec((k, d), lambda i: (0, 0)),
                  pl.BlockSpec((tm, d), lambda i: (i, 0)),
                  pl.BlockSpec((1, d), lambda i: (0, 0)),
                  pl.BlockSpec((1, d), lambda i: (0, 0))],
        out_specs=pl.BlockSpec((tm, d), lambda i: (i, 0)),
        out_shape=jax.ShapeDtypeStruct((n, d), F32),
        compiler_params=pltpu.CompilerParams(
            dimension_semantics=("parallel",), vmem_limit_bytes=VMEM_LIMIT),
        name="proj_out_deepnorm",
    )(a, w, x, g, b)


def _level_masks():
    t = np.arange(CHUNK)
    out = np.zeros((LEVELS + 1, CHUNK, CHUNK), np.float32)
    out[0] = t[:, None] > t[None, :]
    for l in range(LEVELS):
        h = 1 << l
        same = (t[:, None] >> (l + 1)) == (t[None, :] >> (l + 1))
        out[l + 1] = same & ((t[:, None] & h) != 0) & ((t[None, :] & h) == 0)
    return out


def _half_total(cum, level):
    rows, lanes = cum.shape
    blk = 1 << (level + 1)
    idx = (1 << level) - 1
    if blk >= 8:
        x = cum.reshape(rows // blk, blk, lanes)
        return jnp.broadcast_to(x[:, idx:idx + 1, :], x.shape).reshape(rows, lanes)
    x = cum.reshape(rows // 8, 8, lanes)
    sub = lax.broadcasted_iota(jnp.int32, x.shape, 1)
    out = None
    for start in range(8 - blk, -1, -blk):
        piece = jnp.broadcast_to(x[:, start + idx:start + idx + 1, :], x.shape)
        out = piece if out is None else jnp.where(sub < start + blk, piece, out)
    return out.reshape(rows, lanes)


def _gates(fx, lb):
    f = lb + (1.0 - lb) * jax.nn.sigmoid(fx)
    return f, jnp.log2(f), 1.0 - f


def _prefix_scan(logf):
    tiles = (CHUNK // 8, 8, HGRN_DK)
    c = logf.reshape(tiles)
    sub = lax.broadcasted_iota(jnp.int32, (1, 8, HGRN_DK), 1)
    bcast = lambda x, r: jnp.broadcast_to(x[:, r:r + 1, :], tiles)
    c = c + jnp.where((sub & 1) != 0, pltpu.roll(c, 1, 1), 0.0)
    c = c + jnp.where((sub & 2) != 0, jnp.where(sub < 4, bcast(c, 1), bcast(c, 5)), 0.0)
    c = c + jnp.where((sub & 4) != 0, bcast(c, 3), 0.0)
    c = c.reshape(CHUNK, HGRN_DK)
    for level in range(3, LEVELS):
        half = 1 << level
        pieces = []
        for r0 in range(0, CHUNK, 2 * half):
            pieces += [c[r0:r0 + half], c[r0 + half:r0 + 2 * half] + c[r0 + half - 1:r0 + half, :]]
        c = jnp.concatenate(pieces, axis=0)
    return c


def _small_levels(logf, f, q, kk):
    tiles = (CHUNK // 8, 8, HGRN_DK)
    c, f3, q3, k3 = (x.reshape(tiles) for x in (logf, f, q, kk))
    sub = lax.broadcasted_iota(jnp.int32, (1, 8, HGRN_DK), 1)
    bcast = lambda x, r: jnp.broadcast_to(x[:, r:r + 1, :], tiles)
    up = (sub & 1) != 0
    zs = [jnp.where(up, q3 * f3, k3)]
    c = c + jnp.where(up, pltpu.roll(c, 1, 1), 0.0)
    for level, tot in ((1, lambda c: jnp.where(sub < 4, bcast(c, 1), bcast(c, 5))),
                       (2, lambda c: bcast(c, 3))):
        up = (sub & (1 << level)) != 0
        t = tot(c)
        zs.append(jnp.where(up, q3, k3) * jnp.exp2(jnp.where(up, c, t - c)))
        c = c + jnp.where(up, t, 0.0)
    return [z.reshape(CHUNK, HGRN_DK) for z in zs], c.reshape(CHUNK, HGRN_DK)


def _level_large(cum, q, kk, level):
    half = 1 << level
    args, bases, cums = [], [], []
    for r0 in range(0, CHUNK, 2 * half):
        lo, up = slice(r0, r0 + half), slice(r0 + half, r0 + 2 * half)
        tot = cum[r0 + half - 1:r0 + half, :]
        args += [tot - cum[lo], cum[up]]
        bases += [kk[lo], q[up]]
        cums += [cum[lo], cum[up] + tot]
    x = jnp.exp2(jnp.concatenate(args, axis=0))
    return jnp.concatenate(bases, axis=0) * x, jnp.concatenate(cums, axis=0)


def _split3(x):
    hi = x.astype(BF16)
    r = x - hi.astype(F32)
    mid = r.astype(BF16)
    lo = (r - mid.astype(F32)).astype(BF16)
    return hi, mid, lo


def _state_update(s_old, kd, v, blast, pad_rows):
    c = kd.shape[0]
    hi, mid, lo = _split3(jnp.exp2(blast))
    row = lax.broadcasted_iota(jnp.int32, (pad_rows, HGRN_DK), 0)
    dec = jnp.where(row == 0, hi.astype(F32),
                    jnp.where(row == 1, mid.astype(F32), jnp.where(row == 2, lo.astype(F32), 0.0)))
    lhs = jnp.concatenate([kd, dec], axis=0).astype(BF16)
    rhs = jnp.concatenate(
        [jnp.concatenate([v, jnp.zeros((c, HGRN_DV), F32)], axis=1),
         jnp.concatenate([jnp.zeros((pad_rows, HGRN_DV), F32), jnp.ones((pad_rows, HGRN_DV), F32)], axis=1)],
        axis=0).astype(BF16)
    both = _dot_tn(lhs, rhs)
    return both[:, HGRN_DV:] * s_old + both[:, :HGRN_DV]


def _norm_gate(o, g, nw):
    o = o * lax.rsqrt(jnp.mean(o * o, axis=-1, keepdims=True) + RMS_EPS) * nw
    return o * _silu(g)


def _hgrn_seq_kernel(q_ref, fx_ref, i_ref, g_ref, lb_ref, nw_ref, s0_ref, m_ref,
                     og_ref, sout_ref, s_scr, o_scr, *, n_chunks):
    heads = s_scr.shape[0]

    @pl.when(pl.program_id(2) == 0)
    def _():
        s_scr[...] = s0_ref[0]

    def chunk(c, carry):
        rows = pl.ds(pl.multiple_of(c * CHUNK, CHUNK), CHUNK)
        hcols = [slice(hd * HGRN_DK, (hd + 1) * HGRN_DK) for hd in range(heads)]
        q = [q_ref[rows, cs] for cs in hcols]
        v = [i_ref[rows, cs] for cs in hcols]
        gates = [_gates(fx_ref[rows, cs], lb_ref[:, cs]) for cs in hcols]
        kk = [gt[2] for gt in gates]
        cum = [_prefix_scan(gt[1]) for gt in gates]
        mid = CHUNK // 2 - 1
        spread = None
        for b in cum:
            s_hd = jnp.maximum(b[0:1, :] - b[mid:mid + 1, :], b[mid:mid + 1, :] - b[CHUNK - 1:CHUNK, :])
            spread = s_hd if spread is None else jnp.maximum(spread, s_hd)
        factorable = jnp.max(spread) <= MAX_FACTORED_LOG2_DECAY

        vb = [x.astype(BF16) for x in v]
        for hd in range(heads):
            d = cum[hd] - cum[hd][mid:mid + 1, :]
            qe = (q[hd] * jnp.exp2(d)).astype(BF16)
            ke = (kk[hd] * jnp.exp2(-d)).astype(BF16)
            amat = (_dot_nt(qe, ke) * m_ref[0]).astype(BF16)
            blast = cum[hd][CHUNK - 1:CHUNK, :]
            qd = (q[hd] * jnp.exp2(cum[hd])).astype(BF16)
            kd = kk[hd] * jnp.exp2(blast - cum[hd])
            s_old = s_scr[hd]
            base = _dot(qd, s_old.astype(BF16)) + jnp.sum(q[hd] * kk[hd], axis=-1, keepdims=True) * v[hd]
            o_scr[hd] = base
            s_scr[hd] = _state_update(s_old, kd, v[hd], blast, 16)
            og_ref[rows, hcols[hd]] = _norm_gate(
                base + _dot(amat, vb[hd]), g_ref[rows, hcols[hd]], nw_ref[:, hcols[hd]]).astype(og_ref.dtype)

        @pl.when(jnp.logical_not(factorable))
        def _():
            small = [_small_levels(gt[1], gt[0], q[hd], kk[hd]) for hd, gt in enumerate(gates)]
            zs = [sm[0] for sm in small]
            part = [sm[1] for sm in small]
            for level in range(3, LEVELS):
                for hd in range(heads):
                    z, part[hd] = _level_large(part[hd], q[hd], kk[hd], level)
                    zs[hd].append(z)
            nblk = CHUNK // MASK_ROWS
            a = [[None] * nblk for _ in range(heads)]
            for level in range(LEVELS):
                step = 1 << (level - 4) if level >= 4 else 0
                blocks = [i for i in range(nblk) if level < 4 or (i & step)]
                for hd in range(heads):
                    z = zs[hd][level].astype(BF16)
                    lhs = z if level < 4 else jnp.concatenate(
                        [z[i * MASK_ROWS:(i + 1) * MASK_ROWS] for i in blocks], axis=0)
                    term = _dot_nt(lhs, z)
                    for n, i in enumerate(blocks):
                        t = (term[n * MASK_ROWS:(n + 1) * MASK_ROWS]
                             * m_ref[level + 1, i * MASK_ROWS:(i + 1) * MASK_ROWS, :])
                        a[hd][i] = t if a[hd][i] is None else a[hd][i] + t
            for hd in range(heads):
                amat = jnp.concatenate(a[hd], axis=0).astype(BF16)
                og_ref[rows, hcols[hd]] = _norm_gate(
                    o_scr[hd] + _dot(amat, vb[hd]),
                    g_ref[rows, hcols[hd]], nw_ref[:, hcols[hd]]).astype(og_ref.dtype)
        return carry

    lax.fori_loop(0, n_chunks, chunk, 0)

    @pl.when(pl.program_id(2) == pl.num_programs(2) - 1)
    def _():
        sout_ref[0] = s_scr[...]


def _hgrn_seq(h, lb, nw, s0, *, n_seq, seq_len, row_block0):
    hb = HGRN_SEQ_HEADS
    width = hb * HGRN_DK
    groups = HGRN_HEADS // hb
    rb = min(HGRN_SEQ_ROWS, seq_len)
    nrb = seq_len // rb
    assert seq_len % rb == 0 and rb % CHUNK == 0 and (row_block0 * SEQ_BLOCK) % rb == 0
    rb0 = row_block0 * SEQ_BLOCK // rb
    s0_batched = s0.shape[0] != 1
    col = lambda part: (lambda b, hg, r: (rb0 + b * nrb + r, part * groups + hg))
    masks = jnp.asarray(_level_masks())
    return pl.pallas_call(
        functools.partial(_hgrn_seq_kernel, n_chunks=rb // CHUNK),
        grid=(n_seq, groups, nrb),
        in_specs=[pl.BlockSpec((rb, width), col(0)),
                  pl.BlockSpec((rb, width), col(1)),
                  pl.BlockSpec((rb, width), col(2)),
                  pl.BlockSpec((rb, width), col(3)),
                  pl.BlockSpec((1, width), lambda b, hg, r: (0, hg)),
                  pl.BlockSpec((1, width), lambda b, hg, r: (0, hg)),
                  pl.BlockSpec((1, hb, HGRN_DK, HGRN_DV),
                               (lambda b, hg, r: (b, hg, 0, 0)) if s0_batched else (lambda b, hg, r: (0, hg, 0, 0))),
                  pl.BlockSpec((LEVELS + 1, CHUNK, CHUNK), lambda b, hg, r: (0, 0, 0))],
        out_specs=[pl.BlockSpec((rb, width), lambda b, hg, r: (b * nrb + r, hg)),
                   pl.BlockSpec((1, hb, HGRN_DK, HGRN_DV), lambda b, hg, r: (b, hg, 0, 0))],
        out_shape=[jax.ShapeDtypeStruct((n_seq * seq_len, HGRN_HEADS * HGRN_DV), BF16),
                   jax.ShapeDtypeStruct((n_seq, HGRN_HEADS, HGRN_DK, HGRN_DV), F32)],
        scratch_shapes=[pltpu.VMEM((hb, HGRN_DK, HGRN_DV), F32),
                        pltpu.VMEM((hb, CHUNK, HGRN_DV), F32)],
        compiler_params=pltpu.CompilerParams(
            dimension_semantics=("parallel", "parallel", "arbitrary"), vmem_limit_bytes=VMEM_LIMIT),
        name="hgrn_seq",
    )(h, h, h, h, lb, nw, s0, masks)


def _hgrn_step_kernel(q_ref, fx_ref, i_ref, g_ref, lb_ref, nw_ref, s_ref, og_ref, sout_ref, *, n_seq, t):
    lb = lb_ref[...]
    nw = nw_ref[...]
    tile = (STEP_GROUP, t, HGRN_DK)
    sub = lax.broadcasted_iota(jnp.int32, (1, t, HGRN_DK), 1)

    def group(i, carry):
        rows = pl.ds(pl.multiple_of(i * (STEP_GROUP * t), STEP_GROUP * t), STEP_GROUP * t)
        q = q_ref[rows, :].reshape(tile)
        v = i_ref[rows, :].reshape(tile)
        _, logf, kk = _gates(fx_ref[rows, :].reshape(tile), lb)
        cum = logf
        shift = 1
        while shift < t:
            cum = cum + jnp.where(sub >= shift, pltpu.roll(cum, shift, 1), 0.0)
            shift *= 2
        intra = (jnp.sum(q * kk, axis=-1, keepdims=True)) * v
        for d in range(1, t):
            valid = sub >= d
            x = jnp.exp2(jnp.where(valid, cum - pltpu.roll(cum, d, 1), 0.0))
            w = jnp.sum(jnp.where(valid, q * x * pltpu.roll(kk, d, 1), 0.0), axis=-1, keepdims=True)
            intra = intra + w * pltpu.roll(v, d, 1)
        blast = cum[:, t - 1:t, :]
        qd = q * jnp.exp2(cum)
        kd = kk * jnp.exp2(blast - cum)
        seqs = [i * STEP_GROUP + n for n in range(STEP_GROUP)]
        inter = [_dot(qd[n].astype(BF16), s_ref[b, 0].astype(BF16)) for n, b in enumerate(seqs)]
        for n, b in enumerate(seqs):
            sout_ref[b, 0] = _state_update(s_ref[b, 0], kd[n], v[n], blast[n], 8)
        o = jnp.stack(inter, axis=0) + intra
        og = _norm_gate(o, g_ref[rows, :].reshape(tile), nw)
        og_ref[rows, :] = og.reshape(STEP_GROUP * t, HGRN_DK).astype(og_ref.dtype)
        return carry

    lax.fori_loop(0, n_seq // STEP_GROUP, group, 0)


def _hgrn_step(h, lb, nw, s0, *, n_seq, t, seq_per_block):
    seq_per_block = min(seq_per_block, n_seq)
    assert n_seq % seq_per_block == 0 and seq_per_block % STEP_GROUP == 0 and t == 8
    rows = seq_per_block * t
    col = lambda part: (lambda i, hd: (i, part * HGRN_HEADS + hd))
    return pl.pallas_call(
        functools.partial(_hgrn_step_kernel, n_seq=seq_per_block, t=t),
        grid=(n_seq // seq_per_block, HGRN_HEADS),
        in_specs=[pl.BlockSpec((rows, HGRN_DK), col(0)),
                  pl.BlockSpec((rows, HGRN_DK), col(1)),
                  pl.BlockSpec((rows, HGRN_DV), col(2)),
                  pl.BlockSpec((rows, HGRN_DV), col(3)),
                  pl.BlockSpec((1, HGRN_DK), lambda i, hd: (0, hd)),
                  pl.BlockSpec((1, HGRN_DV), lambda i, hd: (0, hd)),
                  pl.BlockSpec((seq_per_block, 1, HGRN_DK, HGRN_DV), lambda i, hd: (i, hd, 0, 0))],
        out_specs=[pl.BlockSpec((rows, HGRN_DV), lambda i, hd: (i, hd)),
                   pl.BlockSpec((seq_per_block, 1, HGRN_DK, HGRN_DV), lambda i, hd: (i, hd, 0, 0))],
        out_shape=[jax.ShapeDtypeStruct((n_seq * t, HGRN_HEADS * HGRN_DV), BF16),
                   jax.ShapeDtypeStruct((n_seq, HGRN_HEADS, HGRN_DK, HGRN_DV), F32)],
        compiler_params=pltpu.CompilerParams(
            dimension_semantics=("parallel", "parallel"), vmem_limit_bytes=VMEM_LIMIT),
        name="hgrn_step",
    )(h, h, h, h, lb, nw, s0)


def _attn_seq_kernel(q_ref, g_ref, kc_ref, vc_ref, kp_ref, vp_ref, meta_ref, sink_ref, og_ref):
    first = pl.program_id(1) == 0
    kprev = jnp.where(first, meta_ref[:, :SWA_KV_WIDTH], kp_ref[...])
    vprev = jnp.where(first, meta_ref[:, SWA_KV_WIDTH:], vp_ref[...])
    kband = (jnp.concatenate([kprev, kc_ref[...]], axis=0) * (SWA_SCALE * LOG2E)).astype(BF16)
    vband = jnp.concatenate([vprev, vc_ref[...]], axis=0).astype(BF16)
    kj = lax.broadcasted_iota(jnp.int32, (2 * SEQ_BLOCK, SEQ_BLOCK), 0)
    qi = lax.broadcasted_iota(jnp.int32, (2 * SEQ_BLOCK, SEQ_BLOCK), 1)
    dist = SEQ_BLOCK + qi - kj
    valid = (dist >= 0) & (dist < WINDOW) & (jnp.logical_not(first) | (kj >= SEQ_BLOCK - N_META))
    madd = jnp.where(valid, 0.0, -jnp.inf)
    zeros = jnp.zeros((2 * SEQ_BLOCK, SWA_HEAD_DIM), BF16)
    pairs = SWA_GROUP // 2
    pw = 2 * SWA_HEAD_DIM

    def block_diag(x):
        return jnp.concatenate([jnp.concatenate([x, zeros], axis=1),
                                jnp.concatenate([zeros, x], axis=1)], axis=0)

    sts = []
    for kvh in range(SWA_KV_HEADS):
        ks = slice(kvh * SWA_HEAD_DIM, (kvh + 1) * SWA_HEAD_DIM)
        col0 = kvh * SWA_GROUP * SWA_HEAD_DIM
        xq = jnp.concatenate([q_ref[:, col0 + p * pw:col0 + (p + 1) * pw].astype(BF16)
                              for p in range(pairs)], axis=0)
        sts.append(_dot_nt(block_diag(kband[:, ks]), xq))
    for kvh in range(SWA_KV_HEADS):
        ks = slice(kvh * SWA_HEAD_DIM, (kvh + 1) * SWA_HEAD_DIM)
        col0 = kvh * SWA_GROUP * SWA_HEAD_DIM
        v2 = block_diag(vband[:, ks])
        pns = []
        for pr in range(pairs):
            head = kvh * SWA_GROUP + 2 * pr
            s2 = (sts[kvh][:, pr * SEQ_BLOCK:(pr + 1) * SEQ_BLOCK].reshape(2, 2 * SEQ_BLOCK, SEQ_BLOCK)
                  + madd[None])
            sink = jnp.concatenate([jnp.full((1, 1, SEQ_BLOCK), sink_ref[head] * LOG2E, F32),
                                    jnp.full((1, 1, SEQ_BLOCK), sink_ref[head + 1] * LOG2E, F32)], axis=0)
            m = jnp.maximum(jnp.max(s2, axis=1, keepdims=True), sink)
            p = jnp.exp2(s2 - m)
            den = jnp.sum(p, axis=1, keepdims=True) + jnp.exp2(sink - m)
            pns.append((p * (1.0 / den)).astype(BF16).reshape(4 * SEQ_BLOCK, SEQ_BLOCK))
        for pr in range(pairs):
            o = _dot_tn(pns[pr], v2)
            cs = slice(col0 + pr * pw, col0 + (pr + 1) * pw)
            og_ref[:, cs] = (o * _silu(g_ref[:, cs])).astype(og_ref.dtype)


def _attn_seq(h1, h1_meta, sinks, *, n_seq, seq_len, meta_row_block):
    nb = seq_len // SEQ_BLOCK
    width = SWA_Q_HEADS * SWA_HEAD_DIM
    kcol = 2 * width // SWA_KV_WIDTH
    return pl.pallas_call(
        _attn_seq_kernel,
        grid=(n_seq, nb),
        in_specs=[pl.BlockSpec((SEQ_BLOCK, width), lambda b, j: (b * nb + j, 0)),
                  pl.BlockSpec((SEQ_BLOCK, width), lambda b, j: (b * nb + j, 1)),
                  pl.BlockSpec((SEQ_BLOCK, SWA_KV_WIDTH), lambda b, j: (b * nb + j, kcol)),
                  pl.BlockSpec((SEQ_BLOCK, SWA_KV_WIDTH), lambda b, j: (b * nb + j, kcol + 1)),
                  pl.BlockSpec((SEQ_BLOCK, SWA_KV_WIDTH), lambda b, j: (b * nb + jnp.maximum(j - 1, 0), kcol)),
                  pl.BlockSpec((SEQ_BLOCK, SWA_KV_WIDTH), lambda b, j: (b * nb + jnp.maximum(j - 1, 0), kcol + 1)),
                  pl.BlockSpec((SEQ_BLOCK, 2 * SWA_KV_WIDTH), lambda b, j: (meta_row_block, kcol // 2)),
                  pl.BlockSpec(memory_space=pltpu.SMEM)],
        out_specs=pl.BlockSpec((SEQ_BLOCK, width), lambda b, j: (b * nb + j, 0)),
        out_shape=jax.ShapeDtypeStruct((n_seq * seq_len, width), BF16),
        compiler_params=pltpu.CompilerParams(
            dimension_semantics=("parallel", "arbitrary"), vmem_limit_bytes=VMEM_LIMIT),
        name="attn_seq",
    )(h1, h1, h1, h1, h1, h1, h1_meta, sinks)


def _attn_step_kernel(q_ref, g_ref, kn_ref, vn_ref, ck_ref, cv_ref, sink_ref, og_ref, nk_ref, nv_ref, *, n_seq, t):
    keys = WINDOW + t
    hd = SWA_HEAD_DIM
    tiles = SWA_Q_HEADS // 2
    kj = lax.broadcasted_iota(jnp.int32, (keys, SWA_Q_HEADS * t), 0)
    qt = lax.broadcasted_iota(jnp.int32, (keys, SWA_Q_HEADS * t), 1) % t
    madd = jnp.where((kj >= qt + 1) & (kj <= WINDOW + qt), 0.0, -jnp.inf)
    low = lax.broadcasted_iota(jnp.int32, (t, 2 * hd), 1) < hd
    zero_tile = jnp.zeros((t, 2 * hd), F32)
    sink = sink_ref[...] * LOG2E

    def group(i, carry):
        seqs = [i * SEQ_UNROLL + n for n in range(SEQ_UNROLL)]
        rows = [pl.ds(pl.multiple_of(b * t, t), t) for b in seqs]
        st, vall = [], []
        for b, rw in zip(seqs, rows):
            kc, vc = ck_ref[b], cv_ref[b]
            kn, vn = kn_ref[rw, :], vn_ref[rw, :]
            nk_ref[b, 0:WINDOW - t, :] = kc[t:, :]
            nk_ref[b, WINDOW - t:WINDOW, :] = kn
            nv_ref[b, 0:WINDOW - t, :] = vc[t:, :]
            nv_ref[b, WINDOW - t:WINDOW, :] = vn
            kall = jnp.concatenate([kc, kn], axis=0).astype(BF16)
            vall.append(jnp.concatenate([vc, vn], axis=0).astype(BF16))
            q = q_ref[rw, :] * (SWA_SCALE * LOG2E)
            qtile = [q[:, j * 2 * hd:(j + 1) * 2 * hd] for j in range(tiles)]
            qswap = [pltpu.roll(x, hd, 1) for x in qtile]
            groups = []
            for kvh in range(SWA_KV_HEADS):
                for gq in range(SWA_GROUP):
                    j = (kvh * SWA_GROUP + gq) // 2
                    src = qtile[j] if gq % 2 == kvh % 2 else qswap[j]
                    half = jnp.where(low, src, 0.0) if kvh % 2 == 0 else jnp.where(low, 0.0, src)
                    groups.append(jnp.concatenate(
                        [half if c == kvh // 2 else zero_tile for c in range(SWA_KV_HEADS // 2)], axis=1))
            qbd = jnp.concatenate(groups, axis=0).astype(BF16)
            st.append(_dot_nt(kall, qbd))
        pn = []
        for s in st:
            s = s + madd
            m = jnp.maximum(jnp.max(s, axis=0, keepdims=True), sink)
            p = jnp.exp2(s - m)
            den = jnp.sum(p, axis=0, keepdims=True) + jnp.exp2(sink - m)
            pn.append((p * (1.0 / den)).astype(BF16))
        of = [_dot_tn(p, vl) for p, vl in zip(pn, vall)]
        for o, rw in zip(of, rows):
            g = g_ref[rw, :]
            out = []
            for j in range(tiles):
                kvh = (2 * j) // SWA_GROUP
                ct = slice((kvh // 2) * 2 * hd, (kvh // 2 + 1) * 2 * hd)
                ra = o[(2 * j) * t:(2 * j + 1) * t, ct]
                rb = o[(2 * j + 1) * t:(2 * j + 2) * t, ct]
                if kvh % 2 == 0:
                    out.append(jnp.where(low, ra, pltpu.roll(rb, hd, 1)))
                else:
                    out.append(jnp.where(low, pltpu.roll(ra, hd, 1), rb))
            og_ref[rw, :] = (jnp.concatenate(out, axis=1) * _silu(g)).astype(og_ref.dtype)
        return carry

    lax.fori_loop(0, n_seq // SEQ_UNROLL, group, 0)


def _attn_step(h1, cache_k, cache_v, sink_cols, *, n_seq, t, seq_per_block):
    assert n_seq % seq_per_block == 0
    rows = seq_per_block * t
    width = SWA_Q_HEADS * SWA_HEAD_DIM
    kcol = 2 * width // SWA_KV_WIDTH
    cache_spec = pl.BlockSpec((seq_per_block, WINDOW, SWA_KV_WIDTH), lambda i: (i, 0, 0))
    return pl.pallas_call(
        functools.partial(_attn_step_kernel, n_seq=seq_per_block, t=t),
        grid=(n_seq // seq_per_block,),
        in_specs=[pl.BlockSpec((rows, width), lambda i: (i, 0)),
                  pl.BlockSpec((rows, width), lambda i: (i, 1)),
                  pl.BlockSpec((rows, SWA_KV_WIDTH), lambda i: (i, kcol)),
                  pl.BlockSpec((rows, SWA_KV_WIDTH), lambda i: (i, kcol + 1)),
                  cache_spec, cache_spec,
                  pl.BlockSpec((1, SWA_Q_HEADS * t), lambda i: (0, 0))],
        out_specs=[pl.BlockSpec((rows, width), lambda i: (i, 0)), cache_spec, cache_spec],
        out_shape=[jax.ShapeDtypeStruct((n_seq * t, width), BF16),
                   jax.ShapeDtypeStruct(cache_k.shape, F32),
                   jax.ShapeDtypeStruct(cache_v.shape, F32)],
        compiler_params=pltpu.CompilerParams(
            dimension_semantics=("parallel",), vmem_limit_bytes=VMEM_LIMIT),
        name="attn_step",
    )(h1, h1, h1, h1, cache_k, cache_v, sink_cols)


def kernel(x_prompt, x_sample, state_hgrn, cache_swa_k, cache_swa_v, meta_tokens,
           hgrn_w_in, hgrn_lb_logits, hgrn_norm_w, hgrn_w_out,
           swa_w_in, swa_sinks, swa_w_out, ln_g, ln_b):
    out_dtype = x_prompt.dtype
    bsz, seq, d = x_prompt.shape
    dec_b, dec_t, _ = x_sample.shape
    n_p = bsz * seq
    n_s = dec_b * dec_t
    width = SWA_Q_HEADS * SWA_HEAD_DIM

    w_out0 = hgrn_w_out[0].astype(BF16)
    w_out1 = swa_w_out[0].astype(BF16)
    in_tile = 512
    q_tiles = width // in_tile
    kv_tile = 2 * SWA_KV_WIDTH // in_tile
    assert width % in_tile == 0 and (2 * SWA_KV_WIDTH) % in_tile == 0 and kv_tile == 1
    regroup = lambda j: jnp.where(j < q_tiles, j, jnp.where(j < 2 * q_tiles, j + kv_tile, q_tiles))
    lb = jnp.cumsum(jax.nn.softmax(hgrn_lb_logits.astype(F32), axis=0), axis=0)[0:1]
    nw = hgrn_norm_w[0].astype(F32).reshape(1, -1)
    sinks = swa_sinks[0].astype(F32).reshape(1, SWA_Q_HEADS)
    sink_cols = jnp.repeat(sinks, dec_t, axis=1)
    g0, b0 = ln_g[0:1].astype(F32), ln_b[0:1].astype(F32)
    g1, b1 = ln_g[1:2].astype(F32), ln_b[1:2].astype(F32)

    x_p = x_prompt.astype(F32).reshape(n_p, d)
    meta_block = jnp.concatenate(
        [jnp.zeros((SEQ_BLOCK - N_META, d), F32), meta_tokens.astype(F32)], axis=0)
    x_s = x_sample.astype(F32).reshape(n_s, d)
    meta_blk = n_s // SEQ_BLOCK

    h_sm, w_in0 = _matmul_cast(x_s, meta_block, hgrn_w_in[0].astype(F32), xs_rows=n_s, xm_block=0, tn=in_tile)
    h_p = _matmul(x_p, w_in0, tm=1024, tn=1024)
    zero_state = jnp.zeros((1, HGRN_HEADS, HGRN_DK, HGRN_DV), F32)
    og_meta, s_meta = _hgrn_seq(h_sm, lb, nw, zero_state, n_seq=1, seq_len=SEQ_BLOCK, row_block0=meta_blk)
    og_s, st_s = _hgrn_step(h_sm, lb, nw, state_hgrn[0].astype(F32), n_seq=dec_b, t=dec_t, seq_per_block=64)
    og_p, st_p = _hgrn_seq(h_p, lb, nw, s_meta, n_seq=bsz, seq_len=seq, row_block0=0)
    x1_sm = _matmul_deepnorm_tail(og_s, og_meta, w_out0, x_s, meta_block, g0, b0)
    x1_p = _matmul_deepnorm(og_p, w_out0, x_p, g0, b0, tm=512)

    h1_sm, w_in1 = _matmul_cast(x1_sm, x1_sm, swa_w_in[0].astype(F32), xs_rows=n_s, xm_block=meta_blk,
                                tn=in_tile, col_tile=regroup)
    h1_p = _matmul(x1_p, w_in1, tm=1024, tn=1536)
    og1_p = _attn_seq(h1_p, h1_sm, sinks.reshape(-1), n_seq=bsz, seq_len=seq, meta_row_block=meta_blk)
    ck = cache_swa_k[0].astype(F32).reshape(dec_b, WINDOW, SWA_KV_WIDTH)
    cv = cache_swa_v[0].astype(F32).reshape(dec_b, WINDOW, SWA_KV_WIDTH)
    og1_s, nk_s, nv_s = _attn_step(h1_sm, ck, cv, sink_cols, n_seq=dec_b, t=dec_t, seq_per_block=16)
    y_p = _matmul_deepnorm(og1_p, w_out1, x1_p, g1, b1, tm=512)
    y_s = _matmul_deepnorm(og1_s, w_out1, x1_sm, g1, b1, tm=512)

    kv_p = h1_p.reshape(bsz, seq, -1)[:, seq - WINDOW:, 2 * width:]
    cache_shape = (1, bsz, WINDOW, SWA_KV_HEADS, SWA_HEAD_DIM)
    return (y_p.reshape(bsz, seq, d).astype(out_dtype),
            y_s.reshape(dec_b, dec_t, d).astype(out_dtype),
            st_p[None].astype(out_dtype),
            st_s[None].astype(out_dtype),
            kv_p[..., :SWA_KV_WIDTH].reshape(cache_shape).astype(out_dtype),
            kv_p[..., SWA_KV_WIDTH:].reshape(cache_shape).astype(out_dtype),
            nk_s.reshape((1,) + cache_swa_k.shape[1:]).astype(out_dtype),
            nv_s.reshape((1,) + cache_swa_v.shape[1:]).astype(out_dtype))
```

```python
import functools

import numpy as np
import jax
import jax.numpy as jnp
from jax import lax
from jax.experimental import pallas as pl
from jax.experimental.pallas import tpu as pltpu

F32 = jnp.float32
BF16 = jnp.bfloat16

D_MODEL = 2048
N_META = 16
DEPTH = 2
HGRN_HEADS = 16
HGRN_DK = 128
HGRN_DV = 128
SWA_Q_HEADS = 32
SWA_KV_HEADS = 4
SWA_GROUP = 8
SWA_HEAD_DIM = 64
SWA_KV_WIDTH = SWA_KV_HEADS * SWA_HEAD_DIM
SWA_SCALE = SWA_HEAD_DIM ** -0.5
WINDOW = 128
DEEPNORM_ALPHA = (2.0 * DEPTH) ** 0.25
LN_EPS = 1e-5
RMS_EPS = 1e-6
LOG2E = 1.4426950408889634

CHUNK = 128
LEVELS = 7
MAX_FACTORED_LOG2_DECAY = 100.0
MASK_ROWS = 16
SEQ_BLOCK = 128
LN_SUB_ROWS = 128
STEP_GROUP = 16
SEQ_UNROLL = 4
HGRN_SEQ_HEADS = 8
HGRN_SEQ_ROWS = 512
VMEM_LIMIT = 56 * 1024 * 1024


def _dot(a, b):
    return jnp.dot(a, b, preferred_element_type=F32)


def _dot_nt(a, b):
    return lax.dot_general(a, b, (((1,), (1,)), ((), ())), preferred_element_type=F32)


def _dot_tn(a, b):
    return lax.dot_general(a, b, (((0,), (0,)), ((), ())), preferred_element_type=F32)


def _silu(x):
    return x * jax.nn.sigmoid(x)


def _mm_kernel(x_ref, w_ref, o_ref):
    o_ref[...] = _dot(x_ref[...].astype(BF16), w_ref[...]).astype(o_ref.dtype)


def _row_tile(n, want):
    if n <= want:
        return n
    return max(t for t in range(16, want + 1, 16) if n % t == 0)


def _matmul(x, w, *, tm, tn, out_dtype=F32):
    n, k = x.shape
    e = w.shape[1]
    tm = _row_tile(n, tm)
    assert n % tm == 0 and e % tn == 0
    return pl.pallas_call(
        _mm_kernel,
        grid=(n // tm, e // tn),
        in_specs=[pl.BlockSpec((tm, k), lambda i, j: (i, 0)),
                  pl.BlockSpec((k, tn), lambda i, j: (0, j))],
        out_specs=pl.BlockSpec((tm, tn), lambda i, j: (i, j)),
        out_shape=jax.ShapeDtypeStruct((n, e), out_dtype),
        compiler_params=pltpu.CompilerParams(
            dimension_semantics=("parallel", "parallel"), vmem_limit_bytes=VMEM_LIMIT),
        name="proj_in",
    )(x, w)


def _mm_cast_kernel(xs_ref, xm_ref, w_ref, o_ref, wb_ref, xb_scr):
    @pl.when(pl.program_id(0) == 0)
    def _():
        n_s = xs_ref.shape[0]
        xb_scr[0:n_s, :] = xs_ref[...].astype(BF16)
        xb_scr[n_s:, :] = xm_ref[...].astype(BF16)

    wb = w_ref[...].astype(BF16)
    wb_ref[...] = wb
    o_ref[...] = _dot(xb_scr[...], wb)


def _matmul_cast(xs, xm, w, *, xs_rows, xm_block, tn, col_tile=lambda j: j):
    k, e = w.shape
    n = xs_rows + SEQ_BLOCK
    assert e % tn == 0
    return pl.pallas_call(
        _mm_cast_kernel,
        grid=(e // tn,),
        in_specs=[pl.BlockSpec((xs_rows, k), lambda j: (0, 0)),
                  pl.BlockSpec((SEQ_BLOCK, k), lambda j: (xm_block, 0)),
                  pl.BlockSpec((k, tn), lambda j: (0, col_tile(j)))],
        out_specs=[pl.BlockSpec((n, tn), lambda j: (0, j)),
                   pl.BlockSpec((k, tn), lambda j: (0, j))],
        out_shape=[jax.ShapeDtypeStruct((n, e), F32), jax.ShapeDtypeStruct((k, e), BF16)],
        scratch_shapes=[pltpu.VMEM((n, k), BF16)],
        compiler_params=pltpu.CompilerParams(
            dimension_semantics=("arbitrary",), vmem_limit_bytes=VMEM_LIMIT),
        name="proj_in_cast",
    )(xs, xm, w)


def _mm_ln_tail_kernel(a_ref, at_ref, w_ref, x_ref, xt_ref, g_ref, b_ref, o_ref):
    tail = pl.program_id(0) == pl.num_programs(0) - 1
    a = jnp.where(tail, at_ref[...], a_ref[...])
    x = jnp.where(tail, xt_ref[...], x_ref[...])
    z = DEEPNORM_ALPHA * x + _dot(a, w_ref[...])
    mu = jnp.mean(z, axis=-1, keepdims=True)
    zc = z - mu
    var = jnp.mean(zc * zc, axis=-1, keepdims=True)
    o_ref[...] = zc * lax.rsqrt(var + LN_EPS) * g_ref[...] + b_ref[...]


def _matmul_deepnorm_tail(a, a_tail, w, x, x_tail, g, b):
    n, k = a.shape
    d = w.shape[1]
    tm = SEQ_BLOCK
    last = n // tm - 1
    body = lambda i: (jnp.minimum(i, last), 0)
    const = lambda i: (0, 0)
    return pl.pallas_call(
        _mm_ln_tail_kernel,
        grid=(n // tm + 1,),
        in_specs=[pl.BlockSpec((tm, k), body), pl.BlockSpec((tm, k), const),
                  pl.BlockSpec((k, d), const),
                  pl.BlockSpec((tm, d), body), pl.BlockSpec((tm, d), const),
                  pl.BlockSpec((1, d), const), pl.BlockSpec((1, d), const)],
        out_specs=pl.BlockSpec((tm, d), lambda i: (i, 0)),
        out_shape=jax.ShapeDtypeStruct((n + tm, d), F32),
        compiler_params=pltpu.CompilerParams(
            dimension_semantics=("parallel",), vmem_limit_bytes=VMEM_LIMIT),
        name="proj_out_deepnorm_tail",
    )(a, a_tail, w, x, x_tail, g, b)


def _mm_ln_kernel(a_ref, w_ref, x_ref, g_ref, b_ref, o_ref):
    tm = a_ref.shape[0]
    sub = LN_SUB_ROWS if tm % LN_SUB_ROWS == 0 else tm
    for r0 in range(0, tm, sub):
        rows = slice(r0, r0 + sub)
        z = DEEPNORM_ALPHA * x_ref[rows, :] + _dot(a_ref[rows, :], w_ref[...])
        mu = jnp.mean(z, axis=-1, keepdims=True)
        zc = z - mu
        var = jnp.mean(zc * zc, axis=-1, keepdims=True)
        o_ref[rows, :] = zc * lax.rsqrt(var + LN_EPS) * g_ref[...] + b_ref[...]


def _matmul_deepnorm(a, w, x, g, b, *, tm):
    n, k = a.shape
    d = w.shape[1]
    tm = _row_tile(n, tm)
    return pl.pallas_call(
        _mm_ln_kernel,
        grid=(n // tm,),
        in_specs=[pl.BlockSpec((tm, k), lambda i: (i, 0)),
                  pl.BlockSpec((k, d), lambda i: (0, 0)),
                  pl.BlockSpec((tm, d), lambda i: (i, 0)),
                  pl.BlockSpec((1, d), lambda i: (0, 0)),
                  pl.BlockSpec((1, d), lambda i: (0, 0))],
        out_specs=pl.BlockSpec((tm, d), lambda i: (i, 0)),
        out_shape=jax.ShapeDtypeStruct((n, d), F32),
        compiler_params=pltpu.CompilerParams(
            dimension_semantics=("parallel",), vmem_limit_bytes=VMEM_LIMIT),
        name="proj_out_deepnorm",
    )(a, w, x, g, b)


def _level_masks():
    t = np.arange(CHUNK)
    out = np.zeros((LEVELS + 1, CHUNK, CHUNK), np.float32)
    out[0] = t[:, None] > t[None, :]
    for l in range(LEVELS):
        h = 1 << l
        same = (t[:, None] >> (l + 1)) == (t[None, :] >> (l + 1))
        out[l + 1] = same & ((t[:, None] & h) != 0) & ((t[None, :] & h) == 0)
    return out


def _half_total(cum, level):
    rows, lanes = cum.shape
    blk = 1 << (level + 1)
    idx = (1 << level) - 1
    if blk >= 8:
        x = cum.reshape(rows // blk, blk, lanes)
        return jnp.broadcast_to(x[:, idx:idx + 1, :], x.shape).reshape(rows, lanes)
    x = cum.reshape(rows // 8, 8, lanes)
    sub = lax.broadcasted_iota(jnp.int32, x.shape, 1)
    out = None
    for start in range(8 - blk, -1, -blk):
        piece = jnp.broadcast_to(x[:, start + idx:start + idx + 1, :], x.shape)
        out = piece if out is None else jnp.where(sub < start + blk, piece, out)
    return out.reshape(rows, lanes)


def _gates(fx, lb):
    f = lb + (1.0 - lb) * jax.nn.sigmoid(fx)
    return f, jnp.log2(f), 1.0 - f


def _prefix_scan(logf):
    tiles = (CHUNK // 8, 8, HGRN_DK)
    c = logf.reshape(tiles)
    sub = lax.broadcasted_iota(jnp.int32, (1, 8, HGRN_DK), 1)
    bcast = lambda x, r: jnp.broadcast_to(x[:, r:r + 1, :], tiles)
    c = c + jnp.where((sub & 1) != 0, pltpu.roll(c, 1, 1), 0.0)
    c = c + jnp.where((sub & 2) != 0, jnp.where(sub < 4, bcast(c, 1), bcast(c, 5)), 0.0)
    c = c + jnp.where((sub & 4) != 0, bcast(c, 3), 0.0)
    c = c.reshape(CHUNK, HGRN_DK)
    for level in range(3, LEVELS):
        half = 1 << level
        pieces = []
        for r0 in range(0, CHUNK, 2 * half):
            pieces += [c[r0:r0 + half], c[r0 + half:r0 + 2 * half] + c[r0 + half - 1:r0 + half, :]]
        c = jnp.concatenate(pieces, axis=0)
    return c


def _small_levels(logf, f, q, kk):
    tiles = (CHUNK // 8, 8, HGRN_DK)
    c, f3, q3, k3 = (x.reshape(tiles) for x in (logf, f, q, kk))
    sub = lax.broadcasted_iota(jnp.int32, (1, 8, HGRN_DK), 1)
    bcast = lambda x, r: jnp.broadcast_to(x[:, r:r + 1, :], tiles)
    up = (sub & 1) != 0
    zs = [jnp.where(up, q3 * f3, k3)]
    c = c + jnp.where(up, pltpu.roll(c, 1, 1), 0.0)
    for level, tot in ((1, lambda c: jnp.where(sub < 4, bcast(c, 1), bcast(c, 5))),
                       (2, lambda c: bcast(c, 3))):
        up = (sub & (1 << level)) != 0
        t = tot(c)
        zs.append(jnp.where(up, q3, k3) * jnp.exp2(jnp.where(up, c, t - c)))
        c = c + jnp.where(up, t, 0.0)
    return [z.reshape(CHUNK, HGRN_DK) for z in zs], c.reshape(CHUNK, HGRN_DK)


def _level_large(cum, q, kk, level):
    half = 1 << level
    args, bases, cums = [], [], []
    for r0 in range(0, CHUNK, 2 * half):
        lo, up = slice(r0, r0 + half), slice(r0 + half, r0 + 2 * half)
        tot = cum[r0 + half - 1:r0 + half, :]
        args += [tot - cum[lo], cum[up]]
        bases += [kk[lo], q[up]]
        cums += [cum[lo], cum[up] + tot]
    x = jnp.exp2(jnp.concatenate(args, axis=0))
    return jnp.concatenate(bases, axis=0) * x, jnp.concatenate(cums, axis=0)


def _split3(x):
    hi = x.astype(BF16)
    r = x - hi.astype(F32)
    mid = r.astype(BF16)
    lo = (r - mid.astype(F32)).astype(BF16)
    return hi, mid, lo


def _state_update(s_old, kd, v, blast, pad_rows):
    c = kd.shape[0]
    hi, mid, lo = _split3(jnp.exp2(blast))
    row = lax.broadcasted_iota(jnp.int32, (pad_rows, HGRN_DK), 0)
    dec = jnp.where(row == 0, hi.astype(F32),
                    jnp.where(row == 1, mid.astype(F32), jnp.where(row == 2, lo.astype(F32), 0.0)))
    piece = BF16 if c % 16 == 0 and pad_rows % 16 == 0 else F32
    lhs = jnp.concatenate([kd.astype(piece), dec.astype(piece)], axis=0).astype(BF16)
    rhs = jnp.concatenate(
        [jnp.concatenate([v.astype(piece), jnp.zeros((c, HGRN_DV), piece)], axis=1),
         jnp.concatenate([jnp.zeros((pad_rows, HGRN_DV), piece), jnp.ones((pad_rows, HGRN_DV), piece)], axis=1)],
        axis=0).astype(BF16)
    both = _dot_tn(lhs, rhs)
    return both[:, HGRN_DV:] * s_old + both[:, :HGRN_DV]


def _norm_gate(o, g, nw):
    o = o * lax.rsqrt(jnp.mean(o * o, axis=-1, keepdims=True) + RMS_EPS) * nw
    return o * _silu(g)


def _hgrn_seq_kernel(q_ref, fx_ref, i_ref, g_ref, lb_ref, nw_ref, s0_ref, m_ref,
                     og_ref, sout_ref, s_scr, o_scr, *, n_chunks):
    heads = s_scr.shape[0]

    @pl.when(pl.program_id(2) == 0)
    def _():
        s_scr[...] = s0_ref[0]

    def chunk(c, carry):
        rows = pl.ds(pl.multiple_of(c * CHUNK, CHUNK), CHUNK)
        hcols = [slice(hd * HGRN_DK, (hd + 1) * HGRN_DK) for hd in range(heads)]
        q = [q_ref[rows, cs] for cs in hcols]
        v = [i_ref[rows, cs] for cs in hcols]
        gates = [_gates(fx_ref[rows, cs], lb_ref[:, cs]) for cs in hcols]
        kk = [gt[2] for gt in gates]
        cum = [_prefix_scan(gt[1]) for gt in gates]
        mid = CHUNK // 2 - 1
        spread = None
        for b in cum:
            s_hd = jnp.maximum(b[0:1, :] - b[mid:mid + 1, :], b[mid:mid + 1, :] - b[CHUNK - 1:CHUNK, :])
            spread = s_hd if spread is None else jnp.maximum(spread, s_hd)
        factorable = jnp.max(spread) <= MAX_FACTORED_LOG2_DECAY

        vb = [x.astype(BF16) for x in v]
        for hd in range(heads):
            d = cum[hd] - cum[hd][mid:mid + 1, :]
            qe = (q[hd] * jnp.exp2(d)).astype(BF16)
            ke = (kk[hd] * jnp.exp2(-d)).astype(BF16)
            amat = (_dot_nt(qe, ke) * m_ref[0]).astype(BF16)
            blast = cum[hd][CHUNK - 1:CHUNK, :]
            qd = (q[hd] * jnp.exp2(cum[hd])).astype(BF16)
            kd = kk[hd] * jnp.exp2(blast - cum[hd])
            s_old = s_scr[hd]
            base = _dot(qd, s_old.astype(BF16)) + jnp.sum(q[hd] * kk[hd], axis=-1, keepdims=True) * v[hd]
            o_scr[hd] = base
            s_scr[hd] = _state_update(s_old, kd, vb[hd], blast, 16)
            og_ref[rows, hcols[hd]] = _norm_gate(
                base + _dot(amat, vb[hd]), g_ref[rows, hcols[hd]], nw_ref[:, hcols[hd]]).astype(og_ref.dtype)

        @pl.when(jnp.logical_not(factorable))
        def _():
            small = [_small_levels(gt[1], gt[0], q[hd], kk[hd]) for hd, gt in enumerate(gates)]
            zs = [sm[0] for sm in small]
            part = [sm[1] for sm in small]
            for level in range(3, LEVELS):
                for hd in range(heads):
                    z, part[hd] = _level_large(part[hd], q[hd], kk[hd], level)
                    zs[hd].append(z)
            nblk = CHUNK // MASK_ROWS
            a = [[None] * nblk for _ in range(heads)]
            for level in range(LEVELS):
                step = 1 << (level - 4) if level >= 4 else 0
                blocks = [i for i in range(nblk) if level < 4 or (i & step)]
                for hd in range(heads):
                    z = zs[hd][level].astype(BF16)
                    lhs = z if level < 4 else jnp.concatenate(
                        [z[i * MASK_ROWS:(i + 1) * MASK_ROWS] for i in blocks], axis=0)
                    term = _dot_nt(lhs, z)
                    for n, i in enumerate(blocks):
                        t = (term[n * MASK_ROWS:(n + 1) * MASK_ROWS]
                             * m_ref[level + 1, i * MASK_ROWS:(i + 1) * MASK_ROWS, :])
                        a[hd][i] = t if a[hd][i] is None else a[hd][i] + t
            for hd in range(heads):
                amat = jnp.concatenate(a[hd], axis=0).astype(BF16)
                og_ref[rows, hcols[hd]] = _norm_gate(
                    o_scr[hd] + _dot(amat, vb[hd]),
                    g_ref[rows, hcols[hd]], nw_ref[:, hcols[hd]]).astype(og_ref.dtype)
        return carry

    lax.fori_loop(0, n_chunks, chunk, 0)

    @pl.when(pl.program_id(2) == pl.num_programs(2) - 1)
    def _():
        sout_ref[0] = s_scr[...]


def _hgrn_seq(h, lb, nw, s0, *, n_seq, seq_len, row_block0):
    hb = HGRN_SEQ_HEADS
    width = hb * HGRN_DK
    groups = HGRN_HEADS // hb
    rb = min(HGRN_SEQ_ROWS, seq_len)
    nrb = seq_len // rb
    assert seq_len % rb == 0 and rb % CHUNK == 0 and (row_block0 * SEQ_BLOCK) % rb == 0
    rb0 = row_block0 * SEQ_BLOCK // rb
    s0_batched = s0.shape[0] != 1
    col = lambda part: (lambda b, hg, r: (rb0 + b * nrb + r, part * groups + hg))
    masks = jnp.asarray(_level_masks())
    return pl.pallas_call(
        functools.partial(_hgrn_seq_kernel, n_chunks=rb // CHUNK),
        grid=(n_seq, groups, nrb),
        in_specs=[pl.BlockSpec((rb, width), col(0)),
                  pl.BlockSpec((rb, width), col(1)),
                  pl.BlockSpec((rb, width), col(2)),
                  pl.BlockSpec((rb, width), col(3)),
                  pl.BlockSpec((1, width), lambda b, hg, r: (0, hg)),
                  pl.BlockSpec((1, width), lambda b, hg, r: (0, hg)),
                  pl.BlockSpec((1, hb, HGRN_DK, HGRN_DV),
                               (lambda b, hg, r: (b, hg, 0, 0)) if s0_batched else (lambda b, hg, r: (0, hg, 0, 0))),
                  pl.BlockSpec((LEVELS + 1, CHUNK, CHUNK), lambda b, hg, r: (0, 0, 0))],
        out_specs=[pl.BlockSpec((rb, width), lambda b, hg, r: (b * nrb + r, hg)),
                   pl.BlockSpec((1, hb, HGRN_DK, HGRN_DV), lambda b, hg, r: (b, hg, 0, 0))],
        out_shape=[jax.ShapeDtypeStruct((n_seq * seq_len, HGRN_HEADS * HGRN_DV), BF16),
                   jax.ShapeDtypeStruct((n_seq, HGRN_HEADS, HGRN_DK, HGRN_DV), F32)],
        scratch_shapes=[pltpu.VMEM((hb, HGRN_DK, HGRN_DV), F32),
                        pltpu.VMEM((hb, CHUNK, HGRN_DV), F32)],
        compiler_params=pltpu.CompilerParams(
            dimension_semantics=("parallel", "parallel", "arbitrary"), vmem_limit_bytes=VMEM_LIMIT),
        name="hgrn_seq",
    )(h, h, h, h, lb, nw, s0, masks)


def _hgrn_step_kernel(q_ref, fx_ref, i_ref, g_ref, lb_ref, nw_ref, s_ref, og_ref, sout_ref, *, n_seq, t):
    lb = lb_ref[...]
    nw = nw_ref[...]
    tile = (STEP_GROUP, t, HGRN_DK)
    sub = lax.broadcasted_iota(jnp.int32, (1, t, HGRN_DK), 1)

    def group(i, carry):
        rows = pl.ds(pl.multiple_of(i * (STEP_GROUP * t), STEP_GROUP * t), STEP_GROUP * t)
        q = q_ref[rows, :].reshape(tile)
        v = i_ref[rows, :].reshape(tile)
        _, logf, kk = _gates(fx_ref[rows, :].reshape(tile), lb)
        cum = logf
        shift = 1
        while shift < t:
            cum = cum + jnp.where(sub >= shift, pltpu.roll(cum, shift, 1), 0.0)
            shift *= 2
        intra = (jnp.sum(q * kk, axis=-1, keepdims=True)) * v
        for d in range(1, t):
            valid = sub >= d
            x = jnp.exp2(jnp.where(valid, cum - pltpu.roll(cum, d, 1), 0.0))
            w = jnp.sum(jnp.where(valid, q * x * pltpu.roll(kk, d, 1), 0.0), axis=-1, keepdims=True)
            intra = intra + w * pltpu.roll(v, d, 1)
        blast = cum[:, t - 1:t, :]
        qd = q * jnp.exp2(cum)
        kd = kk * jnp.exp2(blast - cum)
        seqs = [i * STEP_GROUP + n for n in range(STEP_GROUP)]
        inter = [_dot(qd[n].astype(BF16), s_ref[b, 0].astype(BF16)) for n, b in enumerate(seqs)]
        for n, b in enumerate(seqs):
            sout_ref[b, 0] = _state_update(s_ref[b, 0], kd[n], v[n], blast[n], 8)
        o = jnp.stack(inter, axis=0) + intra
        og = _norm_gate(o, g_ref[rows, :].reshape(tile), nw)
        og_ref[rows, :] = og.reshape(STEP_GROUP * t, HGRN_DK).astype(og_ref.dtype)
        return carry

    lax.fori_loop(0, n_seq // STEP_GROUP, group, 0)


def _hgrn_step(h, lb, nw, s0, *, n_seq, t, seq_per_block):
    seq_per_block = min(seq_per_block, n_seq)
    assert n_seq % seq_per_block == 0 and seq_per_block % STEP_GROUP == 0 and t == 8
    rows = seq_per_block * t
    col = lambda part: (lambda i, hd: (i, part * HGRN_HEADS + hd))
    return pl.pallas_call(
        functools.partial(_hgrn_step_kernel, n_seq=seq_per_block, t=t),
        grid=(n_seq // seq_per_block, HGRN_HEADS),
        in_specs=[pl.BlockSpec((rows, HGRN_DK), col(0)),
                  pl.BlockSpec((rows, HGRN_DK), col(1)),
                  pl.BlockSpec((rows, HGRN_DV), col(2)),
                  pl.BlockSpec((rows, HGRN_DV), col(3)),
                  pl.BlockSpec((1, HGRN_DK), lambda i, hd: (0, hd)),
                  pl.BlockSpec((1, HGRN_DV), lambda i, hd: (0, hd)),
                  pl.BlockSpec((seq_per_block, 1, HGRN_DK, HGRN_DV), lambda i, hd: (i, hd, 0, 0))],
        out_specs=[pl.BlockSpec((rows, HGRN_DV), lambda i, hd: (i, hd)),
                   pl.BlockSpec((seq_per_block, 1, HGRN_DK, HGRN_DV), lambda i, hd: (i, hd, 0, 0))],
        out_shape=[jax.ShapeDtypeStruct((n_seq * t, HGRN_HEADS * HGRN_DV), BF16),
                   jax.ShapeDtypeStruct((n_seq, HGRN_HEADS, HGRN_DK, HGRN_DV), F32)],
        compiler_params=pltpu.CompilerParams(
            dimension_semantics=("parallel", "parallel"), vmem_limit_bytes=VMEM_LIMIT),
        name="hgrn_step",
    )(h, h, h, h, lb, nw, s0)


def _attn_seq_kernel(q_ref, g_ref, kc_ref, vc_ref, kp_ref, vp_ref, meta_ref, sink_ref, og_ref):
    first = pl.program_id(1) == 0
    kprev = jnp.where(first, meta_ref[:, :SWA_KV_WIDTH], kp_ref[...])
    vprev = jnp.where(first, meta_ref[:, SWA_KV_WIDTH:], vp_ref[...])
    kband = (jnp.concatenate([kprev, kc_ref[...]], axis=0) * (SWA_SCALE * LOG2E)).astype(BF16)
    vband = jnp.concatenate([vprev, vc_ref[...]], axis=0).astype(BF16)
    kj = lax.broadcasted_iota(jnp.int32, (2 * SEQ_BLOCK, SEQ_BLOCK), 0)
    qi = lax.broadcasted_iota(jnp.int32, (2 * SEQ_BLOCK, SEQ_BLOCK), 1)
    dist = SEQ_BLOCK + qi - kj
    valid = (dist >= 0) & (dist < WINDOW) & (jnp.logical_not(first) | (kj >= SEQ_BLOCK - N_META))
    madd = jnp.where(valid, 0.0, -jnp.inf)
    zeros = jnp.zeros((2 * SEQ_BLOCK, SWA_HEAD_DIM), BF16)
    pairs = SWA_GROUP // 2
    pw = 2 * SWA_HEAD_DIM

    def block_diag(x):
        return jnp.concatenate([jnp.concatenate([x, zeros], axis=1),
                                jnp.concatenate([zeros, x], axis=1)], axis=0)

    sts = []
    for kvh in range(SWA_KV_HEADS):
        ks = slice(kvh * SWA_HEAD_DIM, (kvh + 1) * SWA_HEAD_DIM)
        col0 = kvh * SWA_GROUP * SWA_HEAD_DIM
        xq = jnp.concatenate([q_ref[:, col0 + p * pw:col0 + (p + 1) * pw].astype(BF16)
                              for p in range(pairs)], axis=0)
        sts.append(_dot_nt(block_diag(kband[:, ks]), xq))
    for kvh in range(SWA_KV_HEADS):
        ks = slice(kvh * SWA_HEAD_DIM, (kvh + 1) * SWA_HEAD_DIM)
        col0 = kvh * SWA_GROUP * SWA_HEAD_DIM
        v2 = block_diag(vband[:, ks])
        pns = []
        for pr in range(pairs):
            head = kvh * SWA_GROUP + 2 * pr
            s2 = (sts[kvh][:, pr * SEQ_BLOCK:(pr + 1) * SEQ_BLOCK].reshape(2, 2 * SEQ_BLOCK, SEQ_BLOCK)
                  + madd[None])
            sink = jnp.concatenate([jnp.full((1, 1, SEQ_BLOCK), sink_ref[head] * LOG2E, F32),
                                    jnp.full((1, 1, SEQ_BLOCK), sink_ref[head + 1] * LOG2E, F32)], axis=0)
            m = jnp.maximum(jnp.max(s2, axis=1, keepdims=True), sink)
            p = jnp.exp2(s2 - m)
            den = jnp.sum(p, axis=1, keepdims=True) + jnp.exp2(sink - m)
            pns.append((p * (1.0 / den)).astype(BF16).reshape(4 * SEQ_BLOCK, SEQ_BLOCK))
        for pr in range(pairs):
            o = _dot_tn(pns[pr], v2)
            cs = slice(col0 + pr * pw, col0 + (pr + 1) * pw)
            og_ref[:, cs] = (o * _silu(g_ref[:, cs])).astype(og_ref.dtype)


def _attn_seq(h1, h1_meta, sinks, *, n_seq, seq_len, meta_row_block):
    nb = seq_len // SEQ_BLOCK
    width = SWA_Q_HEADS * SWA_HEAD_DIM
    kcol = 2 * width // SWA_KV_WIDTH
    return pl.pallas_call(
        _attn_seq_kernel,
        grid=(n_seq, nb),
        in_specs=[pl.BlockSpec((SEQ_BLOCK, width), lambda b, j: (b * nb + j, 0)),
                  pl.BlockSpec((SEQ_BLOCK, width), lambda b, j: (b * nb + j, 1)),
                  pl.BlockSpec((SEQ_BLOCK, SWA_KV_WIDTH), lambda b, j: (b * nb + j, kcol)),
                  pl.BlockSpec((SEQ_BLOCK, SWA_KV_WIDTH), lambda b, j: (b * nb + j, kcol + 1)),
                  pl.BlockSpec((SEQ_BLOCK, SWA_KV_WIDTH), lambda b, j: (b * nb + jnp.maximum(j - 1, 0), kcol)),
                  pl.BlockSpec((SEQ_BLOCK, SWA_KV_WIDTH), lambda b, j: (b * nb + jnp.maximum(j - 1, 0), kcol + 1)),
                  pl.BlockSpec((SEQ_BLOCK, 2 * SWA_KV_WIDTH), lambda b, j: (meta_row_block, kcol // 2)),
                  pl.BlockSpec(memory_space=pltpu.SMEM)],
        out_specs=pl.BlockSpec((SEQ_BLOCK, width), lambda b, j: (b * nb + j, 0)),
        out_shape=jax.ShapeDtypeStruct((n_seq * seq_len, width), BF16),
        compiler_params=pltpu.CompilerParams(
            dimension_semantics=("parallel", "arbitrary"), vmem_limit_bytes=VMEM_LIMIT),
        name="attn_seq",
    )(h1, h1, h1, h1, h1, h1, h1_meta, sinks)


def _attn_step_kernel(q_ref, g_ref, kn_ref, vn_ref, ck_ref, cv_ref, sink_ref, og_ref, nk_ref, nv_ref, *, n_seq, t):
    keys = WINDOW + t
    hd = SWA_HEAD_DIM
    tiles = SWA_Q_HEADS // 2
    kj = lax.broadcasted_iota(jnp.int32, (keys, SWA_Q_HEADS * t), 0)
    qt = lax.broadcasted_iota(jnp.int32, (keys, SWA_Q_HEADS * t), 1) % t
    madd = jnp.where((kj >= qt + 1) & (kj <= WINDOW + qt), 0.0, -jnp.inf)
    low = lax.broadcasted_iota(jnp.int32, (t, 2 * hd), 1) < hd
    zero_tile = jnp.zeros((t, 2 * hd), F32)
    sink = sink_ref[...] * LOG2E

    def group(i, carry):
        seqs = [i * SEQ_UNROLL + n for n in range(SEQ_UNROLL)]
        rows = [pl.ds(pl.multiple_of(b * t, t), t) for b in seqs]
        st, vall = [], []
        for b, rw in zip(seqs, rows):
            kc, vc = ck_ref[b], cv_ref[b]
            kn, vn = kn_ref[rw, :], vn_ref[rw, :]
            nk_ref[b, 0:WINDOW - t, :] = kc[t:, :]
            nk_ref[b, WINDOW - t:WINDOW, :] = kn
            nv_ref[b, 0:WINDOW - t, :] = vc[t:, :]
            nv_ref[b, WINDOW - t:WINDOW, :] = vn
            kall = jnp.concatenate([kc, kn], axis=0).astype(BF16)
            vall.append(jnp.concatenate([vc, vn], axis=0).astype(BF16))
            q = q_ref[rw, :] * (SWA_SCALE * LOG2E)
            qtile = [q[:, j * 2 * hd:(j + 1) * 2 * hd] for j in range(tiles)]
            qswap = [pltpu.roll(x, hd, 1) for x in qtile]
            groups = []
            for kvh in range(SWA_KV_HEADS):
                for gq in range(SWA_GROUP):
                    j = (kvh * SWA_GROUP + gq) // 2
                    src = qtile[j] if gq % 2 == kvh % 2 else qswap[j]
                    half = jnp.where(low, src, 0.0) if kvh % 2 == 0 else jnp.where(low, 0.0, src)
                    groups.append(jnp.concatenate(
                        [half if c == kvh // 2 else zero_tile for c in range(SWA_KV_HEADS // 2)], axis=1))
            qbd = jnp.concatenate(groups, axis=0).astype(BF16)
            st.append(_dot_nt(kall, qbd))
        pn = []
        for s in st:
            s = s + madd
            m = jnp.maximum(jnp.max(s, axis=0, keepdims=True), sink)
            p = jnp.exp2(s - m)
            den = jnp.sum(p, axis=0, keepdims=True) + jnp.exp2(sink - m)
            pn.append((p * (1.0 / den)).astype(BF16))
        of = [_dot_tn(p, vl) for p, vl in zip(pn, vall)]
        for o, rw in zip(of, rows):
            g = g_ref[rw, :]
            out = []
            for j in range(tiles):
                kvh = (2 * j) // SWA_GROUP
                ct = slice((kvh // 2) * 2 * hd, (kvh // 2 + 1) * 2 * hd)
                ra = o[(2 * j) * t:(2 * j + 1) * t, ct]
                rb = o[(2 * j + 1) * t:(2 * j + 2) * t, ct]
                if kvh % 2 == 0:
                    out.append(jnp.where(low, ra, pltpu.roll(rb, hd, 1)))
                else:
                    out.append(jnp.where(low, pltpu.roll(ra, hd, 1), rb))
            og_ref[rw, :] = (jnp.concatenate(out, axis=1) * _silu(g)).astype(og_ref.dtype)
        return carry

    lax.fori_loop(0, n_seq // SEQ_UNROLL, group, 0)


def _attn_step(h1, cache_k, cache_v, sink_cols, *, n_seq, t, seq_per_block):
    assert n_seq % seq_per_block == 0
    rows = seq_per_block * t
    width = SWA_Q_HEADS * SWA_HEAD_DIM
    kcol = 2 * width // SWA_KV_WIDTH
    cache_spec = pl.BlockSpec((seq_per_block, WINDOW, SWA_KV_WIDTH), lambda i: (i, 0, 0))
    return pl.pallas_call(
        functools.partial(_attn_step_kernel, n_seq=seq_per_block, t=t),
        grid=(n_seq // seq_per_block,),
        in_specs=[pl.BlockSpec((rows, width), lambda i: (i, 0)),
                  pl.BlockSpec((rows, width), lambda i: (i, 1)),
                  pl.BlockSpec((rows, SWA_KV_WIDTH), lambda i: (i, kcol)),
                  pl.BlockSpec((rows, SWA_KV_WIDTH), lambda i: (i, kcol + 1)),
                  cache_spec, cache_spec,
                  pl.BlockSpec((1, SWA_Q_HEADS * t), lambda i: (0, 0))],
        out_specs=[pl.BlockSpec((rows, width), lambda i: (i, 0)), cache_spec, cache_spec],
        out_shape=[jax.ShapeDtypeStruct((n_seq * t, width), BF16),
                   jax.ShapeDtypeStruct(cache_k.shape, F32),
                   jax.ShapeDtypeStruct(cache_v.shape, F32)],
        compiler_params=pltpu.CompilerParams(
            dimension_semantics=("parallel",), vmem_limit_bytes=VMEM_LIMIT),
        name="attn_step",
    )(h1, h1, h1, h1, cache_k, cache_v, sink_cols)


def kernel(x_prompt, x_sample, state_hgrn, cache_swa_k, cache_swa_v, meta_tokens,
           hgrn_w_in, hgrn_lb_logits, hgrn_norm_w, hgrn_w_out,
           swa_w_in, swa_sinks, swa_w_out, ln_g, ln_b):
    out_dtype = x_prompt.dtype
    bsz, seq, d = x_prompt.shape
    dec_b, dec_t, _ = x_sample.shape
    n_p = bsz * seq
    n_s = dec_b * dec_t
    width = SWA_Q_HEADS * SWA_HEAD_DIM

    w_out0 = hgrn_w_out[0].astype(BF16)
    w_out1 = swa_w_out[0].astype(BF16)
    in_tile = 512
    q_tiles = width // in_tile
    kv_tile = 2 * SWA_KV_WIDTH // in_tile
    assert width % in_tile == 0 and (2 * SWA_KV_WIDTH) % in_tile == 0 and kv_tile == 1
    regroup = lambda j: jnp.where(j < q_tiles, j, jnp.where(j < 2 * q_tiles, j + kv_tile, q_tiles))
    lb = jnp.cumsum(jax.nn.softmax(hgrn_lb_logits.astype(F32), axis=0), axis=0)[0:1]
    nw = hgrn_norm_w[0].astype(F32).reshape(1, -1)
    sinks = swa_sinks[0].astype(F32).reshape(1, SWA_Q_HEADS)
    sink_cols = jnp.repeat(sinks, dec_t, axis=1)
    g0, b0 = ln_g[0:1].astype(F32), ln_b[0:1].astype(F32)
    g1, b1 = ln_g[1:2].astype(F32), ln_b[1:2].astype(F32)

    x_p = x_prompt.astype(F32).reshape(n_p, d)
    meta_block = jnp.concatenate(
        [jnp.zeros((SEQ_BLOCK - N_META, d), F32), meta_tokens.astype(F32)], axis=0)
    x_s = x_sample.astype(F32).reshape(n_s, d)
    meta_blk = n_s // SEQ_BLOCK

    h_sm, w_in0 = _matmul_cast(x_s, meta_block, hgrn_w_in[0].astype(F32), xs_rows=n_s, xm_block=0, tn=in_tile)
    h_p = _matmul(x_p, w_in0, tm=1024, tn=1024)
    zero_state = jnp.zeros((1, HGRN_HEADS, HGRN_DK, HGRN_DV), F32)
    og_meta, s_meta = _hgrn_seq(h_sm, lb, nw, zero_state, n_seq=1, seq_len=SEQ_BLOCK, row_block0=meta_blk)
    og_s, st_s = _hgrn_step(h_sm, lb, nw, state_hgrn[0].astype(F32), n_seq=dec_b, t=dec_t, seq_per_block=128)
    og_p, st_p = _hgrn_seq(h_p, lb, nw, s_meta, n_seq=bsz, seq_len=seq, row_block0=0)
    x1_sm = _matmul_deepnorm_tail(og_s, og_meta, w_out0, x_s, meta_block, g0, b0)
    x1_p = _matmul_deepnorm(og_p, w_out0, x_p, g0, b0, tm=512)

    h1_sm, w_in1 = _matmul_cast(x1_sm, x1_sm, swa_w_in[0].astype(F32), xs_rows=n_s, xm_block=meta_blk,
                                tn=in_tile, col_tile=regroup)
    h1_p = _matmul(x1_p, w_in1, tm=1024, tn=1536)
    og1_p = _attn_seq(h1_p, h1_sm, sinks.reshape(-1), n_seq=bsz, seq_len=seq, meta_row_block=meta_blk)
    ck = cache_swa_k[0].astype(F32).reshape(dec_b, WINDOW, SWA_KV_WIDTH)
    cv = cache_swa_v[0].astype(F32).reshape(dec_b, WINDOW, SWA_KV_WIDTH)
    og1_s, nk_s, nv_s = _attn_step(h1_sm, ck, cv, sink_cols, n_seq=dec_b, t=dec_t, seq_per_block=16)
    y_p = _matmul_deepnorm(og1_p, w_out1, x1_p, g1, b1, tm=512)
    y_s = _matmul_deepnorm(og1_s, w_out1, x1_sm, g1, b1, tm=512)

    kv_p = h1_p.reshape(bsz, seq, -1)[:, seq - WINDOW:, 2 * width:]
    cache_shape = (1, bsz, WINDOW, SWA_KV_HEADS, SWA_HEAD_DIM)
    return (y_p.reshape(bsz, seq, d).astype(out_dtype),
            y_s.reshape(dec_b, dec_t, d).astype(out_dtype),
            st_p[None].astype(out_dtype),
            st_s[None].astype(out_dtype),
            kv_p[..., :SWA_KV_WIDTH].reshape(cache_shape).astype(out_dtype),
            kv_p[..., SWA_KV_WIDTH:].reshape(cache_shape).astype(out_dtype),
            nk_s.reshape((1,) + cache_swa_k.shape[1:]).astype(out_dtype),
            nv_s.reshape((1,) + cache_swa_v.shape[1:]).astype(out_dtype))
```

```python
import functools

import numpy as np
import jax
import jax.numpy as jnp
from jax import lax
from jax.experimental import pallas as pl
from jax.experimental.pallas import tpu as pltpu

F32 = jnp.float32
BF16 = jnp.bfloat16

D_MODEL = 2048
N_META = 16
DEPTH = 2
HGRN_HEADS = 16
HGRN_DK = 128
HGRN_DV = 128
SWA_Q_HEADS = 32
SWA_KV_HEADS = 4
SWA_GROUP = 8
SWA_HEAD_DIM = 64
SWA_KV_WIDTH = SWA_KV_HEADS * SWA_HEAD_DIM
SWA_SCALE = SWA_HEAD_DIM ** -0.5
WINDOW = 128
DEEPNORM_ALPHA = (2.0 * DEPTH) ** 0.25
LN_EPS = 1e-5
RMS_EPS = 1e-6
LOG2E = 1.4426950408889634

CHUNK = 128
LEVELS = 7
MAX_FACTORED_LOG2_DECAY = 100.0
MASK_ROWS = 16
SEQ_BLOCK = 128
LN_SUB_ROWS = 128
STEP_GROUP = 16
SEQ_UNROLL = 4
HGRN_SEQ_HEADS = 8
HGRN_SEQ_ROWS = 512
VMEM_LIMIT = 56 * 1024 * 1024


def _dot(a, b):
    return jnp.dot(a, b, preferred_element_type=F32)


def _dot_nt(a, b):
    return lax.dot_general(a, b, (((1,), (1,)), ((), ())), preferred_element_type=F32)


def _dot_tn(a, b):
    return lax.dot_general(a, b, (((0,), (0,)), ((), ())), preferred_element_type=F32)


def _silu(x):
    return x * jax.nn.sigmoid(x)


def _mm_kernel(x_ref, w_ref, o_ref):
    o_ref[...] = _dot(x_ref[...].astype(BF16), w_ref[...]).astype(o_ref.dtype)


def _row_tile(n, want):
    if n <= want:
        return n
    return max(t for t in range(16, want + 1, 16) if n % t == 0)


def _matmul(x, w, *, tm, tn, out_dtype=F32):
    n, k = x.shape
    e = w.shape[1]
    tm = _row_tile(n, tm)
    assert n % tm == 0 and e % tn == 0
    return pl.pallas_call(
        _mm_kernel,
        grid=(n // tm, e // tn),
        in_specs=[pl.BlockSpec((tm, k), lambda i, j: (i, 0)),
                  pl.BlockSpec((k, tn), lambda i, j: (0, j))],
        out_specs=pl.BlockSpec((tm, tn), lambda i, j: (i, j)),
        out_shape=jax.ShapeDtypeStruct((n, e), out_dtype),
        compiler_params=pltpu.CompilerParams(
            dimension_semantics=("parallel", "parallel"), vmem_limit_bytes=VMEM_LIMIT),
        name="proj_in",
    )(x, w)


def _mm_cast_kernel(xs_ref, xm_ref, w_ref, o_ref, wb_ref, xb_scr):
    @pl.when(pl.program_id(0) == 0)
    def _():
        n_s = xs_ref.shape[0]
        xb_scr[0:n_s, :] = xs_ref[...].astype(BF16)
        xb_scr[n_s:, :] = xm_ref[...].astype(BF16)

    wb = w_ref[...].astype(BF16)
    wb_ref[...] = wb
    o_ref[...] = _dot(xb_scr[...], wb)


def _matmul_cast(xs, xm, w, *, xs_rows, xm_block, tn, col_tile=lambda j: j):
    k, e = w.shape
    n = xs_rows + SEQ_BLOCK
    assert e % tn == 0
    return pl.pallas_call(
        _mm_cast_kernel,
        grid=(e // tn,),
        in_specs=[pl.BlockSpec((xs_rows, k), lambda j: (0, 0)),
                  pl.BlockSpec((SEQ_BLOCK, k), lambda j: (xm_block, 0)),
                  pl.BlockSpec((k, tn), lambda j: (0, col_tile(j)))],
        out_specs=[pl.BlockSpec((n, tn), lambda j: (0, j)),
                   pl.BlockSpec((k, tn), lambda j: (0, j))],
        out_shape=[jax.ShapeDtypeStruct((n, e), F32), jax.ShapeDtypeStruct((k, e), BF16)],
        scratch_shapes=[pltpu.VMEM((n, k), BF16)],
        compiler_params=pltpu.CompilerParams(
            dimension_semantics=("arbitrary",), vmem_limit_bytes=VMEM_LIMIT),
        name="proj_in_cast",
    )(xs, xm, w)


def _mm_ln_tail_kernel(a_ref, at_ref, w_ref, x_ref, xt_ref, g_ref, b_ref, o_ref):
    tail = pl.program_id(0) == pl.num_programs(0) - 1
    a = jnp.where(tail, at_ref[...], a_ref[...])
    x = jnp.where(tail, xt_ref[...], x_ref[...])
    z = DEEPNORM_ALPHA * x + _dot(a, w_ref[...])
    mu = jnp.mean(z, axis=-1, keepdims=True)
    zc = z - mu
    var = jnp.mean(zc * zc, axis=-1, keepdims=True)
    o_ref[...] = zc * lax.rsqrt(var + LN_EPS) * g_ref[...] + b_ref[...]


def _matmul_deepnorm_tail(a, a_tail, w, x, x_tail, g, b):
    n, k = a.shape
    d = w.shape[1]
    tm = SEQ_BLOCK
    last = n // tm - 1
    body = lambda i: (jnp.minimum(i, last), 0)
    const = lambda i: (0, 0)
    return pl.pallas_call(
        _mm_ln_tail_kernel,
        grid=(n // tm + 1,),
        in_specs=[pl.BlockSpec((tm, k), body), pl.BlockSpec((tm, k), const),
                  pl.BlockSpec((k, d), const),
                  pl.BlockSpec((tm, d), body), pl.BlockSpec((tm, d), const),
                  pl.BlockSpec((1, d), const), pl.BlockSpec((1, d), const)],
        out_specs=pl.BlockSpec((tm, d), lambda i: (i, 0)),
        out_shape=jax.ShapeDtypeStruct((n + tm, d), F32),
        compiler_params=pltpu.CompilerParams(
            dimension_semantics=("parallel",), vmem_limit_bytes=VMEM_LIMIT),
        name="proj_out_deepnorm_tail",
    )(a, a_tail, w, x, x_tail, g, b)


def _mm_ln_kernel(a_ref, w_ref, x_ref, g_ref, b_ref, o_ref):
    tm = a_ref.shape[0]
    sub = LN_SUB_ROWS if tm % LN_SUB_ROWS == 0 else tm
    for r0 in range(0, tm, sub):
        rows = slice(r0, r0 + sub)
        z = DEEPNORM_ALPHA * x_ref[rows, :] + _dot(a_ref[rows, :], w_ref[...])
        mu = jnp.mean(z, axis=-1, keepdims=True)
        zc = z - mu
        var = jnp.mean(zc * zc, axis=-1, keepdims=True)
        o_ref[rows, :] = zc * lax.rsqrt(var + LN_EPS) * g_ref[...] + b_ref[...]


def _matmul_deepnorm(a, w, x, g, b, *, tm):
    n, k = a.shape
    d = w.shape[1]
    tm = _row_tile(n, tm)
    return pl.pallas_call(
        _mm_ln_kernel,
        grid=(n // tm,),
        in_specs=[pl.BlockSpec((tm, k), lambda i: (i, 0)),
                  pl.BlockSpec((k, d), lambda i: (0, 0)),
                  pl.BlockSpec((tm, d), lambda i: (i, 0)),
                  pl.BlockSpec((1, d), lambda i: (0, 0)),
                  pl.BlockSpec((1, d), lambda i: (0, 0))],
        out_specs=pl.BlockSpec((tm, d), lambda i: (i, 0)),
        out_shape=jax.ShapeDtypeStruct((n, d), F32),
        compiler_params=pltpu.CompilerParams(
            dimension_semantics=("parallel",), vmem_limit_bytes=VMEM_LIMIT),
        name="proj_out_deepnorm",
    )(a, w, x, g, b)


def _level_masks():
    t = np.arange(CHUNK)
    out = np.zeros((LEVELS + 1, CHUNK, CHUNK), np.float32)
    out[0] = t[:, None] > t[None, :]
    for l in range(LEVELS):
        h = 1 << l
        same = (t[:, None] >> (l + 1)) == (t[None, :] >> (l + 1))
        out[l + 1] = same & ((t[:, None] & h) != 0) & ((t[None, :] & h) == 0)
    return out


def _gates(fx, lb):
    f = lb + (1.0 - lb) * jax.nn.sigmoid(fx)
    return f, jnp.log2(f), 1.0 - f


def _prefix_scan(logf):
    tiles = (CHUNK // 8, 8, HGRN_DK)
    c = logf.reshape(tiles)
    sub = lax.broadcasted_iota(jnp.int32, (1, 8, HGRN_DK), 1)
    bcast = lambda x, r: jnp.broadcast_to(x[:, r:r + 1, :], tiles)
    c = c + jnp.where((sub & 1) != 0, pltpu.roll(c, 1, 1), 0.0)
    c = c + jnp.where((sub & 2) != 0, jnp.where(sub < 4, bcast(c, 1), bcast(c, 5)), 0.0)
    c = c + jnp.where((sub & 4) != 0, bcast(c, 3), 0.0)
    c = c.reshape(CHUNK, HGRN_DK)
    for level in range(3, LEVELS):
        half = 1 << level
        pieces = []
        for r0 in range(0, CHUNK, 2 * half):
            pieces += [c[r0:r0 + half], c[r0 + half:r0 + 2 * half] + c[r0 + half - 1:r0 + half, :]]
        c = jnp.concatenate(pieces, axis=0)
    return c


def _small_levels(logf, f, q, kk):
    tiles = (CHUNK // 8, 8, HGRN_DK)
    c, f3, q3, k3 = (x.reshape(tiles) for x in (logf, f, q, kk))
    sub = lax.broadcasted_iota(jnp.int32, (1, 8, HGRN_DK), 1)
    bcast = lambda x, r: jnp.broadcast_to(x[:, r:r + 1, :], tiles)
    up = (sub & 1) != 0
    zs = [jnp.where(up, q3 * f3, k3)]
    c = c + jnp.where(up, pltpu.roll(c, 1, 1), 0.0)
    for level, tot in ((1, lambda c: jnp.where(sub < 4, bcast(c, 1), bcast(c, 5))),
                       (2, lambda c: bcast(c, 3))):
        up = (sub & (1 << level)) != 0
        t = tot(c)
        zs.append(jnp.where(up, q3, k3) * jnp.exp2(jnp.where(up, c, t - c)))
        c = c + jnp.where(up, t, 0.0)
    return [z.reshape(CHUNK, HGRN_DK) for z in zs], c.reshape(CHUNK, HGRN_DK)


def _level_large(cum, q, kk, level):
    half = 1 << level
    args, bases, cums = [], [], []
    for r0 in range(0, CHUNK, 2 * half):
        lo, up = slice(r0, r0 + half), slice(r0 + half, r0 + 2 * half)
        tot = cum[r0 + half - 1:r0 + half, :]
        args += [tot - cum[lo], cum[up]]
        bases += [kk[lo], q[up]]
        cums += [cum[lo], cum[up] + tot]
    x = jnp.exp2(jnp.concatenate(args, axis=0))
    return jnp.concatenate(bases, axis=0) * x, jnp.concatenate(cums, axis=0)


def _split3(x):
    hi = x.astype(BF16)
    r = x - hi.astype(F32)
    mid = r.astype(BF16)
    lo = (r - mid.astype(F32)).astype(BF16)
    return hi, mid, lo


def _state_update(s_old, kd, v, blast, pad_rows):
    c = kd.shape[0]
    hi, mid, lo = _split3(jnp.exp2(blast))
    row = lax.broadcasted_iota(jnp.int32, (pad_rows, HGRN_DK), 0)
    dec = jnp.where(row == 0, hi.astype(F32),
                    jnp.where(row == 1, mid.astype(F32), jnp.where(row == 2, lo.astype(F32), 0.0)))
    piece = BF16 if c % 16 == 0 and pad_rows % 16 == 0 else F32
    lhs = jnp.concatenate([kd.astype(piece), dec.astype(piece)], axis=0).astype(BF16)
    rhs = jnp.concatenate(
        [jnp.concatenate([v.astype(piece), jnp.zeros((c, HGRN_DV), piece)], axis=1),
         jnp.concatenate([jnp.zeros((pad_rows, HGRN_DV), piece), jnp.ones((pad_rows, HGRN_DV), piece)], axis=1)],
        axis=0).astype(BF16)
    both = _dot_tn(lhs, rhs)
    return both[:, HGRN_DV:] * s_old + both[:, :HGRN_DV]


def _norm_gate(o, g, nw):
    o = o * lax.rsqrt(jnp.mean(o * o, axis=-1, keepdims=True) + RMS_EPS) * nw
    return o * _silu(g)


def _hgrn_seq_kernel(q_ref, fx_ref, i_ref, g_ref, lb_ref, nw_ref, s0_ref, m_ref,
                     og_ref, sout_ref, s_scr, o_scr, *, n_chunks):
    heads = s_scr.shape[0]

    @pl.when(pl.program_id(2) == 0)
    def _():
        s_scr[...] = s0_ref[0]

    def chunk(c, carry):
        rows = pl.ds(pl.multiple_of(c * CHUNK, CHUNK), CHUNK)
        hcols = [slice(hd * HGRN_DK, (hd + 1) * HGRN_DK) for hd in range(heads)]
        q = [q_ref[rows, cs] for cs in hcols]
        v = [i_ref[rows, cs] for cs in hcols]
        gates = [_gates(fx_ref[rows, cs], lb_ref[:, cs]) for cs in hcols]
        kk = [gt[2] for gt in gates]
        cum = [_prefix_scan(gt[1]) for gt in gates]
        mid = CHUNK // 2 - 1
        spread = None
        for b in cum:
            s_hd = jnp.maximum(b[0:1, :] - b[mid:mid + 1, :], b[mid:mid + 1, :] - b[CHUNK - 1:CHUNK, :])
            spread = s_hd if spread is None else jnp.maximum(spread, s_hd)
        factorable = jnp.max(spread) <= MAX_FACTORED_LOG2_DECAY

        vb = [x.astype(BF16) for x in v]
        for hd in range(heads):
            d = cum[hd] - cum[hd][mid:mid + 1, :]
            qe = (q[hd] * jnp.exp2(d)).astype(BF16)
            ke = (kk[hd] * jnp.exp2(-d)).astype(BF16)
            amat = jnp.where(m_ref[0] > 0.5, _dot_nt(qe, ke), 0.0).astype(BF16)
            blast = cum[hd][CHUNK - 1:CHUNK, :]
            qd = (q[hd] * jnp.exp2(cum[hd])).astype(BF16)
            kd = kk[hd] * jnp.exp2(blast - cum[hd])
            s_old = s_scr[hd]
            base = _dot(qd, s_old.astype(BF16)) + jnp.sum(q[hd] * kk[hd], axis=-1, keepdims=True) * v[hd]
            o_scr[hd] = base
            s_scr[hd] = _state_update(s_old, kd, vb[hd], blast, 16)
            og_ref[rows, hcols[hd]] = _norm_gate(
                base + _dot(amat, vb[hd]), g_ref[rows, hcols[hd]], nw_ref[:, hcols[hd]]).astype(og_ref.dtype)

        @pl.when(jnp.logical_not(factorable))
        def _():
            small = [_small_levels(gt[1], gt[0], q[hd], kk[hd]) for hd, gt in enumerate(gates)]
            zs = [sm[0] for sm in small]
            part = [sm[1] for sm in small]
            for level in range(3, LEVELS):
                for hd in range(heads):
                    z, part[hd] = _level_large(part[hd], q[hd], kk[hd], level)
                    zs[hd].append(z)
            nblk = CHUNK // MASK_ROWS
            a = [[None] * nblk for _ in range(heads)]
            for level in range(LEVELS):
                step = 1 << (level - 4) if level >= 4 else 0
                blocks = [i for i in range(nblk) if level < 4 or (i & step)]
                for hd in range(heads):
                    z = zs[hd][level].astype(BF16)
                    lhs = z if level < 4 else jnp.concatenate(
                        [z[i * MASK_ROWS:(i + 1) * MASK_ROWS] for i in blocks], axis=0)
                    term = _dot_nt(lhs, z)
                    for n, i in enumerate(blocks):
                        t = (term[n * MASK_ROWS:(n + 1) * MASK_ROWS]
                             * m_ref[level + 1, i * MASK_ROWS:(i + 1) * MASK_ROWS, :])
                        a[hd][i] = t if a[hd][i] is None else a[hd][i] + t
            for hd in range(heads):
                amat = jnp.concatenate(a[hd], axis=0).astype(BF16)
                og_ref[rows, hcols[hd]] = _norm_gate(
                    o_scr[hd] + _dot(amat, vb[hd]),
                    g_ref[rows, hcols[hd]], nw_ref[:, hcols[hd]]).astype(og_ref.dtype)
        return carry

    lax.fori_loop(0, n_chunks, chunk, 0)

    @pl.when(pl.program_id(2) == pl.num_programs(2) - 1)
    def _():
        sout_ref[0] = s_scr[...]


def _hgrn_seq(h, lb, nw, s0, *, n_seq, seq_len, row_block0):
    hb = HGRN_SEQ_HEADS
    width = hb * HGRN_DK
    groups = HGRN_HEADS // hb
    rb = min(HGRN_SEQ_ROWS, seq_len)
    nrb = seq_len // rb
    assert seq_len % rb == 0 and rb % CHUNK == 0 and (row_block0 * SEQ_BLOCK) % rb == 0
    rb0 = row_block0 * SEQ_BLOCK // rb
    s0_batched = s0.shape[0] != 1
    col = lambda part: (lambda b, hg, r: (rb0 + b * nrb + r, part * groups + hg))
    masks = jnp.asarray(_level_masks())
    return pl.pallas_call(
        functools.partial(_hgrn_seq_kernel, n_chunks=rb // CHUNK),
        grid=(n_seq, groups, nrb),
        in_specs=[pl.BlockSpec((rb, width), col(0)),
                  pl.BlockSpec((rb, width), col(1)),
                  pl.BlockSpec((rb, width), col(2)),
                  pl.BlockSpec((rb, width), col(3)),
                  pl.BlockSpec((1, width), lambda b, hg, r: (0, hg)),
                  pl.BlockSpec((1, width), lambda b, hg, r: (0, hg)),
                  pl.BlockSpec((1, hb, HGRN_DK, HGRN_DV),
                               (lambda b, hg, r: (b, hg, 0, 0)) if s0_batched else (lambda b, hg, r: (0, hg, 0, 0))),
                  pl.BlockSpec((LEVELS + 1, CHUNK, CHUNK), lambda b, hg, r: (0, 0, 0))],
        out_specs=[pl.BlockSpec((rb, width), lambda b, hg, r: (b * nrb + r, hg)),
                   pl.BlockSpec((1, hb, HGRN_DK, HGRN_DV), lambda b, hg, r: (b, hg, 0, 0))],
        out_shape=[jax.ShapeDtypeStruct((n_seq * seq_len, HGRN_HEADS * HGRN_DV), BF16),
                   jax.ShapeDtypeStruct((n_seq, HGRN_HEADS, HGRN_DK, HGRN_DV), F32)],
        scratch_shapes=[pltpu.VMEM((hb, HGRN_DK, HGRN_DV), F32),
                        pltpu.VMEM((hb, CHUNK, HGRN_DV), F32)],
        compiler_params=pltpu.CompilerParams(
            dimension_semantics=("parallel", "parallel", "arbitrary"), vmem_limit_bytes=VMEM_LIMIT),
        name="hgrn_seq",
    )(h, h, h, h, lb, nw, s0, masks)


def _hgrn_step_kernel(q_ref, fx_ref, i_ref, g_ref, lb_ref, nw_ref, s_ref, og_ref, sout_ref, *, n_seq, t):
    lb = lb_ref[...]
    nw = nw_ref[...]
    tile = (STEP_GROUP, t, HGRN_DK)
    sub = lax.broadcasted_iota(jnp.int32, (1, t, HGRN_DK), 1)

    def group(i, carry):
        rows = pl.ds(pl.multiple_of(i * (STEP_GROUP * t), STEP_GROUP * t), STEP_GROUP * t)
        q = q_ref[rows, :].reshape(tile)
        v = i_ref[rows, :].reshape(tile)
        _, logf, kk = _gates(fx_ref[rows, :].reshape(tile), lb)
        cum = logf
        shift = 1
        while shift < t:
            cum = cum + jnp.where(sub >= shift, pltpu.roll(cum, shift, 1), 0.0)
            shift *= 2
        intra = (jnp.sum(q * kk, axis=-1, keepdims=True)) * v
        for d in range(1, t):
            valid = sub >= d
            x = jnp.exp2(jnp.where(valid, cum - pltpu.roll(cum, d, 1), 0.0))
            w = jnp.sum(jnp.where(valid, q * x * pltpu.roll(kk, d, 1), 0.0), axis=-1, keepdims=True)
            intra = intra + w * pltpu.roll(v, d, 1)
        blast = cum[:, t - 1:t, :]
        qd = q * jnp.exp2(cum)
        kd = kk * jnp.exp2(blast - cum)
        seqs = [i * STEP_GROUP + n for n in range(STEP_GROUP)]
        inter = [_dot(qd[n].astype(BF16), s_ref[b, 0].astype(BF16)) for n, b in enumerate(seqs)]
        for n, b in enumerate(seqs):
            sout_ref[b, 0] = _state_update(s_ref[b, 0], kd[n], v[n], blast[n], 8)
        o = jnp.stack(inter, axis=0) + intra
        og = _norm_gate(o, g_ref[rows, :].reshape(tile), nw)
        og_ref[rows, :] = og.reshape(STEP_GROUP * t, HGRN_DK).astype(og_ref.dtype)
        return carry

    lax.fori_loop(0, n_seq // STEP_GROUP, group, 0)


def _hgrn_step(h, lb, nw, s0, *, n_seq, t, seq_per_block):
    seq_per_block = min(seq_per_block, n_seq)
    assert n_seq % seq_per_block == 0 and seq_per_block % STEP_GROUP == 0 and t == 8
    rows = seq_per_block * t
    col = lambda part: (lambda i, hd: (i, part * HGRN_HEADS + hd))
    return pl.pallas_call(
        functools.partial(_hgrn_step_kernel, n_seq=seq_per_block, t=t),
        grid=(n_seq // seq_per_block, HGRN_HEADS),
        in_specs=[pl.BlockSpec((rows, HGRN_DK), col(0)),
                  pl.BlockSpec((rows, HGRN_DK), col(1)),
                  pl.BlockSpec((rows, HGRN_DV), col(2)),
                  pl.BlockSpec((rows, HGRN_DV), col(3)),
                  pl.BlockSpec((1, HGRN_DK), lambda i, hd: (0, hd)),
                  pl.BlockSpec((1, HGRN_DV), lambda i, hd: (0, hd)),
                  pl.BlockSpec((seq_per_block, 1, HGRN_DK, HGRN_DV), lambda i, hd: (i, hd, 0, 0))],
        out_specs=[pl.BlockSpec((rows, HGRN_DV), lambda i, hd: (i, hd)),
                   pl.BlockSpec((seq_per_block, 1, HGRN_DK, HGRN_DV), lambda i, hd: (i, hd, 0, 0))],
        out_shape=[jax.ShapeDtypeStruct((n_seq * t, HGRN_HEADS * HGRN_DV), BF16),
                   jax.ShapeDtypeStruct((n_seq, HGRN_HEADS, HGRN_DK, HGRN_DV), F32)],
        compiler_params=pltpu.CompilerParams(
            dimension_semantics=("parallel", "parallel"), vmem_limit_bytes=VMEM_LIMIT),
        name="hgrn_step",
    )(h, h, h, h, lb, nw, s0)


def _attn_seq_kernel(q_ref, g_ref, kc_ref, vc_ref, kp_ref, vp_ref, meta_ref, sink_ref, og_ref):
    first = pl.program_id(1) == 0
    kprev = jnp.where(first, meta_ref[:, :SWA_KV_WIDTH], kp_ref[...])
    vprev = jnp.where(first, meta_ref[:, SWA_KV_WIDTH:], vp_ref[...])
    kband = (jnp.concatenate([kprev, kc_ref[...]], axis=0) * (SWA_SCALE * LOG2E)).astype(BF16)
    vband = jnp.concatenate([vprev, vc_ref[...]], axis=0).astype(BF16)
    kj = lax.broadcasted_iota(jnp.int32, (2 * SEQ_BLOCK, SEQ_BLOCK), 0)
    qi = lax.broadcasted_iota(jnp.int32, (2 * SEQ_BLOCK, SEQ_BLOCK), 1)
    dist = SEQ_BLOCK + qi - kj
    valid = (dist >= 0) & (dist < WINDOW) & (jnp.logical_not(first) | (kj >= SEQ_BLOCK - N_META))
    madd = jnp.where(valid, 0.0, -jnp.inf)
    zeros = jnp.zeros((2 * SEQ_BLOCK, SWA_HEAD_DIM), BF16)
    pairs = SWA_GROUP // 2
    pw = 2 * SWA_HEAD_DIM

    def block_diag(x):
        return jnp.concatenate([jnp.concatenate([x, zeros], axis=1),
                                jnp.concatenate([zeros, x], axis=1)], axis=0)

    sts = []
    for kvh in range(SWA_KV_HEADS):
        ks = slice(kvh * SWA_HEAD_DIM, (kvh + 1) * SWA_HEAD_DIM)
        col0 = kvh * SWA_GROUP * SWA_HEAD_DIM
        xq = jnp.concatenate([q_ref[:, col0 + p * pw:col0 + (p + 1) * pw].astype(BF16)
                              for p in range(pairs)], axis=0)
        sts.append(_dot_nt(block_diag(kband[:, ks]), xq))
    for kvh in range(SWA_KV_HEADS):
        ks = slice(kvh * SWA_HEAD_DIM, (kvh + 1) * SWA_HEAD_DIM)
        col0 = kvh * SWA_GROUP * SWA_HEAD_DIM
        v2 = block_diag(vband[:, ks])
        pns = []
        for pr in range(pairs):
            head = kvh * SWA_GROUP + 2 * pr
            s2 = (sts[kvh][:, pr * SEQ_BLOCK:(pr + 1) * SEQ_BLOCK].reshape(2, 2 * SEQ_BLOCK, SEQ_BLOCK)
                  + madd[None])
            sink = jnp.concatenate([jnp.full((1, 1, SEQ_BLOCK), sink_ref[head] * LOG2E, F32),
                                    jnp.full((1, 1, SEQ_BLOCK), sink_ref[head + 1] * LOG2E, F32)], axis=0)
            m = jnp.maximum(jnp.max(s2, axis=1, keepdims=True), sink)
            p = jnp.exp2(s2 - m)
            den = jnp.sum(p, axis=1, keepdims=True) + jnp.exp2(sink - m)
            pns.append((p * (1.0 / den)).astype(BF16).reshape(4 * SEQ_BLOCK, SEQ_BLOCK))
        for pr in range(pairs):
            o = _dot_tn(pns[pr], v2)
            cs = slice(col0 + pr * pw, col0 + (pr + 1) * pw)
            og_ref[:, cs] = (o * _silu(g_ref[:, cs])).astype(og_ref.dtype)


def _attn_seq(h1, h1_meta, sinks, *, n_seq, seq_len, meta_row_block):
    nb = seq_len // SEQ_BLOCK
    width = SWA_Q_HEADS * SWA_HEAD_DIM
    kcol = 2 * width // SWA_KV_WIDTH
    return pl.pallas_call(
        _attn_seq_kernel,
        grid=(n_seq, nb),
        in_specs=[pl.BlockSpec((SEQ_BLOCK, width), lambda b, j: (b * nb + j, 0)),
                  pl.BlockSpec((SEQ_BLOCK, width), lambda b, j: (b * nb + j, 1)),
                  pl.BlockSpec((SEQ_BLOCK, SWA_KV_WIDTH), lambda b, j: (b * nb + j, kcol)),
                  pl.BlockSpec((SEQ_BLOCK, SWA_KV_WIDTH), lambda b, j: (b * nb + j, kcol + 1)),
                  pl.BlockSpec((SEQ_BLOCK, SWA_KV_WIDTH), lambda b, j: (b * nb + jnp.maximum(j - 1, 0), kcol)),
                  pl.BlockSpec((SEQ_BLOCK, SWA_KV_WIDTH), lambda b, j: (b * nb + jnp.maximum(j - 1, 0), kcol + 1)),
                  pl.BlockSpec((SEQ_BLOCK, 2 * SWA_KV_WIDTH), lambda b, j: (meta_row_block, kcol // 2)),
                  pl.BlockSpec(memory_space=pltpu.SMEM)],
        out_specs=pl.BlockSpec((SEQ_BLOCK, width), lambda b, j: (b * nb + j, 0)),
        out_shape=jax.ShapeDtypeStruct((n_seq * seq_len, width), BF16),
        compiler_params=pltpu.CompilerParams(
            dimension_semantics=("parallel", "arbitrary"), vmem_limit_bytes=VMEM_LIMIT),
        name="attn_seq",
    )(h1, h1, h1, h1, h1, h1, h1_meta, sinks)


def _attn_step_kernel(q_ref, g_ref, kn_ref, vn_ref, ck_ref, cv_ref, sink_ref, og_ref, nk_ref, nv_ref, *, n_seq, t):
    keys = WINDOW + t
    hd = SWA_HEAD_DIM
    tiles = SWA_Q_HEADS // 2
    kj = lax.broadcasted_iota(jnp.int32, (keys, SWA_Q_HEADS * t), 0)
    qt = lax.broadcasted_iota(jnp.int32, (keys, SWA_Q_HEADS * t), 1) % t
    madd = jnp.where((kj >= qt + 1) & (kj <= WINDOW + qt), 0.0, -jnp.inf)
    low = lax.broadcasted_iota(jnp.int32, (t, 2 * hd), 1) < hd
    zero_tile = jnp.zeros((t, 2 * hd), F32)
    sink = sink_ref[...] * LOG2E

    def group(i, carry):
        seqs = [i * SEQ_UNROLL + n for n in range(SEQ_UNROLL)]
        rows = [pl.ds(pl.multiple_of(b * t, t), t) for b in seqs]
        st, vall = [], []
        for b, rw in zip(seqs, rows):
            kc, vc = ck_ref[b], cv_ref[b]
            kn, vn = kn_ref[rw, :], vn_ref[rw, :]
            nk_ref[b, 0:WINDOW - t, :] = kc[t:, :]
            nk_ref[b, WINDOW - t:WINDOW, :] = kn
            nv_ref[b, 0:WINDOW - t, :] = vc[t:, :]
            nv_ref[b, WINDOW - t:WINDOW, :] = vn
            kall = jnp.concatenate([kc, kn], axis=0).astype(BF16)
            vall.append(jnp.concatenate([vc, vn], axis=0).astype(BF16))
            q = q_ref[rw, :] * (SWA_SCALE * LOG2E)
            qtile = [q[:, j * 2 * hd:(j + 1) * 2 * hd] for j in range(tiles)]
            qswap = [pltpu.roll(x, hd, 1) for x in qtile]
            groups = []
            for kvh in range(SWA_KV_HEADS):
                for gq in range(SWA_GROUP):
                    j = (kvh * SWA_GROUP + gq) // 2
                    src = qtile[j] if gq % 2 == kvh % 2 else qswap[j]
                    half = jnp.where(low, src, 0.0) if kvh % 2 == 0 else jnp.where(low, 0.0, src)
                    groups.append(jnp.concatenate(
                        [half if c == kvh // 2 else zero_tile for c in range(SWA_KV_HEADS // 2)], axis=1))
            qbd = jnp.concatenate(groups, axis=0).astype(BF16)
            st.append(_dot_nt(kall, qbd))
        pn = []
        for s in st:
            s = s + madd
            m = jnp.maximum(jnp.max(s, axis=0, keepdims=True), sink)
            p = jnp.exp2(s - m)
            den = jnp.sum(p, axis=0, keepdims=True) + jnp.exp2(sink - m)
            pn.append((p * (1.0 / den)).astype(BF16))
        of = [_dot_tn(p, vl) for p, vl in zip(pn, vall)]
        for o, rw in zip(of, rows):
            g = g_ref[rw, :]
            out = []
            for j in range(tiles):
                kvh = (2 * j) // SWA_GROUP
                ct = slice((kvh // 2) * 2 * hd, (kvh // 2 + 1) * 2 * hd)
                ra = o[(2 * j) * t:(2 * j + 1) * t, ct]
                rb = o[(2 * j + 1) * t:(2 * j + 2) * t, ct]
                if kvh % 2 == 0:
                    out.append(jnp.where(low, ra, pltpu.roll(rb, hd, 1)))
                else:
                    out.append(jnp.where(low, pltpu.roll(ra, hd, 1), rb))
            og_ref[rw, :] = (jnp.concatenate(out, axis=1) * _silu(g)).astype(og_ref.dtype)
        return carry

    lax.fori_loop(0, n_seq // SEQ_UNROLL, group, 0)


def _attn_step(h1, cache_k, cache_v, sink_cols, *, n_seq, t, seq_per_block):
    assert n_seq % seq_per_block == 0
    rows = seq_per_block * t
    width = SWA_Q_HEADS * SWA_HEAD_DIM
    kcol = 2 * width // SWA_KV_WIDTH
    cache_spec = pl.BlockSpec((seq_per_block, WINDOW, SWA_KV_WIDTH), lambda i: (i, 0, 0))
    return pl.pallas_call(
        functools.partial(_attn_step_kernel, n_seq=seq_per_block, t=t),
        grid=(n_seq // seq_per_block,),
        in_specs=[pl.BlockSpec((rows, width), lambda i: (i, 0)),
                  pl.BlockSpec((rows, width), lambda i: (i, 1)),
                  pl.BlockSpec((rows, SWA_KV_WIDTH), lambda i: (i, kcol)),
                  pl.BlockSpec((rows, SWA_KV_WIDTH), lambda i: (i, kcol + 1)),
                  cache_spec, cache_spec,
                  pl.BlockSpec((1, SWA_Q_HEADS * t), lambda i: (0, 0))],
        out_specs=[pl.BlockSpec((rows, width), lambda i: (i, 0)), cache_spec, cache_spec],
        out_shape=[jax.ShapeDtypeStruct((n_seq * t, width), BF16),
                   jax.ShapeDtypeStruct(cache_k.shape, F32),
                   jax.ShapeDtypeStruct(cache_v.shape, F32)],
        compiler_params=pltpu.CompilerParams(
            dimension_semantics=("parallel",), vmem_limit_bytes=VMEM_LIMIT),
        name="attn_step",
    )(h1, h1, h1, h1, cache_k, cache_v, sink_cols)


def kernel(x_prompt, x_sample, state_hgrn, cache_swa_k, cache_swa_v, meta_tokens,
           hgrn_w_in, hgrn_lb_logits, hgrn_norm_w, hgrn_w_out,
           swa_w_in, swa_sinks, swa_w_out, ln_g, ln_b):
    out_dtype = x_prompt.dtype
    bsz, seq, d = x_prompt.shape
    dec_b, dec_t, _ = x_sample.shape
    n_p = bsz * seq
    n_s = dec_b * dec_t
    width = SWA_Q_HEADS * SWA_HEAD_DIM

    w_out0 = hgrn_w_out[0].astype(BF16)
    w_out1 = swa_w_out[0].astype(BF16)
    in_tile = 512
    q_tiles = width // in_tile
    kv_tile = 2 * SWA_KV_WIDTH // in_tile
    assert width % in_tile == 0 and (2 * SWA_KV_WIDTH) % in_tile == 0 and kv_tile == 1
    regroup = lambda j: jnp.where(j < q_tiles, j, jnp.where(j < 2 * q_tiles, j + kv_tile, q_tiles))
    lb = jnp.cumsum(jax.nn.softmax(hgrn_lb_logits.astype(F32), axis=0), axis=0)[0:1]
    nw = hgrn_norm_w[0].astype(F32).reshape(1, -1)
    sinks = swa_sinks[0].astype(F32).reshape(1, SWA_Q_HEADS)
    sink_cols = jnp.repeat(sinks, dec_t, axis=1)
    g0, b0 = ln_g[0:1].astype(F32), ln_b[0:1].astype(F32)
    g1, b1 = ln_g[1:2].astype(F32), ln_b[1:2].astype(F32)

    x_p = x_prompt.astype(F32).reshape(n_p, d)
    meta_block = jnp.concatenate(
        [jnp.zeros((SEQ_BLOCK - N_META, d), F32), meta_tokens.astype(F32)], axis=0)
    x_s = x_sample.astype(F32).reshape(n_s, d)
    meta_blk = n_s // SEQ_BLOCK

    h_sm, w_in0 = _matmul_cast(x_s, meta_block, hgrn_w_in[0].astype(F32), xs_rows=n_s, xm_block=0, tn=in_tile)
    h_p = _matmul(x_p, w_in0, tm=1024, tn=1024)
    zero_state = jnp.zeros((1, HGRN_HEADS, HGRN_DK, HGRN_DV), F32)
    og_meta, s_meta = _hgrn_seq(h_sm, lb, nw, zero_state, n_seq=1, seq_len=SEQ_BLOCK, row_block0=meta_blk)
    og_s, st_s = _hgrn_step(h_sm, lb, nw, state_hgrn[0].astype(F32), n_seq=dec_b, t=dec_t, seq_per_block=128)
    og_p, st_p = _hgrn_seq(h_p, lb, nw, s_meta, n_seq=bsz, seq_len=seq, row_block0=0)
    x1_sm = _matmul_deepnorm_tail(og_s, og_meta, w_out0, x_s, meta_block, g0, b0)
    x1_p = _matmul_deepnorm(og_p, w_out0, x_p, g0, b0, tm=512)

    h1_sm, w_in1 = _matmul_cast(x1_sm, x1_sm, swa_w_in[0].astype(F32), xs_rows=n_s, xm_block=meta_blk,
                                tn=in_tile, col_tile=regroup)
    h1_p = _matmul(x1_p, w_in1, tm=1024, tn=1536)
    og1_p = _attn_seq(h1_p, h1_sm, sinks.reshape(-1), n_seq=bsz, seq_len=seq, meta_row_block=meta_blk)
    ck = cache_swa_k[0].astype(F32).reshape(dec_b, WINDOW, SWA_KV_WIDTH)
    cv = cache_swa_v[0].astype(F32).reshape(dec_b, WINDOW, SWA_KV_WIDTH)
    og1_s, nk_s, nv_s = _attn_step(h1_sm, ck, cv, sink_cols, n_seq=dec_b, t=dec_t, seq_per_block=16)
    y_p = _matmul_deepnorm(og1_p, w_out1, x1_p, g1, b1, tm=512)
    y_s = _matmul_deepnorm(og1_s, w_out1, x1_sm, g1, b1, tm=512)

    kv_p = h1_p.reshape(bsz, seq, -1)[:, seq - WINDOW:, 2 * width:]
    cache_shape = (1, bsz, WINDOW, SWA_KV_HEADS, SWA_HEAD_DIM)
    return (y_p.reshape(bsz, seq, d).astype(out_dtype),
            y_s.reshape(dec_b, dec_t, d).astype(out_dtype),
            st_p[None].astype(out_dtype),
            st_s[None].astype(out_dtype),
            kv_p[..., :SWA_KV_WIDTH].reshape(cache_shape).astype(out_dtype),
            kv_p[..., SWA_KV_WIDTH:].reshape(cache_shape).astype(out_dtype),
            nk_s.reshape((1,) + cache_swa_k.shape[1:]).astype(out_dtype),
            nv_s.reshape((1,) + cache_swa_v.shape[1:]).astype(out_dtype))
```

```python
import functools

import numpy as np
import jax
import jax.numpy as jnp
from jax import lax
from jax.experimental import pallas as pl
from jax.experimental.pallas import tpu as pltpu

F32 = jnp.float32
BF16 = jnp.bfloat16

D_MODEL = 2048
N_META = 16
DEPTH = 2
HGRN_HEADS = 16
HGRN_DK = 128
HGRN_DV = 128
SWA_Q_HEADS = 32
SWA_KV_HEADS = 4
SWA_GROUP = 8
SWA_HEAD_DIM = 64
SWA_KV_WIDTH = SWA_KV_HEADS * SWA_HEAD_DIM
SWA_SCALE = SWA_HEAD_DIM ** -0.5
WINDOW = 128
DEEPNORM_ALPHA = (2.0 * DEPTH) ** 0.25
LN_EPS = 1e-5
RMS_EPS = 1e-6
LOG2E = 1.4426950408889634

CHUNK = 128
LEVELS = 7
MAX_FACTORED_LOG2_DECAY = 80.0
MASK_ROWS = 16
SEQ_BLOCK = 128
LN_SUB_ROWS = 128
STEP_GROUP = 16
SEQ_UNROLL = 4
HGRN_SEQ_HEADS = 8
HGRN_SEQ_ROWS = 512
VMEM_LIMIT = 56 * 1024 * 1024


def _dot(a, b):
    return jnp.dot(a, b, preferred_element_type=F32)


def _dot_nt(a, b):
    return lax.dot_general(a, b, (((1,), (1,)), ((), ())), preferred_element_type=F32)


def _dot_tn(a, b):
    return lax.dot_general(a, b, (((0,), (0,)), ((), ())), preferred_element_type=F32)


def _silu(x):
    return x * jax.nn.sigmoid(x)


def _mm_kernel(x_ref, w_ref, o_ref):
    o_ref[...] = _dot(x_ref[...].astype(BF16), w_ref[...]).astype(o_ref.dtype)


def _row_tile(n, want):
    if n <= want:
        return n
    return max(t for t in range(16, want + 1, 16) if n % t == 0)


def _matmul(x, w, *, tm, tn, out_dtype=F32):
    n, k = x.shape
    e = w.shape[1]
    tm = _row_tile(n, tm)
    assert n % tm == 0 and e % tn == 0
    return pl.pallas_call(
        _mm_kernel,
        grid=(n // tm, e // tn),
        in_specs=[pl.BlockSpec((tm, k), lambda i, j: (i, 0)),
                  pl.BlockSpec((k, tn), lambda i, j: (0, j))],
        out_specs=pl.BlockSpec((tm, tn), lambda i, j: (i, j)),
        out_shape=jax.ShapeDtypeStruct((n, e), out_dtype),
        compiler_params=pltpu.CompilerParams(
            dimension_semantics=("parallel", "parallel"), vmem_limit_bytes=VMEM_LIMIT),
        name="proj_in",
    )(x, w)


def _mm_cast_kernel(xs_ref, xm_ref, w_ref, o_ref, wb_ref, xb_scr):
    @pl.when(pl.program_id(0) == 0)
    def _():
        n_s = xs_ref.shape[0]
        xb_scr[0:n_s, :] = xs_ref[...].astype(BF16)
        xb_scr[n_s:, :] = xm_ref[...].astype(BF16)

    wb = w_ref[...].astype(BF16)
    wb_ref[...] = wb
    o_ref[...] = _dot(xb_scr[...], wb)


def _matmul_cast(xs, xm, w, *, xs_rows, xm_block, tn, col_tile=lambda j: j):
    k, e = w.shape
    n = xs_rows + SEQ_BLOCK
    assert e % tn == 0
    return pl.pallas_call(
        _mm_cast_kernel,
        grid=(e // tn,),
        in_specs=[pl.BlockSpec((xs_rows, k), lambda j: (0, 0)),
                  pl.BlockSpec((SEQ_BLOCK, k), lambda j: (xm_block, 0)),
                  pl.BlockSpec((k, tn), lambda j: (0, col_tile(j)))],
        out_specs=[pl.BlockSpec((n, tn), lambda j: (0, j)),
                   pl.BlockSpec((k, tn), lambda j: (0, j))],
        out_shape=[jax.ShapeDtypeStruct((n, e), F32), jax.ShapeDtypeStruct((k, e), BF16)],
        scratch_shapes=[pltpu.VMEM((n, k), BF16)],
        compiler_params=pltpu.CompilerParams(
            dimension_semantics=("arbitrary",), vmem_limit_bytes=VMEM_LIMIT),
        name="proj_in_cast",
    )(xs, xm, w)


def _mm_ln_tail_kernel(a_ref, at_ref, w_ref, x_ref, xt_ref, g_ref, b_ref, o_ref):
    tail = pl.program_id(0) == pl.num_programs(0) - 1
    a = jnp.where(tail, at_ref[...], a_ref[...])
    x = jnp.where(tail, xt_ref[...], x_ref[...])
    z = DEEPNORM_ALPHA * x + _dot(a, w_ref[...])
    mu = jnp.mean(z, axis=-1, keepdims=True)
    zc = z - mu
    var = jnp.mean(zc * zc, axis=-1, keepdims=True)
    o_ref[...] = zc * lax.rsqrt(var + LN_EPS) * g_ref[...] + b_ref[...]


def _matmul_deepnorm_tail(a, a_tail, w, x, x_tail, g, b):
    n, k = a.shape
    d = w.shape[1]
    tm = SEQ_BLOCK
    last = n // tm - 1
    body = lambda i: (jnp.minimum(i, last), 0)
    const = lambda i: (0, 0)
    return pl.pallas_call(
        _mm_ln_tail_kernel,
        grid=(n // tm + 1,),
        in_specs=[pl.BlockSpec((tm, k), body), pl.BlockSpec((tm, k), const),
                  pl.BlockSpec((k, d), const),
                  pl.BlockSpec((tm, d), body), pl.BlockSpec((tm, d), const),
                  pl.BlockSpec((1, d), const), pl.BlockSpec((1, d), const)],
        out_specs=pl.BlockSpec((tm, d), lambda i: (i, 0)),
        out_shape=jax.ShapeDtypeStruct((n + tm, d), F32),
        compiler_params=pltpu.CompilerParams(
            dimension_semantics=("parallel",), vmem_limit_bytes=VMEM_LIMIT),
        name="proj_out_deepnorm_tail",
    )(a, a_tail, w, x, x_tail, g, b)


def _mm_ln_kernel(a_ref, w_ref, x_ref, g_ref, b_ref, o_ref):
    tm = a_ref.shape[0]
    sub = LN_SUB_ROWS if tm % LN_SUB_ROWS == 0 else tm
    for r0 in range(0, tm, sub):
        rows = slice(r0, r0 + sub)
        z = DEEPNORM_ALPHA * x_ref[rows, :] + _dot(a_ref[rows, :], w_ref[...])
        mu = jnp.mean(z, axis=-1, keepdims=True)
        zc = z - mu
        var = jnp.mean(zc * zc, axis=-1, keepdims=True)
        o_ref[rows, :] = zc * lax.rsqrt(var + LN_EPS) * g_ref[...] + b_ref[...]


def _matmul_deepnorm(a, w, x, g, b, *, tm):
    n, k = a.shape
    d = w.shape[1]
    tm = _row_tile(n, tm)
    return pl.pallas_call(
        _mm_ln_kernel,
        grid=(n // tm,),
        in_specs=[pl.BlockSpec((tm, k), lambda i: (i, 0)),
                  pl.BlockSpec((k, d), lambda i: (0, 0)),
                  pl.BlockSpec((tm, d), lambda i: (i, 0)),
                  pl.BlockSpec((1, d), lambda i: (0, 0)),
                  pl.BlockSpec((1, d), lambda i: (0, 0))],
        out_specs=pl.BlockSpec((tm, d), lambda i: (i, 0)),
        out_shape=jax.ShapeDtypeStruct((n, d), F32),
        compiler_params=pltpu.CompilerParams(
            dimension_semantics=("parallel",), vmem_limit_bytes=VMEM_LIMIT),
        name="proj_out_deepnorm",
    )(a, w, x, g, b)


def _level_masks():
    t = np.arange(CHUNK)
    out = np.zeros((LEVELS + 1, CHUNK, CHUNK), np.float32)
    out[0] = t[:, None] > t[None, :]
    for l in range(LEVELS):
        h = 1 << l
        same = (t[:, None] >> (l + 1)) == (t[None, :] >> (l + 1))
        out[l + 1] = same & ((t[:, None] & h) != 0) & ((t[None, :] & h) == 0)
    return out


def _gates(fx, lb):
    f = lb + (1.0 - lb) * jax.nn.sigmoid(fx)
    return f, jnp.log2(f), 1.0 - f


def _prefix_scan(logf):
    tiles = (CHUNK // 8, 8, HGRN_DK)
    c = logf.reshape(tiles)
    sub = lax.broadcasted_iota(jnp.int32, (1, 8, HGRN_DK), 1)
    bcast = lambda x, r: jnp.broadcast_to(x[:, r:r + 1, :], tiles)
    c = c + jnp.where((sub & 1) != 0, pltpu.roll(c, 1, 1), 0.0)
    c = c + jnp.where((sub & 2) != 0, jnp.where(sub < 4, bcast(c, 1), bcast(c, 5)), 0.0)
    c = c + jnp.where((sub & 4) != 0, bcast(c, 3), 0.0)
    c = c.reshape(CHUNK, HGRN_DK)
    for level in range(3, LEVELS):
        half = 1 << level
        pieces = []
        for r0 in range(0, CHUNK, 2 * half):
            pieces += [c[r0:r0 + half], c[r0 + half:r0 + 2 * half] + c[r0 + half - 1:r0 + half, :]]
        c = jnp.concatenate(pieces, axis=0)
    return c


def _small_levels(logf, f, q, kk):
    tiles = (CHUNK // 8, 8, HGRN_DK)
    c, f3, q3, k3 = (x.reshape(tiles) for x in (logf, f, q, kk))
    sub = lax.broadcasted_iota(jnp.int32, (1, 8, HGRN_DK), 1)
    bcast = lambda x, r: jnp.broadcast_to(x[:, r:r + 1, :], tiles)
    up = (sub & 1) != 0
    zs = [jnp.where(up, q3 * f3, k3)]
    c = c + jnp.where(up, pltpu.roll(c, 1, 1), 0.0)
    for level, tot in ((1, lambda c: jnp.where(sub < 4, bcast(c, 1), bcast(c, 5))),
                       (2, lambda c: bcast(c, 3))):
        up = (sub & (1 << level)) != 0
        t = tot(c)
        zs.append(jnp.where(up, q3, k3) * jnp.exp2(jnp.where(up, c, t - c)))
        c = c + jnp.where(up, t, 0.0)
    return [z.reshape(CHUNK, HGRN_DK) for z in zs], c.reshape(CHUNK, HGRN_DK)


def _level_large(cum, q, kk, level):
    half = 1 << level
    args, bases, cums = [], [], []
    for r0 in range(0, CHUNK, 2 * half):
        lo, up = slice(r0, r0 + half), slice(r0 + half, r0 + 2 * half)
        tot = cum[r0 + half - 1:r0 + half, :]
        args += [tot - cum[lo], cum[up]]
        bases += [kk[lo], q[up]]
        cums += [cum[lo], cum[up] + tot]
    x = jnp.exp2(jnp.concatenate(args, axis=0))
    return jnp.concatenate(bases, axis=0) * x, jnp.concatenate(cums, axis=0)


def _split3(x):
    hi = x.astype(BF16)
    r = x - hi.astype(F32)
    mid = r.astype(BF16)
    lo = (r - mid.astype(F32)).astype(BF16)
    return hi, mid, lo


def _state_update(s_old, kd, v, blast, pad_rows):
    c = kd.shape[0]
    hi, mid, lo = _split3(jnp.exp2(blast))
    row = lax.broadcasted_iota(jnp.int32, (pad_rows, HGRN_DK), 0)
    dec = jnp.where(row == 0, hi.astype(F32),
                    jnp.where(row == 1, mid.astype(F32), jnp.where(row == 2, lo.astype(F32), 0.0)))
    piece = BF16 if c % 16 == 0 and pad_rows % 16 == 0 else F32
    lhs = jnp.concatenate([kd.astype(piece), dec.astype(piece)], axis=0).astype(BF16)
    rhs = jnp.concatenate(
        [jnp.concatenate([v.astype(piece), jnp.zeros((c, HGRN_DV), piece)], axis=1),
         jnp.concatenate([jnp.zeros((pad_rows, HGRN_DV), piece), jnp.ones((pad_rows, HGRN_DV), piece)], axis=1)],
        axis=0).astype(BF16)
    both = _dot_tn(lhs, rhs)
    return both[:, HGRN_DV:] * s_old + both[:, :HGRN_DV]


def _norm_gate(o, g, nw):
    o = o * lax.rsqrt(jnp.mean(o * o, axis=-1, keepdims=True) + RMS_EPS) * nw
    return o * _silu(g)


def _hgrn_seq_kernel(q_ref, fx_ref, i_ref, g_ref, lb_ref, nw_ref, s0_ref, m_ref,
                     og_ref, sout_ref, s_scr, o_scr, *, n_chunks):
    heads = s_scr.shape[0]

    @pl.when(pl.program_id(2) == 0)
    def _():
        s_scr[...] = s0_ref[0]

    def chunk(c, carry):
        rows = pl.ds(pl.multiple_of(c * CHUNK, CHUNK), CHUNK)
        hcols = [slice(hd * HGRN_DK, (hd + 1) * HGRN_DK) for hd in range(heads)]
        q = [q_ref[rows, cs] for cs in hcols]
        v = [i_ref[rows, cs] for cs in hcols]
        gates = [_gates(fx_ref[rows, cs], lb_ref[:, cs]) for cs in hcols]
        kk = [gt[2] for gt in gates]
        cum = [_prefix_scan(gt[1]) for gt in gates]
        mid = CHUNK // 2 - 1
        spread = None
        for b in cum:
            s_hd = jnp.maximum(b[0:1, :] - b[mid:mid + 1, :], b[mid:mid + 1, :] - b[CHUNK - 1:CHUNK, :])
            spread = s_hd if spread is None else jnp.maximum(spread, s_hd)
        factorable = jnp.max(spread) <= MAX_FACTORED_LOG2_DECAY

        vb = [x.astype(BF16) for x in v]
        for hd in range(heads):
            d = cum[hd] - cum[hd][mid:mid + 1, :]
            qe = (q[hd] * jnp.exp2(d)).astype(BF16)
            ke = (kk[hd] * jnp.exp2(-d)).astype(BF16)
            amat = jnp.where(m_ref[0] > 0.5, _dot_nt(qe, ke), 0.0).astype(BF16)
            blast = cum[hd][CHUNK - 1:CHUNK, :]
            qd = (q[hd] * jnp.exp2(cum[hd])).astype(BF16)
            kd = kk[hd] * jnp.exp2(blast - cum[hd])
            s_old = s_scr[hd]
            base = _dot(qd, s_old.astype(BF16)) + jnp.sum(q[hd] * kk[hd], axis=-1, keepdims=True) * v[hd]
            o_scr[hd] = base
            s_scr[hd] = _state_update(s_old, kd, vb[hd], blast, 16)
            og_ref[rows, hcols[hd]] = _norm_gate(
                base + _dot(amat, vb[hd]), g_ref[rows, hcols[hd]], nw_ref[:, hcols[hd]]).astype(og_ref.dtype)

        @pl.when(jnp.logical_not(factorable))
        def _():
            small = [_small_levels(gt[1], gt[0], q[hd], kk[hd]) for hd, gt in enumerate(gates)]
            zs = [sm[0] for sm in small]
            part = [sm[1] for sm in small]
            for level in range(3, LEVELS):
                for hd in range(heads):
                    z, part[hd] = _level_large(part[hd], q[hd], kk[hd], level)
                    zs[hd].append(z)
            nblk = CHUNK // MASK_ROWS
            a = [[None] * nblk for _ in range(heads)]
            for level in range(LEVELS):
                step = 1 << (level - 4) if level >= 4 else 0
                blocks = [i for i in range(nblk) if level < 4 or (i & step)]
                for hd in range(heads):
                    z = zs[hd][level].astype(BF16)
                    lhs = z if level < 4 else jnp.concatenate(
                        [z[i * MASK_ROWS:(i + 1) * MASK_ROWS] for i in blocks], axis=0)
                    term = _dot_nt(lhs, z)
                    for n, i in enumerate(blocks):
                        t = (term[n * MASK_ROWS:(n + 1) * MASK_ROWS]
                             * m_ref[level + 1, i * MASK_ROWS:(i + 1) * MASK_ROWS, :])
                        a[hd][i] = t if a[hd][i] is None else a[hd][i] + t
            for hd in range(heads):
                amat = jnp.concatenate(a[hd], axis=0).astype(BF16)
                og_ref[rows, hcols[hd]] = _norm_gate(
                    o_scr[hd] + _dot(amat, vb[hd]),
                    g_ref[rows, hcols[hd]], nw_ref[:, hcols[hd]]).astype(og_ref.dtype)
        return carry

    lax.fori_loop(0, n_chunks, chunk, 0)

    @pl.when(pl.program_id(2) == pl.num_programs(2) - 1)
    def _():
        sout_ref[0] = s_scr[...]


def _hgrn_seq(h, lb, nw, s0, *, n_seq, seq_len, row_block0):
    hb = HGRN_SEQ_HEADS
    width = hb * HGRN_DK
    groups = HGRN_HEADS // hb
    rb = min(HGRN_SEQ_ROWS, seq_len)
    nrb = seq_len // rb
    assert seq_len % rb == 0 and rb % CHUNK == 0 and (row_block0 * SEQ_BLOCK) % rb == 0
    rb0 = row_block0 * SEQ_BLOCK // rb
    s0_batched = s0.shape[0] != 1
    col = lambda part: (lambda b, hg, r: (rb0 + b * nrb + r, part * groups + hg))
    masks = jnp.asarray(_level_masks())
    return pl.pallas_call(
        functools.partial(_hgrn_seq_kernel, n_chunks=rb // CHUNK),
        grid=(n_seq, groups, nrb),
        in_specs=[pl.BlockSpec((rb, width), col(0)),
                  pl.BlockSpec((rb, width), col(1)),
                  pl.BlockSpec((rb, width), col(2)),
                  pl.BlockSpec((rb, width), col(3)),
                  pl.BlockSpec((1, width), lambda b, hg, r: (0, hg)),
                  pl.BlockSpec((1, width), lambda b, hg, r: (0, hg)),
                  pl.BlockSpec((1, hb, HGRN_DK, HGRN_DV),
                               (lambda b, hg, r: (b, hg, 0, 0)) if s0_batched else (lambda b, hg, r: (0, hg, 0, 0))),
                  pl.BlockSpec((LEVELS + 1, CHUNK, CHUNK), lambda b, hg, r: (0, 0, 0))],
        out_specs=[pl.BlockSpec((rb, width), lambda b, hg, r: (b * nrb + r, hg)),
                   pl.BlockSpec((1, hb, HGRN_DK, HGRN_DV), lambda b, hg, r: (b, hg, 0, 0))],
        out_shape=[jax.ShapeDtypeStruct((n_seq * seq_len, HGRN_HEADS * HGRN_DV), BF16),
                   jax.ShapeDtypeStruct((n_seq, HGRN_HEADS, HGRN_DK, HGRN_DV), F32)],
        scratch_shapes=[pltpu.VMEM((hb, HGRN_DK, HGRN_DV), F32),
                        pltpu.VMEM((hb, CHUNK, HGRN_DV), F32)],
        compiler_params=pltpu.CompilerParams(
            dimension_semantics=("parallel", "parallel", "arbitrary"), vmem_limit_bytes=VMEM_LIMIT),
        name="hgrn_seq",
    )(h, h, h, h, lb, nw, s0, masks)


def _hgrn_step_kernel(q_ref, fx_ref, i_ref, g_ref, lb_ref, nw_ref, s_ref, og_ref, sout_ref, *, n_seq, t):
    lb = lb_ref[...]
    nw = nw_ref[...]
    tile = (STEP_GROUP, t, HGRN_DK)
    sub = lax.broadcasted_iota(jnp.int32, (1, t, HGRN_DK), 1)

    def group(i, carry):
        rows = pl.ds(pl.multiple_of(i * (STEP_GROUP * t), STEP_GROUP * t), STEP_GROUP * t)
        q = q_ref[rows, :].reshape(tile)
        v = i_ref[rows, :].reshape(tile)
        _, logf, kk = _gates(fx_ref[rows, :].reshape(tile), lb)
        cum = logf
        shift = 1
        while shift < t:
            cum = cum + jnp.where(sub >= shift, pltpu.roll(cum, shift, 1), 0.0)
            shift *= 2
        intra = (jnp.sum(q * kk, axis=-1, keepdims=True)) * v
        for d in range(1, t):
            valid = sub >= d
            x = jnp.exp2(jnp.where(valid, cum - pltpu.roll(cum, d, 1), 0.0))
            w = jnp.sum(jnp.where(valid, q * x * pltpu.roll(kk, d, 1), 0.0), axis=-1, keepdims=True)
            intra = intra + w * pltpu.roll(v, d, 1)
        blast = cum[:, t - 1:t, :]
        qd = q * jnp.exp2(cum)
        kd = kk * jnp.exp2(blast - cum)
        seqs = [i * STEP_GROUP + n for n in range(STEP_GROUP)]
        inter = [_dot(qd[n].astype(BF16), s_ref[b, 0].astype(BF16)) for n, b in enumerate(seqs)]
        for n, b in enumerate(seqs):
            sout_ref[b, 0] = _state_update(s_ref[b, 0], kd[n], v[n], blast[n], 8)
        o = jnp.stack(inter, axis=0) + intra
        og = _norm_gate(o, g_ref[rows, :].reshape(tile), nw)
        og_ref[rows, :] = og.reshape(STEP_GROUP * t, HGRN_DK).astype(og_ref.dtype)
        return carry

    lax.fori_loop(0, n_seq // STEP_GROUP, group, 0)


def _hgrn_step(h, lb, nw, s0, *, n_seq, t, seq_per_block):
    seq_per_block = min(seq_per_block, n_seq)
    assert n_seq % seq_per_block == 0 and seq_per_block % STEP_GROUP == 0 and t == 8
    rows = seq_per_block * t
    col = lambda part: (lambda i, hd: (i, part * HGRN_HEADS + hd))
    return pl.pallas_call(
        functools.partial(_hgrn_step_kernel, n_seq=seq_per_block, t=t),
        grid=(n_seq // seq_per_block, HGRN_HEADS),
        in_specs=[pl.BlockSpec((rows, HGRN_DK), col(0)),
                  pl.BlockSpec((rows, HGRN_DK), col(1)),
                  pl.BlockSpec((rows, HGRN_DV), col(2)),
                  pl.BlockSpec((rows, HGRN_DV), col(3)),
                  pl.BlockSpec((1, HGRN_DK), lambda i, hd: (0, hd)),
                  pl.BlockSpec((1, HGRN_DV), lambda i, hd: (0, hd)),
                  pl.BlockSpec((seq_per_block, 1, HGRN_DK, HGRN_DV), lambda i, hd: (i, hd, 0, 0))],
        out_specs=[pl.BlockSpec((rows, HGRN_DV), lambda i, hd: (i, hd)),
                   pl.BlockSpec((seq_per_block, 1, HGRN_DK, HGRN_DV), lambda i, hd: (i, hd, 0, 0))],
        out_shape=[jax.ShapeDtypeStruct((n_seq * t, HGRN_HEADS * HGRN_DV), BF16),
                   jax.ShapeDtypeStruct((n_seq, HGRN_HEADS, HGRN_DK, HGRN_DV), F32)],
        compiler_params=pltpu.CompilerParams(
            dimension_semantics=("parallel", "parallel"), vmem_limit_bytes=VMEM_LIMIT),
        name="hgrn_step",
    )(h, h, h, h, lb, nw, s0)


def _attn_seq_kernel(q_ref, g_ref, kc_ref, vc_ref, kp_ref, vp_ref, meta_ref, sink_ref, og_ref):
    first = pl.program_id(1) == 0
    kprev = jnp.where(first, meta_ref[:, :SWA_KV_WIDTH], kp_ref[...])
    vprev = jnp.where(first, meta_ref[:, SWA_KV_WIDTH:], vp_ref[...])
    kband = (jnp.concatenate([kprev, kc_ref[...]], axis=0) * (SWA_SCALE * LOG2E)).astype(BF16)
    vband_t = jnp.concatenate([vprev, vc_ref[...]], axis=0).T.astype(BF16)
    zeros_t = jnp.zeros((SWA_HEAD_DIM, 2 * SEQ_BLOCK), BF16)
    kj = lax.broadcasted_iota(jnp.int32, (2 * SEQ_BLOCK, SEQ_BLOCK), 0)
    qi = lax.broadcasted_iota(jnp.int32, (2 * SEQ_BLOCK, SEQ_BLOCK), 1)
    dist = SEQ_BLOCK + qi - kj
    valid = (dist >= 0) & (dist < WINDOW) & (jnp.logical_not(first) | (kj >= SEQ_BLOCK - N_META))
    madd = jnp.where(valid, 0.0, -jnp.inf)
    zeros = jnp.zeros((2 * SEQ_BLOCK, SWA_HEAD_DIM), BF16)
    pairs = SWA_GROUP // 2
    pw = 2 * SWA_HEAD_DIM

    def block_diag(x):
        return jnp.concatenate([jnp.concatenate([x, zeros], axis=1),
                                jnp.concatenate([zeros, x], axis=1)], axis=0)

    sts = []
    for kvh in range(SWA_KV_HEADS):
        ks = slice(kvh * SWA_HEAD_DIM, (kvh + 1) * SWA_HEAD_DIM)
        col0 = kvh * SWA_GROUP * SWA_HEAD_DIM
        xq = jnp.concatenate([q_ref[:, col0 + p * pw:col0 + (p + 1) * pw].astype(BF16)
                              for p in range(pairs)], axis=0)
        sts.append(_dot_nt(block_diag(kband[:, ks]), xq))
    for kvh in range(SWA_KV_HEADS):
        ks = slice(kvh * SWA_HEAD_DIM, (kvh + 1) * SWA_HEAD_DIM)
        col0 = kvh * SWA_GROUP * SWA_HEAD_DIM
        vt = vband_t[ks, :]
        v2t = jnp.concatenate([jnp.concatenate([vt, zeros_t], axis=1),
                               jnp.concatenate([zeros_t, vt], axis=1)], axis=0)
        pbs, rdens = [], []
        for pr in range(pairs):
            head = kvh * SWA_GROUP + 2 * pr
            s2 = (sts[kvh][:, pr * SEQ_BLOCK:(pr + 1) * SEQ_BLOCK].reshape(2, 2 * SEQ_BLOCK, SEQ_BLOCK)
                  + madd[None])
            sink = jnp.concatenate([jnp.full((1, 1, SEQ_BLOCK), sink_ref[head] * LOG2E, F32),
                                    jnp.full((1, 1, SEQ_BLOCK), sink_ref[head + 1] * LOG2E, F32)], axis=0)
            m = jnp.maximum(jnp.max(s2, axis=1, keepdims=True), sink)
            p = jnp.exp2(s2 - m)
            den = jnp.sum(p, axis=1, keepdims=True) + jnp.exp2(sink - m)
            pbs.append(p.astype(BF16).reshape(4 * SEQ_BLOCK, SEQ_BLOCK))
            rdens.append(1.0 / den)
        for pr in range(pairs):
            ot = _dot(v2t, pbs[pr]).reshape(2, SWA_HEAD_DIM, SEQ_BLOCK) * rdens[pr]
            o = ot.reshape(2 * SWA_HEAD_DIM, SEQ_BLOCK).T
            cs = slice(col0 + pr * pw, col0 + (pr + 1) * pw)
            og_ref[:, cs] = (o * _silu(g_ref[:, cs])).astype(og_ref.dtype)


def _attn_seq(h1, h1_meta, sinks, *, n_seq, seq_len, meta_row_block):
    nb = seq_len // SEQ_BLOCK
    width = SWA_Q_HEADS * SWA_HEAD_DIM
    kcol = 2 * width // SWA_KV_WIDTH
    return pl.pallas_call(
        _attn_seq_kernel,
        grid=(n_seq, nb),
        in_specs=[pl.BlockSpec((SEQ_BLOCK, width), lambda b, j: (b * nb + j, 0)),
                  pl.BlockSpec((SEQ_BLOCK, width), lambda b, j: (b * nb + j, 1)),
                  pl.BlockSpec((SEQ_BLOCK, SWA_KV_WIDTH), lambda b, j: (b * nb + j, kcol)),
                  pl.BlockSpec((SEQ_BLOCK, SWA_KV_WIDTH), lambda b, j: (b * nb + j, kcol + 1)),
                  pl.BlockSpec((SEQ_BLOCK, SWA_KV_WIDTH), lambda b, j: (b * nb + jnp.maximum(j - 1, 0), kcol)),
                  pl.BlockSpec((SEQ_BLOCK, SWA_KV_WIDTH), lambda b, j: (b * nb + jnp.maximum(j - 1, 0), kcol + 1)),
                  pl.BlockSpec((SEQ_BLOCK, 2 * SWA_KV_WIDTH), lambda b, j: (meta_row_block, kcol // 2)),
                  pl.BlockSpec(memory_space=pltpu.SMEM)],
        out_specs=pl.BlockSpec((SEQ_BLOCK, width), lambda b, j: (b * nb + j, 0)),
        out_shape=jax.ShapeDtypeStruct((n_seq * seq_len, width), BF16),
        compiler_params=pltpu.CompilerParams(
            dimension_semantics=("parallel", "arbitrary"), vmem_limit_bytes=VMEM_LIMIT),
        name="attn_seq",
    )(h1, h1, h1, h1, h1, h1, h1_meta, sinks)


def _attn_step_kernel(q_ref, g_ref, kn_ref, vn_ref, ck_ref, cv_ref, sink_ref, og_ref, nk_ref, nv_ref, *, n_seq, t):
    keys = WINDOW + t
    hd = SWA_HEAD_DIM
    tiles = SWA_Q_HEADS // 2
    kj = lax.broadcasted_iota(jnp.int32, (keys, SWA_Q_HEADS * t), 0)
    qt = lax.broadcasted_iota(jnp.int32, (keys, SWA_Q_HEADS * t), 1) % t
    madd = jnp.where((kj >= qt + 1) & (kj <= WINDOW + qt), 0.0, -jnp.inf)
    low = lax.broadcasted_iota(jnp.int32, (t, 2 * hd), 1) < hd
    zero_tile = jnp.zeros((t, 2 * hd), F32)
    sink = sink_ref[...] * LOG2E

    def group(i, carry):
        seqs = [i * SEQ_UNROLL + n for n in range(SEQ_UNROLL)]
        rows = [pl.ds(pl.multiple_of(b * t, t), t) for b in seqs]
        st, vall = [], []
        for b, rw in zip(seqs, rows):
            kc, vc = ck_ref[b], cv_ref[b]
            kn, vn = kn_ref[rw, :], vn_ref[rw, :]
            nk_ref[b, 0:WINDOW - t, :] = kc[t:, :]
            nk_ref[b, WINDOW - t:WINDOW, :] = kn
            nv_ref[b, 0:WINDOW - t, :] = vc[t:, :]
            nv_ref[b, WINDOW - t:WINDOW, :] = vn
            kall = jnp.concatenate([kc, kn], axis=0).astype(BF16)
            vall.append(jnp.concatenate([vc, vn], axis=0).astype(BF16))
            q = q_ref[rw, :] * (SWA_SCALE * LOG2E)
            qtile = [q[:, j * 2 * hd:(j + 1) * 2 * hd] for j in range(tiles)]
            qswap = [pltpu.roll(x, hd, 1) for x in qtile]
            groups = []
            for kvh in range(SWA_KV_HEADS):
                for gq in range(SWA_GROUP):
                    j = (kvh * SWA_GROUP + gq) // 2
                    src = qtile[j] if gq % 2 == kvh % 2 else qswap[j]
                    half = jnp.where(low, src, 0.0) if kvh % 2 == 0 else jnp.where(low, 0.0, src)
                    groups.append(jnp.concatenate(
                        [half if c == kvh // 2 else zero_tile for c in range(SWA_KV_HEADS // 2)], axis=1))
            qbd = jnp.concatenate(groups, axis=0).astype(BF16)
            st.append(_dot_nt(kall, qbd))
        pn = []
        for s in st:
            s = s + madd
            m = jnp.maximum(jnp.max(s, axis=0, keepdims=True), sink)
            p = jnp.exp2(s - m)
            den = jnp.sum(p, axis=0, keepdims=True) + jnp.exp2(sink - m)
            pn.append((p * (1.0 / den)).astype(BF16))
        of = [_dot_tn(p, vl) for p, vl in zip(pn, vall)]
        for o, rw in zip(of, rows):
            g = g_ref[rw, :]
            out = []
            for j in range(tiles):
                kvh = (2 * j) // SWA_GROUP
                ct = slice((kvh // 2) * 2 * hd, (kvh // 2 + 1) * 2 * hd)
                ra = o[(2 * j) * t:(2 * j + 1) * t, ct]
                rb = o[(2 * j + 1) * t:(2 * j + 2) * t, ct]
                if kvh % 2 == 0:
                    out.append(jnp.where(low, ra, pltpu.roll(rb, hd, 1)))
                else:
                    out.append(jnp.where(low, pltpu.roll(ra, hd, 1), rb))
            og_ref[rw, :] = (jnp.concatenate(out, axis=1) * _silu(g)).astype(og_ref.dtype)
        return carry

    lax.fori_loop(0, n_seq // SEQ_UNROLL, group, 0)


def _attn_step(h1, cache_k, cache_v, sink_cols, *, n_seq, t, seq_per_block):
    assert n_seq % seq_per_block == 0
    rows = seq_per_block * t
    width = SWA_Q_HEADS * SWA_HEAD_DIM
    kcol = 2 * width // SWA_KV_WIDTH
    cache_spec = pl.BlockSpec((seq_per_block, WINDOW, SWA_KV_WIDTH), lambda i: (i, 0, 0))
    return pl.pallas_call(
        functools.partial(_attn_step_kernel, n_seq=seq_per_block, t=t),
        grid=(n_seq // seq_per_block,),
        in_specs=[pl.BlockSpec((rows, width), lambda i: (i, 0)),
                  pl.BlockSpec((rows, width), lambda i: (i, 1)),
                  pl.BlockSpec((rows, SWA_KV_WIDTH), lambda i: (i, kcol)),
                  pl.BlockSpec((rows, SWA_KV_WIDTH), lambda i: (i, kcol + 1)),
                  cache_spec, cache_spec,
                  pl.BlockSpec((1, SWA_Q_HEADS * t), lambda i: (0, 0))],
        out_specs=[pl.BlockSpec((rows, width), lambda i: (i, 0)), cache_spec, cache_spec],
        out_shape=[jax.ShapeDtypeStruct((n_seq * t, width), BF16),
                   jax.ShapeDtypeStruct(cache_k.shape, F32),
                   jax.ShapeDtypeStruct(cache_v.shape, F32)],
        compiler_params=pltpu.CompilerParams(
            dimension_semantics=("parallel",), vmem_limit_bytes=VMEM_LIMIT),
        name="attn_step",
    )(h1, h1, h1, h1, cache_k, cache_v, sink_cols)


def kernel(x_prompt, x_sample, state_hgrn, cache_swa_k, cache_swa_v, meta_tokens,
           hgrn_w_in, hgrn_lb_logits, hgrn_norm_w, hgrn_w_out,
           swa_w_in, swa_sinks, swa_w_out, ln_g, ln_b):
    out_dtype = x_prompt.dtype
    bsz, seq, d = x_prompt.shape
    dec_b, dec_t, _ = x_sample.shape
    n_p = bsz * seq
    n_s = dec_b * dec_t
    width = SWA_Q_HEADS * SWA_HEAD_DIM

    w_out0 = hgrn_w_out[0].astype(BF16)
    w_out1 = swa_w_out[0].astype(BF16)
    in_tile = 512
    q_tiles = width // in_tile
    kv_tile = 2 * SWA_KV_WIDTH // in_tile
    assert width % in_tile == 0 and (2 * SWA_KV_WIDTH) % in_tile == 0 and kv_tile == 1
    regroup = lambda j: jnp.where(j < q_tiles, j, jnp.where(j < 2 * q_tiles, j + kv_tile, q_tiles))
    lb = jnp.cumsum(jax.nn.softmax(hgrn_lb_logits.astype(F32), axis=0), axis=0)[0:1]
    nw = hgrn_norm_w[0].astype(F32).reshape(1, -1)
    sinks = swa_sinks[0].astype(F32).reshape(1, SWA_Q_HEADS)
    sink_cols = jnp.repeat(sinks, dec_t, axis=1)
    g0, b0 = ln_g[0:1].astype(F32), ln_b[0:1].astype(F32)
    g1, b1 = ln_g[1:2].astype(F32), ln_b[1:2].astype(F32)

    x_p = x_prompt.astype(F32).reshape(n_p, d)
    meta_block = jnp.concatenate(
        [jnp.zeros((SEQ_BLOCK - N_META, d), F32), meta_tokens.astype(F32)], axis=0)
    x_s = x_sample.astype(F32).reshape(n_s, d)
    meta_blk = n_s // SEQ_BLOCK

    h_sm, w_in0 = _matmul_cast(x_s, meta_block, hgrn_w_in[0].astype(F32), xs_rows=n_s, xm_block=0, tn=in_tile)
    h_p = _matmul(x_p, w_in0, tm=1024, tn=1024)
    zero_state = jnp.zeros((1, HGRN_HEADS, HGRN_DK, HGRN_DV), F32)
    og_meta, s_meta = _hgrn_seq(h_sm, lb, nw, zero_state, n_seq=1, seq_len=SEQ_BLOCK, row_block0=meta_blk)
    og_s, st_s = _hgrn_step(h_sm, lb, nw, state_hgrn[0].astype(F32), n_seq=dec_b, t=dec_t, seq_per_block=128)
    og_p, st_p = _hgrn_seq(h_p, lb, nw, s_meta, n_seq=bsz, seq_len=seq, row_block0=0)
    x1_sm = _matmul_deepnorm_tail(og_s, og_meta, w_out0, x_s, meta_block, g0, b0)
    x1_p = _matmul_deepnorm(og_p, w_out0, x_p, g0, b0, tm=512)

    h1_sm, w_in1 = _matmul_cast(x1_sm, x1_sm, swa_w_in[0].astype(F32), xs_rows=n_s, xm_block=meta_blk,
                                tn=in_tile, col_tile=regroup)
    h1_p = _matmul(x1_p, w_in1, tm=1024, tn=1536)
    og1_p = _attn_seq(h1_p, h1_sm, sinks.reshape(-1), n_seq=bsz, seq_len=seq, meta_row_block=meta_blk)
    ck = cache_swa_k[0].astype(F32).reshape(dec_b, WINDOW, SWA_KV_WIDTH)
    cv = cache_swa_v[0].astype(F32).reshape(dec_b, WINDOW, SWA_KV_WIDTH)
    og1_s, nk_s, nv_s = _attn_step(h1_sm, ck, cv, sink_cols, n_seq=dec_b, t=dec_t, seq_per_block=16)
    y_p = _matmul_deepnorm(og1_p, w_out1, x1_p, g1, b1, tm=512)
    y_s = _matmul_deepnorm(og1_s, w_out1, x1_sm, g1, b1, tm=512)

    kv_p = h1_p.reshape(bsz, seq, -1)[:, seq - WINDOW:, 2 * width:]
    cache_shape = (1, bsz, WINDOW, SWA_KV_HEADS, SWA_HEAD_DIM)
    return (y_p.reshape(bsz, seq, d).astype(out_dtype),
            y_s.reshape(dec_b, dec_t, d).astype(out_dtype),
            st_p[None].astype(out_dtype),
            st_s[None].astype(out_dtype),
            kv_p[..., :SWA_KV_WIDTH].reshape(cache_shape).astype(out_dtype),
            kv_p[..., SWA_KV_WIDTH:].reshape(cache_shape).astype(out_dtype),
            nk_s.reshape((1,) + cache_swa_k.shape[1:]).astype(out_dtype),
            nv_s.reshape((1,) + cache_swa_v.shape[1:]).astype(out_dtype))
```

```python
import functools

import numpy as np
import jax
import jax.numpy as jnp
from jax import lax
from jax.experimental import pallas as pl
from jax.experimental.pallas import tpu as pltpu

F32 = jnp.float32
BF16 = jnp.bfloat16

D_MODEL = 2048
N_META = 16
DEPTH = 2
HGRN_HEADS = 16
HGRN_DK = 128
HGRN_DV = 128
SWA_Q_HEADS = 32
SWA_KV_HEADS = 4
SWA_GROUP = 8
SWA_HEAD_DIM = 64
SWA_KV_WIDTH = SWA_KV_HEADS * SWA_HEAD_DIM
SWA_SCALE = SWA_HEAD_DIM ** -0.5
WINDOW = 128
DEEPNORM_ALPHA = (2.0 * DEPTH) ** 0.25
LN_EPS = 1e-5
RMS_EPS = 1e-6
LOG2E = 1.4426950408889634

CHUNK = 128
LEVELS = 7
MAX_FACTORED_LOG2_DECAY = 80.0
MASK_ROWS = 16
SEQ_BLOCK = 128
LN_SUB_ROWS = 128
STEP_GROUP = 32
SEQ_UNROLL = 8
HGRN_SEQ_HEADS = 16
HGRN_SEQ_ROWS = 256
VMEM_LIMIT = 56 * 1024 * 1024


def _dot(a, b):
    return jnp.dot(a, b, preferred_element_type=F32)


def _dot_nt(a, b):
    return lax.dot_general(a, b, (((1,), (1,)), ((), ())), preferred_element_type=F32)


def _dot_tn(a, b):
    return lax.dot_general(a, b, (((0,), (0,)), ((), ())), preferred_element_type=F32)


def _silu(x):
    return x * jax.nn.sigmoid(x)


def _mm_kernel(x_ref, w_ref, o_ref):
    o_ref[...] = _dot(x_ref[...].astype(BF16), w_ref[...]).astype(o_ref.dtype)


def _row_tile(n, want):
    if n <= want:
        return n
    return max(t for t in range(16, want + 1, 16) if n % t == 0)


def _matmul(x, w, *, tm, tn, out_dtype=F32):
    n, k = x.shape
    e = w.shape[1]
    tm = _row_tile(n, tm)
    assert n % tm == 0 and e % tn == 0
    return pl.pallas_call(
        _mm_kernel,
        grid=(n // tm, e // tn),
        in_specs=[pl.BlockSpec((tm, k), lambda i, j: (i, 0)),
                  pl.BlockSpec((k, tn), lambda i, j: (0, j))],
        out_specs=pl.BlockSpec((tm, tn), lambda i, j: (i, j)),
        out_shape=jax.ShapeDtypeStruct((n, e), out_dtype),
        compiler_params=pltpu.CompilerParams(
            dimension_semantics=("parallel", "parallel"), vmem_limit_bytes=VMEM_LIMIT),
        name="proj_in",
    )(x, w)


def _mm_cast_kernel(xs_ref, xm_ref, w_ref, o_ref, wb_ref, xb_scr):
    @pl.when(pl.program_id(0) == 0)
    def _():
        n_s = xs_ref.shape[0]
        xb_scr[0:n_s, :] = xs_ref[...].astype(BF16)
        xb_scr[n_s:, :] = xm_ref[...].astype(BF16)

    wb = w_ref[...].astype(BF16)
    wb_ref[...] = wb
    o_ref[...] = _dot(xb_scr[...], wb)


def _matmul_cast(xs, xm, w, *, xs_rows, xm_block, tn, col_tile=lambda j: j):
    k, e = w.shape
    n = xs_rows + SEQ_BLOCK
    assert e % tn == 0
    return pl.pallas_call(
        _mm_cast_kernel,
        grid=(e // tn,),
        in_specs=[pl.BlockSpec((xs_rows, k), lambda j: (0, 0)),
                  pl.BlockSpec((SEQ_BLOCK, k), lambda j: (xm_block, 0)),
                  pl.BlockSpec((k, tn), lambda j: (0, col_tile(j)))],
        out_specs=[pl.BlockSpec((n, tn), lambda j: (0, j)),
                   pl.BlockSpec((k, tn), lambda j: (0, j))],
        out_shape=[jax.ShapeDtypeStruct((n, e), F32), jax.ShapeDtypeStruct((k, e), BF16)],
        scratch_shapes=[pltpu.VMEM((n, k), BF16)],
        compiler_params=pltpu.CompilerParams(
            dimension_semantics=("arbitrary",), vmem_limit_bytes=VMEM_LIMIT),
        name="proj_in_cast",
    )(xs, xm, w)


def _mm_ln_tail_kernel(a_ref, at_ref, w_ref, x_ref, xt_ref, g_ref, b_ref, o_ref):
    tail = pl.program_id(0) == pl.num_programs(0) - 1
    a = jnp.where(tail, at_ref[...], a_ref[...])
    x = jnp.where(tail, xt_ref[...], x_ref[...])
    z = DEEPNORM_ALPHA * x + _dot(a, w_ref[...])
    mu = jnp.mean(z, axis=-1, keepdims=True)
    zc = z - mu
    var = jnp.mean(zc * zc, axis=-1, keepdims=True)
    o_ref[...] = zc * lax.rsqrt(var + LN_EPS) * g_ref[...] + b_ref[...]


def _matmul_deepnorm_tail(a, a_tail, w, x, x_tail, g, b):
    n, k = a.shape
    d = w.shape[1]
    tm = SEQ_BLOCK
    last = n // tm - 1
    body = lambda i: (jnp.minimum(i, last), 0)
    const = lambda i: (0, 0)
    return pl.pallas_call(
        _mm_ln_tail_kernel,
        grid=(n // tm + 1,),
        in_specs=[pl.BlockSpec((tm, k), body), pl.BlockSpec((tm, k), const),
                  pl.BlockSpec((k, d), const),
                  pl.BlockSpec((tm, d), body), pl.BlockSpec((tm, d), const),
                  pl.BlockSpec((1, d), const), pl.BlockSpec((1, d), const)],
        out_specs=pl.BlockSpec((tm, d), lambda i: (i, 0)),
        out_shape=jax.ShapeDtypeStruct((n + tm, d), F32),
        compiler_params=pltpu.CompilerParams(
            dimension_semantics=("parallel",), vmem_limit_bytes=VMEM_LIMIT),
        name="proj_out_deepnorm_tail",
    )(a, a_tail, w, x, x_tail, g, b)


def _mm_ln_kernel(a_ref, w_ref, x_ref, g_ref, b_ref, o_ref):
    tm = a_ref.shape[0]
    sub = LN_SUB_ROWS if tm % LN_SUB_ROWS == 0 else tm
    for r0 in range(0, tm, sub):
        rows = slice(r0, r0 + sub)
        z = DEEPNORM_ALPHA * x_ref[rows, :] + _dot(a_ref[rows, :], w_ref[...])
        mu = jnp.mean(z, axis=-1, keepdims=True)
        zc = z - mu
        var = jnp.mean(zc * zc, axis=-1, keepdims=True)
        o_ref[rows, :] = zc * lax.rsqrt(var + LN_EPS) * g_ref[...] + b_ref[...]


def _matmul_deepnorm(a, w, x, g, b, *, tm):
    n, k = a.shape
    d = w.shape[1]
    tm = _row_tile(n, tm)
    return pl.pallas_call(
        _mm_ln_kernel,
        grid=(n // tm,),
        in_specs=[pl.BlockSpec((tm, k), lambda i: (i, 0)),
                  pl.BlockSpec((k, d), lambda i: (0, 0)),
                  pl.BlockSpec((tm, d), lambda i: (i, 0)),
                  pl.BlockSpec((1, d), lambda i: (0, 0)),
                  pl.BlockSpec((1, d), lambda i: (0, 0))],
        out_specs=pl.BlockSpec((tm, d), lambda i: (i, 0)),
        out_shape=jax.ShapeDtypeStruct((n, d), F32),
        compiler_params=pltpu.CompilerParams(
            dimension_semantics=("parallel",), vmem_limit_bytes=VMEM_LIMIT),
        name="proj_out_deepnorm",
    )(a, w, x, g, b)


def _level_masks():
    t = np.arange(CHUNK)
    out = np.zeros((LEVELS + 1, CHUNK, CHUNK), np.float32)
    out[0] = t[:, None] > t[None, :]
    for l in range(LEVELS):
        h = 1 << l
        same = (t[:, None] >> (l + 1)) == (t[None, :] >> (l + 1))
        out[l + 1] = same & ((t[:, None] & h) != 0) & ((t[None, :] & h) == 0)
    return out


def _gates(fx, lb):
    f = lb + (1.0 - lb) * jax.nn.sigmoid(fx)
    return f, jnp.log2(f), 1.0 - f


def _prefix_scan(logf):
    tiles = (CHUNK // 8, 8, HGRN_DK)
    c = logf.reshape(tiles)
    sub = lax.broadcasted_iota(jnp.int32, (1, 8, HGRN_DK), 1)
    bcast = lambda x, r: jnp.broadcast_to(x[:, r:r + 1, :], tiles)
    c = c + jnp.where((sub & 1) != 0, pltpu.roll(c, 1, 1), 0.0)
    c = c + jnp.where((sub & 2) != 0, jnp.where(sub < 4, bcast(c, 1), bcast(c, 5)), 0.0)
    c = c + jnp.where((sub & 4) != 0, bcast(c, 3), 0.0)
    c = c.reshape(CHUNK, HGRN_DK)
    for level in range(3, LEVELS):
        half = 1 << level
        pieces = []
        for r0 in range(0, CHUNK, 2 * half):
            pieces += [c[r0:r0 + half], c[r0 + half:r0 + 2 * half] + c[r0 + half - 1:r0 + half, :]]
        c = jnp.concatenate(pieces, axis=0)
    return c


def _small_levels(logf, f, q, kk):
    tiles = (CHUNK // 8, 8, HGRN_DK)
    c, f3, q3, k3 = (x.reshape(tiles) for x in (logf, f, q, kk))
    sub = lax.broadcasted_iota(jnp.int32, (1, 8, HGRN_DK), 1)
    bcast = lambda x, r: jnp.broadcast_to(x[:, r:r + 1, :], tiles)
    up = (sub & 1) != 0
    zs = [jnp.where(up, q3 * f3, k3)]
    c = c + jnp.where(up, pltpu.roll(c, 1, 1), 0.0)
    for level, tot in ((1, lambda c: jnp.where(sub < 4, bcast(c, 1), bcast(c, 5))),
                       (2, lambda c: bcast(c, 3))):
        up = (sub & (1 << level)) != 0
        t = tot(c)
        zs.append(jnp.where(up, q3, k3) * jnp.exp2(jnp.where(up, c, t - c)))
        c = c + jnp.where(up, t, 0.0)
    return [z.reshape(CHUNK, HGRN_DK) for z in zs], c.reshape(CHUNK, HGRN_DK)


def _level_large(cum, q, kk, level):
    half = 1 << level
    args, bases, cums = [], [], []
    for r0 in range(0, CHUNK, 2 * half):
        lo, up = slice(r0, r0 + half), slice(r0 + half, r0 + 2 * half)
        tot = cum[r0 + half - 1:r0 + half, :]
        args += [tot - cum[lo], cum[up]]
        bases += [kk[lo], q[up]]
        cums += [cum[lo], cum[up] + tot]
    x = jnp.exp2(jnp.concatenate(args, axis=0))
    return jnp.concatenate(bases, axis=0) * x, jnp.concatenate(cums, axis=0)


def _split3(x):
    hi = x.astype(BF16)
    r = x - hi.astype(F32)
    mid = r.astype(BF16)
    lo = (r - mid.astype(F32)).astype(BF16)
    return hi, mid, lo


def _state_update(s_old, kd, v, blast, pad_rows):
    c = kd.shape[0]
    hi, mid, lo = _split3(jnp.exp2(blast))
    row = lax.broadcasted_iota(jnp.int32, (pad_rows, HGRN_DK), 0)
    dec = jnp.where(row == 0, hi.astype(F32),
                    jnp.where(row == 1, mid.astype(F32), jnp.where(row == 2, lo.astype(F32), 0.0)))
    piece = BF16 if c % 16 == 0 and pad_rows % 16 == 0 else F32
    lhs = jnp.concatenate([kd.astype(piece), dec.astype(piece)], axis=0).astype(BF16)
    rhs = jnp.concatenate(
        [jnp.concatenate([v.astype(piece), jnp.zeros((c, HGRN_DV), piece)], axis=1),
         jnp.concatenate([jnp.zeros((pad_rows, HGRN_DV), piece), jnp.ones((pad_rows, HGRN_DV), piece)], axis=1)],
        axis=0).astype(BF16)
    both = _dot_tn(lhs, rhs)
    return both[:, HGRN_DV:] * s_old + both[:, :HGRN_DV]


def _norm_gate(o, g, nw):
    o = o * lax.rsqrt(jnp.mean(o * o, axis=-1, keepdims=True) + RMS_EPS) * nw
    return o * _silu(g)


def _hgrn_seq_kernel(q_ref, fx_ref, i_ref, g_ref, lb_ref, nw_ref, s0_ref, m_ref,
                     og_ref, sout_ref, s_scr, o_scr, *, n_chunks):
    heads = s_scr.shape[0]

    @pl.when(pl.program_id(2) == 0)
    def _():
        s_scr[...] = s0_ref[0]

    def chunk(c, carry):
        rows = pl.ds(pl.multiple_of(c * CHUNK, CHUNK), CHUNK)
        hcols = [slice(hd * HGRN_DK, (hd + 1) * HGRN_DK) for hd in range(heads)]
        q = [q_ref[rows, cs] for cs in hcols]
        v = [i_ref[rows, cs] for cs in hcols]
        gates = [_gates(fx_ref[rows, cs], lb_ref[:, cs]) for cs in hcols]
        kk = [gt[2] for gt in gates]
        cum = [_prefix_scan(gt[1]) for gt in gates]
        mid = CHUNK // 2 - 1
        spread = None
        for b in cum:
            s_hd = jnp.maximum(b[0:1, :] - b[mid:mid + 1, :], b[mid:mid + 1, :] - b[CHUNK - 1:CHUNK, :])
            spread = s_hd if spread is None else jnp.maximum(spread, s_hd)
        factorable = jnp.max(spread) <= MAX_FACTORED_LOG2_DECAY

        vb = [x.astype(BF16) for x in v]
        for hd in range(heads):
            d = cum[hd] - cum[hd][mid:mid + 1, :]
            qe = (q[hd] * jnp.exp2(d)).astype(BF16)
            ke = (kk[hd] * jnp.exp2(-d)).astype(BF16)
            amat = jnp.where(m_ref[0] > 0.5, _dot_nt(qe, ke), 0.0).astype(BF16)
            blast = cum[hd][CHUNK - 1:CHUNK, :]
            qd = (q[hd] * jnp.exp2(cum[hd])).astype(BF16)
            kd = kk[hd] * jnp.exp2(blast - cum[hd])
            s_old = s_scr[hd]
            base = _dot(qd, s_old.astype(BF16)) + jnp.sum(q[hd] * kk[hd], axis=-1, keepdims=True) * v[hd]
            o_scr[hd] = base
            s_scr[hd] = _state_update(s_old, kd, vb[hd], blast, 16)
            og_ref[rows, hcols[hd]] = _norm_gate(
                base + _dot(amat, vb[hd]), g_ref[rows, hcols[hd]], nw_ref[:, hcols[hd]]).astype(og_ref.dtype)

        @pl.when(jnp.logical_not(factorable))
        def _():
            small = [_small_levels(gt[1], gt[0], q[hd], kk[hd]) for hd, gt in enumerate(gates)]
            zs = [sm[0] for sm in small]
            part = [sm[1] for sm in small]
            for level in range(3, LEVELS):
                for hd in range(heads):
                    z, part[hd] = _level_large(part[hd], q[hd], kk[hd], level)
                    zs[hd].append(z)
            nblk = CHUNK // MASK_ROWS
            a = [[None] * nblk for _ in range(heads)]
            for level in range(LEVELS):
                step = 1 << (level - 4) if level >= 4 else 0
                blocks = [i for i in range(nblk) if level < 4 or (i & step)]
                for hd in range(heads):
                    z = zs[hd][level].astype(BF16)
                    lhs = z if level < 4 else jnp.concatenate(
                        [z[i * MASK_ROWS:(i + 1) * MASK_ROWS] for i in blocks], axis=0)
                    term = _dot_nt(lhs, z)
                    for n, i in enumerate(blocks):
                        t = (term[n * MASK_ROWS:(n + 1) * MASK_ROWS]
                             * m_ref[level + 1, i * MASK_ROWS:(i + 1) * MASK_ROWS, :])
                        a[hd][i] = t if a[hd][i] is None else a[hd][i] + t
            for hd in range(heads):
                amat = jnp.concatenate(a[hd], axis=0).astype(BF16)
                og_ref[rows, hcols[hd]] = _norm_gate(
                    o_scr[hd] + _dot(amat, vb[hd]),
                    g_ref[rows, hcols[hd]], nw_ref[:, hcols[hd]]).astype(og_ref.dtype)
        return carry

    lax.fori_loop(0, n_chunks, chunk, 0)

    @pl.when(pl.program_id(2) == pl.num_programs(2) - 1)
    def _():
        sout_ref[0] = s_scr[...]


def _hgrn_seq(h, lb, nw, s0, *, n_seq, seq_len, row_block0):
    hb = HGRN_SEQ_HEADS
    width = hb * HGRN_DK
    groups = HGRN_HEADS // hb
    rb = min(HGRN_SEQ_ROWS, seq_len)
    nrb = seq_len // rb
    assert seq_len % rb == 0 and rb % CHUNK == 0 and (row_block0 * SEQ_BLOCK) % rb == 0
    rb0 = row_block0 * SEQ_BLOCK // rb
    s0_batched = s0.shape[0] != 1
    col = lambda part: (lambda b, hg, r: (rb0 + b * nrb + r, part * groups + hg))
    masks = jnp.asarray(_level_masks())
    return pl.pallas_call(
        functools.partial(_hgrn_seq_kernel, n_chunks=rb // CHUNK),
        grid=(n_seq, groups, nrb),
        in_specs=[pl.BlockSpec((rb, width), col(0)),
                  pl.BlockSpec((rb, width), col(1)),
                  pl.BlockSpec((rb, width), col(2)),
                  pl.BlockSpec((rb, width), col(3)),
                  pl.BlockSpec((1, width), lambda b, hg, r: (0, hg)),
                  pl.BlockSpec((1, width), lambda b, hg, r: (0, hg)),
                  pl.BlockSpec((1, hb, HGRN_DK, HGRN_DV),
                               (lambda b, hg, r: (b, hg, 0, 0)) if s0_batched else (lambda b, hg, r: (0, hg, 0, 0))),
                  pl.BlockSpec((LEVELS + 1, CHUNK, CHUNK), lambda b, hg, r: (0, 0, 0))],
        out_specs=[pl.BlockSpec((rb, width), lambda b, hg, r: (b * nrb + r, hg)),
                   pl.BlockSpec((1, hb, HGRN_DK, HGRN_DV), lambda b, hg, r: (b, hg, 0, 0))],
        out_shape=[jax.ShapeDtypeStruct((n_seq * seq_len, HGRN_HEADS * HGRN_DV), BF16),
                   jax.ShapeDtypeStruct((n_seq, HGRN_HEADS, HGRN_DK, HGRN_DV), F32)],
        scratch_shapes=[pltpu.VMEM((hb, HGRN_DK, HGRN_DV), F32),
                        pltpu.VMEM((hb, CHUNK, HGRN_DV), F32)],
        compiler_params=pltpu.CompilerParams(
            dimension_semantics=("parallel", "parallel", "arbitrary"), vmem_limit_bytes=VMEM_LIMIT),
        name="hgrn_seq",
    )(h, h, h, h, lb, nw, s0, masks)


def _hgrn_step_kernel(q_ref, fx_ref, i_ref, g_ref, lb_ref, nw_ref, s_ref, og_ref, sout_ref, *, n_seq, t):
    lb = lb_ref[...]
    nw = nw_ref[...]
    tile = (STEP_GROUP, t, HGRN_DK)
    sub = lax.broadcasted_iota(jnp.int32, (1, t, HGRN_DK), 1)

    def group(i, carry):
        rows = pl.ds(pl.multiple_of(i * (STEP_GROUP * t), STEP_GROUP * t), STEP_GROUP * t)
        q = q_ref[rows, :].reshape(tile)
        v = i_ref[rows, :].reshape(tile)
        _, logf, kk = _gates(fx_ref[rows, :].reshape(tile), lb)
        cum = logf
        shift = 1
        while shift < t:
            cum = cum + jnp.where(sub >= shift, pltpu.roll(cum, shift, 1), 0.0)
            shift *= 2
        intra = (jnp.sum(q * kk, axis=-1, keepdims=True)) * v
        for d in range(1, t):
            valid = sub >= d
            x = jnp.exp2(jnp.where(valid, cum - pltpu.roll(cum, d, 1), 0.0))
            w = jnp.sum(jnp.where(valid, q * x * pltpu.roll(kk, d, 1), 0.0), axis=-1, keepdims=True)
            intra = intra + w * pltpu.roll(v, d, 1)
        blast = cum[:, t - 1:t, :]
        qd = q * jnp.exp2(cum)
        kd = kk * jnp.exp2(blast - cum)
        seqs = [i * STEP_GROUP + n for n in range(STEP_GROUP)]
        inter = [_dot(qd[n].astype(BF16), s_ref[b, 0].astype(BF16)) for n, b in enumerate(seqs)]
        for n, b in enumerate(seqs):
            sout_ref[b, 0] = _state_update(s_ref[b, 0], kd[n], v[n], blast[n], 8)
        o = jnp.stack(inter, axis=0) + intra
        og = _norm_gate(o, g_ref[rows, :].reshape(tile), nw)
        og_ref[rows, :] = og.reshape(STEP_GROUP * t, HGRN_DK).astype(og_ref.dtype)
        return carry

    lax.fori_loop(0, n_seq // STEP_GROUP, group, 0)


def _hgrn_step(h, lb, nw, s0, *, n_seq, t, seq_per_block):
    seq_per_block = min(seq_per_block, n_seq)
    assert n_seq % seq_per_block == 0 and seq_per_block % STEP_GROUP == 0 and t == 8
    rows = seq_per_block * t
    col = lambda part: (lambda i, hd: (i, part * HGRN_HEADS + hd))
    return pl.pallas_call(
        functools.partial(_hgrn_step_kernel, n_seq=seq_per_block, t=t),
        grid=(n_seq // seq_per_block, HGRN_HEADS),
        in_specs=[pl.BlockSpec((rows, HGRN_DK), col(0)),
                  pl.BlockSpec((rows, HGRN_DK), col(1)),
                  pl.BlockSpec((rows, HGRN_DV), col(2)),
                  pl.BlockSpec((rows, HGRN_DV), col(3)),
                  pl.BlockSpec((1, HGRN_DK), lambda i, hd: (0, hd)),
                  pl.BlockSpec((1, HGRN_DV), lambda i, hd: (0, hd)),
                  pl.BlockSpec((seq_per_block, 1, HGRN_DK, HGRN_DV), lambda i, hd: (i, hd, 0, 0))],
        out_specs=[pl.BlockSpec((rows, HGRN_DV), lambda i, hd: (i, hd)),
                   pl.BlockSpec((seq_per_block, 1, HGRN_DK, HGRN_DV), lambda i, hd: (i, hd, 0, 0))],
        out_shape=[jax.ShapeDtypeStruct((n_seq * t, HGRN_HEADS * HGRN_DV), BF16),
                   jax.ShapeDtypeStruct((n_seq, HGRN_HEADS, HGRN_DK, HGRN_DV), F32)],
        compiler_params=pltpu.CompilerParams(
            dimension_semantics=("parallel", "parallel"), vmem_limit_bytes=VMEM_LIMIT),
        name="hgrn_step",
    )(h, h, h, h, lb, nw, s0)


def _attn_seq_kernel(q_ref, g_ref, kc_ref, vc_ref, kp_ref, vp_ref, meta_ref, sink_ref, og_ref):
    first = pl.program_id(1) == 0
    kprev = jnp.where(first, meta_ref[:, :SWA_KV_WIDTH], kp_ref[...])
    vprev = jnp.where(first, meta_ref[:, SWA_KV_WIDTH:], vp_ref[...])
    kband = (jnp.concatenate([kprev, kc_ref[...]], axis=0) * (SWA_SCALE * LOG2E)).astype(BF16)
    vband_t = jnp.concatenate([vprev, vc_ref[...]], axis=0).T.astype(BF16)
    zeros_t = jnp.zeros((SWA_HEAD_DIM, 2 * SEQ_BLOCK), BF16)
    kj = lax.broadcasted_iota(jnp.int32, (2 * SEQ_BLOCK, SEQ_BLOCK), 0)
    qi = lax.broadcasted_iota(jnp.int32, (2 * SEQ_BLOCK, SEQ_BLOCK), 1)
    dist = SEQ_BLOCK + qi - kj
    valid = (dist >= 0) & (dist < WINDOW) & (jnp.logical_not(first) | (kj >= SEQ_BLOCK - N_META))
    madd = jnp.where(valid, 0.0, -jnp.inf)
    zeros = jnp.zeros((2 * SEQ_BLOCK, SWA_HEAD_DIM), BF16)
    pairs = SWA_GROUP // 2
    pw = 2 * SWA_HEAD_DIM

    def block_diag(x):
        return jnp.concatenate([jnp.concatenate([x, zeros], axis=1),
                                jnp.concatenate([zeros, x], axis=1)], axis=0)

    sts = []
    for kvh in range(SWA_KV_HEADS):
        ks = slice(kvh * SWA_HEAD_DIM, (kvh + 1) * SWA_HEAD_DIM)
        col0 = kvh * SWA_GROUP * SWA_HEAD_DIM
        xq = jnp.concatenate([q_ref[:, col0 + p * pw:col0 + (p + 1) * pw].astype(BF16)
                              for p in range(pairs)], axis=0)
        sts.append(_dot_nt(block_diag(kband[:, ks]), xq))
    for kvh in range(SWA_KV_HEADS):
        ks = slice(kvh * SWA_HEAD_DIM, (kvh + 1) * SWA_HEAD_DIM)
        col0 = kvh * SWA_GROUP * SWA_HEAD_DIM
        vt = vband_t[ks, :]
        v2t = jnp.concatenate([jnp.concatenate([vt, zeros_t], axis=1),
                               jnp.concatenate([zeros_t, vt], axis=1)], axis=0)
        pbs, rdens = [], []
        for pr in range(pairs):
            head = kvh * SWA_GROUP + 2 * pr
            s2 = (sts[kvh][:, pr * SEQ_BLOCK:(pr + 1) * SEQ_BLOCK].reshape(2, 2 * SEQ_BLOCK, SEQ_BLOCK)
                  + madd[None])
            sink = jnp.concatenate([jnp.full((1, 1, SEQ_BLOCK), sink_ref[head] * LOG2E, F32),
                                    jnp.full((1, 1, SEQ_BLOCK), sink_ref[head + 1] * LOG2E, F32)], axis=0)
            m = jnp.maximum(jnp.max(s2, axis=1, keepdims=True), sink)
            p = jnp.exp2(s2 - m)
            den = jnp.sum(p, axis=1, keepdims=True) + jnp.exp2(sink - m)
            pbs.append(p.astype(BF16).reshape(4 * SEQ_BLOCK, SEQ_BLOCK))
            rdens.append(1.0 / den)
        for pr in range(pairs):
            ot = _dot(v2t, pbs[pr]).reshape(2, SWA_HEAD_DIM, SEQ_BLOCK) * rdens[pr]
            o = ot.reshape(2 * SWA_HEAD_DIM, SEQ_BLOCK).T
            cs = slice(col0 + pr * pw, col0 + (pr + 1) * pw)
            og_ref[:, cs] = (o * _silu(g_ref[:, cs])).astype(og_ref.dtype)


def _attn_seq(h1, h1_meta, sinks, *, n_seq, seq_len, meta_row_block):
    nb = seq_len // SEQ_BLOCK
    width = SWA_Q_HEADS * SWA_HEAD_DIM
    kcol = 2 * width // SWA_KV_WIDTH
    return pl.pallas_call(
        _attn_seq_kernel,
        grid=(n_seq, nb),
        in_specs=[pl.BlockSpec((SEQ_BLOCK, width), lambda b, j: (b * nb + j, 0)),
                  pl.BlockSpec((SEQ_BLOCK, width), lambda b, j: (b * nb + j, 1)),
                  pl.BlockSpec((SEQ_BLOCK, SWA_KV_WIDTH), lambda b, j: (b * nb + j, kcol)),
                  pl.BlockSpec((SEQ_BLOCK, SWA_KV_WIDTH), lambda b, j: (b * nb + j, kcol + 1)),
                  pl.BlockSpec((SEQ_BLOCK, SWA_KV_WIDTH), lambda b, j: (b * nb + jnp.maximum(j - 1, 0), kcol)),
                  pl.BlockSpec((SEQ_BLOCK, SWA_KV_WIDTH), lambda b, j: (b * nb + jnp.maximum(j - 1, 0), kcol + 1)),
                  pl.BlockSpec((SEQ_BLOCK, 2 * SWA_KV_WIDTH), lambda b, j: (meta_row_block, kcol // 2)),
                  pl.BlockSpec(memory_space=pltpu.SMEM)],
        out_specs=pl.BlockSpec((SEQ_BLOCK, width), lambda b, j: (b * nb + j, 0)),
        out_shape=jax.ShapeDtypeStruct((n_seq * seq_len, width), BF16),
        compiler_params=pltpu.CompilerParams(
            dimension_semantics=("parallel", "arbitrary"), vmem_limit_bytes=VMEM_LIMIT),
        name="attn_seq",
    )(h1, h1, h1, h1, h1, h1, h1_meta, sinks)


def _attn_step_kernel(q_ref, g_ref, kn_ref, vn_ref, ck_ref, cv_ref, sink_ref, og_ref, nk_ref, nv_ref, *, n_seq, t):
    keys = WINDOW + t
    hd = SWA_HEAD_DIM
    tiles = SWA_Q_HEADS // 2
    kj = lax.broadcasted_iota(jnp.int32, (keys, SWA_Q_HEADS * t), 0)
    qt = lax.broadcasted_iota(jnp.int32, (keys, SWA_Q_HEADS * t), 1) % t
    madd = jnp.where((kj >= qt + 1) & (kj <= WINDOW + qt), 0.0, -jnp.inf)
    low = lax.broadcasted_iota(jnp.int32, (t, 2 * hd), 1) < hd
    zero_tile = jnp.zeros((t, 2 * hd), F32)
    sink = sink_ref[...] * LOG2E

    def group(i, carry):
        seqs = [i * SEQ_UNROLL + n for n in range(SEQ_UNROLL)]
        rows = [pl.ds(pl.multiple_of(b * t, t), t) for b in seqs]
        st, vall = [], []
        for b, rw in zip(seqs, rows):
            kc, vc = ck_ref[b], cv_ref[b]
            kn, vn = kn_ref[rw, :], vn_ref[rw, :]
            nk_ref[b, 0:WINDOW - t, :] = kc[t:, :]
            nk_ref[b, WINDOW - t:WINDOW, :] = kn
            nv_ref[b, 0:WINDOW - t, :] = vc[t:, :]
            nv_ref[b, WINDOW - t:WINDOW, :] = vn
            kall = jnp.concatenate([kc, kn], axis=0).astype(BF16)
            vall.append(jnp.concatenate([vc, vn], axis=0).astype(BF16))
            q = q_ref[rw, :] * (SWA_SCALE * LOG2E)
            qtile = [q[:, j * 2 * hd:(j + 1) * 2 * hd] for j in range(tiles)]
            qswap = [pltpu.roll(x, hd, 1) for x in qtile]
            groups = []
            for kvh in range(SWA_KV_HEADS):
                for gq in range(SWA_GROUP):
                    j = (kvh * SWA_GROUP + gq) // 2
                    src = qtile[j] if gq % 2 == kvh % 2 else qswap[j]
                    half = jnp.where(low, src, 0.0) if kvh % 2 == 0 else jnp.where(low, 0.0, src)
                    groups.append(jnp.concatenate(
                        [half if c == kvh // 2 else zero_tile for c in range(SWA_KV_HEADS // 2)], axis=1))
            qbd = jnp.concatenate(groups, axis=0).astype(BF16)
            st.append(_dot_nt(kall, qbd))
        pn = []
        for s in st:
            s = s + madd
            m = jnp.maximum(jnp.max(s, axis=0, keepdims=True), sink)
            p = jnp.exp2(s - m)
            den = jnp.sum(p, axis=0, keepdims=True) + jnp.exp2(sink - m)
            pn.append((p * (1.0 / den)).astype(BF16))
        of = [_dot_tn(p, vl) for p, vl in zip(pn, vall)]
        for o, rw in zip(of, rows):
            g = g_ref[rw, :]
            out = []
            for j in range(tiles):
                kvh = (2 * j) // SWA_GROUP
                ct = slice((kvh // 2) * 2 * hd, (kvh // 2 + 1) * 2 * hd)
                ra = o[(2 * j) * t:(2 * j + 1) * t, ct]
                rb = o[(2 * j + 1) * t:(2 * j + 2) * t, ct]
                if kvh % 2 == 0:
                    out.append(jnp.where(low, ra, pltpu.roll(rb, hd, 1)))
                else:
                    out.append(jnp.where(low, pltpu.roll(ra, hd, 1), rb))
            og_ref[rw, :] = (jnp.concatenate(out, axis=1) * _silu(g)).astype(og_ref.dtype)
        return carry

    lax.fori_loop(0, n_seq // SEQ_UNROLL, group, 0)


def _attn_step(h1, cache_k, cache_v, sink_cols, *, n_seq, t, seq_per_block):
    assert n_seq % seq_per_block == 0
    rows = seq_per_block * t
    width = SWA_Q_HEADS * SWA_HEAD_DIM
    kcol = 2 * width // SWA_KV_WIDTH
    cache_spec = pl.BlockSpec((seq_per_block, WINDOW, SWA_KV_WIDTH), lambda i: (i, 0, 0))
    return pl.pallas_call(
        functools.partial(_attn_step_kernel, n_seq=seq_per_block, t=t),
        grid=(n_seq // seq_per_block,),
        in_specs=[pl.BlockSpec((rows, width), lambda i: (i, 0)),
                  pl.BlockSpec((rows, width), lambda i: (i, 1)),
                  pl.BlockSpec((rows, SWA_KV_WIDTH), lambda i: (i, kcol)),
                  pl.BlockSpec((rows, SWA_KV_WIDTH), lambda i: (i, kcol + 1)),
                  cache_spec, cache_spec,
                  pl.BlockSpec((1, SWA_Q_HEADS * t), lambda i: (0, 0))],
        out_specs=[pl.BlockSpec((rows, width), lambda i: (i, 0)), cache_spec, cache_spec],
        out_shape=[jax.ShapeDtypeStruct((n_seq * t, width), BF16),
                   jax.ShapeDtypeStruct(cache_k.shape, F32),
                   jax.ShapeDtypeStruct(cache_v.shape, F32)],
        compiler_params=pltpu.CompilerParams(
            dimension_semantics=("parallel",), vmem_limit_bytes=VMEM_LIMIT),
        name="attn_step",
    )(h1, h1, h1, h1, cache_k, cache_v, sink_cols)


def kernel(x_prompt, x_sample, state_hgrn, cache_swa_k, cache_swa_v, meta_tokens,
           hgrn_w_in, hgrn_lb_logits, hgrn_norm_w, hgrn_w_out,
           swa_w_in, swa_sinks, swa_w_out, ln_g, ln_b):
    out_dtype = x_prompt.dtype
    bsz, seq, d = x_prompt.shape
    dec_b, dec_t, _ = x_sample.shape
    n_p = bsz * seq
    n_s = dec_b * dec_t
    width = SWA_Q_HEADS * SWA_HEAD_DIM

    w_out0 = hgrn_w_out[0].astype(BF16)
    w_out1 = swa_w_out[0].astype(BF16)
    in_tile = 512
    q_tiles = width // in_tile
    kv_tile = 2 * SWA_KV_WIDTH // in_tile
    assert width % in_tile == 0 and (2 * SWA_KV_WIDTH) % in_tile == 0 and kv_tile == 1
    regroup = lambda j: jnp.where(j < q_tiles, j, jnp.where(j < 2 * q_tiles, j + kv_tile, q_tiles))
    lb = jnp.cumsum(jax.nn.softmax(hgrn_lb_logits.astype(F32), axis=0), axis=0)[0:1]
    nw = hgrn_norm_w[0].astype(F32).reshape(1, -1)
    sinks = swa_sinks[0].astype(F32).reshape(1, SWA_Q_HEADS)
    sink_cols = jnp.repeat(sinks, dec_t, axis=1)
    g0, b0 = ln_g[0:1].astype(F32), ln_b[0:1].astype(F32)
    g1, b1 = ln_g[1:2].astype(F32), ln_b[1:2].astype(F32)

    x_p = x_prompt.astype(F32).reshape(n_p, d)
    meta_block = jnp.concatenate(
        [jnp.zeros((SEQ_BLOCK - N_META, d), F32), meta_tokens.astype(F32)], axis=0)
    x_s = x_sample.astype(F32).reshape(n_s, d)
    meta_blk = n_s // SEQ_BLOCK

    h_sm, w_in0 = _matmul_cast(x_s, meta_block, hgrn_w_in[0].astype(F32), xs_rows=n_s, xm_block=0, tn=in_tile)
    h_p = _matmul(x_p, w_in0, tm=1024, tn=1024)
    zero_state = jnp.zeros((1, HGRN_HEADS, HGRN_DK, HGRN_DV), F32)
    og_meta, s_meta = _hgrn_seq(h_sm, lb, nw, zero_state, n_seq=1, seq_len=SEQ_BLOCK, row_block0=meta_blk)
    og_s, st_s = _hgrn_step(h_sm, lb, nw, state_hgrn[0].astype(F32), n_seq=dec_b, t=dec_t, seq_per_block=128)
    og_p, st_p = _hgrn_seq(h_p, lb, nw, s_meta, n_seq=bsz, seq_len=seq, row_block0=0)
    x1_sm = _matmul_deepnorm_tail(og_s, og_meta, w_out0, x_s, meta_block, g0, b0)
    x1_p = _matmul_deepnorm(og_p, w_out0, x_p, g0, b0, tm=512)

    h1_sm, w_in1 = _matmul_cast(x1_sm, x1_sm, swa_w_in[0].astype(F32), xs_rows=n_s, xm_block=meta_blk,
                                tn=in_tile, col_tile=regroup)
    h1_p = _matmul(x1_p, w_in1, tm=1024, tn=1536)
    og1_p = _attn_seq(h1_p, h1_sm, sinks.reshape(-1), n_seq=bsz, seq_len=seq, meta_row_block=meta_blk)
    ck = cache_swa_k[0].astype(F32).reshape(dec_b, WINDOW, SWA_KV_WIDTH)
    cv = cache_swa_v[0].astype(F32).reshape(dec_b, WINDOW, SWA_KV_WIDTH)
    og1_s, nk_s, nv_s = _attn_step(h1_sm, ck, cv, sink_cols, n_seq=dec_b, t=dec_t, seq_per_block=16)
    y_p = _matmul_deepnorm(og1_p, w_out1, x1_p, g1, b1, tm=512)
    y_s = _matmul_deepnorm(og1_s, w_out1, x1_sm, g1, b1, tm=512)

    kv_p = h1_p.reshape(bsz, seq, -1)[:, seq - WINDOW:, 2 * width:]
    cache_shape = (1, bsz, WINDOW, SWA_KV_HEADS, SWA_HEAD_DIM)
    return (y_p.reshape(bsz, seq, d).astype(out_dtype),
            y_s.reshape(dec_b, dec_t, d).astype(out_dtype),
            st_p[None].astype(out_dtype),
            st_s[None].astype(out_dtype),
            kv_p[..., :SWA_KV_WIDTH].reshape(cache_shape).astype(out_dtype),
            kv_p[..., SWA_KV_WIDTH:].reshape(cache_shape).astype(out_dtype),
            nk_s.reshape((1,) + cache_swa_k.shape[1:]).astype(out_dtype),
            nv_s.reshape((1,) + cache_swa_v.shape[1:]).astype(out_dtype))
```

```python
import functools

import numpy as np
import jax
import jax.numpy as jnp
from jax import lax
from jax.experimental import pallas as pl
from jax.experimental.pallas import tpu as pltpu

F32 = jnp.float32
BF16 = jnp.bfloat16

D_MODEL = 2048
N_META = 16
DEPTH = 2
HGRN_HEADS = 16
HGRN_DK = 128
HGRN_DV = 128
SWA_Q_HEADS = 32
SWA_KV_HEADS = 4
SWA_GROUP = 8
SWA_HEAD_DIM = 64
SWA_KV_WIDTH = SWA_KV_HEADS * SWA_HEAD_DIM
SWA_SCALE = SWA_HEAD_DIM ** -0.5
WINDOW = 128
DEEPNORM_ALPHA = (2.0 * DEPTH) ** 0.25
LN_EPS = 1e-5
RMS_EPS = 1e-6
LOG2E = 1.4426950408889634

CHUNK = 128
LEVELS = 7
MAX_FACTORED_LOG2_DECAY = 80.0
MASK_ROWS = 16
SEQ_BLOCK = 128
LN_SUB_ROWS = 128
STEP_GROUP = 32
SEQ_UNROLL = 8
HGRN_SEQ_HEADS = 16
HGRN_SEQ_ROWS = 256

V7X_VMEM_BYTES = 64 * 1024 * 1024
VMEM_LIMIT = V7X_VMEM_BYTES - 8 * 1024 * 1024
PROJ_ROW_TILE = 1024
PROJ_IN0_COL_TILE = 1024
PROJ_IN1_COL_TILE = 1536
PROJ_OUT_ROW_TILE = 512
SHORT_COL_TILE = 512
STEP_STATE_SEQS = 128
STEP_CACHE_SEQS = 32


def _dot(a, b):
    return jnp.dot(a, b, preferred_element_type=F32)


def _dot_nt(a, b):
    return lax.dot_general(a, b, (((1,), (1,)), ((), ())), preferred_element_type=F32)


def _dot_tn(a, b):
    return lax.dot_general(a, b, (((0,), (0,)), ((), ())), preferred_element_type=F32)


def _silu(x):
    return x * jax.nn.sigmoid(x)


def _mm_kernel(x_ref, w_ref, o_ref):
    o_ref[...] = _dot(x_ref[...].astype(BF16), w_ref[...])


def _row_tile(n, want):
    if n <= want:
        return n
    return max(t for t in range(16, want + 1, 16) if n % t == 0)


def _matmul(x, w, *, tm, tn):
    n, k = x.shape
    e = w.shape[1]
    tm = _row_tile(n, tm)
    assert n % tm == 0 and e % tn == 0
    return pl.pallas_call(
        _mm_kernel,
        grid=(n // tm, e // tn),
        in_specs=[pl.BlockSpec((tm, k), lambda i, j: (i, 0)),
                  pl.BlockSpec((k, tn), lambda i, j: (0, j))],
        out_specs=pl.BlockSpec((tm, tn), lambda i, j: (i, j)),
        out_shape=jax.ShapeDtypeStruct((n, e), F32),
        compiler_params=pltpu.CompilerParams(
            dimension_semantics=("parallel", "parallel"), vmem_limit_bytes=VMEM_LIMIT),
        name="proj_in",
    )(x, w)


def _mm_cast_kernel(xs_ref, xm_ref, w_ref, o_ref, wb_ref, xb_scr):
    @pl.when(pl.program_id(0) == 0)
    def _():
        n_s = xs_ref.shape[0]
        xb_scr[0:n_s, :] = xs_ref[...].astype(BF16)
        xb_scr[n_s:, :] = xm_ref[...].astype(BF16)

    wb = w_ref[...].astype(BF16)
    wb_ref[...] = wb
    o_ref[...] = _dot(xb_scr[...], wb)


def _matmul_cast(xs, xm, w, *, xs_rows, xm_block, tn, col_tile=lambda j: j):
    k, e = w.shape
    n = xs_rows + SEQ_BLOCK
    assert e % tn == 0
    return pl.pallas_call(
        _mm_cast_kernel,
        grid=(e // tn,),
        in_specs=[pl.BlockSpec((xs_rows, k), lambda j: (0, 0)),
                  pl.BlockSpec((SEQ_BLOCK, k), lambda j: (xm_block, 0)),
                  pl.BlockSpec((k, tn), lambda j: (0, col_tile(j)))],
        out_specs=[pl.BlockSpec((n, tn), lambda j: (0, j)),
                   pl.BlockSpec((k, tn), lambda j: (0, j))],
        out_shape=[jax.ShapeDtypeStruct((n, e), F32), jax.ShapeDtypeStruct((k, e), BF16)],
        scratch_shapes=[pltpu.VMEM((n, k), BF16)],
        compiler_params=pltpu.CompilerParams(
            dimension_semantics=("arbitrary",), vmem_limit_bytes=VMEM_LIMIT),
        name="proj_in_cast",
    )(xs, xm, w)


def _mm_ln_tail_kernel(a_ref, at_ref, w_ref, x_ref, xt_ref, g_ref, b_ref, o_ref):
    tail = pl.program_id(0) == pl.num_programs(0) - 1
    a = jnp.where(tail, at_ref[...], a_ref[...])
    x = jnp.where(tail, xt_ref[...], x_ref[...])
    z = DEEPNORM_ALPHA * x + _dot(a, w_ref[...])
    mu = jnp.mean(z, axis=-1, keepdims=True)
    zc = z - mu
    var = jnp.mean(zc * zc, axis=-1, keepdims=True)
    o_ref[...] = zc * lax.rsqrt(var + LN_EPS) * g_ref[...] + b_ref[...]


def _matmul_deepnorm_tail(a, a_tail, w, x, x_tail, g, b):
    n, k = a.shape
    d = w.shape[1]
    tm = SEQ_BLOCK
    last = n // tm - 1
    body = lambda i: (jnp.minimum(i, last), 0)
    const = lambda i: (0, 0)
    return pl.pallas_call(
        _mm_ln_tail_kernel,
        grid=(n // tm + 1,),
        in_specs=[pl.BlockSpec((tm, k), body), pl.BlockSpec((tm, k), const),
                  pl.BlockSpec((k, d), const),
                  pl.BlockSpec((tm, d), body), pl.BlockSpec((tm, d), const),
                  pl.BlockSpec((1, d), const), pl.BlockSpec((1, d), const)],
        out_specs=pl.BlockSpec((tm, d), lambda i: (i, 0)),
        out_shape=jax.ShapeDtypeStruct((n + tm, d), F32),
        compiler_params=pltpu.CompilerParams(
            dimension_semantics=("parallel",), vmem_limit_bytes=VMEM_LIMIT),
        name="proj_out_deepnorm_tail",
    )(a, a_tail, w, x, x_tail, g, b)


def _mm_ln_kernel(a_ref, w_ref, x_ref, g_ref, b_ref, o_ref):
    tm = a_ref.shape[0]
    sub = LN_SUB_ROWS if tm % LN_SUB_ROWS == 0 else tm
    for r0 in range(0, tm, sub):
        rows = slice(r0, r0 + sub)
        z = DEEPNORM_ALPHA * x_ref[rows, :] + _dot(a_ref[rows, :], w_ref[...])
        mu = jnp.mean(z, axis=-1, keepdims=True)
        zc = z - mu
        var = jnp.mean(zc * zc, axis=-1, keepdims=True)
        o_ref[rows, :] = zc * lax.rsqrt(var + LN_EPS) * g_ref[...] + b_ref[...]


def _matmul_deepnorm(a, w, x, g, b, *, tm):
    n, k = a.shape
    d = w.shape[1]
    tm = _row_tile(n, tm)
    return pl.pallas_call(
        _mm_ln_kernel,
        grid=(n // tm,),
        in_specs=[pl.BlockSpec((tm, k), lambda i: (i, 0)),
                  pl.BlockSpec((k, d), lambda i: (0, 0)),
                  pl.BlockSpec((tm, d), lambda i: (i, 0)),
                  pl.BlockSpec((1, d), lambda i: (0, 0)),
                  pl.BlockSpec((1, d), lambda i: (0, 0))],
        out_specs=pl.BlockSpec((tm, d), lambda i: (i, 0)),
        out_shape=jax.ShapeDtypeStruct((n, d), F32),
        compiler_params=pltpu.CompilerParams(
            dimension_semantics=("parallel",), vmem_limit_bytes=VMEM_LIMIT),
        name="proj_out_deepnorm",
    )(a, w, x, g, b)


def _level_masks():
    t = np.arange(CHUNK)
    out = np.zeros((LEVELS + 1, CHUNK, CHUNK), np.float32)
    out[0] = t[:, None] > t[None, :]
    for l in range(LEVELS):
        h = 1 << l
        same = (t[:, None] >> (l + 1)) == (t[None, :] >> (l + 1))
        out[l + 1] = same & ((t[:, None] & h) != 0) & ((t[None, :] & h) == 0)
    return out


def _gates(fx, lb):
    f = lb + (1.0 - lb) * jax.nn.sigmoid(fx)
    return f, jnp.log2(f), 1.0 - f


def _prefix_scan(logf):
    tiles = (CHUNK // 8, 8, HGRN_DK)
    c = logf.reshape(tiles)
    sub = lax.broadcasted_iota(jnp.int32, (1, 8, HGRN_DK), 1)
    bcast = lambda x, r: jnp.broadcast_to(x[:, r:r + 1, :], tiles)
    c = c + jnp.where((sub & 1) != 0, pltpu.roll(c, 1, 1), 0.0)
    c = c + jnp.where((sub & 2) != 0, jnp.where(sub < 4, bcast(c, 1), bcast(c, 5)), 0.0)
    c = c + jnp.where((sub & 4) != 0, bcast(c, 3), 0.0)
    c = c.reshape(CHUNK, HGRN_DK)
    for level in range(3, LEVELS):
        half = 1 << level
        pieces = []
        for r0 in range(0, CHUNK, 2 * half):
            pieces += [c[r0:r0 + half], c[r0 + half:r0 + 2 * half] + c[r0 + half - 1:r0 + half, :]]
        c = jnp.concatenate(pieces, axis=0)
    return c


def _small_levels(logf, f, q, kk):
    tiles = (CHUNK // 8, 8, HGRN_DK)
    c, f3, q3, k3 = (x.reshape(tiles) for x in (logf, f, q, kk))
    sub = lax.broadcasted_iota(jnp.int32, (1, 8, HGRN_DK), 1)
    bcast = lambda x, r: jnp.broadcast_to(x[:, r:r + 1, :], tiles)
    up = (sub & 1) != 0
    zs = [jnp.where(up, q3 * f3, k3)]
    c = c + jnp.where(up, pltpu.roll(c, 1, 1), 0.0)
    for level, tot in ((1, lambda c: jnp.where(sub < 4, bcast(c, 1), bcast(c, 5))),
                       (2, lambda c: bcast(c, 3))):
        up = (sub & (1 << level)) != 0
        t = tot(c)
        zs.append(jnp.where(up, q3, k3) * jnp.exp2(jnp.where(up, c, t - c)))
        c = c + jnp.where(up, t, 0.0)
    return [z.reshape(CHUNK, HGRN_DK) for z in zs], c.reshape(CHUNK, HGRN_DK)


def _level_large(cum, q, kk, level):
    half = 1 << level
    args, bases, cums = [], [], []
    for r0 in range(0, CHUNK, 2 * half):
        lo, up = slice(r0, r0 + half), slice(r0 + half, r0 + 2 * half)
        tot = cum[r0 + half - 1:r0 + half, :]
        args += [tot - cum[lo], cum[up]]
        bases += [kk[lo], q[up]]
        cums += [cum[lo], cum[up] + tot]
    x = jnp.exp2(jnp.concatenate(args, axis=0))
    return jnp.concatenate(bases, axis=0) * x, jnp.concatenate(cums, axis=0)


def _split3(x):
    hi = x.astype(BF16)
    r = x - hi.astype(F32)
    mid = r.astype(BF16)
    lo = (r - mid.astype(F32)).astype(BF16)
    return hi, mid, lo


def _state_update(s_old, kd, v, blast, pad_rows):
    c = kd.shape[0]
    hi, mid, lo = _split3(jnp.exp2(blast))
    row = lax.broadcasted_iota(jnp.int32, (pad_rows, HGRN_DK), 0)
    dec = jnp.where(row == 0, hi.astype(F32),
                    jnp.where(row == 1, mid.astype(F32), jnp.where(row == 2, lo.astype(F32), 0.0)))
    piece = BF16 if c % 16 == 0 and pad_rows % 16 == 0 else F32
    lhs = jnp.concatenate([kd.astype(piece), dec.astype(piece)], axis=0).astype(BF16)
    rhs = jnp.concatenate(
        [jnp.concatenate([v.astype(piece), jnp.zeros((c, HGRN_DV), piece)], axis=1),
         jnp.concatenate([jnp.zeros((pad_rows, HGRN_DV), piece), jnp.ones((pad_rows, HGRN_DV), piece)], axis=1)],
        axis=0).astype(BF16)
    both = _dot_tn(lhs, rhs)
    return both[:, HGRN_DV:] * s_old + both[:, :HGRN_DV]


def _norm_gate(o, g, nw):
    o = o * lax.rsqrt(jnp.mean(o * o, axis=-1, keepdims=True) + RMS_EPS) * nw
    return o * _silu(g)


def _hgrn_seq_kernel(q_ref, fx_ref, i_ref, g_ref, lb_ref, nw_ref, s0_ref, m_ref,
                     og_ref, sout_ref, s_scr, o_scr, *, n_chunks):
    heads = s_scr.shape[0]

    @pl.when(pl.program_id(2) == 0)
    def _():
        s_scr[...] = s0_ref[0]

    def chunk(c, carry):
        rows = pl.ds(pl.multiple_of(c * CHUNK, CHUNK), CHUNK)
        hcols = [slice(hd * HGRN_DK, (hd + 1) * HGRN_DK) for hd in range(heads)]
        q = [q_ref[rows, cs] for cs in hcols]
        v = [i_ref[rows, cs] for cs in hcols]
        gates = [_gates(fx_ref[rows, cs], lb_ref[:, cs]) for cs in hcols]
        kk = [gt[2] for gt in gates]
        cum = [_prefix_scan(gt[1]) for gt in gates]
        mid = CHUNK // 2 - 1
        spread = None
        for b in cum:
            s_hd = jnp.maximum(b[0:1, :] - b[mid:mid + 1, :], b[mid:mid + 1, :] - b[CHUNK - 1:CHUNK, :])
            spread = s_hd if spread is None else jnp.maximum(spread, s_hd)
        factorable = jnp.max(spread) <= MAX_FACTORED_LOG2_DECAY

        vb = [x.astype(BF16) for x in v]
        for hd in range(heads):
            d = cum[hd] - cum[hd][mid:mid + 1, :]
            qe = (q[hd] * jnp.exp2(d)).astype(BF16)
            ke = (kk[hd] * jnp.exp2(-d)).astype(BF16)
            amat = jnp.where(m_ref[0] > 0.5, _dot_nt(qe, ke), 0.0).astype(BF16)
            blast = cum[hd][CHUNK - 1:CHUNK, :]
            qd = (q[hd] * jnp.exp2(cum[hd])).astype(BF16)
            kd = kk[hd] * jnp.exp2(blast - cum[hd])
            s_old = s_scr[hd]
            base = _dot(qd, s_old.astype(BF16)) + jnp.sum(q[hd] * kk[hd], axis=-1, keepdims=True) * v[hd]
            o_scr[hd] = base
            s_scr[hd] = _state_update(s_old, kd, vb[hd], blast, 16)
            og_ref[rows, hcols[hd]] = _norm_gate(
                base + _dot(amat, vb[hd]), g_ref[rows, hcols[hd]], nw_ref[:, hcols[hd]]).astype(og_ref.dtype)

        @pl.when(jnp.logical_not(factorable))
        def _():
            small = [_small_levels(gt[1], gt[0], q[hd], kk[hd]) for hd, gt in enumerate(gates)]
            zs = [sm[0] for sm in small]
            part = [sm[1] for sm in small]
            for level in range(3, LEVELS):
                for hd in range(heads):
                    z, part[hd] = _level_large(part[hd], q[hd], kk[hd], level)
                    zs[hd].append(z)
            nblk = CHUNK // MASK_ROWS
            a = [[None] * nblk for _ in range(heads)]
            for level in range(LEVELS):
                step = 1 << (level - 4) if level >= 4 else 0
                blocks = [i for i in range(nblk) if level < 4 or (i & step)]
                for hd in range(heads):
                    z = zs[hd][level].astype(BF16)
                    lhs = z if level < 4 else jnp.concatenate(
                        [z[i * MASK_ROWS:(i + 1) * MASK_ROWS] for i in blocks], axis=0)
                    term = _dot_nt(lhs, z)
                    for n, i in enumerate(blocks):
                        t = (term[n * MASK_ROWS:(n + 1) * MASK_ROWS]
                             * m_ref[level + 1, i * MASK_ROWS:(i + 1) * MASK_ROWS, :])
                        a[hd][i] = t if a[hd][i] is None else a[hd][i] + t
            for hd in range(heads):
                amat = jnp.concatenate(a[hd], axis=0).astype(BF16)
                og_ref[rows, hcols[hd]] = _norm_gate(
                    o_scr[hd] + _dot(amat, vb[hd]),
                    g_ref[rows, hcols[hd]], nw_ref[:, hcols[hd]]).astype(og_ref.dtype)
        return carry

    lax.fori_loop(0, n_chunks, chunk, 0)

    @pl.when(pl.program_id(2) == pl.num_programs(2) - 1)
    def _():
        sout_ref[0] = s_scr[...]


def _hgrn_seq(h, lb, nw, s0, *, n_seq, seq_len, row_block0):
    hb = HGRN_SEQ_HEADS
    width = hb * HGRN_DK
    groups = HGRN_HEADS // hb
    rb = min(HGRN_SEQ_ROWS, seq_len)
    nrb = seq_len // rb
    assert seq_len % rb == 0 and rb % CHUNK == 0 and (row_block0 * SEQ_BLOCK) % rb == 0
    rb0 = row_block0 * SEQ_BLOCK // rb
    s0_batched = s0.shape[0] != 1
    col = lambda part: (lambda b, hg, r: (rb0 + b * nrb + r, part * groups + hg))
    masks = jnp.asarray(_level_masks())
    return pl.pallas_call(
        functools.partial(_hgrn_seq_kernel, n_chunks=rb // CHUNK),
        grid=(n_seq, groups, nrb),
        in_specs=[pl.BlockSpec((rb, width), col(0)),
                  pl.BlockSpec((rb, width), col(1)),
                  pl.BlockSpec((rb, width), col(2)),
                  pl.BlockSpec((rb, width), col(3)),
                  pl.BlockSpec((1, width), lambda b, hg, r: (0, hg)),
                  pl.BlockSpec((1, width), lambda b, hg, r: (0, hg)),
                  pl.BlockSpec((1, hb, HGRN_DK, HGRN_DV),
                               (lambda b, hg, r: (b, hg, 0, 0)) if s0_batched else (lambda b, hg, r: (0, hg, 0, 0))),
                  pl.BlockSpec((LEVELS + 1, CHUNK, CHUNK), lambda b, hg, r: (0, 0, 0))],
        out_specs=[pl.BlockSpec((rb, width), lambda b, hg, r: (b * nrb + r, hg)),
                   pl.BlockSpec((1, hb, HGRN_DK, HGRN_DV), lambda b, hg, r: (b, hg, 0, 0))],
        out_shape=[jax.ShapeDtypeStruct((n_seq * seq_len, HGRN_HEADS * HGRN_DV), BF16),
                   jax.ShapeDtypeStruct((n_seq, HGRN_HEADS, HGRN_DK, HGRN_DV), F32)],
        scratch_shapes=[pltpu.VMEM((hb, HGRN_DK, HGRN_DV), F32),
                        pltpu.VMEM((hb, CHUNK, HGRN_DV), F32)],
        compiler_params=pltpu.CompilerParams(
            dimension_semantics=("parallel", "parallel", "arbitrary"), vmem_limit_bytes=VMEM_LIMIT),
        name="hgrn_seq",
    )(h, h, h, h, lb, nw, s0, masks)


def _hgrn_step_kernel(q_ref, fx_ref, i_ref, g_ref, lb_ref, nw_ref, s_ref, og_ref, sout_ref, *, n_seq, t):
    lb = lb_ref[...]
    nw = nw_ref[...]
    tile = (STEP_GROUP, t, HGRN_DK)
    sub = lax.broadcasted_iota(jnp.int32, (1, t, HGRN_DK), 1)

    def group(i, carry):
        rows = pl.ds(pl.multiple_of(i * (STEP_GROUP * t), STEP_GROUP * t), STEP_GROUP * t)
        q = q_ref[rows, :].reshape(tile)
        v = i_ref[rows, :].reshape(tile)
        _, logf, kk = _gates(fx_ref[rows, :].reshape(tile), lb)
        cum = logf
        shift = 1
        while shift < t:
            cum = cum + jnp.where(sub >= shift, pltpu.roll(cum, shift, 1), 0.0)
            shift *= 2
        intra = (jnp.sum(q * kk, axis=-1, keepdims=True)) * v
        for d in range(1, t):
            valid = sub >= d
            x = jnp.exp2(jnp.where(valid, cum - pltpu.roll(cum, d, 1), 0.0))
            w = jnp.sum(jnp.where(valid, q * x * pltpu.roll(kk, d, 1), 0.0), axis=-1, keepdims=True)
            intra = intra + w * pltpu.roll(v, d, 1)
        blast = cum[:, t - 1:t, :]
        qd = q * jnp.exp2(cum)
        kd = kk * jnp.exp2(blast - cum)
        seqs = [i * STEP_GROUP + n for n in range(STEP_GROUP)]
        inter = [_dot(qd[n].astype(BF16), s_ref[b, 0].astype(BF16)) for n, b in enumerate(seqs)]
        for n, b in enumerate(seqs):
            sout_ref[b, 0] = _state_update(s_ref[b, 0], kd[n], v[n], blast[n], 8)
        o = jnp.stack(inter, axis=0) + intra
        og = _norm_gate(o, g_ref[rows, :].reshape(tile), nw)
        og_ref[rows, :] = og.reshape(STEP_GROUP * t, HGRN_DK).astype(og_ref.dtype)
        return carry

    lax.fori_loop(0, n_seq // STEP_GROUP, group, 0)


def _hgrn_step(h, lb, nw, s0, *, n_seq, t, seq_per_block):
    seq_per_block = min(seq_per_block, n_seq)
    assert n_seq % seq_per_block == 0 and seq_per_block % STEP_GROUP == 0 and t == 8
    rows = seq_per_block * t
    col = lambda part: (lambda i, hd: (i, part * HGRN_HEADS + hd))
    return pl.pallas_call(
        functools.partial(_hgrn_step_kernel, n_seq=seq_per_block, t=t),
        grid=(n_seq // seq_per_block, HGRN_HEADS),
        in_specs=[pl.BlockSpec((rows, HGRN_DK), col(0)),
                  pl.BlockSpec((rows, HGRN_DK), col(1)),
                  pl.BlockSpec((rows, HGRN_DV), col(2)),
                  pl.BlockSpec((rows, HGRN_DV), col(3)),
                  pl.BlockSpec((1, HGRN_DK), lambda i, hd: (0, hd)),
                  pl.BlockSpec((1, HGRN_DV), lambda i, hd: (0, hd)),
                  pl.BlockSpec((seq_per_block, 1, HGRN_DK, HGRN_DV), lambda i, hd: (i, hd, 0, 0))],
        out_specs=[pl.BlockSpec((rows, HGRN_DV), lambda i, hd: (i, hd)),
                   pl.BlockSpec((seq_per_block, 1, HGRN_DK, HGRN_DV), lambda i, hd: (i, hd, 0, 0))],
        out_shape=[jax.ShapeDtypeStruct((n_seq * t, HGRN_HEADS * HGRN_DV), BF16),
                   jax.ShapeDtypeStruct((n_seq, HGRN_HEADS, HGRN_DK, HGRN_DV), F32)],
        compiler_params=pltpu.CompilerParams(
            dimension_semantics=("parallel", "parallel"), vmem_limit_bytes=VMEM_LIMIT),
        name="hgrn_step",
    )(h, h, h, h, lb, nw, s0)


def _attn_seq_kernel(q_ref, g_ref, kc_ref, vc_ref, kp_ref, vp_ref, meta_ref, sink_ref, og_ref):
    first = pl.program_id(1) == 0
    kprev = jnp.where(first, meta_ref[:, :SWA_KV_WIDTH], kp_ref[...])
    vprev = jnp.where(first, meta_ref[:, SWA_KV_WIDTH:], vp_ref[...])
    kband = (jnp.concatenate([kprev, kc_ref[...]], axis=0) * (SWA_SCALE * LOG2E)).astype(BF16)
    vband_t = jnp.concatenate([vprev, vc_ref[...]], axis=0).T.astype(BF16)
    zeros_t = jnp.zeros((SWA_HEAD_DIM, 2 * SEQ_BLOCK), BF16)
    kj = lax.broadcasted_iota(jnp.int32, (2 * SEQ_BLOCK, SEQ_BLOCK), 0)
    qi = lax.broadcasted_iota(jnp.int32, (2 * SEQ_BLOCK, SEQ_BLOCK), 1)
    dist = SEQ_BLOCK + qi - kj
    valid = (dist >= 0) & (dist < WINDOW) & (jnp.logical_not(first) | (kj >= SEQ_BLOCK - N_META))
    madd = jnp.where(valid, 0.0, -jnp.inf)
    zeros = jnp.zeros((2 * SEQ_BLOCK, SWA_HEAD_DIM), BF16)
    pairs = SWA_GROUP // 2
    pw = 2 * SWA_HEAD_DIM

    def block_diag(x):
        return jnp.concatenate([jnp.concatenate([x, zeros], axis=1),
                                jnp.concatenate([zeros, x], axis=1)], axis=0)

    sts = []
    for kvh in range(SWA_KV_HEADS):
        ks = slice(kvh * SWA_HEAD_DIM, (kvh + 1) * SWA_HEAD_DIM)
        col0 = kvh * SWA_GROUP * SWA_HEAD_DIM
        xq = jnp.concatenate([q_ref[:, col0 + p * pw:col0 + (p + 1) * pw].astype(BF16)
                              for p in range(pairs)], axis=0)
        sts.append(_dot_nt(block_diag(kband[:, ks]), xq))
    for kvh in range(SWA_KV_HEADS):
        ks = slice(kvh * SWA_HEAD_DIM, (kvh + 1) * SWA_HEAD_DIM)
        col0 = kvh * SWA_GROUP * SWA_HEAD_DIM
        vt = vband_t[ks, :]
        v2t = jnp.concatenate([jnp.concatenate([vt, zeros_t], axis=1),
                               jnp.concatenate([zeros_t, vt], axis=1)], axis=0)
        pbs, rdens = [], []
        for pr in range(pairs):
            head = kvh * SWA_GROUP + 2 * pr
            s2 = (sts[kvh][:, pr * SEQ_BLOCK:(pr + 1) * SEQ_BLOCK].reshape(2, 2 * SEQ_BLOCK, SEQ_BLOCK)
                  + madd[None])
            sink = jnp.concatenate([jnp.full((1, 1, SEQ_BLOCK), sink_ref[head] * LOG2E, F32),
                                    jnp.full((1, 1, SEQ_BLOCK), sink_ref[head + 1] * LOG2E, F32)], axis=0)
            m = jnp.maximum(jnp.max(s2, axis=1, keepdims=True), sink)
            p = jnp.exp2(s2 - m)
            den = jnp.sum(p, axis=1, keepdims=True) + jnp.exp2(sink - m)
            pbs.append(p.astype(BF16).reshape(4 * SEQ_BLOCK, SEQ_BLOCK))
            rdens.append(1.0 / den)
        for pr in range(pairs):
            ot = _dot(v2t, pbs[pr]).reshape(2, SWA_HEAD_DIM, SEQ_BLOCK) * rdens[pr]
            o = ot.reshape(2 * SWA_HEAD_DIM, SEQ_BLOCK).T
            cs = slice(col0 + pr * pw, col0 + (pr + 1) * pw)
            og_ref[:, cs] = (o * _silu(g_ref[:, cs])).astype(og_ref.dtype)


def _attn_seq(h1, h1_meta, sinks, *, n_seq, seq_len, meta_row_block):
    nb = seq_len // SEQ_BLOCK
    width = SWA_Q_HEADS * SWA_HEAD_DIM
    kcol = 2 * width // SWA_KV_WIDTH
    return pl.pallas_call(
        _attn_seq_kernel,
        grid=(n_seq, nb),
        in_specs=[pl.BlockSpec((SEQ_BLOCK, width), lambda b, j: (b * nb + j, 0)),
                  pl.BlockSpec((SEQ_BLOCK, width), lambda b, j: (b * nb + j, 1)),
                  pl.BlockSpec((SEQ_BLOCK, SWA_KV_WIDTH), lambda b, j: (b * nb + j, kcol)),
                  pl.BlockSpec((SEQ_BLOCK, SWA_KV_WIDTH), lambda b, j: (b * nb + j, kcol + 1)),
                  pl.BlockSpec((SEQ_BLOCK, SWA_KV_WIDTH), lambda b, j: (b * nb + jnp.maximum(j - 1, 0), kcol)),
                  pl.BlockSpec((SEQ_BLOCK, SWA_KV_WIDTH), lambda b, j: (b * nb + jnp.maximum(j - 1, 0), kcol + 1)),
                  pl.BlockSpec((SEQ_BLOCK, 2 * SWA_KV_WIDTH), lambda b, j: (meta_row_block, kcol // 2)),
                  pl.BlockSpec(memory_space=pltpu.SMEM)],
        out_specs=pl.BlockSpec((SEQ_BLOCK, width), lambda b, j: (b * nb + j, 0)),
        out_shape=jax.ShapeDtypeStruct((n_seq * seq_len, width), BF16),
        compiler_params=pltpu.CompilerParams(
            dimension_semantics=("parallel", "arbitrary"), vmem_limit_bytes=VMEM_LIMIT),
        name="attn_seq",
    )(h1, h1, h1, h1, h1, h1, h1_meta, sinks)


def _attn_step_kernel(q_ref, g_ref, kn_ref, vn_ref, ck_ref, cv_ref, sink_ref, og_ref, nk_ref, nv_ref, *, n_seq, t):
    keys = WINDOW + t
    hd = SWA_HEAD_DIM
    tiles = SWA_Q_HEADS // 2
    kj = lax.broadcasted_iota(jnp.int32, (keys, SWA_Q_HEADS * t), 0)
    qt = lax.broadcasted_iota(jnp.int32, (keys, SWA_Q_HEADS * t), 1) % t
    madd = jnp.where((kj >= qt + 1) & (kj <= WINDOW + qt), 0.0, -jnp.inf)
    low = lax.broadcasted_iota(jnp.int32, (t, 2 * hd), 1) < hd
    zero_tile = jnp.zeros((t, 2 * hd), F32)
    sink = sink_ref[...] * LOG2E

    def group(i, carry):
        seqs = [i * SEQ_UNROLL + n for n in range(SEQ_UNROLL)]
        rows = [pl.ds(pl.multiple_of(b * t, t), t) for b in seqs]
        st, vall = [], []
        for b, rw in zip(seqs, rows):
            kc, vc = ck_ref[b], cv_ref[b]
            kn, vn = kn_ref[rw, :], vn_ref[rw, :]
            nk_ref[b, 0:WINDOW - t, :] = kc[t:, :]
            nk_ref[b, WINDOW - t:WINDOW, :] = kn
            nv_ref[b, 0:WINDOW - t, :] = vc[t:, :]
            nv_ref[b, WINDOW - t:WINDOW, :] = vn
            kall = jnp.concatenate([kc, kn], axis=0).astype(BF16)
            vall.append(jnp.concatenate([vc, vn], axis=0).astype(BF16))
            q = q_ref[rw, :] * (SWA_SCALE * LOG2E)
            qtile = [q[:, j * 2 * hd:(j + 1) * 2 * hd] for j in range(tiles)]
            qswap = [pltpu.roll(x, hd, 1) for x in qtile]
            groups = []
            for kvh in range(SWA_KV_HEADS):
                for gq in range(SWA_GROUP):
                    j = (kvh * SWA_GROUP + gq) // 2
                    src = qtile[j] if gq % 2 == kvh % 2 else qswap[j]
                    half = jnp.where(low, src, 0.0) if kvh % 2 == 0 else jnp.where(low, 0.0, src)
                    groups.append(jnp.concatenate(
                        [half if c == kvh // 2 else zero_tile for c in range(SWA_KV_HEADS // 2)], axis=1))
            qbd = jnp.concatenate(groups, axis=0).astype(BF16)
            st.append(_dot_nt(kall, qbd))
        pn = []
        for s in st:
            s = s + madd
            m = jnp.maximum(jnp.max(s, axis=0, keepdims=True), sink)
            p = jnp.exp2(s - m)
            den = jnp.sum(p, axis=0, keepdims=True) + jnp.exp2(sink - m)
            pn.append((p * (1.0 / den)).astype(BF16))
        of = [_dot_tn(p, vl) for p, vl in zip(pn, vall)]
        for o, rw in zip(of, rows):
            g = g_ref[rw, :]
            out = []
            for j in range(tiles):
                kvh = (2 * j) // SWA_GROUP
                ct = slice((kvh // 2) * 2 * hd, (kvh // 2 + 1) * 2 * hd)
                ra = o[(2 * j) * t:(2 * j + 1) * t, ct]
                rb = o[(2 * j + 1) * t:(2 * j + 2) * t, ct]
                if kvh % 2 == 0:
                    out.append(jnp.where(low, ra, pltpu.roll(rb, hd, 1)))
                else:
                    out.append(jnp.where(low, pltpu.roll(ra, hd, 1), rb))
            og_ref[rw, :] = (jnp.concatenate(out, axis=1) * _silu(g)).astype(og_ref.dtype)
        return carry

    lax.fori_loop(0, n_seq // SEQ_UNROLL, group, 0)


def _attn_step(h1, cache_k, cache_v, sink_cols, *, n_seq, t, seq_per_block):
    assert n_seq % seq_per_block == 0
    rows = seq_per_block * t
    width = SWA_Q_HEADS * SWA_HEAD_DIM
    kcol = 2 * width // SWA_KV_WIDTH
    cache_spec = pl.BlockSpec((seq_per_block, WINDOW, SWA_KV_WIDTH), lambda i: (i, 0, 0))
    return pl.pallas_call(
        functools.partial(_attn_step_kernel, n_seq=seq_per_block, t=t),
        grid=(n_seq // seq_per_block,),
        in_specs=[pl.BlockSpec((rows, width), lambda i: (i, 0)),
                  pl.BlockSpec((rows, width), lambda i: (i, 1)),
                  pl.BlockSpec((rows, SWA_KV_WIDTH), lambda i: (i, kcol)),
                  pl.BlockSpec((rows, SWA_KV_WIDTH), lambda i: (i, kcol + 1)),
                  cache_spec, cache_spec,
                  pl.BlockSpec((1, SWA_Q_HEADS * t), lambda i: (0, 0))],
        out_specs=[pl.BlockSpec((rows, width), lambda i: (i, 0)), cache_spec, cache_spec],
        out_shape=[jax.ShapeDtypeStruct((n_seq * t, width), BF16),
                   jax.ShapeDtypeStruct(cache_k.shape, F32),
                   jax.ShapeDtypeStruct(cache_v.shape, F32)],
        compiler_params=pltpu.CompilerParams(
            dimension_semantics=("parallel",), vmem_limit_bytes=VMEM_LIMIT),
        name="attn_step",
    )(h1, h1, h1, h1, cache_k, cache_v, sink_cols)


def kernel(x_prompt, x_sample, state_hgrn, cache_swa_k, cache_swa_v, meta_tokens,
           hgrn_w_in, hgrn_lb_logits, hgrn_norm_w, hgrn_w_out,
           swa_w_in, swa_sinks, swa_w_out, ln_g, ln_b):
    out_dtype = x_prompt.dtype
    bsz, seq, d = x_prompt.shape
    dec_b, dec_t, _ = x_sample.shape
    n_p = bsz * seq
    n_s = dec_b * dec_t
    width = SWA_Q_HEADS * SWA_HEAD_DIM

    w_out0 = hgrn_w_out[0].astype(BF16)
    w_out1 = swa_w_out[0].astype(BF16)
    in_tile = SHORT_COL_TILE
    q_tiles = width // in_tile
    kv_tile = 2 * SWA_KV_WIDTH // in_tile
    assert width % in_tile == 0 and (2 * SWA_KV_WIDTH) % in_tile == 0 and kv_tile == 1
    regroup = lambda j: jnp.where(j < q_tiles, j, jnp.where(j < 2 * q_tiles, j + kv_tile, q_tiles))
    lb = jnp.cumsum(jax.nn.softmax(hgrn_lb_logits.astype(F32), axis=0), axis=0)[0:1]
    nw = hgrn_norm_w[0].astype(F32).reshape(1, -1)
    sinks = swa_sinks[0].astype(F32).reshape(1, SWA_Q_HEADS)
    sink_cols = jnp.repeat(sinks, dec_t, axis=1)
    g0, b0 = ln_g[0:1].astype(F32), ln_b[0:1].astype(F32)
    g1, b1 = ln_g[1:2].astype(F32), ln_b[1:2].astype(F32)

    x_p = x_prompt.astype(F32).reshape(n_p, d)
    meta_block = jnp.concatenate(
        [jnp.zeros((SEQ_BLOCK - N_META, d), F32), meta_tokens.astype(F32)], axis=0)
    x_s = x_sample.astype(F32).reshape(n_s, d)
    meta_blk = n_s // SEQ_BLOCK

    h_sm, w_in0 = _matmul_cast(x_s, meta_block, hgrn_w_in[0].astype(F32), xs_rows=n_s, xm_block=0, tn=in_tile)
    h_p = _matmul(x_p, w_in0, tm=PROJ_ROW_TILE, tn=PROJ_IN0_COL_TILE)
    zero_state = jnp.zeros((1, HGRN_HEADS, HGRN_DK, HGRN_DV), F32)
    og_meta, s_meta = _hgrn_seq(h_sm, lb, nw, zero_state, n_seq=1, seq_len=SEQ_BLOCK, row_block0=meta_blk)
    og_s, st_s = _hgrn_step(h_sm, lb, nw, state_hgrn[0].astype(F32), n_seq=dec_b, t=dec_t, seq_per_block=STEP_STATE_SEQS)
    og_p, st_p = _hgrn_seq(h_p, lb, nw, s_meta, n_seq=bsz, seq_len=seq, row_block0=0)
    x1_sm = _matmul_deepnorm_tail(og_s, og_meta, w_out0, x_s, meta_block, g0, b0)
    x1_p = _matmul_deepnorm(og_p, w_out0, x_p, g0, b0, tm=PROJ_OUT_ROW_TILE)

    h1_sm, w_in1 = _matmul_cast(x1_sm, x1_sm, swa_w_in[0].astype(F32), xs_rows=n_s, xm_block=meta_blk,
                                tn=in_tile, col_tile=regroup)
    h1_p = _matmul(x1_p, w_in1, tm=PROJ_ROW_TILE, tn=PROJ_IN1_COL_TILE)
    og1_p = _attn_seq(h1_p, h1_sm, sinks.reshape(-1), n_seq=bsz, seq_len=seq, meta_row_block=meta_blk)
    ck = cache_swa_k[0].astype(F32).reshape(dec_b, WINDOW, SWA_KV_WIDTH)
    cv = cache_swa_v[0].astype(F32).reshape(dec_b, WINDOW, SWA_KV_WIDTH)
    og1_s, nk_s, nv_s = _attn_step(h1_sm, ck, cv, sink_cols, n_seq=dec_b, t=dec_t, seq_per_block=STEP_CACHE_SEQS)
    y_p = _matmul_deepnorm(og1_p, w_out1, x1_p, g1, b1, tm=PROJ_OUT_ROW_TILE)
    y_s = _matmul_deepnorm(og1_s, w_out1, x1_sm, g1, b1, tm=PROJ_OUT_ROW_TILE)

    kv_p = h1_p.reshape(bsz, seq, -1)[:, seq - WINDOW:, 2 * width:]
    cache_shape = (1, bsz, WINDOW, SWA_KV_HEADS, SWA_HEAD_DIM)
    return (y_p.reshape(bsz, seq, d).astype(out_dtype),
            y_s.reshape(dec_b, dec_t, d).astype(out_dtype),
            st_p[None].astype(out_dtype),
            st_s[None].astype(out_dtype),
            kv_p[..., :SWA_KV_WIDTH].reshape(cache_shape).astype(out_dtype),
            kv_p[..., SWA_KV_WIDTH:].reshape(cache_shape).astype(out_dtype),
            nk_s.reshape((1,) + cache_swa_k.shape[1:]).astype(out_dtype),
            nv_s.reshape((1,) + cache_swa_v.shape[1:]).astype(out_dtype))
```

```python
import functools

import numpy as np
import jax
import jax.numpy as jnp
from jax import lax
from jax.experimental import pallas as pl
from jax.experimental.pallas import tpu as pltpu

F32 = jnp.float32
BF16 = jnp.bfloat16

D_MODEL = 2048
N_META = 16
DEPTH = 2
HGRN_HEADS = 16
HGRN_DK = 128
HGRN_DV = 128
SWA_Q_HEADS = 32
SWA_KV_HEADS = 4
SWA_GROUP = 8
SWA_HEAD_DIM = 64
SWA_KV_WIDTH = SWA_KV_HEADS * SWA_HEAD_DIM
SWA_SCALE = SWA_HEAD_DIM ** -0.5
WINDOW = 128
DEEPNORM_ALPHA = (2.0 * DEPTH) ** 0.25
LN_EPS = 1e-5
RMS_EPS = 1e-6
LOG2E = 1.4426950408889634

CHUNK = 128
LEVELS = 7
MAX_FACTORED_LOG2_DECAY = 80.0
MASK_ROWS = 16
SEQ_BLOCK = 128
LN_SUB_ROWS = 128
STEP_GROUP = 32
SEQ_UNROLL = 8
HGRN_SEQ_HEADS = 16
HGRN_SEQ_ROWS = 256

V7X_VMEM_BYTES = 64 * 1024 * 1024
VMEM_LIMIT = V7X_VMEM_BYTES - 4 * 1024 * 1024
PROJ_ROW_TILE = 1024
PROJ_IN0_COL_TILE = 2048
PROJ_IN1_COL_TILE = 2304
PROJ_OUT_ROW_TILE = 1024
SHORT_COL_TILE = 512
STEP_STATE_SEQS = 128
STEP_CACHE_SEQS = 32


def _dot(a, b):
    return jnp.dot(a, b, preferred_element_type=F32)


def _dot_nt(a, b):
    return lax.dot_general(a, b, (((1,), (1,)), ((), ())), preferred_element_type=F32)


def _dot_tn(a, b):
    return lax.dot_general(a, b, (((0,), (0,)), ((), ())), preferred_element_type=F32)


def _silu(x):
    return x * jax.nn.sigmoid(x)


def _mm_kernel(x_ref, w_ref, o_ref):
    o_ref[...] = _dot(x_ref[...].astype(BF16), w_ref[...])


def _row_tile(n, want):
    if n <= want:
        return n
    return max(t for t in range(16, want + 1, 16) if n % t == 0)


def _matmul(x, w, *, tm, tn):
    n, k = x.shape
    e = w.shape[1]
    tm = _row_tile(n, tm)
    assert n % tm == 0 and e % tn == 0
    return pl.pallas_call(
        _mm_kernel,
        grid=(n // tm, e // tn),
        in_specs=[pl.BlockSpec((tm, k), lambda i, j: (i, 0)),
                  pl.BlockSpec((k, tn), lambda i, j: (0, j))],
        out_specs=pl.BlockSpec((tm, tn), lambda i, j: (i, j)),
        out_shape=jax.ShapeDtypeStruct((n, e), F32),
        compiler_params=pltpu.CompilerParams(
            dimension_semantics=("parallel", "parallel"), vmem_limit_bytes=VMEM_LIMIT),
        name="proj_in",
    )(x, w)


def _mm_cast_kernel(xs_ref, xm_ref, w_ref, o_ref, wb_ref, xb_scr):
    @pl.when(pl.program_id(0) == 0)
    def _():
        n_s = xs_ref.shape[0]
        xb_scr[0:n_s, :] = xs_ref[...].astype(BF16)
        xb_scr[n_s:, :] = xm_ref[...].astype(BF16)

    wb = w_ref[...].astype(BF16)
    wb_ref[...] = wb
    o_ref[...] = _dot(xb_scr[...], wb)


def _matmul_cast(xs, xm, w, *, xs_rows, xm_block, tn, col_tile=lambda j: j):
    k, e = w.shape
    n = xs_rows + SEQ_BLOCK
    assert e % tn == 0
    return pl.pallas_call(
        _mm_cast_kernel,
        grid=(e // tn,),
        in_specs=[pl.BlockSpec((xs_rows, k), lambda j: (0, 0)),
                  pl.BlockSpec((SEQ_BLOCK, k), lambda j: (xm_block, 0)),
                  pl.BlockSpec((k, tn), lambda j: (0, col_tile(j)))],
        out_specs=[pl.BlockSpec((n, tn), lambda j: (0, j)),
                   pl.BlockSpec((k, tn), lambda j: (0, j))],
        out_shape=[jax.ShapeDtypeStruct((n, e), F32), jax.ShapeDtypeStruct((k, e), BF16)],
        scratch_shapes=[pltpu.VMEM((n, k), BF16)],
        compiler_params=pltpu.CompilerParams(
            dimension_semantics=("arbitrary",), vmem_limit_bytes=VMEM_LIMIT),
        name="proj_in_cast",
    )(xs, xm, w)


def _mm_ln_tail_kernel(a_ref, at_ref, w_ref, x_ref, xt_ref, g_ref, b_ref, o_ref):
    tail = pl.program_id(0) == pl.num_programs(0) - 1
    a = jnp.where(tail, at_ref[...], a_ref[...])
    x = jnp.where(tail, xt_ref[...], x_ref[...])
    z = DEEPNORM_ALPHA * x + _dot(a, w_ref[...])
    mu = jnp.mean(z, axis=-1, keepdims=True)
    zc = z - mu
    var = jnp.mean(zc * zc, axis=-1, keepdims=True)
    o_ref[...] = zc * lax.rsqrt(var + LN_EPS) * g_ref[...] + b_ref[...]


def _matmul_deepnorm_tail(a, a_tail, w, x, x_tail, g, b):
    n, k = a.shape
    d = w.shape[1]
    tm = SEQ_BLOCK
    last = n // tm - 1
    body = lambda i: (jnp.minimum(i, last), 0)
    const = lambda i: (0, 0)
    return pl.pallas_call(
        _mm_ln_tail_kernel,
        grid=(n // tm + 1,),
        in_specs=[pl.BlockSpec((tm, k), body), pl.BlockSpec((tm, k), const),
                  pl.BlockSpec((k, d), const),
                  pl.BlockSpec((tm, d), body), pl.BlockSpec((tm, d), const),
                  pl.BlockSpec((1, d), const), pl.BlockSpec((1, d), const)],
        out_specs=pl.BlockSpec((tm, d), lambda i: (i, 0)),
        out_shape=jax.ShapeDtypeStruct((n + tm, d), F32),
        compiler_params=pltpu.CompilerParams(
            dimension_semantics=("parallel",), vmem_limit_bytes=VMEM_LIMIT),
        name="proj_out_deepnorm_tail",
    )(a, a_tail, w, x, x_tail, g, b)


def _mm_ln_kernel(a_ref, w_ref, x_ref, g_ref, b_ref, o_ref):
    tm = a_ref.shape[0]
    sub = LN_SUB_ROWS if tm % LN_SUB_ROWS == 0 else tm
    for r0 in range(0, tm, sub):
        rows = slice(r0, r0 + sub)
        z = DEEPNORM_ALPHA * x_ref[rows, :] + _dot(a_ref[rows, :], w_ref[...])
        mu = jnp.mean(z, axis=-1, keepdims=True)
        zc = z - mu
        var = jnp.mean(zc * zc, axis=-1, keepdims=True)
        o_ref[rows, :] = zc * lax.rsqrt(var + LN_EPS) * g_ref[...] + b_ref[...]


def _matmul_deepnorm(a, w, x, g, b, *, tm):
    n, k = a.shape
    d = w.shape[1]
    tm = _row_tile(n, tm)
    return pl.pallas_call(
        _mm_ln_kernel,
        grid=(n // tm,),
        in_specs=[pl.BlockSpec((tm, k), lambda i: (i, 0)),
                  pl.BlockSpec((k, d), lambda i: (0, 0)),
                  pl.BlockSpec((tm, d), lambda i: (i, 0)),
                  pl.BlockSpec((1, d), lambda i: (0, 0)),
                  pl.BlockSpec((1, d), lambda i: (0, 0))],
        out_specs=pl.BlockSpec((tm, d), lambda i: (i, 0)),
        out_shape=jax.ShapeDtypeStruct((n, d), F32),
        compiler_params=pltpu.CompilerParams(
            dimension_semantics=("parallel",), vmem_limit_bytes=VMEM_LIMIT),
        name="proj_out_deepnorm",
    )(a, w, x, g, b)


def _level_masks():
    t = np.arange(CHUNK)
    out = np.zeros((LEVELS + 1, CHUNK, CHUNK), np.float32)
    out[0] = t[:, None] > t[None, :]
    for l in range(LEVELS):
        h = 1 << l
        same = (t[:, None] >> (l + 1)) == (t[None, :] >> (l + 1))
        out[l + 1] = same & ((t[:, None] & h) != 0) & ((t[None, :] & h) == 0)
    return out


def _gates(fx, lb):
    f = lb + (1.0 - lb) * jax.nn.sigmoid(fx)
    return f, jnp.log2(f), 1.0 - f


def _prefix_scan(logf):
    tiles = (CHUNK // 8, 8, HGRN_DK)
    c = logf.reshape(tiles)
    sub = lax.broadcasted_iota(jnp.int32, (1, 8, HGRN_DK), 1)
    bcast = lambda x, r: jnp.broadcast_to(x[:, r:r + 1, :], tiles)
    c = c + jnp.where((sub & 1) != 0, pltpu.roll(c, 1, 1), 0.0)
    c = c + jnp.where((sub & 2) != 0, jnp.where(sub < 4, bcast(c, 1), bcast(c, 5)), 0.0)
    c = c + jnp.where((sub & 4) != 0, bcast(c, 3), 0.0)
    c = c.reshape(CHUNK, HGRN_DK)
    for level in range(3, LEVELS):
        half = 1 << level
        pieces = []
        for r0 in range(0, CHUNK, 2 * half):
            pieces += [c[r0:r0 + half], c[r0 + half:r0 + 2 * half] + c[r0 + half - 1:r0 + half, :]]
        c = jnp.concatenate(pieces, axis=0)
    return c


def _small_levels(logf, f, q, kk):
    tiles = (CHUNK // 8, 8, HGRN_DK)
    c, f3, q3, k3 = (x.reshape(tiles) for x in (logf, f, q, kk))
    sub = lax.broadcasted_iota(jnp.int32, (1, 8, HGRN_DK), 1)
    bcast = lambda x, r: jnp.broadcast_to(x[:, r:r + 1, :], tiles)
    up = (sub & 1) != 0
    zs = [jnp.where(up, q3 * f3, k3)]
    c = c + jnp.where(up, pltpu.roll(c, 1, 1), 0.0)
    for level, tot in ((1, lambda c: jnp.where(sub < 4, bcast(c, 1), bcast(c, 5))),
                       (2, lambda c: bcast(c, 3))):
        up = (sub & (1 << level)) != 0
        t = tot(c)
        zs.append(jnp.where(up, q3, k3) * jnp.exp2(jnp.where(up, c, t - c)))
        c = c + jnp.where(up, t, 0.0)
    return [z.reshape(CHUNK, HGRN_DK) for z in zs], c.reshape(CHUNK, HGRN_DK)


def _level_large(cum, q, kk, level):
    half = 1 << level
    args, bases, cums = [], [], []
    for r0 in range(0, CHUNK, 2 * half):
        lo, up = slice(r0, r0 + half), slice(r0 + half, r0 + 2 * half)
        tot = cum[r0 + half - 1:r0 + half, :]
        args += [tot - cum[lo], cum[up]]
        bases += [kk[lo], q[up]]
        cums += [cum[lo], cum[up] + tot]
    x = jnp.exp2(jnp.concatenate(args, axis=0))
    return jnp.concatenate(bases, axis=0) * x, jnp.concatenate(cums, axis=0)


def _split3(x):
    hi = x.astype(BF16)
    r = x - hi.astype(F32)
    mid = r.astype(BF16)
    lo = (r - mid.astype(F32)).astype(BF16)
    return hi, mid, lo


def _state_update(s_old, kd, v, blast, pad_rows):
    c = kd.shape[0]
    hi, mid, lo = _split3(jnp.exp2(blast))
    row = lax.broadcasted_iota(jnp.int32, (pad_rows, HGRN_DK), 0)
    dec = jnp.where(row == 0, hi.astype(F32),
                    jnp.where(row == 1, mid.astype(F32), jnp.where(row == 2, lo.astype(F32), 0.0)))
    piece = BF16 if c % 16 == 0 and pad_rows % 16 == 0 else F32
    lhs = jnp.concatenate([kd.astype(piece), dec.astype(piece)], axis=0).astype(BF16)
    rhs = jnp.concatenate(
        [jnp.concatenate([v.astype(piece), jnp.zeros((c, HGRN_DV), piece)], axis=1),
         jnp.concatenate([jnp.zeros((pad_rows, HGRN_DV), piece), jnp.ones((pad_rows, HGRN_DV), piece)], axis=1)],
        axis=0).astype(BF16)
    both = _dot_tn(lhs, rhs)
    return both[:, HGRN_DV:] * s_old + both[:, :HGRN_DV]


def _norm_gate(o, g, nw):
    o = o * lax.rsqrt(jnp.mean(o * o, axis=-1, keepdims=True) + RMS_EPS) * nw
    return o * _silu(g)


def _hgrn_seq_kernel(q_ref, fx_ref, i_ref, g_ref, lb_ref, nw_ref, s0_ref, m_ref,
                     og_ref, sout_ref, s_scr, o_scr, *, n_chunks):
    heads = s_scr.shape[0]

    @pl.when(pl.program_id(2) == 0)
    def _():
        s_scr[...] = s0_ref[0]

    def chunk(c, carry):
        rows = pl.ds(pl.multiple_of(c * CHUNK, CHUNK), CHUNK)
        hcols = [slice(hd * HGRN_DK, (hd + 1) * HGRN_DK) for hd in range(heads)]
        q = [q_ref[rows, cs] for cs in hcols]
        v = [i_ref[rows, cs] for cs in hcols]
        gates = [_gates(fx_ref[rows, cs], lb_ref[:, cs]) for cs in hcols]
        kk = [gt[2] for gt in gates]
        cum = [_prefix_scan(gt[1]) for gt in gates]
        mid = CHUNK // 2 - 1
        spread = None
        for b in cum:
            s_hd = jnp.maximum(b[0:1, :] - b[mid:mid + 1, :], b[mid:mid + 1, :] - b[CHUNK - 1:CHUNK, :])
            spread = s_hd if spread is None else jnp.maximum(spread, s_hd)
        factorable = jnp.max(spread) <= MAX_FACTORED_LOG2_DECAY

        vb = [x.astype(BF16) for x in v]
        for hd in range(heads):
            d = cum[hd] - cum[hd][mid:mid + 1, :]
            qe = (q[hd] * jnp.exp2(d)).astype(BF16)
            ke = (kk[hd] * jnp.exp2(-d)).astype(BF16)
            amat = jnp.where(m_ref[0] > 0.5, _dot_nt(qe, ke), 0.0).astype(BF16)
            blast = cum[hd][CHUNK - 1:CHUNK, :]
            qd = (q[hd] * jnp.exp2(cum[hd])).astype(BF16)
            kd = kk[hd] * jnp.exp2(blast - cum[hd])
            s_old = s_scr[hd]
            base = _dot(qd, s_old.astype(BF16)) + jnp.sum(q[hd] * kk[hd], axis=-1, keepdims=True) * v[hd]
            o_scr[hd] = base
            s_scr[hd] = _state_update(s_old, kd, vb[hd], blast, 16)
            og_ref[rows, hcols[hd]] = _norm_gate(
                base + _dot(amat, vb[hd]), g_ref[rows, hcols[hd]], nw_ref[:, hcols[hd]]).astype(og_ref.dtype)

        @pl.when(jnp.logical_not(factorable))
        def _():
            small = [_small_levels(gt[1], gt[0], q[hd], kk[hd]) for hd, gt in enumerate(gates)]
            zs = [sm[0] for sm in small]
            part = [sm[1] for sm in small]
            for level in range(3, LEVELS):
                for hd in range(heads):
                    z, part[hd] = _level_large(part[hd], q[hd], kk[hd], level)
                    zs[hd].append(z)
            nblk = CHUNK // MASK_ROWS
            a = [[None] * nblk for _ in range(heads)]
            for level in range(LEVELS):
                step = 1 << (level - 4) if level >= 4 else 0
                blocks = [i for i in range(nblk) if level < 4 or (i & step)]
                for hd in range(heads):
                    z = zs[hd][level].astype(BF16)
                    lhs = z if level < 4 else jnp.concatenate(
                        [z[i * MASK_ROWS:(i + 1) * MASK_ROWS] for i in blocks], axis=0)
                    term = _dot_nt(lhs, z)
                    for n, i in enumerate(blocks):
                        t = (term[n * MASK_ROWS:(n + 1) * MASK_ROWS]
                             * m_ref[level + 1, i * MASK_ROWS:(i + 1) * MASK_ROWS, :])
                        a[hd][i] = t if a[hd][i] is None else a[hd][i] + t
            for hd in range(heads):
                amat = jnp.concatenate(a[hd], axis=0).astype(BF16)
                og_ref[rows, hcols[hd]] = _norm_gate(
                    o_scr[hd] + _dot(amat, vb[hd]),
                    g_ref[rows, hcols[hd]], nw_ref[:, hcols[hd]]).astype(og_ref.dtype)
        return carry

    lax.fori_loop(0, n_chunks, chunk, 0)

    @pl.when(pl.program_id(2) == pl.num_programs(2) - 1)
    def _():
        sout_ref[0] = s_scr[...]


def _hgrn_seq(h, lb, nw, s0, *, n_seq, seq_len, row_block0):
    hb = HGRN_SEQ_HEADS
    width = hb * HGRN_DK
    groups = HGRN_HEADS // hb
    rb = min(HGRN_SEQ_ROWS, seq_len)
    nrb = seq_len // rb
    assert seq_len % rb == 0 and rb % CHUNK == 0 and (row_block0 * SEQ_BLOCK) % rb == 0
    rb0 = row_block0 * SEQ_BLOCK // rb
    s0_batched = s0.shape[0] != 1
    col = lambda part: (lambda b, hg, r: (rb0 + b * nrb + r, part * groups + hg))
    masks = jnp.asarray(_level_masks())
    return pl.pallas_call(
        functools.partial(_hgrn_seq_kernel, n_chunks=rb // CHUNK),
        grid=(n_seq, groups, nrb),
        in_specs=[pl.BlockSpec((rb, width), col(0)),
                  pl.BlockSpec((rb, width), col(1)),
                  pl.BlockSpec((rb, width), col(2)),
                  pl.BlockSpec((rb, width), col(3)),
                  pl.BlockSpec((1, width), lambda b, hg, r: (0, hg)),
                  pl.BlockSpec((1, width), lambda b, hg, r: (0, hg)),
                  pl.BlockSpec((1, hb, HGRN_DK, HGRN_DV),
                               (lambda b, hg, r: (b, hg, 0, 0)) if s0_batched else (lambda b, hg, r: (0, hg, 0, 0))),
                  pl.BlockSpec((LEVELS + 1, CHUNK, CHUNK), lambda b, hg, r: (0, 0, 0))],
        out_specs=[pl.BlockSpec((rb, width), lambda b, hg, r: (b * nrb + r, hg)),
                   pl.BlockSpec((1, hb, HGRN_DK, HGRN_DV), lambda b, hg, r: (b, hg, 0, 0))],
        out_shape=[jax.ShapeDtypeStruct((n_seq * seq_len, HGRN_HEADS * HGRN_DV), BF16),
                   jax.ShapeDtypeStruct((n_seq, HGRN_HEADS, HGRN_DK, HGRN_DV), F32)],
        scratch_shapes=[pltpu.VMEM((hb, HGRN_DK, HGRN_DV), F32),
                        pltpu.VMEM((hb, CHUNK, HGRN_DV), F32)],
        compiler_params=pltpu.CompilerParams(
            dimension_semantics=("parallel", "parallel", "arbitrary"), vmem_limit_bytes=VMEM_LIMIT),
        name="hgrn_seq",
    )(h, h, h, h, lb, nw, s0, masks)


def _hgrn_step_kernel(q_ref, fx_ref, i_ref, g_ref, lb_ref, nw_ref, s_ref, og_ref, sout_ref, *, n_seq, t):
    lb = lb_ref[...]
    nw = nw_ref[...]
    tile = (STEP_GROUP, t, HGRN_DK)
    sub = lax.broadcasted_iota(jnp.int32, (1, t, HGRN_DK), 1)

    def group(i, carry):
        rows = pl.ds(pl.multiple_of(i * (STEP_GROUP * t), STEP_GROUP * t), STEP_GROUP * t)
        q = q_ref[rows, :].reshape(tile)
        v = i_ref[rows, :].reshape(tile)
        _, logf, kk = _gates(fx_ref[rows, :].reshape(tile), lb)
        cum = logf
        shift = 1
        while shift < t:
            cum = cum + jnp.where(sub >= shift, pltpu.roll(cum, shift, 1), 0.0)
            shift *= 2
        intra = (jnp.sum(q * kk, axis=-1, keepdims=True)) * v
        for d in range(1, t):
            valid = sub >= d
            x = jnp.exp2(jnp.where(valid, cum - pltpu.roll(cum, d, 1), 0.0))
            w = jnp.sum(jnp.where(valid, q * x * pltpu.roll(kk, d, 1), 0.0), axis=-1, keepdims=True)
            intra = intra + w * pltpu.roll(v, d, 1)
        blast = cum[:, t - 1:t, :]
        qd = q * jnp.exp2(cum)
        kd = kk * jnp.exp2(blast - cum)
        seqs = [i * STEP_GROUP + n for n in range(STEP_GROUP)]
        inter = [_dot(qd[n].astype(BF16), s_ref[b, 0].astype(BF16)) for n, b in enumerate(seqs)]
        for n, b in enumerate(seqs):
            sout_ref[b, 0] = _state_update(s_ref[b, 0], kd[n], v[n], blast[n], 8)
        o = jnp.stack(inter, axis=0) + intra
        og = _norm_gate(o, g_ref[rows, :].reshape(tile), nw)
        og_ref[rows, :] = og.reshape(STEP_GROUP * t, HGRN_DK).astype(og_ref.dtype)
        return carry

    lax.fori_loop(0, n_seq // STEP_GROUP, group, 0)


def _hgrn_step(h, lb, nw, s0, *, n_seq, t, seq_per_block):
    seq_per_block = min(seq_per_block, n_seq)
    assert n_seq % seq_per_block == 0 and seq_per_block % STEP_GROUP == 0 and t == 8
    rows = seq_per_block * t
    col = lambda part: (lambda i, hd: (i, part * HGRN_HEADS + hd))
    return pl.pallas_call(
        functools.partial(_hgrn_step_kernel, n_seq=seq_per_block, t=t),
        grid=(n_seq // seq_per_block, HGRN_HEADS),
        in_specs=[pl.BlockSpec((rows, HGRN_DK), col(0)),
                  pl.BlockSpec((rows, HGRN_DK), col(1)),
                  pl.BlockSpec((rows, HGRN_DV), col(2)),
                  pl.BlockSpec((rows, HGRN_DV), col(3)),
                  pl.BlockSpec((1, HGRN_DK), lambda i, hd: (0, hd)),
                  pl.BlockSpec((1, HGRN_DV), lambda i, hd: (0, hd)),
                  pl.BlockSpec((seq_per_block, 1, HGRN_DK, HGRN_DV), lambda i, hd: (i, hd, 0, 0))],
        out_specs=[pl.BlockSpec((rows, HGRN_DV), lambda i, hd: (i, hd)),
                   pl.BlockSpec((seq_per_block, 1, HGRN_DK, HGRN_DV), lambda i, hd: (i, hd, 0, 0))],
        out_shape=[jax.ShapeDtypeStruct((n_seq * t, HGRN_HEADS * HGRN_DV), BF16),
                   jax.ShapeDtypeStruct((n_seq, HGRN_HEADS, HGRN_DK, HGRN_DV), F32)],
        compiler_params=pltpu.CompilerParams(
            dimension_semantics=("parallel", "parallel"), vmem_limit_bytes=VMEM_LIMIT),
        name="hgrn_step",
    )(h, h, h, h, lb, nw, s0)


def _attn_seq_kernel(q_ref, g_ref, kc_ref, vc_ref, kp_ref, vp_ref, meta_ref, sink_ref, og_ref):
    first = pl.program_id(1) == 0
    kprev = jnp.where(first, meta_ref[:, :SWA_KV_WIDTH], kp_ref[...])
    vprev = jnp.where(first, meta_ref[:, SWA_KV_WIDTH:], vp_ref[...])
    kband = (jnp.concatenate([kprev, kc_ref[...]], axis=0) * (SWA_SCALE * LOG2E)).astype(BF16)
    vband_t = jnp.concatenate([vprev, vc_ref[...]], axis=0).T.astype(BF16)
    zeros_t = jnp.zeros((SWA_HEAD_DIM, 2 * SEQ_BLOCK), BF16)
    kj = lax.broadcasted_iota(jnp.int32, (2 * SEQ_BLOCK, SEQ_BLOCK), 0)
    qi = lax.broadcasted_iota(jnp.int32, (2 * SEQ_BLOCK, SEQ_BLOCK), 1)
    dist = SEQ_BLOCK + qi - kj
    valid = (dist >= 0) & (dist < WINDOW) & (jnp.logical_not(first) | (kj >= SEQ_BLOCK - N_META))
    madd = jnp.where(valid, 0.0, -jnp.inf)
    zeros = jnp.zeros((2 * SEQ_BLOCK, SWA_HEAD_DIM), BF16)
    pairs = SWA_GROUP // 2
    pw = 2 * SWA_HEAD_DIM

    def block_diag(x):
        return jnp.concatenate([jnp.concatenate([x, zeros], axis=1),
                                jnp.concatenate([zeros, x], axis=1)], axis=0)

    sts = []
    for kvh in range(SWA_KV_HEADS):
        ks = slice(kvh * SWA_HEAD_DIM, (kvh + 1) * SWA_HEAD_DIM)
        col0 = kvh * SWA_GROUP * SWA_HEAD_DIM
        xq = jnp.concatenate([q_ref[:, col0 + p * pw:col0 + (p + 1) * pw].astype(BF16)
                              for p in range(pairs)], axis=0)
        sts.append(_dot_nt(block_diag(kband[:, ks]), xq))
    for kvh in range(SWA_KV_HEADS):
        ks = slice(kvh * SWA_HEAD_DIM, (kvh + 1) * SWA_HEAD_DIM)
        col0 = kvh * SWA_GROUP * SWA_HEAD_DIM
        vt = vband_t[ks, :]
        v2t = jnp.concatenate([jnp.concatenate([vt, zeros_t], axis=1),
                               jnp.concatenate([zeros_t, vt], axis=1)], axis=0)
        pbs, rdens = [], []
        for pr in range(pairs):
            head = kvh * SWA_GROUP + 2 * pr
            s2 = (sts[kvh][:, pr * SEQ_BLOCK:(pr + 1) * SEQ_BLOCK].reshape(2, 2 * SEQ_BLOCK, SEQ_BLOCK)
                  + madd[None])
            sink = jnp.concatenate([jnp.full((1, 1, SEQ_BLOCK), sink_ref[head] * LOG2E, F32),
                                    jnp.full((1, 1, SEQ_BLOCK), sink_ref[head + 1] * LOG2E, F32)], axis=0)
            m = jnp.maximum(jnp.max(s2, axis=1, keepdims=True), sink)
            p = jnp.exp2(s2 - m)
            den = jnp.sum(p, axis=1, keepdims=True) + jnp.exp2(sink - m)
            pbs.append(p.astype(BF16).reshape(4 * SEQ_BLOCK, SEQ_BLOCK))
            rdens.append(1.0 / den)
        for pr in range(pairs):
            ot = _dot(v2t, pbs[pr]).reshape(2, SWA_HEAD_DIM, SEQ_BLOCK) * rdens[pr]
            o = ot.reshape(2 * SWA_HEAD_DIM, SEQ_BLOCK).T
            cs = slice(col0 + pr * pw, col0 + (pr + 1) * pw)
            og_ref[:, cs] = (o * _silu(g_ref[:, cs])).astype(og_ref.dtype)


def _attn_seq(h1, h1_meta, sinks, *, n_seq, seq_len, meta_row_block):
    nb = seq_len // SEQ_BLOCK
    width = SWA_Q_HEADS * SWA_HEAD_DIM
    kcol = 2 * width // SWA_KV_WIDTH
    return pl.pallas_call(
        _attn_seq_kernel,
        grid=(n_seq, nb),
        in_specs=[pl.BlockSpec((SEQ_BLOCK, width), lambda b, j: (b * nb + j, 0)),
                  pl.BlockSpec((SEQ_BLOCK, width), lambda b, j: (b * nb + j, 1)),
                  pl.BlockSpec((SEQ_BLOCK, SWA_KV_WIDTH), lambda b, j: (b * nb + j, kcol)),
                  pl.BlockSpec((SEQ_BLOCK, SWA_KV_WIDTH), lambda b, j: (b * nb + j, kcol + 1)),
                  pl.BlockSpec((SEQ_BLOCK, SWA_KV_WIDTH), lambda b, j: (b * nb + jnp.maximum(j - 1, 0), kcol)),
                  pl.BlockSpec((SEQ_BLOCK, SWA_KV_WIDTH), lambda b, j: (b * nb + jnp.maximum(j - 1, 0), kcol + 1)),
                  pl.BlockSpec((SEQ_BLOCK, 2 * SWA_KV_WIDTH), lambda b, j: (meta_row_block, kcol // 2)),
                  pl.BlockSpec(memory_space=pltpu.SMEM)],
        out_specs=pl.BlockSpec((SEQ_BLOCK, width), lambda b, j: (b * nb + j, 0)),
        out_shape=jax.ShapeDtypeStruct((n_seq * seq_len, width), BF16),
        compiler_params=pltpu.CompilerParams(
            dimension_semantics=("parallel", "arbitrary"), vmem_limit_bytes=VMEM_LIMIT),
        name="attn_seq",
    )(h1, h1, h1, h1, h1, h1, h1_meta, sinks)


def _attn_step_kernel(q_ref, g_ref, kn_ref, vn_ref, ck_ref, cv_ref, sink_ref, og_ref, nk_ref, nv_ref, *, n_seq, t):
    keys = WINDOW + t
    hd = SWA_HEAD_DIM
    tiles = SWA_Q_HEADS // 2
    kj = lax.broadcasted_iota(jnp.int32, (keys, SWA_Q_HEADS * t), 0)
    qt = lax.broadcasted_iota(jnp.int32, (keys, SWA_Q_HEADS * t), 1) % t
    madd = jnp.where((kj >= qt + 1) & (kj <= WINDOW + qt), 0.0, -jnp.inf)
    low = lax.broadcasted_iota(jnp.int32, (t, 2 * hd), 1) < hd
    zero_tile = jnp.zeros((t, 2 * hd), F32)
    sink = sink_ref[...] * LOG2E

    def group(i, carry):
        seqs = [i * SEQ_UNROLL + n for n in range(SEQ_UNROLL)]
        rows = [pl.ds(pl.multiple_of(b * t, t), t) for b in seqs]
        st, vall = [], []
        for b, rw in zip(seqs, rows):
            kc, vc = ck_ref[b], cv_ref[b]
            kn, vn = kn_ref[rw, :], vn_ref[rw, :]
            nk_ref[b, 0:WINDOW - t, :] = kc[t:, :]
            nk_ref[b, WINDOW - t:WINDOW, :] = kn
            nv_ref[b, 0:WINDOW - t, :] = vc[t:, :]
            nv_ref[b, WINDOW - t:WINDOW, :] = vn
            kall = jnp.concatenate([kc, kn], axis=0).astype(BF16)
            vall.append(jnp.concatenate([vc, vn], axis=0).astype(BF16))
            q = q_ref[rw, :] * (SWA_SCALE * LOG2E)
            qtile = [q[:, j * 2 * hd:(j + 1) * 2 * hd] for j in range(tiles)]
            qswap = [pltpu.roll(x, hd, 1) for x in qtile]
            groups = []
            for kvh in range(SWA_KV_HEADS):
                for gq in range(SWA_GROUP):
                    j = (kvh * SWA_GROUP + gq) // 2
                    src = qtile[j] if gq % 2 == kvh % 2 else qswap[j]
                    half = jnp.where(low, src, 0.0) if kvh % 2 == 0 else jnp.where(low, 0.0, src)
                    groups.append(jnp.concatenate(
                        [half if c == kvh // 2 else zero_tile for c in range(SWA_KV_HEADS // 2)], axis=1))
            qbd = jnp.concatenate(groups, axis=0).astype(BF16)
            st.append(_dot_nt(kall, qbd))
        pn = []
        for s in st:
            s = s + madd
            m = jnp.maximum(jnp.max(s, axis=0, keepdims=True), sink)
            p = jnp.exp2(s - m)
            den = jnp.sum(p, axis=0, keepdims=True) + jnp.exp2(sink - m)
            pn.append((p * (1.0 / den)).astype(BF16))
        of = [_dot_tn(p, vl) for p, vl in zip(pn, vall)]
        for o, rw in zip(of, rows):
            g = g_ref[rw, :]
            out = []
            for j in range(tiles):
                kvh = (2 * j) // SWA_GROUP
                ct = slice((kvh // 2) * 2 * hd, (kvh // 2 + 1) * 2 * hd)
                ra = o[(2 * j) * t:(2 * j + 1) * t, ct]
                rb = o[(2 * j + 1) * t:(2 * j + 2) * t, ct]
                if kvh % 2 == 0:
                    out.append(jnp.where(low, ra, pltpu.roll(rb, hd, 1)))
                else:
                    out.append(jnp.where(low, pltpu.roll(ra, hd, 1), rb))
            og_ref[rw, :] = (jnp.concatenate(out, axis=1) * _silu(g)).astype(og_ref.dtype)
        return carry

    lax.fori_loop(0, n_seq // SEQ_UNROLL, group, 0)


def _attn_step(h1, cache_k, cache_v, sink_cols, *, n_seq, t, seq_per_block):
    assert n_seq % seq_per_block == 0
    rows = seq_per_block * t
    width = SWA_Q_HEADS * SWA_HEAD_DIM
    kcol = 2 * width // SWA_KV_WIDTH
    cache_spec = pl.BlockSpec((seq_per_block, WINDOW, SWA_KV_WIDTH), lambda i: (i, 0, 0))
    return pl.pallas_call(
        functools.partial(_attn_step_kernel, n_seq=seq_per_block, t=t),
        grid=(n_seq // seq_per_block,),
        in_specs=[pl.BlockSpec((rows, width), lambda i: (i, 0)),
                  pl.BlockSpec((rows, width), lambda i: (i, 1)),
                  pl.BlockSpec((rows, SWA_KV_WIDTH), lambda i: (i, kcol)),
                  pl.BlockSpec((rows, SWA_KV_WIDTH), lambda i: (i, kcol + 1)),
                  cache_spec, cache_spec,
                  pl.BlockSpec((1, SWA_Q_HEADS * t), lambda i: (0, 0))],
        out_specs=[pl.BlockSpec((rows, width), lambda i: (i, 0)), cache_spec, cache_spec],
        out_shape=[jax.ShapeDtypeStruct((n_seq * t, width), BF16),
                   jax.ShapeDtypeStruct(cache_k.shape, F32),
                   jax.ShapeDtypeStruct(cache_v.shape, F32)],
        compiler_params=pltpu.CompilerParams(
            dimension_semantics=("parallel",), vmem_limit_bytes=VMEM_LIMIT),
        name="attn_step",
    )(h1, h1, h1, h1, cache_k, cache_v, sink_cols)


def kernel(x_prompt, x_sample, state_hgrn, cache_swa_k, cache_swa_v, meta_tokens,
           hgrn_w_in, hgrn_lb_logits, hgrn_norm_w, hgrn_w_out,
           swa_w_in, swa_sinks, swa_w_out, ln_g, ln_b):
    out_dtype = x_prompt.dtype
    bsz, seq, d = x_prompt.shape
    dec_b, dec_t, _ = x_sample.shape
    n_p = bsz * seq
    n_s = dec_b * dec_t
    width = SWA_Q_HEADS * SWA_HEAD_DIM

    w_out0 = hgrn_w_out[0].astype(BF16)
    w_out1 = swa_w_out[0].astype(BF16)
    in_tile = SHORT_COL_TILE
    q_tiles = width // in_tile
    kv_tile = 2 * SWA_KV_WIDTH // in_tile
    assert width % in_tile == 0 and (2 * SWA_KV_WIDTH) % in_tile == 0 and kv_tile == 1
    regroup = lambda j: jnp.where(j < q_tiles, j, jnp.where(j < 2 * q_tiles, j + kv_tile, q_tiles))
    lb = jnp.cumsum(jax.nn.softmax(hgrn_lb_logits.astype(F32), axis=0), axis=0)[0:1]
    nw = hgrn_norm_w[0].astype(F32).reshape(1, -1)
    sinks = swa_sinks[0].astype(F32).reshape(1, SWA_Q_HEADS)
    sink_cols = jnp.repeat(sinks, dec_t, axis=1)
    g0, b0 = ln_g[0:1].astype(F32), ln_b[0:1].astype(F32)
    g1, b1 = ln_g[1:2].astype(F32), ln_b[1:2].astype(F32)

    x_p = x_prompt.astype(F32).reshape(n_p, d)
    meta_block = jnp.concatenate(
        [jnp.zeros((SEQ_BLOCK - N_META, d), F32), meta_tokens.astype(F32)], axis=0)
    x_s = x_sample.astype(F32).reshape(n_s, d)
    meta_blk = n_s // SEQ_BLOCK

    h_sm, w_in0 = _matmul_cast(x_s, meta_block, hgrn_w_in[0].astype(F32), xs_rows=n_s, xm_block=0, tn=in_tile)
    h_p = _matmul(x_p, w_in0, tm=PROJ_ROW_TILE, tn=PROJ_IN0_COL_TILE)
    zero_state = jnp.zeros((1, HGRN_HEADS, HGRN_DK, HGRN_DV), F32)
    og_meta, s_meta = _hgrn_seq(h_sm, lb, nw, zero_state, n_seq=1, seq_len=SEQ_BLOCK, row_block0=meta_blk)
    og_s, st_s = _hgrn_step(h_sm, lb, nw, state_hgrn[0].astype(F32), n_seq=dec_b, t=dec_t, seq_per_block=STEP_STATE_SEQS)
    og_p, st_p = _hgrn_seq(h_p, lb, nw, s_meta, n_seq=bsz, seq_len=seq, row_block0=0)
    x1_sm = _matmul_deepnorm_tail(og_s, og_meta, w_out0, x_s, meta_block, g0, b0)
    x1_p = _matmul_deepnorm(og_p, w_out0, x_p, g0, b0, tm=PROJ_OUT_ROW_TILE)

    h1_sm, w_in1 = _matmul_cast(x1_sm, x1_sm, swa_w_in[0].astype(F32), xs_rows=n_s, xm_block=meta_blk,
                                tn=in_tile, col_tile=regroup)
    h1_p = _matmul(x1_p, w_in1, tm=PROJ_ROW_TILE, tn=PROJ_IN1_COL_TILE)
    og1_p = _attn_seq(h1_p, h1_sm, sinks.reshape(-1), n_seq=bsz, seq_len=seq, meta_row_block=meta_blk)
    ck = cache_swa_k[0].astype(F32).reshape(dec_b, WINDOW, SWA_KV_WIDTH)
    cv = cache_swa_v[0].astype(F32).reshape(dec_b, WINDOW, SWA_KV_WIDTH)
    og1_s, nk_s, nv_s = _attn_step(h1_sm, ck, cv, sink_cols, n_seq=dec_b, t=dec_t, seq_per_block=STEP_CACHE_SEQS)
    y_p = _matmul_deepnorm(og1_p, w_out1, x1_p, g1, b1, tm=PROJ_OUT_ROW_TILE)
    y_s = _matmul_deepnorm(og1_s, w_out1, x1_sm, g1, b1, tm=PROJ_OUT_ROW_TILE)

    kv_p = h1_p.reshape(bsz, seq, -1)[:, seq - WINDOW:, 2 * width:]
    cache_shape = (1, bsz, WINDOW, SWA_KV_HEADS, SWA_HEAD_DIM)
    return (y_p.reshape(bsz, seq, d).astype(out_dtype),
            y_s.reshape(dec_b, dec_t, d).astype(out_dtype),
            st_p[None].astype(out_dtype),
            st_s[None].astype(out_dtype),
            kv_p[..., :SWA_KV_WIDTH].reshape(cache_shape).astype(out_dtype),
            kv_p[..., SWA_KV_WIDTH:].reshape(cache_shape).astype(out_dtype),
            nk_s.reshape((1,) + cache_swa_k.shape[1:]).astype(out_dtype),
            nv_s.reshape((1,) + cache_swa_v.shape[1:]).astype(out_dtype))
```

```python
import functools

import numpy as np
import jax
import jax.numpy as jnp
from jax import lax
from jax.experimental import pallas as pl
from jax.experimental.pallas import tpu as pltpu

F32 = jnp.float32
BF16 = jnp.bfloat16

D_MODEL = 2048
N_META = 16
DEPTH = 2
HGRN_HEADS = 16
HGRN_DK = 128
HGRN_DV = 128
SWA_Q_HEADS = 32
SWA_KV_HEADS = 4
SWA_GROUP = 8
SWA_HEAD_DIM = 64
SWA_KV_WIDTH = SWA_KV_HEADS * SWA_HEAD_DIM
SWA_SCALE = SWA_HEAD_DIM ** -0.5
WINDOW = 128
DEEPNORM_ALPHA = (2.0 * DEPTH) ** 0.25
LN_EPS = 1e-5
RMS_EPS = 1e-6
LOG2E = 1.4426950408889634

CHUNK = 128
LEVELS = 7
MAX_FACTORED_LOG2_DECAY = 80.0
MASK_ROWS = 16
SEQ_BLOCK = 128
LN_SUB_ROWS = 128
STEP_GROUP = 32
SEQ_UNROLL = 8
HGRN_SEQ_HEADS = 16
HGRN_SEQ_ROWS = 256

V7X_VMEM_BYTES = 64 * 1024 * 1024
VMEM_LIMIT = V7X_VMEM_BYTES - 4 * 1024 * 1024
PROJ_ROW_TILE = 1024
PROJ_IN0_COL_TILE = 2048
PROJ_IN1_COL_TILE = 2304
PROJ_OUT_ROW_TILE = 512
SHORT_COL_TILE = 512
SHORT_IN0_COL_TILE = 1024
STEP_STATE_SEQS = 128
STEP_CACHE_SEQS = 32


def _dot(a, b):
    return jnp.dot(a, b, preferred_element_type=F32)


def _dot_nt(a, b):
    return lax.dot_general(a, b, (((1,), (1,)), ((), ())), preferred_element_type=F32)


def _dot_tn(a, b):
    return lax.dot_general(a, b, (((0,), (0,)), ((), ())), preferred_element_type=F32)


def _silu(x):
    return x * jax.nn.sigmoid(x)


def _mm_kernel(x_ref, w_ref, o_ref):
    o_ref[...] = _dot(x_ref[...].astype(BF16), w_ref[...])


def _row_tile(n, want):
    if n <= want:
        return n
    return max(t for t in range(16, want + 1, 16) if n % t == 0)


def _matmul(x, w, *, tm, tn):
    n, k = x.shape
    e = w.shape[1]
    tm = _row_tile(n, tm)
    assert n % tm == 0 and e % tn == 0
    return pl.pallas_call(
        _mm_kernel,
        grid=(n // tm, e // tn),
        in_specs=[pl.BlockSpec((tm, k), lambda i, j: (i, 0)),
                  pl.BlockSpec((k, tn), lambda i, j: (0, j))],
        out_specs=pl.BlockSpec((tm, tn), lambda i, j: (i, j)),
        out_shape=jax.ShapeDtypeStruct((n, e), F32),
        compiler_params=pltpu.CompilerParams(
            dimension_semantics=("parallel", "parallel"), vmem_limit_bytes=VMEM_LIMIT),
        name="proj_in",
    )(x, w)


def _mm_cast_kernel(xs_ref, xm_ref, w_ref, o_ref, wb_ref, xb_scr):
    @pl.when(pl.program_id(0) == 0)
    def _():
        n_s = xs_ref.shape[0]
        xb_scr[0:n_s, :] = xs_ref[...].astype(BF16)
        xb_scr[n_s:, :] = xm_ref[...].astype(BF16)

    wb = w_ref[...].astype(BF16)
    wb_ref[...] = wb
    o_ref[...] = _dot(xb_scr[...], wb)


def _matmul_cast(xs, xm, w, *, xs_rows, xm_block, tn, col_tile=lambda j: j):
    k, e = w.shape
    n = xs_rows + SEQ_BLOCK
    assert e % tn == 0
    return pl.pallas_call(
        _mm_cast_kernel,
        grid=(e // tn,),
        in_specs=[pl.BlockSpec((xs_rows, k), lambda j: (0, 0), pipeline_mode=pl.Buffered(1)),
                  pl.BlockSpec((SEQ_BLOCK, k), lambda j: (xm_block, 0), pipeline_mode=pl.Buffered(1)),
                  pl.BlockSpec((k, tn), lambda j: (0, col_tile(j)))],
        out_specs=[pl.BlockSpec((n, tn), lambda j: (0, j)),
                   pl.BlockSpec((k, tn), lambda j: (0, j))],
        out_shape=[jax.ShapeDtypeStruct((n, e), F32), jax.ShapeDtypeStruct((k, e), BF16)],
        scratch_shapes=[pltpu.VMEM((n, k), BF16)],
        compiler_params=pltpu.CompilerParams(
            dimension_semantics=("arbitrary",), vmem_limit_bytes=VMEM_LIMIT),
        name="proj_in_cast",
    )(xs, xm, w)


def _mm_ln_tail_kernel(a_ref, at_ref, w_ref, x_ref, xt_ref, g_ref, b_ref, o_ref):
    tail = pl.program_id(0) == pl.num_programs(0) - 1
    a = jnp.where(tail, at_ref[...], a_ref[...])
    x = jnp.where(tail, xt_ref[...], x_ref[...])
    z = DEEPNORM_ALPHA * x + _dot(a, w_ref[...])
    mu = jnp.mean(z, axis=-1, keepdims=True)
    zc = z - mu
    var = jnp.mean(zc * zc, axis=-1, keepdims=True)
    o_ref[...] = zc * lax.rsqrt(var + LN_EPS) * g_ref[...] + b_ref[...]


def _matmul_deepnorm_tail(a, a_tail, w, x, x_tail, g, b):
    n, k = a.shape
    d = w.shape[1]
    tm = SEQ_BLOCK
    last = n // tm - 1
    body = lambda i: (jnp.minimum(i, last), 0)
    const = lambda i: (0, 0)
    return pl.pallas_call(
        _mm_ln_tail_kernel,
        grid=(n // tm + 1,),
        in_specs=[pl.BlockSpec((tm, k), body), pl.BlockSpec((tm, k), const),
                  pl.BlockSpec((k, d), const),
                  pl.BlockSpec((tm, d), body), pl.BlockSpec((tm, d), const),
                  pl.BlockSpec((1, d), const), pl.BlockSpec((1, d), const)],
        out_specs=pl.BlockSpec((tm, d), lambda i: (i, 0)),
        out_shape=jax.ShapeDtypeStruct((n + tm, d), F32),
        compiler_params=pltpu.CompilerParams(
            dimension_semantics=("parallel",), vmem_limit_bytes=VMEM_LIMIT),
        name="proj_out_deepnorm_tail",
    )(a, a_tail, w, x, x_tail, g, b)


def _mm_ln_kernel(a_ref, w_ref, x_ref, g_ref, b_ref, o_ref):
    tm = a_ref.shape[0]
    sub = LN_SUB_ROWS if tm % LN_SUB_ROWS == 0 else tm
    for r0 in range(0, tm, sub):
        rows = slice(r0, r0 + sub)
        z = DEEPNORM_ALPHA * x_ref[rows, :] + _dot(a_ref[rows, :], w_ref[...])
        mu = jnp.mean(z, axis=-1, keepdims=True)
        zc = z - mu
        var = jnp.mean(zc * zc, axis=-1, keepdims=True)
        o_ref[rows, :] = zc * lax.rsqrt(var + LN_EPS) * g_ref[...] + b_ref[...]


def _matmul_deepnorm(a, w, x, g, b, *, tm):
    n, k = a.shape
    d = w.shape[1]
    tm = _row_tile(n, tm)
    return pl.pallas_call(
        _mm_ln_kernel,
        grid=(n // tm,),
        in_specs=[pl.BlockSpec((tm, k), lambda i: (i, 0)),
                  pl.BlockSpec((k, d), lambda i: (0, 0)),
                  pl.BlockSpec((tm, d), lambda i: (i, 0)),
                  pl.BlockSpec((1, d), lambda i: (0, 0)),
                  pl.BlockSpec((1, d), lambda i: (0, 0))],
        out_specs=pl.BlockSpec((tm, d), lambda i: (i, 0)),
        out_shape=jax.ShapeDtypeStruct((n, d), F32),
        compiler_params=pltpu.CompilerParams(
            dimension_semantics=("parallel",), vmem_limit_bytes=VMEM_LIMIT),
        name="proj_out_deepnorm",
    )(a, w, x, g, b)


def _level_masks():
    t = np.arange(CHUNK)
    out = np.zeros((LEVELS + 1, CHUNK, CHUNK), np.float32)
    out[0] = t[:, None] > t[None, :]
    for l in range(LEVELS):
        h = 1 << l
        same = (t[:, None] >> (l + 1)) == (t[None, :] >> (l + 1))
        out[l + 1] = same & ((t[:, None] & h) != 0) & ((t[None, :] & h) == 0)
    return out


def _gates(fx, lb):
    f = lb + (1.0 - lb) * jax.nn.sigmoid(fx)
    return f, jnp.log2(f), 1.0 - f


def _prefix_scan(logf):
    tiles = (CHUNK // 8, 8, HGRN_DK)
    c = logf.reshape(tiles)
    sub = lax.broadcasted_iota(jnp.int32, (1, 8, HGRN_DK), 1)
    bcast = lambda x, r: jnp.broadcast_to(x[:, r:r + 1, :], tiles)
    c = c + jnp.where((sub & 1) != 0, pltpu.roll(c, 1, 1), 0.0)
    c = c + jnp.where((sub & 2) != 0, jnp.where(sub < 4, bcast(c, 1), bcast(c, 5)), 0.0)
    c = c + jnp.where((sub & 4) != 0, bcast(c, 3), 0.0)
    c = c.reshape(CHUNK, HGRN_DK)
    for level in range(3, LEVELS):
        half = 1 << level
        pieces = []
        for r0 in range(0, CHUNK, 2 * half):
            pieces += [c[r0:r0 + half], c[r0 + half:r0 + 2 * half] + c[r0 + half - 1:r0 + half, :]]
        c = jnp.concatenate(pieces, axis=0)
    return c


def _small_levels(logf, f, q, kk):
    tiles = (CHUNK // 8, 8, HGRN_DK)
    c, f3, q3, k3 = (x.reshape(tiles) for x in (logf, f, q, kk))
    sub = lax.broadcasted_iota(jnp.int32, (1, 8, HGRN_DK), 1)
    bcast = lambda x, r: jnp.broadcast_to(x[:, r:r + 1, :], tiles)
    up = (sub & 1) != 0
    zs = [jnp.where(up, q3 * f3, k3)]
    c = c + jnp.where(up, pltpu.roll(c, 1, 1), 0.0)
    for level, tot in ((1, lambda c: jnp.where(sub < 4, bcast(c, 1), bcast(c, 5))),
                       (2, lambda c: bcast(c, 3))):
        up = (sub & (1 << level)) != 0
        t = tot(c)
        zs.append(jnp.where(up, q3, k3) * jnp.exp2(jnp.where(up, c, t - c)))
        c = c + jnp.where(up, t, 0.0)
    return [z.reshape(CHUNK, HGRN_DK) for z in zs], c.reshape(CHUNK, HGRN_DK)


def _level_large(cum, q, kk, level):
    half = 1 << level
    args, bases, cums = [], [], []
    for r0 in range(0, CHUNK, 2 * half):
        lo, up = slice(r0, r0 + half), slice(r0 + half, r0 + 2 * half)
        tot = cum[r0 + half - 1:r0 + half, :]
        args += [tot - cum[lo], cum[up]]
        bases += [kk[lo], q[up]]
        cums += [cum[lo], cum[up] + tot]
    x = jnp.exp2(jnp.concatenate(args, axis=0))
    return jnp.concatenate(bases, axis=0) * x, jnp.concatenate(cums, axis=0)


def _split3(x):
    hi = x.astype(BF16)
    r = x - hi.astype(F32)
    mid = r.astype(BF16)
    lo = (r - mid.astype(F32)).astype(BF16)
    return hi, mid, lo


def _state_update(s_old, kd, v, blast, pad_rows):
    c = kd.shape[0]
    hi, mid, lo = _split3(jnp.exp2(blast))
    row = lax.broadcasted_iota(jnp.int32, (pad_rows, HGRN_DK), 0)
    dec = jnp.where(row == 0, hi.astype(F32),
                    jnp.where(row == 1, mid.astype(F32), jnp.where(row == 2, lo.astype(F32), 0.0)))
    piece = BF16 if c % 16 == 0 and pad_rows % 16 == 0 else F32
    lhs = jnp.concatenate([kd.astype(piece), dec.astype(piece)], axis=0).astype(BF16)
    rhs = jnp.concatenate(
        [jnp.concatenate([v.astype(piece), jnp.zeros((c, HGRN_DV), piece)], axis=1),
         jnp.concatenate([jnp.zeros((pad_rows, HGRN_DV), piece), jnp.ones((pad_rows, HGRN_DV), piece)], axis=1)],
        axis=0).astype(BF16)
    both = _dot_tn(lhs, rhs)
    return both[:, HGRN_DV:] * s_old + both[:, :HGRN_DV]


def _norm_gate(o, g, nw):
    o = o * lax.rsqrt(jnp.mean(o * o, axis=-1, keepdims=True) + RMS_EPS) * nw
    return o * _silu(g)


def _hgrn_seq_kernel(q_ref, fx_ref, i_ref, g_ref, lb_ref, nw_ref, s0_ref, m_ref,
                     og_ref, sout_ref, s_scr, o_scr, *, n_chunks):
    heads = s_scr.shape[0]

    @pl.when(pl.program_id(2) == 0)
    def _():
        s_scr[...] = s0_ref[0]

    def chunk(c, carry):
        rows = pl.ds(pl.multiple_of(c * CHUNK, CHUNK), CHUNK)
        hcols = [slice(hd * HGRN_DK, (hd + 1) * HGRN_DK) for hd in range(heads)]
        q = [q_ref[rows, cs] for cs in hcols]
        v = [i_ref[rows, cs] for cs in hcols]
        gates = [_gates(fx_ref[rows, cs], lb_ref[:, cs]) for cs in hcols]
        kk = [gt[2] for gt in gates]
        cum = [_prefix_scan(gt[1]) for gt in gates]
        mid = CHUNK // 2 - 1
        spread = None
        for b in cum:
            s_hd = jnp.maximum(b[0:1, :] - b[mid:mid + 1, :], b[mid:mid + 1, :] - b[CHUNK - 1:CHUNK, :])
            spread = s_hd if spread is None else jnp.maximum(spread, s_hd)
        factorable = jnp.max(spread) <= MAX_FACTORED_LOG2_DECAY

        vb = [x.astype(BF16) for x in v]
        for hd in range(heads):
            d = cum[hd] - cum[hd][mid:mid + 1, :]
            qe = (q[hd] * jnp.exp2(d)).astype(BF16)
            ke = (kk[hd] * jnp.exp2(-d)).astype(BF16)
            amat = jnp.where(m_ref[0] > 0.5, _dot_nt(qe, ke), 0.0).astype(BF16)
            blast = cum[hd][CHUNK - 1:CHUNK, :]
            qd = (q[hd] * jnp.exp2(cum[hd])).astype(BF16)
            kd = kk[hd] * jnp.exp2(blast - cum[hd])
            s_old = s_scr[hd]
            base = _dot(qd, s_old.astype(BF16)) + jnp.sum(q[hd] * kk[hd], axis=-1, keepdims=True) * v[hd]
            o_scr[hd] = base
            s_scr[hd] = _state_update(s_old, kd, vb[hd], blast, 16)
            og_ref[rows, hcols[hd]] = _norm_gate(
                base + _dot(amat, vb[hd]), g_ref[rows, hcols[hd]], nw_ref[:, hcols[hd]]).astype(og_ref.dtype)

        @pl.when(jnp.logical_not(factorable))
        def _():
            small = [_small_levels(gt[1], gt[0], q[hd], kk[hd]) for hd, gt in enumerate(gates)]
            zs = [sm[0] for sm in small]
            part = [sm[1] for sm in small]
            for level in range(3, LEVELS):
                for hd in range(heads):
                    z, part[hd] = _level_large(part[hd], q[hd], kk[hd], level)
                    zs[hd].append(z)
            nblk = CHUNK // MASK_ROWS
            a = [[None] * nblk for _ in range(heads)]
            for level in range(LEVELS):
                step = 1 << (level - 4) if level >= 4 else 0
                blocks = [i for i in range(nblk) if level < 4 or (i & step)]
                for hd in range(heads):
                    z = zs[hd][level].astype(BF16)
                    lhs = z if level < 4 else jnp.concatenate(
                        [z[i * MASK_ROWS:(i + 1) * MASK_ROWS] for i in blocks], axis=0)
                    term = _dot_nt(lhs, z)
                    for n, i in enumerate(blocks):
                        t = (term[n * MASK_ROWS:(n + 1) * MASK_ROWS]
                             * m_ref[level + 1, i * MASK_ROWS:(i + 1) * MASK_ROWS, :])
                        a[hd][i] = t if a[hd][i] is None else a[hd][i] + t
            for hd in range(heads):
                amat = jnp.concatenate(a[hd], axis=0).astype(BF16)
                og_ref[rows, hcols[hd]] = _norm_gate(
                    o_scr[hd] + _dot(amat, vb[hd]),
                    g_ref[rows, hcols[hd]], nw_ref[:, hcols[hd]]).astype(og_ref.dtype)
        return carry

    lax.fori_loop(0, n_chunks, chunk, 0)

    @pl.when(pl.program_id(2) == pl.num_programs(2) - 1)
    def _():
        sout_ref[0] = s_scr[...]


def _hgrn_seq(h, lb, nw, s0, *, n_seq, seq_len, row_block0):
    hb = HGRN_SEQ_HEADS
    width = hb * HGRN_DK
    groups = HGRN_HEADS // hb
    rb = min(HGRN_SEQ_ROWS, seq_len)
    nrb = seq_len // rb
    assert seq_len % rb == 0 and rb % CHUNK == 0 and (row_block0 * SEQ_BLOCK) % rb == 0
    rb0 = row_block0 * SEQ_BLOCK // rb
    s0_batched = s0.shape[0] != 1
    col = lambda part: (lambda b, hg, r: (rb0 + b * nrb + r, part * groups + hg))
    masks = jnp.asarray(_level_masks())
    return pl.pallas_call(
        functools.partial(_hgrn_seq_kernel, n_chunks=rb // CHUNK),
        grid=(n_seq, groups, nrb),
        in_specs=[pl.BlockSpec((rb, width), col(0)),
                  pl.BlockSpec((rb, width), col(1)),
                  pl.BlockSpec((rb, width), col(2)),
                  pl.BlockSpec((rb, width), col(3)),
                  pl.BlockSpec((1, width), lambda b, hg, r: (0, hg)),
                  pl.BlockSpec((1, width), lambda b, hg, r: (0, hg)),
                  pl.BlockSpec((1, hb, HGRN_DK, HGRN_DV),
                               (lambda b, hg, r: (b, hg, 0, 0)) if s0_batched else (lambda b, hg, r: (0, hg, 0, 0))),
                  pl.BlockSpec((LEVELS + 1, CHUNK, CHUNK), lambda b, hg, r: (0, 0, 0))],
        out_specs=[pl.BlockSpec((rb, width), lambda b, hg, r: (b * nrb + r, hg)),
                   pl.BlockSpec((1, hb, HGRN_DK, HGRN_DV), lambda b, hg, r: (b, hg, 0, 0))],
        out_shape=[jax.ShapeDtypeStruct((n_seq * seq_len, HGRN_HEADS * HGRN_DV), BF16),
                   jax.ShapeDtypeStruct((n_seq, HGRN_HEADS, HGRN_DK, HGRN_DV), F32)],
        scratch_shapes=[pltpu.VMEM((hb, HGRN_DK, HGRN_DV), F32),
                        pltpu.VMEM((hb, CHUNK, HGRN_DV), F32)],
        compiler_params=pltpu.CompilerParams(
            dimension_semantics=("parallel", "parallel", "arbitrary"), vmem_limit_bytes=VMEM_LIMIT),
        name="hgrn_seq",
    )(h, h, h, h, lb, nw, s0, masks)


def _hgrn_step_kernel(q_ref, fx_ref, i_ref, g_ref, lb_ref, nw_ref, s_ref, og_ref, sout_ref, *, n_seq, t):
    lb = lb_ref[...]
    nw = nw_ref[...]
    tile = (STEP_GROUP, t, HGRN_DK)
    sub = lax.broadcasted_iota(jnp.int32, (1, t, HGRN_DK), 1)

    def group(i, carry):
        rows = pl.ds(pl.multiple_of(i * (STEP_GROUP * t), STEP_GROUP * t), STEP_GROUP * t)
        q = q_ref[rows, :].reshape(tile)
        v = i_ref[rows, :].reshape(tile)
        _, logf, kk = _gates(fx_ref[rows, :].reshape(tile), lb)
        cum = logf
        shift = 1
        while shift < t:
            cum = cum + jnp.where(sub >= shift, pltpu.roll(cum, shift, 1), 0.0)
            shift *= 2
        intra = (jnp.sum(q * kk, axis=-1, keepdims=True)) * v
        for d in range(1, t):
            valid = sub >= d
            x = jnp.exp2(jnp.where(valid, cum - pltpu.roll(cum, d, 1), 0.0))
            w = jnp.sum(jnp.where(valid, q * x * pltpu.roll(kk, d, 1), 0.0), axis=-1, keepdims=True)
            intra = intra + w * pltpu.roll(v, d, 1)
        blast = cum[:, t - 1:t, :]
        qd = q * jnp.exp2(cum)
        kd = kk * jnp.exp2(blast - cum)
        seqs = [i * STEP_GROUP + n for n in range(STEP_GROUP)]
        inter = [_dot(qd[n].astype(BF16), s_ref[b, 0].astype(BF16)) for n, b in enumerate(seqs)]
        for n, b in enumerate(seqs):
            sout_ref[b, 0] = _state_update(s_ref[b, 0], kd[n], v[n], blast[n], 8)
        o = jnp.stack(inter, axis=0) + intra
        og = _norm_gate(o, g_ref[rows, :].reshape(tile), nw)
        og_ref[rows, :] = og.reshape(STEP_GROUP * t, HGRN_DK).astype(og_ref.dtype)
        return carry

    lax.fori_loop(0, n_seq // STEP_GROUP, group, 0)


def _hgrn_step(h, lb, nw, s0, *, n_seq, t, seq_per_block):
    seq_per_block = min(seq_per_block, n_seq)
    assert n_seq % seq_per_block == 0 and seq_per_block % STEP_GROUP == 0 and t == 8
    rows = seq_per_block * t
    col = lambda part: (lambda i, hd: (i, part * HGRN_HEADS + hd))
    return pl.pallas_call(
        functools.partial(_hgrn_step_kernel, n_seq=seq_per_block, t=t),
        grid=(n_seq // seq_per_block, HGRN_HEADS),
        in_specs=[pl.BlockSpec((rows, HGRN_DK), col(0)),
                  pl.BlockSpec((rows, HGRN_DK), col(1)),
                  pl.BlockSpec((rows, HGRN_DV), col(2)),
                  pl.BlockSpec((rows, HGRN_DV), col(3)),
                  pl.BlockSpec((1, HGRN_DK), lambda i, hd: (0, hd)),
                  pl.BlockSpec((1, HGRN_DV), lambda i, hd: (0, hd)),
                  pl.BlockSpec((seq_per_block, 1, HGRN_DK, HGRN_DV), lambda i, hd: (i, hd, 0, 0))],
        out_specs=[pl.BlockSpec((rows, HGRN_DV), lambda i, hd: (i, hd)),
                   pl.BlockSpec((seq_per_block, 1, HGRN_DK, HGRN_DV), lambda i, hd: (i, hd, 0, 0))],
        out_shape=[jax.ShapeDtypeStruct((n_seq * t, HGRN_HEADS * HGRN_DV), BF16),
                   jax.ShapeDtypeStruct((n_seq, HGRN_HEADS, HGRN_DK, HGRN_DV), F32)],
        compiler_params=pltpu.CompilerParams(
            dimension_semantics=("parallel", "parallel"), vmem_limit_bytes=VMEM_LIMIT),
        name="hgrn_step",
    )(h, h, h, h, lb, nw, s0)


def _attn_seq_kernel(q_ref, g_ref, kc_ref, vc_ref, kp_ref, vp_ref, meta_ref, sink_ref, og_ref):
    first = pl.program_id(1) == 0
    kprev = jnp.where(first, meta_ref[:, :SWA_KV_WIDTH], kp_ref[...])
    vprev = jnp.where(first, meta_ref[:, SWA_KV_WIDTH:], vp_ref[...])
    kband = (jnp.concatenate([kprev, kc_ref[...]], axis=0) * (SWA_SCALE * LOG2E)).astype(BF16)
    vband_t = jnp.concatenate([vprev, vc_ref[...]], axis=0).T.astype(BF16)
    zeros_t = jnp.zeros((SWA_HEAD_DIM, 2 * SEQ_BLOCK), BF16)
    kj = lax.broadcasted_iota(jnp.int32, (2 * SEQ_BLOCK, SEQ_BLOCK), 0)
    qi = lax.broadcasted_iota(jnp.int32, (2 * SEQ_BLOCK, SEQ_BLOCK), 1)
    dist = SEQ_BLOCK + qi - kj
    valid = (dist >= 0) & (dist < WINDOW) & (jnp.logical_not(first) | (kj >= SEQ_BLOCK - N_META))
    madd = jnp.where(valid, 0.0, -jnp.inf)
    zeros = jnp.zeros((2 * SEQ_BLOCK, SWA_HEAD_DIM), BF16)
    pairs = SWA_GROUP // 2
    pw = 2 * SWA_HEAD_DIM

    def block_diag(x):
        return jnp.concatenate([jnp.concatenate([x, zeros], axis=1),
                                jnp.concatenate([zeros, x], axis=1)], axis=0)

    sts = []
    for kvh in range(SWA_KV_HEADS):
        ks = slice(kvh * SWA_HEAD_DIM, (kvh + 1) * SWA_HEAD_DIM)
        col0 = kvh * SWA_GROUP * SWA_HEAD_DIM
        xq = jnp.concatenate([q_ref[:, col0 + p * pw:col0 + (p + 1) * pw].astype(BF16)
                              for p in range(pairs)], axis=0)
        sts.append(_dot_nt(block_diag(kband[:, ks]), xq))
    for kvh in range(SWA_KV_HEADS):
        ks = slice(kvh * SWA_HEAD_DIM, (kvh + 1) * SWA_HEAD_DIM)
        col0 = kvh * SWA_GROUP * SWA_HEAD_DIM
        vt = vband_t[ks, :]
        v2t = jnp.concatenate([jnp.concatenate([vt, zeros_t], axis=1),
                               jnp.concatenate([zeros_t, vt], axis=1)], axis=0)
        pbs, rdens = [], []
        for pr in range(pairs):
            head = kvh * SWA_GROUP + 2 * pr
            s2 = (sts[kvh][:, pr * SEQ_BLOCK:(pr + 1) * SEQ_BLOCK].reshape(2, 2 * SEQ_BLOCK, SEQ_BLOCK)
                  + madd[None])
            sink = jnp.concatenate([jnp.full((1, 1, SEQ_BLOCK), sink_ref[head] * LOG2E, F32),
                                    jnp.full((1, 1, SEQ_BLOCK), sink_ref[head + 1] * LOG2E, F32)], axis=0)
            m = jnp.maximum(jnp.max(s2, axis=1, keepdims=True), sink)
            p = jnp.exp2(s2 - m)
            den = jnp.sum(p, axis=1, keepdims=True) + jnp.exp2(sink - m)
            pbs.append(p.astype(BF16).reshape(4 * SEQ_BLOCK, SEQ_BLOCK))
            rdens.append(1.0 / den)
        for pr in range(pairs):
            ot = _dot(v2t, pbs[pr]).reshape(2, SWA_HEAD_DIM, SEQ_BLOCK) * rdens[pr]
            o = ot.reshape(2 * SWA_HEAD_DIM, SEQ_BLOCK).T
            cs = slice(col0 + pr * pw, col0 + (pr + 1) * pw)
            og_ref[:, cs] = (o * _silu(g_ref[:, cs])).astype(og_ref.dtype)


def _attn_seq(h1, h1_meta, sinks, *, n_seq, seq_len, meta_row_block):
    nb = seq_len // SEQ_BLOCK
    width = SWA_Q_HEADS * SWA_HEAD_DIM
    kcol = 2 * width // SWA_KV_WIDTH
    return pl.pallas_call(
        _attn_seq_kernel,
        grid=(n_seq, nb),
        in_specs=[pl.BlockSpec((SEQ_BLOCK, width), lambda b, j: (b * nb + j, 0)),
                  pl.BlockSpec((SEQ_BLOCK, width), lambda b, j: (b * nb + j, 1)),
                  pl.BlockSpec((SEQ_BLOCK, SWA_KV_WIDTH), lambda b, j: (b * nb + j, kcol)),
                  pl.BlockSpec((SEQ_BLOCK, SWA_KV_WIDTH), lambda b, j: (b * nb + j, kcol + 1)),
                  pl.BlockSpec((SEQ_BLOCK, SWA_KV_WIDTH), lambda b, j: (b * nb + jnp.maximum(j - 1, 0), kcol)),
                  pl.BlockSpec((SEQ_BLOCK, SWA_KV_WIDTH), lambda b, j: (b * nb + jnp.maximum(j - 1, 0), kcol + 1)),
                  pl.BlockSpec((SEQ_BLOCK, 2 * SWA_KV_WIDTH), lambda b, j: (meta_row_block, kcol // 2)),
                  pl.BlockSpec(memory_space=pltpu.SMEM)],
        out_specs=pl.BlockSpec((SEQ_BLOCK, width), lambda b, j: (b * nb + j, 0)),
        out_shape=jax.ShapeDtypeStruct((n_seq * seq_len, width), BF16),
        compiler_params=pltpu.CompilerParams(
            dimension_semantics=("parallel", "arbitrary"), vmem_limit_bytes=VMEM_LIMIT),
        name="attn_seq",
    )(h1, h1, h1, h1, h1, h1, h1_meta, sinks)


def _attn_step_kernel(q_ref, g_ref, kn_ref, vn_ref, ck_ref, cv_ref, sink_ref, og_ref, nk_ref, nv_ref, *, n_seq, t):
    keys = WINDOW + t
    hd = SWA_HEAD_DIM
    tiles = SWA_Q_HEADS // 2
    kj = lax.broadcasted_iota(jnp.int32, (keys, SWA_Q_HEADS * t), 0)
    qt = lax.broadcasted_iota(jnp.int32, (keys, SWA_Q_HEADS * t), 1) % t
    madd = jnp.where((kj >= qt + 1) & (kj <= WINDOW + qt), 0.0, -jnp.inf)
    low = lax.broadcasted_iota(jnp.int32, (t, 2 * hd), 1) < hd
    zero_tile = jnp.zeros((t, 2 * hd), F32)
    sink = sink_ref[...] * LOG2E

    def group(i, carry):
        seqs = [i * SEQ_UNROLL + n for n in range(SEQ_UNROLL)]
        rows = [pl.ds(pl.multiple_of(b * t, t), t) for b in seqs]
        st, vall = [], []
        for b, rw in zip(seqs, rows):
            kc, vc = ck_ref[b], cv_ref[b]
            kn, vn = kn_ref[rw, :], vn_ref[rw, :]
            nk_ref[b, 0:WINDOW - t, :] = kc[t:, :]
            nk_ref[b, WINDOW - t:WINDOW, :] = kn
            nv_ref[b, 0:WINDOW - t, :] = vc[t:, :]
            nv_ref[b, WINDOW - t:WINDOW, :] = vn
            kall = jnp.concatenate([kc, kn], axis=0).astype(BF16)
            vall.append(jnp.concatenate([vc, vn], axis=0).astype(BF16))
            q = q_ref[rw, :] * (SWA_SCALE * LOG2E)
            qtile = [q[:, j * 2 * hd:(j + 1) * 2 * hd] for j in range(tiles)]
            qswap = [pltpu.roll(x, hd, 1) for x in qtile]
            groups = []
            for kvh in range(SWA_KV_HEADS):
                for gq in range(SWA_GROUP):
                    j = (kvh * SWA_GROUP + gq) // 2
                    src = qtile[j] if gq % 2 == kvh % 2 else qswap[j]
                    half = jnp.where(low, src, 0.0) if kvh % 2 == 0 else jnp.where(low, 0.0, src)
                    groups.append(jnp.concatenate(
                        [half if c == kvh // 2 else zero_tile for c in range(SWA_KV_HEADS // 2)], axis=1))
            qbd = jnp.concatenate(groups, axis=0).astype(BF16)
            st.append(_dot_nt(kall, qbd))
        pn = []
        for s in st:
            s = s + madd
            m = jnp.maximum(jnp.max(s, axis=0, keepdims=True), sink)
            p = jnp.exp2(s - m)
            den = jnp.sum(p, axis=0, keepdims=True) + jnp.exp2(sink - m)
            pn.append((p * (1.0 / den)).astype(BF16))
        of = [_dot_tn(p, vl) for p, vl in zip(pn, vall)]
        for o, rw in zip(of, rows):
            g = g_ref[rw, :]
            out = []
            for j in range(tiles):
                kvh = (2 * j) // SWA_GROUP
                ct = slice((kvh // 2) * 2 * hd, (kvh // 2 + 1) * 2 * hd)
                ra = o[(2 * j) * t:(2 * j + 1) * t, ct]
                rb = o[(2 * j + 1) * t:(2 * j + 2) * t, ct]
                if kvh % 2 == 0:
                    out.append(jnp.where(low, ra, pltpu.roll(rb, hd, 1)))
                else:
                    out.append(jnp.where(low, pltpu.roll(ra, hd, 1), rb))
            og_ref[rw, :] = (jnp.concatenate(out, axis=1) * _silu(g)).astype(og_ref.dtype)
        return carry

    lax.fori_loop(0, n_seq // SEQ_UNROLL, group, 0)


def _attn_step(h1, cache_k, cache_v, sink_cols, *, n_seq, t, seq_per_block):
    assert n_seq % seq_per_block == 0
    rows = seq_per_block * t
    width = SWA_Q_HEADS * SWA_HEAD_DIM
    kcol = 2 * width // SWA_KV_WIDTH
    cache_spec = pl.BlockSpec((seq_per_block, WINDOW, SWA_KV_WIDTH), lambda i: (i, 0, 0))
    return pl.pallas_call(
        functools.partial(_attn_step_kernel, n_seq=seq_per_block, t=t),
        grid=(n_seq // seq_per_block,),
        in_specs=[pl.BlockSpec((rows, width), lambda i: (i, 0)),
                  pl.BlockSpec((rows, width), lambda i: (i, 1)),
                  pl.BlockSpec((rows, SWA_KV_WIDTH), lambda i: (i, kcol)),
                  pl.BlockSpec((rows, SWA_KV_WIDTH), lambda i: (i, kcol + 1)),
                  cache_spec, cache_spec,
                  pl.BlockSpec((1, SWA_Q_HEADS * t), lambda i: (0, 0))],
        out_specs=[pl.BlockSpec((rows, width), lambda i: (i, 0)), cache_spec, cache_spec],
        out_shape=[jax.ShapeDtypeStruct((n_seq * t, width), BF16),
                   jax.ShapeDtypeStruct(cache_k.shape, F32),
                   jax.ShapeDtypeStruct(cache_v.shape, F32)],
        compiler_params=pltpu.CompilerParams(
            dimension_semantics=("parallel",), vmem_limit_bytes=VMEM_LIMIT),
        name="attn_step",
    )(h1, h1, h1, h1, cache_k, cache_v, sink_cols)


def kernel(x_prompt, x_sample, state_hgrn, cache_swa_k, cache_swa_v, meta_tokens,
           hgrn_w_in, hgrn_lb_logits, hgrn_norm_w, hgrn_w_out,
           swa_w_in, swa_sinks, swa_w_out, ln_g, ln_b):
    out_dtype = x_prompt.dtype
    bsz, seq, d = x_prompt.shape
    dec_b, dec_t, _ = x_sample.shape
    n_p = bsz * seq
    n_s = dec_b * dec_t
    width = SWA_Q_HEADS * SWA_HEAD_DIM

    w_out0 = hgrn_w_out[0].astype(BF16)
    w_out1 = swa_w_out[0].astype(BF16)
    in_tile = SHORT_COL_TILE
    q_tiles = width // in_tile
    kv_tile = 2 * SWA_KV_WIDTH // in_tile
    assert width % in_tile == 0 and (2 * SWA_KV_WIDTH) % in_tile == 0 and kv_tile == 1
    regroup = lambda j: jnp.where(j < q_tiles, j, jnp.where(j < 2 * q_tiles, j + kv_tile, q_tiles))
    lb = jnp.cumsum(jax.nn.softmax(hgrn_lb_logits.astype(F32), axis=0), axis=0)[0:1]
    nw = hgrn_norm_w[0].astype(F32).reshape(1, -1)
    sinks = swa_sinks[0].astype(F32).reshape(1, SWA_Q_HEADS)
    sink_cols = jnp.repeat(sinks, dec_t, axis=1)
    g0, b0 = ln_g[0:1].astype(F32), ln_b[0:1].astype(F32)
    g1, b1 = ln_g[1:2].astype(F32), ln_b[1:2].astype(F32)

    x_p = x_prompt.astype(F32).reshape(n_p, d)
    meta_block = jnp.concatenate(
        [jnp.zeros((SEQ_BLOCK - N_META, d), F32), meta_tokens.astype(F32)], axis=0)
    x_s = x_sample.astype(F32).reshape(n_s, d)
    meta_blk = n_s // SEQ_BLOCK

    h_sm, w_in0 = _matmul_cast(x_s, meta_block, hgrn_w_in[0].astype(F32), xs_rows=n_s, xm_block=0, tn=SHORT_IN0_COL_TILE)
    h_p = _matmul(x_p, w_in0, tm=PROJ_ROW_TILE, tn=PROJ_IN0_COL_TILE)
    zero_state = jnp.zeros((1, HGRN_HEADS, HGRN_DK, HGRN_DV), F32)
    og_meta, s_meta = _hgrn_seq(h_sm, lb, nw, zero_state, n_seq=1, seq_len=SEQ_BLOCK, row_block0=meta_blk)
    og_s, st_s = _hgrn_step(h_sm, lb, nw, state_hgrn[0].astype(F32), n_seq=dec_b, t=dec_t, seq_per_block=STEP_STATE_SEQS)
    og_p, st_p = _hgrn_seq(h_p, lb, nw, s_meta, n_seq=bsz, seq_len=seq, row_block0=0)
    x1_sm = _matmul_deepnorm_tail(og_s, og_meta, w_out0, x_s, meta_block, g0, b0)
    x1_p = _matmul_deepnorm(og_p, w_out0, x_p, g0, b0, tm=PROJ_OUT_ROW_TILE)

    h1_sm, w_in1 = _matmul_cast(x1_sm, x1_sm, swa_w_in[0].astype(F32), xs_rows=n_s, xm_block=meta_blk,
                                tn=in_tile, col_tile=regroup)
    h1_p = _matmul(x1_p, w_in1, tm=PROJ_ROW_TILE, tn=PROJ_IN1_COL_TILE)
    og1_p = _attn_seq(h1_p, h1_sm, sinks.reshape(-1), n_seq=bsz, seq_len=seq, meta_row_block=meta_blk)
    ck = cache_swa_k[0].astype(F32).reshape(dec_b, WINDOW, SWA_KV_WIDTH)
    cv = cache_swa_v[0].astype(F32).reshape(dec_b, WINDOW, SWA_KV_WIDTH)
    og1_s, nk_s, nv_s = _attn_step(h1_sm, ck, cv, sink_cols, n_seq=dec_b, t=dec_t, seq_per_block=STEP_CACHE_SEQS)
    y_p = _matmul_deepnorm(og1_p, w_out1, x1_p, g1, b1, tm=PROJ_OUT_ROW_TILE)
    y_s = _matmul_deepnorm(og1_s, w_out1, x1_sm, g1, b1, tm=PROJ_OUT_ROW_TILE)

    kv_p = h1_p.reshape(bsz, seq, -1)[:, seq - WINDOW:, 2 * width:]
    cache_shape = (1, bsz, WINDOW, SWA_KV_HEADS, SWA_HEAD_DIM)
    return (y_p.reshape(bsz, seq, d).astype(out_dtype),
            y_s.reshape(dec_b, dec_t, d).astype(out_dtype),
            st_p[None].astype(out_dtype),
            st_s[None].astype(out_dtype),
            kv_p[..., :SWA_KV_WIDTH].reshape(cache_shape).astype(out_dtype),
            kv_p[..., SWA_KV_WIDTH:].reshape(cache_shape).astype(out_dtype),
            nk_s.reshape((1,) + cache_swa_k.shape[1:]).astype(out_dtype),
            nv_s.reshape((1,) + cache_swa_v.shape[1:]).astype(out_dtype))
```

```python
import functools

import numpy as np
import jax
import jax.numpy as jnp
from jax import lax
from jax.experimental import pallas as pl
from jax.experimental.pallas import tpu as pltpu

F32 = jnp.float32
BF16 = jnp.bfloat16

D_MODEL = 2048
N_META = 16
DEPTH = 2
HGRN_HEADS = 16
HGRN_DK = 128
HGRN_DV = 128
SWA_Q_HEADS = 32
SWA_KV_HEADS = 4
SWA_GROUP = 8
SWA_HEAD_DIM = 64
SWA_KV_WIDTH = SWA_KV_HEADS * SWA_HEAD_DIM
SWA_SCALE = SWA_HEAD_DIM ** -0.5
WINDOW = 128
DEEPNORM_ALPHA = (2.0 * DEPTH) ** 0.25
LN_EPS = 1e-5
RMS_EPS = 1e-6
LOG2E = 1.4426950408889634

CHUNK = 128
LEVELS = 7
MAX_FACTORED_LOG2_DECAY = 80.0
MASK_ROWS = 16
SEQ_BLOCK = 128
LN_SUB_ROWS = 128
STEP_GROUP = 32
SEQ_UNROLL = 8
HGRN_SEQ_HEADS = 16
HGRN_SEQ_ROWS = 256

V7X_VMEM_BYTES = 64 * 1024 * 1024
VMEM_LIMIT = V7X_VMEM_BYTES - 4 * 1024 * 1024
PROJ_ROW_TILE = 1024
PROJ_IN0_COL_TILE = 2048
PROJ_IN1_COL_TILE = 2304
PROJ_OUT_ROW_TILE = 512
SHORT_COL_TILE = 512
SHORT_IN0_COL_TILE = 1024
STEP_STATE_SEQS = 64
STEP_CACHE_SEQS = 32


def _dot(a, b):
    return jnp.dot(a, b, preferred_element_type=F32)


def _dot_nt(a, b):
    return lax.dot_general(a, b, (((1,), (1,)), ((), ())), preferred_element_type=F32)


def _dot_tn(a, b):
    return lax.dot_general(a, b, (((0,), (0,)), ((), ())), preferred_element_type=F32)


def _silu(x):
    return x * jax.nn.sigmoid(x)


def _mm_kernel(x_ref, w_ref, o_ref):
    o_ref[...] = _dot(x_ref[...].astype(BF16), w_ref[...])


def _row_tile(n, want):
    if n <= want:
        return n
    return max(t for t in range(16, want + 1, 16) if n % t == 0)


def _matmul(x, w, *, tm, tn):
    n, k = x.shape
    e = w.shape[1]
    tm = _row_tile(n, tm)
    assert n % tm == 0 and e % tn == 0
    return pl.pallas_call(
        _mm_kernel,
        grid=(n // tm, e // tn),
        in_specs=[pl.BlockSpec((tm, k), lambda i, j: (i, 0)),
                  pl.BlockSpec((k, tn), lambda i, j: (0, j))],
        out_specs=pl.BlockSpec((tm, tn), lambda i, j: (i, j)),
        out_shape=jax.ShapeDtypeStruct((n, e), F32),
        compiler_params=pltpu.CompilerParams(
            dimension_semantics=("parallel", "parallel"), vmem_limit_bytes=VMEM_LIMIT),
        name="proj_in",
    )(x, w)


def _mm_cast_kernel(xs_ref, xm_ref, w_ref, o_ref, wb_ref, xb_scr):
    @pl.when(pl.program_id(0) == 0)
    def _():
        n_s = xs_ref.shape[0]
        xb_scr[0:n_s, :] = xs_ref[...].astype(BF16)
        xb_scr[n_s:, :] = xm_ref[...].astype(BF16)

    wb = w_ref[...].astype(BF16)
    wb_ref[...] = wb
    o_ref[...] = _dot(xb_scr[...], wb)


def _matmul_cast(xs, xm, w, *, xs_rows, xm_block, tn, col_tile=lambda j: j):
    k, e = w.shape
    n = xs_rows + SEQ_BLOCK
    assert e % tn == 0
    return pl.pallas_call(
        _mm_cast_kernel,
        grid=(e // tn,),
        in_specs=[pl.BlockSpec((xs_rows, k), lambda j: (0, 0), pipeline_mode=pl.Buffered(1)),
                  pl.BlockSpec((SEQ_BLOCK, k), lambda j: (xm_block, 0), pipeline_mode=pl.Buffered(1)),
                  pl.BlockSpec((k, tn), lambda j: (0, col_tile(j)))],
        out_specs=[pl.BlockSpec((n, tn), lambda j: (0, j)),
                   pl.BlockSpec((k, tn), lambda j: (0, j))],
        out_shape=[jax.ShapeDtypeStruct((n, e), F32), jax.ShapeDtypeStruct((k, e), BF16)],
        scratch_shapes=[pltpu.VMEM((n, k), BF16)],
        compiler_params=pltpu.CompilerParams(
            dimension_semantics=("arbitrary",), vmem_limit_bytes=VMEM_LIMIT),
        name="proj_in_cast",
    )(xs, xm, w)


def _mm_ln_tail_kernel(a_ref, at_ref, w_ref, x_ref, xt_ref, g_ref, b_ref, o_ref):
    tail = pl.program_id(0) == pl.num_programs(0) - 1
    a = jnp.where(tail, at_ref[...], a_ref[...])
    x = jnp.where(tail, xt_ref[...], x_ref[...])
    z = DEEPNORM_ALPHA * x + _dot(a, w_ref[...])
    mu = jnp.mean(z, axis=-1, keepdims=True)
    zc = z - mu
    var = jnp.mean(zc * zc, axis=-1, keepdims=True)
    o_ref[...] = zc * lax.rsqrt(var + LN_EPS) * g_ref[...] + b_ref[...]


def _matmul_deepnorm_tail(a, a_tail, w, x, x_tail, g, b):
    n, k = a.shape
    d = w.shape[1]
    tm = SEQ_BLOCK
    last = n // tm - 1
    body = lambda i: (jnp.minimum(i, last), 0)
    const = lambda i: (0, 0)
    return pl.pallas_call(
        _mm_ln_tail_kernel,
        grid=(n // tm + 1,),
        in_specs=[pl.BlockSpec((tm, k), body), pl.BlockSpec((tm, k), const),
                  pl.BlockSpec((k, d), const),
                  pl.BlockSpec((tm, d), body), pl.BlockSpec((tm, d), const),
                  pl.BlockSpec((1, d), const), pl.BlockSpec((1, d), const)],
        out_specs=pl.BlockSpec((tm, d), lambda i: (i, 0)),
        out_shape=jax.ShapeDtypeStruct((n + tm, d), F32),
        compiler_params=pltpu.CompilerParams(
            dimension_semantics=("parallel",), vmem_limit_bytes=VMEM_LIMIT),
        name="proj_out_deepnorm_tail",
    )(a, a_tail, w, x, x_tail, g, b)


def _mm_ln_kernel(a_ref, w_ref, x_ref, g_ref, b_ref, o_ref):
    tm = a_ref.shape[0]
    sub = LN_SUB_ROWS if tm % LN_SUB_ROWS == 0 else tm
    for r0 in range(0, tm, sub):
        rows = slice(r0, r0 + sub)
        z = DEEPNORM_ALPHA * x_ref[rows, :] + _dot(a_ref[rows, :], w_ref[...])
        mu = jnp.mean(z, axis=-1, keepdims=True)
        zc = z - mu
        var = jnp.mean(zc * zc, axis=-1, keepdims=True)
        o_ref[rows, :] = zc * lax.rsqrt(var + LN_EPS) * g_ref[...] + b_ref[...]


def _matmul_deepnorm(a, w, x, g, b, *, tm):
    n, k = a.shape
    d = w.shape[1]
    tm = _row_tile(n, tm)
    return pl.pallas_call(
        _mm_ln_kernel,
        grid=(n // tm,),
        in_specs=[pl.BlockSpec((tm, k), lambda i: (i, 0)),
                  pl.BlockSpec((k, d), lambda i: (0, 0)),
                  pl.BlockSpec((tm, d), lambda i: (i, 0)),
                  pl.BlockSpec((1, d), lambda i: (0, 0)),
                  pl.BlockSpec((1, d), lambda i: (0, 0))],
        out_specs=pl.BlockSpec((tm, d), lambda i: (i, 0)),
        out_shape=jax.ShapeDtypeStruct((n, d), F32),
        compiler_params=pltpu.CompilerParams(
            dimension_semantics=("parallel",), vmem_limit_bytes=VMEM_LIMIT),
        name="proj_out_deepnorm",
    )(a, w, x, g, b)


def _level_masks():
    t = np.arange(CHUNK)
    out = np.zeros((LEVELS + 1, CHUNK, CHUNK), np.float32)
    out[0] = t[:, None] > t[None, :]
    for l in range(LEVELS):
        h = 1 << l
        same = (t[:, None] >> (l + 1)) == (t[None, :] >> (l + 1))
        out[l + 1] = same & ((t[:, None] & h) != 0) & ((t[None, :] & h) == 0)
    return out


def _gates(fx, lb):
    f = lb + (1.0 - lb) * jax.nn.sigmoid(fx)
    return f, jnp.log2(f), 1.0 - f


def _prefix_scan(logf):
    tiles = (CHUNK // 8, 8, HGRN_DK)
    c = logf.reshape(tiles)
    sub = lax.broadcasted_iota(jnp.int32, (1, 8, HGRN_DK), 1)
    bcast = lambda x, r: jnp.broadcast_to(x[:, r:r + 1, :], tiles)
    c = c + jnp.where((sub & 1) != 0, pltpu.roll(c, 1, 1), 0.0)
    c = c + jnp.where((sub & 2) != 0, jnp.where(sub < 4, bcast(c, 1), bcast(c, 5)), 0.0)
    c = c + jnp.where((sub & 4) != 0, bcast(c, 3), 0.0)
    c = c.reshape(CHUNK, HGRN_DK)
    for level in range(3, LEVELS):
        half = 1 << level
        pieces = []
        for r0 in range(0, CHUNK, 2 * half):
            pieces += [c[r0:r0 + half], c[r0 + half:r0 + 2 * half] + c[r0 + half - 1:r0 + half, :]]
        c = jnp.concatenate(pieces, axis=0)
    return c


def _small_levels(logf, f, q, kk):
    tiles = (CHUNK // 8, 8, HGRN_DK)
    c, f3, q3, k3 = (x.reshape(tiles) for x in (logf, f, q, kk))
    sub = lax.broadcasted_iota(jnp.int32, (1, 8, HGRN_DK), 1)
    bcast = lambda x, r: jnp.broadcast_to(x[:, r:r + 1, :], tiles)
    up = (sub & 1) != 0
    zs = [jnp.where(up, q3 * f3, k3)]
    c = c + jnp.where(up, pltpu.roll(c, 1, 1), 0.0)
    for level, tot in ((1, lambda c: jnp.where(sub < 4, bcast(c, 1), bcast(c, 5))),
                       (2, lambda c: bcast(c, 3))):
        up = (sub & (1 << level)) != 0
        t = tot(c)
        zs.append(jnp.where(up, q3, k3) * jnp.exp2(jnp.where(up, c, t - c)))
        c = c + jnp.where(up, t, 0.0)
    return [z.reshape(CHUNK, HGRN_DK) for z in zs], c.reshape(CHUNK, HGRN_DK)


def _level_large(cum, q, kk, level):
    half = 1 << level
    args, bases, cums = [], [], []
    for r0 in range(0, CHUNK, 2 * half):
        lo, up = slice(r0, r0 + half), slice(r0 + half, r0 + 2 * half)
        tot = cum[r0 + half - 1:r0 + half, :]
        args += [tot - cum[lo], cum[up]]
        bases += [kk[lo], q[up]]
        cums += [cum[lo], cum[up] + tot]
    x = jnp.exp2(jnp.concatenate(args, axis=0))
    return jnp.concatenate(bases, axis=0) * x, jnp.concatenate(cums, axis=0)


def _split3(x):
    hi = x.astype(BF16)
    r = x - hi.astype(F32)
    mid = r.astype(BF16)
    lo = (r - mid.astype(F32)).astype(BF16)
    return hi, mid, lo


def _state_update(s_old, kd, v, blast, pad_rows):
    c = kd.shape[0]
    hi, mid, lo = _split3(jnp.exp2(blast))
    row = lax.broadcasted_iota(jnp.int32, (pad_rows, HGRN_DK), 0)
    dec = jnp.where(row == 0, hi.astype(F32),
                    jnp.where(row == 1, mid.astype(F32), jnp.where(row == 2, lo.astype(F32), 0.0)))
    piece = BF16 if c % 16 == 0 and pad_rows % 16 == 0 else F32
    lhs = jnp.concatenate([kd.astype(piece), dec.astype(piece)], axis=0).astype(BF16)
    rhs = jnp.concatenate(
        [jnp.concatenate([v.astype(piece), jnp.zeros((c, HGRN_DV), piece)], axis=1),
         jnp.concatenate([jnp.zeros((pad_rows, HGRN_DV), piece), jnp.ones((pad_rows, HGRN_DV), piece)], axis=1)],
        axis=0).astype(BF16)
    both = _dot_tn(lhs, rhs)
    return both[:, HGRN_DV:] * s_old + both[:, :HGRN_DV]


def _norm_gate(o, g, nw):
    o = o * lax.rsqrt(jnp.mean(o * o, axis=-1, keepdims=True) + RMS_EPS) * nw
    return o * _silu(g)


def _hgrn_seq_kernel(q_ref, fx_ref, i_ref, g_ref, lb_ref, nw_ref, s0_ref, m_ref,
                     og_ref, sout_ref, s_scr, o_scr, *, n_chunks):
    heads = s_scr.shape[0]

    @pl.when(pl.program_id(2) == 0)
    def _():
        s_scr[...] = s0_ref[0]

    def chunk(c, carry):
        rows = pl.ds(pl.multiple_of(c * CHUNK, CHUNK), CHUNK)
        hcols = [slice(hd * HGRN_DK, (hd + 1) * HGRN_DK) for hd in range(heads)]
        q = [q_ref[rows, cs] for cs in hcols]
        v = [i_ref[rows, cs] for cs in hcols]
        gates = [_gates(fx_ref[rows, cs], lb_ref[:, cs]) for cs in hcols]
        kk = [gt[2] for gt in gates]
        cum = [_prefix_scan(gt[1]) for gt in gates]
        mid = CHUNK // 2 - 1
        spread = None
        for b in cum:
            s_hd = jnp.maximum(b[0:1, :] - b[mid:mid + 1, :], b[mid:mid + 1, :] - b[CHUNK - 1:CHUNK, :])
            spread = s_hd if spread is None else jnp.maximum(spread, s_hd)
        factorable = jnp.max(spread) <= MAX_FACTORED_LOG2_DECAY

        vb = [x.astype(BF16) for x in v]
        for hd in range(heads):
            d = cum[hd] - cum[hd][mid:mid + 1, :]
            qe = (q[hd] * jnp.exp2(d)).astype(BF16)
            ke = (kk[hd] * jnp.exp2(-d)).astype(BF16)
            amat = jnp.where(m_ref[0] > 0.5, _dot_nt(qe, ke), 0.0).astype(BF16)
            blast = cum[hd][CHUNK - 1:CHUNK, :]
            qd = (q[hd] * jnp.exp2(cum[hd])).astype(BF16)
            kd = kk[hd] * jnp.exp2(blast - cum[hd])
            s_old = s_scr[hd]
            base = _dot(qd, s_old.astype(BF16)) + jnp.sum(q[hd] * kk[hd], axis=-1, keepdims=True) * v[hd]
            o_scr[hd] = base
            s_scr[hd] = _state_update(s_old, kd, vb[hd], blast, 16)
            og_ref[rows, hcols[hd]] = _norm_gate(
                base + _dot(amat, vb[hd]), g_ref[rows, hcols[hd]], nw_ref[:, hcols[hd]]).astype(og_ref.dtype)

        @pl.when(jnp.logical_not(factorable))
        def _():
            small = [_small_levels(gt[1], gt[0], q[hd], kk[hd]) for hd, gt in enumerate(gates)]
            zs = [sm[0] for sm in small]
            part = [sm[1] for sm in small]
            for level in range(3, LEVELS):
                for hd in range(heads):
                    z, part[hd] = _level_large(part[hd], q[hd], kk[hd], level)
                    zs[hd].append(z)
            nblk = CHUNK // MASK_ROWS
            a = [[None] * nblk for _ in range(heads)]
            for level in range(LEVELS):
                step = 1 << (level - 4) if level >= 4 else 0
                blocks = [i for i in range(nblk) if level < 4 or (i & step)]
                for hd in range(heads):
                    z = zs[hd][level].astype(BF16)
                    lhs = z if level < 4 else jnp.concatenate(
                        [z[i * MASK_ROWS:(i + 1) * MASK_ROWS] for i in blocks], axis=0)
                    term = _dot_nt(lhs, z)
                    for n, i in enumerate(blocks):
                        t = (term[n * MASK_ROWS:(n + 1) * MASK_ROWS]
                             * m_ref[level + 1, i * MASK_ROWS:(i + 1) * MASK_ROWS, :])
                        a[hd][i] = t if a[hd][i] is None else a[hd][i] + t
            for hd in range(heads):
                amat = jnp.concatenate(a[hd], axis=0).astype(BF16)
                og_ref[rows, hcols[hd]] = _norm_gate(
                    o_scr[hd] + _dot(amat, vb[hd]),
                    g_ref[rows, hcols[hd]], nw_ref[:, hcols[hd]]).astype(og_ref.dtype)
        return carry

    lax.fori_loop(0, n_chunks, chunk, 0)

    @pl.when(pl.program_id(2) == pl.num_programs(2) - 1)
    def _():
        sout_ref[0] = s_scr[...]


def _hgrn_seq(h, lb, nw, s0, *, n_seq, seq_len, row_block0):
    hb = HGRN_SEQ_HEADS
    width = hb * HGRN_DK
    groups = HGRN_HEADS // hb
    rb = min(HGRN_SEQ_ROWS, seq_len)
    nrb = seq_len // rb
    assert seq_len % rb == 0 and rb % CHUNK == 0 and (row_block0 * SEQ_BLOCK) % rb == 0
    rb0 = row_block0 * SEQ_BLOCK // rb
    s0_batched = s0.shape[0] != 1
    col = lambda part: (lambda b, hg, r: (rb0 + b * nrb + r, part * groups + hg))
    masks = jnp.asarray(_level_masks())
    return pl.pallas_call(
        functools.partial(_hgrn_seq_kernel, n_chunks=rb // CHUNK),
        grid=(n_seq, groups, nrb),
        in_specs=[pl.BlockSpec((rb, width), col(0)),
                  pl.BlockSpec((rb, width), col(1)),
                  pl.BlockSpec((rb, width), col(2)),
                  pl.BlockSpec((rb, width), col(3)),
                  pl.BlockSpec((1, width), lambda b, hg, r: (0, hg)),
                  pl.BlockSpec((1, width), lambda b, hg, r: (0, hg)),
                  pl.BlockSpec((1, hb, HGRN_DK, HGRN_DV),
                               (lambda b, hg, r: (b, hg, 0, 0)) if s0_batched else (lambda b, hg, r: (0, hg, 0, 0))),
                  pl.BlockSpec((LEVELS + 1, CHUNK, CHUNK), lambda b, hg, r: (0, 0, 0))],
        out_specs=[pl.BlockSpec((rb, width), lambda b, hg, r: (b * nrb + r, hg)),
                   pl.BlockSpec((1, hb, HGRN_DK, HGRN_DV), lambda b, hg, r: (b, hg, 0, 0))],
        out_shape=[jax.ShapeDtypeStruct((n_seq * seq_len, HGRN_HEADS * HGRN_DV), BF16),
                   jax.ShapeDtypeStruct((n_seq, HGRN_HEADS, HGRN_DK, HGRN_DV), F32)],
        scratch_shapes=[pltpu.VMEM((hb, HGRN_DK, HGRN_DV), F32),
                        pltpu.VMEM((hb, CHUNK, HGRN_DV), F32)],
        compiler_params=pltpu.CompilerParams(
            dimension_semantics=("parallel", "parallel", "arbitrary"), vmem_limit_bytes=VMEM_LIMIT),
        name="hgrn_seq",
    )(h, h, h, h, lb, nw, s0, masks)


def _hgrn_step_kernel(q_ref, fx_ref, i_ref, g_ref, lb_ref, nw_ref, s_ref, og_ref, sout_ref, *, n_seq, t):
    lb = lb_ref[...]
    nw = nw_ref[...]
    tile = (STEP_GROUP, t, HGRN_DK)
    sub = lax.broadcasted_iota(jnp.int32, (1, t, HGRN_DK), 1)

    def group(i, carry):
        rows = pl.ds(pl.multiple_of(i * (STEP_GROUP * t), STEP_GROUP * t), STEP_GROUP * t)
        q = q_ref[rows, :].reshape(tile)
        v = i_ref[rows, :].reshape(tile)
        _, logf, kk = _gates(fx_ref[rows, :].reshape(tile), lb)
        cum = logf
        shift = 1
        while shift < t:
            cum = cum + jnp.where(sub >= shift, pltpu.roll(cum, shift, 1), 0.0)
            shift *= 2
        intra = (jnp.sum(q * kk, axis=-1, keepdims=True)) * v
        for d in range(1, t):
            valid = sub >= d
            x = jnp.exp2(jnp.where(valid, cum - pltpu.roll(cum, d, 1), 0.0))
            w = jnp.sum(jnp.where(valid, q * x * pltpu.roll(kk, d, 1), 0.0), axis=-1, keepdims=True)
            intra = intra + w * pltpu.roll(v, d, 1)
        blast = cum[:, t - 1:t, :]
        qd = q * jnp.exp2(cum)
        kd = kk * jnp.exp2(blast - cum)
        seqs = [i * STEP_GROUP + n for n in range(STEP_GROUP)]
        inter = [_dot(qd[n].astype(BF16), s_ref[b, 0].astype(BF16)) for n, b in enumerate(seqs)]
        for n, b in enumerate(seqs):
            sout_ref[b, 0] = _state_update(s_ref[b, 0], kd[n], v[n], blast[n], 8)
        o = jnp.stack(inter, axis=0) + intra
        og = _norm_gate(o, g_ref[rows, :].reshape(tile), nw)
        og_ref[rows, :] = og.reshape(STEP_GROUP * t, HGRN_DK).astype(og_ref.dtype)
        return carry

    lax.fori_loop(0, n_seq // STEP_GROUP, group, 0)


def _hgrn_step(h, lb, nw, s0, *, n_seq, t, seq_per_block):
    seq_per_block = min(seq_per_block, n_seq)
    assert n_seq % seq_per_block == 0 and seq_per_block % STEP_GROUP == 0 and t == 8
    rows = seq_per_block * t
    col = lambda part: (lambda i, hd: (i, part * HGRN_HEADS + hd))
    return pl.pallas_call(
        functools.partial(_hgrn_step_kernel, n_seq=seq_per_block, t=t),
        grid=(n_seq // seq_per_block, HGRN_HEADS),
        in_specs=[pl.BlockSpec((rows, HGRN_DK), col(0)),
                  pl.BlockSpec((rows, HGRN_DK), col(1)),
                  pl.BlockSpec((rows, HGRN_DV), col(2)),
                  pl.BlockSpec((rows, HGRN_DV), col(3)),
                  pl.BlockSpec((1, HGRN_DK), lambda i, hd: (0, hd)),
                  pl.BlockSpec((1, HGRN_DV), lambda i, hd: (0, hd)),
                  pl.BlockSpec((seq_per_block, 1, HGRN_DK, HGRN_DV), lambda i, hd: (i, hd, 0, 0))],
        out_specs=[pl.BlockSpec((rows, HGRN_DV), lambda i, hd: (i, hd)),
                   pl.BlockSpec((seq_per_block, 1, HGRN_DK, HGRN_DV), lambda i, hd: (i, hd, 0, 0))],
        out_shape=[jax.ShapeDtypeStruct((n_seq * t, HGRN_HEADS * HGRN_DV), BF16),
                   jax.ShapeDtypeStruct((n_seq, HGRN_HEADS, HGRN_DK, HGRN_DV), F32)],
        compiler_params=pltpu.CompilerParams(
            dimension_semantics=("parallel", "parallel"), vmem_limit_bytes=VMEM_LIMIT),
        name="hgrn_step",
    )(h, h, h, h, lb, nw, s0)


def _attn_seq_kernel(q_ref, g_ref, kc_ref, vc_ref, kp_ref, vp_ref, meta_ref, sink_ref, og_ref):
    first = pl.program_id(1) == 0
    kprev = jnp.where(first, meta_ref[:, :SWA_KV_WIDTH], kp_ref[...])
    vprev = jnp.where(first, meta_ref[:, SWA_KV_WIDTH:], vp_ref[...])
    kband = (jnp.concatenate([kprev, kc_ref[...]], axis=0) * (SWA_SCALE * LOG2E)).astype(BF16)
    vband_t = jnp.concatenate([vprev, vc_ref[...]], axis=0).T.astype(BF16)
    zeros_t = jnp.zeros((SWA_HEAD_DIM, 2 * SEQ_BLOCK), BF16)
    kj = lax.broadcasted_iota(jnp.int32, (2 * SEQ_BLOCK, SEQ_BLOCK), 0)
    qi = lax.broadcasted_iota(jnp.int32, (2 * SEQ_BLOCK, SEQ_BLOCK), 1)
    dist = SEQ_BLOCK + qi - kj
    valid = (dist >= 0) & (dist < WINDOW) & (jnp.logical_not(first) | (kj >= SEQ_BLOCK - N_META))
    madd = jnp.where(valid, 0.0, -jnp.inf)
    zeros = jnp.zeros((2 * SEQ_BLOCK, SWA_HEAD_DIM), BF16)
    pairs = SWA_GROUP // 2
    pw = 2 * SWA_HEAD_DIM

    def block_diag(x):
        return jnp.concatenate([jnp.concatenate([x, zeros], axis=1),
                                jnp.concatenate([zeros, x], axis=1)], axis=0)

    sts = []
    for kvh in range(SWA_KV_HEADS):
        ks = slice(kvh * SWA_HEAD_DIM, (kvh + 1) * SWA_HEAD_DIM)
        col0 = kvh * SWA_GROUP * SWA_HEAD_DIM
        xq = jnp.concatenate([q_ref[:, col0 + p * pw:col0 + (p + 1) * pw].astype(BF16)
                              for p in range(pairs)], axis=0)
        sts.append(_dot_nt(block_diag(kband[:, ks]), xq))
    for kvh in range(SWA_KV_HEADS):
        ks = slice(kvh * SWA_HEAD_DIM, (kvh + 1) * SWA_HEAD_DIM)
        col0 = kvh * SWA_GROUP * SWA_HEAD_DIM
        vt = vband_t[ks, :]
        v2t = jnp.concatenate([jnp.concatenate([vt, zeros_t], axis=1),
                               jnp.concatenate([zeros_t, vt], axis=1)], axis=0)
        pbs, rdens = [], []
        for pr in range(pairs):
            head = kvh * SWA_GROUP + 2 * pr
            s2 = (sts[kvh][:, pr * SEQ_BLOCK:(pr + 1) * SEQ_BLOCK].reshape(2, 2 * SEQ_BLOCK, SEQ_BLOCK)
                  + madd[None])
            sink = jnp.concatenate([jnp.full((1, 1, SEQ_BLOCK), sink_ref[head] * LOG2E, F32),
                                    jnp.full((1, 1, SEQ_BLOCK), sink_ref[head + 1] * LOG2E, F32)], axis=0)
            m = jnp.maximum(jnp.max(s2, axis=1, keepdims=True), sink)
            p = jnp.exp2(s2 - m)
            den = jnp.sum(p, axis=1, keepdims=True) + jnp.exp2(sink - m)
            pbs.append(p.astype(BF16).reshape(4 * SEQ_BLOCK, SEQ_BLOCK))
            rdens.append(1.0 / den)
        for pr in range(pairs):
            ot = _dot(v2t, pbs[pr]).reshape(2, SWA_HEAD_DIM, SEQ_BLOCK) * rdens[pr]
            o = ot.reshape(2 * SWA_HEAD_DIM, SEQ_BLOCK).T
            cs = slice(col0 + pr * pw, col0 + (pr + 1) * pw)
            og_ref[:, cs] = (o * _silu(g_ref[:, cs])).astype(og_ref.dtype)


def _attn_seq(h1, h1_meta, sinks, *, n_seq, seq_len, meta_row_block):
    nb = seq_len // SEQ_BLOCK
    width = SWA_Q_HEADS * SWA_HEAD_DIM
    kcol = 2 * width // SWA_KV_WIDTH
    return pl.pallas_call(
        _attn_seq_kernel,
        grid=(n_seq, nb),
        in_specs=[pl.BlockSpec((SEQ_BLOCK, width), lambda b, j: (b * nb + j, 0)),
                  pl.BlockSpec((SEQ_BLOCK, width), lambda b, j: (b * nb + j, 1)),
                  pl.BlockSpec((SEQ_BLOCK, SWA_KV_WIDTH), lambda b, j: (b * nb + j, kcol)),
                  pl.BlockSpec((SEQ_BLOCK, SWA_KV_WIDTH), lambda b, j: (b * nb + j, kcol + 1)),
                  pl.BlockSpec((SEQ_BLOCK, SWA_KV_WIDTH), lambda b, j: (b * nb + jnp.maximum(j - 1, 0), kcol)),
                  pl.BlockSpec((SEQ_BLOCK, SWA_KV_WIDTH), lambda b, j: (b * nb + jnp.maximum(j - 1, 0), kcol + 1)),
                  pl.BlockSpec((SEQ_BLOCK, 2 * SWA_KV_WIDTH), lambda b, j: (meta_row_block, kcol // 2)),
                  pl.BlockSpec(memory_space=pltpu.SMEM)],
        out_specs=pl.BlockSpec((SEQ_BLOCK, width), lambda b, j: (b * nb + j, 0)),
        out_shape=jax.ShapeDtypeStruct((n_seq * seq_len, width), BF16),
        compiler_params=pltpu.CompilerParams(
            dimension_semantics=("parallel", "arbitrary"), vmem_limit_bytes=VMEM_LIMIT),
        name="attn_seq",
    )(h1, h1, h1, h1, h1, h1, h1_meta, sinks)


def _attn_step_kernel(q_ref, g_ref, kn_ref, vn_ref, ck_ref, cv_ref, sink_ref, og_ref, nk_ref, nv_ref, *, n_seq, t):
    keys = WINDOW + t
    hd = SWA_HEAD_DIM
    tiles = SWA_Q_HEADS // 2
    kj = lax.broadcasted_iota(jnp.int32, (keys, SWA_Q_HEADS * t), 0)
    qt = lax.broadcasted_iota(jnp.int32, (keys, SWA_Q_HEADS * t), 1) % t
    madd = jnp.where((kj >= qt + 1) & (kj <= WINDOW + qt), 0.0, -jnp.inf)
    low = lax.broadcasted_iota(jnp.int32, (t, 2 * hd), 1) < hd
    zero_tile = jnp.zeros((t, 2 * hd), F32)
    sink = sink_ref[...] * LOG2E

    def group(i, carry):
        seqs = [i * SEQ_UNROLL + n for n in range(SEQ_UNROLL)]
        rows = [pl.ds(pl.multiple_of(b * t, t), t) for b in seqs]
        st, vall = [], []
        for b, rw in zip(seqs, rows):
            kc, vc = ck_ref[b], cv_ref[b]
            kn, vn = kn_ref[rw, :], vn_ref[rw, :]
            nk_ref[b, 0:WINDOW - t, :] = kc[t:, :]
            nk_ref[b, WINDOW - t:WINDOW, :] = kn
            nv_ref[b, 0:WINDOW - t, :] = vc[t:, :]
            nv_ref[b, WINDOW - t:WINDOW, :] = vn
            kall = jnp.concatenate([kc, kn], axis=0).astype(BF16)
            vall.append(jnp.concatenate([vc, vn], axis=0).astype(BF16))
            q = q_ref[rw, :] * (SWA_SCALE * LOG2E)
            qtile = [q[:, j * 2 * hd:(j + 1) * 2 * hd] for j in range(tiles)]
            qswap = [pltpu.roll(x, hd, 1) for x in qtile]
            groups = []
            for kvh in range(SWA_KV_HEADS):
                for gq in range(SWA_GROUP):
                    j = (kvh * SWA_GROUP + gq) // 2
                    src = qtile[j] if gq % 2 == kvh % 2 else qswap[j]
                    half = jnp.where(low, src, 0.0) if kvh % 2 == 0 else jnp.where(low, 0.0, src)
                    groups.append(jnp.concatenate(
                        [half if c == kvh // 2 else zero_tile for c in range(SWA_KV_HEADS // 2)], axis=1))
            qbd = jnp.concatenate(groups, axis=0).astype(BF16)
            st.append(_dot_nt(kall, qbd))
        pn = []
        for s in st:
            s = s + madd
            m = jnp.maximum(jnp.max(s, axis=0, keepdims=True), sink)
            p = jnp.exp2(s - m)
            den = jnp.sum(p, axis=0, keepdims=True) + jnp.exp2(sink - m)
            pn.append((p * (1.0 / den)).astype(BF16))
        of = [_dot_tn(p, vl) for p, vl in zip(pn, vall)]
        for o, rw in zip(of, rows):
            g = g_ref[rw, :]
            out = []
            for j in range(tiles):
                kvh = (2 * j) // SWA_GROUP
                ct = slice((kvh // 2) * 2 * hd, (kvh // 2 + 1) * 2 * hd)
                ra = o[(2 * j) * t:(2 * j + 1) * t, ct]
                rb = o[(2 * j + 1) * t:(2 * j + 2) * t, ct]
                if kvh % 2 == 0:
                    out.append(jnp.where(low, ra, pltpu.roll(rb, hd, 1)))
                else:
                    out.append(jnp.where(low, pltpu.roll(ra, hd, 1), rb))
            og_ref[rw, :] = (jnp.concatenate(out, axis=1) * _silu(g)).astype(og_ref.dtype)
        return carry

    lax.fori_loop(0, n_seq // SEQ_UNROLL, group, 0)


def _attn_step(h1, cache_k, cache_v, sink_cols, *, n_seq, t, seq_per_block):
    assert n_seq % seq_per_block == 0
    rows = seq_per_block * t
    width = SWA_Q_HEADS * SWA_HEAD_DIM
    kcol = 2 * width // SWA_KV_WIDTH
    cache_spec = pl.BlockSpec((seq_per_block, WINDOW, SWA_KV_WIDTH), lambda i: (i, 0, 0))
    return pl.pallas_call(
        functools.partial(_attn_step_kernel, n_seq=seq_per_block, t=t),
        grid=(n_seq // seq_per_block,),
        in_specs=[pl.BlockSpec((rows, width), lambda i: (i, 0)),
                  pl.BlockSpec((rows, width), lambda i: (i, 1)),
                  pl.BlockSpec((rows, SWA_KV_WIDTH), lambda i: (i, kcol)),
                  pl.BlockSpec((rows, SWA_KV_WIDTH), lambda i: (i, kcol + 1)),
                  cache_spec, cache_spec,
                  pl.BlockSpec((1, SWA_Q_HEADS * t), lambda i: (0, 0))],
        out_specs=[pl.BlockSpec((rows, width), lambda i: (i, 0)), cache_spec, cache_spec],
        out_shape=[jax.ShapeDtypeStruct((n_seq * t, width), BF16),
                   jax.ShapeDtypeStruct(cache_k.shape, F32),
                   jax.ShapeDtypeStruct(cache_v.shape, F32)],
        compiler_params=pltpu.CompilerParams(
            dimension_semantics=("parallel",), vmem_limit_bytes=VMEM_LIMIT),
        name="attn_step",
    )(h1, h1, h1, h1, cache_k, cache_v, sink_cols)


def kernel(x_prompt, x_sample, state_hgrn, cache_swa_k, cache_swa_v, meta_tokens,
           hgrn_w_in, hgrn_lb_logits, hgrn_norm_w, hgrn_w_out,
           swa_w_in, swa_sinks, swa_w_out, ln_g, ln_b):
    out_dtype = x_prompt.dtype
    bsz, seq, d = x_prompt.shape
    dec_b, dec_t, _ = x_sample.shape
    n_p = bsz * seq
    n_s = dec_b * dec_t
    width = SWA_Q_HEADS * SWA_HEAD_DIM

    w_out0 = hgrn_w_out[0].astype(BF16)
    w_out1 = swa_w_out[0].astype(BF16)
    in_tile = SHORT_COL_TILE
    q_tiles = width // in_tile
    kv_tile = 2 * SWA_KV_WIDTH // in_tile
    assert width % in_tile == 0 and (2 * SWA_KV_WIDTH) % in_tile == 0 and kv_tile == 1
    regroup = lambda j: jnp.where(j < q_tiles, j, jnp.where(j < 2 * q_tiles, j + kv_tile, q_tiles))
    lb = jnp.cumsum(jax.nn.softmax(hgrn_lb_logits.astype(F32), axis=0), axis=0)[0:1]
    nw = hgrn_norm_w[0].astype(F32).reshape(1, -1)
    sinks = swa_sinks[0].astype(F32).reshape(1, SWA_Q_HEADS)
    sink_cols = jnp.repeat(sinks, dec_t, axis=1)
    g0, b0 = ln_g[0:1].astype(F32), ln_b[0:1].astype(F32)
    g1, b1 = ln_g[1:2].astype(F32), ln_b[1:2].astype(F32)

    x_p = x_prompt.astype(F32).reshape(n_p, d)
    meta_block = jnp.concatenate(
        [jnp.zeros((SEQ_BLOCK - N_META, d), F32), meta_tokens.astype(F32)], axis=0)
    x_s = x_sample.astype(F32).reshape(n_s, d)
    meta_blk = n_s // SEQ_BLOCK

    h_sm, w_in0 = _matmul_cast(x_s, meta_block, hgrn_w_in[0].astype(F32), xs_rows=n_s, xm_block=0, tn=SHORT_IN0_COL_TILE)
    h_p = _matmul(x_p, w_in0, tm=PROJ_ROW_TILE, tn=PROJ_IN0_COL_TILE)
    zero_state = jnp.zeros((1, HGRN_HEADS, HGRN_DK, HGRN_DV), F32)
    og_meta, s_meta = _hgrn_seq(h_sm, lb, nw, zero_state, n_seq=1, seq_len=SEQ_BLOCK, row_block0=meta_blk)
    og_s, st_s = _hgrn_step(h_sm, lb, nw, state_hgrn[0].astype(F32), n_seq=dec_b, t=dec_t, seq_per_block=STEP_STATE_SEQS)
    og_p, st_p = _hgrn_seq(h_p, lb, nw, s_meta, n_seq=bsz, seq_len=seq, row_block0=0)
    x1_sm = _matmul_deepnorm_tail(og_s, og_meta, w_out0, x_s, meta_block, g0, b0)
    x1_p = _matmul_deepnorm(og_p, w_out0, x_p, g0, b0, tm=PROJ_OUT_ROW_TILE)

    h1_sm, w_in1 = _matmul_cast(x1_sm, x1_sm, swa_w_in[0].astype(F32), xs_rows=n_s, xm_block=meta_blk,
                                tn=in_tile, col_tile=regroup)
    h1_p = _matmul(x1_p, w_in1, tm=PROJ_ROW_TILE, tn=PROJ_IN1_COL_TILE)
    og1_p = _attn_seq(h1_p, h1_sm, sinks.reshape(-1), n_seq=bsz, seq_len=seq, meta_row_block=meta_blk)
    ck = cache_swa_k[0].astype(F32).reshape(dec_b, WINDOW, SWA_KV_WIDTH)
    cv = cache_swa_v[0].astype(F32).reshape(dec_b, WINDOW, SWA_KV_WIDTH)
    og1_s, nk_s, nv_s = _attn_step(h1_sm, ck, cv, sink_cols, n_seq=dec_b, t=dec_t, seq_per_block=STEP_CACHE_SEQS)
    y_p = _matmul_deepnorm(og1_p, w_out1, x1_p, g1, b1, tm=PROJ_OUT_ROW_TILE)
    y_s = _matmul_deepnorm(og1_s, w_out1, x1_sm, g1, b1, tm=PROJ_OUT_ROW_TILE)

    kv_p = h1_p.reshape(bsz, seq, -1)[:, seq - WINDOW:, 2 * width:]
    cache_shape = (1, bsz, WINDOW, SWA_KV_HEADS, SWA_HEAD_DIM)
    return (y_p.reshape(bsz, seq, d).astype(out_dtype),
            y_s.reshape(dec_b, dec_t, d).astype(out_dtype),
            st_p[None].astype(out_dtype),
            st_s[None].astype(out_dtype),
            kv_p[..., :SWA_KV_WIDTH].reshape(cache_shape).astype(out_dtype),
            kv_p[..., SWA_KV_WIDTH:].reshape(cache_shape).astype(out_dtype),
            nk_s.reshape((1,) + cache_swa_k.shape[1:]).astype(out_dtype),
            nv_s.reshape((1,) + cache_swa_v.shape[1:]).astype(out_dtype))
```

```python
import functools

import numpy as np
import jax
import jax.numpy as jnp
from jax import lax
from jax.experimental import pallas as pl
from jax.experimental.pallas import tpu as pltpu

F32 = jnp.float32
BF16 = jnp.bfloat16

D_MODEL = 2048
N_META = 16
DEPTH = 2
HGRN_HEADS = 16
HGRN_DK = 128
HGRN_DV = 128
SWA_Q_HEADS = 32
SWA_KV_HEADS = 4
SWA_GROUP = 8
SWA_HEAD_DIM = 64
SWA_KV_WIDTH = SWA_KV_HEADS * SWA_HEAD_DIM
SWA_SCALE = SWA_HEAD_DIM ** -0.5
WINDOW = 128
DEEPNORM_ALPHA = (2.0 * DEPTH) ** 0.25
LN_EPS = 1e-5
RMS_EPS = 1e-6
LOG2E = 1.4426950408889634

CHUNK = 128
LEVELS = 7
MAX_FACTORED_LOG2_DECAY = 80.0
MASK_ROWS = 16
SEQ_BLOCK = 128
LN_SUB_ROWS = 128
STEP_GROUP = 32
SEQ_UNROLL = 8
HGRN_SEQ_HEADS = 16
HGRN_SEQ_ROWS = 256

V7X_VMEM_BYTES = 64 * 1024 * 1024
VMEM_LIMIT = V7X_VMEM_BYTES - 4 * 1024 * 1024
PROJ_ROW_TILE = 1024
PROJ_IN0_COL_TILE = 2048
PROJ_IN1_COL_TILE = 2304
PROJ_OUT_ROW_TILE = 512
SHORT_COL_TILE = 512
SHORT_IN0_COL_TILE = 1024
STEP_STATE_SEQS = 128
STEP_CACHE_SEQS = 32


def _dot(a, b):
    return jnp.dot(a, b, preferred_element_type=F32)


def _dot_nt(a, b):
    return lax.dot_general(a, b, (((1,), (1,)), ((), ())), preferred_element_type=F32)


def _dot_tn(a, b):
    return lax.dot_general(a, b, (((0,), (0,)), ((), ())), preferred_element_type=F32)


def _silu(x):
    return x * jax.nn.sigmoid(x)


def _mm_kernel(x_ref, w_ref, o_ref):
    o_ref[...] = _dot(x_ref[...].astype(BF16), w_ref[...])


def _row_tile(n, want):
    if n <= want:
        return n
    return max(t for t in range(16, want + 1, 16) if n % t == 0)


def _matmul(x, w, *, tm, tn):
    n, k = x.shape
    e = w.shape[1]
    tm = _row_tile(n, tm)
    assert n % tm == 0 and e % tn == 0
    return pl.pallas_call(
        _mm_kernel,
        grid=(n // tm, e // tn),
        in_specs=[pl.BlockSpec((tm, k), lambda i, j: (i, 0)),
                  pl.BlockSpec((k, tn), lambda i, j: (0, j))],
        out_specs=pl.BlockSpec((tm, tn), lambda i, j: (i, j)),
        out_shape=jax.ShapeDtypeStruct((n, e), F32),
        compiler_params=pltpu.CompilerParams(
            dimension_semantics=("parallel", "parallel"), vmem_limit_bytes=VMEM_LIMIT),
        name="proj_in",
    )(x, w)


def _mm_cast_kernel(xs_ref, xm_ref, w_ref, o_ref, wb_ref, xb_scr):
    @pl.when(pl.program_id(0) == 0)
    def _():
        n_s = xs_ref.shape[0]
        xb_scr[0:n_s, :] = xs_ref[...].astype(BF16)
        xb_scr[n_s:, :] = xm_ref[...].astype(BF16)

    wb = w_ref[...].astype(BF16)
    wb_ref[...] = wb
    o_ref[...] = _dot(xb_scr[...], wb)


def _matmul_cast(xs, xm, w, *, xs_rows, xm_block, tn, col_tile=lambda j: j):
    k, e = w.shape
    n = xs_rows + SEQ_BLOCK
    assert e % tn == 0
    return pl.pallas_call(
        _mm_cast_kernel,
        grid=(e // tn,),
        in_specs=[pl.BlockSpec((xs_rows, k), lambda j: (0, 0), pipeline_mode=pl.Buffered(1)),
                  pl.BlockSpec((SEQ_BLOCK, k), lambda j: (xm_block, 0), pipeline_mode=pl.Buffered(1)),
                  pl.BlockSpec((k, tn), lambda j: (0, col_tile(j)))],
        out_specs=[pl.BlockSpec((n, tn), lambda j: (0, j)),
                   pl.BlockSpec((k, tn), lambda j: (0, j))],
        out_shape=[jax.ShapeDtypeStruct((n, e), F32), jax.ShapeDtypeStruct((k, e), BF16)],
        scratch_shapes=[pltpu.VMEM((n, k), BF16)],
        compiler_params=pltpu.CompilerParams(
            dimension_semantics=("arbitrary",), vmem_limit_bytes=VMEM_LIMIT),
        name="proj_in_cast",
    )(xs, xm, w)


def _mm_ln_tail_kernel(a_ref, at_ref, w_ref, x_ref, xt_ref, g_ref, b_ref, o_ref):
    tail = pl.program_id(0) == pl.num_programs(0) - 1
    a = jnp.where(tail, at_ref[...], a_ref[...])
    x = jnp.where(tail, xt_ref[...], x_ref[...])
    z = DEEPNORM_ALPHA * x + _dot(a, w_ref[...])
    mu = jnp.mean(z, axis=-1, keepdims=True)
    zc = z - mu
    var = jnp.mean(zc * zc, axis=-1, keepdims=True)
    o_ref[...] = zc * lax.rsqrt(var + LN_EPS) * g_ref[...] + b_ref[...]


def _matmul_deepnorm_tail(a, a_tail, w, x, x_tail, g, b):
    n, k = a.shape
    d = w.shape[1]
    tm = SEQ_BLOCK
    last = n // tm - 1
    body = lambda i: (jnp.minimum(i, last), 0)
    const = lambda i: (0, 0)
    return pl.pallas_call(
        _mm_ln_tail_kernel,
        grid=(n // tm + 1,),
        in_specs=[pl.BlockSpec((tm, k), body), pl.BlockSpec((tm, k), const),
                  pl.BlockSpec((k, d), const),
                  pl.BlockSpec((tm, d), body), pl.BlockSpec((tm, d), const),
                  pl.BlockSpec((1, d), const), pl.BlockSpec((1, d), const)],
        out_specs=pl.BlockSpec((tm, d), lambda i: (i, 0)),
        out_shape=jax.ShapeDtypeStruct((n + tm, d), F32),
        compiler_params=pltpu.CompilerParams(
            dimension_semantics=("parallel",), vmem_limit_bytes=VMEM_LIMIT),
        name="proj_out_deepnorm_tail",
    )(a, a_tail, w, x, x_tail, g, b)


def _mm_ln_kernel(a_ref, w_ref, x_ref, g_ref, b_ref, o_ref):
    tm = a_ref.shape[0]
    sub = LN_SUB_ROWS if tm % LN_SUB_ROWS == 0 else tm
    for r0 in range(0, tm, sub):
        rows = slice(r0, r0 + sub)
        z = DEEPNORM_ALPHA * x_ref[rows, :] + _dot(a_ref[rows, :], w_ref[...])
        mu = jnp.mean(z, axis=-1, keepdims=True)
        zc = z - mu
        var = jnp.mean(zc * zc, axis=-1, keepdims=True)
        o_ref[rows, :] = zc * lax.rsqrt(var + LN_EPS) * g_ref[...] + b_ref[...]


def _matmul_deepnorm(a, w, x, g, b, *, tm):
    n, k = a.shape
    d = w.shape[1]
    tm = _row_tile(n, tm)
    return pl.pallas_call(
        _mm_ln_kernel,
        grid=(n // tm,),
        in_specs=[pl.BlockSpec((tm, k), lambda i: (i, 0)),
                  pl.BlockSpec((k, d), lambda i: (0, 0)),
                  pl.BlockSpec((tm, d), lambda i: (i, 0)),
                  pl.BlockSpec((1, d), lambda i: (0, 0)),
                  pl.BlockSpec((1, d), lambda i: (0, 0))],
        out_specs=pl.BlockSpec((tm, d), lambda i: (i, 0)),
        out_shape=jax.ShapeDtypeStruct((n, d), F32),
        compiler_params=pltpu.CompilerParams(
            dimension_semantics=("parallel",), vmem_limit_bytes=VMEM_LIMIT),
        name="proj_out_deepnorm",
    )(a, w, x, g, b)


def _level_masks():
    t = np.arange(CHUNK)
    out = np.zeros((LEVELS + 1, CHUNK, CHUNK), np.float32)
    out[0] = t[:, None] > t[None, :]
    for l in range(LEVELS):
        h = 1 << l
        same = (t[:, None] >> (l + 1)) == (t[None, :] >> (l + 1))
        out[l + 1] = same & ((t[:, None] & h) != 0) & ((t[None, :] & h) == 0)
    return out


def _gates(fx, lb):
    f = lb + (1.0 - lb) * jax.nn.sigmoid(fx)
    return f, jnp.log2(f), 1.0 - f


def _prefix_scan(logf):
    tiles = (CHUNK // 8, 8, HGRN_DK)
    c = logf.reshape(tiles)
    sub = lax.broadcasted_iota(jnp.int32, (1, 8, HGRN_DK), 1)
    bcast = lambda x, r: jnp.broadcast_to(x[:, r:r + 1, :], tiles)
    c = c + jnp.where((sub & 1) != 0, pltpu.roll(c, 1, 1), 0.0)
    c = c + jnp.where((sub & 2) != 0, jnp.where(sub < 4, bcast(c, 1), bcast(c, 5)), 0.0)
    c = c + jnp.where((sub & 4) != 0, bcast(c, 3), 0.0)
    c = c.reshape(CHUNK, HGRN_DK)
    for level in range(3, LEVELS):
        half = 1 << level
        pieces = []
        for r0 in range(0, CHUNK, 2 * half):
            pieces += [c[r0:r0 + half], c[r0 + half:r0 + 2 * half] + c[r0 + half - 1:r0 + half, :]]
        c = jnp.concatenate(pieces, axis=0)
    return c


def _small_levels(logf, f, q, kk):
    tiles = (CHUNK // 8, 8, HGRN_DK)
    c, f3, q3, k3 = (x.reshape(tiles) for x in (logf, f, q, kk))
    sub = lax.broadcasted_iota(jnp.int32, (1, 8, HGRN_DK), 1)
    bcast = lambda x, r: jnp.broadcast_to(x[:, r:r + 1, :], tiles)
    up = (sub & 1) != 0
    zs = [jnp.where(up, q3 * f3, k3)]
    c = c + jnp.where(up, pltpu.roll(c, 1, 1), 0.0)
    for level, tot in ((1, lambda c: jnp.where(sub < 4, bcast(c, 1), bcast(c, 5))),
                       (2, lambda c: bcast(c, 3))):
        up = (sub & (1 << level)) != 0
        t = tot(c)
        zs.append(jnp.where(up, q3, k3) * jnp.exp2(jnp.where(up, c, t - c)))
        c = c + jnp.where(up, t, 0.0)
    return [z.reshape(CHUNK, HGRN_DK) for z in zs], c.reshape(CHUNK, HGRN_DK)


def _level_large(cum, q, kk, level):
    half = 1 << level
    args, bases, cums = [], [], []
    for r0 in range(0, CHUNK, 2 * half):
        lo, up = slice(r0, r0 + half), slice(r0 + half, r0 + 2 * half)
        tot = cum[r0 + half - 1:r0 + half, :]
        args += [tot - cum[lo], cum[up]]
        bases += [kk[lo], q[up]]
        cums += [cum[lo], cum[up] + tot]
    x = jnp.exp2(jnp.concatenate(args, axis=0))
    return jnp.concatenate(bases, axis=0) * x, jnp.concatenate(cums, axis=0)


def _split3(x):
    hi = x.astype(BF16)
    r = x - hi.astype(F32)
    mid = r.astype(BF16)
    lo = (r - mid.astype(F32)).astype(BF16)
    return hi, mid, lo


def _state_update(s_old, kd, v, blast, pad_rows):
    c = kd.shape[0]
    hi, mid, lo = _split3(jnp.exp2(blast))
    row = lax.broadcasted_iota(jnp.int32, (pad_rows, HGRN_DK), 0)
    dec = jnp.where(row == 0, hi.astype(F32),
                    jnp.where(row == 1, mid.astype(F32), jnp.where(row == 2, lo.astype(F32), 0.0)))
    piece = BF16 if c % 16 == 0 and pad_rows % 16 == 0 else F32
    lhs = jnp.concatenate([kd.astype(piece), dec.astype(piece)], axis=0).astype(BF16)
    rhs = jnp.concatenate(
        [jnp.concatenate([v.astype(piece), jnp.zeros((c, HGRN_DV), piece)], axis=1),
         jnp.concatenate([jnp.zeros((pad_rows, HGRN_DV), piece), jnp.ones((pad_rows, HGRN_DV), piece)], axis=1)],
        axis=0).astype(BF16)
    both = _dot_tn(lhs, rhs)
    return both[:, HGRN_DV:] * s_old + both[:, :HGRN_DV]


def _norm_gate(o, g, nw):
    o = o * lax.rsqrt(jnp.mean(o * o, axis=-1, keepdims=True) + RMS_EPS) * nw
    return o * _silu(g)


def _hgrn_seq_kernel(q_ref, fx_ref, i_ref, g_ref, lb_ref, nw_ref, s0_ref, m_ref,
                     og_ref, sout_ref, s_scr, o_scr, *, n_chunks):
    heads = s_scr.shape[0]

    @pl.when(pl.program_id(2) == 0)
    def _():
        s_scr[...] = s0_ref[0]

    def chunk(c, carry):
        rows = pl.ds(pl.multiple_of(c * CHUNK, CHUNK), CHUNK)
        hcols = [slice(hd * HGRN_DK, (hd + 1) * HGRN_DK) for hd in range(heads)]
        q = [q_ref[rows, cs] for cs in hcols]
        v = [i_ref[rows, cs] for cs in hcols]
        gates = [_gates(fx_ref[rows, cs], lb_ref[:, cs]) for cs in hcols]
        kk = [gt[2] for gt in gates]
        cum = [_prefix_scan(gt[1]) for gt in gates]
        mid = CHUNK // 2 - 1
        spread = None
        for b in cum:
            s_hd = jnp.maximum(b[0:1, :] - b[mid:mid + 1, :], b[mid:mid + 1, :] - b[CHUNK - 1:CHUNK, :])
            spread = s_hd if spread is None else jnp.maximum(spread, s_hd)
        factorable = jnp.max(spread) <= MAX_FACTORED_LOG2_DECAY

        vb = [x.astype(BF16) for x in v]
        for hd in range(heads):
            d = cum[hd] - cum[hd][mid:mid + 1, :]
            qe = (q[hd] * jnp.exp2(d)).astype(BF16)
            ke = (kk[hd] * jnp.exp2(-d)).astype(BF16)
            amat = jnp.where(m_ref[0] > 0.5, _dot_nt(qe, ke), 0.0).astype(BF16)
            blast = cum[hd][CHUNK - 1:CHUNK, :]
            qd = (q[hd] * jnp.exp2(cum[hd])).astype(BF16)
            kd = kk[hd] * jnp.exp2(blast - cum[hd])
            s_old = s_scr[hd]
            base = _dot(qd, s_old.astype(BF16)) + jnp.sum(q[hd] * kk[hd], axis=-1, keepdims=True) * v[hd]
            o_scr[hd] = base
            s_scr[hd] = _state_update(s_old, kd, vb[hd], blast, 16)
            og_ref[rows, hcols[hd]] = _norm_gate(
                base + _dot(amat, vb[hd]), g_ref[rows, hcols[hd]], nw_ref[:, hcols[hd]]).astype(og_ref.dtype)

        @pl.when(jnp.logical_not(factorable))
        def _():
            small = [_small_levels(gt[1], gt[0], q[hd], kk[hd]) for hd, gt in enumerate(gates)]
            zs = [sm[0] for sm in small]
            part = [sm[1] for sm in small]
            for level in range(3, LEVELS):
                for hd in range(heads):
                    z, part[hd] = _level_large(part[hd], q[hd], kk[hd], level)
                    zs[hd].append(z)
            nblk = CHUNK // MASK_ROWS
            a = [[None] * nblk for _ in range(heads)]
            for level in range(LEVELS):
                step = 1 << (level - 4) if level >= 4 else 0
                blocks = [i for i in range(nblk) if level < 4 or (i & step)]
                for hd in range(heads):
                    z = zs[hd][level].astype(BF16)
                    lhs = z if level < 4 else jnp.concatenate(
                        [z[i * MASK_ROWS:(i + 1) * MASK_ROWS] for i in blocks], axis=0)
                    term = _dot_nt(lhs, z)
                    for n, i in enumerate(blocks):
                        t = (term[n * MASK_ROWS:(n + 1) * MASK_ROWS]
                             * m_ref[level + 1, i * MASK_ROWS:(i + 1) * MASK_ROWS, :])
                        a[hd][i] = t if a[hd][i] is None else a[hd][i] + t
            for hd in range(heads):
                amat = jnp.concatenate(a[hd], axis=0).astype(BF16)
                og_ref[rows, hcols[hd]] = _norm_gate(
                    o_scr[hd] + _dot(amat, vb[hd]),
                    g_ref[rows, hcols[hd]], nw_ref[:, hcols[hd]]).astype(og_ref.dtype)
        return carry

    lax.fori_loop(0, n_chunks, chunk, 0)

    @pl.when(pl.program_id(2) == pl.num_programs(2) - 1)
    def _():
        sout_ref[0] = s_scr[...]


def _hgrn_seq(h, lb, nw, s0, *, n_seq, seq_len, row_block0):
    hb = HGRN_SEQ_HEADS
    width = hb * HGRN_DK
    groups = HGRN_HEADS // hb
    rb = min(HGRN_SEQ_ROWS, seq_len)
    nrb = seq_len // rb
    assert seq_len % rb == 0 and rb % CHUNK == 0 and (row_block0 * SEQ_BLOCK) % rb == 0
    rb0 = row_block0 * SEQ_BLOCK // rb
    s0_batched = s0.shape[0] != 1
    col = lambda part: (lambda b, hg, r: (rb0 + b * nrb + r, part * groups + hg))
    masks = jnp.asarray(_level_masks())
    return pl.pallas_call(
        functools.partial(_hgrn_seq_kernel, n_chunks=rb // CHUNK),
        grid=(n_seq, groups, nrb),
        in_specs=[pl.BlockSpec((rb, width), col(0)),
                  pl.BlockSpec((rb, width), col(1)),
                  pl.BlockSpec((rb, width), col(2)),
                  pl.BlockSpec((rb, width), col(3)),
                  pl.BlockSpec((1, width), lambda b, hg, r: (0, hg)),
                  pl.BlockSpec((1, width), lambda b, hg, r: (0, hg)),
                  pl.BlockSpec((1, hb, HGRN_DK, HGRN_DV),
                               (lambda b, hg, r: (b, hg, 0, 0)) if s0_batched else (lambda b, hg, r: (0, hg, 0, 0))),
                  pl.BlockSpec((LEVELS + 1, CHUNK, CHUNK), lambda b, hg, r: (0, 0, 0))],
        out_specs=[pl.BlockSpec((rb, width), lambda b, hg, r: (b * nrb + r, hg)),
                   pl.BlockSpec((1, hb, HGRN_DK, HGRN_DV), lambda b, hg, r: (b, hg, 0, 0))],
        out_shape=[jax.ShapeDtypeStruct((n_seq * seq_len, HGRN_HEADS * HGRN_DV), BF16),
                   jax.ShapeDtypeStruct((n_seq, HGRN_HEADS, HGRN_DK, HGRN_DV), F32)],
        scratch_shapes=[pltpu.VMEM((hb, HGRN_DK, HGRN_DV), F32),
                        pltpu.VMEM((hb, CHUNK, HGRN_DV), F32)],
        compiler_params=pltpu.CompilerParams(
            dimension_semantics=("parallel", "parallel", "arbitrary"), vmem_limit_bytes=VMEM_LIMIT),
        name="hgrn_seq",
    )(h, h, h, h, lb, nw, s0, masks)


def _hgrn_step_kernel(q_ref, fx_ref, i_ref, g_ref, lb_ref, nw_ref, s_ref, og_ref, sout_ref, *, n_seq, t):
    lb = lb_ref[...]
    nw = nw_ref[...]
    tile = (STEP_GROUP, t, HGRN_DK)
    sub = lax.broadcasted_iota(jnp.int32, (1, t, HGRN_DK), 1)

    def group(i, carry):
        rows = pl.ds(pl.multiple_of(i * (STEP_GROUP * t), STEP_GROUP * t), STEP_GROUP * t)
        q = q_ref[rows, :].reshape(tile)
        v = i_ref[rows, :].reshape(tile)
        _, logf, kk = _gates(fx_ref[rows, :].reshape(tile), lb)
        cum = logf
        shift = 1
        while shift < t:
            cum = cum + jnp.where(sub >= shift, pltpu.roll(cum, shift, 1), 0.0)
            shift *= 2
        intra = (jnp.sum(q * kk, axis=-1, keepdims=True)) * v
        for d in range(1, t):
            valid = sub >= d
            x = jnp.exp2(jnp.where(valid, cum - pltpu.roll(cum, d, 1), 0.0))
            w = jnp.sum(jnp.where(valid, q * x * pltpu.roll(kk, d, 1), 0.0), axis=-1, keepdims=True)
            intra = intra + w * pltpu.roll(v, d, 1)
        blast = cum[:, t - 1:t, :]
        qd = q * jnp.exp2(cum)
        kd = kk * jnp.exp2(blast - cum)
        seqs = [i * STEP_GROUP + n for n in range(STEP_GROUP)]
        inter = [_dot(qd[n].astype(BF16), s_ref[b, 0].astype(BF16)) for n, b in enumerate(seqs)]
        for n, b in enumerate(seqs):
            sout_ref[b, 0] = _state_update(s_ref[b, 0], kd[n], v[n], blast[n], 8)
        o = jnp.stack(inter, axis=0) + intra
        og = _norm_gate(o, g_ref[rows, :].reshape(tile), nw)
        og_ref[rows, :] = og.reshape(STEP_GROUP * t, HGRN_DK).astype(og_ref.dtype)
        return carry

    lax.fori_loop(0, n_seq // STEP_GROUP, group, 0)


def _hgrn_step(h, lb, nw, s0, *, n_seq, t, seq_per_block):
    seq_per_block = min(seq_per_block, n_seq)
    assert n_seq % seq_per_block == 0 and seq_per_block % STEP_GROUP == 0 and t == 8
    rows = seq_per_block * t
    col = lambda part: (lambda i, hd: (i, part * HGRN_HEADS + hd))
    return pl.pallas_call(
        functools.partial(_hgrn_step_kernel, n_seq=seq_per_block, t=t),
        grid=(n_seq // seq_per_block, HGRN_HEADS),
        in_specs=[pl.BlockSpec((rows, HGRN_DK), col(0)),
                  pl.BlockSpec((rows, HGRN_DK), col(1)),
                  pl.BlockSpec((rows, HGRN_DV), col(2)),
                  pl.BlockSpec((rows, HGRN_DV), col(3)),
                  pl.BlockSpec((1, HGRN_DK), lambda i, hd: (0, hd)),
                  pl.BlockSpec((1, HGRN_DV), lambda i, hd: (0, hd)),
                  pl.BlockSpec((seq_per_block, 1, HGRN_DK, HGRN_DV), lambda i, hd: (i, hd, 0, 0))],
        out_specs=[pl.BlockSpec((rows, HGRN_DV), lambda i, hd: (i, hd)),
                   pl.BlockSpec((seq_per_block, 1, HGRN_DK, HGRN_DV), lambda i, hd: (i, hd, 0, 0))],
        out_shape=[jax.ShapeDtypeStruct((n_seq * t, HGRN_HEADS * HGRN_DV), BF16),
                   jax.ShapeDtypeStruct((n_seq, HGRN_HEADS, HGRN_DK, HGRN_DV), F32)],
        compiler_params=pltpu.CompilerParams(
            dimension_semantics=("parallel", "parallel"), vmem_limit_bytes=VMEM_LIMIT),
        name="hgrn_step",
    )(h, h, h, h, lb, nw, s0)


def _attn_seq_kernel(q_ref, g_ref, kc_ref, vc_ref, kp_ref, vp_ref, meta_ref, sink_ref, og_ref):
    first = pl.program_id(1) == 0
    kprev = jnp.where(first, meta_ref[:, :SWA_KV_WIDTH], kp_ref[...])
    vprev = jnp.where(first, meta_ref[:, SWA_KV_WIDTH:], vp_ref[...])
    kband = (jnp.concatenate([kprev, kc_ref[...]], axis=0) * (SWA_SCALE * LOG2E)).astype(BF16)
    vband_t = jnp.concatenate([vprev, vc_ref[...]], axis=0).T.astype(BF16)
    zeros_t = jnp.zeros((SWA_HEAD_DIM, 2 * SEQ_BLOCK), BF16)
    kj = lax.broadcasted_iota(jnp.int32, (2 * SEQ_BLOCK, SEQ_BLOCK), 0)
    qi = lax.broadcasted_iota(jnp.int32, (2 * SEQ_BLOCK, SEQ_BLOCK), 1)
    dist = SEQ_BLOCK + qi - kj
    valid = (dist >= 0) & (dist < WINDOW) & (jnp.logical_not(first) | (kj >= SEQ_BLOCK - N_META))
    madd = jnp.where(valid, 0.0, -jnp.inf)
    zeros = jnp.zeros((2 * SEQ_BLOCK, SWA_HEAD_DIM), BF16)
    pairs = SWA_GROUP // 2
    pw = 2 * SWA_HEAD_DIM

    def block_diag(x):
        return jnp.concatenate([jnp.concatenate([x, zeros], axis=1),
                                jnp.concatenate([zeros, x], axis=1)], axis=0)

    sts = []
    for kvh in range(SWA_KV_HEADS):
        ks = slice(kvh * SWA_HEAD_DIM, (kvh + 1) * SWA_HEAD_DIM)
        col0 = kvh * SWA_GROUP * SWA_HEAD_DIM
        xq = jnp.concatenate([q_ref[:, col0 + p * pw:col0 + (p + 1) * pw].astype(BF16)
                              for p in range(pairs)], axis=0)
        sts.append(_dot_nt(block_diag(kband[:, ks]), xq))
    for kvh in range(SWA_KV_HEADS):
        ks = slice(kvh * SWA_HEAD_DIM, (kvh + 1) * SWA_HEAD_DIM)
        col0 = kvh * SWA_GROUP * SWA_HEAD_DIM
        vt = vband_t[ks, :]
        v2t = jnp.concatenate([jnp.concatenate([vt, zeros_t], axis=1),
                               jnp.concatenate([zeros_t, vt], axis=1)], axis=0)
        pbs, rdens = [], []
        for pr in range(pairs):
            head = kvh * SWA_GROUP + 2 * pr
            s2 = (sts[kvh][:, pr * SEQ_BLOCK:(pr + 1) * SEQ_BLOCK].reshape(2, 2 * SEQ_BLOCK, SEQ_BLOCK)
                  + madd[None])
            sink = jnp.concatenate([jnp.full((1, 1, SEQ_BLOCK), sink_ref[head] * LOG2E, F32),
                                    jnp.full((1, 1, SEQ_BLOCK), sink_ref[head + 1] * LOG2E, F32)], axis=0)
            m = jnp.maximum(jnp.max(s2, axis=1, keepdims=True), sink)
            p = jnp.exp2(s2 - m)
            den = jnp.sum(p, axis=1, keepdims=True) + jnp.exp2(sink - m)
            pbs.append(p.astype(BF16).reshape(4 * SEQ_BLOCK, SEQ_BLOCK))
            rdens.append(1.0 / den)
        for pr in range(pairs):
            ot = _dot(v2t, pbs[pr]).reshape(2, SWA_HEAD_DIM, SEQ_BLOCK) * rdens[pr]
            o = ot.reshape(2 * SWA_HEAD_DIM, SEQ_BLOCK).T
            cs = slice(col0 + pr * pw, col0 + (pr + 1) * pw)
            og_ref[:, cs] = (o * _silu(g_ref[:, cs])).astype(og_ref.dtype)


def _attn_seq(h1, h1_meta, sinks, *, n_seq, seq_len, meta_row_block):
    nb = seq_len // SEQ_BLOCK
    width = SWA_Q_HEADS * SWA_HEAD_DIM
    kcol = 2 * width // SWA_KV_WIDTH
    return pl.pallas_call(
        _attn_seq_kernel,
        grid=(n_seq, nb),
        in_specs=[pl.BlockSpec((SEQ_BLOCK, width), lambda b, j: (b * nb + j, 0)),
                  pl.BlockSpec((SEQ_BLOCK, width), lambda b, j: (b * nb + j, 1)),
                  pl.BlockSpec((SEQ_BLOCK, SWA_KV_WIDTH), lambda b, j: (b * nb + j, kcol)),
                  pl.BlockSpec((SEQ_BLOCK, SWA_KV_WIDTH), lambda b, j: (b * nb + j, kcol + 1)),
                  pl.BlockSpec((SEQ_BLOCK, SWA_KV_WIDTH), lambda b, j: (b * nb + jnp.maximum(j - 1, 0), kcol)),
                  pl.BlockSpec((SEQ_BLOCK, SWA_KV_WIDTH), lambda b, j: (b * nb + jnp.maximum(j - 1, 0), kcol + 1)),
                  pl.BlockSpec((SEQ_BLOCK, 2 * SWA_KV_WIDTH), lambda b, j: (meta_row_block, kcol // 2)),
                  pl.BlockSpec(memory_space=pltpu.SMEM)],
        out_specs=pl.BlockSpec((SEQ_BLOCK, width), lambda b, j: (b * nb + j, 0)),
        out_shape=jax.ShapeDtypeStruct((n_seq * seq_len, width), BF16),
        compiler_params=pltpu.CompilerParams(
            dimension_semantics=("parallel", "arbitrary"), vmem_limit_bytes=VMEM_LIMIT),
        name="attn_seq",
    )(h1, h1, h1, h1, h1, h1, h1_meta, sinks)


def _attn_step_kernel(q_ref, g_ref, kn_ref, vn_ref, ck_ref, cv_ref, sink_ref, og_ref, nk_ref, nv_ref, *, n_seq, t):
    keys = WINDOW + t
    hd = SWA_HEAD_DIM
    tiles = SWA_Q_HEADS // 2
    kj = lax.broadcasted_iota(jnp.int32, (keys, SWA_Q_HEADS * t), 0)
    qt = lax.broadcasted_iota(jnp.int32, (keys, SWA_Q_HEADS * t), 1) % t
    madd = jnp.where((kj >= qt + 1) & (kj <= WINDOW + qt), 0.0, -jnp.inf)
    low = lax.broadcasted_iota(jnp.int32, (t, 2 * hd), 1) < hd
    zero_tile = jnp.zeros((t, 2 * hd), F32)
    sink = sink_ref[...] * LOG2E

    def group(i, carry):
        seqs = [i * SEQ_UNROLL + n for n in range(SEQ_UNROLL)]
        rows = [pl.ds(pl.multiple_of(b * t, t), t) for b in seqs]
        st, vall = [], []
        for b, rw in zip(seqs, rows):
            kc, vc = ck_ref[b], cv_ref[b]
            kn, vn = kn_ref[rw, :], vn_ref[rw, :]
            nk_ref[b, 0:WINDOW - t, :] = kc[t:, :]
            nk_ref[b, WINDOW - t:WINDOW, :] = kn
            nv_ref[b, 0:WINDOW - t, :] = vc[t:, :]
            nv_ref[b, WINDOW - t:WINDOW, :] = vn
            kall = jnp.concatenate([kc, kn], axis=0).astype(BF16)
            vall.append(jnp.concatenate([vc, vn], axis=0).astype(BF16))
            q = q_ref[rw, :] * (SWA_SCALE * LOG2E)
            qtile = [q[:, j * 2 * hd:(j + 1) * 2 * hd] for j in range(tiles)]
            qswap = [pltpu.roll(x, hd, 1) for x in qtile]
            groups = []
            for kvh in range(SWA_KV_HEADS):
                for gq in range(SWA_GROUP):
                    j = (kvh * SWA_GROUP + gq) // 2
                    src = qtile[j] if gq % 2 == kvh % 2 else qswap[j]
                    half = jnp.where(low, src, 0.0) if kvh % 2 == 0 else jnp.where(low, 0.0, src)
                    groups.append(jnp.concatenate(
                        [half if c == kvh // 2 else zero_tile for c in range(SWA_KV_HEADS // 2)], axis=1))
            qbd = jnp.concatenate(groups, axis=0).astype(BF16)
            st.append(_dot_nt(kall, qbd))
        pn = []
        for s in st:
            s = s + madd
            m = jnp.maximum(jnp.max(s, axis=0, keepdims=True), sink)
            p = jnp.exp2(s - m)
            den = jnp.sum(p, axis=0, keepdims=True) + jnp.exp2(sink - m)
            pn.append((p * (1.0 / den)).astype(BF16))
        of = [_dot_tn(p, vl) for p, vl in zip(pn, vall)]
        for o, rw in zip(of, rows):
            g = g_ref[rw, :]
            out = []
            for j in range(tiles):
                kvh = (2 * j) // SWA_GROUP
                ct = slice((kvh // 2) * 2 * hd, (kvh // 2 + 1) * 2 * hd)
                ra = o[(2 * j) * t:(2 * j + 1) * t, ct]
                rb = o[(2 * j + 1) * t:(2 * j + 2) * t, ct]
                if kvh % 2 == 0:
                    out.append(jnp.where(low, ra, pltpu.roll(rb, hd, 1)))
                else:
                    out.append(jnp.where(low, pltpu.roll(ra, hd, 1), rb))
            og_ref[rw, :] = (jnp.concatenate(out, axis=1) * _silu(g)).astype(og_ref.dtype)
        return carry

    lax.fori_loop(0, n_seq // SEQ_UNROLL, group, 0)


def _attn_step(h1, cache_k, cache_v, sink_cols, *, n_seq, t, seq_per_block):
    assert n_seq % seq_per_block == 0
    rows = seq_per_block * t
    width = SWA_Q_HEADS * SWA_HEAD_DIM
    kcol = 2 * width // SWA_KV_WIDTH
    cache_spec = pl.BlockSpec((seq_per_block, WINDOW, SWA_KV_WIDTH), lambda i: (i, 0, 0))
    return pl.pallas_call(
        functools.partial(_attn_step_kernel, n_seq=seq_per_block, t=t),
        grid=(n_seq // seq_per_block,),
        in_specs=[pl.BlockSpec((rows, width), lambda i: (i, 0)),
                  pl.BlockSpec((rows, width), lambda i: (i, 1)),
                  pl.BlockSpec((rows, SWA_KV_WIDTH), lambda i: (i, kcol)),
                  pl.BlockSpec((rows, SWA_KV_WIDTH), lambda i: (i, kcol + 1)),
                  cache_spec, cache_spec,
                  pl.BlockSpec((1, SWA_Q_HEADS * t), lambda i: (0, 0))],
        out_specs=[pl.BlockSpec((rows, width), lambda i: (i, 0)), cache_spec, cache_spec],
        out_shape=[jax.ShapeDtypeStruct((n_seq * t, width), BF16),
                   jax.ShapeDtypeStruct(cache_k.shape, F32),
                   jax.ShapeDtypeStruct(cache_v.shape, F32)],
        compiler_params=pltpu.CompilerParams(
            dimension_semantics=("parallel",), vmem_limit_bytes=VMEM_LIMIT),
        name="attn_step",
    )(h1, h1, h1, h1, cache_k, cache_v, sink_cols)


def kernel(x_prompt, x_sample, state_hgrn, cache_swa_k, cache_swa_v, meta_tokens,
           hgrn_w_in, hgrn_lb_logits, hgrn_norm_w, hgrn_w_out,
           swa_w_in, swa_sinks, swa_w_out, ln_g, ln_b):
    out_dtype = x_prompt.dtype
    bsz, seq, d = x_prompt.shape
    dec_b, dec_t, _ = x_sample.shape
    n_p = bsz * seq
    n_s = dec_b * dec_t
    width = SWA_Q_HEADS * SWA_HEAD_DIM

    w_out0 = hgrn_w_out[0].astype(BF16)
    w_out1 = swa_w_out[0].astype(BF16)
    in_tile = SHORT_COL_TILE
    q_tiles = width // in_tile
    kv_tile = 2 * SWA_KV_WIDTH // in_tile
    assert width % in_tile == 0 and (2 * SWA_KV_WIDTH) % in_tile == 0 and kv_tile == 1
    regroup = lambda j: jnp.where(j < q_tiles, j, jnp.where(j < 2 * q_tiles, j + kv_tile, q_tiles))
    lb = jnp.cumsum(jax.nn.softmax(hgrn_lb_logits.astype(F32), axis=0), axis=0)[0:1]
    nw = hgrn_norm_w[0].astype(F32).reshape(1, -1)
    sinks = swa_sinks[0].astype(F32).reshape(1, SWA_Q_HEADS)
    sink_cols = jnp.repeat(sinks, dec_t, axis=1)
    g0, b0 = ln_g[0:1].astype(F32), ln_b[0:1].astype(F32)
    g1, b1 = ln_g[1:2].astype(F32), ln_b[1:2].astype(F32)

    x_p = x_prompt.astype(F32).reshape(n_p, d)
    meta_block = jnp.concatenate(
        [jnp.zeros((SEQ_BLOCK - N_META, d), F32), meta_tokens.astype(F32)], axis=0)
    x_s = x_sample.astype(F32).reshape(n_s, d)
    meta_blk = n_s // SEQ_BLOCK

    h_sm, w_in0 = _matmul_cast(x_s, meta_block, hgrn_w_in[0].astype(F32), xs_rows=n_s, xm_block=0, tn=SHORT_IN0_COL_TILE)
    h_p = _matmul(x_p, w_in0, tm=PROJ_ROW_TILE, tn=PROJ_IN0_COL_TILE)
    zero_state = jnp.zeros((1, HGRN_HEADS, HGRN_DK, HGRN_DV), F32)
    og_meta, s_meta = _hgrn_seq(h_sm, lb, nw, zero_state, n_seq=1, seq_len=SEQ_BLOCK, row_block0=meta_blk)
    og_s, st_s = _hgrn_step(h_sm, lb, nw, state_hgrn[0].astype(F32), n_seq=dec_b, t=dec_t, seq_per_block=STEP_STATE_SEQS)
    og_p, st_p = _hgrn_seq(h_p, lb, nw, s_meta, n_seq=bsz, seq_len=seq, row_block0=0)
    x1_sm = _matmul_deepnorm_tail(og_s, og_meta, w_out0, x_s, meta_block, g0, b0)
    x1_p = _matmul_deepnorm(og_p, w_out0, x_p, g0, b0, tm=PROJ_OUT_ROW_TILE)

    h1_sm, w_in1 = _matmul_cast(x1_sm, x1_sm, swa_w_in[0].astype(F32), xs_rows=n_s, xm_block=meta_blk,
                                tn=in_tile, col_tile=regroup)
    h1_p = _matmul(x1_p, w_in1, tm=PROJ_ROW_TILE, tn=PROJ_IN1_COL_TILE)
    og1_p = _attn_seq(h1_p, h1_sm, sinks.reshape(-1), n_seq=bsz, seq_len=seq, meta_row_block=meta_blk)
    ck = cache_swa_k[0].astype(F32).reshape(dec_b, WINDOW, SWA_KV_WIDTH)
    cv = cache_swa_v[0].astype(F32).reshape(dec_b, WINDOW, SWA_KV_WIDTH)
    og1_s, nk_s, nv_s = _attn_step(h1_sm, ck, cv, sink_cols, n_seq=dec_b, t=dec_t, seq_per_block=STEP_CACHE_SEQS)
    y_p = _matmul_deepnorm(og1_p, w_out1, x1_p, g1, b1, tm=PROJ_OUT_ROW_TILE)
    y_s = _matmul_deepnorm(og1_s, w_out1, x1_sm, g1, b1, tm=PROJ_OUT_ROW_TILE)

    kv_p = h1_p.reshape(bsz, seq, -1)[:, seq - WINDOW:, 2 * width:]
    cache_shape = (1, bsz, WINDOW, SWA_KV_HEADS, SWA_HEAD_DIM)
    return (y_p.reshape(bsz, seq, d).astype(out_dtype),
            y_s.reshape(dec_b, dec_t, d).astype(out_dtype),
            st_p[None].astype(out_dtype),
            st_s[None].astype(out_dtype),
            kv_p[..., :SWA_KV_WIDTH].reshape(cache_shape).astype(out_dtype),
            kv_p[..., SWA_KV_WIDTH:].reshape(cache_shape).astype(out_dtype),
            nk_s.reshape((1,) + cache_swa_k.shape[1:]).astype(out_dtype),
            nv_s.reshape((1,) + cache_swa_v.shape[1:]).astype(out_dtype))
```

```python
import functools

import numpy as np
import jax
import jax.numpy as jnp
from jax import lax
from jax.experimental import pallas as pl
from jax.experimental.pallas import tpu as pltpu

F32 = jnp.float32
BF16 = jnp.bfloat16

D_MODEL = 2048
N_META = 16
DEPTH = 2
HGRN_HEADS = 16
HGRN_DK = 128
HGRN_DV = 128
SWA_Q_HEADS = 32
SWA_KV_HEADS = 4
SWA_GROUP = 8
SWA_HEAD_DIM = 64
SWA_KV_WIDTH = SWA_KV_HEADS * SWA_HEAD_DIM
SWA_SCALE = SWA_HEAD_DIM ** -0.5
WINDOW = 128
DEEPNORM_ALPHA = (2.0 * DEPTH) ** 0.25
LN_EPS = 1e-5
RMS_EPS = 1e-6
LOG2E = 1.4426950408889634

CHUNK = 128
LEVELS = 7
MAX_FACTORED_LOG2_DECAY = 80.0
MASK_ROWS = 16
SEQ_BLOCK = 128
LN_SUB_ROWS = 128
STEP_GROUP = 32
SEQ_UNROLL = 8
HGRN_SEQ_HEADS = 16
HGRN_SEQ_ROWS = 256

V7X_VMEM_BYTES = 64 * 1024 * 1024
VMEM_LIMIT = V7X_VMEM_BYTES - 4 * 1024 * 1024
PROJ_ROW_TILE = 1024
PROJ_IN0_COL_TILE = 2048
PROJ_IN1_COL_TILE = 2304
PROJ_OUT_ROW_TILE = 512
SHORT_COL_TILE = 512
SHORT_IN0_COL_TILE = 1024
STEP_STATE_SEQS = 64
STEP_HEADS = 2
STEP_CACHE_SEQS = 32


def _dot(a, b):
    return jnp.dot(a, b, preferred_element_type=F32)


def _dot_nt(a, b):
    return lax.dot_general(a, b, (((1,), (1,)), ((), ())), preferred_element_type=F32)


def _dot_tn(a, b):
    return lax.dot_general(a, b, (((0,), (0,)), ((), ())), preferred_element_type=F32)


def _silu(x):
    return x * jax.nn.sigmoid(x)


def _mm_kernel(x_ref, w_ref, o_ref):
    o_ref[...] = _dot(x_ref[...].astype(BF16), w_ref[...])


def _row_tile(n, want):
    if n <= want:
        return n
    return max(t for t in range(16, want + 1, 16) if n % t == 0)


def _matmul(x, w, *, tm, tn):
    n, k = x.shape
    e = w.shape[1]
    tm = _row_tile(n, tm)
    assert n % tm == 0 and e % tn == 0
    return pl.pallas_call(
        _mm_kernel,
        grid=(n // tm, e // tn),
        in_specs=[pl.BlockSpec((tm, k), lambda i, j: (i, 0)),
                  pl.BlockSpec((k, tn), lambda i, j: (0, j))],
        out_specs=pl.BlockSpec((tm, tn), lambda i, j: (i, j)),
        out_shape=jax.ShapeDtypeStruct((n, e), F32),
        compiler_params=pltpu.CompilerParams(
            dimension_semantics=("parallel", "parallel"), vmem_limit_bytes=VMEM_LIMIT),
        name="proj_in",
    )(x, w)


def _mm_cast_kernel(xs_ref, xm_ref, w_ref, o_ref, wb_ref, xb_scr):
    @pl.when(pl.program_id(0) == 0)
    def _():
        n_s = xs_ref.shape[0]
        xb_scr[0:n_s, :] = xs_ref[...].astype(BF16)
        xb_scr[n_s:, :] = xm_ref[...].astype(BF16)

    wb = w_ref[...].astype(BF16)
    wb_ref[...] = wb
    o_ref[...] = _dot(xb_scr[...], wb)


def _matmul_cast(xs, xm, w, *, xs_rows, xm_block, tn, col_tile=lambda j: j):
    k, e = w.shape
    n = xs_rows + SEQ_BLOCK
    assert e % tn == 0
    return pl.pallas_call(
        _mm_cast_kernel,
        grid=(e // tn,),
        in_specs=[pl.BlockSpec((xs_rows, k), lambda j: (0, 0), pipeline_mode=pl.Buffered(1)),
                  pl.BlockSpec((SEQ_BLOCK, k), lambda j: (xm_block, 0), pipeline_mode=pl.Buffered(1)),
                  pl.BlockSpec((k, tn), lambda j: (0, col_tile(j)))],
        out_specs=[pl.BlockSpec((n, tn), lambda j: (0, j)),
                   pl.BlockSpec((k, tn), lambda j: (0, j))],
        out_shape=[jax.ShapeDtypeStruct((n, e), F32), jax.ShapeDtypeStruct((k, e), BF16)],
        scratch_shapes=[pltpu.VMEM((n, k), BF16)],
        compiler_params=pltpu.CompilerParams(
            dimension_semantics=("arbitrary",), vmem_limit_bytes=VMEM_LIMIT),
        name="proj_in_cast",
    )(xs, xm, w)


def _mm_ln_tail_kernel(a_ref, at_ref, w_ref, x_ref, xt_ref, g_ref, b_ref, o_ref):
    tail = pl.program_id(0) == pl.num_programs(0) - 1
    a = jnp.where(tail, at_ref[...], a_ref[...])
    x = jnp.where(tail, xt_ref[...], x_ref[...])
    z = DEEPNORM_ALPHA * x + _dot(a, w_ref[...])
    mu = jnp.mean(z, axis=-1, keepdims=True)
    zc = z - mu
    var = jnp.mean(zc * zc, axis=-1, keepdims=True)
    o_ref[...] = zc * lax.rsqrt(var + LN_EPS) * g_ref[...] + b_ref[...]


def _matmul_deepnorm_tail(a, a_tail, w, x, x_tail, g, b):
    n, k = a.shape
    d = w.shape[1]
    tm = SEQ_BLOCK
    last = n // tm - 1
    body = lambda i: (jnp.minimum(i, last), 0)
    const = lambda i: (0, 0)
    return pl.pallas_call(
        _mm_ln_tail_kernel,
        grid=(n // tm + 1,),
        in_specs=[pl.BlockSpec((tm, k), body), pl.BlockSpec((tm, k), const),
                  pl.BlockSpec((k, d), const),
                  pl.BlockSpec((tm, d), body), pl.BlockSpec((tm, d), const),
                  pl.BlockSpec((1, d), const), pl.BlockSpec((1, d), const)],
        out_specs=pl.BlockSpec((tm, d), lambda i: (i, 0)),
        out_shape=jax.ShapeDtypeStruct((n + tm, d), F32),
        compiler_params=pltpu.CompilerParams(
            dimension_semantics=("parallel",), vmem_limit_bytes=VMEM_LIMIT),
        name="proj_out_deepnorm_tail",
    )(a, a_tail, w, x, x_tail, g, b)


def _mm_ln_kernel(a_ref, w_ref, x_ref, g_ref, b_ref, o_ref):
    tm = a_ref.shape[0]
    sub = LN_SUB_ROWS if tm % LN_SUB_ROWS == 0 else tm
    for r0 in range(0, tm, sub):
        rows = slice(r0, r0 + sub)
        z = DEEPNORM_ALPHA * x_ref[rows, :] + _dot(a_ref[rows, :], w_ref[...])
        mu = jnp.mean(z, axis=-1, keepdims=True)
        zc = z - mu
        var = jnp.mean(zc * zc, axis=-1, keepdims=True)
        o_ref[rows, :] = zc * lax.rsqrt(var + LN_EPS) * g_ref[...] + b_ref[...]


def _matmul_deepnorm(a, w, x, g, b, *, tm):
    n, k = a.shape
    d = w.shape[1]
    tm = _row_tile(n, tm)
    return pl.pallas_call(
        _mm_ln_kernel,
        grid=(n // tm,),
        in_specs=[pl.BlockSpec((tm, k), lambda i: (i, 0)),
                  pl.BlockSpec((k, d), lambda i: (0, 0)),
                  pl.BlockSpec((tm, d), lambda i: (i, 0)),
                  pl.BlockSpec((1, d), lambda i: (0, 0)),
                  pl.BlockSpec((1, d), lambda i: (0, 0))],
        out_specs=pl.BlockSpec((tm, d), lambda i: (i, 0)),
        out_shape=jax.ShapeDtypeStruct((n, d), F32),
        compiler_params=pltpu.CompilerParams(
            dimension_semantics=("parallel",), vmem_limit_bytes=VMEM_LIMIT),
        name="proj_out_deepnorm",
    )(a, w, x, g, b)


def _level_masks():
    t = np.arange(CHUNK)
    out = np.zeros((LEVELS + 1, CHUNK, CHUNK), np.float32)
    out[0] = t[:, None] > t[None, :]
    for l in range(LEVELS):
        h = 1 << l
        same = (t[:, None] >> (l + 1)) == (t[None, :] >> (l + 1))
        out[l + 1] = same & ((t[:, None] & h) != 0) & ((t[None, :] & h) == 0)
    return out


def _gates(fx, lb):
    f = lb + (1.0 - lb) * jax.nn.sigmoid(fx)
    return f, jnp.log2(f), 1.0 - f


def _prefix_scan(logf):
    tiles = (CHUNK // 8, 8, HGRN_DK)
    c = logf.reshape(tiles)
    sub = lax.broadcasted_iota(jnp.int32, (1, 8, HGRN_DK), 1)
    bcast = lambda x, r: jnp.broadcast_to(x[:, r:r + 1, :], tiles)
    c = c + jnp.where((sub & 1) != 0, pltpu.roll(c, 1, 1), 0.0)
    c = c + jnp.where((sub & 2) != 0, jnp.where(sub < 4, bcast(c, 1), bcast(c, 5)), 0.0)
    c = c + jnp.where((sub & 4) != 0, bcast(c, 3), 0.0)
    c = c.reshape(CHUNK, HGRN_DK)
    for level in range(3, LEVELS):
        half = 1 << level
        pieces = []
        for r0 in range(0, CHUNK, 2 * half):
            pieces += [c[r0:r0 + half], c[r0 + half:r0 + 2 * half] + c[r0 + half - 1:r0 + half, :]]
        c = jnp.concatenate(pieces, axis=0)
    return c


def _small_levels(logf, f, q, kk):
    tiles = (CHUNK // 8, 8, HGRN_DK)
    c, f3, q3, k3 = (x.reshape(tiles) for x in (logf, f, q, kk))
    sub = lax.broadcasted_iota(jnp.int32, (1, 8, HGRN_DK), 1)
    bcast = lambda x, r: jnp.broadcast_to(x[:, r:r + 1, :], tiles)
    up = (sub & 1) != 0
    zs = [jnp.where(up, q3 * f3, k3)]
    c = c + jnp.where(up, pltpu.roll(c, 1, 1), 0.0)
    for level, tot in ((1, lambda c: jnp.where(sub < 4, bcast(c, 1), bcast(c, 5))),
                       (2, lambda c: bcast(c, 3))):
        up = (sub & (1 << level)) != 0
        t = tot(c)
        zs.append(jnp.where(up, q3, k3) * jnp.exp2(jnp.where(up, c, t - c)))
        c = c + jnp.where(up, t, 0.0)
    return [z.reshape(CHUNK, HGRN_DK) for z in zs], c.reshape(CHUNK, HGRN_DK)


def _level_large(cum, q, kk, level):
    half = 1 << level
    args, bases, cums = [], [], []
    for r0 in range(0, CHUNK, 2 * half):
        lo, up = slice(r0, r0 + half), slice(r0 + half, r0 + 2 * half)
        tot = cum[r0 + half - 1:r0 + half, :]
        args += [tot - cum[lo], cum[up]]
        bases += [kk[lo], q[up]]
        cums += [cum[lo], cum[up] + tot]
    x = jnp.exp2(jnp.concatenate(args, axis=0))
    return jnp.concatenate(bases, axis=0) * x, jnp.concatenate(cums, axis=0)


def _split3(x):
    hi = x.astype(BF16)
    r = x - hi.astype(F32)
    mid = r.astype(BF16)
    lo = (r - mid.astype(F32)).astype(BF16)
    return hi, mid, lo


def _state_update(s_old, kd, v, blast, pad_rows):
    c = kd.shape[0]
    hi, mid, lo = _split3(jnp.exp2(blast))
    row = lax.broadcasted_iota(jnp.int32, (pad_rows, HGRN_DK), 0)
    dec = jnp.where(row == 0, hi.astype(F32),
                    jnp.where(row == 1, mid.astype(F32), jnp.where(row == 2, lo.astype(F32), 0.0)))
    piece = BF16 if c % 16 == 0 and pad_rows % 16 == 0 else F32
    lhs = jnp.concatenate([kd.astype(piece), dec.astype(piece)], axis=0).astype(BF16)
    rhs = jnp.concatenate(
        [jnp.concatenate([v.astype(piece), jnp.zeros((c, HGRN_DV), piece)], axis=1),
         jnp.concatenate([jnp.zeros((pad_rows, HGRN_DV), piece), jnp.ones((pad_rows, HGRN_DV), piece)], axis=1)],
        axis=0).astype(BF16)
    both = _dot_tn(lhs, rhs)
    return both[:, HGRN_DV:] * s_old + both[:, :HGRN_DV]


def _norm_gate(o, g, nw):
    o = o * lax.rsqrt(jnp.mean(o * o, axis=-1, keepdims=True) + RMS_EPS) * nw
    return o * _silu(g)


def _hgrn_seq_kernel(q_ref, fx_ref, i_ref, g_ref, lb_ref, nw_ref, s0_ref, m_ref,
                     og_ref, sout_ref, s_scr, o_scr, *, n_chunks):
    heads = s_scr.shape[0]

    @pl.when(pl.program_id(2) == 0)
    def _():
        s_scr[...] = s0_ref[0]

    def chunk(c, carry):
        rows = pl.ds(pl.multiple_of(c * CHUNK, CHUNK), CHUNK)
        hcols = [slice(hd * HGRN_DK, (hd + 1) * HGRN_DK) for hd in range(heads)]
        q = [q_ref[rows, cs] for cs in hcols]
        v = [i_ref[rows, cs] for cs in hcols]
        gates = [_gates(fx_ref[rows, cs], lb_ref[:, cs]) for cs in hcols]
        kk = [gt[2] for gt in gates]
        cum = [_prefix_scan(gt[1]) for gt in gates]
        mid = CHUNK // 2 - 1
        spread = None
        for b in cum:
            s_hd = jnp.maximum(b[0:1, :] - b[mid:mid + 1, :], b[mid:mid + 1, :] - b[CHUNK - 1:CHUNK, :])
            spread = s_hd if spread is None else jnp.maximum(spread, s_hd)
        factorable = jnp.max(spread) <= MAX_FACTORED_LOG2_DECAY

        vb = [x.astype(BF16) for x in v]
        for hd in range(heads):
            d = cum[hd] - cum[hd][mid:mid + 1, :]
            qe = (q[hd] * jnp.exp2(d)).astype(BF16)
            ke = (kk[hd] * jnp.exp2(-d)).astype(BF16)
            amat = jnp.where(m_ref[0] > 0.5, _dot_nt(qe, ke), 0.0).astype(BF16)
            blast = cum[hd][CHUNK - 1:CHUNK, :]
            qd = (q[hd] * jnp.exp2(cum[hd])).astype(BF16)
            kd = kk[hd] * jnp.exp2(blast - cum[hd])
            s_old = s_scr[hd]
            base = _dot(qd, s_old.astype(BF16)) + jnp.sum(q[hd] * kk[hd], axis=-1, keepdims=True) * v[hd]
            o_scr[hd] = base
            s_scr[hd] = _state_update(s_old, kd, vb[hd], blast, 16)
            og_ref[rows, hcols[hd]] = _norm_gate(
                base + _dot(amat, vb[hd]), g_ref[rows, hcols[hd]], nw_ref[:, hcols[hd]]).astype(og_ref.dtype)

        @pl.when(jnp.logical_not(factorable))
        def _():
            small = [_small_levels(gt[1], gt[0], q[hd], kk[hd]) for hd, gt in enumerate(gates)]
            zs = [sm[0] for sm in small]
            part = [sm[1] for sm in small]
            for level in range(3, LEVELS):
                for hd in range(heads):
                    z, part[hd] = _level_large(part[hd], q[hd], kk[hd], level)
                    zs[hd].append(z)
            nblk = CHUNK // MASK_ROWS
            a = [[None] * nblk for _ in range(heads)]
            for level in range(LEVELS):
                step = 1 << (level - 4) if level >= 4 else 0
                blocks = [i for i in range(nblk) if level < 4 or (i & step)]
                for hd in range(heads):
                    z = zs[hd][level].astype(BF16)
                    lhs = z if level < 4 else jnp.concatenate(
                        [z[i * MASK_ROWS:(i + 1) * MASK_ROWS] for i in blocks], axis=0)
                    term = _dot_nt(lhs, z)
                    for n, i in enumerate(blocks):
                        t = (term[n * MASK_ROWS:(n + 1) * MASK_ROWS]
                             * m_ref[level + 1, i * MASK_ROWS:(i + 1) * MASK_ROWS, :])
                        a[hd][i] = t if a[hd][i] is None else a[hd][i] + t
            for hd in range(heads):
                amat = jnp.concatenate(a[hd], axis=0).astype(BF16)
                og_ref[rows, hcols[hd]] = _norm_gate(
                    o_scr[hd] + _dot(amat, vb[hd]),
                    g_ref[rows, hcols[hd]], nw_ref[:, hcols[hd]]).astype(og_ref.dtype)
        return carry

    lax.fori_loop(0, n_chunks, chunk, 0)

    @pl.when(pl.program_id(2) == pl.num_programs(2) - 1)
    def _():
        sout_ref[0] = s_scr[...]


def _hgrn_seq(h, lb, nw, s0, *, n_seq, seq_len, row_block0):
    hb = HGRN_SEQ_HEADS
    width = hb * HGRN_DK
    groups = HGRN_HEADS // hb
    rb = min(HGRN_SEQ_ROWS, seq_len)
    nrb = seq_len // rb
    assert seq_len % rb == 0 and rb % CHUNK == 0 and (row_block0 * SEQ_BLOCK) % rb == 0
    rb0 = row_block0 * SEQ_BLOCK // rb
    s0_batched = s0.shape[0] != 1
    col = lambda part: (lambda b, hg, r: (rb0 + b * nrb + r, part * groups + hg))
    masks = jnp.asarray(_level_masks())
    return pl.pallas_call(
        functools.partial(_hgrn_seq_kernel, n_chunks=rb // CHUNK),
        grid=(n_seq, groups, nrb),
        in_specs=[pl.BlockSpec((rb, width), col(0)),
                  pl.BlockSpec((rb, width), col(1)),
                  pl.BlockSpec((rb, width), col(2)),
                  pl.BlockSpec((rb, width), col(3)),
                  pl.BlockSpec((1, width), lambda b, hg, r: (0, hg)),
                  pl.BlockSpec((1, width), lambda b, hg, r: (0, hg)),
                  pl.BlockSpec((1, hb, HGRN_DK, HGRN_DV),
                               (lambda b, hg, r: (b, hg, 0, 0)) if s0_batched else (lambda b, hg, r: (0, hg, 0, 0))),
                  pl.BlockSpec((LEVELS + 1, CHUNK, CHUNK), lambda b, hg, r: (0, 0, 0))],
        out_specs=[pl.BlockSpec((rb, width), lambda b, hg, r: (b * nrb + r, hg)),
                   pl.BlockSpec((1, hb, HGRN_DK, HGRN_DV), lambda b, hg, r: (b, hg, 0, 0))],
        out_shape=[jax.ShapeDtypeStruct((n_seq * seq_len, HGRN_HEADS * HGRN_DV), BF16),
                   jax.ShapeDtypeStruct((n_seq, HGRN_HEADS, HGRN_DK, HGRN_DV), F32)],
        scratch_shapes=[pltpu.VMEM((hb, HGRN_DK, HGRN_DV), F32),
                        pltpu.VMEM((hb, CHUNK, HGRN_DV), F32)],
        compiler_params=pltpu.CompilerParams(
            dimension_semantics=("parallel", "parallel", "arbitrary"), vmem_limit_bytes=VMEM_LIMIT),
        name="hgrn_seq",
    )(h, h, h, h, lb, nw, s0, masks)


def _hgrn_step_kernel(q_ref, fx_ref, i_ref, g_ref, lb_ref, nw_ref, s_ref, og_ref, sout_ref, *, n_seq, t):
    tile = (STEP_GROUP, t, HGRN_DK)
    sub = lax.broadcasted_iota(jnp.int32, (1, t, HGRN_DK), 1)

    for hd in range(s_ref.shape[1]):
        cols = slice(hd * HGRN_DK, (hd + 1) * HGRN_DK)
        lb = lb_ref[:, cols]
        nw = nw_ref[:, cols]

        def group(i, carry, hd=hd, cols=cols, lb=lb, nw=nw):
            rows = pl.ds(pl.multiple_of(i * (STEP_GROUP * t), STEP_GROUP * t), STEP_GROUP * t)
            q = q_ref[rows, cols].reshape(tile)
            v = i_ref[rows, cols].reshape(tile)
            _, logf, kk = _gates(fx_ref[rows, cols].reshape(tile), lb)
            cum = logf
            shift = 1
            while shift < t:
                cum = cum + jnp.where(sub >= shift, pltpu.roll(cum, shift, 1), 0.0)
                shift *= 2
            intra = (jnp.sum(q * kk, axis=-1, keepdims=True)) * v
            for d in range(1, t):
                valid = sub >= d
                x = jnp.exp2(jnp.where(valid, cum - pltpu.roll(cum, d, 1), 0.0))
                w = jnp.sum(jnp.where(valid, q * x * pltpu.roll(kk, d, 1), 0.0), axis=-1, keepdims=True)
                intra = intra + w * pltpu.roll(v, d, 1)
            blast = cum[:, t - 1:t, :]
            qd = q * jnp.exp2(cum)
            kd = kk * jnp.exp2(blast - cum)
            seqs = [i * STEP_GROUP + n for n in range(STEP_GROUP)]
            inter = [_dot(qd[n].astype(BF16), s_ref[b, hd].astype(BF16)) for n, b in enumerate(seqs)]
            for n, b in enumerate(seqs):
                sout_ref[b, hd] = _state_update(s_ref[b, hd], kd[n], v[n], blast[n], 8)
            o = jnp.stack(inter, axis=0) + intra
            og = _norm_gate(o, g_ref[rows, cols].reshape(tile), nw)
            og_ref[rows, cols] = og.reshape(STEP_GROUP * t, HGRN_DK).astype(og_ref.dtype)
            return carry

        lax.fori_loop(0, n_seq // STEP_GROUP, group, 0)


def _hgrn_step(h, lb, nw, s0, *, n_seq, t, seq_per_block):
    seq_per_block = min(seq_per_block, n_seq)
    assert n_seq % seq_per_block == 0 and seq_per_block % STEP_GROUP == 0 and t == 8
    rows = seq_per_block * t
    hb = STEP_HEADS
    width = hb * HGRN_DK
    groups = HGRN_HEADS // hb
    col = lambda part: (lambda i, hg: (i, part * groups + hg))
    state_spec = pl.BlockSpec((seq_per_block, hb, HGRN_DK, HGRN_DV), lambda i, hg: (i, hg, 0, 0))
    return pl.pallas_call(
        functools.partial(_hgrn_step_kernel, n_seq=seq_per_block, t=t),
        grid=(n_seq // seq_per_block, groups),
        in_specs=[pl.BlockSpec((rows, width), col(0)),
                  pl.BlockSpec((rows, width), col(1)),
                  pl.BlockSpec((rows, width), col(2)),
                  pl.BlockSpec((rows, width), col(3)),
                  pl.BlockSpec((1, width), lambda i, hg: (0, hg)),
                  pl.BlockSpec((1, width), lambda i, hg: (0, hg)),
                  state_spec],
        out_specs=[pl.BlockSpec((rows, width), lambda i, hg: (i, hg)), state_spec],
        out_shape=[jax.ShapeDtypeStruct((n_seq * t, HGRN_HEADS * HGRN_DV), BF16),
                   jax.ShapeDtypeStruct((n_seq, HGRN_HEADS, HGRN_DK, HGRN_DV), F32)],
        compiler_params=pltpu.CompilerParams(
            dimension_semantics=("parallel", "parallel"), vmem_limit_bytes=VMEM_LIMIT),
        name="hgrn_step",
    )(h, h, h, h, lb, nw, s0)


def _attn_seq_kernel(q_ref, g_ref, kc_ref, vc_ref, kp_ref, vp_ref, meta_ref, sink_ref, og_ref):
    first = pl.program_id(1) == 0
    kprev = jnp.where(first, meta_ref[:, :SWA_KV_WIDTH], kp_ref[...])
    vprev = jnp.where(first, meta_ref[:, SWA_KV_WIDTH:], vp_ref[...])
    kband = (jnp.concatenate([kprev, kc_ref[...]], axis=0) * (SWA_SCALE * LOG2E)).astype(BF16)
    vband_t = jnp.concatenate([vprev, vc_ref[...]], axis=0).T.astype(BF16)
    zeros_t = jnp.zeros((SWA_HEAD_DIM, 2 * SEQ_BLOCK), BF16)
    kj = lax.broadcasted_iota(jnp.int32, (2 * SEQ_BLOCK, SEQ_BLOCK), 0)
    qi = lax.broadcasted_iota(jnp.int32, (2 * SEQ_BLOCK, SEQ_BLOCK), 1)
    dist = SEQ_BLOCK + qi - kj
    valid = (dist >= 0) & (dist < WINDOW) & (jnp.logical_not(first) | (kj >= SEQ_BLOCK - N_META))
    madd = jnp.where(valid, 0.0, -jnp.inf)
    zeros = jnp.zeros((2 * SEQ_BLOCK, SWA_HEAD_DIM), BF16)
    pairs = SWA_GROUP // 2
    pw = 2 * SWA_HEAD_DIM

    def block_diag(x):
        return jnp.concatenate([jnp.concatenate([x, zeros], axis=1),
                                jnp.concatenate([zeros, x], axis=1)], axis=0)

    sts = []
    for kvh in range(SWA_KV_HEADS):
        ks = slice(kvh * SWA_HEAD_DIM, (kvh + 1) * SWA_HEAD_DIM)
        col0 = kvh * SWA_GROUP * SWA_HEAD_DIM
        xq = jnp.concatenate([q_ref[:, col0 + p * pw:col0 + (p + 1) * pw].astype(BF16)
                              for p in range(pairs)], axis=0)
        sts.append(_dot_nt(block_diag(kband[:, ks]), xq))
    for kvh in range(SWA_KV_HEADS):
        ks = slice(kvh * SWA_HEAD_DIM, (kvh + 1) * SWA_HEAD_DIM)
        col0 = kvh * SWA_GROUP * SWA_HEAD_DIM
        vt = vband_t[ks, :]
        v2t = jnp.concatenate([jnp.concatenate([vt, zeros_t], axis=1),
                               jnp.concatenate([zeros_t, vt], axis=1)], axis=0)
        pbs, rdens = [], []
        for pr in range(pairs):
            head = kvh * SWA_GROUP + 2 * pr
            s2 = (sts[kvh][:, pr * SEQ_BLOCK:(pr + 1) * SEQ_BLOCK].reshape(2, 2 * SEQ_BLOCK, SEQ_BLOCK)
                  + madd[None])
            sink = jnp.concatenate([jnp.full((1, 1, SEQ_BLOCK), sink_ref[head] * LOG2E, F32),
                                    jnp.full((1, 1, SEQ_BLOCK), sink_ref[head + 1] * LOG2E, F32)], axis=0)
            m = jnp.maximum(jnp.max(s2, axis=1, keepdims=True), sink)
            p = jnp.exp2(s2 - m)
            den = jnp.sum(p, axis=1, keepdims=True) + jnp.exp2(sink - m)
            pbs.append(p.astype(BF16).reshape(4 * SEQ_BLOCK, SEQ_BLOCK))
            rdens.append(1.0 / den)
        for pr in range(pairs):
            ot = _dot(v2t, pbs[pr]).reshape(2, SWA_HEAD_DIM, SEQ_BLOCK) * rdens[pr]
            o = ot.reshape(2 * SWA_HEAD_DIM, SEQ_BLOCK).T
            cs = slice(col0 + pr * pw, col0 + (pr + 1) * pw)
            og_ref[:, cs] = (o * _silu(g_ref[:, cs])).astype(og_ref.dtype)


def _attn_seq(h1, h1_meta, sinks, *, n_seq, seq_len, meta_row_block):
    nb = seq_len // SEQ_BLOCK
    width = SWA_Q_HEADS * SWA_HEAD_DIM
    kcol = 2 * width // SWA_KV_WIDTH
    return pl.pallas_call(
        _attn_seq_kernel,
        grid=(n_seq, nb),
        in_specs=[pl.BlockSpec((SEQ_BLOCK, width), lambda b, j: (b * nb + j, 0)),
                  pl.BlockSpec((SEQ_BLOCK, width), lambda b, j: (b * nb + j, 1)),
                  pl.BlockSpec((SEQ_BLOCK, SWA_KV_WIDTH), lambda b, j: (b * nb + j, kcol)),
                  pl.BlockSpec((SEQ_BLOCK, SWA_KV_WIDTH), lambda b, j: (b * nb + j, kcol + 1)),
                  pl.BlockSpec((SEQ_BLOCK, SWA_KV_WIDTH), lambda b, j: (b * nb + jnp.maximum(j - 1, 0), kcol)),
                  pl.BlockSpec((SEQ_BLOCK, SWA_KV_WIDTH), lambda b, j: (b * nb + jnp.maximum(j - 1, 0), kcol + 1)),
                  pl.BlockSpec((SEQ_BLOCK, 2 * SWA_KV_WIDTH), lambda b, j: (meta_row_block, kcol // 2)),
                  pl.BlockSpec(memory_space=pltpu.SMEM)],
        out_specs=pl.BlockSpec((SEQ_BLOCK, width), lambda b, j: (b * nb + j, 0)),
        out_shape=jax.ShapeDtypeStruct((n_seq * seq_len, width), BF16),
        compiler_params=pltpu.CompilerParams(
            dimension_semantics=("parallel", "arbitrary"), vmem_limit_bytes=VMEM_LIMIT),
        name="attn_seq",
    )(h1, h1, h1, h1, h1, h1, h1_meta, sinks)


def _attn_step_kernel(q_ref, g_ref, kn_ref, vn_ref, ck_ref, cv_ref, sink_ref, og_ref, nk_ref, nv_ref, *, n_seq, t):
    keys = WINDOW + t
    hd = SWA_HEAD_DIM
    tiles = SWA_Q_HEADS // 2
    kj = lax.broadcasted_iota(jnp.int32, (keys, SWA_Q_HEADS * t), 0)
    qt = lax.broadcasted_iota(jnp.int32, (keys, SWA_Q_HEADS * t), 1) % t
    madd = jnp.where((kj >= qt + 1) & (kj <= WINDOW + qt), 0.0, -jnp.inf)
    low = lax.broadcasted_iota(jnp.int32, (t, 2 * hd), 1) < hd
    zero_tile = jnp.zeros((t, 2 * hd), F32)
    sink = sink_ref[...] * LOG2E

    def group(i, carry):
        seqs = [i * SEQ_UNROLL + n for n in range(SEQ_UNROLL)]
        rows = [pl.ds(pl.multiple_of(b * t, t), t) for b in seqs]
        st, vall = [], []
        for b, rw in zip(seqs, rows):
            kc, vc = ck_ref[b], cv_ref[b]
            kn, vn = kn_ref[rw, :], vn_ref[rw, :]
            nk_ref[b, 0:WINDOW - t, :] = kc[t:, :]
            nk_ref[b, WINDOW - t:WINDOW, :] = kn
            nv_ref[b, 0:WINDOW - t, :] = vc[t:, :]
            nv_ref[b, WINDOW - t:WINDOW, :] = vn
            kall = jnp.concatenate([kc, kn], axis=0).astype(BF16)
            vall.append(jnp.concatenate([vc, vn], axis=0).astype(BF16))
            q = q_ref[rw, :] * (SWA_SCALE * LOG2E)
            qtile = [q[:, j * 2 * hd:(j + 1) * 2 * hd] for j in range(tiles)]
            qswap = [pltpu.roll(x, hd, 1) for x in qtile]
            groups = []
            for kvh in range(SWA_KV_HEADS):
                for gq in range(SWA_GROUP):
                    j = (kvh * SWA_GROUP + gq) // 2
                    src = qtile[j] if gq % 2 == kvh % 2 else qswap[j]
                    half = jnp.where(low, src, 0.0) if kvh % 2 == 0 else jnp.where(low, 0.0, src)
                    groups.append(jnp.concatenate(
                        [half if c == kvh // 2 else zero_tile for c in range(SWA_KV_HEADS // 2)], axis=1))
            qbd = jnp.concatenate(groups, axis=0).astype(BF16)
            st.append(_dot_nt(kall, qbd))
        pn = []
        for s in st:
            s = s + madd
            m = jnp.maximum(jnp.max(s, axis=0, keepdims=True), sink)
            p = jnp.exp2(s - m)
            den = jnp.sum(p, axis=0, keepdims=True) + jnp.exp2(sink - m)
            pn.append((p * (1.0 / den)).astype(BF16))
        of = [_dot_tn(p, vl) for p, vl in zip(pn, vall)]
        for o, rw in zip(of, rows):
            g = g_ref[rw, :]
            out = []
            for j in range(tiles):
                kvh = (2 * j) // SWA_GROUP
                ct = slice((kvh // 2) * 2 * hd, (kvh // 2 + 1) * 2 * hd)
                ra = o[(2 * j) * t:(2 * j + 1) * t, ct]
                rb = o[(2 * j + 1) * t:(2 * j + 2) * t, ct]
                if kvh % 2 == 0:
                    out.append(jnp.where(low, ra, pltpu.roll(rb, hd, 1)))
                else:
                    out.append(jnp.where(low, pltpu.roll(ra, hd, 1), rb))
            og_ref[rw, :] = (jnp.concatenate(out, axis=1) * _silu(g)).astype(og_ref.dtype)
        return carry

    lax.fori_loop(0, n_seq // SEQ_UNROLL, group, 0)


def _attn_step(h1, cache_k, cache_v, sink_cols, *, n_seq, t, seq_per_block):
    assert n_seq % seq_per_block == 0
    rows = seq_per_block * t
    width = SWA_Q_HEADS * SWA_HEAD_DIM
    kcol = 2 * width // SWA_KV_WIDTH
    cache_spec = pl.BlockSpec((seq_per_block, WINDOW, SWA_KV_WIDTH), lambda i: (i, 0, 0))
    return pl.pallas_call(
        functools.partial(_attn_step_kernel, n_seq=seq_per_block, t=t),
        grid=(n_seq // seq_per_block,),
        in_specs=[pl.BlockSpec((rows, width), lambda i: (i, 0)),
                  pl.BlockSpec((rows, width), lambda i: (i, 1)),
                  pl.BlockSpec((rows, SWA_KV_WIDTH), lambda i: (i, kcol)),
                  pl.BlockSpec((rows, SWA_KV_WIDTH), lambda i: (i, kcol + 1)),
                  cache_spec, cache_spec,
                  pl.BlockSpec((1, SWA_Q_HEADS * t), lambda i: (0, 0))],
        out_specs=[pl.BlockSpec((rows, width), lambda i: (i, 0)), cache_spec, cache_spec],
        out_shape=[jax.ShapeDtypeStruct((n_seq * t, width), BF16),
                   jax.ShapeDtypeStruct(cache_k.shape, F32),
                   jax.ShapeDtypeStruct(cache_v.shape, F32)],
        compiler_params=pltpu.CompilerParams(
            dimension_semantics=("parallel",), vmem_limit_bytes=VMEM_LIMIT),
        name="attn_step",
    )(h1, h1, h1, h1, cache_k, cache_v, sink_cols)


def kernel(x_prompt, x_sample, state_hgrn, cache_swa_k, cache_swa_v, meta_tokens,
           hgrn_w_in, hgrn_lb_logits, hgrn_norm_w, hgrn_w_out,
           swa_w_in, swa_sinks, swa_w_out, ln_g, ln_b):
    out_dtype = x_prompt.dtype
    bsz, seq, d = x_prompt.shape
    dec_b, dec_t, _ = x_sample.shape
    n_p = bsz * seq
    n_s = dec_b * dec_t
    width = SWA_Q_HEADS * SWA_HEAD_DIM

    w_out0 = hgrn_w_out[0].astype(BF16)
    w_out1 = swa_w_out[0].astype(BF16)
    in_tile = SHORT_COL_TILE
    q_tiles = width // in_tile
    kv_tile = 2 * SWA_KV_WIDTH // in_tile
    assert width % in_tile == 0 and (2 * SWA_KV_WIDTH) % in_tile == 0 and kv_tile == 1
    regroup = lambda j: jnp.where(j < q_tiles, j, jnp.where(j < 2 * q_tiles, j + kv_tile, q_tiles))
    lb = jnp.cumsum(jax.nn.softmax(hgrn_lb_logits.astype(F32), axis=0), axis=0)[0:1]
    nw = hgrn_norm_w[0].astype(F32).reshape(1, -1)
    sinks = swa_sinks[0].astype(F32).reshape(1, SWA_Q_HEADS)
    sink_cols = jnp.repeat(sinks, dec_t, axis=1)
    g0, b0 = ln_g[0:1].astype(F32), ln_b[0:1].astype(F32)
    g1, b1 = ln_g[1:2].astype(F32), ln_b[1:2].astype(F32)

    x_p = x_prompt.astype(F32).reshape(n_p, d)
    meta_block = jnp.concatenate(
        [jnp.zeros((SEQ_BLOCK - N_META, d), F32), meta_tokens.astype(F32)], axis=0)
    x_s = x_sample.astype(F32).reshape(n_s, d)
    meta_blk = n_s // SEQ_BLOCK

    h_sm, w_in0 = _matmul_cast(x_s, meta_block, hgrn_w_in[0].astype(F32), xs_rows=n_s, xm_block=0, tn=SHORT_IN0_COL_TILE)
    h_p = _matmul(x_p, w_in0, tm=PROJ_ROW_TILE, tn=PROJ_IN0_COL_TILE)
    zero_state = jnp.zeros((1, HGRN_HEADS, HGRN_DK, HGRN_DV), F32)
    og_meta, s_meta = _hgrn_seq(h_sm, lb, nw, zero_state, n_seq=1, seq_len=SEQ_BLOCK, row_block0=meta_blk)
    og_s, st_s = _hgrn_step(h_sm, lb, nw, state_hgrn[0].astype(F32), n_seq=dec_b, t=dec_t, seq_per_block=STEP_STATE_SEQS)
    og_p, st_p = _hgrn_seq(h_p, lb, nw, s_meta, n_seq=bsz, seq_len=seq, row_block0=0)
    x1_sm = _matmul_deepnorm_tail(og_s, og_meta, w_out0, x_s, meta_block, g0, b0)
    x1_p = _matmul_deepnorm(og_p, w_out0, x_p, g0, b0, tm=PROJ_OUT_ROW_TILE)

    h1_sm, w_in1 = _matmul_cast(x1_sm, x1_sm, swa_w_in[0].astype(F32), xs_rows=n_s, xm_block=meta_blk,
                                tn=in_tile, col_tile=regroup)
    h1_p = _matmul(x1_p, w_in1, tm=PROJ_ROW_TILE, tn=PROJ_IN1_COL_TILE)
    og1_p = _attn_seq(h1_p, h1_sm, sinks.reshape(-1), n_seq=bsz, seq_len=seq, meta_row_block=meta_blk)
    ck = cache_swa_k[0].astype(F32).reshape(dec_b, WINDOW, SWA_KV_WIDTH)
    cv = cache_swa_v[0].astype(F32).reshape(dec_b, WINDOW, SWA_KV_WIDTH)
    og1_s, nk_s, nv_s = _attn_step(h1_sm, ck, cv, sink_cols, n_seq=dec_b, t=dec_t, seq_per_block=STEP_CACHE_SEQS)
    y_p = _matmul_deepnorm(og1_p, w_out1, x1_p, g1, b1, tm=PROJ_OUT_ROW_TILE)
    y_s = _matmul_deepnorm(og1_s, w_out1, x1_sm, g1, b1, tm=PROJ_OUT_ROW_TILE)

    kv_p = h1_p.reshape(bsz, seq, -1)[:, seq - WINDOW:, 2 * width:]
    cache_shape = (1, bsz, WINDOW, SWA_KV_HEADS, SWA_HEAD_DIM)
    return (y_p.reshape(bsz, seq, d).astype(out_dtype),
            y_s.reshape(dec_b, dec_t, d).astype(out_dtype),
            st_p[None].astype(out_dtype),
            st_s[None].astype(out_dtype),
            kv_p[..., :SWA_KV_WIDTH].reshape(cache_shape).astype(out_dtype),
            kv_p[..., SWA_KV_WIDTH:].reshape(cache_shape).astype(out_dtype),
            nk_s.reshape((1,) + cache_swa_k.shape[1:]).astype(out_dtype),
            nv_s.reshape((1,) + cache_swa_v.shape[1:]).astype(out_dtype))
```

```python
import functools

import numpy as np
import jax
import jax.numpy as jnp
from jax import lax
from jax.experimental import pallas as pl
from jax.experimental.pallas import tpu as pltpu

F32 = jnp.float32
BF16 = jnp.bfloat16

D_MODEL = 2048
N_META = 16
DEPTH = 2
HGRN_HEADS = 16
HGRN_DK = 128
HGRN_DV = 128
SWA_Q_HEADS = 32
SWA_KV_HEADS = 4
SWA_GROUP = 8
SWA_HEAD_DIM = 64
SWA_KV_WIDTH = SWA_KV_HEADS * SWA_HEAD_DIM
SWA_SCALE = SWA_HEAD_DIM ** -0.5
WINDOW = 128
DEEPNORM_ALPHA = (2.0 * DEPTH) ** 0.25
LN_EPS = 1e-5
RMS_EPS = 1e-6
LOG2E = 1.4426950408889634

CHUNK = 128
LEVELS = 7
MAX_FACTORED_LOG2_DECAY = 80.0
MASK_ROWS = 16
SEQ_BLOCK = 128
LN_SUB_ROWS = 128
STEP_GROUP = 32
SEQ_UNROLL = 8
HGRN_SEQ_HEADS = 16
HGRN_SEQ_ROWS = 256

V7X_VMEM_BYTES = 64 * 1024 * 1024
VMEM_LIMIT = V7X_VMEM_BYTES - 4 * 1024 * 1024
PROJ_ROW_TILE = 1024
PROJ_IN0_COL_TILE = 2048
PROJ_IN1_COL_TILE = 2304
PROJ_OUT_ROW_TILE = 512
SHORT_COL_TILE = 512
SHORT_IN0_COL_TILE = 1024
STEP_STATE_SEQS = 64
STEP_HEADS = 2
STEP_CACHE_SEQS = 32


def _dot(a, b):
    return jnp.dot(a, b, preferred_element_type=F32)


def _dot_nt(a, b):
    return lax.dot_general(a, b, (((1,), (1,)), ((), ())), preferred_element_type=F32)


def _dot_tn(a, b):
    return lax.dot_general(a, b, (((0,), (0,)), ((), ())), preferred_element_type=F32)


def _silu(x):
    return x * jax.nn.sigmoid(x)


def _mm_kernel(x_ref, w_ref, *refs):
    n_side = (len(refs) - 1) // 2
    side_in, o_ref, side_out = refs[:n_side], refs[n_side], refs[n_side + 1:]
    o_ref[...] = _dot(x_ref[...].astype(BF16), w_ref[...])
    for src, dst in zip(side_in, side_out):
        dst[...] = src[...].astype(BF16)


def _row_tile(n, want):
    if n <= want:
        return n
    return max(t for t in range(16, want + 1, 16) if n % t == 0)


def _matmul(x, w, *, tm, tn, cast_along=()):
    n, k = x.shape
    e = w.shape[1]
    tm = _row_tile(n, tm)
    assert n % tm == 0 and e % tn == 0
    ncol = e // tn
    steps = (n // tm) * ncol
    side_specs = []
    for a in cast_along:
        assert a.shape[0] % (16 * steps) == 0
        side_specs.append(pl.BlockSpec((a.shape[0] // steps, a.shape[1]), lambda i, j: (i * ncol + j, 0)))
    out = pl.pallas_call(
        _mm_kernel,
        grid=(n // tm, ncol),
        in_specs=[pl.BlockSpec((tm, k), lambda i, j: (i, 0)),
                  pl.BlockSpec((k, tn), lambda i, j: (0, j))] + side_specs,
        out_specs=[pl.BlockSpec((tm, tn), lambda i, j: (i, j))] + side_specs,
        out_shape=[jax.ShapeDtypeStruct((n, e), F32)] + [jax.ShapeDtypeStruct(a.shape, BF16) for a in cast_along],
        compiler_params=pltpu.CompilerParams(
            dimension_semantics=("parallel", "parallel"), vmem_limit_bytes=VMEM_LIMIT),
        name="proj_in",
    )(x, w, *cast_along)
    return out if cast_along else out[0]


def _mm_cast_kernel(xs_ref, xm_ref, w_ref, o_ref, wb_ref, xb_scr):
    @pl.when(pl.program_id(0) == 0)
    def _():
        n_s = xs_ref.shape[0]
        xb_scr[0:n_s, :] = xs_ref[...].astype(BF16)
        xb_scr[n_s:, :] = xm_ref[...].astype(BF16)

    wb = w_ref[...].astype(BF16)
    wb_ref[...] = wb
    o_ref[...] = _dot(xb_scr[...], wb)


def _matmul_cast(xs, xm, w, *, xs_rows, xm_block, tn, col_tile=lambda j: j):
    k, e = w.shape
    n = xs_rows + SEQ_BLOCK
    assert e % tn == 0
    return pl.pallas_call(
        _mm_cast_kernel,
        grid=(e // tn,),
        in_specs=[pl.BlockSpec((xs_rows, k), lambda j: (0, 0), pipeline_mode=pl.Buffered(1)),
                  pl.BlockSpec((SEQ_BLOCK, k), lambda j: (xm_block, 0), pipeline_mode=pl.Buffered(1)),
                  pl.BlockSpec((k, tn), lambda j: (0, col_tile(j)))],
        out_specs=[pl.BlockSpec((n, tn), lambda j: (0, j)),
                   pl.BlockSpec((k, tn), lambda j: (0, j))],
        out_shape=[jax.ShapeDtypeStruct((n, e), F32), jax.ShapeDtypeStruct((k, e), BF16)],
        scratch_shapes=[pltpu.VMEM((n, k), BF16)],
        compiler_params=pltpu.CompilerParams(
            dimension_semantics=("arbitrary",), vmem_limit_bytes=VMEM_LIMIT),
        name="proj_in_cast",
    )(xs, xm, w)


def _mm_ln_tail_kernel(a_ref, at_ref, w_ref, x_ref, xt_ref, g_ref, b_ref, o_ref):
    tail = pl.program_id(0) == pl.num_programs(0) - 1
    a = jnp.where(tail, at_ref[...], a_ref[...])
    x = jnp.where(tail, xt_ref[...], x_ref[...])
    z = DEEPNORM_ALPHA * x + _dot(a, w_ref[...])
    mu = jnp.mean(z, axis=-1, keepdims=True)
    zc = z - mu
    var = jnp.mean(zc * zc, axis=-1, keepdims=True)
    o_ref[...] = zc * lax.rsqrt(var + LN_EPS) * g_ref[...] + b_ref[...]


def _matmul_deepnorm_tail(a, a_tail, w, x, x_tail, g, b):
    n, k = a.shape
    d = w.shape[1]
    tm = SEQ_BLOCK
    last = n // tm - 1
    body = lambda i: (jnp.minimum(i, last), 0)
    const = lambda i: (0, 0)
    return pl.pallas_call(
        _mm_ln_tail_kernel,
        grid=(n // tm + 1,),
        in_specs=[pl.BlockSpec((tm, k), body), pl.BlockSpec((tm, k), const),
                  pl.BlockSpec((k, d), const),
                  pl.BlockSpec((tm, d), body), pl.BlockSpec((tm, d), const),
                  pl.BlockSpec((1, d), const), pl.BlockSpec((1, d), const)],
        out_specs=pl.BlockSpec((tm, d), lambda i: (i, 0)),
        out_shape=jax.ShapeDtypeStruct((n + tm, d), F32),
        compiler_params=pltpu.CompilerParams(
            dimension_semantics=("parallel",), vmem_limit_bytes=VMEM_LIMIT),
        name="proj_out_deepnorm_tail",
    )(a, a_tail, w, x, x_tail, g, b)


def _mm_ln_kernel(a_ref, w_ref, x_ref, g_ref, b_ref, o_ref):
    tm = a_ref.shape[0]
    sub = LN_SUB_ROWS if tm % LN_SUB_ROWS == 0 else tm
    for r0 in range(0, tm, sub):
        rows = slice(r0, r0 + sub)
        z = DEEPNORM_ALPHA * x_ref[rows, :] + _dot(a_ref[rows, :], w_ref[...])
        mu = jnp.mean(z, axis=-1, keepdims=True)
        zc = z - mu
        var = jnp.mean(zc * zc, axis=-1, keepdims=True)
        o_ref[rows, :] = zc * lax.rsqrt(var + LN_EPS) * g_ref[...] + b_ref[...]


def _matmul_deepnorm(a, w, x, g, b, *, tm):
    n, k = a.shape
    d = w.shape[1]
    tm = _row_tile(n, tm)
    return pl.pallas_call(
        _mm_ln_kernel,
        grid=(n // tm,),
        in_specs=[pl.BlockSpec((tm, k), lambda i: (i, 0)),
                  pl.BlockSpec((k, d), lambda i: (0, 0)),
                  pl.BlockSpec((tm, d), lambda i: (i, 0)),
                  pl.BlockSpec((1, d), lambda i: (0, 0)),
                  pl.BlockSpec((1, d), lambda i: (0, 0))],
        out_specs=pl.BlockSpec((tm, d), lambda i: (i, 0)),
        out_shape=jax.ShapeDtypeStruct((n, d), F32),
        compiler_params=pltpu.CompilerParams(
            dimension_semantics=("parallel",), vmem_limit_bytes=VMEM_LIMIT),
        name="proj_out_deepnorm",
    )(a, w, x, g, b)


def _level_masks():
    t = np.arange(CHUNK)
    out = np.zeros((LEVELS + 1, CHUNK, CHUNK), np.float32)
    out[0] = t[:, None] > t[None, :]
    for l in range(LEVELS):
        h = 1 << l
        same = (t[:, None] >> (l + 1)) == (t[None, :] >> (l + 1))
        out[l + 1] = same & ((t[:, None] & h) != 0) & ((t[None, :] & h) == 0)
    return out


def _gates(fx, lb):
    f = lb + (1.0 - lb) * jax.nn.sigmoid(fx)
    return f, jnp.log2(f), 1.0 - f


def _prefix_scan(logf):
    tiles = (CHUNK // 8, 8, HGRN_DK)
    c = logf.reshape(tiles)
    sub = lax.broadcasted_iota(jnp.int32, (1, 8, HGRN_DK), 1)
    bcast = lambda x, r: jnp.broadcast_to(x[:, r:r + 1, :], tiles)
    c = c + jnp.where((sub & 1) != 0, pltpu.roll(c, 1, 1), 0.0)
    c = c + jnp.where((sub & 2) != 0, jnp.where(sub < 4, bcast(c, 1), bcast(c, 5)), 0.0)
    c = c + jnp.where((sub & 4) != 0, bcast(c, 3), 0.0)
    c = c.reshape(CHUNK, HGRN_DK)
    for level in range(3, LEVELS):
        half = 1 << level
        pieces = []
        for r0 in range(0, CHUNK, 2 * half):
            pieces += [c[r0:r0 + half], c[r0 + half:r0 + 2 * half] + c[r0 + half - 1:r0 + half, :]]
        c = jnp.concatenate(pieces, axis=0)
    return c


def _small_levels(logf, f, q, kk):
    tiles = (CHUNK // 8, 8, HGRN_DK)
    c, f3, q3, k3 = (x.reshape(tiles) for x in (logf, f, q, kk))
    sub = lax.broadcasted_iota(jnp.int32, (1, 8, HGRN_DK), 1)
    bcast = lambda x, r: jnp.broadcast_to(x[:, r:r + 1, :], tiles)
    up = (sub & 1) != 0
    zs = [jnp.where(up, q3 * f3, k3)]
    c = c + jnp.where(up, pltpu.roll(c, 1, 1), 0.0)
    for level, tot in ((1, lambda c: jnp.where(sub < 4, bcast(c, 1), bcast(c, 5))),
                       (2, lambda c: bcast(c, 3))):
        up = (sub & (1 << level)) != 0
        t = tot(c)
        zs.append(jnp.where(up, q3, k3) * jnp.exp2(jnp.where(up, c, t - c)))
        c = c + jnp.where(up, t, 0.0)
    return [z.reshape(CHUNK, HGRN_DK) for z in zs], c.reshape(CHUNK, HGRN_DK)


def _level_large(cum, q, kk, level):
    half = 1 << level
    args, bases, cums = [], [], []
    for r0 in range(0, CHUNK, 2 * half):
        lo, up = slice(r0, r0 + half), slice(r0 + half, r0 + 2 * half)
        tot = cum[r0 + half - 1:r0 + half, :]
        args += [tot - cum[lo], cum[up]]
        bases += [kk[lo], q[up]]
        cums += [cum[lo], cum[up] + tot]
    x = jnp.exp2(jnp.concatenate(args, axis=0))
    return jnp.concatenate(bases, axis=0) * x, jnp.concatenate(cums, axis=0)


def _split3(x):
    hi = x.astype(BF16)
    r = x - hi.astype(F32)
    mid = r.astype(BF16)
    lo = (r - mid.astype(F32)).astype(BF16)
    return hi, mid, lo


def _state_update(s_old, kd, v, blast, pad_rows):
    c = kd.shape[0]
    hi, mid, lo = _split3(jnp.exp2(blast))
    row = lax.broadcasted_iota(jnp.int32, (pad_rows, HGRN_DK), 0)
    dec = jnp.where(row == 0, hi.astype(F32),
                    jnp.where(row == 1, mid.astype(F32), jnp.where(row == 2, lo.astype(F32), 0.0)))
    piece = BF16 if c % 16 == 0 and pad_rows % 16 == 0 else F32
    lhs = jnp.concatenate([kd.astype(piece), dec.astype(piece)], axis=0).astype(BF16)
    rhs = jnp.concatenate(
        [jnp.concatenate([v.astype(piece), jnp.zeros((c, HGRN_DV), piece)], axis=1),
         jnp.concatenate([jnp.zeros((pad_rows, HGRN_DV), piece), jnp.ones((pad_rows, HGRN_DV), piece)], axis=1)],
        axis=0).astype(BF16)
    both = _dot_tn(lhs, rhs)
    return both[:, HGRN_DV:] * s_old + both[:, :HGRN_DV]


def _norm_gate(o, g, nw):
    o = o * lax.rsqrt(jnp.mean(o * o, axis=-1, keepdims=True) + RMS_EPS) * nw
    return o * _silu(g)


def _hgrn_seq_kernel(q_ref, fx_ref, i_ref, g_ref, lb_ref, nw_ref, s0_ref, m_ref,
                     og_ref, sout_ref, s_scr, o_scr, *, n_chunks):
    heads = s_scr.shape[0]

    @pl.when(pl.program_id(2) == 0)
    def _():
        s_scr[...] = s0_ref[0]

    def chunk(c, carry):
        rows = pl.ds(pl.multiple_of(c * CHUNK, CHUNK), CHUNK)
        hcols = [slice(hd * HGRN_DK, (hd + 1) * HGRN_DK) for hd in range(heads)]
        q = [q_ref[rows, cs] for cs in hcols]
        v = [i_ref[rows, cs] for cs in hcols]
        gates = [_gates(fx_ref[rows, cs], lb_ref[:, cs]) for cs in hcols]
        kk = [gt[2] for gt in gates]
        cum = [_prefix_scan(gt[1]) for gt in gates]
        mid = CHUNK // 2 - 1
        spread = None
        for b in cum:
            s_hd = jnp.maximum(b[0:1, :] - b[mid:mid + 1, :], b[mid:mid + 1, :] - b[CHUNK - 1:CHUNK, :])
            spread = s_hd if spread is None else jnp.maximum(spread, s_hd)
        factorable = jnp.max(spread) <= MAX_FACTORED_LOG2_DECAY

        vb = [x.astype(BF16) for x in v]
        for hd in range(heads):
            d = cum[hd] - cum[hd][mid:mid + 1, :]
            qe = (q[hd] * jnp.exp2(d)).astype(BF16)
            ke = (kk[hd] * jnp.exp2(-d)).astype(BF16)
            amat = jnp.where(m_ref[0] > 0.5, _dot_nt(qe, ke), 0.0).astype(BF16)
            blast = cum[hd][CHUNK - 1:CHUNK, :]
            qd = (q[hd] * jnp.exp2(cum[hd])).astype(BF16)
            kd = kk[hd] * jnp.exp2(blast - cum[hd])
            s_old = s_scr[hd]
            base = _dot(qd, s_old.astype(BF16)) + jnp.sum(q[hd] * kk[hd], axis=-1, keepdims=True) * v[hd]
            o_scr[hd] = base
            s_scr[hd] = _state_update(s_old, kd, vb[hd], blast, 16)
            og_ref[rows, hcols[hd]] = _norm_gate(
                base + _dot(amat, vb[hd]), g_ref[rows, hcols[hd]], nw_ref[:, hcols[hd]]).astype(og_ref.dtype)

        @pl.when(jnp.logical_not(factorable))
        def _():
            small = [_small_levels(gt[1], gt[0], q[hd], kk[hd]) for hd, gt in enumerate(gates)]
            zs = [sm[0] for sm in small]
            part = [sm[1] for sm in small]
            for level in range(3, LEVELS):
                for hd in range(heads):
                    z, part[hd] = _level_large(part[hd], q[hd], kk[hd], level)
                    zs[hd].append(z)
            nblk = CHUNK // MASK_ROWS
            a = [[None] * nblk for _ in range(heads)]
            for level in range(LEVELS):
                step = 1 << (level - 4) if level >= 4 else 0
                blocks = [i for i in range(nblk) if level < 4 or (i & step)]
                for hd in range(heads):
                    z = zs[hd][level].astype(BF16)
                    lhs = z if level < 4 else jnp.concatenate(
                        [z[i * MASK_ROWS:(i + 1) * MASK_ROWS] for i in blocks], axis=0)
                    term = _dot_nt(lhs, z)
                    for n, i in enumerate(blocks):
                        t = (term[n * MASK_ROWS:(n + 1) * MASK_ROWS]
                             * m_ref[level + 1, i * MASK_ROWS:(i + 1) * MASK_ROWS, :])
                        a[hd][i] = t if a[hd][i] is None else a[hd][i] + t
            for hd in range(heads):
                amat = jnp.concatenate(a[hd], axis=0).astype(BF16)
                og_ref[rows, hcols[hd]] = _norm_gate(
                    o_scr[hd] + _dot(amat, vb[hd]),
                    g_ref[rows, hcols[hd]], nw_ref[:, hcols[hd]]).astype(og_ref.dtype)
        return carry

    lax.fori_loop(0, n_chunks, chunk, 0)

    @pl.when(pl.program_id(2) == pl.num_programs(2) - 1)
    def _():
        sout_ref[0] = s_scr[...]


def _hgrn_seq(h, lb, nw, s0, *, n_seq, seq_len, row_block0):
    hb = HGRN_SEQ_HEADS
    width = hb * HGRN_DK
    groups = HGRN_HEADS // hb
    rb = min(HGRN_SEQ_ROWS, seq_len)
    nrb = seq_len // rb
    assert seq_len % rb == 0 and rb % CHUNK == 0 and (row_block0 * SEQ_BLOCK) % rb == 0
    rb0 = row_block0 * SEQ_BLOCK // rb
    s0_batched = s0.shape[0] != 1
    col = lambda part: (lambda b, hg, r: (rb0 + b * nrb + r, part * groups + hg))
    masks = jnp.asarray(_level_masks())
    return pl.pallas_call(
        functools.partial(_hgrn_seq_kernel, n_chunks=rb // CHUNK),
        grid=(n_seq, groups, nrb),
        in_specs=[pl.BlockSpec((rb, width), col(0)),
                  pl.BlockSpec((rb, width), col(1)),
                  pl.BlockSpec((rb, width), col(2)),
                  pl.BlockSpec((rb, width), col(3)),
                  pl.BlockSpec((1, width), lambda b, hg, r: (0, hg)),
                  pl.BlockSpec((1, width), lambda b, hg, r: (0, hg)),
                  pl.BlockSpec((1, hb, HGRN_DK, HGRN_DV),
                               (lambda b, hg, r: (b, hg, 0, 0)) if s0_batched else (lambda b, hg, r: (0, hg, 0, 0))),
                  pl.BlockSpec((LEVELS + 1, CHUNK, CHUNK), lambda b, hg, r: (0, 0, 0))],
        out_specs=[pl.BlockSpec((rb, width), lambda b, hg, r: (b * nrb + r, hg)),
                   pl.BlockSpec((1, hb, HGRN_DK, HGRN_DV), lambda b, hg, r: (b, hg, 0, 0))],
        out_shape=[jax.ShapeDtypeStruct((n_seq * seq_len, HGRN_HEADS * HGRN_DV), BF16),
                   jax.ShapeDtypeStruct((n_seq, HGRN_HEADS, HGRN_DK, HGRN_DV), F32)],
        scratch_shapes=[pltpu.VMEM((hb, HGRN_DK, HGRN_DV), F32),
                        pltpu.VMEM((hb, CHUNK, HGRN_DV), F32)],
        compiler_params=pltpu.CompilerParams(
            dimension_semantics=("parallel", "parallel", "arbitrary"), vmem_limit_bytes=VMEM_LIMIT),
        name="hgrn_seq",
    )(h, h, h, h, lb, nw, s0, masks)


def _hgrn_step_kernel(q_ref, fx_ref, i_ref, g_ref, lb_ref, nw_ref, s_ref, og_ref, sout_ref, *, n_seq, t):
    tile = (STEP_GROUP, t, HGRN_DK)
    sub = lax.broadcasted_iota(jnp.int32, (1, t, HGRN_DK), 1)

    for hd in range(s_ref.shape[1]):
        cols = slice(hd * HGRN_DK, (hd + 1) * HGRN_DK)
        lb = lb_ref[:, cols]
        nw = nw_ref[:, cols]

        def group(i, carry, hd=hd, cols=cols, lb=lb, nw=nw):
            rows = pl.ds(pl.multiple_of(i * (STEP_GROUP * t), STEP_GROUP * t), STEP_GROUP * t)
            q = q_ref[rows, cols].reshape(tile)
            v = i_ref[rows, cols].reshape(tile)
            _, logf, kk = _gates(fx_ref[rows, cols].reshape(tile), lb)
            cum = logf
            shift = 1
            while shift < t:
                cum = cum + jnp.where(sub >= shift, pltpu.roll(cum, shift, 1), 0.0)
                shift *= 2
            intra = (jnp.sum(q * kk, axis=-1, keepdims=True)) * v
            for d in range(1, t):
                valid = sub >= d
                x = jnp.exp2(jnp.where(valid, cum - pltpu.roll(cum, d, 1), 0.0))
                w = jnp.sum(jnp.where(valid, q * x * pltpu.roll(kk, d, 1), 0.0), axis=-1, keepdims=True)
                intra = intra + w * pltpu.roll(v, d, 1)
            blast = cum[:, t - 1:t, :]
            qd = q * jnp.exp2(cum)
            kd = kk * jnp.exp2(blast - cum)
            seqs = [i * STEP_GROUP + n for n in range(STEP_GROUP)]
            inter = [_dot(qd[n].astype(BF16), s_ref[b, hd].astype(BF16)) for n, b in enumerate(seqs)]
            for n, b in enumerate(seqs):
                sout_ref[b, hd] = _state_update(s_ref[b, hd], kd[n], v[n], blast[n], 8)
            o = jnp.stack(inter, axis=0) + intra
            og = _norm_gate(o, g_ref[rows, cols].reshape(tile), nw)
            og_ref[rows, cols] = og.reshape(STEP_GROUP * t, HGRN_DK).astype(og_ref.dtype)
            return carry

        lax.fori_loop(0, n_seq // STEP_GROUP, group, 0)


def _hgrn_step(h, lb, nw, s0, *, n_seq, t, seq_per_block):
    seq_per_block = min(seq_per_block, n_seq)
    assert n_seq % seq_per_block == 0 and seq_per_block % STEP_GROUP == 0 and t == 8
    rows = seq_per_block * t
    hb = STEP_HEADS
    width = hb * HGRN_DK
    groups = HGRN_HEADS // hb
    col = lambda part: (lambda i, hg: (i, part * groups + hg))
    state_spec = pl.BlockSpec((seq_per_block, hb, HGRN_DK, HGRN_DV), lambda i, hg: (i, hg, 0, 0))
    return pl.pallas_call(
        functools.partial(_hgrn_step_kernel, n_seq=seq_per_block, t=t),
        grid=(n_seq // seq_per_block, groups),
        in_specs=[pl.BlockSpec((rows, width), col(0)),
                  pl.BlockSpec((rows, width), col(1)),
                  pl.BlockSpec((rows, width), col(2)),
                  pl.BlockSpec((rows, width), col(3)),
                  pl.BlockSpec((1, width), lambda i, hg: (0, hg)),
                  pl.BlockSpec((1, width), lambda i, hg: (0, hg)),
                  state_spec],
        out_specs=[pl.BlockSpec((rows, width), lambda i, hg: (i, hg)), state_spec],
        out_shape=[jax.ShapeDtypeStruct((n_seq * t, HGRN_HEADS * HGRN_DV), BF16),
                   jax.ShapeDtypeStruct((n_seq, HGRN_HEADS, HGRN_DK, HGRN_DV), F32)],
        compiler_params=pltpu.CompilerParams(
            dimension_semantics=("parallel", "parallel"), vmem_limit_bytes=VMEM_LIMIT),
        name="hgrn_step",
    )(h, h, h, h, lb, nw, s0)


def _attn_seq_kernel(q_ref, g_ref, kc_ref, vc_ref, kp_ref, vp_ref, meta_ref, sink_ref, og_ref):
    first = pl.program_id(1) == 0
    kprev = jnp.where(first, meta_ref[:, :SWA_KV_WIDTH], kp_ref[...])
    vprev = jnp.where(first, meta_ref[:, SWA_KV_WIDTH:], vp_ref[...])
    kband = (jnp.concatenate([kprev, kc_ref[...]], axis=0) * (SWA_SCALE * LOG2E)).astype(BF16)
    vband_t = jnp.concatenate([vprev, vc_ref[...]], axis=0).T.astype(BF16)
    zeros_t = jnp.zeros((SWA_HEAD_DIM, 2 * SEQ_BLOCK), BF16)
    kj = lax.broadcasted_iota(jnp.int32, (2 * SEQ_BLOCK, SEQ_BLOCK), 0)
    qi = lax.broadcasted_iota(jnp.int32, (2 * SEQ_BLOCK, SEQ_BLOCK), 1)
    dist = SEQ_BLOCK + qi - kj
    valid = (dist >= 0) & (dist < WINDOW) & (jnp.logical_not(first) | (kj >= SEQ_BLOCK - N_META))
    madd = jnp.where(valid, 0.0, -jnp.inf)
    zeros = jnp.zeros((2 * SEQ_BLOCK, SWA_HEAD_DIM), BF16)
    pairs = SWA_GROUP // 2
    pw = 2 * SWA_HEAD_DIM

    def block_diag(x):
        return jnp.concatenate([jnp.concatenate([x, zeros], axis=1),
                                jnp.concatenate([zeros, x], axis=1)], axis=0)

    sts = []
    for kvh in range(SWA_KV_HEADS):
        ks = slice(kvh * SWA_HEAD_DIM, (kvh + 1) * SWA_HEAD_DIM)
        col0 = kvh * SWA_GROUP * SWA_HEAD_DIM
        xq = jnp.concatenate([q_ref[:, col0 + p * pw:col0 + (p + 1) * pw].astype(BF16)
                              for p in range(pairs)], axis=0)
        sts.append(_dot_nt(block_diag(kband[:, ks]), xq))
    for kvh in range(SWA_KV_HEADS):
        ks = slice(kvh * SWA_HEAD_DIM, (kvh + 1) * SWA_HEAD_DIM)
        col0 = kvh * SWA_GROUP * SWA_HEAD_DIM
        vt = vband_t[ks, :]
        v2t = jnp.concatenate([jnp.concatenate([vt, zeros_t], axis=1),
                               jnp.concatenate([zeros_t, vt], axis=1)], axis=0)
        pbs, rdens = [], []
        for pr in range(pairs):
            head = kvh * SWA_GROUP + 2 * pr
            s2 = (sts[kvh][:, pr * SEQ_BLOCK:(pr + 1) * SEQ_BLOCK].reshape(2, 2 * SEQ_BLOCK, SEQ_BLOCK)
                  + madd[None])
            sink = jnp.concatenate([jnp.full((1, 1, SEQ_BLOCK), sink_ref[head] * LOG2E, F32),
                                    jnp.full((1, 1, SEQ_BLOCK), sink_ref[head + 1] * LOG2E, F32)], axis=0)
            m = jnp.maximum(jnp.max(s2, axis=1, keepdims=True), sink)
            p = jnp.exp2(s2 - m)
            den = jnp.sum(p, axis=1, keepdims=True) + jnp.exp2(sink - m)
            pbs.append(p.astype(BF16).reshape(4 * SEQ_BLOCK, SEQ_BLOCK))
            rdens.append(1.0 / den)
        for pr in range(pairs):
            ot = _dot(v2t, pbs[pr]).reshape(2, SWA_HEAD_DIM, SEQ_BLOCK) * rdens[pr]
            o = ot.reshape(2 * SWA_HEAD_DIM, SEQ_BLOCK).T
            cs = slice(col0 + pr * pw, col0 + (pr + 1) * pw)
            og_ref[:, cs] = (o * _silu(g_ref[:, cs])).astype(og_ref.dtype)


def _attn_seq(h1, h1_meta, sinks, *, n_seq, seq_len, meta_row_block):
    nb = seq_len // SEQ_BLOCK
    width = SWA_Q_HEADS * SWA_HEAD_DIM
    kcol = 2 * width // SWA_KV_WIDTH
    return pl.pallas_call(
        _attn_seq_kernel,
        grid=(n_seq, nb),
        in_specs=[pl.BlockSpec((SEQ_BLOCK, width), lambda b, j: (b * nb + j, 0)),
                  pl.BlockSpec((SEQ_BLOCK, width), lambda b, j: (b * nb + j, 1)),
                  pl.BlockSpec((SEQ_BLOCK, SWA_KV_WIDTH), lambda b, j: (b * nb + j, kcol)),
                  pl.BlockSpec((SEQ_BLOCK, SWA_KV_WIDTH), lambda b, j: (b * nb + j, kcol + 1)),
                  pl.BlockSpec((SEQ_BLOCK, SWA_KV_WIDTH), lambda b, j: (b * nb + jnp.maximum(j - 1, 0), kcol)),
                  pl.BlockSpec((SEQ_BLOCK, SWA_KV_WIDTH), lambda b, j: (b * nb + jnp.maximum(j - 1, 0), kcol + 1)),
                  pl.BlockSpec((SEQ_BLOCK, 2 * SWA_KV_WIDTH), lambda b, j: (meta_row_block, kcol // 2)),
                  pl.BlockSpec(memory_space=pltpu.SMEM)],
        out_specs=pl.BlockSpec((SEQ_BLOCK, width), lambda b, j: (b * nb + j, 0)),
        out_shape=jax.ShapeDtypeStruct((n_seq * seq_len, width), BF16),
        compiler_params=pltpu.CompilerParams(
            dimension_semantics=("parallel", "arbitrary"), vmem_limit_bytes=VMEM_LIMIT),
        name="attn_seq",
    )(h1, h1, h1, h1, h1, h1, h1_meta, sinks)


def _attn_step_kernel(q_ref, g_ref, kn_ref, vn_ref, ck_ref, cv_ref, sink_ref, og_ref, nk_ref, nv_ref, *, n_seq, t):
    keys = WINDOW + t
    hd = SWA_HEAD_DIM
    tiles = SWA_Q_HEADS // 2
    kj = lax.broadcasted_iota(jnp.int32, (keys, SWA_Q_HEADS * t), 0)
    qt = lax.broadcasted_iota(jnp.int32, (keys, SWA_Q_HEADS * t), 1) % t
    madd = jnp.where((kj >= qt + 1) & (kj <= WINDOW + qt), 0.0, -jnp.inf)
    low = lax.broadcasted_iota(jnp.int32, (t, 2 * hd), 1) < hd
    zero_tile = jnp.zeros((t, 2 * hd), F32)
    sink = sink_ref[...] * LOG2E

    def group(i, carry):
        seqs = [i * SEQ_UNROLL + n for n in range(SEQ_UNROLL)]
        rows = [pl.ds(pl.multiple_of(b * t, t), t) for b in seqs]
        st, vall = [], []
        for b, rw in zip(seqs, rows):
            kc, vc = ck_ref[b], cv_ref[b]
            kn, vn = kn_ref[rw, :], vn_ref[rw, :]
            nk_ref[b, 0:WINDOW - t, :] = kc[t:, :]
            nk_ref[b, WINDOW - t:WINDOW, :] = kn
            nv_ref[b, 0:WINDOW - t, :] = vc[t:, :]
            nv_ref[b, WINDOW - t:WINDOW, :] = vn
            kall = jnp.concatenate([kc, kn], axis=0).astype(BF16)
            vall.append(jnp.concatenate([vc, vn], axis=0).astype(BF16))
            q = q_ref[rw, :] * (SWA_SCALE * LOG2E)
            qtile = [q[:, j * 2 * hd:(j + 1) * 2 * hd] for j in range(tiles)]
            qswap = [pltpu.roll(x, hd, 1) for x in qtile]
            groups = []
            for kvh in range(SWA_KV_HEADS):
                for gq in range(SWA_GROUP):
                    j = (kvh * SWA_GROUP + gq) // 2
                    src = qtile[j] if gq % 2 == kvh % 2 else qswap[j]
                    half = jnp.where(low, src, 0.0) if kvh % 2 == 0 else jnp.where(low, 0.0, src)
                    groups.append(jnp.concatenate(
                        [half if c == kvh // 2 else zero_tile for c in range(SWA_KV_HEADS // 2)], axis=1))
            qbd = jnp.concatenate(groups, axis=0).astype(BF16)
            st.append(_dot_nt(kall, qbd))
        pn = []
        for s in st:
            s = s + madd
            m = jnp.maximum(jnp.max(s, axis=0, keepdims=True), sink)
            p = jnp.exp2(s - m)
            den = jnp.sum(p, axis=0, keepdims=True) + jnp.exp2(sink - m)
            pn.append((p * (1.0 / den)).astype(BF16))
        of = [_dot_tn(p, vl) for p, vl in zip(pn, vall)]
        for o, rw in zip(of, rows):
            g = g_ref[rw, :]
            out = []
            for j in range(tiles):
                kvh = (2 * j) // SWA_GROUP
                ct = slice((kvh // 2) * 2 * hd, (kvh // 2 + 1) * 2 * hd)
                ra = o[(2 * j) * t:(2 * j + 1) * t, ct]
                rb = o[(2 * j + 1) * t:(2 * j + 2) * t, ct]
                if kvh % 2 == 0:
                    out.append(jnp.where(low, ra, pltpu.roll(rb, hd, 1)))
                else:
                    out.append(jnp.where(low, pltpu.roll(ra, hd, 1), rb))
            og_ref[rw, :] = (jnp.concatenate(out, axis=1) * _silu(g)).astype(og_ref.dtype)
        return carry

    lax.fori_loop(0, n_seq // SEQ_UNROLL, group, 0)


def _attn_step(h1, cache_k, cache_v, sink_cols, *, n_seq, t, seq_per_block):
    assert n_seq % seq_per_block == 0
    rows = seq_per_block * t
    width = SWA_Q_HEADS * SWA_HEAD_DIM
    kcol = 2 * width // SWA_KV_WIDTH
    cache_spec = pl.BlockSpec((seq_per_block, WINDOW, SWA_KV_WIDTH), lambda i: (i, 0, 0))
    return pl.pallas_call(
        functools.partial(_attn_step_kernel, n_seq=seq_per_block, t=t),
        grid=(n_seq // seq_per_block,),
        in_specs=[pl.BlockSpec((rows, width), lambda i: (i, 0)),
                  pl.BlockSpec((rows, width), lambda i: (i, 1)),
                  pl.BlockSpec((rows, SWA_KV_WIDTH), lambda i: (i, kcol)),
                  pl.BlockSpec((rows, SWA_KV_WIDTH), lambda i: (i, kcol + 1)),
                  cache_spec, cache_spec,
                  pl.BlockSpec((1, SWA_Q_HEADS * t), lambda i: (0, 0))],
        out_specs=[pl.BlockSpec((rows, width), lambda i: (i, 0)), cache_spec, cache_spec],
        out_shape=[jax.ShapeDtypeStruct((n_seq * t, width), BF16),
                   jax.ShapeDtypeStruct(cache_k.shape, F32),
                   jax.ShapeDtypeStruct(cache_v.shape, F32)],
        compiler_params=pltpu.CompilerParams(
            dimension_semantics=("parallel",), vmem_limit_bytes=VMEM_LIMIT),
        name="attn_step",
    )(h1, h1, h1, h1, cache_k, cache_v, sink_cols)


def kernel(x_prompt, x_sample, state_hgrn, cache_swa_k, cache_swa_v, meta_tokens,
           hgrn_w_in, hgrn_lb_logits, hgrn_norm_w, hgrn_w_out,
           swa_w_in, swa_sinks, swa_w_out, ln_g, ln_b):
    out_dtype = x_prompt.dtype
    bsz, seq, d = x_prompt.shape
    dec_b, dec_t, _ = x_sample.shape
    n_p = bsz * seq
    n_s = dec_b * dec_t
    width = SWA_Q_HEADS * SWA_HEAD_DIM

    in_tile = SHORT_COL_TILE
    q_tiles = width // in_tile
    kv_tile = 2 * SWA_KV_WIDTH // in_tile
    assert width % in_tile == 0 and (2 * SWA_KV_WIDTH) % in_tile == 0 and kv_tile == 1
    regroup = lambda j: jnp.where(j < q_tiles, j, jnp.where(j < 2 * q_tiles, j + kv_tile, q_tiles))
    lb = jnp.cumsum(jax.nn.softmax(hgrn_lb_logits.astype(F32), axis=0), axis=0)[0:1]
    nw = hgrn_norm_w[0].astype(F32).reshape(1, -1)
    sinks = swa_sinks[0].astype(F32).reshape(1, SWA_Q_HEADS)
    sink_cols = jnp.repeat(sinks, dec_t, axis=1)
    g0, b0 = ln_g[0:1].astype(F32), ln_b[0:1].astype(F32)
    g1, b1 = ln_g[1:2].astype(F32), ln_b[1:2].astype(F32)

    x_p = x_prompt.astype(F32).reshape(n_p, d)
    meta_block = jnp.concatenate(
        [jnp.zeros((SEQ_BLOCK - N_META, d), F32), meta_tokens.astype(F32)], axis=0)
    x_s = x_sample.astype(F32).reshape(n_s, d)
    meta_blk = n_s // SEQ_BLOCK

    h_sm, w_in0 = _matmul_cast(x_s, meta_block, hgrn_w_in[0].astype(F32), xs_rows=n_s, xm_block=0, tn=SHORT_IN0_COL_TILE)
    h_p, w_out0, w_out1 = _matmul(x_p, w_in0, tm=PROJ_ROW_TILE, tn=PROJ_IN0_COL_TILE,
                                  cast_along=(hgrn_w_out[0].astype(F32), swa_w_out[0].astype(F32)))
    zero_state = jnp.zeros((1, HGRN_HEADS, HGRN_DK, HGRN_DV), F32)
    og_meta, s_meta = _hgrn_seq(h_sm, lb, nw, zero_state, n_seq=1, seq_len=SEQ_BLOCK, row_block0=meta_blk)
    og_s, st_s = _hgrn_step(h_sm, lb, nw, state_hgrn[0].astype(F32), n_seq=dec_b, t=dec_t, seq_per_block=STEP_STATE_SEQS)
    og_p, st_p = _hgrn_seq(h_p, lb, nw, s_meta, n_seq=bsz, seq_len=seq, row_block0=0)
    x1_sm = _matmul_deepnorm_tail(og_s, og_meta, w_out0, x_s, meta_block, g0, b0)
    x1_p = _matmul_deepnorm(og_p, w_out0, x_p, g0, b0, tm=PROJ_OUT_ROW_TILE)

    h1_sm, w_in1 = _matmul_cast(x1_sm, x1_sm, swa_w_in[0].astype(F32), xs_rows=n_s, xm_block=meta_blk,
                                tn=in_tile, col_tile=regroup)
    h1_p = _matmul(x1_p, w_in1, tm=PROJ_ROW_TILE, tn=PROJ_IN1_COL_TILE)
    og1_p = _attn_seq(h1_p, h1_sm, sinks.reshape(-1), n_seq=bsz, seq_len=seq, meta_row_block=meta_blk)
    ck = cache_swa_k[0].astype(F32).reshape(dec_b, WINDOW, SWA_KV_WIDTH)
    cv = cache_swa_v[0].astype(F32).reshape(dec_b, WINDOW, SWA_KV_WIDTH)
    og1_s, nk_s, nv_s = _attn_step(h1_sm, ck, cv, sink_cols, n_seq=dec_b, t=dec_t, seq_per_block=STEP_CACHE_SEQS)
    y_p = _matmul_deepnorm(og1_p, w_out1, x1_p, g1, b1, tm=PROJ_OUT_ROW_TILE)
    y_s = _matmul_deepnorm(og1_s, w_out1, x1_sm, g1, b1, tm=PROJ_OUT_ROW_TILE)

    kv_p = h1_p.reshape(bsz, seq, -1)[:, seq - WINDOW:, 2 * width:]
    cache_shape = (1, bsz, WINDOW, SWA_KV_HEADS, SWA_HEAD_DIM)
    return (y_p.reshape(bsz, seq, d).astype(out_dtype),
            y_s.reshape(dec_b, dec_t, d).astype(out_dtype),
            st_p[None].astype(out_dtype),
            st_s[None].astype(out_dtype),
            kv_p[..., :SWA_KV_WIDTH].reshape(cache_shape).astype(out_dtype),
            kv_p[..., SWA_KV_WIDTH:].reshape(cache_shape).astype(out_dtype),
            nk_s.reshape((1,) + cache_swa_k.shape[1:]).astype(out_dtype),
            nv_s.reshape((1,) + cache_swa_v.shape[1:]).astype(out_dtype))
```

```python
import functools

import numpy as np
import jax
import jax.numpy as jnp
from jax import lax
from jax.experimental import pallas as pl
from jax.experimental.pallas import tpu as pltpu

F32 = jnp.float32
BF16 = jnp.bfloat16

D_MODEL = 2048
N_META = 16
DEPTH = 2
HGRN_HEADS = 16
HGRN_DK = 128
HGRN_DV = 128
SWA_Q_HEADS = 32
SWA_KV_HEADS = 4
SWA_GROUP = 8
SWA_HEAD_DIM = 64
SWA_KV_WIDTH = SWA_KV_HEADS * SWA_HEAD_DIM
SWA_SCALE = SWA_HEAD_DIM ** -0.5
WINDOW = 128
DEEPNORM_ALPHA = (2.0 * DEPTH) ** 0.25
LN_EPS = 1e-5
RMS_EPS = 1e-6
LOG2E = 1.4426950408889634

CHUNK = 128
LEVELS = 7
MAX_FACTORED_LOG2_DECAY = 80.0
MASK_ROWS = 16
SEQ_BLOCK = 128
LN_SUB_ROWS = 128
STEP_GROUP = 32
SEQ_UNROLL = 8
HGRN_SEQ_HEADS = 16
HGRN_SEQ_ROWS = 256

V7X_VMEM_BYTES = 64 * 1024 * 1024
VMEM_LIMIT = V7X_VMEM_BYTES - 4 * 1024 * 1024
PROJ_ROW_TILE = 1024
PROJ_IN0_COL_TILE = 2048
PROJ_IN1_COL_TILE = 2304
PROJ_OUT_ROW_TILE = 512
SHORT_IN1_COL_TILE = 1536
SHORT_IN0_COL_TILE = 1024
STEP_STATE_SEQS = 64
STEP_HEADS = 2
STEP_CACHE_SEQS = 32


def _dot(a, b):
    return jnp.dot(a, b, preferred_element_type=F32)


def _dot_nt(a, b):
    return lax.dot_general(a, b, (((1,), (1,)), ((), ())), preferred_element_type=F32)


def _dot_tn(a, b):
    return lax.dot_general(a, b, (((0,), (0,)), ((), ())), preferred_element_type=F32)


def _silu(x):
    return x * jax.nn.sigmoid(x)


def _mm_kernel(x_ref, w_ref, *refs, col_orders):
    n_side = len(col_orders)
    side_in, o_ref, side_out = refs[:n_side], refs[n_side], refs[n_side + 1:]
    o_ref[...] = _dot(x_ref[...].astype(BF16), w_ref[...])
    for src, dst, order in zip(side_in, side_out, col_orders):
        if order is None:
            dst[...] = src[...].astype(BF16)
        else:
            dst[...] = jnp.concatenate([src[:, a:b] for a, b in order], axis=1).astype(BF16)


def _row_tile(n, want):
    if n <= want:
        return n
    return max(t for t in range(16, want + 1, 16) if n % t == 0)


def _matmul(x, w, *, tm, tn, cast_along=(), col_orders=None):
    col_orders = tuple(col_orders or (None,) * len(cast_along))
    n, k = x.shape
    e = w.shape[1]
    tm = _row_tile(n, tm)
    assert n % tm == 0 and e % tn == 0
    ncol = e // tn
    steps = (n // tm) * ncol
    side_specs = []
    for a in cast_along:
        assert a.shape[0] % (16 * steps) == 0
        side_specs.append(pl.BlockSpec((a.shape[0] // steps, a.shape[1]), lambda i, j: (i * ncol + j, 0)))
    out = pl.pallas_call(
        functools.partial(_mm_kernel, col_orders=col_orders),
        grid=(n // tm, ncol),
        in_specs=[pl.BlockSpec((tm, k), lambda i, j: (i, 0)),
                  pl.BlockSpec((k, tn), lambda i, j: (0, j))] + side_specs,
        out_specs=[pl.BlockSpec((tm, tn), lambda i, j: (i, j))] + side_specs,
        out_shape=[jax.ShapeDtypeStruct((n, e), F32)] + [jax.ShapeDtypeStruct(a.shape, BF16) for a in cast_along],
        compiler_params=pltpu.CompilerParams(
            dimension_semantics=("parallel", "parallel"), vmem_limit_bytes=VMEM_LIMIT),
        name="proj_in",
    )(x, w, *cast_along)
    return out if cast_along else out[0]


def _mm_cast_kernel(xs_ref, xm_ref, w_ref, o_ref, wb_ref, xb_scr):
    @pl.when(pl.program_id(0) == 0)
    def _():
        n_s = xs_ref.shape[0]
        xb_scr[0:n_s, :] = xs_ref[...].astype(BF16)
        xb_scr[n_s:, :] = xm_ref[...].astype(BF16)

    wb = w_ref[...].astype(BF16)
    wb_ref[...] = wb
    o_ref[...] = _dot(xb_scr[...], wb)


def _matmul_cast(xs, xm, w, *, xs_rows, xm_block, tn):
    k, e = w.shape
    n = xs_rows + SEQ_BLOCK
    assert e % tn == 0
    return pl.pallas_call(
        _mm_cast_kernel,
        grid=(e // tn,),
        in_specs=[pl.BlockSpec((xs_rows, k), lambda j: (0, 0), pipeline_mode=pl.Buffered(1)),
                  pl.BlockSpec((SEQ_BLOCK, k), lambda j: (xm_block, 0), pipeline_mode=pl.Buffered(1)),
                  pl.BlockSpec((k, tn), lambda j: (0, j))],
        out_specs=[pl.BlockSpec((n, tn), lambda j: (0, j)),
                   pl.BlockSpec((k, tn), lambda j: (0, j))],
        out_shape=[jax.ShapeDtypeStruct((n, e), F32), jax.ShapeDtypeStruct((k, e), BF16)],
        scratch_shapes=[pltpu.VMEM((n, k), BF16)],
        compiler_params=pltpu.CompilerParams(
            dimension_semantics=("arbitrary",), vmem_limit_bytes=VMEM_LIMIT),
        name="proj_in_cast",
    )(xs, xm, w)


def _mm_ln_tail_kernel(a_ref, at_ref, w_ref, x_ref, xt_ref, g_ref, b_ref, o_ref):
    tail = pl.program_id(0) == pl.num_programs(0) - 1
    a = jnp.where(tail, at_ref[...], a_ref[...])
    x = jnp.where(tail, xt_ref[...], x_ref[...])
    z = DEEPNORM_ALPHA * x + _dot(a, w_ref[...])
    mu = jnp.mean(z, axis=-1, keepdims=True)
    zc = z - mu
    var = jnp.mean(zc * zc, axis=-1, keepdims=True)
    o_ref[...] = zc * lax.rsqrt(var + LN_EPS) * g_ref[...] + b_ref[...]


def _matmul_deepnorm_tail(a, a_tail, w, x, x_tail, g, b):
    n, k = a.shape
    d = w.shape[1]
    tm = SEQ_BLOCK
    last = n // tm - 1
    body = lambda i: (jnp.minimum(i, last), 0)
    const = lambda i: (0, 0)
    return pl.pallas_call(
        _mm_ln_tail_kernel,
        grid=(n // tm + 1,),
        in_specs=[pl.BlockSpec((tm, k), body), pl.BlockSpec((tm, k), const),
                  pl.BlockSpec((k, d), const),
                  pl.BlockSpec((tm, d), body), pl.BlockSpec((tm, d), const),
                  pl.BlockSpec((1, d), const), pl.BlockSpec((1, d), const)],
        out_specs=pl.BlockSpec((tm, d), lambda i: (i, 0)),
        out_shape=jax.ShapeDtypeStruct((n + tm, d), F32),
        compiler_params=pltpu.CompilerParams(
            dimension_semantics=("parallel",), vmem_limit_bytes=VMEM_LIMIT),
        name="proj_out_deepnorm_tail",
    )(a, a_tail, w, x, x_tail, g, b)


def _mm_ln_kernel(a_ref, w_ref, x_ref, g_ref, b_ref, o_ref):
    tm = a_ref.shape[0]
    sub = LN_SUB_ROWS if tm % LN_SUB_ROWS == 0 else tm
    for r0 in range(0, tm, sub):
        rows = slice(r0, r0 + sub)
        z = DEEPNORM_ALPHA * x_ref[rows, :] + _dot(a_ref[rows, :], w_ref[...])
        mu = jnp.mean(z, axis=-1, keepdims=True)
        zc = z - mu
        var = jnp.mean(zc * zc, axis=-1, keepdims=True)
        o_ref[rows, :] = zc * lax.rsqrt(var + LN_EPS) * g_ref[...] + b_ref[...]


def _matmul_deepnorm(a, w, x, g, b, *, tm):
    n, k = a.shape
    d = w.shape[1]
    tm = _row_tile(n, tm)
    return pl.pallas_call(
        _mm_ln_kernel,
        grid=(n // tm,),
        in_specs=[pl.BlockSpec((tm, k), lambda i: (i, 0)),
                  pl.BlockSpec((k, d), lambda i: (0, 0)),
                  pl.BlockSpec((tm, d), lambda i: (i, 0)),
                  pl.BlockSpec((1, d), lambda i: (0, 0)),
                  pl.BlockSpec((1, d), lambda i: (0, 0))],
        out_specs=pl.BlockSpec((tm, d), lambda i: (i, 0)),
        out_shape=jax.ShapeDtypeStruct((n, d), F32),
        compiler_params=pltpu.CompilerParams(
            dimension_semantics=("parallel",), vmem_limit_bytes=VMEM_LIMIT),
        name="proj_out_deepnorm",
    )(a, w, x, g, b)


def _level_masks():
    t = np.arange(CHUNK)
    out = np.zeros((LEVELS + 1, CHUNK, CHUNK), np.float32)
    out[0] = t[:, None] > t[None, :]
    for l in range(LEVELS):
        h = 1 << l
        same = (t[:, None] >> (l + 1)) == (t[None, :] >> (l + 1))
        out[l + 1] = same & ((t[:, None] & h) != 0) & ((t[None, :] & h) == 0)
    return out


def _gates(fx, lb):
    f = lb + (1.0 - lb) * jax.nn.sigmoid(fx)
    return f, jnp.log2(f), 1.0 - f


def _prefix_scan(logf):
    tiles = (CHUNK // 8, 8, HGRN_DK)
    c = logf.reshape(tiles)
    sub = lax.broadcasted_iota(jnp.int32, (1, 8, HGRN_DK), 1)
    bcast = lambda x, r: jnp.broadcast_to(x[:, r:r + 1, :], tiles)
    c = c + jnp.where((sub & 1) != 0, pltpu.roll(c, 1, 1), 0.0)
    c = c + jnp.where((sub & 2) != 0, jnp.where(sub < 4, bcast(c, 1), bcast(c, 5)), 0.0)
    c = c + jnp.where((sub & 4) != 0, bcast(c, 3), 0.0)
    c = c.reshape(CHUNK, HGRN_DK)
    for level in range(3, LEVELS):
        half = 1 << level
        pieces = []
        for r0 in range(0, CHUNK, 2 * half):
            pieces += [c[r0:r0 + half], c[r0 + half:r0 + 2 * half] + c[r0 + half - 1:r0 + half, :]]
        c = jnp.concatenate(pieces, axis=0)
    return c


def _small_levels(logf, f, q, kk):
    tiles = (CHUNK // 8, 8, HGRN_DK)
    c, f3, q3, k3 = (x.reshape(tiles) for x in (logf, f, q, kk))
    sub = lax.broadcasted_iota(jnp.int32, (1, 8, HGRN_DK), 1)
    bcast = lambda x, r: jnp.broadcast_to(x[:, r:r + 1, :], tiles)
    up = (sub & 1) != 0
    zs = [jnp.where(up, q3 * f3, k3)]
    c = c + jnp.where(up, pltpu.roll(c, 1, 1), 0.0)
    for level, tot in ((1, lambda c: jnp.where(sub < 4, bcast(c, 1), bcast(c, 5))),
                       (2, lambda c: bcast(c, 3))):
        up = (sub & (1 << level)) != 0
        t = tot(c)
        zs.append(jnp.where(up, q3, k3) * jnp.exp2(jnp.where(up, c, t - c)))
        c = c + jnp.where(up, t, 0.0)
    return [z.reshape(CHUNK, HGRN_DK) for z in zs], c.reshape(CHUNK, HGRN_DK)


def _level_large(cum, q, kk, level):
    half = 1 << level
    args, bases, cums = [], [], []
    for r0 in range(0, CHUNK, 2 * half):
        lo, up = slice(r0, r0 + half), slice(r0 + half, r0 + 2 * half)
        tot = cum[r0 + half - 1:r0 + half, :]
        args += [tot - cum[lo], cum[up]]
        bases += [kk[lo], q[up]]
        cums += [cum[lo], cum[up] + tot]
    x = jnp.exp2(jnp.concatenate(args, axis=0))
    return jnp.concatenate(bases, axis=0) * x, jnp.concatenate(cums, axis=0)


def _split3(x):
    hi = x.astype(BF16)
    r = x - hi.astype(F32)
    mid = r.astype(BF16)
    lo = (r - mid.astype(F32)).astype(BF16)
    return hi, mid, lo


def _state_update(s_old, kd, v, blast, pad_rows):
    c = kd.shape[0]
    hi, mid, lo = _split3(jnp.exp2(blast))
    row = lax.broadcasted_iota(jnp.int32, (pad_rows, HGRN_DK), 0)
    dec = jnp.where(row == 0, hi.astype(F32),
                    jnp.where(row == 1, mid.astype(F32), jnp.where(row == 2, lo.astype(F32), 0.0)))
    piece = BF16 if c % 16 == 0 and pad_rows % 16 == 0 else F32
    lhs = jnp.concatenate([kd.astype(piece), dec.astype(piece)], axis=0).astype(BF16)
    rhs = jnp.concatenate(
        [jnp.concatenate([v.astype(piece), jnp.zeros((c, HGRN_DV), piece)], axis=1),
         jnp.concatenate([jnp.zeros((pad_rows, HGRN_DV), piece), jnp.ones((pad_rows, HGRN_DV), piece)], axis=1)],
        axis=0).astype(BF16)
    both = _dot_tn(lhs, rhs)
    return both[:, HGRN_DV:] * s_old + both[:, :HGRN_DV]


def _norm_gate(o, g, nw):
    o = o * lax.rsqrt(jnp.mean(o * o, axis=-1, keepdims=True) + RMS_EPS) * nw
    return o * _silu(g)


def _hgrn_seq_kernel(q_ref, fx_ref, i_ref, g_ref, lb_ref, nw_ref, s0_ref, m_ref,
                     og_ref, sout_ref, s_scr, o_scr, *, n_chunks):
    heads = s_scr.shape[0]

    @pl.when(pl.program_id(2) == 0)
    def _():
        s_scr[...] = s0_ref[0]

    def chunk(c, carry):
        rows = pl.ds(pl.multiple_of(c * CHUNK, CHUNK), CHUNK)
        hcols = [slice(hd * HGRN_DK, (hd + 1) * HGRN_DK) for hd in range(heads)]
        q = [q_ref[rows, cs] for cs in hcols]
        v = [i_ref[rows, cs] for cs in hcols]
        gates = [_gates(fx_ref[rows, cs], lb_ref[:, cs]) for cs in hcols]
        kk = [gt[2] for gt in gates]
        cum = [_prefix_scan(gt[1]) for gt in gates]
        mid = CHUNK // 2 - 1
        spread = None
        for b in cum:
            s_hd = jnp.maximum(b[0:1, :] - b[mid:mid + 1, :], b[mid:mid + 1, :] - b[CHUNK - 1:CHUNK, :])
            spread = s_hd if spread is None else jnp.maximum(spread, s_hd)
        factorable = jnp.max(spread) <= MAX_FACTORED_LOG2_DECAY

        vb = [x.astype(BF16) for x in v]
        for hd in range(heads):
            d = cum[hd] - cum[hd][mid:mid + 1, :]
            qe = (q[hd] * jnp.exp2(d)).astype(BF16)
            ke = (kk[hd] * jnp.exp2(-d)).astype(BF16)
            amat = jnp.where(m_ref[0] > 0.5, _dot_nt(qe, ke), 0.0).astype(BF16)
            blast = cum[hd][CHUNK - 1:CHUNK, :]
            qd = (q[hd] * jnp.exp2(cum[hd])).astype(BF16)
            kd = kk[hd] * jnp.exp2(blast - cum[hd])
            s_old = s_scr[hd]
            base = _dot(qd, s_old.astype(BF16)) + jnp.sum(q[hd] * kk[hd], axis=-1, keepdims=True) * v[hd]
            o_scr[hd] = base
            s_scr[hd] = _state_update(s_old, kd, vb[hd], blast, 16)
            og_ref[rows, hcols[hd]] = _norm_gate(
                base + _dot(amat, vb[hd]), g_ref[rows, hcols[hd]], nw_ref[:, hcols[hd]]).astype(og_ref.dtype)

        @pl.when(jnp.logical_not(factorable))
        def _():
            small = [_small_levels(gt[1], gt[0], q[hd], kk[hd]) for hd, gt in enumerate(gates)]
            zs = [sm[0] for sm in small]
            part = [sm[1] for sm in small]
            for level in range(3, LEVELS):
                for hd in range(heads):
                    z, part[hd] = _level_large(part[hd], q[hd], kk[hd], level)
                    zs[hd].append(z)
            nblk = CHUNK // MASK_ROWS
            a = [[None] * nblk for _ in range(heads)]
            for level in range(LEVELS):
                step = 1 << (level - 4) if level >= 4 else 0
                blocks = [i for i in range(nblk) if level < 4 or (i & step)]
                for hd in range(heads):
                    z = zs[hd][level].astype(BF16)
                    lhs = z if level < 4 else jnp.concatenate(
                        [z[i * MASK_ROWS:(i + 1) * MASK_ROWS] for i in blocks], axis=0)
                    term = _dot_nt(lhs, z)
                    for n, i in enumerate(blocks):
                        t = (term[n * MASK_ROWS:(n + 1) * MASK_ROWS]
                             * m_ref[level + 1, i * MASK_ROWS:(i + 1) * MASK_ROWS, :])
                        a[hd][i] = t if a[hd][i] is None else a[hd][i] + t
            for hd in range(heads):
                amat = jnp.concatenate(a[hd], axis=0).astype(BF16)
                og_ref[rows, hcols[hd]] = _norm_gate(
                    o_scr[hd] + _dot(amat, vb[hd]),
                    g_ref[rows, hcols[hd]], nw_ref[:, hcols[hd]]).astype(og_ref.dtype)
        return carry

    lax.fori_loop(0, n_chunks, chunk, 0)

    @pl.when(pl.program_id(2) == pl.num_programs(2) - 1)
    def _():
        sout_ref[0] = s_scr[...]


def _hgrn_seq(h, lb, nw, s0, *, n_seq, seq_len, row_block0):
    hb = HGRN_SEQ_HEADS
    width = hb * HGRN_DK
    groups = HGRN_HEADS // hb
    rb = min(HGRN_SEQ_ROWS, seq_len)
    nrb = seq_len // rb
    assert seq_len % rb == 0 and rb % CHUNK == 0 and (row_block0 * SEQ_BLOCK) % rb == 0
    rb0 = row_block0 * SEQ_BLOCK // rb
    s0_batched = s0.shape[0] != 1
    col = lambda part: (lambda b, hg, r: (rb0 + b * nrb + r, part * groups + hg))
    masks = jnp.asarray(_level_masks())
    return pl.pallas_call(
        functools.partial(_hgrn_seq_kernel, n_chunks=rb // CHUNK),
        grid=(n_seq, groups, nrb),
        in_specs=[pl.BlockSpec((rb, width), col(0)),
                  pl.BlockSpec((rb, width), col(1)),
                  pl.BlockSpec((rb, width), col(2)),
                  pl.BlockSpec((rb, width), col(3)),
                  pl.BlockSpec((1, width), lambda b, hg, r: (0, hg)),
                  pl.BlockSpec((1, width), lambda b, hg, r: (0, hg)),
                  pl.BlockSpec((1, hb, HGRN_DK, HGRN_DV),
                               (lambda b, hg, r: (b, hg, 0, 0)) if s0_batched else (lambda b, hg, r: (0, hg, 0, 0))),
                  pl.BlockSpec((LEVELS + 1, CHUNK, CHUNK), lambda b, hg, r: (0, 0, 0))],
        out_specs=[pl.BlockSpec((rb, width), lambda b, hg, r: (b * nrb + r, hg)),
                   pl.BlockSpec((1, hb, HGRN_DK, HGRN_DV), lambda b, hg, r: (b, hg, 0, 0))],
        out_shape=[jax.ShapeDtypeStruct((n_seq * seq_len, HGRN_HEADS * HGRN_DV), BF16),
                   jax.ShapeDtypeStruct((n_seq, HGRN_HEADS, HGRN_DK, HGRN_DV), F32)],
        scratch_shapes=[pltpu.VMEM((hb, HGRN_DK, HGRN_DV), F32),
                        pltpu.VMEM((hb, CHUNK, HGRN_DV), F32)],
        compiler_params=pltpu.CompilerParams(
            dimension_semantics=("parallel", "parallel", "arbitrary"), vmem_limit_bytes=VMEM_LIMIT),
        name="hgrn_seq",
    )(h, h, h, h, lb, nw, s0, masks)


def _hgrn_step_kernel(q_ref, fx_ref, i_ref, g_ref, lb_ref, nw_ref, s_ref, og_ref, sout_ref, *, n_seq, t):
    tile = (STEP_GROUP, t, HGRN_DK)
    sub = lax.broadcasted_iota(jnp.int32, (1, t, HGRN_DK), 1)

    for hd in range(s_ref.shape[1]):
        cols = slice(hd * HGRN_DK, (hd + 1) * HGRN_DK)
        lb = lb_ref[:, cols]
        nw = nw_ref[:, cols]

        def group(i, carry, hd=hd, cols=cols, lb=lb, nw=nw):
            rows = pl.ds(pl.multiple_of(i * (STEP_GROUP * t), STEP_GROUP * t), STEP_GROUP * t)
            q = q_ref[rows, cols].reshape(tile)
            v = i_ref[rows, cols].reshape(tile)
            _, logf, kk = _gates(fx_ref[rows, cols].reshape(tile), lb)
            cum = logf
            shift = 1
            while shift < t:
                cum = cum + jnp.where(sub >= shift, pltpu.roll(cum, shift, 1), 0.0)
                shift *= 2
            intra = (jnp.sum(q * kk, axis=-1, keepdims=True)) * v
            for d in range(1, t):
                valid = sub >= d
                x = jnp.exp2(jnp.where(valid, cum - pltpu.roll(cum, d, 1), 0.0))
                w = jnp.sum(jnp.where(valid, q * x * pltpu.roll(kk, d, 1), 0.0), axis=-1, keepdims=True)
                intra = intra + w * pltpu.roll(v, d, 1)
            blast = cum[:, t - 1:t, :]
            qd = q * jnp.exp2(cum)
            kd = kk * jnp.exp2(blast - cum)
            seqs = [i * STEP_GROUP + n for n in range(STEP_GROUP)]
            inter = [_dot(qd[n].astype(BF16), s_ref[b, hd].astype(BF16)) for n, b in enumerate(seqs)]
            for n, b in enumerate(seqs):
                sout_ref[b, hd] = _state_update(s_ref[b, hd], kd[n], v[n], blast[n], 8)
            o = jnp.stack(inter, axis=0) + intra
            og = _norm_gate(o, g_ref[rows, cols].reshape(tile), nw)
            og_ref[rows, cols] = og.reshape(STEP_GROUP * t, HGRN_DK).astype(og_ref.dtype)
            return carry

        lax.fori_loop(0, n_seq // STEP_GROUP, group, 0)


def _hgrn_step(h, lb, nw, s0, *, n_seq, t, seq_per_block):
    seq_per_block = min(seq_per_block, n_seq)
    assert n_seq % seq_per_block == 0 and seq_per_block % STEP_GROUP == 0 and t == 8
    rows = seq_per_block * t
    hb = STEP_HEADS
    width = hb * HGRN_DK
    groups = HGRN_HEADS // hb
    col = lambda part: (lambda i, hg: (i, part * groups + hg))
    state_spec = pl.BlockSpec((seq_per_block, hb, HGRN_DK, HGRN_DV), lambda i, hg: (i, hg, 0, 0))
    return pl.pallas_call(
        functools.partial(_hgrn_step_kernel, n_seq=seq_per_block, t=t),
        grid=(n_seq // seq_per_block, groups),
        in_specs=[pl.BlockSpec((rows, width), col(0)),
                  pl.BlockSpec((rows, width), col(1)),
                  pl.BlockSpec((rows, width), col(2)),
                  pl.BlockSpec((rows, width), col(3)),
                  pl.BlockSpec((1, width), lambda i, hg: (0, hg)),
                  pl.BlockSpec((1, width), lambda i, hg: (0, hg)),
                  state_spec],
        out_specs=[pl.BlockSpec((rows, width), lambda i, hg: (i, hg)), state_spec],
        out_shape=[jax.ShapeDtypeStruct((n_seq * t, HGRN_HEADS * HGRN_DV), BF16),
                   jax.ShapeDtypeStruct((n_seq, HGRN_HEADS, HGRN_DK, HGRN_DV), F32)],
        compiler_params=pltpu.CompilerParams(
            dimension_semantics=("parallel", "parallel"), vmem_limit_bytes=VMEM_LIMIT),
        name="hgrn_step",
    )(h, h, h, h, lb, nw, s0)


def _attn_seq_kernel(q_ref, g_ref, kc_ref, vc_ref, kp_ref, vp_ref, meta_ref, sink_ref, og_ref):
    first = pl.program_id(1) == 0
    kprev = jnp.where(first, meta_ref[:, :SWA_KV_WIDTH], kp_ref[...])
    vprev = jnp.where(first, meta_ref[:, SWA_KV_WIDTH:], vp_ref[...])
    kband = (jnp.concatenate([kprev, kc_ref[...]], axis=0) * (SWA_SCALE * LOG2E)).astype(BF16)
    vband_t = jnp.concatenate([vprev, vc_ref[...]], axis=0).T.astype(BF16)
    zeros_t = jnp.zeros((SWA_HEAD_DIM, 2 * SEQ_BLOCK), BF16)
    kj = lax.broadcasted_iota(jnp.int32, (2 * SEQ_BLOCK, SEQ_BLOCK), 0)
    qi = lax.broadcasted_iota(jnp.int32, (2 * SEQ_BLOCK, SEQ_BLOCK), 1)
    dist = SEQ_BLOCK + qi - kj
    valid = (dist >= 0) & (dist < WINDOW) & (jnp.logical_not(first) | (kj >= SEQ_BLOCK - N_META))
    madd = jnp.where(valid, 0.0, -jnp.inf)
    zeros = jnp.zeros((2 * SEQ_BLOCK, SWA_HEAD_DIM), BF16)
    pairs = SWA_GROUP // 2
    pw = 2 * SWA_HEAD_DIM

    def block_diag(x):
        return jnp.concatenate([jnp.concatenate([x, zeros], axis=1),
                                jnp.concatenate([zeros, x], axis=1)], axis=0)

    sts = []
    for kvh in range(SWA_KV_HEADS):
        ks = slice(kvh * SWA_HEAD_DIM, (kvh + 1) * SWA_HEAD_DIM)
        col0 = kvh * SWA_GROUP * SWA_HEAD_DIM
        xq = jnp.concatenate([q_ref[:, col0 + p * pw:col0 + (p + 1) * pw].astype(BF16)
                              for p in range(pairs)], axis=0)
        sts.append(_dot_nt(block_diag(kband[:, ks]), xq))
    for kvh in range(SWA_KV_HEADS):
        ks = slice(kvh * SWA_HEAD_DIM, (kvh + 1) * SWA_HEAD_DIM)
        col0 = kvh * SWA_GROUP * SWA_HEAD_DIM
        vt = vband_t[ks, :]
        v2t = jnp.concatenate([jnp.concatenate([vt, zeros_t], axis=1),
                               jnp.concatenate([zeros_t, vt], axis=1)], axis=0)
        pbs, rdens = [], []
        for pr in range(pairs):
            head = kvh * SWA_GROUP + 2 * pr
            s2 = (sts[kvh][:, pr * SEQ_BLOCK:(pr + 1) * SEQ_BLOCK].reshape(2, 2 * SEQ_BLOCK, SEQ_BLOCK)
                  + madd[None])
            sink = jnp.concatenate([jnp.full((1, 1, SEQ_BLOCK), sink_ref[head] * LOG2E, F32),
                                    jnp.full((1, 1, SEQ_BLOCK), sink_ref[head + 1] * LOG2E, F32)], axis=0)
            m = jnp.maximum(jnp.max(s2, axis=1, keepdims=True), sink)
            p = jnp.exp2(s2 - m)
            den = jnp.sum(p, axis=1, keepdims=True) + jnp.exp2(sink - m)
            pbs.append(p.astype(BF16).reshape(4 * SEQ_BLOCK, SEQ_BLOCK))
            rdens.append(1.0 / den)
        for pr in range(pairs):
            ot = _dot(v2t, pbs[pr]).reshape(2, SWA_HEAD_DIM, SEQ_BLOCK) * rdens[pr]
            o = ot.reshape(2 * SWA_HEAD_DIM, SEQ_BLOCK).T
            cs = slice(col0 + pr * pw, col0 + (pr + 1) * pw)
            og_ref[:, cs] = (o * _silu(g_ref[:, cs])).astype(og_ref.dtype)


def _attn_seq(h1, h1_meta, sinks, *, n_seq, seq_len, meta_row_block):
    nb = seq_len // SEQ_BLOCK
    width = SWA_Q_HEADS * SWA_HEAD_DIM
    kcol = 2 * width // SWA_KV_WIDTH
    return pl.pallas_call(
        _attn_seq_kernel,
        grid=(n_seq, nb),
        in_specs=[pl.BlockSpec((SEQ_BLOCK, width), lambda b, j: (b * nb + j, 0)),
                  pl.BlockSpec((SEQ_BLOCK, width), lambda b, j: (b * nb + j, 1)),
                  pl.BlockSpec((SEQ_BLOCK, SWA_KV_WIDTH), lambda b, j: (b * nb + j, kcol)),
                  pl.BlockSpec((SEQ_BLOCK, SWA_KV_WIDTH), lambda b, j: (b * nb + j, kcol + 1)),
                  pl.BlockSpec((SEQ_BLOCK, SWA_KV_WIDTH), lambda b, j: (b * nb + jnp.maximum(j - 1, 0), kcol)),
                  pl.BlockSpec((SEQ_BLOCK, SWA_KV_WIDTH), lambda b, j: (b * nb + jnp.maximum(j - 1, 0), kcol + 1)),
                  pl.BlockSpec((SEQ_BLOCK, 2 * SWA_KV_WIDTH), lambda b, j: (meta_row_block, kcol // 2)),
                  pl.BlockSpec(memory_space=pltpu.SMEM)],
        out_specs=pl.BlockSpec((SEQ_BLOCK, width), lambda b, j: (b * nb + j, 0)),
        out_shape=jax.ShapeDtypeStruct((n_seq * seq_len, width), BF16),
        compiler_params=pltpu.CompilerParams(
            dimension_semantics=("parallel", "arbitrary"), vmem_limit_bytes=VMEM_LIMIT),
        name="attn_seq",
    )(h1, h1, h1, h1, h1, h1, h1_meta, sinks)


def _attn_step_kernel(q_ref, g_ref, kn_ref, vn_ref, ck_ref, cv_ref, sink_ref, og_ref, nk_ref, nv_ref, *, n_seq, t):
    keys = WINDOW + t
    hd = SWA_HEAD_DIM
    tiles = SWA_Q_HEADS // 2
    kj = lax.broadcasted_iota(jnp.int32, (keys, SWA_Q_HEADS * t), 0)
    qt = lax.broadcasted_iota(jnp.int32, (keys, SWA_Q_HEADS * t), 1) % t
    madd = jnp.where((kj >= qt + 1) & (kj <= WINDOW + qt), 0.0, -jnp.inf)
    low = lax.broadcasted_iota(jnp.int32, (t, 2 * hd), 1) < hd
    zero_tile = jnp.zeros((t, 2 * hd), F32)
    sink = sink_ref[...] * LOG2E

    def group(i, carry):
        seqs = [i * SEQ_UNROLL + n for n in range(SEQ_UNROLL)]
        rows = [pl.ds(pl.multiple_of(b * t, t), t) for b in seqs]
        st, vall = [], []
        for b, rw in zip(seqs, rows):
            kc, vc = ck_ref[b], cv_ref[b]
            kn, vn = kn_ref[rw, :], vn_ref[rw, :]
            nk_ref[b, 0:WINDOW - t, :] = kc[t:, :]
            nk_ref[b, WINDOW - t:WINDOW, :] = kn
            nv_ref[b, 0:WINDOW - t, :] = vc[t:, :]
            nv_ref[b, WINDOW - t:WINDOW, :] = vn
            kall = jnp.concatenate([kc, kn], axis=0).astype(BF16)
            vall.append(jnp.concatenate([vc, vn], axis=0).astype(BF16))
            q = q_ref[rw, :] * (SWA_SCALE * LOG2E)
            qtile = [q[:, j * 2 * hd:(j + 1) * 2 * hd] for j in range(tiles)]
            qswap = [pltpu.roll(x, hd, 1) for x in qtile]
            groups = []
            for kvh in range(SWA_KV_HEADS):
                for gq in range(SWA_GROUP):
                    j = (kvh * SWA_GROUP + gq) // 2
                    src = qtile[j] if gq % 2 == kvh % 2 else qswap[j]
                    half = jnp.where(low, src, 0.0) if kvh % 2 == 0 else jnp.where(low, 0.0, src)
                    groups.append(jnp.concatenate(
                        [half if c == kvh // 2 else zero_tile for c in range(SWA_KV_HEADS // 2)], axis=1))
            qbd = jnp.concatenate(groups, axis=0).astype(BF16)
            st.append(_dot_nt(kall, qbd))
        pn = []
        for s in st:
            s = s + madd
            m = jnp.maximum(jnp.max(s, axis=0, keepdims=True), sink)
            p = jnp.exp2(s - m)
            den = jnp.sum(p, axis=0, keepdims=True) + jnp.exp2(sink - m)
            pn.append((p * (1.0 / den)).astype(BF16))
        of = [_dot_tn(p, vl) for p, vl in zip(pn, vall)]
        for o, rw in zip(of, rows):
            g = g_ref[rw, :]
            out = []
            for j in range(tiles):
                kvh = (2 * j) // SWA_GROUP
                ct = slice((kvh // 2) * 2 * hd, (kvh // 2 + 1) * 2 * hd)
                ra = o[(2 * j) * t:(2 * j + 1) * t, ct]
                rb = o[(2 * j + 1) * t:(2 * j + 2) * t, ct]
                if kvh % 2 == 0:
                    out.append(jnp.where(low, ra, pltpu.roll(rb, hd, 1)))
                else:
                    out.append(jnp.where(low, pltpu.roll(ra, hd, 1), rb))
            og_ref[rw, :] = (jnp.concatenate(out, axis=1) * _silu(g)).astype(og_ref.dtype)
        return carry

    lax.fori_loop(0, n_seq // SEQ_UNROLL, group, 0)


def _attn_step(h1, cache_k, cache_v, sink_cols, *, n_seq, t, seq_per_block):
    assert n_seq % seq_per_block == 0
    rows = seq_per_block * t
    width = SWA_Q_HEADS * SWA_HEAD_DIM
    kcol = 2 * width // SWA_KV_WIDTH
    cache_spec = pl.BlockSpec((seq_per_block, WINDOW, SWA_KV_WIDTH), lambda i: (i, 0, 0))
    return pl.pallas_call(
        functools.partial(_attn_step_kernel, n_seq=seq_per_block, t=t),
        grid=(n_seq // seq_per_block,),
        in_specs=[pl.BlockSpec((rows, width), lambda i: (i, 0)),
                  pl.BlockSpec((rows, width), lambda i: (i, 1)),
                  pl.BlockSpec((rows, SWA_KV_WIDTH), lambda i: (i, kcol)),
                  pl.BlockSpec((rows, SWA_KV_WIDTH), lambda i: (i, kcol + 1)),
                  cache_spec, cache_spec,
                  pl.BlockSpec((1, SWA_Q_HEADS * t), lambda i: (0, 0))],
        out_specs=[pl.BlockSpec((rows, width), lambda i: (i, 0)), cache_spec, cache_spec],
        out_shape=[jax.ShapeDtypeStruct((n_seq * t, width), BF16),
                   jax.ShapeDtypeStruct(cache_k.shape, F32),
                   jax.ShapeDtypeStruct(cache_v.shape, F32)],
        compiler_params=pltpu.CompilerParams(
            dimension_semantics=("parallel",), vmem_limit_bytes=VMEM_LIMIT),
        name="attn_step",
    )(h1, h1, h1, h1, cache_k, cache_v, sink_cols)


def kernel(x_prompt, x_sample, state_hgrn, cache_swa_k, cache_swa_v, meta_tokens,
           hgrn_w_in, hgrn_lb_logits, hgrn_norm_w, hgrn_w_out,
           swa_w_in, swa_sinks, swa_w_out, ln_g, ln_b):
    out_dtype = x_prompt.dtype
    bsz, seq, d = x_prompt.shape
    dec_b, dec_t, _ = x_sample.shape
    n_p = bsz * seq
    n_s = dec_b * dec_t
    width = SWA_Q_HEADS * SWA_HEAD_DIM

    kv2 = 2 * SWA_KV_WIDTH
    regroup = ((0, width), (width + kv2, 2 * width + kv2), (width, width + kv2))
    lb = jnp.cumsum(jax.nn.softmax(hgrn_lb_logits.astype(F32), axis=0), axis=0)[0:1]
    nw = hgrn_norm_w[0].astype(F32).reshape(1, -1)
    sinks = swa_sinks[0].astype(F32).reshape(1, SWA_Q_HEADS)
    sink_cols = jnp.repeat(sinks, dec_t, axis=1)
    g0, b0 = ln_g[0:1].astype(F32), ln_b[0:1].astype(F32)
    g1, b1 = ln_g[1:2].astype(F32), ln_b[1:2].astype(F32)

    x_p = x_prompt.astype(F32).reshape(n_p, d)
    meta_block = jnp.concatenate(
        [jnp.zeros((SEQ_BLOCK - N_META, d), F32), meta_tokens.astype(F32)], axis=0)
    x_s = x_sample.astype(F32).reshape(n_s, d)
    meta_blk = n_s // SEQ_BLOCK

    h_sm, w_in0 = _matmul_cast(x_s, meta_block, hgrn_w_in[0].astype(F32), xs_rows=n_s, xm_block=0, tn=SHORT_IN0_COL_TILE)
    h_p, w_out0, w_out1, w_in1 = _matmul(
        x_p, w_in0, tm=PROJ_ROW_TILE, tn=PROJ_IN0_COL_TILE,
        cast_along=(hgrn_w_out[0].astype(F32), swa_w_out[0].astype(F32), swa_w_in[0].astype(F32)),
        col_orders=(None, None, regroup))
    zero_state = jnp.zeros((1, HGRN_HEADS, HGRN_DK, HGRN_DV), F32)
    og_meta, s_meta = _hgrn_seq(h_sm, lb, nw, zero_state, n_seq=1, seq_len=SEQ_BLOCK, row_block0=meta_blk)
    og_s, st_s = _hgrn_step(h_sm, lb, nw, state_hgrn[0].astype(F32), n_seq=dec_b, t=dec_t, seq_per_block=STEP_STATE_SEQS)
    og_p, st_p = _hgrn_seq(h_p, lb, nw, s_meta, n_seq=bsz, seq_len=seq, row_block0=0)
    x1_sm = _matmul_deepnorm_tail(og_s, og_meta, w_out0, x_s, meta_block, g0, b0)
    x1_p = _matmul_deepnorm(og_p, w_out0, x_p, g0, b0, tm=PROJ_OUT_ROW_TILE)

    h1_sm = _matmul(x1_sm, w_in1, tm=x1_sm.shape[0], tn=SHORT_IN1_COL_TILE)
    h1_p = _matmul(x1_p, w_in1, tm=PROJ_ROW_TILE, tn=PROJ_IN1_COL_TILE)
    og1_p = _attn_seq(h1_p, h1_sm, sinks.reshape(-1), n_seq=bsz, seq_len=seq, meta_row_block=meta_blk)
    ck = cache_swa_k[0].astype(F32).reshape(dec_b, WINDOW, SWA_KV_WIDTH)
    cv = cache_swa_v[0].astype(F32).reshape(dec_b, WINDOW, SWA_KV_WIDTH)
    og1_s, nk_s, nv_s = _attn_step(h1_sm, ck, cv, sink_cols, n_seq=dec_b, t=dec_t, seq_per_block=STEP_CACHE_SEQS)
    y_p = _matmul_deepnorm(og1_p, w_out1, x1_p, g1, b1, tm=PROJ_OUT_ROW_TILE)
    y_s = _matmul_deepnorm(og1_s, w_out1, x1_sm, g1, b1, tm=PROJ_OUT_ROW_TILE)

    kv_p = h1_p.reshape(bsz, seq, -1)[:, seq - WINDOW:, 2 * width:]
    cache_shape = (1, bsz, WINDOW, SWA_KV_HEADS, SWA_HEAD_DIM)
    return (y_p.reshape(bsz, seq, d).astype(out_dtype),
            y_s.reshape(dec_b, dec_t, d).astype(out_dtype),
            st_p[None].astype(out_dtype),
            st_s[None].astype(out_dtype),
            kv_p[..., :SWA_KV_WIDTH].reshape(cache_shape).astype(out_dtype),
            kv_p[..., SWA_KV_WIDTH:].reshape(cache_shape).astype(out_dtype),
            nk_s.reshape((1,) + cache_swa_k.shape[1:]).astype(out_dtype),
            nv_s.reshape((1,) + cache_swa_v.shape[1:]).astype(out_dtype))
```

```python
import functools

import numpy as np
import jax
import jax.numpy as jnp
from jax import lax
from jax.experimental import pallas as pl
from jax.experimental.pallas import tpu as pltpu

F32 = jnp.float32
BF16 = jnp.bfloat16

D_MODEL = 2048
N_META = 16
DEPTH = 2
HGRN_HEADS = 16
HGRN_DK = 128
HGRN_DV = 128
SWA_Q_HEADS = 32
SWA_KV_HEADS = 4
SWA_GROUP = 8
SWA_HEAD_DIM = 64
SWA_KV_WIDTH = SWA_KV_HEADS * SWA_HEAD_DIM
SWA_SCALE = SWA_HEAD_DIM ** -0.5
WINDOW = 128
DEEPNORM_ALPHA = (2.0 * DEPTH) ** 0.25
LN_EPS = 1e-5
RMS_EPS = 1e-6
LOG2E = 1.4426950408889634

CHUNK = 128
LEVELS = 7
MAX_FACTORED_LOG2_DECAY = 80.0
MASK_ROWS = 16
SEQ_BLOCK = 128
LN_SUB_ROWS = 128
STEP_GROUP = 32
SEQ_UNROLL = 8
HGRN_SEQ_HEADS = 16
HGRN_SEQ_ROWS = 256

V7X_VMEM_BYTES = 64 * 1024 * 1024
VMEM_LIMIT = V7X_VMEM_BYTES - 4 * 1024 * 1024
PROJ_ROW_TILE = 1024
PROJ_IN0_COL_TILE = 2048
PROJ_IN1_COL_TILE = 2304
PROJ_OUT_ROW_TILE = 512
SHORT_COL_TILE = 512
SHORT_IN0_COL_TILE = 1024
STEP_STATE_SEQS = 64
STATE_RING = 3
STEP_HEADS = 2
STEP_CACHE_SEQS = 32


def _dot(a, b):
    return jnp.dot(a, b, preferred_element_type=F32)


def _dot_nt(a, b):
    return lax.dot_general(a, b, (((1,), (1,)), ((), ())), preferred_element_type=F32)


def _dot_tn(a, b):
    return lax.dot_general(a, b, (((0,), (0,)), ((), ())), preferred_element_type=F32)


def _silu(x):
    return x * jax.nn.sigmoid(x)


def _mm_kernel(x_ref, w_ref, *refs):
    n_side = (len(refs) - 1) // 2
    side_in, o_ref, side_out = refs[:n_side], refs[n_side], refs[n_side + 1:]
    o_ref[...] = _dot(x_ref[...].astype(BF16), w_ref[...])
    for src, dst in zip(side_in, side_out):
        dst[...] = src[...].astype(BF16)


def _row_tile(n, want):
    if n <= want:
        return n
    return max(t for t in range(16, want + 1, 16) if n % t == 0)


def _matmul(x, w, *, tm, tn, cast_along=()):
    n, k = x.shape
    e = w.shape[1]
    tm = _row_tile(n, tm)
    assert n % tm == 0 and e % tn == 0
    ncol = e // tn
    steps = (n // tm) * ncol
    side_specs = []
    for a in cast_along:
        assert a.shape[0] % (16 * steps) == 0
        side_specs.append(pl.BlockSpec((a.shape[0] // steps, a.shape[1]), lambda i, j: (i * ncol + j, 0)))
    out = pl.pallas_call(
        _mm_kernel,
        grid=(n // tm, ncol),
        in_specs=[pl.BlockSpec((tm, k), lambda i, j: (i, 0)),
                  pl.BlockSpec((k, tn), lambda i, j: (0, j))] + side_specs,
        out_specs=[pl.BlockSpec((tm, tn), lambda i, j: (i, j))] + side_specs,
        out_shape=[jax.ShapeDtypeStruct((n, e), F32)] + [jax.ShapeDtypeStruct(a.shape, BF16) for a in cast_along],
        compiler_params=pltpu.CompilerParams(
            dimension_semantics=("parallel", "parallel"), vmem_limit_bytes=VMEM_LIMIT),
        name="proj_in",
    )(x, w, *cast_along)
    return out if cast_along else out[0]


def _mm_cast_kernel(xs_ref, xm_ref, w_ref, o_ref, wb_ref, xb_scr):
    @pl.when(pl.program_id(0) == 0)
    def _():
        n_s = xs_ref.shape[0]
        xb_scr[0:n_s, :] = xs_ref[...].astype(BF16)
        xb_scr[n_s:, :] = xm_ref[...].astype(BF16)

    wb = w_ref[...].astype(BF16)
    wb_ref[...] = wb
    o_ref[...] = _dot(xb_scr[...], wb)


def _matmul_cast(xs, xm, w, *, xs_rows, xm_block, tn, col_tile=lambda j: j):
    k, e = w.shape
    n = xs_rows + SEQ_BLOCK
    assert e % tn == 0
    return pl.pallas_call(
        _mm_cast_kernel,
        grid=(e // tn,),
        in_specs=[pl.BlockSpec((xs_rows, k), lambda j: (0, 0), pipeline_mode=pl.Buffered(1)),
                  pl.BlockSpec((SEQ_BLOCK, k), lambda j: (xm_block, 0), pipeline_mode=pl.Buffered(1)),
                  pl.BlockSpec((k, tn), lambda j: (0, col_tile(j)))],
        out_specs=[pl.BlockSpec((n, tn), lambda j: (0, j)),
                   pl.BlockSpec((k, tn), lambda j: (0, j))],
        out_shape=[jax.ShapeDtypeStruct((n, e), F32), jax.ShapeDtypeStruct((k, e), BF16)],
        scratch_shapes=[pltpu.VMEM((n, k), BF16)],
        compiler_params=pltpu.CompilerParams(
            dimension_semantics=("arbitrary",), vmem_limit_bytes=VMEM_LIMIT),
        name="proj_in_cast",
    )(xs, xm, w)


def _mm_ln_tail_kernel(a_ref, at_ref, w_ref, x_ref, xt_ref, g_ref, b_ref, o_ref):
    tail = pl.program_id(0) == pl.num_programs(0) - 1
    a = jnp.where(tail, at_ref[...], a_ref[...])
    x = jnp.where(tail, xt_ref[...], x_ref[...])
    z = DEEPNORM_ALPHA * x + _dot(a, w_ref[...])
    mu = jnp.mean(z, axis=-1, keepdims=True)
    zc = z - mu
    var = jnp.mean(zc * zc, axis=-1, keepdims=True)
    o_ref[...] = zc * lax.rsqrt(var + LN_EPS) * g_ref[...] + b_ref[...]


def _matmul_deepnorm_tail(a, a_tail, w, x, x_tail, g, b):
    n, k = a.shape
    d = w.shape[1]
    tm = SEQ_BLOCK
    last = n // tm - 1
    body = lambda i: (jnp.minimum(i, last), 0)
    const = lambda i: (0, 0)
    return pl.pallas_call(
        _mm_ln_tail_kernel,
        grid=(n // tm + 1,),
        in_specs=[pl.BlockSpec((tm, k), body), pl.BlockSpec((tm, k), const),
                  pl.BlockSpec((k, d), const),
                  pl.BlockSpec((tm, d), body), pl.BlockSpec((tm, d), const),
                  pl.BlockSpec((1, d), const), pl.BlockSpec((1, d), const)],
        out_specs=pl.BlockSpec((tm, d), lambda i: (i, 0)),
        out_shape=jax.ShapeDtypeStruct((n + tm, d), F32),
        compiler_params=pltpu.CompilerParams(
            dimension_semantics=("parallel",), vmem_limit_bytes=VMEM_LIMIT),
        name="proj_out_deepnorm_tail",
    )(a, a_tail, w, x, x_tail, g, b)


def _mm_ln_kernel(a_ref, w_ref, x_ref, g_ref, b_ref, o_ref):
    tm = a_ref.shape[0]
    sub = LN_SUB_ROWS if tm % LN_SUB_ROWS == 0 else tm
    for r0 in range(0, tm, sub):
        rows = slice(r0, r0 + sub)
        z = DEEPNORM_ALPHA * x_ref[rows, :] + _dot(a_ref[rows, :], w_ref[...])
        mu = jnp.mean(z, axis=-1, keepdims=True)
        zc = z - mu
        var = jnp.mean(zc * zc, axis=-1, keepdims=True)
        o_ref[rows, :] = zc * lax.rsqrt(var + LN_EPS) * g_ref[...] + b_ref[...]


def _matmul_deepnorm(a, w, x, g, b, *, tm):
    n, k = a.shape
    d = w.shape[1]
    tm = _row_tile(n, tm)
    return pl.pallas_call(
        _mm_ln_kernel,
        grid=(n // tm,),
        in_specs=[pl.BlockSpec((tm, k), lambda i: (i, 0)),
                  pl.BlockSpec((k, d), lambda i: (0, 0)),
                  pl.BlockSpec((tm, d), lambda i: (i, 0)),
                  pl.BlockSpec((1, d), lambda i: (0, 0)),
                  pl.BlockSpec((1, d), lambda i: (0, 0))],
        out_specs=pl.BlockSpec((tm, d), lambda i: (i, 0)),
        out_shape=jax.ShapeDtypeStruct((n, d), F32),
        compiler_params=pltpu.CompilerParams(
            dimension_semantics=("parallel",), vmem_limit_bytes=VMEM_LIMIT),
        name="proj_out_deepnorm",
    )(a, w, x, g, b)


def _level_masks():
    t = np.arange(CHUNK)
    out = np.zeros((LEVELS + 1, CHUNK, CHUNK), np.float32)
    out[0] = t[:, None] > t[None, :]
    for l in range(LEVELS):
        h = 1 << l
        same = (t[:, None] >> (l + 1)) == (t[None, :] >> (l + 1))
        out[l + 1] = same & ((t[:, None] & h) != 0) & ((t[None, :] & h) == 0)
    return out


def _gates(fx, lb):
    f = lb + (1.0 - lb) * jax.nn.sigmoid(fx)
    return f, jnp.log2(f), 1.0 - f


def _prefix_scan(logf):
    tiles = (CHUNK // 8, 8, HGRN_DK)
    c = logf.reshape(tiles)
    sub = lax.broadcasted_iota(jnp.int32, (1, 8, HGRN_DK), 1)
    bcast = lambda x, r: jnp.broadcast_to(x[:, r:r + 1, :], tiles)
    c = c + jnp.where((sub & 1) != 0, pltpu.roll(c, 1, 1), 0.0)
    c = c + jnp.where((sub & 2) != 0, jnp.where(sub < 4, bcast(c, 1), bcast(c, 5)), 0.0)
    c = c + jnp.where((sub & 4) != 0, bcast(c, 3), 0.0)
    c = c.reshape(CHUNK, HGRN_DK)
    for level in range(3, LEVELS):
        half = 1 << level
        pieces = []
        for r0 in range(0, CHUNK, 2 * half):
            pieces += [c[r0:r0 + half], c[r0 + half:r0 + 2 * half] + c[r0 + half - 1:r0 + half, :]]
        c = jnp.concatenate(pieces, axis=0)
    return c


def _small_levels(logf, f, q, kk):
    tiles = (CHUNK // 8, 8, HGRN_DK)
    c, f3, q3, k3 = (x.reshape(tiles) for x in (logf, f, q, kk))
    sub = lax.broadcasted_iota(jnp.int32, (1, 8, HGRN_DK), 1)
    bcast = lambda x, r: jnp.broadcast_to(x[:, r:r + 1, :], tiles)
    up = (sub & 1) != 0
    zs = [jnp.where(up, q3 * f3, k3)]
    c = c + jnp.where(up, pltpu.roll(c, 1, 1), 0.0)
    for level, tot in ((1, lambda c: jnp.where(sub < 4, bcast(c, 1), bcast(c, 5))),
                       (2, lambda c: bcast(c, 3))):
        up = (sub & (1 << level)) != 0
        t = tot(c)
        zs.append(jnp.where(up, q3, k3) * jnp.exp2(jnp.where(up, c, t - c)))
        c = c + jnp.where(up, t, 0.0)
    return [z.reshape(CHUNK, HGRN_DK) for z in zs], c.reshape(CHUNK, HGRN_DK)


def _level_large(cum, q, kk, level):
    half = 1 << level
    args, bases, cums = [], [], []
    for r0 in range(0, CHUNK, 2 * half):
        lo, up = slice(r0, r0 + half), slice(r0 + half, r0 + 2 * half)
        tot = cum[r0 + half - 1:r0 + half, :]
        args += [tot - cum[lo], cum[up]]
        bases += [kk[lo], q[up]]
        cums += [cum[lo], cum[up] + tot]
    x = jnp.exp2(jnp.concatenate(args, axis=0))
    return jnp.concatenate(bases, axis=0) * x, jnp.concatenate(cums, axis=0)


def _split3(x):
    hi = x.astype(BF16)
    r = x - hi.astype(F32)
    mid = r.astype(BF16)
    lo = (r - mid.astype(F32)).astype(BF16)
    return hi, mid, lo


def _state_update(s_old, kd, v, blast, pad_rows):
    c = kd.shape[0]
    hi, mid, lo = _split3(jnp.exp2(blast))
    row = lax.broadcasted_iota(jnp.int32, (pad_rows, HGRN_DK), 0)
    dec = jnp.where(row == 0, hi.astype(F32),
                    jnp.where(row == 1, mid.astype(F32), jnp.where(row == 2, lo.astype(F32), 0.0)))
    piece = BF16 if c % 16 == 0 and pad_rows % 16 == 0 else F32
    lhs = jnp.concatenate([kd.astype(piece), dec.astype(piece)], axis=0).astype(BF16)
    rhs = jnp.concatenate(
        [jnp.concatenate([v.astype(piece), jnp.zeros((c, HGRN_DV), piece)], axis=1),
         jnp.concatenate([jnp.zeros((pad_rows, HGRN_DV), piece), jnp.ones((pad_rows, HGRN_DV), piece)], axis=1)],
        axis=0).astype(BF16)
    both = _dot_tn(lhs, rhs)
    return both[:, HGRN_DV:] * s_old + both[:, :HGRN_DV]


def _norm_gate(o, g, nw):
    o = o * lax.rsqrt(jnp.mean(o * o, axis=-1, keepdims=True) + RMS_EPS) * nw
    return o * _silu(g)


def _hgrn_seq_kernel(q_ref, fx_ref, i_ref, g_ref, lb_ref, nw_ref, s0_ref, m_ref,
                     og_ref, sout_ref, s_scr, o_scr, *, n_chunks):
    heads = s_scr.shape[0]

    @pl.when(pl.program_id(2) == 0)
    def _():
        s_scr[...] = s0_ref[0]

    def chunk(c, carry):
        rows = pl.ds(pl.multiple_of(c * CHUNK, CHUNK), CHUNK)
        hcols = [slice(hd * HGRN_DK, (hd + 1) * HGRN_DK) for hd in range(heads)]
        q = [q_ref[rows, cs] for cs in hcols]
        v = [i_ref[rows, cs] for cs in hcols]
        gates = [_gates(fx_ref[rows, cs], lb_ref[:, cs]) for cs in hcols]
        kk = [gt[2] for gt in gates]
        cum = [_prefix_scan(gt[1]) for gt in gates]
        mid = CHUNK // 2 - 1
        spread = None
        for b in cum:
            s_hd = jnp.maximum(b[0:1, :] - b[mid:mid + 1, :], b[mid:mid + 1, :] - b[CHUNK - 1:CHUNK, :])
            spread = s_hd if spread is None else jnp.maximum(spread, s_hd)
        factorable = jnp.max(spread) <= MAX_FACTORED_LOG2_DECAY

        vb = [x.astype(BF16) for x in v]
        for hd in range(heads):
            d = cum[hd] - cum[hd][mid:mid + 1, :]
            qe = (q[hd] * jnp.exp2(d)).astype(BF16)
            ke = (kk[hd] * jnp.exp2(-d)).astype(BF16)
            amat = jnp.where(m_ref[0] > 0.5, _dot_nt(qe, ke), 0.0).astype(BF16)
            blast = cum[hd][CHUNK - 1:CHUNK, :]
            qd = (q[hd] * jnp.exp2(cum[hd])).astype(BF16)
            kd = kk[hd] * jnp.exp2(blast - cum[hd])
            s_old = s_scr[hd]
            base = _dot(qd, s_old.astype(BF16)) + jnp.sum(q[hd] * kk[hd], axis=-1, keepdims=True) * v[hd]
            o_scr[hd] = base
            s_scr[hd] = _state_update(s_old, kd, vb[hd], blast, 16)
            og_ref[rows, hcols[hd]] = _norm_gate(
                base + _dot(amat, vb[hd]), g_ref[rows, hcols[hd]], nw_ref[:, hcols[hd]]).astype(og_ref.dtype)

        @pl.when(jnp.logical_not(factorable))
        def _():
            small = [_small_levels(gt[1], gt[0], q[hd], kk[hd]) for hd, gt in enumerate(gates)]
            zs = [sm[0] for sm in small]
            part = [sm[1] for sm in small]
            for level in range(3, LEVELS):
                for hd in range(heads):
                    z, part[hd] = _level_large(part[hd], q[hd], kk[hd], level)
                    zs[hd].append(z)
            nblk = CHUNK // MASK_ROWS
            a = [[None] * nblk for _ in range(heads)]
            for level in range(LEVELS):
                step = 1 << (level - 4) if level >= 4 else 0
                blocks = [i for i in range(nblk) if level < 4 or (i & step)]
                for hd in range(heads):
                    z = zs[hd][level].astype(BF16)
                    lhs = z if level < 4 else jnp.concatenate(
                        [z[i * MASK_ROWS:(i + 1) * MASK_ROWS] for i in blocks], axis=0)
                    term = _dot_nt(lhs, z)
                    for n, i in enumerate(blocks):
                        t = (term[n * MASK_ROWS:(n + 1) * MASK_ROWS]
                             * m_ref[level + 1, i * MASK_ROWS:(i + 1) * MASK_ROWS, :])
                        a[hd][i] = t if a[hd][i] is None else a[hd][i] + t
            for hd in range(heads):
                amat = jnp.concatenate(a[hd], axis=0).astype(BF16)
                og_ref[rows, hcols[hd]] = _norm_gate(
                    o_scr[hd] + _dot(amat, vb[hd]),
                    g_ref[rows, hcols[hd]], nw_ref[:, hcols[hd]]).astype(og_ref.dtype)
        return carry

    lax.fori_loop(0, n_chunks, chunk, 0)

    @pl.when(pl.program_id(2) == pl.num_programs(2) - 1)
    def _():
        sout_ref[0] = s_scr[...]


def _hgrn_seq(h, lb, nw, s0, *, n_seq, seq_len, row_block0):
    hb = HGRN_SEQ_HEADS
    width = hb * HGRN_DK
    groups = HGRN_HEADS // hb
    rb = min(HGRN_SEQ_ROWS, seq_len)
    nrb = seq_len // rb
    assert seq_len % rb == 0 and rb % CHUNK == 0 and (row_block0 * SEQ_BLOCK) % rb == 0
    rb0 = row_block0 * SEQ_BLOCK // rb
    s0_batched = s0.shape[0] != 1
    col = lambda part: (lambda b, hg, r: (rb0 + b * nrb + r, part * groups + hg))
    masks = jnp.asarray(_level_masks())
    return pl.pallas_call(
        functools.partial(_hgrn_seq_kernel, n_chunks=rb // CHUNK),
        grid=(n_seq, groups, nrb),
        in_specs=[pl.BlockSpec((rb, width), col(0)),
                  pl.BlockSpec((rb, width), col(1)),
                  pl.BlockSpec((rb, width), col(2)),
                  pl.BlockSpec((rb, width), col(3)),
                  pl.BlockSpec((1, width), lambda b, hg, r: (0, hg)),
                  pl.BlockSpec((1, width), lambda b, hg, r: (0, hg)),
                  pl.BlockSpec((1, hb, HGRN_DK, HGRN_DV),
                               (lambda b, hg, r: (b, hg, 0, 0)) if s0_batched else (lambda b, hg, r: (0, hg, 0, 0))),
                  pl.BlockSpec((LEVELS + 1, CHUNK, CHUNK), lambda b, hg, r: (0, 0, 0))],
        out_specs=[pl.BlockSpec((rb, width), lambda b, hg, r: (b * nrb + r, hg)),
                   pl.BlockSpec((1, hb, HGRN_DK, HGRN_DV), lambda b, hg, r: (b, hg, 0, 0))],
        out_shape=[jax.ShapeDtypeStruct((n_seq * seq_len, HGRN_HEADS * HGRN_DV), BF16),
                   jax.ShapeDtypeStruct((n_seq, HGRN_HEADS, HGRN_DK, HGRN_DV), F32)],
        scratch_shapes=[pltpu.VMEM((hb, HGRN_DK, HGRN_DV), F32),
                        pltpu.VMEM((hb, CHUNK, HGRN_DV), F32)],
        compiler_params=pltpu.CompilerParams(
            dimension_semantics=("parallel", "parallel", "arbitrary"), vmem_limit_bytes=VMEM_LIMIT),
        name="hgrn_seq",
    )(h, h, h, h, lb, nw, s0, masks)


def _hgrn_step_kernel(q_ref, fx_ref, i_ref, g_ref, lb_ref, nw_ref, s_hbm, og_ref, sout_ref, sbuf, sem,
                      *, n_seq, t, heads, groups, n_steps):
    step = pl.program_id(0) * groups + pl.program_id(1)

    def state_copy(s, slot):
        src = s_hbm.at[pl.ds((s // groups) * n_seq, n_seq), pl.ds((s % groups) * heads, heads)]
        return pltpu.make_async_copy(src, sbuf.at[slot], sem.at[slot])

    @pl.when(step == 0)
    def _():
        for k in range(min(STATE_RING - 1, n_steps)):
            state_copy(k, k).start()

    ahead = step + (STATE_RING - 1)

    @pl.when(ahead < n_steps)
    def _():
        state_copy(ahead, ahead % STATE_RING).start()

    slot = step % STATE_RING
    state_copy(step, slot).wait()
    s_ref = sbuf.at[slot]

    tile = (STEP_GROUP, t, HGRN_DK)
    sub = lax.broadcasted_iota(jnp.int32, (1, t, HGRN_DK), 1)

    for hd in range(heads):
        cols = slice(hd * HGRN_DK, (hd + 1) * HGRN_DK)
        lb = lb_ref[:, cols]
        nw = nw_ref[:, cols]

        def group(i, carry, hd=hd, cols=cols, lb=lb, nw=nw):
            rows = pl.ds(pl.multiple_of(i * (STEP_GROUP * t), STEP_GROUP * t), STEP_GROUP * t)
            q = q_ref[rows, cols].reshape(tile)
            v = i_ref[rows, cols].reshape(tile)
            _, logf, kk = _gates(fx_ref[rows, cols].reshape(tile), lb)
            cum = logf
            shift = 1
            while shift < t:
                cum = cum + jnp.where(sub >= shift, pltpu.roll(cum, shift, 1), 0.0)
                shift *= 2
            intra = (jnp.sum(q * kk, axis=-1, keepdims=True)) * v
            for d in range(1, t):
                valid = sub >= d
                x = jnp.exp2(jnp.where(valid, cum - pltpu.roll(cum, d, 1), 0.0))
                w = jnp.sum(jnp.where(valid, q * x * pltpu.roll(kk, d, 1), 0.0), axis=-1, keepdims=True)
                intra = intra + w * pltpu.roll(v, d, 1)
            blast = cum[:, t - 1:t, :]
            qd = q * jnp.exp2(cum)
            kd = kk * jnp.exp2(blast - cum)
            seqs = [i * STEP_GROUP + n for n in range(STEP_GROUP)]
            inter = [_dot(qd[n].astype(BF16), s_ref[b, hd].astype(BF16)) for n, b in enumerate(seqs)]
            for n, b in enumerate(seqs):
                sout_ref[b, hd] = _state_update(s_ref[b, hd], kd[n], v[n], blast[n], 8)
            o = jnp.stack(inter, axis=0) + intra
            og = _norm_gate(o, g_ref[rows, cols].reshape(tile), nw)
            og_ref[rows, cols] = og.reshape(STEP_GROUP * t, HGRN_DK).astype(og_ref.dtype)
            return carry

        lax.fori_loop(0, n_seq // STEP_GROUP, group, 0)


def _hgrn_step(h, lb, nw, s0, *, n_seq, t, seq_per_block):
    seq_per_block = min(seq_per_block, n_seq)
    assert n_seq % seq_per_block == 0 and seq_per_block % STEP_GROUP == 0 and t == 8
    rows = seq_per_block * t
    hb = STEP_HEADS
    width = hb * HGRN_DK
    groups = HGRN_HEADS // hb
    col = lambda part: (lambda i, hg: (i, part * groups + hg))
    state_spec = pl.BlockSpec((seq_per_block, hb, HGRN_DK, HGRN_DV), lambda i, hg: (i, hg, 0, 0))
    n_steps = (n_seq // seq_per_block) * groups
    return pl.pallas_call(
        functools.partial(_hgrn_step_kernel, n_seq=seq_per_block, t=t, heads=hb, groups=groups, n_steps=n_steps),
        grid=(n_seq // seq_per_block, groups),
        in_specs=[pl.BlockSpec((rows, width), col(0)),
                  pl.BlockSpec((rows, width), col(1)),
                  pl.BlockSpec((rows, width), col(2)),
                  pl.BlockSpec((rows, width), col(3)),
                  pl.BlockSpec((1, width), lambda i, hg: (0, hg)),
                  pl.BlockSpec((1, width), lambda i, hg: (0, hg)),
                  pl.BlockSpec(memory_space=pl.ANY)],
        out_specs=[pl.BlockSpec((rows, width), lambda i, hg: (i, hg)), state_spec],
        out_shape=[jax.ShapeDtypeStruct((n_seq * t, HGRN_HEADS * HGRN_DV), BF16),
                   jax.ShapeDtypeStruct((n_seq, HGRN_HEADS, HGRN_DK, HGRN_DV), F32)],
        scratch_shapes=[pltpu.VMEM((STATE_RING, seq_per_block, hb, HGRN_DK, HGRN_DV), F32),
                        pltpu.SemaphoreType.DMA((STATE_RING,))],
        compiler_params=pltpu.CompilerParams(
            dimension_semantics=("arbitrary", "arbitrary"), vmem_limit_bytes=VMEM_LIMIT),
        name="hgrn_step",
    )(h, h, h, h, lb, nw, s0)


def _attn_seq_kernel(q_ref, g_ref, kc_ref, vc_ref, kp_ref, vp_ref, meta_ref, sink_ref, og_ref):
    first = pl.program_id(1) == 0
    kprev = jnp.where(first, meta_ref[:, :SWA_KV_WIDTH], kp_ref[...])
    vprev = jnp.where(first, meta_ref[:, SWA_KV_WIDTH:], vp_ref[...])
    kband = (jnp.concatenate([kprev, kc_ref[...]], axis=0) * (SWA_SCALE * LOG2E)).astype(BF16)
    vband_t = jnp.concatenate([vprev, vc_ref[...]], axis=0).T.astype(BF16)
    zeros_t = jnp.zeros((SWA_HEAD_DIM, 2 * SEQ_BLOCK), BF16)
    kj = lax.broadcasted_iota(jnp.int32, (2 * SEQ_BLOCK, SEQ_BLOCK), 0)
    qi = lax.broadcasted_iota(jnp.int32, (2 * SEQ_BLOCK, SEQ_BLOCK), 1)
    dist = SEQ_BLOCK + qi - kj
    valid = (dist >= 0) & (dist < WINDOW) & (jnp.logical_not(first) | (kj >= SEQ_BLOCK - N_META))
    madd = jnp.where(valid, 0.0, -jnp.inf)
    zeros = jnp.zeros((2 * SEQ_BLOCK, SWA_HEAD_DIM), BF16)
    pairs = SWA_GROUP // 2
    pw = 2 * SWA_HEAD_DIM

    def block_diag(x):
        return jnp.concatenate([jnp.concatenate([x, zeros], axis=1),
                                jnp.concatenate([zeros, x], axis=1)], axis=0)

    sts = []
    for kvh in range(SWA_KV_HEADS):
        ks = slice(kvh * SWA_HEAD_DIM, (kvh + 1) * SWA_HEAD_DIM)
        col0 = kvh * SWA_GROUP * SWA_HEAD_DIM
        xq = jnp.concatenate([q_ref[:, col0 + p * pw:col0 + (p + 1) * pw].astype(BF16)
                              for p in range(pairs)], axis=0)
        sts.append(_dot_nt(block_diag(kband[:, ks]), xq))
    for kvh in range(SWA_KV_HEADS):
        ks = slice(kvh * SWA_HEAD_DIM, (kvh + 1) * SWA_HEAD_DIM)
        col0 = kvh * SWA_GROUP * SWA_HEAD_DIM
        vt = vband_t[ks, :]
        v2t = jnp.concatenate([jnp.concatenate([vt, zeros_t], axis=1),
                               jnp.concatenate([zeros_t, vt], axis=1)], axis=0)
        pbs, rdens = [], []
        for pr in range(pairs):
            head = kvh * SWA_GROUP + 2 * pr
            s2 = (sts[kvh][:, pr * SEQ_BLOCK:(pr + 1) * SEQ_BLOCK].reshape(2, 2 * SEQ_BLOCK, SEQ_BLOCK)
                  + madd[None])
            sink = jnp.concatenate([jnp.full((1, 1, SEQ_BLOCK), sink_ref[head] * LOG2E, F32),
                                    jnp.full((1, 1, SEQ_BLOCK), sink_ref[head + 1] * LOG2E, F32)], axis=0)
            m = jnp.maximum(jnp.max(s2, axis=1, keepdims=True), sink)
            p = jnp.exp2(s2 - m)
            den = jnp.sum(p, axis=1, keepdims=True) + jnp.exp2(sink - m)
            pbs.append(p.astype(BF16).reshape(4 * SEQ_BLOCK, SEQ_BLOCK))
            rdens.append(1.0 / den)
        for pr in range(pairs):
            ot = _dot(v2t, pbs[pr]).reshape(2, SWA_HEAD_DIM, SEQ_BLOCK) * rdens[pr]
            o = ot.reshape(2 * SWA_HEAD_DIM, SEQ_BLOCK).T
            cs = slice(col0 + pr * pw, col0 + (pr + 1) * pw)
            og_ref[:, cs] = (o * _silu(g_ref[:, cs])).astype(og_ref.dtype)


def _attn_seq(h1, h1_meta, sinks, *, n_seq, seq_len, meta_row_block):
    nb = seq_len // SEQ_BLOCK
    width = SWA_Q_HEADS * SWA_HEAD_DIM
    kcol = 2 * width // SWA_KV_WIDTH
    return pl.pallas_call(
        _attn_seq_kernel,
        grid=(n_seq, nb),
        in_specs=[pl.BlockSpec((SEQ_BLOCK, width), lambda b, j: (b * nb + j, 0)),
                  pl.BlockSpec((SEQ_BLOCK, width), lambda b, j: (b * nb + j, 1)),
                  pl.BlockSpec((SEQ_BLOCK, SWA_KV_WIDTH), lambda b, j: (b * nb + j, kcol)),
                  pl.BlockSpec((SEQ_BLOCK, SWA_KV_WIDTH), lambda b, j: (b * nb + j, kcol + 1)),
                  pl.BlockSpec((SEQ_BLOCK, SWA_KV_WIDTH), lambda b, j: (b * nb + jnp.maximum(j - 1, 0), kcol)),
                  pl.BlockSpec((SEQ_BLOCK, SWA_KV_WIDTH), lambda b, j: (b * nb + jnp.maximum(j - 1, 0), kcol + 1)),
                  pl.BlockSpec((SEQ_BLOCK, 2 * SWA_KV_WIDTH), lambda b, j: (meta_row_block, kcol // 2)),
                  pl.BlockSpec(memory_space=pltpu.SMEM)],
        out_specs=pl.BlockSpec((SEQ_BLOCK, width), lambda b, j: (b * nb + j, 0)),
        out_shape=jax.ShapeDtypeStruct((n_seq * seq_len, width), BF16),
        compiler_params=pltpu.CompilerParams(
            dimension_semantics=("parallel", "arbitrary"), vmem_limit_bytes=VMEM_LIMIT),
        name="attn_seq",
    )(h1, h1, h1, h1, h1, h1, h1_meta, sinks)


def _attn_step_kernel(q_ref, g_ref, kn_ref, vn_ref, ck_ref, cv_ref, sink_ref, og_ref, nk_ref, nv_ref, *, n_seq, t):
    keys = WINDOW + t
    hd = SWA_HEAD_DIM
    tiles = SWA_Q_HEADS // 2
    kj = lax.broadcasted_iota(jnp.int32, (keys, SWA_Q_HEADS * t), 0)
    qt = lax.broadcasted_iota(jnp.int32, (keys, SWA_Q_HEADS * t), 1) % t
    madd = jnp.where((kj >= qt + 1) & (kj <= WINDOW + qt), 0.0, -jnp.inf)
    low = lax.broadcasted_iota(jnp.int32, (t, 2 * hd), 1) < hd
    zero_tile = jnp.zeros((t, 2 * hd), F32)
    sink = sink_ref[...] * LOG2E

    def group(i, carry):
        seqs = [i * SEQ_UNROLL + n for n in range(SEQ_UNROLL)]
        rows = [pl.ds(pl.multiple_of(b * t, t), t) for b in seqs]
        st, vall = [], []
        for b, rw in zip(seqs, rows):
            kc, vc = ck_ref[b], cv_ref[b]
            kn, vn = kn_ref[rw, :], vn_ref[rw, :]
            nk_ref[b, 0:WINDOW - t, :] = kc[t:, :]
            nk_ref[b, WINDOW - t:WINDOW, :] = kn
            nv_ref[b, 0:WINDOW - t, :] = vc[t:, :]
            nv_ref[b, WINDOW - t:WINDOW, :] = vn
            kall = jnp.concatenate([kc, kn], axis=0).astype(BF16)
            vall.append(jnp.concatenate([vc, vn], axis=0).astype(BF16))
            q = q_ref[rw, :] * (SWA_SCALE * LOG2E)
            qtile = [q[:, j * 2 * hd:(j + 1) * 2 * hd] for j in range(tiles)]
            qswap = [pltpu.roll(x, hd, 1) for x in qtile]
            groups = []
            for kvh in range(SWA_KV_HEADS):
                for gq in range(SWA_GROUP):
                    j = (kvh * SWA_GROUP + gq) // 2
                    src = qtile[j] if gq % 2 == kvh % 2 else qswap[j]
                    half = jnp.where(low, src, 0.0) if kvh % 2 == 0 else jnp.where(low, 0.0, src)
                    groups.append(jnp.concatenate(
                        [half if c == kvh // 2 else zero_tile for c in range(SWA_KV_HEADS // 2)], axis=1))
            qbd = jnp.concatenate(groups, axis=0).astype(BF16)
            st.append(_dot_nt(kall, qbd))
        pn = []
        for s in st:
            s = s + madd
            m = jnp.maximum(jnp.max(s, axis=0, keepdims=True), sink)
            p = jnp.exp2(s - m)
            den = jnp.sum(p, axis=0, keepdims=True) + jnp.exp2(sink - m)
            pn.append((p * (1.0 / den)).astype(BF16))
        of = [_dot_tn(p, vl) for p, vl in zip(pn, vall)]
        for o, rw in zip(of, rows):
            g = g_ref[rw, :]
            out = []
            for j in range(tiles):
                kvh = (2 * j) // SWA_GROUP
                ct = slice((kvh // 2) * 2 * hd, (kvh // 2 + 1) * 2 * hd)
                ra = o[(2 * j) * t:(2 * j + 1) * t, ct]
                rb = o[(2 * j + 1) * t:(2 * j + 2) * t, ct]
                if kvh % 2 == 0:
                    out.append(jnp.where(low, ra, pltpu.roll(rb, hd, 1)))
                else:
                    out.append(jnp.where(low, pltpu.roll(ra, hd, 1), rb))
            og_ref[rw, :] = (jnp.concatenate(out, axis=1) * _silu(g)).astype(og_ref.dtype)
        return carry

    lax.fori_loop(0, n_seq // SEQ_UNROLL, group, 0)


def _attn_step(h1, cache_k, cache_v, sink_cols, *, n_seq, t, seq_per_block):
    assert n_seq % seq_per_block == 0
    rows = seq_per_block * t
    width = SWA_Q_HEADS * SWA_HEAD_DIM
    kcol = 2 * width // SWA_KV_WIDTH
    cache_spec = pl.BlockSpec((seq_per_block, WINDOW, SWA_KV_WIDTH), lambda i: (i, 0, 0))
    return pl.pallas_call(
        functools.partial(_attn_step_kernel, n_seq=seq_per_block, t=t),
        grid=(n_seq // seq_per_block,),
        in_specs=[pl.BlockSpec((rows, width), lambda i: (i, 0)),
                  pl.BlockSpec((rows, width), lambda i: (i, 1)),
                  pl.BlockSpec((rows, SWA_KV_WIDTH), lambda i: (i, kcol)),
                  pl.BlockSpec((rows, SWA_KV_WIDTH), lambda i: (i, kcol + 1)),
                  cache_spec, cache_spec,
                  pl.BlockSpec((1, SWA_Q_HEADS * t), lambda i: (0, 0))],
        out_specs=[pl.BlockSpec((rows, width), lambda i: (i, 0)), cache_spec, cache_spec],
        out_shape=[jax.ShapeDtypeStruct((n_seq * t, width), BF16),
                   jax.ShapeDtypeStruct(cache_k.shape, F32),
                   jax.ShapeDtypeStruct(cache_v.shape, F32)],
        compiler_params=pltpu.CompilerParams(
            dimension_semantics=("parallel",), vmem_limit_bytes=VMEM_LIMIT),
        name="attn_step",
    )(h1, h1, h1, h1, cache_k, cache_v, sink_cols)


def kernel(x_prompt, x_sample, state_hgrn, cache_swa_k, cache_swa_v, meta_tokens,
           hgrn_w_in, hgrn_lb_logits, hgrn_norm_w, hgrn_w_out,
           swa_w_in, swa_sinks, swa_w_out, ln_g, ln_b):
    out_dtype = x_prompt.dtype
    bsz, seq, d = x_prompt.shape
    dec_b, dec_t, _ = x_sample.shape
    n_p = bsz * seq
    n_s = dec_b * dec_t
    width = SWA_Q_HEADS * SWA_HEAD_DIM

    in_tile = SHORT_COL_TILE
    q_tiles = width // in_tile
    kv_tile = 2 * SWA_KV_WIDTH // in_tile
    assert width % in_tile == 0 and (2 * SWA_KV_WIDTH) % in_tile == 0 and kv_tile == 1
    regroup = lambda j: jnp.where(j < q_tiles, j, jnp.where(j < 2 * q_tiles, j + kv_tile, q_tiles))
    lb = jnp.cumsum(jax.nn.softmax(hgrn_lb_logits.astype(F32), axis=0), axis=0)[0:1]
    nw = hgrn_norm_w[0].astype(F32).reshape(1, -1)
    sinks = swa_sinks[0].astype(F32).reshape(1, SWA_Q_HEADS)
    sink_cols = jnp.repeat(sinks, dec_t, axis=1)
    g0, b0 = ln_g[0:1].astype(F32), ln_b[0:1].astype(F32)
    g1, b1 = ln_g[1:2].astype(F32), ln_b[1:2].astype(F32)

    x_p = x_prompt.astype(F32).reshape(n_p, d)
    meta_block = jnp.concatenate(
        [jnp.zeros((SEQ_BLOCK - N_META, d), F32), meta_tokens.astype(F32)], axis=0)
    x_s = x_sample.astype(F32).reshape(n_s, d)
    meta_blk = n_s // SEQ_BLOCK

    h_sm, w_in0 = _matmul_cast(x_s, meta_block, hgrn_w_in[0].astype(F32), xs_rows=n_s, xm_block=0, tn=SHORT_IN0_COL_TILE)
    h_p, w_out0, w_out1 = _matmul(x_p, w_in0, tm=PROJ_ROW_TILE, tn=PROJ_IN0_COL_TILE,
                                  cast_along=(hgrn_w_out[0].astype(F32), swa_w_out[0].astype(F32)))
    zero_state = jnp.zeros((1, HGRN_HEADS, HGRN_DK, HGRN_DV), F32)
    og_meta, s_meta = _hgrn_seq(h_sm, lb, nw, zero_state, n_seq=1, seq_len=SEQ_BLOCK, row_block0=meta_blk)
    og_s, st_s = _hgrn_step(h_sm, lb, nw, state_hgrn[0].astype(F32), n_seq=dec_b, t=dec_t, seq_per_block=STEP_STATE_SEQS)
    og_p, st_p = _hgrn_seq(h_p, lb, nw, s_meta, n_seq=bsz, seq_len=seq, row_block0=0)
    x1_sm = _matmul_deepnorm_tail(og_s, og_meta, w_out0, x_s, meta_block, g0, b0)
    x1_p = _matmul_deepnorm(og_p, w_out0, x_p, g0, b0, tm=PROJ_OUT_ROW_TILE)

    h1_sm, w_in1 = _matmul_cast(x1_sm, x1_sm, swa_w_in[0].astype(F32), xs_rows=n_s, xm_block=meta_blk,
                                tn=in_tile, col_tile=regroup)
    h1_p = _matmul(x1_p, w_in1, tm=PROJ_ROW_TILE, tn=PROJ_IN1_COL_TILE)
    og1_p = _attn_seq(h1_p, h1_sm, sinks.reshape(-1), n_seq=bsz, seq_len=seq, meta_row_block=meta_blk)
    ck = cache_swa_k[0].astype(F32).reshape(dec_b, WINDOW, SWA_KV_WIDTH)
    cv = cache_swa_v[0].astype(F32).reshape(dec_b, WINDOW, SWA_KV_WIDTH)
    og1_s, nk_s, nv_s = _attn_step(h1_sm, ck, cv, sink_cols, n_seq=dec_b, t=dec_t, seq_per_block=STEP_CACHE_SEQS)
    y_p = _matmul_deepnorm(og1_p, w_out1, x1_p, g1, b1, tm=PROJ_OUT_ROW_TILE)
    y_s = _matmul_deepnorm(og1_s, w_out1, x1_sm, g1, b1, tm=PROJ_OUT_ROW_TILE)

    kv_p = h1_p.reshape(bsz, seq, -1)[:, seq - WINDOW:, 2 * width:]
    cache_shape = (1, bsz, WINDOW, SWA_KV_HEADS, SWA_HEAD_DIM)
    return (y_p.reshape(bsz, seq, d).astype(out_dtype),
            y_s.reshape(dec_b, dec_t, d).astype(out_dtype),
            st_p[None].astype(out_dtype),
            st_s[None].astype(out_dtype),
            kv_p[..., :SWA_KV_WIDTH].reshape(cache_shape).astype(out_dtype),
            kv_p[..., SWA_KV_WIDTH:].reshape(cache_shape).astype(out_dtype),
            nk_s.reshape((1,) + cache_swa_k.shape[1:]).astype(out_dtype),
            nv_s.reshape((1,) + cache_swa_v.shape[1:]).astype(out_dtype))
```

```python
import functools

import numpy as np
import jax
import jax.numpy as jnp
from jax import lax
from jax.experimental import pallas as pl
from jax.experimental.pallas import tpu as pltpu

F32 = jnp.float32
BF16 = jnp.bfloat16

D_MODEL = 2048
N_META = 16
DEPTH = 2
HGRN_HEADS = 16
HGRN_DK = 128
HGRN_DV = 128
SWA_Q_HEADS = 32
SWA_KV_HEADS = 4
SWA_GROUP = 8
SWA_HEAD_DIM = 64
SWA_KV_WIDTH = SWA_KV_HEADS * SWA_HEAD_DIM
SWA_SCALE = SWA_HEAD_DIM ** -0.5
WINDOW = 128
DEEPNORM_ALPHA = (2.0 * DEPTH) ** 0.25
LN_EPS = 1e-5
RMS_EPS = 1e-6
LOG2E = 1.4426950408889634

CHUNK = 128
LEVELS = 7
MAX_FACTORED_LOG2_DECAY = 80.0
MASK_ROWS = 16
SEQ_BLOCK = 128
LN_SUB_ROWS = 128
STEP_GROUP = 32
SEQ_UNROLL = 8
HGRN_SEQ_HEADS = 16
HGRN_SEQ_ROWS = 256

V7X_VMEM_BYTES = 64 * 1024 * 1024
VMEM_LIMIT = V7X_VMEM_BYTES - 4 * 1024 * 1024
PROJ_ROW_TILE = 1024
PROJ_IN0_COL_TILE = 2048
PROJ_IN1_COL_TILE = 2304
PROJ_OUT_ROW_TILE = 512
SHORT_COL_TILE = 512
SHORT_IN0_COL_TILE = 1024
STEP_STATE_SEQS = 64
STATE_RING = 4
STEP_HEADS = 2
STEP_CACHE_SEQS = 32


def _dot(a, b):
    return jnp.dot(a, b, preferred_element_type=F32)


def _dot_nt(a, b):
    return lax.dot_general(a, b, (((1,), (1,)), ((), ())), preferred_element_type=F32)


def _dot_tn(a, b):
    return lax.dot_general(a, b, (((0,), (0,)), ((), ())), preferred_element_type=F32)


def _silu(x):
    return x * jax.nn.sigmoid(x)


def _mm_kernel(x_ref, w_ref, *refs):
    n_side = (len(refs) - 1) // 2
    side_in, o_ref, side_out = refs[:n_side], refs[n_side], refs[n_side + 1:]
    o_ref[...] = _dot(x_ref[...].astype(BF16), w_ref[...])
    for src, dst in zip(side_in, side_out):
        dst[...] = src[...].astype(BF16)


def _row_tile(n, want):
    if n <= want:
        return n
    return max(t for t in range(16, want + 1, 16) if n % t == 0)


def _matmul(x, w, *, tm, tn, cast_along=()):
    n, k = x.shape
    e = w.shape[1]
    tm = _row_tile(n, tm)
    assert n % tm == 0 and e % tn == 0
    ncol = e // tn
    steps = (n // tm) * ncol
    side_specs = []
    for a in cast_along:
        assert a.shape[0] % (16 * steps) == 0
        side_specs.append(pl.BlockSpec((a.shape[0] // steps, a.shape[1]), lambda i, j: (i * ncol + j, 0)))
    out = pl.pallas_call(
        _mm_kernel,
        grid=(n // tm, ncol),
        in_specs=[pl.BlockSpec((tm, k), lambda i, j: (i, 0)),
                  pl.BlockSpec((k, tn), lambda i, j: (0, j))] + side_specs,
        out_specs=[pl.BlockSpec((tm, tn), lambda i, j: (i, j))] + side_specs,
        out_shape=[jax.ShapeDtypeStruct((n, e), F32)] + [jax.ShapeDtypeStruct(a.shape, BF16) for a in cast_along],
        compiler_params=pltpu.CompilerParams(
            dimension_semantics=("parallel", "parallel"), vmem_limit_bytes=VMEM_LIMIT),
        name="proj_in",
    )(x, w, *cast_along)
    return out if cast_along else out[0]


def _mm_cast_kernel(xs_ref, xm_ref, w_ref, o_ref, wb_ref, xb_scr):
    @pl.when(pl.program_id(0) == 0)
    def _():
        n_s = xs_ref.shape[0]
        xb_scr[0:n_s, :] = xs_ref[...].astype(BF16)
        xb_scr[n_s:, :] = xm_ref[...].astype(BF16)

    wb = w_ref[...].astype(BF16)
    wb_ref[...] = wb
    o_ref[...] = _dot(xb_scr[...], wb)


def _matmul_cast(xs, xm, w, *, xs_rows, xm_block, tn, col_tile=lambda j: j):
    k, e = w.shape
    n = xs_rows + SEQ_BLOCK
    assert e % tn == 0
    return pl.pallas_call(
        _mm_cast_kernel,
        grid=(e // tn,),
        in_specs=[pl.BlockSpec((xs_rows, k), lambda j: (0, 0), pipeline_mode=pl.Buffered(1)),
                  pl.BlockSpec((SEQ_BLOCK, k), lambda j: (xm_block, 0), pipeline_mode=pl.Buffered(1)),
                  pl.BlockSpec((k, tn), lambda j: (0, col_tile(j)))],
        out_specs=[pl.BlockSpec((n, tn), lambda j: (0, j)),
                   pl.BlockSpec((k, tn), lambda j: (0, j))],
        out_shape=[jax.ShapeDtypeStruct((n, e), F32), jax.ShapeDtypeStruct((k, e), BF16)],
        scratch_shapes=[pltpu.VMEM((n, k), BF16)],
        compiler_params=pltpu.CompilerParams(
            dimension_semantics=("arbitrary",), vmem_limit_bytes=VMEM_LIMIT),
        name="proj_in_cast",
    )(xs, xm, w)


def _mm_ln_tail_kernel(a_ref, at_ref, w_ref, x_ref, xt_ref, g_ref, b_ref, o_ref):
    tail = pl.program_id(0) == pl.num_programs(0) - 1
    a = jnp.where(tail, at_ref[...], a_ref[...])
    x = jnp.where(tail, xt_ref[...], x_ref[...])
    z = DEEPNORM_ALPHA * x + _dot(a, w_ref[...])
    mu = jnp.mean(z, axis=-1, keepdims=True)
    zc = z - mu
    var = jnp.mean(zc * zc, axis=-1, keepdims=True)
    o_ref[...] = zc * lax.rsqrt(var + LN_EPS) * g_ref[...] + b_ref[...]


def _matmul_deepnorm_tail(a, a_tail, w, x, x_tail, g, b):
    n, k = a.shape
    d = w.shape[1]
    tm = SEQ_BLOCK
    last = n // tm - 1
    body = lambda i: (jnp.minimum(i, last), 0)
    const = lambda i: (0, 0)
    return pl.pallas_call(
        _mm_ln_tail_kernel,
        grid=(n // tm + 1,),
        in_specs=[pl.BlockSpec((tm, k), body), pl.BlockSpec((tm, k), const),
                  pl.BlockSpec((k, d), const),
                  pl.BlockSpec((tm, d), body), pl.BlockSpec((tm, d), const),
                  pl.BlockSpec((1, d), const), pl.BlockSpec((1, d), const)],
        out_specs=pl.BlockSpec((tm, d), lambda i: (i, 0)),
        out_shape=jax.ShapeDtypeStruct((n + tm, d), F32),
        compiler_params=pltpu.CompilerParams(
            dimension_semantics=("parallel",), vmem_limit_bytes=VMEM_LIMIT),
        name="proj_out_deepnorm_tail",
    )(a, a_tail, w, x, x_tail, g, b)


def _mm_ln_kernel(a_ref, w_ref, x_ref, g_ref, b_ref, o_ref):
    tm = a_ref.shape[0]
    sub = LN_SUB_ROWS if tm % LN_SUB_ROWS == 0 else tm
    for r0 in range(0, tm, sub):
        rows = slice(r0, r0 + sub)
        z = DEEPNORM_ALPHA * x_ref[rows, :] + _dot(a_ref[rows, :], w_ref[...])
        mu = jnp.mean(z, axis=-1, keepdims=True)
        zc = z - mu
        var = jnp.mean(zc * zc, axis=-1, keepdims=True)
        o_ref[rows, :] = zc * lax.rsqrt(var + LN_EPS) * g_ref[...] + b_ref[...]


def _matmul_deepnorm(a, w, x, g, b, *, tm):
    n, k = a.shape
    d = w.shape[1]
    tm = _row_tile(n, tm)
    return pl.pallas_call(
        _mm_ln_kernel,
        grid=(n // tm,),
        in_specs=[pl.BlockSpec((tm, k), lambda i: (i, 0)),
                  pl.BlockSpec((k, d), lambda i: (0, 0)),
                  pl.BlockSpec((tm, d), lambda i: (i, 0)),
                  pl.BlockSpec((1, d), lambda i: (0, 0)),
                  pl.BlockSpec((1, d), lambda i: (0, 0))],
        out_specs=pl.BlockSpec((tm, d), lambda i: (i, 0)),
        out_shape=jax.ShapeDtypeStruct((n, d), F32),
        compiler_params=pltpu.CompilerParams(
            dimension_semantics=("parallel",), vmem_limit_bytes=VMEM_LIMIT),
        name="proj_out_deepnorm",
    )(a, w, x, g, b)


def _level_masks():
    t = np.arange(CHUNK)
    out = np.zeros((LEVELS + 1, CHUNK, CHUNK), np.float32)
    out[0] = t[:, None] > t[None, :]
    for l in range(LEVELS):
        h = 1 << l
        same = (t[:, None] >> (l + 1)) == (t[None, :] >> (l + 1))
        out[l + 1] = same & ((t[:, None] & h) != 0) & ((t[None, :] & h) == 0)
    return out


def _gates(fx, lb):
    f = lb + (1.0 - lb) * jax.nn.sigmoid(fx)
    return f, jnp.log2(f), 1.0 - f


def _prefix_scan(logf):
    tiles = (CHUNK // 8, 8, HGRN_DK)
    c = logf.reshape(tiles)
    sub = lax.broadcasted_iota(jnp.int32, (1, 8, HGRN_DK), 1)
    bcast = lambda x, r: jnp.broadcast_to(x[:, r:r + 1, :], tiles)
    c = c + jnp.where((sub & 1) != 0, pltpu.roll(c, 1, 1), 0.0)
    c = c + jnp.where((sub & 2) != 0, jnp.where(sub < 4, bcast(c, 1), bcast(c, 5)), 0.0)
    c = c + jnp.where((sub & 4) != 0, bcast(c, 3), 0.0)
    c = c.reshape(CHUNK, HGRN_DK)
    for level in range(3, LEVELS):
        half = 1 << level
        pieces = []
        for r0 in range(0, CHUNK, 2 * half):
            pieces += [c[r0:r0 + half], c[r0 + half:r0 + 2 * half] + c[r0 + half - 1:r0 + half, :]]
        c = jnp.concatenate(pieces, axis=0)
    return c


def _small_levels(logf, f, q, kk):
    tiles = (CHUNK // 8, 8, HGRN_DK)
    c, f3, q3, k3 = (x.reshape(tiles) for x in (logf, f, q, kk))
    sub = lax.broadcasted_iota(jnp.int32, (1, 8, HGRN_DK), 1)
    bcast = lambda x, r: jnp.broadcast_to(x[:, r:r + 1, :], tiles)
    up = (sub & 1) != 0
    zs = [jnp.where(up, q3 * f3, k3)]
    c = c + jnp.where(up, pltpu.roll(c, 1, 1), 0.0)
    for level, tot in ((1, lambda c: jnp.where(sub < 4, bcast(c, 1), bcast(c, 5))),
                       (2, lambda c: bcast(c, 3))):
        up = (sub & (1 << level)) != 0
        t = tot(c)
        zs.append(jnp.where(up, q3, k3) * jnp.exp2(jnp.where(up, c, t - c)))
        c = c + jnp.where(up, t, 0.0)
    return [z.reshape(CHUNK, HGRN_DK) for z in zs], c.reshape(CHUNK, HGRN_DK)


def _level_large(cum, q, kk, level):
    half = 1 << level
    args, bases, cums = [], [], []
    for r0 in range(0, CHUNK, 2 * half):
        lo, up = slice(r0, r0 + half), slice(r0 + half, r0 + 2 * half)
        tot = cum[r0 + half - 1:r0 + half, :]
        args += [tot - cum[lo], cum[up]]
        bases += [kk[lo], q[up]]
        cums += [cum[lo], cum[up] + tot]
    x = jnp.exp2(jnp.concatenate(args, axis=0))
    return jnp.concatenate(bases, axis=0) * x, jnp.concatenate(cums, axis=0)


def _split3(x):
    hi = x.astype(BF16)
    r = x - hi.astype(F32)
    mid = r.astype(BF16)
    lo = (r - mid.astype(F32)).astype(BF16)
    return hi, mid, lo


def _state_update(s_old, kd, v, blast, pad_rows):
    c = kd.shape[0]
    hi, mid, lo = _split3(jnp.exp2(blast))
    row = lax.broadcasted_iota(jnp.int32, (pad_rows, HGRN_DK), 0)
    dec = jnp.where(row == 0, hi.astype(F32),
                    jnp.where(row == 1, mid.astype(F32), jnp.where(row == 2, lo.astype(F32), 0.0)))
    piece = BF16 if c % 16 == 0 and pad_rows % 16 == 0 else F32
    lhs = jnp.concatenate([kd.astype(piece), dec.astype(piece)], axis=0).astype(BF16)
    rhs = jnp.concatenate(
        [jnp.concatenate([v.astype(piece), jnp.zeros((c, HGRN_DV), piece)], axis=1),
         jnp.concatenate([jnp.zeros((pad_rows, HGRN_DV), piece), jnp.ones((pad_rows, HGRN_DV), piece)], axis=1)],
        axis=0).astype(BF16)
    both = _dot_tn(lhs, rhs)
    return both[:, HGRN_DV:] * s_old + both[:, :HGRN_DV]


def _norm_gate(o, g, nw):
    o = o * lax.rsqrt(jnp.mean(o * o, axis=-1, keepdims=True) + RMS_EPS) * nw
    return o * _silu(g)


def _hgrn_seq_kernel(q_ref, fx_ref, i_ref, g_ref, lb_ref, nw_ref, s0_ref, m_ref,
                     og_ref, sout_ref, s_scr, o_scr, *, n_chunks):
    heads = s_scr.shape[0]

    @pl.when(pl.program_id(2) == 0)
    def _():
        s_scr[...] = s0_ref[0]

    def chunk(c, carry):
        rows = pl.ds(pl.multiple_of(c * CHUNK, CHUNK), CHUNK)
        hcols = [slice(hd * HGRN_DK, (hd + 1) * HGRN_DK) for hd in range(heads)]
        q = [q_ref[rows, cs] for cs in hcols]
        v = [i_ref[rows, cs] for cs in hcols]
        gates = [_gates(fx_ref[rows, cs], lb_ref[:, cs]) for cs in hcols]
        kk = [gt[2] for gt in gates]
        cum = [_prefix_scan(gt[1]) for gt in gates]
        mid = CHUNK // 2 - 1
        spread = None
        for b in cum:
            s_hd = jnp.maximum(b[0:1, :] - b[mid:mid + 1, :], b[mid:mid + 1, :] - b[CHUNK - 1:CHUNK, :])
            spread = s_hd if spread is None else jnp.maximum(spread, s_hd)
        factorable = jnp.max(spread) <= MAX_FACTORED_LOG2_DECAY

        vb = [x.astype(BF16) for x in v]
        for hd in range(heads):
            d = cum[hd] - cum[hd][mid:mid + 1, :]
            qe = (q[hd] * jnp.exp2(d)).astype(BF16)
            ke = (kk[hd] * jnp.exp2(-d)).astype(BF16)
            amat = jnp.where(m_ref[0] > 0.5, _dot_nt(qe, ke), 0.0).astype(BF16)
            blast = cum[hd][CHUNK - 1:CHUNK, :]
            qd = (q[hd] * jnp.exp2(cum[hd])).astype(BF16)
            kd = kk[hd] * jnp.exp2(blast - cum[hd])
            s_old = s_scr[hd]
            base = _dot(qd, s_old.astype(BF16)) + jnp.sum(q[hd] * kk[hd], axis=-1, keepdims=True) * v[hd]
            o_scr[hd] = base
            s_scr[hd] = _state_update(s_old, kd, vb[hd], blast, 16)
            og_ref[rows, hcols[hd]] = _norm_gate(
                base + _dot(amat, vb[hd]), g_ref[rows, hcols[hd]], nw_ref[:, hcols[hd]]).astype(og_ref.dtype)

        @pl.when(jnp.logical_not(factorable))
        def _():
            small = [_small_levels(gt[1], gt[0], q[hd], kk[hd]) for hd, gt in enumerate(gates)]
            zs = [sm[0] for sm in small]
            part = [sm[1] for sm in small]
            for level in range(3, LEVELS):
                for hd in range(heads):
                    z, part[hd] = _level_large(part[hd], q[hd], kk[hd], level)
                    zs[hd].append(z)
            nblk = CHUNK // MASK_ROWS
            a = [[None] * nblk for _ in range(heads)]
            for level in range(LEVELS):
                step = 1 << (level - 4) if level >= 4 else 0
                blocks = [i for i in range(nblk) if level < 4 or (i & step)]
                for hd in range(heads):
                    z = zs[hd][level].astype(BF16)
                    lhs = z if level < 4 else jnp.concatenate(
                        [z[i * MASK_ROWS:(i + 1) * MASK_ROWS] for i in blocks], axis=0)
                    term = _dot_nt(lhs, z)
                    for n, i in enumerate(blocks):
                        t = (term[n * MASK_ROWS:(n + 1) * MASK_ROWS]
                             * m_ref[level + 1, i * MASK_ROWS:(i + 1) * MASK_ROWS, :])
                        a[hd][i] = t if a[hd][i] is None else a[hd][i] + t
            for hd in range(heads):
                amat = jnp.concatenate(a[hd], axis=0).astype(BF16)
                og_ref[rows, hcols[hd]] = _norm_gate(
                    o_scr[hd] + _dot(amat, vb[hd]),
                    g_ref[rows, hcols[hd]], nw_ref[:, hcols[hd]]).astype(og_ref.dtype)
        return carry

    lax.fori_loop(0, n_chunks, chunk, 0)

    @pl.when(pl.program_id(2) == pl.num_programs(2) - 1)
    def _():
        sout_ref[0] = s_scr[...]


def _hgrn_seq(h, lb, nw, s0, *, n_seq, seq_len, row_block0):
    hb = HGRN_SEQ_HEADS
    width = hb * HGRN_DK
    groups = HGRN_HEADS // hb
    rb = min(HGRN_SEQ_ROWS, seq_len)
    nrb = seq_len // rb
    assert seq_len % rb == 0 and rb % CHUNK == 0 and (row_block0 * SEQ_BLOCK) % rb == 0
    rb0 = row_block0 * SEQ_BLOCK // rb
    s0_batched = s0.shape[0] != 1
    col = lambda part: (lambda b, hg, r: (rb0 + b * nrb + r, part * groups + hg))
    masks = jnp.asarray(_level_masks())
    return pl.pallas_call(
        functools.partial(_hgrn_seq_kernel, n_chunks=rb // CHUNK),
        grid=(n_seq, groups, nrb),
        in_specs=[pl.BlockSpec((rb, width), col(0)),
                  pl.BlockSpec((rb, width), col(1)),
                  pl.BlockSpec((rb, width), col(2)),
                  pl.BlockSpec((rb, width), col(3)),
                  pl.BlockSpec((1, width), lambda b, hg, r: (0, hg)),
                  pl.BlockSpec((1, width), lambda b, hg, r: (0, hg)),
                  pl.BlockSpec((1, hb, HGRN_DK, HGRN_DV),
                               (lambda b, hg, r: (b, hg, 0, 0)) if s0_batched else (lambda b, hg, r: (0, hg, 0, 0))),
                  pl.BlockSpec((LEVELS + 1, CHUNK, CHUNK), lambda b, hg, r: (0, 0, 0))],
        out_specs=[pl.BlockSpec((rb, width), lambda b, hg, r: (b * nrb + r, hg)),
                   pl.BlockSpec((1, hb, HGRN_DK, HGRN_DV), lambda b, hg, r: (b, hg, 0, 0))],
        out_shape=[jax.ShapeDtypeStruct((n_seq * seq_len, HGRN_HEADS * HGRN_DV), BF16),
                   jax.ShapeDtypeStruct((n_seq, HGRN_HEADS, HGRN_DK, HGRN_DV), F32)],
        scratch_shapes=[pltpu.VMEM((hb, HGRN_DK, HGRN_DV), F32),
                        pltpu.VMEM((hb, CHUNK, HGRN_DV), F32)],
        compiler_params=pltpu.CompilerParams(
            dimension_semantics=("parallel", "parallel", "arbitrary"), vmem_limit_bytes=VMEM_LIMIT),
        name="hgrn_seq",
    )(h, h, h, h, lb, nw, s0, masks)


def _hgrn_step_kernel(q_ref, fx_ref, i_ref, g_ref, lb_ref, nw_ref, s_hbm, og_ref, sout_ref, sbuf, sem,
                      *, n_seq, t, heads, groups, n_steps):
    step = pl.program_id(0) * groups + pl.program_id(1)

    def state_copy(s, slot):
        src = s_hbm.at[pl.ds((s // groups) * n_seq, n_seq), pl.ds((s % groups) * heads, heads)]
        return pltpu.make_async_copy(src, sbuf.at[slot], sem.at[slot])

    @pl.when(step == 0)
    def _():
        for k in range(min(STATE_RING - 1, n_steps)):
            state_copy(k, k).start()

    ahead = step + (STATE_RING - 1)

    @pl.when(ahead < n_steps)
    def _():
        state_copy(ahead, ahead % STATE_RING).start()

    slot = step % STATE_RING
    state_copy(step, slot).wait()
    s_ref = sbuf.at[slot]

    tile = (STEP_GROUP, t, HGRN_DK)
    sub = lax.broadcasted_iota(jnp.int32, (1, t, HGRN_DK), 1)

    for hd in range(heads):
        cols = slice(hd * HGRN_DK, (hd + 1) * HGRN_DK)
        lb = lb_ref[:, cols]
        nw = nw_ref[:, cols]

        def group(i, carry, hd=hd, cols=cols, lb=lb, nw=nw):
            rows = pl.ds(pl.multiple_of(i * (STEP_GROUP * t), STEP_GROUP * t), STEP_GROUP * t)
            q = q_ref[rows, cols].reshape(tile)
            v = i_ref[rows, cols].reshape(tile)
            _, logf, kk = _gates(fx_ref[rows, cols].reshape(tile), lb)
            cum = logf
            shift = 1
            while shift < t:
                cum = cum + jnp.where(sub >= shift, pltpu.roll(cum, shift, 1), 0.0)
                shift *= 2
            intra = (jnp.sum(q * kk, axis=-1, keepdims=True)) * v
            for d in range(1, t):
                valid = sub >= d
                x = jnp.exp2(jnp.where(valid, cum - pltpu.roll(cum, d, 1), 0.0))
                w = jnp.sum(jnp.where(valid, q * x * pltpu.roll(kk, d, 1), 0.0), axis=-1, keepdims=True)
                intra = intra + w * pltpu.roll(v, d, 1)
            blast = cum[:, t - 1:t, :]
            qd = q * jnp.exp2(cum)
            kd = kk * jnp.exp2(blast - cum)
            seqs = [i * STEP_GROUP + n for n in range(STEP_GROUP)]
            inter = [_dot(qd[n].astype(BF16), s_ref[b, hd].astype(BF16)) for n, b in enumerate(seqs)]
            for n, b in enumerate(seqs):
                sout_ref[b, hd] = _state_update(s_ref[b, hd], kd[n], v[n], blast[n], 8)
            o = jnp.stack(inter, axis=0) + intra
            og = _norm_gate(o, g_ref[rows, cols].reshape(tile), nw)
            og_ref[rows, cols] = og.reshape(STEP_GROUP * t, HGRN_DK).astype(og_ref.dtype)
            return carry

        lax.fori_loop(0, n_seq // STEP_GROUP, group, 0)


def _hgrn_step(h, lb, nw, s0, *, n_seq, t, seq_per_block):
    seq_per_block = min(seq_per_block, n_seq)
    assert n_seq % seq_per_block == 0 and seq_per_block % STEP_GROUP == 0 and t == 8
    rows = seq_per_block * t
    hb = STEP_HEADS
    width = hb * HGRN_DK
    groups = HGRN_HEADS // hb
    col = lambda part: (lambda i, hg: (i, part * groups + hg))
    state_spec = pl.BlockSpec((seq_per_block, hb, HGRN_DK, HGRN_DV), lambda i, hg: (i, hg, 0, 0))
    n_steps = (n_seq // seq_per_block) * groups
    return pl.pallas_call(
        functools.partial(_hgrn_step_kernel, n_seq=seq_per_block, t=t, heads=hb, groups=groups, n_steps=n_steps),
        grid=(n_seq // seq_per_block, groups),
        in_specs=[pl.BlockSpec((rows, width), col(0)),
                  pl.BlockSpec((rows, width), col(1)),
                  pl.BlockSpec((rows, width), col(2)),
                  pl.BlockSpec((rows, width), col(3)),
                  pl.BlockSpec((1, width), lambda i, hg: (0, hg)),
                  pl.BlockSpec((1, width), lambda i, hg: (0, hg)),
                  pl.BlockSpec(memory_space=pl.ANY)],
        out_specs=[pl.BlockSpec((rows, width), lambda i, hg: (i, hg)), state_spec],
        out_shape=[jax.ShapeDtypeStruct((n_seq * t, HGRN_HEADS * HGRN_DV), BF16),
                   jax.ShapeDtypeStruct((n_seq, HGRN_HEADS, HGRN_DK, HGRN_DV), F32)],
        scratch_shapes=[pltpu.VMEM((STATE_RING, seq_per_block, hb, HGRN_DK, HGRN_DV), F32),
                        pltpu.SemaphoreType.DMA((STATE_RING,))],
        compiler_params=pltpu.CompilerParams(
            dimension_semantics=("arbitrary", "arbitrary"), vmem_limit_bytes=VMEM_LIMIT),
        name="hgrn_step",
    )(h, h, h, h, lb, nw, s0)


def _attn_seq_kernel(q_ref, g_ref, kc_ref, vc_ref, kp_ref, vp_ref, meta_ref, sink_ref, og_ref):
    first = pl.program_id(1) == 0
    kprev = jnp.where(first, meta_ref[:, :SWA_KV_WIDTH], kp_ref[...])
    vprev = jnp.where(first, meta_ref[:, SWA_KV_WIDTH:], vp_ref[...])
    kband = (jnp.concatenate([kprev, kc_ref[...]], axis=0) * (SWA_SCALE * LOG2E)).astype(BF16)
    vband_t = jnp.concatenate([vprev, vc_ref[...]], axis=0).T.astype(BF16)
    zeros_t = jnp.zeros((SWA_HEAD_DIM, 2 * SEQ_BLOCK), BF16)
    kj = lax.broadcasted_iota(jnp.int32, (2 * SEQ_BLOCK, SEQ_BLOCK), 0)
    qi = lax.broadcasted_iota(jnp.int32, (2 * SEQ_BLOCK, SEQ_BLOCK), 1)
    dist = SEQ_BLOCK + qi - kj
    valid = (dist >= 0) & (dist < WINDOW) & (jnp.logical_not(first) | (kj >= SEQ_BLOCK - N_META))
    madd = jnp.where(valid, 0.0, -jnp.inf)
    zeros = jnp.zeros((2 * SEQ_BLOCK, SWA_HEAD_DIM), BF16)
    pairs = SWA_GROUP // 2
    pw = 2 * SWA_HEAD_DIM

    def block_diag(x):
        return jnp.concatenate([jnp.concatenate([x, zeros], axis=1),
                                jnp.concatenate([zeros, x], axis=1)], axis=0)

    sts = []
    for kvh in range(SWA_KV_HEADS):
        ks = slice(kvh * SWA_HEAD_DIM, (kvh + 1) * SWA_HEAD_DIM)
        col0 = kvh * SWA_GROUP * SWA_HEAD_DIM
        xq = jnp.concatenate([q_ref[:, col0 + p * pw:col0 + (p + 1) * pw].astype(BF16)
                              for p in range(pairs)], axis=0)
        sts.append(_dot_nt(block_diag(kband[:, ks]), xq))
    for kvh in range(SWA_KV_HEADS):
        ks = slice(kvh * SWA_HEAD_DIM, (kvh + 1) * SWA_HEAD_DIM)
        col0 = kvh * SWA_GROUP * SWA_HEAD_DIM
        vt = vband_t[ks, :]
        v2t = jnp.concatenate([jnp.concatenate([vt, zeros_t], axis=1),
                               jnp.concatenate([zeros_t, vt], axis=1)], axis=0)
        pbs, rdens = [], []
        for pr in range(pairs):
            head = kvh * SWA_GROUP + 2 * pr
            s2 = (sts[kvh][:, pr * SEQ_BLOCK:(pr + 1) * SEQ_BLOCK].reshape(2, 2 * SEQ_BLOCK, SEQ_BLOCK)
                  + madd[None])
            sink = jnp.concatenate([jnp.full((1, 1, SEQ_BLOCK), sink_ref[head] * LOG2E, F32),
                                    jnp.full((1, 1, SEQ_BLOCK), sink_ref[head + 1] * LOG2E, F32)], axis=0)
            m = jnp.maximum(jnp.max(s2, axis=1, keepdims=True), sink)
            p = jnp.exp2(s2 - m)
            den = jnp.sum(p, axis=1, keepdims=True) + jnp.exp2(sink - m)
            pbs.append(p.astype(BF16).reshape(4 * SEQ_BLOCK, SEQ_BLOCK))
            rdens.append(1.0 / den)
        for pr in range(pairs):
            ot = _dot(v2t, pbs[pr]).reshape(2, SWA_HEAD_DIM, SEQ_BLOCK) * rdens[pr]
            o = ot.reshape(2 * SWA_HEAD_DIM, SEQ_BLOCK).T
            cs = slice(col0 + pr * pw, col0 + (pr + 1) * pw)
            og_ref[:, cs] = (o * _silu(g_ref[:, cs])).astype(og_ref.dtype)


def _attn_seq(h1, h1_meta, sinks, *, n_seq, seq_len, meta_row_block):
    nb = seq_len // SEQ_BLOCK
    width = SWA_Q_HEADS * SWA_HEAD_DIM
    kcol = 2 * width // SWA_KV_WIDTH
    return pl.pallas_call(
        _attn_seq_kernel,
        grid=(n_seq, nb),
        in_specs=[pl.BlockSpec((SEQ_BLOCK, width), lambda b, j: (b * nb + j, 0)),
                  pl.BlockSpec((SEQ_BLOCK, width), lambda b, j: (b * nb + j, 1)),
                  pl.BlockSpec((SEQ_BLOCK, SWA_KV_WIDTH), lambda b, j: (b * nb + j, kcol)),
                  pl.BlockSpec((SEQ_BLOCK, SWA_KV_WIDTH), lambda b, j: (b * nb + j, kcol + 1)),
                  pl.BlockSpec((SEQ_BLOCK, SWA_KV_WIDTH), lambda b, j: (b * nb + jnp.maximum(j - 1, 0), kcol)),
                  pl.BlockSpec((SEQ_BLOCK, SWA_KV_WIDTH), lambda b, j: (b * nb + jnp.maximum(j - 1, 0), kcol + 1)),
                  pl.BlockSpec((SEQ_BLOCK, 2 * SWA_KV_WIDTH), lambda b, j: (meta_row_block, kcol // 2)),
                  pl.BlockSpec(memory_space=pltpu.SMEM)],
        out_specs=pl.BlockSpec((SEQ_BLOCK, width), lambda b, j: (b * nb + j, 0)),
        out_shape=jax.ShapeDtypeStruct((n_seq * seq_len, width), BF16),
        compiler_params=pltpu.CompilerParams(
            dimension_semantics=("parallel", "arbitrary"), vmem_limit_bytes=VMEM_LIMIT),
        name="attn_seq",
    )(h1, h1, h1, h1, h1, h1, h1_meta, sinks)


def _attn_step_kernel(q_ref, g_ref, kn_ref, vn_ref, ck_ref, cv_ref, sink_ref, og_ref, nk_ref, nv_ref, *, n_seq, t):
    keys = WINDOW + t
    hd = SWA_HEAD_DIM
    tiles = SWA_Q_HEADS // 2
    kj = lax.broadcasted_iota(jnp.int32, (keys, SWA_Q_HEADS * t), 0)
    qt = lax.broadcasted_iota(jnp.int32, (keys, SWA_Q_HEADS * t), 1) % t
    madd = jnp.where((kj >= qt + 1) & (kj <= WINDOW + qt), 0.0, -jnp.inf)
    low = lax.broadcasted_iota(jnp.int32, (t, 2 * hd), 1) < hd
    zero_tile = jnp.zeros((t, 2 * hd), F32)
    sink = sink_ref[...] * LOG2E

    def group(i, carry):
        seqs = [i * SEQ_UNROLL + n for n in range(SEQ_UNROLL)]
        rows = [pl.ds(pl.multiple_of(b * t, t), t) for b in seqs]
        st, vall = [], []
        for b, rw in zip(seqs, rows):
            kc, vc = ck_ref[b], cv_ref[b]
            kn, vn = kn_ref[rw, :], vn_ref[rw, :]
            nk_ref[b, 0:WINDOW - t, :] = kc[t:, :]
            nk_ref[b, WINDOW - t:WINDOW, :] = kn
            nv_ref[b, 0:WINDOW - t, :] = vc[t:, :]
            nv_ref[b, WINDOW - t:WINDOW, :] = vn
            kall = jnp.concatenate([kc, kn], axis=0).astype(BF16)
            vall.append(jnp.concatenate([vc, vn], axis=0).astype(BF16))
            q = q_ref[rw, :] * (SWA_SCALE * LOG2E)
            qtile = [q[:, j * 2 * hd:(j + 1) * 2 * hd] for j in range(tiles)]
            qswap = [pltpu.roll(x, hd, 1) for x in qtile]
            groups = []
            for kvh in range(SWA_KV_HEADS):
                for gq in range(SWA_GROUP):
                    j = (kvh * SWA_GROUP + gq) // 2
                    src = qtile[j] if gq % 2 == kvh % 2 else qswap[j]
                    half = jnp.where(low, src, 0.0) if kvh % 2 == 0 else jnp.where(low, 0.0, src)
                    groups.append(jnp.concatenate(
                        [half if c == kvh // 2 else zero_tile for c in range(SWA_KV_HEADS // 2)], axis=1))
            qbd = jnp.concatenate(groups, axis=0).astype(BF16)
            st.append(_dot_nt(kall, qbd))
        pn = []
        for s in st:
            s = s + madd
            m = jnp.maximum(jnp.max(s, axis=0, keepdims=True), sink)
            p = jnp.exp2(s - m)
            den = jnp.sum(p, axis=0, keepdims=True) + jnp.exp2(sink - m)
            pn.append((p * (1.0 / den)).astype(BF16))
        of = [_dot_tn(p, vl) for p, vl in zip(pn, vall)]
        for o, rw in zip(of, rows):
            g = g_ref[rw, :]
            out = []
            for j in range(tiles):
                kvh = (2 * j) // SWA_GROUP
                ct = slice((kvh // 2) * 2 * hd, (kvh // 2 + 1) * 2 * hd)
                ra = o[(2 * j) * t:(2 * j + 1) * t, ct]
                rb = o[(2 * j + 1) * t:(2 * j + 2) * t, ct]
                if kvh % 2 == 0:
                    out.append(jnp.where(low, ra, pltpu.roll(rb, hd, 1)))
                else:
                    out.append(jnp.where(low, pltpu.roll(ra, hd, 1), rb))
            og_ref[rw, :] = (jnp.concatenate(out, axis=1) * _silu(g)).astype(og_ref.dtype)
        return carry

    lax.fori_loop(0, n_seq // SEQ_UNROLL, group, 0)


def _attn_step(h1, cache_k, cache_v, sink_cols, *, n_seq, t, seq_per_block):
    assert n_seq % seq_per_block == 0
    rows = seq_per_block * t
    width = SWA_Q_HEADS * SWA_HEAD_DIM
    kcol = 2 * width // SWA_KV_WIDTH
    cache_spec = pl.BlockSpec((seq_per_block, WINDOW, SWA_KV_WIDTH), lambda i: (i, 0, 0))
    return pl.pallas_call(
        functools.partial(_attn_step_kernel, n_seq=seq_per_block, t=t),
        grid=(n_seq // seq_per_block,),
        in_specs=[pl.BlockSpec((rows, width), lambda i: (i, 0)),
                  pl.BlockSpec((rows, width), lambda i: (i, 1)),
                  pl.BlockSpec((rows, SWA_KV_WIDTH), lambda i: (i, kcol)),
                  pl.BlockSpec((rows, SWA_KV_WIDTH), lambda i: (i, kcol + 1)),
                  cache_spec, cache_spec,
                  pl.BlockSpec((1, SWA_Q_HEADS * t), lambda i: (0, 0))],
        out_specs=[pl.BlockSpec((rows, width), lambda i: (i, 0)), cache_spec, cache_spec],
        out_shape=[jax.ShapeDtypeStruct((n_seq * t, width), BF16),
                   jax.ShapeDtypeStruct(cache_k.shape, F32),
                   jax.ShapeDtypeStruct(cache_v.shape, F32)],
        compiler_params=pltpu.CompilerParams(
            dimension_semantics=("parallel",), vmem_limit_bytes=VMEM_LIMIT),
        name="attn_step",
    )(h1, h1, h1, h1, cache_k, cache_v, sink_cols)


def kernel(x_prompt, x_sample, state_hgrn, cache_swa_k, cache_swa_v, meta_tokens,
           hgrn_w_in, hgrn_lb_logits, hgrn_norm_w, hgrn_w_out,
           swa_w_in, swa_sinks, swa_w_out, ln_g, ln_b):
    out_dtype = x_prompt.dtype
    bsz, seq, d = x_prompt.shape
    dec_b, dec_t, _ = x_sample.shape
    n_p = bsz * seq
    n_s = dec_b * dec_t
    width = SWA_Q_HEADS * SWA_HEAD_DIM

    in_tile = SHORT_COL_TILE
    q_tiles = width // in_tile
    kv_tile = 2 * SWA_KV_WIDTH // in_tile
    assert width % in_tile == 0 and (2 * SWA_KV_WIDTH) % in_tile == 0 and kv_tile == 1
    regroup = lambda j: jnp.where(j < q_tiles, j, jnp.where(j < 2 * q_tiles, j + kv_tile, q_tiles))
    lb = jnp.cumsum(jax.nn.softmax(hgrn_lb_logits.astype(F32), axis=0), axis=0)[0:1]
    nw = hgrn_norm_w[0].astype(F32).reshape(1, -1)
    sinks = swa_sinks[0].astype(F32).reshape(1, SWA_Q_HEADS)
    sink_cols = jnp.repeat(sinks, dec_t, axis=1)
    g0, b0 = ln_g[0:1].astype(F32), ln_b[0:1].astype(F32)
    g1, b1 = ln_g[1:2].astype(F32), ln_b[1:2].astype(F32)

    x_p = x_prompt.astype(F32).reshape(n_p, d)
    meta_block = jnp.concatenate(
        [jnp.zeros((SEQ_BLOCK - N_META, d), F32), meta_tokens.astype(F32)], axis=0)
    x_s = x_sample.astype(F32).reshape(n_s, d)
    meta_blk = n_s // SEQ_BLOCK

    h_sm, w_in0 = _matmul_cast(x_s, meta_block, hgrn_w_in[0].astype(F32), xs_rows=n_s, xm_block=0, tn=SHORT_IN0_COL_TILE)
    h_p, w_out0, w_out1 = _matmul(x_p, w_in0, tm=PROJ_ROW_TILE, tn=PROJ_IN0_COL_TILE,
                                  cast_along=(hgrn_w_out[0].astype(F32), swa_w_out[0].astype(F32)))
    zero_state = jnp.zeros((1, HGRN_HEADS, HGRN_DK, HGRN_DV), F32)
    og_meta, s_meta = _hgrn_seq(h_sm, lb, nw, zero_state, n_seq=1, seq_len=SEQ_BLOCK, row_block0=meta_blk)
    og_s, st_s = _hgrn_step(h_sm, lb, nw, state_hgrn[0].astype(F32), n_seq=dec_b, t=dec_t, seq_per_block=STEP_STATE_SEQS)
    og_p, st_p = _hgrn_seq(h_p, lb, nw, s_meta, n_seq=bsz, seq_len=seq, row_block0=0)
    x1_sm = _matmul_deepnorm_tail(og_s, og_meta, w_out0, x_s, meta_block, g0, b0)
    x1_p = _matmul_deepnorm(og_p, w_out0, x_p, g0, b0, tm=PROJ_OUT_ROW_TILE)

    h1_sm, w_in1 = _matmul_cast(x1_sm, x1_sm, swa_w_in[0].astype(F32), xs_rows=n_s, xm_block=meta_blk,
                                tn=in_tile, col_tile=regroup)
    h1_p = _matmul(x1_p, w_in1, tm=PROJ_ROW_TILE, tn=PROJ_IN1_COL_TILE)
    og1_p = _attn_seq(h1_p, h1_sm, sinks.reshape(-1), n_seq=bsz, seq_len=seq, meta_row_block=meta_blk)
    ck = cache_swa_k[0].astype(F32).reshape(dec_b, WINDOW, SWA_KV_WIDTH)
    cv = cache_swa_v[0].astype(F32).reshape(dec_b, WINDOW, SWA_KV_WIDTH)
    og1_s, nk_s, nv_s = _attn_step(h1_sm, ck, cv, sink_cols, n_seq=dec_b, t=dec_t, seq_per_block=STEP_CACHE_SEQS)
    y_p = _matmul_deepnorm(og1_p, w_out1, x1_p, g1, b1, tm=PROJ_OUT_ROW_TILE)
    y_s = _matmul_deepnorm(og1_s, w_out1, x1_sm, g1, b1, tm=PROJ_OUT_ROW_TILE)

    kv_p = h1_p.reshape(bsz, seq, -1)[:, seq - WINDOW:, 2 * width:]
    cache_shape = (1, bsz, WINDOW, SWA_KV_HEADS, SWA_HEAD_DIM)
    return (y_p.reshape(bsz, seq, d).astype(out_dtype),
            y_s.reshape(dec_b, dec_t, d).astype(out_dtype),
            st_p[None].astype(out_dtype),
            st_s[None].astype(out_dtype),
            kv_p[..., :SWA_KV_WIDTH].reshape(cache_shape).astype(out_dtype),
            kv_p[..., SWA_KV_WIDTH:].reshape(cache_shape).astype(out_dtype),
            nk_s.reshape((1,) + cache_swa_k.shape[1:]).astype(out_dtype),
            nv_s.reshape((1,) + cache_swa_v.shape[1:]).astype(out_dtype))
```

```python
import functools

import numpy as np
import jax
import jax.numpy as jnp
from jax import lax
from jax.experimental import pallas as pl
from jax.experimental.pallas import tpu as pltpu

F32 = jnp.float32
BF16 = jnp.bfloat16

D_MODEL = 2048
N_META = 16
DEPTH = 2
HGRN_HEADS = 16
HGRN_DK = 128
HGRN_DV = 128
SWA_Q_HEADS = 32
SWA_KV_HEADS = 4
SWA_GROUP = 8
SWA_HEAD_DIM = 64
SWA_KV_WIDTH = SWA_KV_HEADS * SWA_HEAD_DIM
SWA_SCALE = SWA_HEAD_DIM ** -0.5
WINDOW = 128
DEEPNORM_ALPHA = (2.0 * DEPTH) ** 0.25
LN_EPS = 1e-5
RMS_EPS = 1e-6
LOG2E = 1.4426950408889634

CHUNK = 128
LEVELS = 7
MAX_FACTORED_LOG2_DECAY = 80.0
MASK_ROWS = 16
SEQ_BLOCK = 128
LN_SUB_ROWS = 128
STEP_GROUP = 32
SEQ_UNROLL = 8
HGRN_SEQ_HEADS = 16
HGRN_SEQ_ROWS = 256

V7X_VMEM_BYTES = 64 * 1024 * 1024
VMEM_LIMIT = V7X_VMEM_BYTES - 4 * 1024 * 1024
PROJ_ROW_TILE = 1024
PROJ_IN0_COL_TILE = 2048
PROJ_IN1_COL_TILE = 2304
PROJ_OUT_ROW_TILE = 512
SHORT_COL_TILE = 512
SHORT_IN0_COL_TILE = 1024
STEP_STATE_SEQS = 32
STATE_RING = 4
STEP_HEADS = 2
STEP_CACHE_SEQS = 32


def _dot(a, b):
    return jnp.dot(a, b, preferred_element_type=F32)


def _dot_nt(a, b):
    return lax.dot_general(a, b, (((1,), (1,)), ((), ())), preferred_element_type=F32)


def _dot_tn(a, b):
    return lax.dot_general(a, b, (((0,), (0,)), ((), ())), preferred_element_type=F32)


def _silu(x):
    return x * jax.nn.sigmoid(x)


def _mm_kernel(x_ref, w_ref, *refs):
    n_side = (len(refs) - 1) // 2
    side_in, o_ref, side_out = refs[:n_side], refs[n_side], refs[n_side + 1:]
    o_ref[...] = _dot(x_ref[...].astype(BF16), w_ref[...])
    for src, dst in zip(side_in, side_out):
        dst[...] = src[...].astype(BF16)


def _row_tile(n, want):
    if n <= want:
        return n
    return max(t for t in range(16, want + 1, 16) if n % t == 0)


def _matmul(x, w, *, tm, tn, cast_along=()):
    n, k = x.shape
    e = w.shape[1]
    tm = _row_tile(n, tm)
    assert n % tm == 0 and e % tn == 0
    ncol = e // tn
    steps = (n // tm) * ncol
    side_specs = []
    for a in cast_along:
        assert a.shape[0] % (16 * steps) == 0
        side_specs.append(pl.BlockSpec((a.shape[0] // steps, a.shape[1]), lambda i, j: (i * ncol + j, 0)))
    out = pl.pallas_call(
        _mm_kernel,
        grid=(n // tm, ncol),
        in_specs=[pl.BlockSpec((tm, k), lambda i, j: (i, 0)),
                  pl.BlockSpec((k, tn), lambda i, j: (0, j))] + side_specs,
        out_specs=[pl.BlockSpec((tm, tn), lambda i, j: (i, j))] + side_specs,
        out_shape=[jax.ShapeDtypeStruct((n, e), F32)] + [jax.ShapeDtypeStruct(a.shape, BF16) for a in cast_along],
        compiler_params=pltpu.CompilerParams(
            dimension_semantics=("parallel", "parallel"), vmem_limit_bytes=VMEM_LIMIT),
        name="proj_in",
    )(x, w, *cast_along)
    return out if cast_along else out[0]


def _mm_cast_kernel(xs_ref, xm_ref, w_ref, o_ref, wb_ref, xb_scr):
    @pl.when(pl.program_id(0) == 0)
    def _():
        n_s = xs_ref.shape[0]
        xb_scr[0:n_s, :] = xs_ref[...].astype(BF16)
        xb_scr[n_s:, :] = xm_ref[...].astype(BF16)

    wb = w_ref[...].astype(BF16)
    wb_ref[...] = wb
    o_ref[...] = _dot(xb_scr[...], wb)


def _matmul_cast(xs, xm, w, *, xs_rows, xm_block, tn, col_tile=lambda j: j):
    k, e = w.shape
    n = xs_rows + SEQ_BLOCK
    assert e % tn == 0
    return pl.pallas_call(
        _mm_cast_kernel,
        grid=(e // tn,),
        in_specs=[pl.BlockSpec((xs_rows, k), lambda j: (0, 0), pipeline_mode=pl.Buffered(1)),
                  pl.BlockSpec((SEQ_BLOCK, k), lambda j: (xm_block, 0), pipeline_mode=pl.Buffered(1)),
                  pl.BlockSpec((k, tn), lambda j: (0, col_tile(j)))],
        out_specs=[pl.BlockSpec((n, tn), lambda j: (0, j)),
                   pl.BlockSpec((k, tn), lambda j: (0, j))],
        out_shape=[jax.ShapeDtypeStruct((n, e), F32), jax.ShapeDtypeStruct((k, e), BF16)],
        scratch_shapes=[pltpu.VMEM((n, k), BF16)],
        compiler_params=pltpu.CompilerParams(
            dimension_semantics=("arbitrary",), vmem_limit_bytes=VMEM_LIMIT),
        name="proj_in_cast",
    )(xs, xm, w)


def _mm_ln_tail_kernel(a_ref, at_ref, w_ref, x_ref, xt_ref, g_ref, b_ref, o_ref):
    tail = pl.program_id(0) == pl.num_programs(0) - 1
    a = jnp.where(tail, at_ref[...], a_ref[...])
    x = jnp.where(tail, xt_ref[...], x_ref[...])
    z = DEEPNORM_ALPHA * x + _dot(a, w_ref[...])
    mu = jnp.mean(z, axis=-1, keepdims=True)
    zc = z - mu
    var = jnp.mean(zc * zc, axis=-1, keepdims=True)
    o_ref[...] = zc * lax.rsqrt(var + LN_EPS) * g_ref[...] + b_ref[...]


def _matmul_deepnorm_tail(a, a_tail, w, x, x_tail, g, b):
    n, k = a.shape
    d = w.shape[1]
    tm = SEQ_BLOCK
    last = n // tm - 1
    body = lambda i: (jnp.minimum(i, last), 0)
    const = lambda i: (0, 0)
    return pl.pallas_call(
        _mm_ln_tail_kernel,
        grid=(n // tm + 1,),
        in_specs=[pl.BlockSpec((tm, k), body), pl.BlockSpec((tm, k), const),
                  pl.BlockSpec((k, d), const),
                  pl.BlockSpec((tm, d), body), pl.BlockSpec((tm, d), const),
                  pl.BlockSpec((1, d), const), pl.BlockSpec((1, d), const)],
        out_specs=pl.BlockSpec((tm, d), lambda i: (i, 0)),
        out_shape=jax.ShapeDtypeStruct((n + tm, d), F32),
        compiler_params=pltpu.CompilerParams(
            dimension_semantics=("parallel",), vmem_limit_bytes=VMEM_LIMIT),
        name="proj_out_deepnorm_tail",
    )(a, a_tail, w, x, x_tail, g, b)


def _mm_ln_kernel(a_ref, w_ref, x_ref, g_ref, b_ref, o_ref):
    tm = a_ref.shape[0]
    sub = LN_SUB_ROWS if tm % LN_SUB_ROWS == 0 else tm
    for r0 in range(0, tm, sub):
        rows = slice(r0, r0 + sub)
        z = DEEPNORM_ALPHA * x_ref[rows, :] + _dot(a_ref[rows, :], w_ref[...])
        mu = jnp.mean(z, axis=-1, keepdims=True)
        zc = z - mu
        var = jnp.mean(zc * zc, axis=-1, keepdims=True)
        o_ref[rows, :] = zc * lax.rsqrt(var + LN_EPS) * g_ref[...] + b_ref[...]


def _matmul_deepnorm(a, w, x, g, b, *, tm):
    n, k = a.shape
    d = w.shape[1]
    tm = _row_tile(n, tm)
    return pl.pallas_call(
        _mm_ln_kernel,
        grid=(n // tm,),
        in_specs=[pl.BlockSpec((tm, k), lambda i: (i, 0)),
                  pl.BlockSpec((k, d), lambda i: (0, 0)),
                  pl.BlockSpec((tm, d), lambda i: (i, 0)),
                  pl.BlockSpec((1, d), lambda i: (0, 0)),
                  pl.BlockSpec((1, d), lambda i: (0, 0))],
        out_specs=pl.BlockSpec((tm, d), lambda i: (i, 0)),
        out_shape=jax.ShapeDtypeStruct((n, d), F32),
        compiler_params=pltpu.CompilerParams(
            dimension_semantics=("parallel",), vmem_limit_bytes=VMEM_LIMIT),
        name="proj_out_deepnorm",
    )(a, w, x, g, b)


def _level_masks():
    t = np.arange(CHUNK)
    out = np.zeros((LEVELS + 1, CHUNK, CHUNK), np.float32)
    out[0] = t[:, None] > t[None, :]
    for l in range(LEVELS):
        h = 1 << l
        same = (t[:, None] >> (l + 1)) == (t[None, :] >> (l + 1))
        out[l + 1] = same & ((t[:, None] & h) != 0) & ((t[None, :] & h) == 0)
    return out


def _gates(fx, lb):
    f = lb + (1.0 - lb) * jax.nn.sigmoid(fx)
    return f, jnp.log2(f), 1.0 - f


def _prefix_scan(logf):
    tiles = (CHUNK // 8, 8, HGRN_DK)
    c = logf.reshape(tiles)
    sub = lax.broadcasted_iota(jnp.int32, (1, 8, HGRN_DK), 1)
    bcast = lambda x, r: jnp.broadcast_to(x[:, r:r + 1, :], tiles)
    c = c + jnp.where((sub & 1) != 0, pltpu.roll(c, 1, 1), 0.0)
    c = c + jnp.where((sub & 2) != 0, jnp.where(sub < 4, bcast(c, 1), bcast(c, 5)), 0.0)
    c = c + jnp.where((sub & 4) != 0, bcast(c, 3), 0.0)
    c = c.reshape(CHUNK, HGRN_DK)
    for level in range(3, LEVELS):
        half = 1 << level
        pieces = []
        for r0 in range(0, CHUNK, 2 * half):
            pieces += [c[r0:r0 + half], c[r0 + half:r0 + 2 * half] + c[r0 + half - 1:r0 + half, :]]
        c = jnp.concatenate(pieces, axis=0)
    return c


def _small_levels(logf, f, q, kk):
    tiles = (CHUNK // 8, 8, HGRN_DK)
    c, f3, q3, k3 = (x.reshape(tiles) for x in (logf, f, q, kk))
    sub = lax.broadcasted_iota(jnp.int32, (1, 8, HGRN_DK), 1)
    bcast = lambda x, r: jnp.broadcast_to(x[:, r:r + 1, :], tiles)
    up = (sub & 1) != 0
    zs = [jnp.where(up, q3 * f3, k3)]
    c = c + jnp.where(up, pltpu.roll(c, 1, 1), 0.0)
    for level, tot in ((1, lambda c: jnp.where(sub < 4, bcast(c, 1), bcast(c, 5))),
                       (2, lambda c: bcast(c, 3))):
        up = (sub & (1 << level)) != 0
        t = tot(c)
        zs.append(jnp.where(up, q3, k3) * jnp.exp2(jnp.where(up, c, t - c)))
        c = c + jnp.where(up, t, 0.0)
    return [z.reshape(CHUNK, HGRN_DK) for z in zs], c.reshape(CHUNK, HGRN_DK)


def _level_large(cum, q, kk, level):
    half = 1 << level
    args, bases, cums = [], [], []
    for r0 in range(0, CHUNK, 2 * half):
        lo, up = slice(r0, r0 + half), slice(r0 + half, r0 + 2 * half)
        tot = cum[r0 + half - 1:r0 + half, :]
        args += [tot - cum[lo], cum[up]]
        bases += [kk[lo], q[up]]
        cums += [cum[lo], cum[up] + tot]
    x = jnp.exp2(jnp.concatenate(args, axis=0))
    return jnp.concatenate(bases, axis=0) * x, jnp.concatenate(cums, axis=0)


def _split3(x):
    hi = x.astype(BF16)
    r = x - hi.astype(F32)
    mid = r.astype(BF16)
    lo = (r - mid.astype(F32)).astype(BF16)
    return hi, mid, lo


def _state_update(s_old, kd, v, blast, pad_rows):
    c = kd.shape[0]
    hi, mid, lo = _split3(jnp.exp2(blast))
    row = lax.broadcasted_iota(jnp.int32, (pad_rows, HGRN_DK), 0)
    dec = jnp.where(row == 0, hi.astype(F32),
                    jnp.where(row == 1, mid.astype(F32), jnp.where(row == 2, lo.astype(F32), 0.0)))
    piece = BF16 if c % 16 == 0 and pad_rows % 16 == 0 else F32
    lhs = jnp.concatenate([kd.astype(piece), dec.astype(piece)], axis=0).astype(BF16)
    rhs = jnp.concatenate(
        [jnp.concatenate([v.astype(piece), jnp.zeros((c, HGRN_DV), piece)], axis=1),
         jnp.concatenate([jnp.zeros((pad_rows, HGRN_DV), piece), jnp.ones((pad_rows, HGRN_DV), piece)], axis=1)],
        axis=0).astype(BF16)
    both = _dot_tn(lhs, rhs)
    return both[:, HGRN_DV:] * s_old + both[:, :HGRN_DV]


def _norm_gate(o, g, nw):
    o = o * lax.rsqrt(jnp.mean(o * o, axis=-1, keepdims=True) + RMS_EPS) * nw
    return o * _silu(g)


def _hgrn_seq_kernel(q_ref, fx_ref, i_ref, g_ref, lb_ref, nw_ref, s0_ref, m_ref,
                     og_ref, sout_ref, s_scr, o_scr, *, n_chunks):
    heads = s_scr.shape[0]

    @pl.when(pl.program_id(2) == 0)
    def _():
        s_scr[...] = s0_ref[0]

    def chunk(c, carry):
        rows = pl.ds(pl.multiple_of(c * CHUNK, CHUNK), CHUNK)
        hcols = [slice(hd * HGRN_DK, (hd + 1) * HGRN_DK) for hd in range(heads)]
        q = [q_ref[rows, cs] for cs in hcols]
        v = [i_ref[rows, cs] for cs in hcols]
        gates = [_gates(fx_ref[rows, cs], lb_ref[:, cs]) for cs in hcols]
        kk = [gt[2] for gt in gates]
        cum = [_prefix_scan(gt[1]) for gt in gates]
        mid = CHUNK // 2 - 1
        spread = None
        for b in cum:
            s_hd = jnp.maximum(b[0:1, :] - b[mid:mid + 1, :], b[mid:mid + 1, :] - b[CHUNK - 1:CHUNK, :])
            spread = s_hd if spread is None else jnp.maximum(spread, s_hd)
        factorable = jnp.max(spread) <= MAX_FACTORED_LOG2_DECAY

        vb = [x.astype(BF16) for x in v]
        for hd in range(heads):
            d = cum[hd] - cum[hd][mid:mid + 1, :]
            qe = (q[hd] * jnp.exp2(d)).astype(BF16)
            ke = (kk[hd] * jnp.exp2(-d)).astype(BF16)
            amat = jnp.where(m_ref[0] > 0.5, _dot_nt(qe, ke), 0.0).astype(BF16)
            blast = cum[hd][CHUNK - 1:CHUNK, :]
            qd = (q[hd] * jnp.exp2(cum[hd])).astype(BF16)
            kd = kk[hd] * jnp.exp2(blast - cum[hd])
            s_old = s_scr[hd]
            base = _dot(qd, s_old.astype(BF16)) + jnp.sum(q[hd] * kk[hd], axis=-1, keepdims=True) * v[hd]
            o_scr[hd] = base
            s_scr[hd] = _state_update(s_old, kd, vb[hd], blast, 16)
            og_ref[rows, hcols[hd]] = _norm_gate(
                base + _dot(amat, vb[hd]), g_ref[rows, hcols[hd]], nw_ref[:, hcols[hd]]).astype(og_ref.dtype)

        @pl.when(jnp.logical_not(factorable))
        def _():
            small = [_small_levels(gt[1], gt[0], q[hd], kk[hd]) for hd, gt in enumerate(gates)]
            zs = [sm[0] for sm in small]
            part = [sm[1] for sm in small]
            for level in range(3, LEVELS):
                for hd in range(heads):
                    z, part[hd] = _level_large(part[hd], q[hd], kk[hd], level)
                    zs[hd].append(z)
            nblk = CHUNK // MASK_ROWS
            a = [[None] * nblk for _ in range(heads)]
            for level in range(LEVELS):
                step = 1 << (level - 4) if level >= 4 else 0
                blocks = [i for i in range(nblk) if level < 4 or (i & step)]
                for hd in range(heads):
                    z = zs[hd][level].astype(BF16)
                    lhs = z if level < 4 else jnp.concatenate(
                        [z[i * MASK_ROWS:(i + 1) * MASK_ROWS] for i in blocks], axis=0)
                    term = _dot_nt(lhs, z)
                    for n, i in enumerate(blocks):
                        t = (term[n * MASK_ROWS:(n + 1) * MASK_ROWS]
                             * m_ref[level + 1, i * MASK_ROWS:(i + 1) * MASK_ROWS, :])
                        a[hd][i] = t if a[hd][i] is None else a[hd][i] + t
            for hd in range(heads):
                amat = jnp.concatenate(a[hd], axis=0).astype(BF16)
                og_ref[rows, hcols[hd]] = _norm_gate(
                    o_scr[hd] + _dot(amat, vb[hd]),
                    g_ref[rows, hcols[hd]], nw_ref[:, hcols[hd]]).astype(og_ref.dtype)
        return carry

    lax.fori_loop(0, n_chunks, chunk, 0)

    @pl.when(pl.program_id(2) == pl.num_programs(2) - 1)
    def _():
        sout_ref[0] = s_scr[...]


def _hgrn_seq(h, lb, nw, s0, *, n_seq, seq_len, row_block0):
    hb = HGRN_SEQ_HEADS
    width = hb * HGRN_DK
    groups = HGRN_HEADS // hb
    rb = min(HGRN_SEQ_ROWS, seq_len)
    nrb = seq_len // rb
    assert seq_len % rb == 0 and rb % CHUNK == 0 and (row_block0 * SEQ_BLOCK) % rb == 0
    rb0 = row_block0 * SEQ_BLOCK // rb
    s0_batched = s0.shape[0] != 1
    col = lambda part: (lambda b, hg, r: (rb0 + b * nrb + r, part * groups + hg))
    masks = jnp.asarray(_level_masks())
    return pl.pallas_call(
        functools.partial(_hgrn_seq_kernel, n_chunks=rb // CHUNK),
        grid=(n_seq, groups, nrb),
        in_specs=[pl.BlockSpec((rb, width), col(0)),
                  pl.BlockSpec((rb, width), col(1)),
                  pl.BlockSpec((rb, width), col(2)),
                  pl.BlockSpec((rb, width), col(3)),
                  pl.BlockSpec((1, width), lambda b, hg, r: (0, hg)),
                  pl.BlockSpec((1, width), lambda b, hg, r: (0, hg)),
                  pl.BlockSpec((1, hb, HGRN_DK, HGRN_DV),
                               (lambda b, hg, r: (b, hg, 0, 0)) if s0_batched else (lambda b, hg, r: (0, hg, 0, 0))),
                  pl.BlockSpec((LEVELS + 1, CHUNK, CHUNK), lambda b, hg, r: (0, 0, 0))],
        out_specs=[pl.BlockSpec((rb, width), lambda b, hg, r: (b * nrb + r, hg)),
                   pl.BlockSpec((1, hb, HGRN_DK, HGRN_DV), lambda b, hg, r: (b, hg, 0, 0))],
        out_shape=[jax.ShapeDtypeStruct((n_seq * seq_len, HGRN_HEADS * HGRN_DV), BF16),
                   jax.ShapeDtypeStruct((n_seq, HGRN_HEADS, HGRN_DK, HGRN_DV), F32)],
        scratch_shapes=[pltpu.VMEM((hb, HGRN_DK, HGRN_DV), F32),
                        pltpu.VMEM((hb, CHUNK, HGRN_DV), F32)],
        compiler_params=pltpu.CompilerParams(
            dimension_semantics=("parallel", "parallel", "arbitrary"), vmem_limit_bytes=VMEM_LIMIT),
        name="hgrn_seq",
    )(h, h, h, h, lb, nw, s0, masks)


def _hgrn_step_kernel(q_ref, fx_ref, i_ref, g_ref, lb_ref, nw_ref, s_hbm, og_ref, sout_ref, sbuf, sem,
                      *, n_seq, t, heads, groups, n_steps):
    step = pl.program_id(0) * groups + pl.program_id(1)

    def state_copy(s, slot):
        src = s_hbm.at[pl.ds((s // groups) * n_seq, n_seq), pl.ds((s % groups) * heads, heads)]
        return pltpu.make_async_copy(src, sbuf.at[slot], sem.at[slot])

    @pl.when(step == 0)
    def _():
        for k in range(min(STATE_RING - 1, n_steps)):
            state_copy(k, k).start()

    ahead = step + (STATE_RING - 1)

    @pl.when(ahead < n_steps)
    def _():
        state_copy(ahead, ahead % STATE_RING).start()

    slot = step % STATE_RING
    state_copy(step, slot).wait()
    s_ref = sbuf.at[slot]

    tile = (STEP_GROUP, t, HGRN_DK)
    sub = lax.broadcasted_iota(jnp.int32, (1, t, HGRN_DK), 1)

    for hd in range(heads):
        cols = slice(hd * HGRN_DK, (hd + 1) * HGRN_DK)
        lb = lb_ref[:, cols]
        nw = nw_ref[:, cols]

        def group(i, carry, hd=hd, cols=cols, lb=lb, nw=nw):
            rows = pl.ds(pl.multiple_of(i * (STEP_GROUP * t), STEP_GROUP * t), STEP_GROUP * t)
            q = q_ref[rows, cols].reshape(tile)
            v = i_ref[rows, cols].reshape(tile)
            _, logf, kk = _gates(fx_ref[rows, cols].reshape(tile), lb)
            cum = logf
            shift = 1
            while shift < t:
                cum = cum + jnp.where(sub >= shift, pltpu.roll(cum, shift, 1), 0.0)
                shift *= 2
            intra = (jnp.sum(q * kk, axis=-1, keepdims=True)) * v
            for d in range(1, t):
                valid = sub >= d
                x = jnp.exp2(jnp.where(valid, cum - pltpu.roll(cum, d, 1), 0.0))
                w = jnp.sum(jnp.where(valid, q * x * pltpu.roll(kk, d, 1), 0.0), axis=-1, keepdims=True)
                intra = intra + w * pltpu.roll(v, d, 1)
            blast = cum[:, t - 1:t, :]
            qd = q * jnp.exp2(cum)
            kd = kk * jnp.exp2(blast - cum)
            seqs = [i * STEP_GROUP + n for n in range(STEP_GROUP)]
            inter = [_dot(qd[n].astype(BF16), s_ref[b, hd].astype(BF16)) for n, b in enumerate(seqs)]
            for n, b in enumerate(seqs):
                sout_ref[b, hd] = _state_update(s_ref[b, hd], kd[n], v[n], blast[n], 8)
            o = jnp.stack(inter, axis=0) + intra
            og = _norm_gate(o, g_ref[rows, cols].reshape(tile), nw)
            og_ref[rows, cols] = og.reshape(STEP_GROUP * t, HGRN_DK).astype(og_ref.dtype)
            return carry

        lax.fori_loop(0, n_seq // STEP_GROUP, group, 0)


def _hgrn_step(h, lb, nw, s0, *, n_seq, t, seq_per_block):
    seq_per_block = min(seq_per_block, n_seq)
    assert n_seq % seq_per_block == 0 and seq_per_block % STEP_GROUP == 0 and t == 8
    rows = seq_per_block * t
    hb = STEP_HEADS
    width = hb * HGRN_DK
    groups = HGRN_HEADS // hb
    col = lambda part: (lambda i, hg: (i, part * groups + hg))
    state_spec = pl.BlockSpec((seq_per_block, hb, HGRN_DK, HGRN_DV), lambda i, hg: (i, hg, 0, 0))
    n_steps = (n_seq // seq_per_block) * groups
    return pl.pallas_call(
        functools.partial(_hgrn_step_kernel, n_seq=seq_per_block, t=t, heads=hb, groups=groups, n_steps=n_steps),
        grid=(n_seq // seq_per_block, groups),
        in_specs=[pl.BlockSpec((rows, width), col(0)),
                  pl.BlockSpec((rows, width), col(1)),
                  pl.BlockSpec((rows, width), col(2)),
                  pl.BlockSpec((rows, width), col(3)),
                  pl.BlockSpec((1, width), lambda i, hg: (0, hg)),
                  pl.BlockSpec((1, width), lambda i, hg: (0, hg)),
                  pl.BlockSpec(memory_space=pl.ANY)],
        out_specs=[pl.BlockSpec((rows, width), lambda i, hg: (i, hg)), state_spec],
        out_shape=[jax.ShapeDtypeStruct((n_seq * t, HGRN_HEADS * HGRN_DV), BF16),
                   jax.ShapeDtypeStruct((n_seq, HGRN_HEADS, HGRN_DK, HGRN_DV), F32)],
        scratch_shapes=[pltpu.VMEM((STATE_RING, seq_per_block, hb, HGRN_DK, HGRN_DV), F32),
                        pltpu.SemaphoreType.DMA((STATE_RING,))],
        compiler_params=pltpu.CompilerParams(
            dimension_semantics=("arbitrary", "arbitrary"), vmem_limit_bytes=VMEM_LIMIT),
        name="hgrn_step",
    )(h, h, h, h, lb, nw, s0)


def _attn_seq_kernel(q_ref, g_ref, kc_ref, vc_ref, kp_ref, vp_ref, meta_ref, sink_ref, og_ref):
    first = pl.program_id(1) == 0
    kprev = jnp.where(first, meta_ref[:, :SWA_KV_WIDTH], kp_ref[...])
    vprev = jnp.where(first, meta_ref[:, SWA_KV_WIDTH:], vp_ref[...])
    kband = (jnp.concatenate([kprev, kc_ref[...]], axis=0) * (SWA_SCALE * LOG2E)).astype(BF16)
    vband_t = jnp.concatenate([vprev, vc_ref[...]], axis=0).T.astype(BF16)
    zeros_t = jnp.zeros((SWA_HEAD_DIM, 2 * SEQ_BLOCK), BF16)
    kj = lax.broadcasted_iota(jnp.int32, (2 * SEQ_BLOCK, SEQ_BLOCK), 0)
    qi = lax.broadcasted_iota(jnp.int32, (2 * SEQ_BLOCK, SEQ_BLOCK), 1)
    dist = SEQ_BLOCK + qi - kj
    valid = (dist >= 0) & (dist < WINDOW) & (jnp.logical_not(first) | (kj >= SEQ_BLOCK - N_META))
    madd = jnp.where(valid, 0.0, -jnp.inf)
    zeros = jnp.zeros((2 * SEQ_BLOCK, SWA_HEAD_DIM), BF16)
    pairs = SWA_GROUP // 2
    pw = 2 * SWA_HEAD_DIM

    def block_diag(x):
        return jnp.concatenate([jnp.concatenate([x, zeros], axis=1),
                                jnp.concatenate([zeros, x], axis=1)], axis=0)

    sts = []
    for kvh in range(SWA_KV_HEADS):
        ks = slice(kvh * SWA_HEAD_DIM, (kvh + 1) * SWA_HEAD_DIM)
        col0 = kvh * SWA_GROUP * SWA_HEAD_DIM
        xq = jnp.concatenate([q_ref[:, col0 + p * pw:col0 + (p + 1) * pw].astype(BF16)
                              for p in range(pairs)], axis=0)
        sts.append(_dot_nt(block_diag(kband[:, ks]), xq))
    for kvh in range(SWA_KV_HEADS):
        ks = slice(kvh * SWA_HEAD_DIM, (kvh + 1) * SWA_HEAD_DIM)
        col0 = kvh * SWA_GROUP * SWA_HEAD_DIM
        vt = vband_t[ks, :]
        v2t = jnp.concatenate([jnp.concatenate([vt, zeros_t], axis=1),
                               jnp.concatenate([zeros_t, vt], axis=1)], axis=0)
        pbs, rdens = [], []
        for pr in range(pairs):
            head = kvh * SWA_GROUP + 2 * pr
            s2 = (sts[kvh][:, pr * SEQ_BLOCK:(pr + 1) * SEQ_BLOCK].reshape(2, 2 * SEQ_BLOCK, SEQ_BLOCK)
                  + madd[None])
            sink = jnp.concatenate([jnp.full((1, 1, SEQ_BLOCK), sink_ref[head] * LOG2E, F32),
                                    jnp.full((1, 1, SEQ_BLOCK), sink_ref[head + 1] * LOG2E, F32)], axis=0)
            m = jnp.maximum(jnp.max(s2, axis=1, keepdims=True), sink)
            p = jnp.exp2(s2 - m)
            den = jnp.sum(p, axis=1, keepdims=True) + jnp.exp2(sink - m)
            pbs.append(p.astype(BF16).reshape(4 * SEQ_BLOCK, SEQ_BLOCK))
            rdens.append(1.0 / den)
        for pr in range(pairs):
            ot = _dot(v2t, pbs[pr]).reshape(2, SWA_HEAD_DIM, SEQ_BLOCK) * rdens[pr]
            o = ot.reshape(2 * SWA_HEAD_DIM, SEQ_BLOCK).T
            cs = slice(col0 + pr * pw, col0 + (pr + 1) * pw)
            og_ref[:, cs] = (o * _silu(g_ref[:, cs])).astype(og_ref.dtype)


def _attn_seq(h1, h1_meta, sinks, *, n_seq, seq_len, meta_row_block):
    nb = seq_len // SEQ_BLOCK
    width = SWA_Q_HEADS * SWA_HEAD_DIM
    kcol = 2 * width // SWA_KV_WIDTH
    return pl.pallas_call(
        _attn_seq_kernel,
        grid=(n_seq, nb),
        in_specs=[pl.BlockSpec((SEQ_BLOCK, width), lambda b, j: (b * nb + j, 0)),
                  pl.BlockSpec((SEQ_BLOCK, width), lambda b, j: (b * nb + j, 1)),
                  pl.BlockSpec((SEQ_BLOCK, SWA_KV_WIDTH), lambda b, j: (b * nb + j, kcol)),
                  pl.BlockSpec((SEQ_BLOCK, SWA_KV_WIDTH), lambda b, j: (b * nb + j, kcol + 1)),
                  pl.BlockSpec((SEQ_BLOCK, SWA_KV_WIDTH), lambda b, j: (b * nb + jnp.maximum(j - 1, 0), kcol)),
                  pl.BlockSpec((SEQ_BLOCK, SWA_KV_WIDTH), lambda b, j: (b * nb + jnp.maximum(j - 1, 0), kcol + 1)),
                  pl.BlockSpec((SEQ_BLOCK, 2 * SWA_KV_WIDTH), lambda b, j: (meta_row_block, kcol // 2)),
                  pl.BlockSpec(memory_space=pltpu.SMEM)],
        out_specs=pl.BlockSpec((SEQ_BLOCK, width), lambda b, j: (b * nb + j, 0)),
        out_shape=jax.ShapeDtypeStruct((n_seq * seq_len, width), BF16),
        compiler_params=pltpu.CompilerParams(
            dimension_semantics=("parallel", "arbitrary"), vmem_limit_bytes=VMEM_LIMIT),
        name="attn_seq",
    )(h1, h1, h1, h1, h1, h1, h1_meta, sinks)


def _attn_step_kernel(q_ref, g_ref, kn_ref, vn_ref, ck_ref, cv_ref, sink_ref, og_ref, nk_ref, nv_ref, *, n_seq, t):
    keys = WINDOW + t
    hd = SWA_HEAD_DIM
    tiles = SWA_Q_HEADS // 2
    kj = lax.broadcasted_iota(jnp.int32, (keys, SWA_Q_HEADS * t), 0)
    qt = lax.broadcasted_iota(jnp.int32, (keys, SWA_Q_HEADS * t), 1) % t
    madd = jnp.where((kj >= qt + 1) & (kj <= WINDOW + qt), 0.0, -jnp.inf)
    low = lax.broadcasted_iota(jnp.int32, (t, 2 * hd), 1) < hd
    zero_tile = jnp.zeros((t, 2 * hd), F32)
    sink = sink_ref[...] * LOG2E

    def group(i, carry):
        seqs = [i * SEQ_UNROLL + n for n in range(SEQ_UNROLL)]
        rows = [pl.ds(pl.multiple_of(b * t, t), t) for b in seqs]
        st, vall = [], []
        for b, rw in zip(seqs, rows):
            kc, vc = ck_ref[b], cv_ref[b]
            kn, vn = kn_ref[rw, :], vn_ref[rw, :]
            nk_ref[b, 0:WINDOW - t, :] = kc[t:, :]
            nk_ref[b, WINDOW - t:WINDOW, :] = kn
            nv_ref[b, 0:WINDOW - t, :] = vc[t:, :]
            nv_ref[b, WINDOW - t:WINDOW, :] = vn
            kall = jnp.concatenate([kc, kn], axis=0).astype(BF16)
            vall.append(jnp.concatenate([vc, vn], axis=0).astype(BF16))
            q = q_ref[rw, :] * (SWA_SCALE * LOG2E)
            qtile = [q[:, j * 2 * hd:(j + 1) * 2 * hd] for j in range(tiles)]
            qswap = [pltpu.roll(x, hd, 1) for x in qtile]
            groups = []
            for kvh in range(SWA_KV_HEADS):
                for gq in range(SWA_GROUP):
                    j = (kvh * SWA_GROUP + gq) // 2
                    src = qtile[j] if gq % 2 == kvh % 2 else qswap[j]
                    half = jnp.where(low, src, 0.0) if kvh % 2 == 0 else jnp.where(low, 0.0, src)
                    groups.append(jnp.concatenate(
                        [half if c == kvh // 2 else zero_tile for c in range(SWA_KV_HEADS // 2)], axis=1))
            qbd = jnp.concatenate(groups, axis=0).astype(BF16)
            st.append(_dot_nt(kall, qbd))
        pn = []
        for s in st:
            s = s + madd
            m = jnp.maximum(jnp.max(s, axis=0, keepdims=True), sink)
            p = jnp.exp2(s - m)
            den = jnp.sum(p, axis=0, keepdims=True) + jnp.exp2(sink - m)
            pn.append((p * (1.0 / den)).astype(BF16))
        of = [_dot_tn(p, vl) for p, vl in zip(pn, vall)]
        for o, rw in zip(of, rows):
            g = g_ref[rw, :]
            out = []
            for j in range(tiles):
                kvh = (2 * j) // SWA_GROUP
                ct = slice((kvh // 2) * 2 * hd, (kvh // 2 + 1) * 2 * hd)
                ra = o[(2 * j) * t:(2 * j + 1) * t, ct]
                rb = o[(2 * j + 1) * t:(2 * j + 2) * t, ct]
                if kvh % 2 == 0:
                    out.append(jnp.where(low, ra, pltpu.roll(rb, hd, 1)))
                else:
                    out.append(jnp.where(low, pltpu.roll(ra, hd, 1), rb))
            og_ref[rw, :] = (jnp.concatenate(out, axis=1) * _silu(g)).astype(og_ref.dtype)
        return carry

    lax.fori_loop(0, n_seq // SEQ_UNROLL, group, 0)


def _attn_step(h1, cache_k, cache_v, sink_cols, *, n_seq, t, seq_per_block):
    assert n_seq % seq_per_block == 0
    rows = seq_per_block * t
    width = SWA_Q_HEADS * SWA_HEAD_DIM
    kcol = 2 * width // SWA_KV_WIDTH
    cache_spec = pl.BlockSpec((seq_per_block, WINDOW, SWA_KV_WIDTH), lambda i: (i, 0, 0))
    return pl.pallas_call(
        functools.partial(_attn_step_kernel, n_seq=seq_per_block, t=t),
        grid=(n_seq // seq_per_block,),
        in_specs=[pl.BlockSpec((rows, width), lambda i: (i, 0)),
                  pl.BlockSpec((rows, width), lambda i: (i, 1)),
                  pl.BlockSpec((rows, SWA_KV_WIDTH), lambda i: (i, kcol)),
                  pl.BlockSpec((rows, SWA_KV_WIDTH), lambda i: (i, kcol + 1)),
                  cache_spec, cache_spec,
                  pl.BlockSpec((1, SWA_Q_HEADS * t), lambda i: (0, 0))],
        out_specs=[pl.BlockSpec((rows, width), lambda i: (i, 0)), cache_spec, cache_spec],
        out_shape=[jax.ShapeDtypeStruct((n_seq * t, width), BF16),
                   jax.ShapeDtypeStruct(cache_k.shape, F32),
                   jax.ShapeDtypeStruct(cache_v.shape, F32)],
        compiler_params=pltpu.CompilerParams(
            dimension_semantics=("parallel",), vmem_limit_bytes=VMEM_LIMIT),
        name="attn_step",
    )(h1, h1, h1, h1, cache_k, cache_v, sink_cols)


def kernel(x_prompt, x_sample, state_hgrn, cache_swa_k, cache_swa_v, meta_tokens,
           hgrn_w_in, hgrn_lb_logits, hgrn_norm_w, hgrn_w_out,
           swa_w_in, swa_sinks, swa_w_out, ln_g, ln_b):
    out_dtype = x_prompt.dtype
    bsz, seq, d = x_prompt.shape
    dec_b, dec_t, _ = x_sample.shape
    n_p = bsz * seq
    n_s = dec_b * dec_t
    width = SWA_Q_HEADS * SWA_HEAD_DIM

    in_tile = SHORT_COL_TILE
    q_tiles = width // in_tile
    kv_tile = 2 * SWA_KV_WIDTH // in_tile
    assert width % in_tile == 0 and (2 * SWA_KV_WIDTH) % in_tile == 0 and kv_tile == 1
    regroup = lambda j: jnp.where(j < q_tiles, j, jnp.where(j < 2 * q_tiles, j + kv_tile, q_tiles))
    lb = jnp.cumsum(jax.nn.softmax(hgrn_lb_logits.astype(F32), axis=0), axis=0)[0:1]
    nw = hgrn_norm_w[0].astype(F32).reshape(1, -1)
    sinks = swa_sinks[0].astype(F32).reshape(1, SWA_Q_HEADS)
    sink_cols = jnp.repeat(sinks, dec_t, axis=1)
    g0, b0 = ln_g[0:1].astype(F32), ln_b[0:1].astype(F32)
    g1, b1 = ln_g[1:2].astype(F32), ln_b[1:2].astype(F32)

    x_p = x_prompt.astype(F32).reshape(n_p, d)
    meta_block = jnp.concatenate(
        [jnp.zeros((SEQ_BLOCK - N_META, d), F32), meta_tokens.astype(F32)], axis=0)
    x_s = x_sample.astype(F32).reshape(n_s, d)
    meta_blk = n_s // SEQ_BLOCK

    h_sm, w_in0 = _matmul_cast(x_s, meta_block, hgrn_w_in[0].astype(F32), xs_rows=n_s, xm_block=0, tn=SHORT_IN0_COL_TILE)
    h_p, w_out0, w_out1 = _matmul(x_p, w_in0, tm=PROJ_ROW_TILE, tn=PROJ_IN0_COL_TILE,
                                  cast_along=(hgrn_w_out[0].astype(F32), swa_w_out[0].astype(F32)))
    zero_state = jnp.zeros((1, HGRN_HEADS, HGRN_DK, HGRN_DV), F32)
    og_meta, s_meta = _hgrn_seq(h_sm, lb, nw, zero_state, n_seq=1, seq_len=SEQ_BLOCK, row_block0=meta_blk)
    og_s, st_s = _hgrn_step(h_sm, lb, nw, state_hgrn[0].astype(F32), n_seq=dec_b, t=dec_t, seq_per_block=STEP_STATE_SEQS)
    og_p, st_p = _hgrn_seq(h_p, lb, nw, s_meta, n_seq=bsz, seq_len=seq, row_block0=0)
    x1_sm = _matmul_deepnorm_tail(og_s, og_meta, w_out0, x_s, meta_block, g0, b0)
    x1_p = _matmul_deepnorm(og_p, w_out0, x_p, g0, b0, tm=PROJ_OUT_ROW_TILE)

    h1_sm, w_in1 = _matmul_cast(x1_sm, x1_sm, swa_w_in[0].astype(F32), xs_rows=n_s, xm_block=meta_blk,
                                tn=in_tile, col_tile=regroup)
    h1_p = _matmul(x1_p, w_in1, tm=PROJ_ROW_TILE, tn=PROJ_IN1_COL_TILE)
    og1_p = _attn_seq(h1_p, h1_sm, sinks.reshape(-1), n_seq=bsz, seq_len=seq, meta_row_block=meta_blk)
    ck = cache_swa_k[0].astype(F32).reshape(dec_b, WINDOW, SWA_KV_WIDTH)
    cv = cache_swa_v[0].astype(F32).reshape(dec_b, WINDOW, SWA_KV_WIDTH)
    og1_s, nk_s, nv_s = _attn_step(h1_sm, ck, cv, sink_cols, n_seq=dec_b, t=dec_t, seq_per_block=STEP_CACHE_SEQS)
    y_p = _matmul_deepnorm(og1_p, w_out1, x1_p, g1, b1, tm=PROJ_OUT_ROW_TILE)
    y_s = _matmul_deepnorm(og1_s, w_out1, x1_sm, g1, b1, tm=PROJ_OUT_ROW_TILE)

    kv_p = h1_p.reshape(bsz, seq, -1)[:, seq - WINDOW:, 2 * width:]
    cache_shape = (1, bsz, WINDOW, SWA_KV_HEADS, SWA_HEAD_DIM)
    return (y_p.reshape(bsz, seq, d).astype(out_dtype),
            y_s.reshape(dec_b, dec_t, d).astype(out_dtype),
            st_p[None].astype(out_dtype),
            st_s[None].astype(out_dtype),
            kv_p[..., :SWA_KV_WIDTH].reshape(cache_shape).astype(out_dtype),
            kv_p[..., SWA_KV_WIDTH:].reshape(cache_shape).astype(out_dtype),
            nk_s.reshape((1,) + cache_swa_k.shape[1:]).astype(out_dtype),
            nv_s.reshape((1,) + cache_swa_v.shape[1:]).astype(out_dtype))
```
